```python
import jax, jax.numpy as jnp
from jax import lax
import numpy as np

D_MODEL = 2048
BATCH = 8
SEQ = 4096
DEPTH = 1

CHUNK = 64
D_FF = 5632
W_CONV = D_MODEL // 2
CONV_K = 31
W_POOL = D_MODEL // 2
POOL_WINDOWS = (2, 4, 8, 16)
N_POOL_GROUPS = len(POOL_WINDOWS)
POOL_GROUP_IN = W_POOL // N_POOL_GROUPS
POOL_GROUP_OUT = D_MODEL // N_POOL_GROUPS
N_BRANCHES = 2
N_IN_COLS = 2 * W_CONV + W_POOL + N_BRANCHES * D_MODEL
N_ADA = 3
EPS = 1e-6

kernel_name = "hybrid_conformer_pool_gated_block"


def rms_norm(x, g):
    xf = x.astype(jnp.float32)
    y = xf * lax.rsqrt(jnp.mean(xf * xf, axis=-1, keepdims=True) + EPS)
    return y.astype(x.dtype) * g


def layer_norm(x, g, b):
    xf = x.astype(jnp.float32)
    mu = jnp.mean(xf, axis=-1, keepdims=True)
    var = jnp.mean(jnp.square(xf - mu), axis=-1, keepdims=True)
    y = (xf - mu) * lax.rsqrt(var + EPS)
    return y.astype(x.dtype) * g + b


def modulate(n, shift, scale):
    return n * (1.0 + scale[:, None, :]) + shift[:, None, :]


def swiglu(n, w_in, w_out):
    hu = n @ w_in
    h, u = jnp.split(hu, 2, axis=-1)
    return (jax.nn.silu(h) * u) @ w_out


def causal_mean_pool(v, window):
    seq = v.shape[1]
    vf = v.astype(jnp.float32)
    csum = jnp.cumsum(vf, axis=1)
    lagged = jnp.pad(csum, ((0, 0), (window, 0), (0, 0)))[:, :seq]
    count = jnp.minimum(jnp.arange(1, seq + 1, dtype=jnp.float32), float(window))
    return ((csum - lagged) / count[None, :, None]).astype(v.dtype)


def conformer_conv_branch(glu_in, conv_w, conv_b, ln_a_g, ln_a_b, w_a_out, b_a_out):
    a, g = jnp.split(glu_in, 2, axis=-1)
    a = a * jax.nn.sigmoid(g)
    a = lax.conv_general_dilated(
        a, conv_w[:, None, :], window_strides=(1,), padding=[(CONV_K - 1, 0)],
        dimension_numbers=('NWC', 'WIO', 'NWC'), feature_group_count=W_CONV) + conv_b
    a = jax.nn.silu(layer_norm(a, ln_a_g, ln_a_b))
    return a @ w_a_out + b_a_out


def pool_branch(v, w_b_group, b_b_group, ls_b):
    bsz, seq, _ = v.shape
    groups = jnp.split(v, N_POOL_GROUPS, axis=-1)
    mixed = jnp.stack([causal_mean_pool(vg, w) - vg for vg, w in zip(groups, POOL_WINDOWS)], axis=2)
    y = jnp.einsum('bsgc,gco->bsgo', mixed, w_b_group) + b_b_group
    return y.reshape(bsz, seq, D_MODEL) * ls_b


def _fwd_setup_inputs(seed: int = 0) -> dict:
    key = jax.random.key(seed)
    ks = jax.random.split(key, 32)
    f = jnp.float32
    D = D_MODEL

    def nrm(k, shape, scale):
        return jax.random.normal(k, shape, f) * scale

    def gain(k, shape):
        return 1.0 + 0.05 * jax.random.normal(k, shape, f)

    return {
        'x': nrm(ks[0], (BATCH, SEQ, D), 1.0),
        'c': nrm(ks[1], (BATCH, D), 1.0),
        'w_ada': nrm(ks[2], (D, N_ADA * 3 * D), 0.5 * D ** -0.5),
        'b_ada': nrm(ks[3], (N_ADA * 3 * D,), 0.01),
        'g_ffn1': gain(ks[4], (D,)),
        'w1_in': nrm(ks[5], (D, 2 * D_FF), D ** -0.5),
        'w1_out': nrm(ks[6], (D_FF, D), D_FF ** -0.5),
        'g_mix': gain(ks[7], (D,)),
        'w_in': nrm(ks[8], (D, N_IN_COLS), D ** -0.5),
        'conv_w': nrm(ks[9], (CONV_K, W_CONV), CONV_K ** -0.5),
        'conv_b': nrm(ks[10], (W_CONV,), 0.01),
        'ln_a_g': gain(ks[11], (W_CONV,)),
        'ln_a_b': nrm(ks[12], (W_CONV,), 0.01),
        'w_a_out': nrm(ks[13], (W_CONV, D), W_CONV ** -0.5),
        'b_a_out': nrm(ks[14], (D,), 0.01),
        'w_b_group': nrm(ks[15], (N_POOL_GROUPS, POOL_GROUP_IN, POOL_GROUP_OUT), POOL_GROUP_IN ** -0.5),
        'b_b_group': nrm(ks[16], (N_POOL_GROUPS, POOL_GROUP_OUT), 0.01),
        'ls_b': gain(ks[17], (D,)),
        'w_out': nrm(ks[18], (D, D), D ** -0.5),
        'g_ffn2': gain(ks[19], (D,)),
        'w2_in': nrm(ks[20], (D, 2 * D_FF), D ** -0.5),
        'w2_out': nrm(ks[21], (D_FF, D), D_FF ** -0.5),
        'g_final': gain(ks[22], (D,)),
    }


def _fwd_reference(x, c, w_ada, b_ada, g_ffn1, w1_in, w1_out, g_mix, w_in, conv_w, conv_b,
              ln_a_g, ln_a_b, w_a_out, b_a_out, w_b_group, b_b_group, ls_b, w_out,
              g_ffn2, w2_in, w2_out, g_final):
    bsz = x.shape[0]
    ada = (jax.nn.silu(c) @ w_ada + b_ada).reshape(bsz, N_ADA, 3, D_MODEL)
    h = x
    for _ in range(DEPTH):
        n = modulate(rms_norm(h, g_ffn1), ada[:, 0, 0], ada[:, 0, 1])
        h = h + 0.5 * ada[:, 0, 2][:, None, :] * swiglu(n, w1_in, w1_out)

        n = modulate(rms_norm(h, g_mix), ada[:, 1, 0], ada[:, 1, 1])
        proj = n @ w_in
        glu_in = proj[..., :2 * W_CONV]
        pool_in = proj[..., 2 * W_CONV:2 * W_CONV + W_POOL]
        gate_a = jax.nn.sigmoid(proj[..., 2 * W_CONV + W_POOL:2 * W_CONV + W_POOL + D_MODEL])
        gate_b = jax.nn.sigmoid(proj[..., 2 * W_CONV + W_POOL + D_MODEL:])
        y_a = conformer_conv_branch(glu_in, conv_w, conv_b, ln_a_g, ln_a_b, w_a_out, b_a_out)
        y_b = pool_branch(pool_in, w_b_group, b_b_group, ls_b)
        mix = (gate_a * y_a + gate_b * y_b) @ w_out
        h = h + ada[:, 1, 2][:, None, :] * mix

        n = modulate(rms_norm(h, g_ffn2), ada[:, 2, 0], ada[:, 2, 1])
        h = h + 0.5 * ada[:, 2, 2][:, None, :] * swiglu(n, w2_in, w2_out)
    return rms_norm(h, g_final)


import jax as _jax
import jax.numpy as _jnp

TWIN_FORMAT = 'train_step'
FWD_PARAMS = ['x', 'c', 'w_ada', 'b_ada', 'g_ffn1', 'w1_in', 'w1_out', 'g_mix', 'w_in', 'conv_w', 'conv_b', 'ln_a_g', 'ln_a_b', 'w_a_out', 'b_a_out', 'w_b_group', 'b_b_group', 'ls_b', 'w_out', 'g_ffn2', 'w2_in', 'w2_out', 'g_final']
TWIN_WEIGHTS = ['w_ada', 'b_ada', 'g_ffn1', 'w1_in', 'w1_out', 'g_mix', 'w_in', 'conv_w', 'conv_b', 'ln_a_g', 'ln_a_b', 'w_a_out', 'b_a_out', 'w_b_group', 'b_b_group', 'ls_b', 'w_out', 'g_ffn2', 'w2_in', 'w2_out', 'g_final']
TWIN_DIFF_INPUT = 'x'
TWIN_INPUTS = ['x', 'c', 'w_ada', 'b_ada', 'g_ffn1', 'w1_in', 'w1_out', 'g_mix', 'w_in', 'conv_w', 'conv_b', 'ln_a_g', 'ln_a_b', 'w_a_out', 'b_a_out', 'w_b_group', 'b_b_group', 'ls_b', 'w_out', 'g_ffn2', 'w2_in', 'w2_out', 'g_final', 'loss_target', 'm_w_ada', 'm_b_ada', 'm_g_ffn1', 'm_w1_in', 'm_w1_out', 'm_g_mix', 'm_w_in', 'm_conv_w', 'm_conv_b', 'm_ln_a_g', 'm_ln_a_b', 'm_w_a_out', 'm_b_a_out', 'm_w_b_group', 'm_b_b_group', 'm_ls_b', 'm_w_out', 'm_g_ffn2', 'm_w2_in', 'm_w2_out', 'm_g_final', 'v_w_ada', 'v_b_ada', 'v_g_ffn1', 'v_w1_in', 'v_w1_out', 'v_g_mix', 'v_w_in', 'v_conv_w', 'v_conv_b', 'v_ln_a_g', 'v_ln_a_b', 'v_w_a_out', 'v_b_a_out', 'v_w_b_group', 'v_b_b_group', 'v_ls_b', 'v_w_out', 'v_g_ffn2', 'v_w2_in', 'v_w2_out', 'v_g_final']
TWIN_OUTPUTS = ['loss', 'grad_x', 'grad_w_ada', 'grad_b_ada', 'grad_g_ffn1', 'grad_w1_in', 'grad_w1_out', 'grad_g_mix', 'grad_w_in', 'grad_conv_w', 'grad_conv_b', 'grad_ln_a_g', 'grad_ln_a_b', 'grad_w_a_out', 'grad_b_a_out', 'grad_w_b_group', 'grad_b_b_group', 'grad_ls_b', 'grad_w_out', 'grad_g_ffn2', 'grad_w2_in', 'grad_w2_out', 'grad_g_final', 'delta_w_ada', 'delta_b_ada', 'delta_g_ffn1', 'delta_w1_in', 'delta_w1_out', 'delta_g_mix', 'delta_w_in', 'delta_conv_w', 'delta_conv_b', 'delta_ln_a_g', 'delta_ln_a_b', 'delta_w_a_out', 'delta_b_a_out', 'delta_w_b_group', 'delta_b_b_group', 'delta_ls_b', 'delta_w_out', 'delta_g_ffn2', 'delta_w2_in', 'delta_w2_out', 'delta_g_final', 'new_m_w_ada', 'new_m_b_ada', 'new_m_g_ffn1', 'new_m_w1_in', 'new_m_w1_out', 'new_m_g_mix', 'new_m_w_in', 'new_m_conv_w', 'new_m_conv_b', 'new_m_ln_a_g', 'new_m_ln_a_b', 'new_m_w_a_out', 'new_m_b_a_out', 'new_m_w_b_group', 'new_m_b_b_group', 'new_m_ls_b', 'new_m_w_out', 'new_m_g_ffn2', 'new_m_w2_in', 'new_m_w2_out', 'new_m_g_final', 'new_v_w_ada', 'new_v_b_ada', 'new_v_g_ffn1', 'new_v_w1_in', 'new_v_w1_out', 'new_v_g_mix', 'new_v_w_in', 'new_v_conv_w', 'new_v_conv_b', 'new_v_ln_a_g', 'new_v_ln_a_b', 'new_v_w_a_out', 'new_v_b_a_out', 'new_v_w_b_group', 'new_v_b_b_group', 'new_v_ls_b', 'new_v_w_out', 'new_v_g_ffn2', 'new_v_w2_in', 'new_v_w2_out', 'new_v_g_final']
TWIN_LEAF_KINDS = {'loss': 'loss', 'grad_x': 'grad_x', 'grad_w_ada': 'grad_w', 'grad_b_ada': 'grad_w', 'grad_g_ffn1': 'grad_w', 'grad_w1_in': 'grad_w', 'grad_w1_out': 'grad_w', 'grad_g_mix': 'grad_w', 'grad_w_in': 'grad_w', 'grad_conv_w': 'grad_w', 'grad_conv_b': 'grad_w', 'grad_ln_a_g': 'grad_w', 'grad_ln_a_b': 'grad_w', 'grad_w_a_out': 'grad_w', 'grad_b_a_out': 'grad_w', 'grad_w_b_group': 'grad_w', 'grad_b_b_group': 'grad_w', 'grad_ls_b': 'grad_w', 'grad_w_out': 'grad_w', 'grad_g_ffn2': 'grad_w', 'grad_w2_in': 'grad_w', 'grad_w2_out': 'grad_w', 'grad_g_final': 'grad_w', 'delta_w_ada': 'delta_w', 'delta_b_ada': 'delta_w', 'delta_g_ffn1': 'delta_w', 'delta_w1_in': 'delta_w', 'delta_w1_out': 'delta_w', 'delta_g_mix': 'delta_w', 'delta_w_in': 'delta_w', 'delta_conv_w': 'delta_w', 'delta_conv_b': 'delta_w', 'delta_ln_a_g': 'delta_w', 'delta_ln_a_b': 'delta_w', 'delta_w_a_out': 'delta_w', 'delta_b_a_out': 'delta_w', 'delta_w_b_group': 'delta_w', 'delta_b_b_group': 'delta_w', 'delta_ls_b': 'delta_w', 'delta_w_out': 'delta_w', 'delta_g_ffn2': 'delta_w', 'delta_w2_in': 'delta_w', 'delta_w2_out': 'delta_w', 'delta_g_final': 'delta_w', 'new_m_w_ada': 'new_m', 'new_m_b_ada': 'new_m', 'new_m_g_ffn1': 'new_m', 'new_m_w1_in': 'new_m', 'new_m_w1_out': 'new_m', 'new_m_g_mix': 'new_m', 'new_m_w_in': 'new_m', 'new_m_conv_w': 'new_m', 'new_m_conv_b': 'new_m', 'new_m_ln_a_g': 'new_m', 'new_m_ln_a_b': 'new_m', 'new_m_w_a_out': 'new_m', 'new_m_b_a_out': 'new_m', 'new_m_w_b_group': 'new_m', 'new_m_b_b_group': 'new_m', 'new_m_ls_b': 'new_m', 'new_m_w_out': 'new_m', 'new_m_g_ffn2': 'new_m', 'new_m_w2_in': 'new_m', 'new_m_w2_out': 'new_m', 'new_m_g_final': 'new_m', 'new_v_w_ada': 'new_v', 'new_v_b_ada': 'new_v', 'new_v_g_ffn1': 'new_v', 'new_v_w1_in': 'new_v', 'new_v_w1_out': 'new_v', 'new_v_g_mix': 'new_v', 'new_v_w_in': 'new_v', 'new_v_conv_w': 'new_v', 'new_v_conv_b': 'new_v', 'new_v_ln_a_g': 'new_v', 'new_v_ln_a_b': 'new_v', 'new_v_w_a_out': 'new_v', 'new_v_b_a_out': 'new_v', 'new_v_w_b_group': 'new_v', 'new_v_b_b_group': 'new_v', 'new_v_ls_b': 'new_v', 'new_v_w_out': 'new_v', 'new_v_g_ffn2': 'new_v', 'new_v_w2_in': 'new_v', 'new_v_w2_out': 'new_v', 'new_v_g_final': 'new_v'}


def _forward(args):
    return _fwd_reference(*[args[k] for k in FWD_PARAMS])


def _output_shape():
    def fwd():
        inp = _fwd_setup_inputs(0)
        return _fwd_reference(*[inp[k] for k in FWD_PARAMS])
    out = _jax.eval_shape(fwd)
    return out.shape, out.dtype

N_MICROBATCH = 1
ADAM_LR = 0.001
ADAM_B1 = 0.9
ADAM_B2 = 0.999
ADAM_EPS = 1e-08
ADAM_WD = 0.01
ADAM_STEP = 10
PER_EXAMPLE_BATCH_AXIS = {'x': 0, 'c': 0, 'loss_target': 0}
SHARED_INPUTS = []
_WEIGHT_DTYPES = {'w_ada': _jnp.float32, 'b_ada': _jnp.float32, 'g_ffn1': _jnp.float32, 'w1_in': _jnp.float32, 'w1_out': _jnp.float32, 'g_mix': _jnp.float32, 'w_in': _jnp.float32, 'conv_w': _jnp.float32, 'conv_b': _jnp.float32, 'ln_a_g': _jnp.float32, 'ln_a_b': _jnp.float32, 'w_a_out': _jnp.float32, 'b_a_out': _jnp.float32, 'w_b_group': _jnp.float32, 'b_b_group': _jnp.float32, 'ls_b': _jnp.float32, 'w_out': _jnp.float32, 'g_ffn2': _jnp.float32, 'w2_in': _jnp.float32, 'w2_out': _jnp.float32, 'g_final': _jnp.float32}
MOMENT_SCALE = {'w_ada': 1.773818e-02, 'b_ada': 3.204768e-02, 'g_ffn1': 1.310745e-02, 'w1_in': 5.897099e-03, 'w1_out': 9.610183e-03, 'g_mix': 1.777168e-02, 'w_in': 9.512614e-03, 'conv_w': 1.262634e-02, 'conv_b': 2.358923e-02, 'ln_a_g': 1.489079e-02, 'ln_a_b': 1.247489e-02, 'w_a_out': 8.697696e-03, 'b_a_out': 1.495848e-02, 'w_b_group': 1.338913e-02, 'b_b_group': 1.563496e-02, 'ls_b': 1.367685e-02, 'w_out': 1.599925e-02, 'g_ffn2': 1.359656e-02, 'w2_in': 5.743355e-03, 'w2_out': 9.374942e-03, 'g_final': 1.599600e+01}


def _to_microbatches(a, axis):
    t = _jnp.moveaxis(a, axis, 0)
    t = t.reshape((N_MICROBATCH, t.shape[0] // N_MICROBATCH) + t.shape[1:])
    return _jnp.moveaxis(t, 1, axis + 1)


def setup_inputs(seed: int = 0) -> dict:
    inp = _fwd_setup_inputs(seed)
    key = _jax.random.fold_in(_jax.random.key(seed), 7919)
    shape, _ = _output_shape()
    out = dict(inp)
    out["loss_target"] = _jax.random.normal(_jax.random.fold_in(key, 0), shape, _jnp.float32)
    for i, name in enumerate(TWIN_WEIGHTS):
        w = inp[name].astype(_jnp.float32)
        if MOMENT_SCALE is None:
            s = _jnp.sqrt(_jnp.mean(_jnp.square(w)) + 1e-30)
        else:
            s = MOMENT_SCALE[name]
        km, kv = _jax.random.split(_jax.random.fold_in(key, i + 1))
        out[name] = w
        out["m_" + name] = s * _jax.random.normal(km, w.shape, _jnp.float32)
        out["v_" + name] = (s * s) * _jax.random.uniform(kv, w.shape, _jnp.float32, 0.5, 1.5)
    if N_MICROBATCH > 1:
        for name, axis in PER_EXAMPLE_BATCH_AXIS.items():
            out[name] = _to_microbatches(out[name], axis)
    return {'x': out['x'], 'c': out['c'], 'w_ada': out['w_ada'], 'b_ada': out['b_ada'], 'g_ffn1': out['g_ffn1'], 'w1_in': out['w1_in'], 'w1_out': out['w1_out'], 'g_mix': out['g_mix'], 'w_in': out['w_in'], 'conv_w': out['conv_w'], 'conv_b': out['conv_b'], 'ln_a_g': out['ln_a_g'], 'ln_a_b': out['ln_a_b'], 'w_a_out': out['w_a_out'], 'b_a_out': out['b_a_out'], 'w_b_group': out['w_b_group'], 'b_b_group': out['b_b_group'], 'ls_b': out['ls_b'], 'w_out': out['w_out'], 'g_ffn2': out['g_ffn2'], 'w2_in': out['w2_in'], 'w2_out': out['w2_out'], 'g_final': out['g_final'], 'loss_target': out['loss_target'], 'm_w_ada': out['m_w_ada'], 'm_b_ada': out['m_b_ada'], 'm_g_ffn1': out['m_g_ffn1'], 'm_w1_in': out['m_w1_in'], 'm_w1_out': out['m_w1_out'], 'm_g_mix': out['m_g_mix'], 'm_w_in': out['m_w_in'], 'm_conv_w': out['m_conv_w'], 'm_conv_b': out['m_conv_b'], 'm_ln_a_g': out['m_ln_a_g'], 'm_ln_a_b': out['m_ln_a_b'], 'm_w_a_out': out['m_w_a_out'], 'm_b_a_out': out['m_b_a_out'], 'm_w_b_group': out['m_w_b_group'], 'm_b_b_group': out['m_b_b_group'], 'm_ls_b': out['m_ls_b'], 'm_w_out': out['m_w_out'], 'm_g_ffn2': out['m_g_ffn2'], 'm_w2_in': out['m_w2_in'], 'm_w2_out': out['m_w2_out'], 'm_g_final': out['m_g_final'], 'v_w_ada': out['v_w_ada'], 'v_b_ada': out['v_b_ada'], 'v_g_ffn1': out['v_g_ffn1'], 'v_w1_in': out['v_w1_in'], 'v_w1_out': out['v_w1_out'], 'v_g_mix': out['v_g_mix'], 'v_w_in': out['v_w_in'], 'v_conv_w': out['v_conv_w'], 'v_conv_b': out['v_conv_b'], 'v_ln_a_g': out['v_ln_a_g'], 'v_ln_a_b': out['v_ln_a_b'], 'v_w_a_out': out['v_w_a_out'], 'v_b_a_out': out['v_b_a_out'], 'v_w_b_group': out['v_w_b_group'], 'v_b_b_group': out['v_b_b_group'], 'v_ls_b': out['v_ls_b'], 'v_w_out': out['v_w_out'], 'v_g_ffn2': out['v_g_ffn2'], 'v_w2_in': out['v_w2_in'], 'v_w2_out': out['v_w2_out'], 'v_g_final': out['v_g_final']}


def _loss(weights, diff, rest, loss_target):
    with _jax.named_scope("forward"):
        args = {**rest, TWIN_DIFF_INPUT: diff, **{k: w.astype(_WEIGHT_DTYPES[k]) for k, w in weights.items()}}
        y = _forward(args)
    with _jax.named_scope("loss_head"):
        err = _jnp.square(y.astype(_jnp.float32) - loss_target)
        return 0.5 * _jnp.sum(_jnp.mean(err, axis=-1)) if err.ndim else 0.5 * err


def _adamw(w, g, m, v):
    m = ADAM_B1 * m + (1.0 - ADAM_B1) * g
    v = ADAM_B2 * v + (1.0 - ADAM_B2) * _jnp.square(g)
    m_hat = m / (1.0 - ADAM_B1 ** ADAM_STEP)
    v_hat = v / (1.0 - ADAM_B2 ** ADAM_STEP)
    delta = -ADAM_LR * (m_hat / (_jnp.sqrt(v_hat) + ADAM_EPS) + ADAM_WD * w)
    return delta, m, v


def reference(x, c, w_ada, b_ada, g_ffn1, w1_in, w1_out, g_mix, w_in, conv_w, conv_b, ln_a_g, ln_a_b, w_a_out, b_a_out, w_b_group, b_b_group, ls_b, w_out, g_ffn2, w2_in, w2_out, g_final, loss_target, m_w_ada, m_b_ada, m_g_ffn1, m_w1_in, m_w1_out, m_g_mix, m_w_in, m_conv_w, m_conv_b, m_ln_a_g, m_ln_a_b, m_w_a_out, m_b_a_out, m_w_b_group, m_b_b_group, m_ls_b, m_w_out, m_g_ffn2, m_w2_in, m_w2_out, m_g_final, v_w_ada, v_b_ada, v_g_ffn1, v_w1_in, v_w1_out, v_g_mix, v_w_in, v_conv_w, v_conv_b, v_ln_a_g, v_ln_a_b, v_w_a_out, v_b_a_out, v_w_b_group, v_b_b_group, v_ls_b, v_w_out, v_g_ffn2, v_w2_in, v_w2_out, v_g_final):
    given = dict(x=x, c=c, w_ada=w_ada, b_ada=b_ada, g_ffn1=g_ffn1, w1_in=w1_in, w1_out=w1_out, g_mix=g_mix, w_in=w_in, conv_w=conv_w, conv_b=conv_b, ln_a_g=ln_a_g, ln_a_b=ln_a_b, w_a_out=w_a_out, b_a_out=b_a_out, w_b_group=w_b_group, b_b_group=b_b_group, ls_b=ls_b, w_out=w_out, g_ffn2=g_ffn2, w2_in=w2_in, w2_out=w2_out, g_final=g_final, loss_target=loss_target, m_w_ada=m_w_ada, m_b_ada=m_b_ada, m_g_ffn1=m_g_ffn1, m_w1_in=m_w1_in, m_w1_out=m_w1_out, m_g_mix=m_g_mix, m_w_in=m_w_in, m_conv_w=m_conv_w, m_conv_b=m_conv_b, m_ln_a_g=m_ln_a_g, m_ln_a_b=m_ln_a_b, m_w_a_out=m_w_a_out, m_b_a_out=m_b_a_out, m_w_b_group=m_w_b_group, m_b_b_group=m_b_b_group, m_ls_b=m_ls_b, m_w_out=m_w_out, m_g_ffn2=m_g_ffn2, m_w2_in=m_w2_in, m_w2_out=m_w2_out, m_g_final=m_g_final, v_w_ada=v_w_ada, v_b_ada=v_b_ada, v_g_ffn1=v_g_ffn1, v_w1_in=v_w1_in, v_w1_out=v_w1_out, v_g_mix=v_g_mix, v_w_in=v_w_in, v_conv_w=v_conv_w, v_conv_b=v_conv_b, v_ln_a_g=v_ln_a_g, v_ln_a_b=v_ln_a_b, v_w_a_out=v_w_a_out, v_b_a_out=v_b_a_out, v_w_b_group=v_w_b_group, v_b_b_group=v_b_b_group, v_ls_b=v_ls_b, v_w_out=v_w_out, v_g_ffn2=v_g_ffn2, v_w2_in=v_w2_in, v_w2_out=v_w2_out, v_g_final=v_g_final)
    weights = {n: given[n] for n in TWIN_WEIGHTS}
    shared = {n: given[n] for n in SHARED_INPUTS}
    per_example = {n: given[n] for n in ['x', 'c']}
    grad_fn = _jax.value_and_grad(_loss, argnums=(0, 1))

    def one_microbatch(ex, loss_target):
        ex = dict(ex)
        diff = ex.pop(TWIN_DIFF_INPUT)
        return grad_fn(weights, diff, {**shared, **ex}, loss_target)

    if N_MICROBATCH == 1:
        loss, (grad_w, grad_x) = one_microbatch(per_example, given["loss_target"])
    else:
        def body(carry, xs):
            loss_sum, grad_sum = carry
            l_k, (gw_k, gx_k) = one_microbatch(xs[0], xs[1])
            with _jax.named_scope("update"):
                return (loss_sum + l_k, _jax.tree.map(_jnp.add, grad_sum, gw_k)), gx_k

        init = (_jnp.zeros((), _jnp.float32), _jax.tree.map(_jnp.zeros_like, weights))
        (loss, grad_w), grad_x = _jax.lax.scan(body, init, (per_example, given["loss_target"]))
    with _jax.named_scope("update"):
        delta_w, new_m, new_v = {}, {}, {}
        for n in TWIN_WEIGHTS:
            delta_w[n], new_m[n], new_v[n] = _adamw(weights[n], grad_w[n], given["m_" + n], given["v_" + n])
    return (loss, grad_x, *[grad_w[n] for n in TWIN_WEIGHTS], *[delta_w[n] for n in TWIN_WEIGHTS],
            *[new_m[n] for n in TWIN_WEIGHTS], *[new_v[n] for n in TWIN_WEIGHTS])
```

```python
import math

import jax
import jax.numpy as jnp
from jax import lax
from jax.experimental import pallas as pl
from jax.experimental.pallas import tpu as pltpu

F32 = jnp.float32
BF16 = jnp.bfloat16
MESH = pl.DeviceIdType.MESH
ANY = pl.BlockSpec(memory_space=pl.ANY)

EPS = 1e-6
CONV_K = 31
HALO = 32
POOL_WINDOWS = (2, 4, 8, 16)
N_CHIPS = 4
N_DEV = 8
LANES = 128

ADAM_LR = 0.001
ADAM_B1 = 0.9
ADAM_B2 = 0.999
ADAM_EPS = 1e-08
ADAM_WD = 0.01
ADAM_STEP = 10

DN = {
    "nn": (((1,), (0,)), ((), ())),
    "nt": (((1,), (1,)), ((), ())),
    "tn": (((0,), (0,)), ((), ())),
}


def _pcall(body, *, name, out_shape, grid=None, in_specs=None, out_specs=None, scratch=(), aliases=None,
           prefetch=0, vmem_mb=None):
    params = {}
    if grid is not None:
        params["dimension_semantics"] = ("arbitrary",) * len(grid)
    if vmem_mb is not None:
        params["vmem_limit_bytes"] = vmem_mb << 20
    kw = dict(name=name, out_shape=out_shape, compiler_params=pltpu.CompilerParams(**params))
    if aliases:
        kw["input_output_aliases"] = aliases
    if prefetch:
        kw["grid_spec"] = pltpu.PrefetchScalarGridSpec(
            num_scalar_prefetch=prefetch, grid=grid, in_specs=in_specs, out_specs=out_specs,
            scratch_shapes=list(scratch))
    else:
        if grid is not None:
            kw["grid"] = grid
        kw["in_specs"] = in_specs
        kw["out_specs"] = out_specs
        kw["scratch_shapes"] = list(scratch)
    return pl.pallas_call(body, **kw)


def _tile(dim, pref):
    t = min(dim, pref)
    assert dim % t == 0, (dim, pref)
    return t


def _sds(shape, dtype):
    return jax.ShapeDtypeStruct(tuple(shape), dtype)


def _sigmoid(v):
    return 1.0 / (1.0 + jnp.exp(-v))


def _vec(w):
    return pl.BlockSpec((1, w), lambda *_: (0, 0))


def _acc_rows(ref, val, i):
    @pl.when(i == 0)
    def _():
        ref[...] = jnp.zeros_like(ref)

    ref[...] += jnp.sum(val, axis=0, keepdims=True)


def _matmul(name, a, bs, *, mode, grid, a_spec, b_specs, out_shape, out_specs, acc_shape, epilogue,
            extras=(), extra_specs=(), vmem_mb=48):
    nb, ne, nk = len(bs), len(extras), grid[2]
    dn = DN[mode]

    def body(*refs):
        a_ref, b_refs, ex = refs[0], refs[1:1 + nb], refs[1 + nb:1 + nb + ne]
        if nk == 1:
            outs = refs[1 + nb + ne:]
            accs = [lax.dot_general(a_ref[...], b[...], dn, preferred_element_type=F32) for b in b_refs]
            epilogue(accs, ex, outs)
            return
        outs, acc_refs = refs[1 + nb + ne:-nb], refs[-nb:]
        k = pl.program_id(2)

        @pl.when(k == 0)
        def _():
            for acc in acc_refs:
                acc[...] = jnp.zeros_like(acc)

        for acc, b in zip(acc_refs, b_refs):
            acc[...] += lax.dot_general(a_ref[...], b[...], dn, preferred_element_type=F32)

        @pl.when(k == nk - 1)
        def _():
            epilogue([acc[...] for acc in acc_refs], ex, outs)

    scratch = [pltpu.VMEM(acc_shape, F32) for _ in range(nb)] if nk > 1 else []
    return _pcall(body, name=name, out_shape=out_shape, grid=grid,
                  in_specs=[a_spec, *b_specs, *extra_specs], out_specs=out_specs, scratch=scratch,
                  vmem_mb=vmem_mb)(a, *bs, *extras)


def _ep_store(dtype):
    def ep(accs, ex, outs):
        outs[0][...] = accs[0].astype(dtype)
    return ep


def _ep_halves(h):
    def ep(accs, ex, outs):
        outs[0][0] = accs[0][:h]
        outs[0][1] = accs[0][h:]
    return ep


def _place():
    x, y, c = lax.axis_index("x"), lax.axis_index("y"), lax.axis_index("c")
    chips = [(1 - x, y), (x, 1 - y), (1 - x, 1 - y)]
    return x, y, c, chips


def _allgather_small(name, block):
    m_per, n = block.shape

    def body(x_ref, out_ref, send_sems, recv_sems, local_sem):
        x, y, c, chips = _place()
        me, sibling = (x, y, c), (x, y, 1 - c)

        def rows(px, py, pc):
            return out_ref.at[pl.ds((4 * px + 2 * py + pc) * m_per, m_per), :]

        def copy(k, blk, to, src=None):
            return pltpu.make_async_remote_copy(
                src_ref=rows(*blk) if src is None else src, dst_ref=rows(*blk),
                send_sem=send_sems.at[k], recv_sem=recv_sems.at[k], device_id=to, device_id_type=MESH)

        mine = pltpu.make_async_copy(x_ref, rows(*me), local_sem)
        mine.start()
        first = [copy(0, me, sibling, src=x_ref)]
        first += [copy(1 + j, me, (*chip, c), src=x_ref) for j, chip in enumerate(chips)]
        for cp in first:
            cp.start()
        passed = [copy(4 + j, (*chip, c), sibling) for j, chip in enumerate(chips)]
        for j, chip in enumerate(chips):
            copy(1 + j, (*chip, c), me).wait_recv()
            passed[j].start()
        copy(0, sibling, me).wait_recv()
        for j, chip in enumerate(chips):
            copy(4 + j, (*chip, 1 - c), me).wait_recv()
        for cp in first + passed:
            cp.wait_send()
        mine.wait()

    return _pcall(
        body, name=name, out_shape=_sds((N_DEV * m_per, n), block.dtype),
        in_specs=[pl.BlockSpec(memory_space=pltpu.VMEM)], out_specs=pl.BlockSpec(memory_space=pltpu.VMEM),
        scratch=[pltpu.SemaphoreType.DMA((7,)), pltpu.SemaphoreType.DMA((7,)), pltpu.SemaphoreType.DMA],
    )(block)


def _allgather_weights(name, shards):
    n = len(shards)
    halves = [w.reshape(2, w.shape[0] // 2, w.shape[1]) for w in shards]

    def body(*refs):
        ins, outs = refs[:n], refs[n:2 * n]
        send1, recv1, send2, recv2, lsem = refs[2 * n:]
        x, y, c, chips = _place()
        q = 2 * x + y
        local = []
        for w in range(n):
            for hf in range(2):
                cp = pltpu.make_async_copy(ins[w].at[hf], outs[w].at[q, hf], lsem.at[2 * w + hf])
                cp.start()
                local.append(cp)

        def ici(w, j, src_chip_q, to):
            return pltpu.make_async_remote_copy(
                src_ref=ins[w].at[c], dst_ref=outs[w].at[src_chip_q, c],
                send_sem=send1.at[3 * w + j], recv_sem=recv1.at[3 * w + j], device_id=to, device_id_type=MESH)

        def d2d(w, j, blk_q, half):
            return pltpu.make_async_remote_copy(
                src_ref=outs[w].at[blk_q, half], dst_ref=outs[w].at[blk_q, half],
                send_sem=send2.at[3 * w + j], recv_sem=recv2.at[3 * w + j], device_id=(x, y, 1 - c),
                device_id_type=MESH)

        first = [ici(w, j, q, (*chip, c)) for w in range(n) for j, chip in enumerate(chips)]
        for cp in first:
            cp.start()
        passed = []
        for w in range(n):
            for j, (px, py) in enumerate(chips):
                ici(w, j, 2 * px + py, (x, y, c)).wait_recv()
                cp = d2d(w, j, 2 * px + py, c)
                cp.start()
                passed.append(cp)
        for w in range(n):
            for j, (px, py) in enumerate(chips):
                d2d(w, j, 2 * px + py, 1 - c).wait_recv()
        for cp in first + passed:
            cp.wait_send()
        for cp in local:
            cp.wait()

    outs = _pcall(
        body, name=name, out_shape=[_sds((N_CHIPS, *h.shape), h.dtype) for h in halves],
        in_specs=[ANY] * n, out_specs=[ANY] * n,
        scratch=[pltpu.SemaphoreType.DMA((3 * n,)) for _ in range(4)] + [pltpu.SemaphoreType.DMA((2 * n,))],
    )(*halves)
    return [o.reshape(N_CHIPS, 2 * o.shape[2], o.shape[3]) for o in outs]


def _exchange_sibling_halves(name, grads):
    n = len(grads)

    def body(*refs):
        ins, outs, send, recv = refs[:n], refs[n:2 * n], refs[2 * n], refs[2 * n + 1]
        x, y, c, _ = _place()
        cps = []
        for w in range(n):
            cp = pltpu.make_async_remote_copy(
                src_ref=ins[w].at[1 - c], dst_ref=outs[w], send_sem=send.at[w], recv_sem=recv.at[w],
                device_id=(x, y, 1 - c), device_id_type=MESH)
            cp.start()
            cps.append(cp)
        for cp in cps:
            cp.wait()

    return _pcall(
        body, name=name, out_shape=[_sds(g.shape[1:], g.dtype) for g in grads],
        in_specs=[ANY] * n, out_specs=[ANY] * n,
        scratch=[pltpu.SemaphoreType.DMA((n,)), pltpu.SemaphoreType.DMA((n,))],
    )(*grads)


def _exchange_chip_sums(name, sums):
    n = len(sums)

    def body(*refs):
        ins, outs, send, recv = refs[:n], refs[n:2 * n], refs[2 * n], refs[2 * n + 1]
        x, y, c, chips = _place()
        cps = []
        for w in range(n):
            for j, (px, py) in enumerate(chips):
                cp = pltpu.make_async_remote_copy(
                    src_ref=ins[w].at[2 * px + py], dst_ref=outs[w].at[j],
                    send_sem=send.at[3 * w + j], recv_sem=recv.at[3 * w + j], device_id=(px, py, c),
                    device_id_type=MESH)
                cp.start()
                cps.append(cp)
        for cp in cps:
            cp.wait()

    return _pcall(
        body, name=name, out_shape=[_sds((3, *s.shape[1:]), s.dtype) for s in sums],
        in_specs=[ANY] * n, out_specs=[ANY] * n,
        scratch=[pltpu.SemaphoreType.DMA((3 * n,)), pltpu.SemaphoreType.DMA((3 * n,))],
    )(*sums)


def _exchange_final_halves(name, finals):
    n = len(finals)

    def body(*refs):
        ins, outs, send, recv = refs[:n], refs[n:2 * n], refs[2 * n], refs[2 * n + 1]
        x, y, c, _ = _place()
        cps = []
        for w in range(n):
            cp = pltpu.make_async_remote_copy(
                src_ref=outs[w].at[c], dst_ref=outs[w].at[c], send_sem=send.at[w], recv_sem=recv.at[w],
                device_id=(x, y, 1 - c), device_id_type=MESH)
            cp.start()
            cps.append(cp)
        for cp in cps:
            cp.wait_send()
        for w in range(n):
            pltpu.make_async_remote_copy(
                src_ref=outs[w].at[1 - c], dst_ref=outs[w].at[1 - c], send_sem=send.at[w], recv_sem=recv.at[w],
                device_id=(x, y, 1 - c), device_id_type=MESH).wait_recv()

    return _pcall(
        body, name=name, out_shape=[_sds(f.shape, f.dtype) for f in finals],
        in_specs=[ANY] * n, out_specs=[ANY] * n, aliases={w: w for w in range(n)},
        scratch=[pltpu.SemaphoreType.DMA((n,)), pltpu.SemaphoreType.DMA((n,))],
    )(*finals)


def _row_tile(rows, cols, budget_elems=393216):
    best = 8
    for t in range(8, rows + 1, 8):
        if rows % t == 0 and t * cols <= budget_elems:
            best = t
    return best if rows % best == 0 else rows


def _sum_with_sibling(name, grad, recv, c_idx):
    _, _, h, cols = grad.shape
    tr = _row_tile(h, cols)

    def body(s_ref, g_ref, r_ref, p_ref, pb_ref):
        p = g_ref[...] + r_ref[...]
        p_ref[...] = p
        pb_ref[...] = p.astype(BF16)

    blk = pl.BlockSpec((None, tr, cols), lambda k, r, s: (k, r, 0))
    return _pcall(
        body, name=name, out_shape=[_sds((N_CHIPS, h, cols), F32), _sds((N_CHIPS, h, cols), BF16)],
        grid=(N_CHIPS, h // tr), prefetch=1,
        in_specs=[pl.BlockSpec((None, None, tr, cols), lambda k, r, s: (s[0], k, r, 0)), blk],
        out_specs=[blk, blk], vmem_mb=32,
    )(c_idx, grad, recv)


def _sum_chips(name, own, recv, qc_idx):
    _, h, cols = own.shape
    tr = _row_tile(h, cols)

    def body(s_ref, p_ref, t_ref, o_ref):
        o_ref[...] = ((p_ref[...] + t_ref[0].astype(F32)) + t_ref[1].astype(F32)) + t_ref[2].astype(F32)

    return _pcall(
        body, name=name, out_shape=_sds((2, h, cols), F32), grid=(h // tr,), prefetch=1,
        in_specs=[pl.BlockSpec((None, tr, cols), lambda r, s: (s[0], r, 0)),
                  pl.BlockSpec((3, tr, cols), lambda r, s: (0, r, 0))],
        out_specs=pl.BlockSpec((None, tr, cols), lambda r, s: (s[1], r, 0)), vmem_mb=32,
    )(qc_idx, own, recv)


def _reduce_scatter(tag, grads, c_idx, qc_idx):
    names = [f"{tag}{i}" for i in range(len(grads))]
    from_sibling = _exchange_sibling_halves(f"{tag}_rs_sibling", grads)
    sums = [_sum_with_sibling(f"{nm}_sum_sibling", g, r, c_idx) for nm, g, r in zip(names, grads, from_sibling)]
    from_chips = _exchange_chip_sums(f"{tag}_rs_chips", [s[1] for s in sums])
    finals = [_sum_chips(f"{nm}_sum_chips", s[0], t, qc_idx) for nm, s, t in zip(names, sums, from_chips)]
    finals = _exchange_final_halves(f"{tag}_rs_final", finals)
    return [f.reshape(2 * f.shape[1], f.shape[2]) for f in finals]


def _cast_bf16(name, w):
    rows, cols = w.shape
    tr = _row_tile(rows, cols, 1 << 20)

    def body(w_ref, o_ref):
        o_ref[...] = w_ref[...].astype(BF16)

    spec = pl.BlockSpec((tr, cols), lambda i: (i, 0))
    return _pcall(body, name=name, out_shape=_sds(w.shape, BF16), grid=(rows // tr,), in_specs=[spec],
                  out_specs=spec, vmem_mb=32)(w)


def _rms(h):
    r = lax.rsqrt(jnp.mean(h * h, axis=-1, keepdims=True) + EPS)
    return r, h * r


def _norm_mod(name, h, g, sc, sh, ts):
    s_len, d = h.shape

    def body(h_ref, g_ref, sc_ref, sh_ref, n_ref):
        _, xhat = _rms(h_ref[...])
        n_ref[...] = ((xhat * g_ref[...]) * (1.0 + sc_ref[...]) + sh_ref[...]).astype(BF16)

    row = pl.BlockSpec((ts, d), lambda i: (i, 0))
    return _pcall(body, name=name, out_shape=_sds((s_len, d), BF16), grid=(s_len // ts,),
                  in_specs=[row, _vec(d), _vec(d), _vec(d)], out_specs=row, vmem_mb=32)(h, g, sc, sh)


def _residual_norm_mod(name, h, f, gate, cmul, g, sc, sh, ts):
    s_len, d = h.shape

    def body(h_ref, f_ref, gt_ref, g_ref, sc_ref, sh_ref, ho_ref, n_ref):
        hn = h_ref[...] + (cmul * gt_ref[...]) * f_ref[...]
        ho_ref[...] = hn
        _, xhat = _rms(hn)
        n_ref[...] = ((xhat * g_ref[...]) * (1.0 + sc_ref[...]) + sh_ref[...]).astype(BF16)

    row = pl.BlockSpec((ts, d), lambda i: (i, 0))
    return _pcall(body, name=name, out_shape=[_sds((s_len, d), F32), _sds((s_len, d), BF16)],
                  grid=(s_len // ts,), in_specs=[row, row, _vec(d), _vec(d), _vec(d), _vec(d)],
                  out_specs=[row, row], vmem_mb=32)(h, f, gate, g, sc, sh)


def _final_loss(name, h, f, tgt, gate, cmul, g, ts):
    s_len, d = h.shape

    def body(h_ref, f_ref, t_ref, gt_ref, g_ref, dh_ref, df_ref, dg_ref, dgt_ref, loss_ref):
        i = pl.program_id(0)
        fv = f_ref[...]
        coef = cmul * gt_ref[...]
        hn = h_ref[...] + coef * fv
        r, xhat = _rms(hn)
        err = xhat * g_ref[...] - t_ref[...]
        _acc_rows(loss_ref, (0.5 / d) * (err * err), i)
        dy = err * (1.0 / d)
        _acc_rows(dg_ref, dy * xhat, i)
        dxhat = dy * g_ref[...]
        dh = r * (dxhat - xhat * jnp.mean(dxhat * xhat, axis=-1, keepdims=True))
        dh_ref[...] = dh
        _acc_rows(dgt_ref, cmul * (dh * fv), i)
        df_ref[...] = (coef * dh).astype(BF16)

    row = pl.BlockSpec((ts, d), lambda i: (i, 0))
    return _pcall(body, name=name,
                  out_shape=[_sds((s_len, d), F32), _sds((s_len, d), BF16)] + [_sds((1, d), F32)] * 3,
                  grid=(s_len // ts,), in_specs=[row, row, row, _vec(d), _vec(d)],
                  out_specs=[row, row, _vec(d), _vec(d), _vec(d)], vmem_mb=40)(h, f, tgt, gate, g)


def _norm_mod_bwd(name, h, dn, dh_next, g, sc, ts, prev=None):
    s_len, d = h.shape
    has_prev = prev is not None
    cmul = prev[2] if has_prev else None

    def body(*refs):
        if has_prev:
            h_ref, dn_ref, dhn_ref, f_ref, g_ref, sc_ref, gt_ref, dh_ref, df_ref, dsh_ref, dsc_ref, dg_ref, dgt_ref = refs
        else:
            h_ref, dn_ref, dhn_ref, g_ref, sc_ref, dh_ref, dsh_ref, dsc_ref, dg_ref = refs
        i = pl.program_id(0)
        r, xhat = _rms(h_ref[...])
        dn_v = dn_ref[...]
        gv = g_ref[...]
        _acc_rows(dsh_ref, dn_v, i)
        _acc_rows(dsc_ref, dn_v * (xhat * gv), i)
        dnrm = dn_v * (1.0 + sc_ref[...])
        _acc_rows(dg_ref, dnrm * xhat, i)
        dxhat = dnrm * gv
        dh = dhn_ref[...] + r * (dxhat - xhat * jnp.mean(dxhat * xhat, axis=-1, keepdims=True))
        dh_ref[...] = dh
        if has_prev:
            _acc_rows(dgt_ref, cmul * (dh * f_ref[...]), i)
            df_ref[...] = ((cmul * gt_ref[...]) * dh).astype(BF16)

    row = pl.BlockSpec((ts, d), lambda i: (i, 0))
    if has_prev:
        ins, in_specs = [h, dn, dh_next, prev[0], g, sc, prev[1]], [row, row, row, row, _vec(d), _vec(d), _vec(d)]
        out_shape = [_sds((s_len, d), F32), _sds((s_len, d), BF16)] + [_sds((1, d), F32)] * 4
        out_specs = [row, row] + [_vec(d)] * 4
    else:
        ins, in_specs = [h, dn, dh_next, g, sc], [row, row, row, _vec(d), _vec(d)]
        out_shape = [_sds((s_len, d), F32)] + [_sds((1, d), F32)] * 3
        out_specs = [row] + [_vec(d)] * 3
    return _pcall(body, name=name, out_shape=out_shape, grid=(s_len // ts,), in_specs=in_specs,
                  out_specs=out_specs, vmem_mb=40)(*ins)


def _cols(ref, lo, hi, npc):
    parts = []
    while lo < hi:
        q, o = divmod(lo, npc)
        n = min(hi - lo, npc - o)
        parts.append(ref[q, :, o:o + n])
        lo += n
    return parts[0] if len(parts) == 1 else jnp.concatenate(parts, axis=-1)


def _store_cols(ref, lo, val, npc):
    off, width = 0, val.shape[-1]
    while off < width:
        q, o = divmod(lo + off, npc)
        n = min(width - off, npc - o)
        ref[q, :, o:o + n] = val[:, off:off + n]
        off += n


def _chips_covering(cols, npc):
    return -(-cols // npc)


def _conv_ln(a0s_ref, cw_ref, cb_ref, lg_ref, lb_ref, ts):
    a1 = cb_ref[...] + cw_ref[0:1, :] * a0s_ref[pl.ds(HALO - CONV_K + 1, ts), :]
    for k in range(1, CONV_K):
        a1 = a1 + cw_ref[k:k + 1, :] * a0s_ref[pl.ds(HALO - CONV_K + 1 + k, ts), :]
    mu = jnp.mean(a1, axis=-1, keepdims=True)
    ctr = a1 - mu
    rstd = lax.rsqrt(jnp.mean(ctr * ctr, axis=-1, keepdims=True) + EPS)
    xh = ctr * rstd
    return xh, rstd, xh * lg_ref[...] + lb_ref[...]


def _stage_glu(p_ref, ph_ref, a0s_ref, i, wc, npc, ts):
    a0 = _cols(p_ref, 0, wc, npc) * _sigmoid(_cols(p_ref, wc, 2 * wc, npc))
    a0h = _cols(ph_ref, 0, wc, npc) * _sigmoid(_cols(ph_ref, wc, 2 * wc, npc))
    a0s_ref[0:HALO, :] = jnp.where(i > 0, a0h, 0.0)
    a0s_ref[HALO:HALO + ts, :] = a0


def _mixer_mid(name, proj, cw, cb, lg, lb, wc, wp, ts):
    _, s_len, npc = proj.shape
    nq = _chips_covering(2 * wc + wp, npc)
    gi = wp // len(POOL_WINDOWS)
    hb = ts // HALO

    def body(p_ref, ph_ref, cw_ref, cb_ref, lg_ref, lb_ref, a3_ref, mx_ref, a0s_ref, vs_ref):
        i = pl.program_id(0)
        _stage_glu(p_ref, ph_ref, a0s_ref, i, wc, npc, ts)
        vs_ref[0:HALO, :] = jnp.where(i > 0, _cols(ph_ref, 2 * wc, 2 * wc + wp, npc), 0.0)
        vs_ref[HALO:HALO + ts, :] = _cols(p_ref, 2 * wc, 2 * wc + wp, npc)
        _, _, a2 = _conv_ln(a0s_ref, cw_ref, cb_ref, lg_ref, lb_ref, ts)
        a3_ref[...] = (a2 * _sigmoid(a2)).astype(BF16)
        t_abs = i * ts + lax.broadcasted_iota(jnp.int32, (ts, 1), 0)
        for g, win in enumerate(POOL_WINDOWS):
            cs = slice(g * gi, (g + 1) * gi)
            acc = vs_ref[pl.ds(HALO, ts), cs]
            for dlt in range(1, win):
                acc = acc + vs_ref[pl.ds(HALO - dlt, ts), cs]
            cnt = jnp.minimum(t_abs + 1, win).astype(F32)
            mx_ref[:, cs] = (acc / cnt - vs_ref[pl.ds(HALO, ts), cs]).astype(BF16)

    return _pcall(
        body, name=name, out_shape=[_sds((s_len, wc), BF16), _sds((s_len, wp), BF16)], grid=(s_len // ts,),
        in_specs=[pl.BlockSpec((nq, ts, npc), lambda i: (0, i, 0)),
                  pl.BlockSpec((nq, HALO, npc), lambda i: (0, jnp.maximum(i * hb - 1, 0), 0)),
                  pl.BlockSpec((HALO, wc), lambda i: (0, 0)), _vec(wc), _vec(wc), _vec(wc)],
        out_specs=[pl.BlockSpec((ts, wc), lambda i: (i, 0)), pl.BlockSpec((ts, wp), lambda i: (i, 0))],
        scratch=[pltpu.VMEM((HALO + ts, wc), F32), pltpu.VMEM((HALO + ts, wp), F32)], vmem_mb=48,
    )(proj, proj, cw, cb, lg, lb)


def _gates_fwd(name, proj, ya, yb, b_a, b_b, ls, wc, wp, ts):
    _, s_len, npc = proj.shape
    d = ya.shape[1]
    g0 = 2 * wc + wp

    def body(p_ref, ya_ref, yb_ref, ba_ref, bb_ref, ls_ref, z_ref):
        ga = _sigmoid(_cols(p_ref, g0, g0 + d, npc))
        gb = _sigmoid(_cols(p_ref, g0 + d, g0 + 2 * d, npc))
        z = ga * (ya_ref[...] + ba_ref[...]) + gb * ((yb_ref[...] + bb_ref[...]) * ls_ref[...])
        z_ref[...] = z.astype(BF16)

    row = pl.BlockSpec((ts, d), lambda i: (i, 0))
    return _pcall(body, name=name, out_shape=_sds((s_len, d), BF16), grid=(s_len // ts,),
                  in_specs=[pl.BlockSpec((N_CHIPS, ts, npc), lambda i: (0, i, 0)), row, row, _vec(d), _vec(d), _vec(d)],
                  out_specs=row, vmem_mb=48)(proj, ya, yb, b_a, b_b, ls)


def _gates_bwd(name, proj, dz, ya, yb, b_a, b_b, ls, wc, wp, ts):
    _, s_len, npc = proj.shape
    d = ya.shape[1]
    g0 = 2 * wc + wp

    def body(p_ref, dz_ref, ya_ref, yb_ref, ba_ref, bb_ref, ls_ref, dya_ref, dyb_ref, dgt_ref, dba_ref, dls_ref,
             dbb_ref):
        i = pl.program_id(0)
        ga = _sigmoid(_cols(p_ref, g0, g0 + d, npc))
        gb = _sigmoid(_cols(p_ref, g0 + d, g0 + 2 * d, npc))
        dz_v = dz_ref[...]
        y_a = ya_ref[...] + ba_ref[...]
        y_b0 = yb_ref[...] + bb_ref[...]
        ls_v = ls_ref[...]
        dya = dz_v * ga
        dya_ref[...] = dya.astype(BF16)
        _acc_rows(dba_ref, dya, i)
        t = dz_v * gb
        _acc_rows(dls_ref, t * y_b0, i)
        dyb = t * ls_v
        dyb_ref[...] = dyb.astype(BF16)
        _acc_rows(dbb_ref, dyb, i)
        dgt_ref[:, 0:d] = (dz_v * y_a * ga * (1.0 - ga)).astype(BF16)
        dgt_ref[:, d:2 * d] = (dz_v * (y_b0 * ls_v) * gb * (1.0 - gb)).astype(BF16)

    row = pl.BlockSpec((ts, d), lambda i: (i, 0))
    return _pcall(
        body, name=name,
        out_shape=[_sds((s_len, d), BF16), _sds((s_len, d), BF16), _sds((s_len, 2 * d), BF16)] + [_sds((1, d), F32)] * 3,
        grid=(s_len // ts,),
        in_specs=[pl.BlockSpec((N_CHIPS, ts, npc), lambda i: (0, i, 0)), row, row, row, _vec(d), _vec(d), _vec(d)],
        out_specs=[row, row, pl.BlockSpec((ts, 2 * d), lambda i: (i, 0))] + [_vec(d)] * 3, vmem_mb=48,
    )(proj, dz, ya, yb, b_a, b_b, ls)


def _conv_branch_bwd(name, proj, da3, cw, cb, lg, lb, wc, wp, ts):
    _, s_len, npc = proj.shape
    nq = _chips_covering(2 * wc, npc)
    hb = ts // HALO

    def body(p_ref, ph_ref, da3_ref, cw_ref, cb_ref, lg_ref, lb_ref, da1_ref, dlg_ref, dlb_ref, dcb_ref, dcw_ref,
             a0s_ref):
        i = pl.program_id(0)
        _stage_glu(p_ref, ph_ref, a0s_ref, i, wc, npc, ts)
        xh, rstd, a2 = _conv_ln(a0s_ref, cw_ref, cb_ref, lg_ref, lb_ref, ts)
        sig = _sigmoid(a2)
        da2 = da3_ref[...] * (sig * (1.0 + a2 * (1.0 - sig)))
        _acc_rows(dlg_ref, da2 * xh, i)
        _acc_rows(dlb_ref, da2, i)
        dxh = da2 * lg_ref[...]
        da1 = rstd * (dxh - jnp.mean(dxh, axis=-1, keepdims=True)
                      - xh * jnp.mean(dxh * xh, axis=-1, keepdims=True))
        da1_ref[...] = da1
        _acc_rows(dcb_ref, da1, i)

        @pl.when(i == 0)
        def _():
            dcw_ref[...] = jnp.zeros_like(dcw_ref)

        for k in range(CONV_K):
            dcw_ref[k:k + 1, :] += jnp.sum(da1 * a0s_ref[pl.ds(HALO - CONV_K + 1 + k, ts), :], axis=0,
                                           keepdims=True)

    return _pcall(
        body, name=name,
        out_shape=[_sds((s_len, wc), F32)] + [_sds((1, wc), F32)] * 3 + [_sds((HALO, wc), F32)],
        grid=(s_len // ts,),
        in_specs=[pl.BlockSpec((nq, ts, npc), lambda i: (0, i, 0)),
                  pl.BlockSpec((nq, HALO, npc), lambda i: (0, jnp.maximum(i * hb - 1, 0), 0)),
                  pl.BlockSpec((ts, wc), lambda i: (i, 0)),
                  pl.BlockSpec((HALO, wc), lambda i: (0, 0)), _vec(wc), _vec(wc), _vec(wc)],
        out_specs=[pl.BlockSpec((ts, wc), lambda i: (i, 0)), _vec(wc), _vec(wc), _vec(wc),
                   pl.BlockSpec((HALO, wc), lambda i: (0, 0))],
        scratch=[pltpu.VMEM((HALO + ts, wc), F32)], vmem_mb=48,
    )(proj, proj, da3, cw, cb, lg, lb)


def _mixer_in_bwd(name, proj, da1, dmixed, dgates, cw, wc, wp, ts):
    _, s_len, npc = proj.shape
    nq = _chips_covering(2 * wc, npc)
    gi = wp // len(POOL_WINDOWS)
    hb = ts // HALO
    n_tiles = s_len // ts
    last_hb = s_len // HALO - 1
    d2 = dgates.shape[1]

    def body(p_ref, d1_ref, d1n_ref, dm_ref, dmn_ref, dgt_ref, cw_ref, o_ref, d1s_ref, es_ref):
        i = pl.program_id(0)
        more = i < n_tiles - 1
        d1s_ref[0:ts, :] = d1_ref[...]
        d1s_ref[ts:ts + HALO, :] = jnp.where(more, d1n_ref[...], 0.0)
        da0 = cw_ref[0:1, :] * d1s_ref[pl.ds(CONV_K - 1, ts), :]
        for k in range(1, CONV_K):
            da0 = da0 + cw_ref[k:k + 1, :] * d1s_ref[pl.ds(CONV_K - 1 - k, ts), :]
        glu_a = _cols(p_ref, 0, wc, npc)
        sig = _sigmoid(_cols(p_ref, wc, 2 * wc, npc))
        _store_cols(o_ref, 0, (da0 * sig).astype(BF16), npc)
        _store_cols(o_ref, wc, (da0 * glu_a * sig * (1.0 - sig)).astype(BF16), npc)

        t_abs = i * ts + lax.broadcasted_iota(jnp.int32, (ts + HALO, 1), 0)
        dm = dm_ref[...]
        dm_ext = jnp.concatenate([dm, jnp.where(more, dmn_ref[...], 0.0)], axis=0)
        for g, win in enumerate(POOL_WINDOWS):
            cs = slice(g * gi, (g + 1) * gi)
            cnt = jnp.minimum(t_abs + 1, win).astype(F32)
            es_ref[:, cs] = dm_ext[:, cs] / cnt
        parts = []
        for g, win in enumerate(POOL_WINDOWS):
            cs = slice(g * gi, (g + 1) * gi)
            acc = es_ref[pl.ds(0, ts), cs]
            for dlt in range(1, win):
                acc = acc + es_ref[pl.ds(dlt, ts), cs]
            parts.append(acc - dm[:, cs])
        _store_cols(o_ref, 2 * wc, jnp.concatenate(parts, axis=-1).astype(BF16), npc)
        _store_cols(o_ref, 2 * wc + wp, dgt_ref[...], npc)

    nxt = lambda i: (jnp.minimum((i + 1) * hb, last_hb), 0)
    return _pcall(
        body, name=name, out_shape=_sds((N_CHIPS, s_len, npc), BF16), grid=(n_tiles,),
        in_specs=[pl.BlockSpec((nq, ts, npc), lambda i: (0, i, 0)),
                  pl.BlockSpec((ts, wc), lambda i: (i, 0)), pl.BlockSpec((HALO, wc), nxt),
                  pl.BlockSpec((ts, wp), lambda i: (i, 0)), pl.BlockSpec((HALO, wp), nxt),
                  pl.BlockSpec((ts, d2), lambda i: (i, 0)),
                  pl.BlockSpec((HALO, wc), lambda i: (0, 0))],
        out_specs=pl.BlockSpec((N_CHIPS, ts, npc), lambda i: (0, i, 0)),
        scratch=[pltpu.VMEM((ts + HALO, wc), F32), pltpu.VMEM((ts + HALO, wp), F32)], vmem_mb=48,
    )(proj, da1, da1, dmixed, dmixed, dgates, cw)


def _ada_fwd(name, c_all, w, b):
    d, cols = w.shape
    tn = 512 if cols % 512 == 0 else cols

    def body(c_ref, w_ref, b_ref, o_ref):
        cv = c_ref[...]
        sc = (cv * _sigmoid(cv)).astype(BF16)
        o_ref[...] = jnp.dot(sc, w_ref[...].astype(BF16), preferred_element_type=F32) + b_ref[...]

    return _pcall(body, name=name, out_shape=_sds((N_DEV, cols), F32), grid=(cols // tn,),
                  in_specs=[pl.BlockSpec((N_DEV, d), lambda j: (0, 0)), pl.BlockSpec((d, tn), lambda j: (0, j)),
                            pl.BlockSpec((1, tn), lambda j: (0, j))],
                  out_specs=pl.BlockSpec((N_DEV, tn), lambda j: (0, j)), vmem_mb=32)(c_all, w, b)


def _adam_math(w, g, m, v):
    m_new = ADAM_B1 * m + (1.0 - ADAM_B1) * g
    v_new = ADAM_B2 * v + (1.0 - ADAM_B2) * (g * g)
    m_hat = m_new / (1.0 - ADAM_B1 ** ADAM_STEP)
    v_hat = v_new / (1.0 - ADAM_B2 ** ADAM_STEP)
    delta = -ADAM_LR * (m_hat / (jnp.sqrt(v_hat) + ADAM_EPS) + ADAM_WD * w)
    return delta, m_new, v_new


def _adamw(name, w, g, m, v):
    rows, cols = w.shape
    tr = _row_tile(rows, cols, 262144)

    def body(w_ref, g_ref, m_ref, v_ref, d_ref, mo_ref, vo_ref):
        d_ref[...], mo_ref[...], vo_ref[...] = _adam_math(w_ref[...], g_ref[...], m_ref[...], v_ref[...])

    spec = pl.BlockSpec((tr, cols), lambda i: (i, 0))
    return _pcall(body, name=name, out_shape=[_sds(w.shape, F32)] * 3, grid=(rows // tr,), in_specs=[spec] * 4,
                  out_specs=[spec] * 3, vmem_mb=40)(w, g, m, v)


def _ada_grad_adamw(name, c_t, d_ada, w, m, v):
    rows, cols = w.shape
    tr = _tile(rows, 256)
    tc = _tile(cols, 1536) if cols % 1536 == 0 else cols

    def body(c_ref, da_ref, w_ref, m_ref, v_ref, g_ref, d_ref, mo_ref, vo_ref):
        cv = c_ref[...]
        sc = cv * _sigmoid(cv)
        g = sc[:, 0:1] * da_ref[0:1, :]
        for b in range(1, N_DEV):
            g = g + sc[:, b:b + 1] * da_ref[b:b + 1, :]
        g_ref[...] = g
        d_ref[...], mo_ref[...], vo_ref[...] = _adam_math(w_ref[...], g, m_ref[...], v_ref[...])

    spec = pl.BlockSpec((tr, tc), lambda i, j: (i, j))
    return _pcall(body, name=name, out_shape=[_sds(w.shape, F32)] * 4, grid=(rows // tr, cols // tc),
                  in_specs=[pl.BlockSpec((tr, N_DEV), lambda i, j: (i, 0)),
                            pl.BlockSpec((N_DEV, tc), lambda i, j: (0, j)), spec, spec, spec],
                  out_specs=[spec] * 4, vmem_mb=40)(c_t, d_ada, w, m, v)


def _sum_devices(name, gathered, m_per):
    n = gathered.shape[1]

    def body(g_ref, o_ref):
        acc = g_ref[0:m_per, :]
        for dev in range(1, N_DEV):
            acc = acc + g_ref[dev * m_per:(dev + 1) * m_per, :]
        o_ref[...] = acc

    return _pcall(body, name=name, out_shape=_sds((m_per, n), F32),
                  in_specs=[pl.BlockSpec(memory_space=pltpu.VMEM)],
                  out_specs=pl.BlockSpec(memory_space=pltpu.VMEM))(gathered)


def _ffn_fwd(tag, n, w_in_g, w_out2d, dims):
    s_len, d, f_dim = dims["S"], dims["D"], dims["F"]
    p = f_dim // 2
    tm, tn = _tile(s_len, 1024), math.gcd(p, 256)
    nbp = p // tn

    def ep(accs, ex, outs):
        hh, uu = accs
        outs[0][0] = hh.astype(BF16)
        outs[0][1] = uu.astype(BF16)
        outs[1][...] = (hh * _sigmoid(hh) * uu).astype(BF16)

    hu, act = _matmul(
        f"{tag}_swiglu", n, [w_in_g, w_in_g], mode="nn", grid=(s_len // tm, f_dim // tn, 1),
        a_spec=pl.BlockSpec((tm, d), lambda i, j, k: (i, 0)),
        b_specs=[pl.BlockSpec((None, d, tn), lambda i, j, k: (j // nbp, 0, j % nbp)),
                 pl.BlockSpec((None, d, tn), lambda i, j, k: (2 + j // nbp, 0, j % nbp))],
        out_shape=[_sds((2, s_len, f_dim), BF16), _sds((s_len, f_dim), BF16)],
        out_specs=[pl.BlockSpec((2, tm, tn), lambda i, j, k: (0, i, j)),
                   pl.BlockSpec((tm, tn), lambda i, j, k: (i, j))],
        acc_shape=(tm, tn), epilogue=ep)
    tf = f_dim // 4
    tn2 = _tile(d, 1024)
    f = _matmul(
        f"{tag}_down", act, [w_out2d], mode="nn", grid=(s_len // tm, d // tn2, 4),
        a_spec=pl.BlockSpec((tm, tf), lambda i, j, k: (i, k)),
        b_specs=[pl.BlockSpec((tf, tn2), lambda i, j, k: (k, j))],
        out_shape=_sds((s_len, d), F32), out_specs=pl.BlockSpec((tm, tn2), lambda i, j, k: (i, j)),
        acc_shape=(tm, tn2), epilogue=_ep_store(F32))
    return hu, act, f


def _ffn_bwd(tag, n, hu, act, df, w_in_g, w_out2d, dims):
    s_len, d, f_dim = dims["S"], dims["D"], dims["F"]
    tf = f_dim // 4
    tk = _tile(s_len, 512)
    tn = _tile(d, 1024)
    g_out = _matmul(
        f"{tag}_dw_out", act, [df], mode="tn", grid=(4, d // tn, s_len // tk),
        a_spec=pl.BlockSpec((tk, tf), lambda i, j, k: (k, i)),
        b_specs=[pl.BlockSpec((tk, tn), lambda i, j, k: (k, j))],
        out_shape=_sds((2, 4, tf // 2, d), F32),
        out_specs=pl.BlockSpec((2, None, tf // 2, tn), lambda i, j, k: (0, i, 0, j)),
        acc_shape=(tf, tn), epilogue=_ep_halves(tf // 2))

    def ep_dhu(accs, ex, outs):
        da = accs[0]
        hh, uu = ex[0][0].astype(F32), ex[0][1].astype(F32)
        sig = _sigmoid(hh)
        outs[0][0] = (da * uu * (sig * (1.0 + hh * (1.0 - sig)))).astype(BF16)
        outs[0][1] = (da * (hh * sig)).astype(BF16)

    tm = _tile(s_len, 512)
    hu_spec = pl.BlockSpec((2, tm, tf), lambda i, j, k: (0, i, j))
    dhu = _matmul(
        f"{tag}_dhu", df, [w_out2d], mode="nt", grid=(s_len // tm, 4, 1),
        a_spec=pl.BlockSpec((tm, d), lambda i, j, k: (i, 0)),
        b_specs=[pl.BlockSpec((tf, d), lambda i, j, k: (j, 0))],
        extras=[hu], extra_specs=[hu_spec],
        out_shape=_sds((2, s_len, f_dim), BF16), out_specs=hu_spec, acc_shape=(tm, tf), epilogue=ep_dhu)

    hd = d // 2
    g_in = _matmul(
        f"{tag}_dw_in", n, [dhu], mode="tn", grid=(2, 8, s_len // tk),
        a_spec=pl.BlockSpec((tk, hd), lambda i, j, k: (k, i)),
        b_specs=[pl.BlockSpec((None, tk, tf), lambda i, j, k: (j // 4, k, j % 4))],
        out_shape=_sds((2, 4, hd, f_dim // 2), F32),
        out_specs=pl.BlockSpec((None, None, hd, tf), lambda i, j, k: (i, j // 2, 0, j % 2)),
        acc_shape=(hd, tf), epilogue=_ep_store(F32))

    tm2 = _tile(s_len, 1024)
    dn = _matmul(
        f"{tag}_dn", dhu, [w_in_g], mode="nt", grid=(s_len // tm2, d // tn, 8),
        a_spec=pl.BlockSpec((None, tm2, tf), lambda i, j, k: (k // 4, i, k % 4)),
        b_specs=[pl.BlockSpec((None, tn, tf), lambda i, j, k: (k // 2, j, k % 2))],
        out_shape=_sds((s_len, d), F32), out_specs=pl.BlockSpec((tm2, tn), lambda i, j, k: (i, j)),
        acc_shape=(tm2, tn), epilogue=_ep_store(F32))
    return g_out, g_in, dn


def kernel(x, c, w_ada, b_ada, g_ffn1, w1_in, w1_out, g_mix, w_in, conv_w, conv_b, ln_a_g, ln_a_b, w_a_out, b_a_out, w_b_group, b_b_group, ls_b, w_out, g_ffn2, w2_in, w2_out, g_final, loss_target, m_w_ada, m_b_ada, m_g_ffn1, m_w1_in, m_w1_out, m_g_mix, m_w_in, m_conv_w, m_conv_b, m_ln_a_g, m_ln_a_b, m_w_a_out, m_b_a_out, m_w_b_group, m_b_b_group, m_ls_b, m_w_out, m_g_ffn2, m_w2_in, m_w2_out, m_g_final, v_w_ada, v_b_ada, v_g_ffn1, v_w1_in, v_w1_out, v_g_mix, v_w_in, v_conv_w, v_conv_b, v_ln_a_g, v_ln_a_b, v_w_a_out, v_b_a_out, v_w_b_group, v_b_b_group, v_ls_b, v_w_out, v_g_ffn2, v_w2_in, v_w2_out, v_g_final):
    weights = dict(w_ada=w_ada, b_ada=b_ada, g_ffn1=g_ffn1, w1_in=w1_in, w1_out=w1_out, g_mix=g_mix, w_in=w_in,
                   conv_w=conv_w, conv_b=conv_b, ln_a_g=ln_a_g, ln_a_b=ln_a_b, w_a_out=w_a_out, b_a_out=b_a_out,
                   w_b_group=w_b_group, b_b_group=b_b_group, ls_b=ls_b, w_out=w_out, g_ffn2=g_ffn2, w2_in=w2_in,
                   w2_out=w2_out, g_final=g_final)
    mom1 = dict(w_ada=m_w_ada, b_ada=m_b_ada, g_ffn1=m_g_ffn1, w1_in=m_w1_in, w1_out=m_w1_out, g_mix=m_g_mix,
                w_in=m_w_in, conv_w=m_conv_w, conv_b=m_conv_b, ln_a_g=m_ln_a_g, ln_a_b=m_ln_a_b, w_a_out=m_w_a_out,
                b_a_out=m_b_a_out, w_b_group=m_w_b_group, b_b_group=m_b_b_group, ls_b=m_ls_b, w_out=m_w_out,
                g_ffn2=m_g_ffn2, w2_in=m_w2_in, w2_out=m_w2_out, g_final=m_g_final)
    mom2 = dict(w_ada=v_w_ada, b_ada=v_b_ada, g_ffn1=v_g_ffn1, w1_in=v_w1_in, w1_out=v_w1_out, g_mix=v_g_mix,
                w_in=v_w_in, conv_w=v_conv_w, conv_b=v_conv_b, ln_a_g=v_ln_a_g, ln_a_b=v_ln_a_b, w_a_out=v_w_a_out,
                b_a_out=v_b_a_out, w_b_group=v_w_b_group, b_b_group=v_b_b_group, ls_b=v_ls_b, w_out=v_w_out,
                g_ffn2=v_g_ffn2, w2_in=v_w2_in, w2_out=v_w2_out, g_final=v_g_final)
    order = list(weights)

    s_len, d = x.shape[1], x.shape[2]
    f_dim = w1_out.shape[0] * N_CHIPS
    wc = conv_w.shape[1] * N_CHIPS
    wp = w_b_group.shape[0] * w_b_group.shape[1]
    n_groups, gi, goq = w_b_group.shape
    npc = w_in.shape[1]
    ada_c = w_ada.shape[1]
    dims = dict(S=s_len, D=d, F=f_dim)
    ts = _tile(s_len, 256)

    xi, yi, ci = lax.axis_index("x"), lax.axis_index("y"), lax.axis_index("c")
    q = 2 * xi + yi
    dev = 2 * q + ci
    c_idx = jnp.reshape(ci, (1,)).astype(jnp.int32)
    qc_idx = jnp.stack([q, ci]).astype(jnp.int32)

    cwq = conv_w.shape[1]
    pack0 = jnp.concatenate([c.reshape(-1), conv_w.reshape(-1), b_b_group.reshape(-1)])
    n0 = -(-pack0.shape[0] // (8 * LANES)) * LANES
    pack0 = jnp.pad(pack0, (0, 8 * n0 - pack0.shape[0])).reshape(8, n0)
    g0 = _allgather_small("gather_small_in", pack0).reshape(N_DEV, 8 * n0)
    c_all = g0[:, :d]
    south = g0[0::2]
    cw_full = jnp.concatenate([south[k, d:d + CONV_K * cwq].reshape(CONV_K, cwq) for k in range(N_CHIPS)], axis=1)
    cw_pad = jnp.pad(cw_full, ((0, HALO - CONV_K), (0, 0)))
    o_bb = d + CONV_K * cwq
    bb_full = jnp.concatenate([south[k, o_bb:o_bb + n_groups * goq].reshape(n_groups, goq) for k in range(N_CHIPS)],
                              axis=1).reshape(1, d)

    b_ada_mine = lax.dynamic_slice(b_ada, (q * ada_c,), (ada_c,)).reshape(1, ada_c)
    ada_piece = _ada_fwd("ada_fwd", c_all, w_ada, b_ada_mine)
    g1 = _allgather_small("gather_ada", ada_piece).reshape(N_DEV, N_DEV, ada_c)
    ada_rows = lax.dynamic_index_in_dim(g1[0::2], dev, axis=1, keepdims=False)
    ada = ada_rows.reshape(3, 3, 1, d)
    (sh1, sc1, gt1), (sh2, sc2, gt2), (sh3, sc3, gt3) = [[ada[i, j] for j in range(3)] for i in range(3)]

    row = lambda vct: vct.reshape(1, -1)
    g1v, gmv, g2v, gfv = row(g_ffn1), row(g_mix), row(g_ffn2), row(g_final)

    big = ["w1_in", "w1_out", "w_in", "w_a_out", "w_b_group", "w_out", "w2_in", "w2_out"]
    as2d = lambda a: a.reshape(-1, a.shape[-1])
    shards = [_cast_bf16(f"cast_{nm}", as2d(weights[nm])) for nm in big]
    gathered = dict(zip(big, _allgather_weights("gather_weights", shards)))
    w1_in_g, w2_in_g, w_in_g = gathered["w1_in"], gathered["w2_in"], gathered["w_in"]
    w1_out_2d = gathered["w1_out"].reshape(f_dim, d)
    w2_out_2d = gathered["w2_out"].reshape(f_dim, d)
    w_out_2d = gathered["w_out"].reshape(d, d)
    w_a_g = gathered["w_a_out"]
    w_b_g = gathered["w_b_group"]

    x2 = x[0]
    tgt = loss_target[0]

    n1 = _norm_mod("ffn1_norm", x2, g1v, sc1, sh1, ts)
    hu1, act1, f1 = _ffn_fwd("ffn1", n1, w1_in_g, w1_out_2d, dims)
    h1, n2 = _residual_norm_mod("mix_norm", x2, f1, gt1, 0.5, gmv, sc2, sh2, ts)

    tm = _tile(s_len, 1024)
    tnp = npc // 2
    proj = _matmul(
        "mix_proj", n2, [w_in_g], mode="nn", grid=(s_len // tm, 8, 1),
        a_spec=pl.BlockSpec((tm, d), lambda i, j, k: (i, 0)),
        b_specs=[pl.BlockSpec((None, d, tnp), lambda i, j, k: (j // 2, 0, j % 2))],
        out_shape=_sds((N_CHIPS, s_len, npc), F32),
        out_specs=pl.BlockSpec((None, tm, tnp), lambda i, j, k: (j // 2, i, j % 2)),
        acc_shape=(tm, tnp), epilogue=_ep_store(F32))
    cbv, lgv, lbv = row(conv_b), row(ln_a_g), row(ln_a_b)
    a3, mixed = _mixer_mid("mix_mid", proj, cw_pad, cbv, lgv, lbv, wc, wp, ts)
    dq = d // N_CHIPS
    ya = _matmul(
        "mix_ya", a3, [w_a_g], mode="nn", grid=(s_len // tm, N_CHIPS, 1),
        a_spec=pl.BlockSpec((tm, wc), lambda i, j, k: (i, 0)),
        b_specs=[pl.BlockSpec((None, wc, dq), lambda i, j, k: (j, 0, 0))],
        out_shape=_sds((s_len, d), F32), out_specs=pl.BlockSpec((tm, dq), lambda i, j, k: (i, j)),
        acc_shape=(tm, dq), epilogue=_ep_store(F32))
    yb = _matmul(
        "mix_yb", mixed, [w_b_g], mode="nn", grid=(s_len // tm, n_groups * N_CHIPS, 1),
        a_spec=pl.BlockSpec((tm, gi), lambda i, j, k: (i, j // N_CHIPS)),
        b_specs=[pl.BlockSpec((None, gi, goq), lambda i, j, k: (j % N_CHIPS, j // N_CHIPS, 0))],
        out_shape=_sds((s_len, d), F32), out_specs=pl.BlockSpec((tm, goq), lambda i, j, k: (i, j)),
        acc_shape=(tm, goq), epilogue=_ep_store(F32))
    bav, lsv = row(b_a_out), row(ls_b)
    z = _gates_fwd("mix_gates", proj, ya, yb, bav, bb_full, lsv, wc, wp, ts)
    tn = _tile(d, 1024)
    mix = _matmul(
        "mix_out", z, [w_out_2d], mode="nn", grid=(s_len // tm, d // tn, 1),
        a_spec=pl.BlockSpec((tm, d), lambda i, j, k: (i, 0)),
        b_specs=[pl.BlockSpec((d, tn), lambda i, j, k: (0, j))],
        out_shape=_sds((s_len, d), F32), out_specs=pl.BlockSpec((tm, tn), lambda i, j, k: (i, j)),
        acc_shape=(tm, tn), epilogue=_ep_store(F32))
    h2, n3 = _residual_norm_mod("ffn2_norm", h1, mix, gt2, 1.0, g2v, sc3, sh3, ts)
    hu2, act2, f3 = _ffn_fwd("ffn2", n3, w2_in_g, w2_out_2d, dims)

    dh3, df3, d_gf, d_gt3, loss_cols = _final_loss("final_loss", h2, f3, tgt, gt3, 0.5, gfv, ts)
    gw2_out, gw2_in, dn3 = _ffn_bwd("ffn2", n3, hu2, act2, df3, w2_in_g, w2_out_2d, dims)
    dh2, dmix, d_sh3, d_sc3, d_g2, d_gt2 = _norm_mod_bwd("ffn2_norm_bwd", h2, dn3, dh3, g2v, sc3, ts,
                                                         prev=(mix, gt2, 1.0))

    tk = _tile(s_len, 512)
    hq = d // (2 * N_CHIPS)
    gw_out = _matmul(
        "mix_dw_out", z, [dmix], mode="tn", grid=(N_CHIPS, d // tn, s_len // tk),
        a_spec=pl.BlockSpec((tk, 2 * hq), lambda i, j, k: (k, i)),
        b_specs=[pl.BlockSpec((tk, tn), lambda i, j, k: (k, j))],
        out_shape=_sds((2, N_CHIPS, hq, d), F32),
        out_specs=pl.BlockSpec((2, None, hq, tn), lambda i, j, k: (0, i, 0, j)),
        acc_shape=(2 * hq, tn), epilogue=_ep_halves(hq))
    dz = _matmul(
        "mix_dz", dmix, [w_out_2d], mode="nt", grid=(s_len // tm, d // tn, 1),
        a_spec=pl.BlockSpec((tm, d), lambda i, j, k: (i, 0)),
        b_specs=[pl.BlockSpec((tn, d), lambda i, j, k: (j, 0))],
        out_shape=_sds((s_len, d), F32), out_specs=pl.BlockSpec((tm, tn), lambda i, j, k: (i, j)),
        acc_shape=(tm, tn), epilogue=_ep_store(F32))
    dya, dyb, dgates, d_ba, d_ls, d_bb = _gates_bwd("mix_gates_bwd", proj, dz, ya, yb, bav, bb_full, lsv, wc, wp, ts)
    gw_a = _matmul(
        "mix_dw_a", a3, [dya], mode="tn", grid=(1, N_CHIPS, s_len // tk),
        a_spec=pl.BlockSpec((tk, wc), lambda i, j, k: (k, 0)),
        b_specs=[pl.BlockSpec((tk, dq), lambda i, j, k: (k, j))],
        out_shape=_sds((2, N_CHIPS, wc // 2, dq), F32),
        out_specs=pl.BlockSpec((2, None, wc // 2, dq), lambda i, j, k: (0, j, 0, 0)),
        acc_shape=(wc, dq), epilogue=_ep_halves(wc // 2))
    da3 = _matmul(
        "mix_da3", dya, [w_a_g], mode="nt", grid=(s_len // tm, 1, N_CHIPS),
        a_spec=pl.BlockSpec((tm, dq), lambda i, j, k: (i, k)),
        b_specs=[pl.BlockSpec((None, wc, dq), lambda i, j, k: (k, 0, 0))],
        out_shape=_sds((s_len, wc), F32), out_specs=pl.BlockSpec((tm, wc), lambda i, j, k: (i, 0)),
        acc_shape=(tm, wc), epilogue=_ep_store(F32))
    gpr = n_groups // 2
    gw_b = _matmul(
        "mix_dw_b", mixed, [dyb], mode="tn", grid=(1, n_groups * N_CHIPS, s_len // tk),
        a_spec=pl.BlockSpec((tk, gi), lambda i, j, k: (k, j // N_CHIPS)),
        b_specs=[pl.BlockSpec((tk, goq), lambda i, j, k: (k, j))],
        out_shape=_sds((2, N_CHIPS, gpr * gi, goq), F32),
        out_specs=pl.BlockSpec((None, None, gi, goq),
                               lambda i, j, k: ((j // N_CHIPS) // gpr, j % N_CHIPS, (j // N_CHIPS) % gpr, 0)),
        acc_shape=(gi, goq), epilogue=_ep_store(F32))
    dmixed = _matmul(
        "mix_dmixed", dyb, [w_b_g], mode="nt", grid=(s_len // tm, n_groups, N_CHIPS),
        a_spec=pl.BlockSpec((tm, goq), lambda i, j, k: (i, j * N_CHIPS + k)),
        b_specs=[pl.BlockSpec((None, gi, goq), lambda i, j, k: (k, j, 0))],
        out_shape=_sds((s_len, wp), F32), out_specs=pl.BlockSpec((tm, gi), lambda i, j, k: (i, j)),
        acc_shape=(tm, gi), epilogue=_ep_store(F32))
    da1, d_lg, d_lb, d_cb, d_cw = _conv_branch_bwd("mix_conv_bwd", proj, da3, cw_pad, cbv, lgv, lbv, wc, wp, ts)
    dproj = _mixer_in_bwd("mix_in_bwd", proj, da1, dmixed, dgates, cw_pad, wc, wp, ts)
    hd = d // 2
    gw_in = _matmul(
        "mix_dw_in", n2, [dproj], mode="tn", grid=(2, 8, s_len // tk),
        a_spec=pl.BlockSpec((tk, hd), lambda i, j, k: (k, i)),
        b_specs=[pl.BlockSpec((None, tk, tnp), lambda i, j, k: (j // 2, k, j % 2))],
        out_shape=_sds((2, N_CHIPS, hd, npc), F32),
        out_specs=pl.BlockSpec((None, None, hd, tnp), lambda i, j, k: (i, j // 2, 0, j % 2)),
        acc_shape=(hd, tnp), epilogue=_ep_store(F32))
    dn2 = _matmul(
        "mix_dn", dproj, [w_in_g], mode="nt", grid=(s_len // tm, d // tn, N_CHIPS),
        a_spec=pl.BlockSpec((None, tm, npc), lambda i, j, k: (k, i, 0)),
        b_specs=[pl.BlockSpec((None, tn, npc), lambda i, j, k: (k, j, 0))],
        out_shape=_sds((s_len, d), F32), out_specs=pl.BlockSpec((tm, tn), lambda i, j, k: (i, j)),
        acc_shape=(tm, tn), epilogue=_ep_store(F32))
    dh1, df1, d_sh2, d_sc2, d_gm, d_gt1 = _norm_mod_bwd("mix_norm_bwd", h1, dn2, dh2, gmv, sc2, ts,
                                                        prev=(f1, gt1, 0.5))
    gw1_out, gw1_in, dn1 = _ffn_bwd("ffn1", n1, hu1, act1, df1, w1_in_g, w1_out_2d, dims)
    grad_x, d_sh1, d_sc1, d_g1 = _norm_mod_bwd("ffn1_norm_bwd", x2, dn1, dh1, g1v, sc1, ts)

    full_grads = dict(w1_in=gw1_in, w1_out=gw1_out, w_in=gw_in, w_a_out=gw_a, w_b_group=gw_b, w_out=gw_out,
                      w2_in=gw2_in, w2_out=gw2_out)
    reduced = dict(zip(big, _reduce_scatter("g", [full_grads[nm] for nm in big], c_idx, qc_idx)))

    d_ada = jnp.concatenate([d_sh1, d_sc1, d_gt1, d_sh2, d_sc2, d_gt2, d_sh3, d_sc3, d_gt3], axis=1)
    small = [d_ada, d_g1, d_gm, d_cw[:CONV_K].reshape(1, -1), d_cb, d_lg, d_lb, d_ba, d_bb, d_ls, d_g2, d_gf,
             loss_cols]
    sizes = [a.shape[1] for a in small]
    pack1 = jnp.concatenate(small, axis=1).reshape(-1)
    n1p = -(-pack1.shape[0] // (8 * LANES)) * LANES
    pack1 = jnp.pad(pack1, (0, 8 * n1p - pack1.shape[0])).reshape(8, n1p)
    g2 = _allgather_small("gather_small_grads", pack1)
    total = _sum_devices("sum_small_grads", g2, 8).reshape(-1)
    offs = [0]
    for sz in sizes:
        offs.append(offs[-1] + sz)
    tot = [total[offs[k]:offs[k + 1]] for k in range(len(sizes))]
    d_ada_all = g2.reshape(N_DEV, 8 * n1p)[:, :sizes[0]]
    loss = jnp.sum(tot[12])

    grads = dict(reduced)
    grads["w_b_group"] = reduced["w_b_group"].reshape(w_b_group.shape)
    grads["b_ada"] = tot[0]
    grads["g_ffn1"], grads["g_mix"] = tot[1], tot[2]
    grads["conv_w"] = lax.dynamic_slice(tot[3].reshape(CONV_K, wc), (0, q * cwq), (CONV_K, cwq))
    grads["conv_b"], grads["ln_a_g"], grads["ln_a_b"], grads["b_a_out"] = tot[4], tot[5], tot[6], tot[7]
    grads["b_b_group"] = lax.dynamic_slice(tot[8].reshape(n_groups, N_CHIPS * goq), (0, q * goq), (n_groups, goq))
    grads["ls_b"], grads["g_ffn2"], grads["g_final"] = tot[9], tot[10], tot[11]

    delta, new_m, new_v = {}, {}, {}
    d_ada_mine = lax.dynamic_slice(d_ada_all, (0, q * ada_c), (N_DEV, ada_c))
    grads["w_ada"], delta["w_ada"], new_m["w_ada"], new_v["w_ada"] = _ada_grad_adamw(
        "adamw_w_ada", c_all.T, d_ada_mine, w_ada, m_w_ada, v_w_ada)
    for nm in big:
        shp = weights[nm].shape
        dl, mo, vo = _adamw(f"adamw_{nm}", as2d(weights[nm]), as2d(grads[nm]), as2d(mom1[nm]), as2d(mom2[nm]))
        delta[nm], new_m[nm], new_v[nm] = dl.reshape(shp), mo.reshape(shp), vo.reshape(shp)
    smalls = [nm for nm in order if nm not in big and nm != "w_ada"]
    flat = lambda src: jnp.concatenate([src[nm].reshape(-1) for nm in smalls])
    n_small = sum(weights[nm].size for nm in smalls)
    rows_s = -(-n_small // (8 * LANES)) * 8
    packed = [jnp.pad(flat(src), (0, rows_s * LANES - n_small)).reshape(rows_s, LANES)
              for src in (weights, grads, mom1, mom2)]
    dl_s, mo_s, vo_s = _adamw("adamw_small", *packed)
    off = 0
    for nm in smalls:
        sz, shp = weights[nm].size, weights[nm].shape
        delta[nm] = dl_s.reshape(-1)[off:off + sz].reshape(shp)
        new_m[nm] = mo_s.reshape(-1)[off:off + sz].reshape(shp)
        new_v[nm] = vo_s.reshape(-1)[off:off + sz].reshape(shp)
        grads[nm] = grads[nm].reshape(shp)
        off += sz

    return (loss, grad_x[None], *[grads[nm] for nm in order], *[delta[nm] for nm in order],
            *[new_m[nm] for nm in order], *[new_v[nm] for nm in order])
```

```python
import math

import jax
import jax.numpy as jnp
from jax import lax
from jax.experimental import pallas as pl
from jax.experimental.pallas import tpu as pltpu

F32 = jnp.float32
BF16 = jnp.bfloat16
MESH = pl.DeviceIdType.MESH
ANY = pl.BlockSpec(memory_space=pl.ANY)
HBM = pl.BlockSpec(memory_space=pltpu.HBM)
SEM = pl.BlockSpec(memory_space=pltpu.SEMAPHORE)
EFFECT = pltpu.SideEffectType.DATAFLOW_SIDE_EFFECTING

EPS = 1e-6
CONV_K = 31
HALO = 32
POOL_WINDOWS = (2, 4, 8, 16)
N_CHIPS = 4
N_DEV = 8
LANES = 128

ADAM_LR = 0.001
ADAM_B1 = 0.9
ADAM_B2 = 0.999
ADAM_EPS = 1e-08
ADAM_WD = 0.01
ADAM_STEP = 10

DN = {
    "nn": (((1,), (0,)), ((), ())),
    "nt": (((1,), (1,)), ((), ())),
    "tn": (((0,), (0,)), ((), ())),
}


_PREVIOUS = []


def _ordered(call, args, n_lead, body, token=None):
    dep = [p for p in _PREVIOUS if all(p is not a for a in args)]

    def wrapped(*refs):
        return body(*refs[:n_lead], *refs[n_lead + len(dep):])

    outs = call(wrapped, [ANY] * len(dep))(*args, *dep)
    seq = outs if isinstance(outs, (list, tuple)) else [outs]
    _PREVIOUS[:] = [seq[token] if token is not None else
                    next(o for o in seq if jnp.issubdtype(o.dtype, jnp.floating))]
    return outs


def _pcall(body, *, name, out_shape, grid=None, in_specs=None, out_specs=None, scratch=(), aliases=None,
           prefetch=0, vmem_mb=None):
    params = {}
    if grid is not None:
        params["dimension_semantics"] = ("arbitrary",) * len(grid)
    if vmem_mb is not None:
        params["vmem_limit_bytes"] = vmem_mb << 20
    kw = dict(name=name, out_shape=out_shape, compiler_params=pltpu.CompilerParams(**params))
    if aliases:
        kw["input_output_aliases"] = aliases

    def call(wrapped, dep_specs):
        specs = list(in_specs) + dep_specs
        if prefetch:
            return pl.pallas_call(wrapped, grid_spec=pltpu.PrefetchScalarGridSpec(
                num_scalar_prefetch=prefetch, grid=grid, in_specs=specs, out_specs=out_specs,
                scratch_shapes=list(scratch)), **kw)
        if grid is not None:
            return pl.pallas_call(wrapped, grid=grid, in_specs=specs, out_specs=out_specs,
                                  scratch_shapes=list(scratch), **kw)
        return pl.pallas_call(wrapped, in_specs=specs, out_specs=out_specs, scratch_shapes=list(scratch), **kw)

    return lambda *args: _ordered(call, args, prefetch + len(in_specs), body)


def _tile(dim, pref):
    t = min(dim, pref)
    assert dim % t == 0, (dim, pref)
    return t


def _sds(shape, dtype):
    return jax.ShapeDtypeStruct(tuple(shape), dtype)


def _sigmoid(v):
    return 1.0 / (1.0 + jnp.exp(-v))


def _vec(w):
    return pl.BlockSpec((1, w), lambda *_: (0, 0))


def _acc_rows(ref, val, i):
    @pl.when(i == 0)
    def _():
        ref[...] = jnp.zeros_like(ref)

    ref[...] += jnp.sum(val, axis=0, keepdims=True)


def _matmul(name, a, bs, *, mode, grid, a_spec, b_specs, out_shape, out_specs, acc_shape, epilogue,
            extras=(), extra_specs=(), vmem_mb=48):
    nb, ne, nk = len(bs), len(extras), grid[2]
    dn = DN[mode]

    def body(*refs):
        a_ref, b_refs, ex = refs[0], refs[1:1 + nb], refs[1 + nb:1 + nb + ne]
        if nk == 1:
            outs = refs[1 + nb + ne:]
            accs = [lax.dot_general(a_ref[...], b[...], dn, preferred_element_type=F32) for b in b_refs]
            epilogue(accs, ex, outs)
            return
        outs, acc_refs = refs[1 + nb + ne:-nb], refs[-nb:]
        k = pl.program_id(2)

        @pl.when(k == 0)
        def _():
            for acc in acc_refs:
                acc[...] = jnp.zeros_like(acc)

        for acc, b in zip(acc_refs, b_refs):
            acc[...] += lax.dot_general(a_ref[...], b[...], dn, preferred_element_type=F32)

        @pl.when(k == nk - 1)
        def _():
            epilogue([acc[...] for acc in acc_refs], ex, outs)

    scratch = [pltpu.VMEM(acc_shape, F32) for _ in range(nb)] if nk > 1 else []
    return _pcall(body, name=name, out_shape=out_shape, grid=grid,
                  in_specs=[a_spec, *b_specs, *extra_specs], out_specs=out_specs, scratch=scratch,
                  vmem_mb=vmem_mb)(a, *bs, *extras)


def _ep_store(dtype):
    def ep(accs, ex, outs):
        outs[0][...] = accs[0].astype(dtype)
    return ep


def _ep_halves(h):
    def ep(accs, ex, outs):
        outs[0][0] = accs[0][:h]
        outs[0][1] = accs[0][h:]
    return ep


def _place():
    x, y, c = lax.axis_index("x"), lax.axis_index("y"), lax.axis_index("c")
    chips = [(1 - x, y), (x, 1 - y), (1 - x, 1 - y)]
    return x, y, c, chips


def _allgather_small(name, block):
    m_per, n = block.shape

    def body(x_ref, out_ref, send_sems, recv_sems, local_sem):
        x, y, c, chips = _place()
        me, sibling = (x, y, c), (x, y, 1 - c)

        def rows(px, py, pc):
            return out_ref.at[pl.ds((4 * px + 2 * py + pc) * m_per, m_per), :]

        def copy(k, blk, to, src=None):
            return pltpu.make_async_remote_copy(
                src_ref=rows(*blk) if src is None else src, dst_ref=rows(*blk),
                send_sem=send_sems.at[k], recv_sem=recv_sems.at[k], device_id=to, device_id_type=MESH)

        mine = pltpu.make_async_copy(x_ref, rows(*me), local_sem)
        mine.start()
        first = [copy(0, me, sibling, src=x_ref)]
        first += [copy(1 + j, me, (*chip, c), src=x_ref) for j, chip in enumerate(chips)]
        for cp in first:
            cp.start()
        passed = [copy(4 + j, (*chip, c), sibling) for j, chip in enumerate(chips)]
        for j, chip in enumerate(chips):
            copy(1 + j, (*chip, c), me).wait_recv()
            passed[j].start()
        copy(0, sibling, me).wait_recv()
        for j, chip in enumerate(chips):
            copy(4 + j, (*chip, 1 - c), me).wait_recv()
        for cp in first + passed:
            cp.wait_send()
        mine.wait()

    return _pcall(
        body, name=name, out_shape=_sds((N_DEV * m_per, n), block.dtype),
        in_specs=[pl.BlockSpec(memory_space=pltpu.VMEM)], out_specs=pl.BlockSpec(memory_space=pltpu.VMEM),
        scratch=[pltpu.SemaphoreType.DMA((7,)), pltpu.SemaphoreType.DMA((7,)), pltpu.SemaphoreType.DMA],
    )(block)


class _SplitCopies:
    def __init__(self, name, arrays, plan, n_copies):
        self.name, self.plan, self.n = name, plan, len(arrays)
        n = self.n

        def body(*refs):
            send, recv, token = refs[n], refs[n + 1], refs[-1]
            for k, (src, dst, _, peer) in enumerate(plan(refs[:n])):
                pltpu.make_async_remote_copy(src_ref=src, dst_ref=dst, send_sem=send.at[k], recv_sem=recv.at[k],
                                             device_id=peer, device_id_type=MESH).start()
            token[...] = jnp.zeros_like(token)

        def call(wrapped, dep_specs):
            return pl.pallas_call(
                wrapped, name=f"{name}_start",
                out_shape=(pltpu.SemaphoreType.DMA((n_copies,)), pltpu.SemaphoreType.DMA((n_copies,)),
                           *[pltpu.HBM(a.shape, a.dtype) for a in arrays], _sds((8, LANES), F32)),
                in_specs=[HBM] * n + dep_specs,
                out_specs=(SEM, SEM, *[HBM] * n, pl.BlockSpec(memory_space=pltpu.VMEM)),
                input_output_aliases={i: 2 + i for i in range(n)},
                compiler_params=pltpu.CompilerParams(has_side_effects=EFFECT))

        outs = _ordered(call, [pltpu.with_memory_space_constraint(a, pltpu.HBM) for a in arrays], n, body, token=-1)
        self.send, self.recv, self.arrays = outs[0], outs[1], list(outs[2:2 + n])

    def wait(self):
        n, plan = self.n, self.plan

        def body(*refs):
            send, recv, token = refs[n], refs[n + 1], refs[-1]
            for k, (src, _, landing, peer) in enumerate(plan(refs[:n])):
                cp = pltpu.make_async_remote_copy(src_ref=src, dst_ref=landing, send_sem=send.at[k],
                                                  recv_sem=recv.at[k], device_id=peer, device_id_type=MESH)
                cp.wait_send()
                cp.wait_recv()
            token[...] = jnp.zeros_like(token)

        def call(wrapped, dep_specs):
            return pl.pallas_call(
                wrapped, name=f"{self.name}_wait",
                out_shape=(*[pltpu.HBM(a.shape, a.dtype) for a in self.arrays], _sds((8, LANES), F32)),
                in_specs=[HBM] * n + [SEM, SEM] + dep_specs,
                out_specs=(*[HBM] * n, pl.BlockSpec(memory_space=pltpu.VMEM)),
                input_output_aliases={i: i for i in range(n)},
                compiler_params=pltpu.CompilerParams(has_side_effects=EFFECT))

        return list(_ordered(call, [*self.arrays, self.send, self.recv], n + 2, body, token=-1))[:n]


def _gather_ici(name, gathered):
    def plan(refs):
        x, y, c, chips = _place()
        q = 2 * x + y
        return [(g.at[q, c], g.at[q, c], g.at[2 * px + py, c], (px, py, c)) for g in refs for px, py in chips]

    return _SplitCopies(name, gathered, plan, 3 * len(gathered))


def _gather_d2d(name, gathered):
    def plan(refs):
        x, y, c, chips = _place()
        return [(g.at[2 * px + py, c], g.at[2 * px + py, c], g.at[2 * px + py, 1 - c], (x, y, 1 - c))
                for g in refs for px, py in chips]

    return _SplitCopies(name, gathered, plan, 3 * len(gathered))


def _scatter_sibling(name, grads):
    n = len(grads)

    def plan(refs):
        x, y, c, _ = _place()
        return [(refs[w].at[1 - c], refs[n + w], refs[n + w], (x, y, 1 - c)) for w in range(n)]

    landing = [lax.empty(g.shape[1:], g.dtype) for g in grads]
    return _SplitCopies(name, [*grads, *landing], plan, n)


def _scatter_chips(name, sums):
    n = len(sums)

    def plan(refs):
        x, y, c, chips = _place()
        return [(refs[w].at[2 * px + py], refs[n + w].at[j], refs[n + w].at[j], (px, py, c))
                for w in range(n) for j, (px, py) in enumerate(chips)]

    landing = [lax.empty((3, *s.shape[1:]), s.dtype) for s in sums]
    return _SplitCopies(name, [*sums, *landing], plan, 3 * n)


def _share_final(name, finals):
    def plan(refs):
        x, y, c, _ = _place()
        return [(f.at[c], f.at[c], f.at[1 - c], (x, y, 1 - c)) for f in refs]

    return _SplitCopies(name, finals, plan, len(finals))


def _row_tile(rows, cols, budget_elems=393216):
    best = 8
    for t in range(8, rows + 1, 8):
        if rows % t == 0 and t * cols <= budget_elems:
            best = t
    return best if rows % best == 0 else rows


def _sum_with_sibling(name, grad, recv, c_idx):
    _, _, h, cols = grad.shape
    tr = _row_tile(h, cols)

    def body(s_ref, g_ref, r_ref, p_ref, pb_ref):
        p = g_ref[...] + r_ref[...]
        p_ref[...] = p
        pb_ref[...] = p.astype(BF16)

    blk = pl.BlockSpec((None, tr, cols), lambda k, r, s: (k, r, 0))
    return _pcall(
        body, name=name, out_shape=[_sds((N_CHIPS, h, cols), F32), _sds((N_CHIPS, h, cols), BF16)],
        grid=(N_CHIPS, h // tr), prefetch=1,
        in_specs=[pl.BlockSpec((None, None, tr, cols), lambda k, r, s: (s[0], k, r, 0)), blk],
        out_specs=[blk, blk], vmem_mb=32,
    )(c_idx, grad, recv)


def _sum_chips(name, own, recv, qc_idx):
    _, h, cols = own.shape
    tr = _row_tile(h, cols)

    def body(s_ref, p_ref, t_ref, o_ref):
        o_ref[...] = ((p_ref[...] + t_ref[0].astype(F32)) + t_ref[1].astype(F32)) + t_ref[2].astype(F32)

    return _pcall(
        body, name=name, out_shape=_sds((2, h, cols), F32), grid=(h // tr,), prefetch=1,
        in_specs=[pl.BlockSpec((None, tr, cols), lambda r, s: (s[0], r, 0)),
                  pl.BlockSpec((3, tr, cols), lambda r, s: (0, r, 0))],
        out_specs=pl.BlockSpec((None, tr, cols), lambda r, s: (s[1], r, 0)), vmem_mb=32,
    )(qc_idx, own, recv)


class _ReduceScatter:
    def __init__(self, tag, names, grads, c_idx, qc_idx):
        self.tag, self.names, self.n, self.c_idx, self.qc_idx = tag, names, len(grads), c_idx, qc_idx
        self.copies = _scatter_sibling(f"{tag}_rs_sibling", grads)

    def step2(self):
        n = self.n
        arrs = self.copies.wait()
        sums = [_sum_with_sibling(f"{nm}_sum_sibling", arrs[w], arrs[n + w], self.c_idx)
                for w, nm in enumerate(self.names)]
        self.own = [s[0] for s in sums]
        self.copies = _scatter_chips(f"{self.tag}_rs_chips", [s[1] for s in sums])

    def step3(self):
        n = self.n
        arrs = self.copies.wait()
        finals = [_sum_chips(f"{nm}_sum_chips", self.own[w], arrs[n + w], self.qc_idx)
                  for w, nm in enumerate(self.names)]
        self.copies = _share_final(f"{self.tag}_rs_final", finals)

    def result(self):
        return {nm: f.reshape(2 * f.shape[1], f.shape[2]) for nm, f in zip(self.names, self.copies.wait())}


def _cast_into_gathered(name, w, q_idx):
    rows, cols = w.shape
    h = rows // 2
    tr = _row_tile(h, cols, 1 << 20)
    nr = h // tr

    def body(s_ref, w_ref, o_ref):
        o_ref[...] = w_ref[...].astype(BF16)

    return _pcall(body, name=name, out_shape=_sds((N_CHIPS, 2, h, cols), BF16), grid=(2, nr), prefetch=1,
                  in_specs=[pl.BlockSpec((tr, cols), lambda hf, r, s: (hf * nr + r, 0))],
                  out_specs=pl.BlockSpec((None, None, tr, cols), lambda hf, r, s: (s[0], hf, r, 0)),
                  vmem_mb=32)(q_idx, w)


def _rms(h):
    r = lax.rsqrt(jnp.mean(h * h, axis=-1, keepdims=True) + EPS)
    return r, h * r


def _norm_mod(name, h, g, sc, sh, ts):
    s_len, d = h.shape

    def body(h_ref, g_ref, sc_ref, sh_ref, n_ref):
        _, xhat = _rms(h_ref[...])
        n_ref[...] = ((xhat * g_ref[...]) * (1.0 + sc_ref[...]) + sh_ref[...]).astype(BF16)

    row = pl.BlockSpec((ts, d), lambda i: (i, 0))
    return _pcall(body, name=name, out_shape=_sds((s_len, d), BF16), grid=(s_len // ts,),
                  in_specs=[row, _vec(d), _vec(d), _vec(d)], out_specs=row, vmem_mb=32)(h, g, sc, sh)


def _residual_norm_mod(name, h, f, gate, cmul, g, sc, sh, ts):
    s_len, d = h.shape

    def body(h_ref, f_ref, gt_ref, g_ref, sc_ref, sh_ref, ho_ref, n_ref):
        hn = h_ref[...] + (cmul * gt_ref[...]) * f_ref[...]
        ho_ref[...] = hn
        _, xhat = _rms(hn)
        n_ref[...] = ((xhat * g_ref[...]) * (1.0 + sc_ref[...]) + sh_ref[...]).astype(BF16)

    row = pl.BlockSpec((ts, d), lambda i: (i, 0))
    return _pcall(body, name=name, out_shape=[_sds((s_len, d), F32), _sds((s_len, d), BF16)],
                  grid=(s_len // ts,), in_specs=[row, row, _vec(d), _vec(d), _vec(d), _vec(d)],
                  out_specs=[row, row], vmem_mb=32)(h, f, gate, g, sc, sh)


def _final_loss(name, h, f, tgt, gate, cmul, g, ts):
    s_len, d = h.shape

    def body(h_ref, f_ref, t_ref, gt_ref, g_ref, dh_ref, df_ref, dg_ref, dgt_ref, loss_ref):
        i = pl.program_id(0)
        fv = f_ref[...]
        coef = cmul * gt_ref[...]
        hn = h_ref[...] + coef * fv
        r, xhat = _rms(hn)
        err = xhat * g_ref[...] - t_ref[...]
        _acc_rows(loss_ref, (0.5 / d) * (err * err), i)
        dy = err * (1.0 / d)
        _acc_rows(dg_ref, dy * xhat, i)
        dxhat = dy * g_ref[...]
        dh = r * (dxhat - xhat * jnp.mean(dxhat * xhat, axis=-1, keepdims=True))
        dh_ref[...] = dh
        _acc_rows(dgt_ref, cmul * (dh * fv), i)
        df_ref[...] = (coef * dh).astype(BF16)

    row = pl.BlockSpec((ts, d), lambda i: (i, 0))
    return _pcall(body, name=name,
                  out_shape=[_sds((s_len, d), F32), _sds((s_len, d), BF16)] + [_sds((1, d), F32)] * 3,
                  grid=(s_len // ts,), in_specs=[row, row, row, _vec(d), _vec(d)],
                  out_specs=[row, row, _vec(d), _vec(d), _vec(d)], vmem_mb=40)(h, f, tgt, gate, g)


def _norm_mod_bwd(name, h, dn, dh_next, g, sc, ts, prev=None):
    s_len, d = h.shape
    has_prev = prev is not None
    cmul = prev[2] if has_prev else None

    def body(*refs):
        if has_prev:
            h_ref, dn_ref, dhn_ref, f_ref, g_ref, sc_ref, gt_ref, dh_ref, df_ref, dsh_ref, dsc_ref, dg_ref, dgt_ref = refs
        else:
            h_ref, dn_ref, dhn_ref, g_ref, sc_ref, dh_ref, dsh_ref, dsc_ref, dg_ref = refs
        i = pl.program_id(0)
        r, xhat = _rms(h_ref[...])
        dn_v = dn_ref[...]
        gv = g_ref[...]
        _acc_rows(dsh_ref, dn_v, i)
        _acc_rows(dsc_ref, dn_v * (xhat * gv), i)
        dnrm = dn_v * (1.0 + sc_ref[...])
        _acc_rows(dg_ref, dnrm * xhat, i)
        dxhat = dnrm * gv
        dh = dhn_ref[...] + r * (dxhat - xhat * jnp.mean(dxhat * xhat, axis=-1, keepdims=True))
        dh_ref[...] = dh
        if has_prev:
            _acc_rows(dgt_ref, cmul * (dh * f_ref[...]), i)
            df_ref[...] = ((cmul * gt_ref[...]) * dh).astype(BF16)

    row = pl.BlockSpec((ts, d), lambda i: (i, 0))
    if has_prev:
        ins, in_specs = [h, dn, dh_next, prev[0], g, sc, prev[1]], [row, row, row, row, _vec(d), _vec(d), _vec(d)]
        out_shape = [_sds((s_len, d), F32), _sds((s_len, d), BF16)] + [_sds((1, d), F32)] * 4
        out_specs = [row, row] + [_vec(d)] * 4
    else:
        ins, in_specs = [h, dn, dh_next, g, sc], [row, row, row, _vec(d), _vec(d)]
        out_shape = [_sds((s_len, d), F32)] + [_sds((1, d), F32)] * 3
        out_specs = [row] + [_vec(d)] * 3
    return _pcall(body, name=name, out_shape=out_shape, grid=(s_len // ts,), in_specs=in_specs,
                  out_specs=out_specs, vmem_mb=40)(*ins)


def _cols(ref, lo, hi, npc):
    parts = []
    while lo < hi:
        q, o = divmod(lo, npc)
        n = min(hi - lo, npc - o)
        parts.append(ref[q, :, o:o + n])
        lo += n
    return parts[0] if len(parts) == 1 else jnp.concatenate(parts, axis=-1)


def _store_cols(ref, lo, val, npc):
    off, width = 0, val.shape[-1]
    while off < width:
        q, o = divmod(lo + off, npc)
        n = min(width - off, npc - o)
        ref[q, :, o:o + n] = val[:, off:off + n]
        off += n


def _chips_covering(cols, npc):
    return -(-cols // npc)


def _conv_ln(a0s_ref, cw_ref, cb_ref, lg_ref, lb_ref, ts):
    a1 = cb_ref[...] + cw_ref[0:1, :] * a0s_ref[pl.ds(HALO - CONV_K + 1, ts), :]
    for k in range(1, CONV_K):
        a1 = a1 + cw_ref[k:k + 1, :] * a0s_ref[pl.ds(HALO - CONV_K + 1 + k, ts), :]
    mu = jnp.mean(a1, axis=-1, keepdims=True)
    ctr = a1 - mu
    rstd = lax.rsqrt(jnp.mean(ctr * ctr, axis=-1, keepdims=True) + EPS)
    xh = ctr * rstd
    return xh, rstd, xh * lg_ref[...] + lb_ref[...]


def _stage_glu(p_ref, ph_ref, a0s_ref, i, wc, npc, ts):
    a0 = _cols(p_ref, 0, wc, npc) * _sigmoid(_cols(p_ref, wc, 2 * wc, npc))
    a0h = _cols(ph_ref, 0, wc, npc) * _sigmoid(_cols(ph_ref, wc, 2 * wc, npc))
    a0s_ref[0:HALO, :] = jnp.where(i > 0, a0h, 0.0)
    a0s_ref[HALO:HALO + ts, :] = a0


def _mixer_mid(name, proj, cw, cb, lg, lb, wc, wp, ts):
    _, s_len, npc = proj.shape
    nq = _chips_covering(2 * wc + wp, npc)
    gi = wp // len(POOL_WINDOWS)
    hb = ts // HALO

    def body(p_ref, ph_ref, cw_ref, cb_ref, lg_ref, lb_ref, a3_ref, mx_ref, a0s_ref, vs_ref):
        i = pl.program_id(0)
        _stage_glu(p_ref, ph_ref, a0s_ref, i, wc, npc, ts)
        vs_ref[0:HALO, :] = jnp.where(i > 0, _cols(ph_ref, 2 * wc, 2 * wc + wp, npc), 0.0)
        vs_ref[HALO:HALO + ts, :] = _cols(p_ref, 2 * wc, 2 * wc + wp, npc)
        _, _, a2 = _conv_ln(a0s_ref, cw_ref, cb_ref, lg_ref, lb_ref, ts)
        a3_ref[...] = (a2 * _sigmoid(a2)).astype(BF16)
        t_abs = i * ts + lax.broadcasted_iota(jnp.int32, (ts, 1), 0)
        for g, win in enumerate(POOL_WINDOWS):
            cs = slice(g * gi, (g + 1) * gi)
            acc = vs_ref[pl.ds(HALO, ts), cs]
            for dlt in range(1, win):
                acc = acc + vs_ref[pl.ds(HALO - dlt, ts), cs]
            cnt = jnp.minimum(t_abs + 1, win).astype(F32)
            mx_ref[:, cs] = (acc / cnt - vs_ref[pl.ds(HALO, ts), cs]).astype(BF16)

    return _pcall(
        body, name=name, out_shape=[_sds((s_len, wc), BF16), _sds((s_len, wp), BF16)], grid=(s_len // ts,),
        in_specs=[pl.BlockSpec((nq, ts, npc), lambda i: (0, i, 0)),
                  pl.BlockSpec((nq, HALO, npc), lambda i: (0, jnp.maximum(i * hb - 1, 0), 0)),
                  pl.BlockSpec((HALO, wc), lambda i: (0, 0)), _vec(wc), _vec(wc), _vec(wc)],
        out_specs=[pl.BlockSpec((ts, wc), lambda i: (i, 0)), pl.BlockSpec((ts, wp), lambda i: (i, 0))],
        scratch=[pltpu.VMEM((HALO + ts, wc), F32), pltpu.VMEM((HALO + ts, wp), F32)], vmem_mb=48,
    )(proj, proj, cw, cb, lg, lb)


def _gates_fwd(name, proj, ya, yb, b_a, b_b, ls, wc, wp, ts):
    _, s_len, npc = proj.shape
    d = ya.shape[1]
    g0 = 2 * wc + wp

    def body(p_ref, ya_ref, yb_ref, ba_ref, bb_ref, ls_ref, z_ref):
        ga = _sigmoid(_cols(p_ref, g0, g0 + d, npc))
        gb = _sigmoid(_cols(p_ref, g0 + d, g0 + 2 * d, npc))
        z = ga * (ya_ref[...] + ba_ref[...]) + gb * ((yb_ref[...] + bb_ref[...]) * ls_ref[...])
        z_ref[...] = z.astype(BF16)

    row = pl.BlockSpec((ts, d), lambda i: (i, 0))
    return _pcall(body, name=name, out_shape=_sds((s_len, d), BF16), grid=(s_len // ts,),
                  in_specs=[pl.BlockSpec((N_CHIPS, ts, npc), lambda i: (0, i, 0)), row, row, _vec(d), _vec(d), _vec(d)],
                  out_specs=row, vmem_mb=48)(proj, ya, yb, b_a, b_b, ls)


def _gates_bwd(name, proj, dz, ya, yb, b_a, b_b, ls, wc, wp, ts):
    _, s_len, npc = proj.shape
    d = ya.shape[1]
    g0 = 2 * wc + wp

    def body(p_ref, dz_ref, ya_ref, yb_ref, ba_ref, bb_ref, ls_ref, dya_ref, dyb_ref, dgt_ref, dba_ref, dls_ref,
             dbb_ref):
        i = pl.program_id(0)
        ga = _sigmoid(_cols(p_ref, g0, g0 + d, npc))
        gb = _sigmoid(_cols(p_ref, g0 + d, g0 + 2 * d, npc))
        dz_v = dz_ref[...]
        y_a = ya_ref[...] + ba_ref[...]
        y_b0 = yb_ref[...] + bb_ref[...]
        ls_v = ls_ref[...]
        dya = dz_v * ga
        dya_ref[...] = dya.astype(BF16)
        _acc_rows(dba_ref, dya, i)
        t = dz_v * gb
        _acc_rows(dls_ref, t * y_b0, i)
        dyb = t * ls_v
        dyb_ref[...] = dyb.astype(BF16)
        _acc_rows(dbb_ref, dyb, i)
        dgt_ref[:, 0:d] = (dz_v * y_a * ga * (1.0 - ga)).astype(BF16)
        dgt_ref[:, d:2 * d] = (dz_v * (y_b0 * ls_v) * gb * (1.0 - gb)).astype(BF16)

    row = pl.BlockSpec((ts, d), lambda i: (i, 0))
    return _pcall(
        body, name=name,
        out_shape=[_sds((s_len, d), BF16), _sds((s_len, d), BF16), _sds((s_len, 2 * d), BF16)] + [_sds((1, d), F32)] * 3,
        grid=(s_len // ts,),
        in_specs=[pl.BlockSpec((N_CHIPS, ts, npc), lambda i: (0, i, 0)), row, row, row, _vec(d), _vec(d), _vec(d)],
        out_specs=[row, row, pl.BlockSpec((ts, 2 * d), lambda i: (i, 0))] + [_vec(d)] * 3, vmem_mb=48,
    )(proj, dz, ya, yb, b_a, b_b, ls)


def _conv_branch_bwd(name, proj, da3, cw, cb, lg, lb, wc, wp, ts):
    _, s_len, npc = proj.shape
    nq = _chips_covering(2 * wc, npc)
    hb = ts // HALO

    def body(p_ref, ph_ref, da3_ref, cw_ref, cb_ref, lg_ref, lb_ref, da1_ref, dlg_ref, dlb_ref, dcb_ref, dcw_ref,
             a0s_ref):
        i = pl.program_id(0)
        _stage_glu(p_ref, ph_ref, a0s_ref, i, wc, npc, ts)
        xh, rstd, a2 = _conv_ln(a0s_ref, cw_ref, cb_ref, lg_ref, lb_ref, ts)
        sig = _sigmoid(a2)
        da2 = da3_ref[...] * (sig * (1.0 + a2 * (1.0 - sig)))
        _acc_rows(dlg_ref, da2 * xh, i)
        _acc_rows(dlb_ref, da2, i)
        dxh = da2 * lg_ref[...]
        da1 = rstd * (dxh - jnp.mean(dxh, axis=-1, keepdims=True)
                      - xh * jnp.mean(dxh * xh, axis=-1, keepdims=True))
        da1_ref[...] = da1
        _acc_rows(dcb_ref, da1, i)

        @pl.when(i == 0)
        def _():
            dcw_ref[...] = jnp.zeros_like(dcw_ref)

        for k in range(CONV_K):
            dcw_ref[k:k + 1, :] += jnp.sum(da1 * a0s_ref[pl.ds(HALO - CONV_K + 1 + k, ts), :], axis=0,
                                           keepdims=True)

    return _pcall(
        body, name=name,
        out_shape=[_sds((s_len, wc), F32)] + [_sds((1, wc), F32)] * 3 + [_sds((HALO, wc), F32)],
        grid=(s_len // ts,),
        in_specs=[pl.BlockSpec((nq, ts, npc), lambda i: (0, i, 0)),
                  pl.BlockSpec((nq, HALO, npc), lambda i: (0, jnp.maximum(i * hb - 1, 0), 0)),
                  pl.BlockSpec((ts, wc), lambda i: (i, 0)),
                  pl.BlockSpec((HALO, wc), lambda i: (0, 0)), _vec(wc), _vec(wc), _vec(wc)],
        out_specs=[pl.BlockSpec((ts, wc), lambda i: (i, 0)), _vec(wc), _vec(wc), _vec(wc),
                   pl.BlockSpec((HALO, wc), lambda i: (0, 0))],
        scratch=[pltpu.VMEM((HALO + ts, wc), F32)], vmem_mb=48,
    )(proj, proj, da3, cw, cb, lg, lb)


def _mixer_in_bwd(name, proj, da1, dmixed, dgates, cw, wc, wp, ts):
    _, s_len, npc = proj.shape
    nq = _chips_covering(2 * wc, npc)
    gi = wp // len(POOL_WINDOWS)
    hb = ts // HALO
    n_tiles = s_len // ts
    last_hb = s_len // HALO - 1
    d2 = dgates.shape[1]

    def body(p_ref, d1_ref, d1n_ref, dm_ref, dmn_ref, dgt_ref, cw_ref, o_ref, d1s_ref, es_ref):
        i = pl.program_id(0)
        more = i < n_tiles - 1
        d1s_ref[0:ts, :] = d1_ref[...]
        d1s_ref[ts:ts + HALO, :] = jnp.where(more, d1n_ref[...], 0.0)
        da0 = cw_ref[0:1, :] * d1s_ref[pl.ds(CONV_K - 1, ts), :]
        for k in range(1, CONV_K):
            da0 = da0 + cw_ref[k:k + 1, :] * d1s_ref[pl.ds(CONV_K - 1 - k, ts), :]
        glu_a = _cols(p_ref, 0, wc, npc)
        sig = _sigmoid(_cols(p_ref, wc, 2 * wc, npc))
        _store_cols(o_ref, 0, (da0 * sig).astype(BF16), npc)
        _store_cols(o_ref, wc, (da0 * glu_a * sig * (1.0 - sig)).astype(BF16), npc)

        t_abs = i * ts + lax.broadcasted_iota(jnp.int32, (ts + HALO, 1), 0)
        dm = dm_ref[...]
        dm_ext = jnp.concatenate([dm, jnp.where(more, dmn_ref[...], 0.0)], axis=0)
        for g, win in enumerate(POOL_WINDOWS):
            cs = slice(g * gi, (g + 1) * gi)
            cnt = jnp.minimum(t_abs + 1, win).astype(F32)
            es_ref[:, cs] = dm_ext[:, cs] / cnt
        parts = []
        for g, win in enumerate(POOL_WINDOWS):
            cs = slice(g * gi, (g + 1) * gi)
            acc = es_ref[pl.ds(0, ts), cs]
            for dlt in range(1, win):
                acc = acc + es_ref[pl.ds(dlt, ts), cs]
            parts.append(acc - dm[:, cs])
        _store_cols(o_ref, 2 * wc, jnp.concatenate(parts, axis=-1).astype(BF16), npc)
        _store_cols(o_ref, 2 * wc + wp, dgt_ref[...], npc)

    nxt = lambda i: (jnp.minimum((i + 1) * hb, last_hb), 0)
    return _pcall(
        body, name=name, out_shape=_sds((N_CHIPS, s_len, npc), BF16), grid=(n_tiles,),
        in_specs=[pl.BlockSpec((nq, ts, npc), lambda i: (0, i, 0)),
                  pl.BlockSpec((ts, wc), lambda i: (i, 0)), pl.BlockSpec((HALO, wc), nxt),
                  pl.BlockSpec((ts, wp), lambda i: (i, 0)), pl.BlockSpec((HALO, wp), nxt),
                  pl.BlockSpec((ts, d2), lambda i: (i, 0)),
                  pl.BlockSpec((HALO, wc), lambda i: (0, 0))],
        out_specs=pl.BlockSpec((N_CHIPS, ts, npc), lambda i: (0, i, 0)),
        scratch=[pltpu.VMEM((ts + HALO, wc), F32), pltpu.VMEM((ts + HALO, wp), F32)], vmem_mb=48,
    )(proj, da1, da1, dmixed, dmixed, dgates, cw)


def _ada_fwd(name, c_all, w, b):
    d, cols = w.shape
    tn = 512 if cols % 512 == 0 else cols

    def body(c_ref, w_ref, b_ref, o_ref):
        cv = c_ref[...]
        sc = (cv * _sigmoid(cv)).astype(BF16)
        o_ref[...] = jnp.dot(sc, w_ref[...].astype(BF16), preferred_element_type=F32) + b_ref[...]

    return _pcall(body, name=name, out_shape=_sds((N_DEV, cols), F32), grid=(cols // tn,),
                  in_specs=[pl.BlockSpec((N_DEV, d), lambda j: (0, 0)), pl.BlockSpec((d, tn), lambda j: (0, j)),
                            pl.BlockSpec((1, tn), lambda j: (0, j))],
                  out_specs=pl.BlockSpec((N_DEV, tn), lambda j: (0, j)), vmem_mb=32)(c_all, w, b)


def _adam_math(w, g, m, v):
    m_new = ADAM_B1 * m + (1.0 - ADAM_B1) * g
    v_new = ADAM_B2 * v + (1.0 - ADAM_B2) * (g * g)
    m_hat = m_new / (1.0 - ADAM_B1 ** ADAM_STEP)
    v_hat = v_new / (1.0 - ADAM_B2 ** ADAM_STEP)
    delta = -ADAM_LR * (m_hat / (jnp.sqrt(v_hat) + ADAM_EPS) + ADAM_WD * w)
    return delta, m_new, v_new


def _adamw(name, w, g, m, v):
    rows, cols = w.shape
    tr = _row_tile(rows, cols, 262144)

    def body(w_ref, g_ref, m_ref, v_ref, d_ref, mo_ref, vo_ref):
        d_ref[...], mo_ref[...], vo_ref[...] = _adam_math(w_ref[...], g_ref[...], m_ref[...], v_ref[...])

    spec = pl.BlockSpec((tr, cols), lambda i: (i, 0))
    return _pcall(body, name=name, out_shape=[_sds(w.shape, F32)] * 3, grid=(rows // tr,), in_specs=[spec] * 4,
                  out_specs=[spec] * 3, vmem_mb=40)(w, g, m, v)


def _ada_grad_adamw(name, c_t, d_ada, w, m, v):
    rows, cols = w.shape
    tr = _tile(rows, 256)
    tc = _tile(cols, 1536) if cols % 1536 == 0 else cols

    def body(c_ref, da_ref, w_ref, m_ref, v_ref, g_ref, d_ref, mo_ref, vo_ref):
        cv = c_ref[...]
        sc = cv * _sigmoid(cv)
        g = sc[:, 0:1] * da_ref[0:1, :]
        for b in range(1, N_DEV):
            g = g + sc[:, b:b + 1] * da_ref[b:b + 1, :]
        g_ref[...] = g
        d_ref[...], mo_ref[...], vo_ref[...] = _adam_math(w_ref[...], g, m_ref[...], v_ref[...])

    spec = pl.BlockSpec((tr, tc), lambda i, j: (i, j))
    return _pcall(body, name=name, out_shape=[_sds(w.shape, F32)] * 4, grid=(rows // tr, cols // tc),
                  in_specs=[pl.BlockSpec((tr, N_DEV), lambda i, j: (i, 0)),
                            pl.BlockSpec((N_DEV, tc), lambda i, j: (0, j)), spec, spec, spec],
                  out_specs=[spec] * 4, vmem_mb=40)(c_t, d_ada, w, m, v)


def _sum_devices(name, gathered, m_per):
    n = gathered.shape[1]

    def body(g_ref, o_ref):
        acc = g_ref[0:m_per, :]
        for dev in range(1, N_DEV):
            acc = acc + g_ref[dev * m_per:(dev + 1) * m_per, :]
        o_ref[...] = acc

    return _pcall(body, name=name, out_shape=_sds((m_per, n), F32),
                  in_specs=[pl.BlockSpec(memory_space=pltpu.VMEM)],
                  out_specs=pl.BlockSpec(memory_space=pltpu.VMEM))(gathered)


def _ffn_fwd(tag, n, w_in_g, w_out2d, dims, after_swiglu=lambda: None):
    s_len, d, f_dim = dims["S"], dims["D"], dims["F"]
    p = f_dim // 2
    tm, tn = _tile(s_len, 1024), math.gcd(p, 256)
    nbp = p // tn

    def ep(accs, ex, outs):
        hh, uu = accs
        outs[0][0] = hh.astype(BF16)
        outs[0][1] = uu.astype(BF16)
        outs[1][...] = (hh * _sigmoid(hh) * uu).astype(BF16)

    hu, act = _matmul(
        f"{tag}_swiglu", n, [w_in_g, w_in_g], mode="nn", grid=(s_len // tm, f_dim // tn, 1),
        a_spec=pl.BlockSpec((tm, d), lambda i, j, k: (i, 0)),
        b_specs=[pl.BlockSpec((None, d, tn), lambda i, j, k: (j // nbp, 0, j % nbp)),
                 pl.BlockSpec((None, d, tn), lambda i, j, k: (2 + j // nbp, 0, j % nbp))],
        out_shape=[_sds((2, s_len, f_dim), BF16), _sds((s_len, f_dim), BF16)],
        out_specs=[pl.BlockSpec((2, tm, tn), lambda i, j, k: (0, i, j)),
                   pl.BlockSpec((tm, tn), lambda i, j, k: (i, j))],
        acc_shape=(tm, tn), epilogue=ep)
    after_swiglu()
    tf = f_dim // 4
    tn2 = _tile(d, 1024)
    f = _matmul(
        f"{tag}_down", act, [w_out2d], mode="nn", grid=(s_len // tm, d // tn2, 4),
        a_spec=pl.BlockSpec((tm, tf), lambda i, j, k: (i, k)),
        b_specs=[pl.BlockSpec((tf, tn2), lambda i, j, k: (k, j))],
        out_shape=_sds((s_len, d), F32), out_specs=pl.BlockSpec((tm, tn2), lambda i, j, k: (i, j)),
        acc_shape=(tm, tn2), epilogue=_ep_store(F32))
    return hu, act, f


def _ffn_bwd(tag, n, hu, act, df, w_in_g, w_out2d, dims, after_dw_out, after_dw_in):
    s_len, d, f_dim = dims["S"], dims["D"], dims["F"]
    tf = f_dim // 4
    tk = _tile(s_len, 512)
    tn = _tile(d, 1024)
    g_out = _matmul(
        f"{tag}_dw_out", act, [df], mode="tn", grid=(4, d // tn, s_len // tk),
        a_spec=pl.BlockSpec((tk, tf), lambda i, j, k: (k, i)),
        b_specs=[pl.BlockSpec((tk, tn), lambda i, j, k: (k, j))],
        out_shape=_sds((2, 4, tf // 2, d), F32),
        out_specs=pl.BlockSpec((2, None, tf // 2, tn), lambda i, j, k: (0, i, 0, j)),
        acc_shape=(tf, tn), epilogue=_ep_halves(tf // 2))
    after_dw_out(g_out)

    def ep_dhu(accs, ex, outs):
        da = accs[0]
        hh, uu = ex[0][0].astype(F32), ex[0][1].astype(F32)
        sig = _sigmoid(hh)
        outs[0][0] = (da * uu * (sig * (1.0 + hh * (1.0 - sig)))).astype(BF16)
        outs[0][1] = (da * (hh * sig)).astype(BF16)

    tm = _tile(s_len, 512)
    hu_spec = pl.BlockSpec((2, tm, tf), lambda i, j, k: (0, i, j))
    dhu = _matmul(
        f"{tag}_dhu", df, [w_out2d], mode="nt", grid=(s_len // tm, 4, 1),
        a_spec=pl.BlockSpec((tm, d), lambda i, j, k: (i, 0)),
        b_specs=[pl.BlockSpec((tf, d), lambda i, j, k: (j, 0))],
        extras=[hu], extra_specs=[hu_spec],
        out_shape=_sds((2, s_len, f_dim), BF16), out_specs=hu_spec, acc_shape=(tm, tf), epilogue=ep_dhu)

    hd = d // 2
    g_in = _matmul(
        f"{tag}_dw_in", n, [dhu], mode="tn", grid=(2, 8, s_len // tk),
        a_spec=pl.BlockSpec((tk, hd), lambda i, j, k: (k, i)),
        b_specs=[pl.BlockSpec((None, tk, tf), lambda i, j, k: (j // 4, k, j % 4))],
        out_shape=_sds((2, 4, hd, f_dim // 2), F32),
        out_specs=pl.BlockSpec((None, None, hd, tf), lambda i, j, k: (i, j // 2, 0, j % 2)),
        acc_shape=(hd, tf), epilogue=_ep_store(F32))
    after_dw_in(g_in)

    tm2 = _tile(s_len, 1024)
    dn = _matmul(
        f"{tag}_dn", dhu, [w_in_g], mode="nt", grid=(s_len // tm2, d // tn, 8),
        a_spec=pl.BlockSpec((None, tm2, tf), lambda i, j, k: (k // 4, i, k % 4)),
        b_specs=[pl.BlockSpec((None, tn, tf), lambda i, j, k: (k // 2, j, k % 2))],
        out_shape=_sds((s_len, d), F32), out_specs=pl.BlockSpec((tm2, tn), lambda i, j, k: (i, j)),
        acc_shape=(tm2, tn), epilogue=_ep_store(F32))
    return dn


def kernel(x, c, w_ada, b_ada, g_ffn1, w1_in, w1_out, g_mix, w_in, conv_w, conv_b, ln_a_g, ln_a_b, w_a_out, b_a_out, w_b_group, b_b_group, ls_b, w_out, g_ffn2, w2_in, w2_out, g_final, loss_target, m_w_ada, m_b_ada, m_g_ffn1, m_w1_in, m_w1_out, m_g_mix, m_w_in, m_conv_w, m_conv_b, m_ln_a_g, m_ln_a_b, m_w_a_out, m_b_a_out, m_w_b_group, m_b_b_group, m_ls_b, m_w_out, m_g_ffn2, m_w2_in, m_w2_out, m_g_final, v_w_ada, v_b_ada, v_g_ffn1, v_w1_in, v_w1_out, v_g_mix, v_w_in, v_conv_w, v_conv_b, v_ln_a_g, v_ln_a_b, v_w_a_out, v_b_a_out, v_w_b_group, v_b_b_group, v_ls_b, v_w_out, v_g_ffn2, v_w2_in, v_w2_out, v_g_final):
    weights = dict(w_ada=w_ada, b_ada=b_ada, g_ffn1=g_ffn1, w1_in=w1_in, w1_out=w1_out, g_mix=g_mix, w_in=w_in,
                   conv_w=conv_w, conv_b=conv_b, ln_a_g=ln_a_g, ln_a_b=ln_a_b, w_a_out=w_a_out, b_a_out=b_a_out,
                   w_b_group=w_b_group, b_b_group=b_b_group, ls_b=ls_b, w_out=w_out, g_ffn2=g_ffn2, w2_in=w2_in,
                   w2_out=w2_out, g_final=g_final)
    mom1 = dict(w_ada=m_w_ada, b_ada=m_b_ada, g_ffn1=m_g_ffn1, w1_in=m_w1_in, w1_out=m_w1_out, g_mix=m_g_mix,
                w_in=m_w_in, conv_w=m_conv_w, conv_b=m_conv_b, ln_a_g=m_ln_a_g, ln_a_b=m_ln_a_b, w_a_out=m_w_a_out,
                b_a_out=m_b_a_out, w_b_group=m_w_b_group, b_b_group=m_b_b_group, ls_b=m_ls_b, w_out=m_w_out,
                g_ffn2=m_g_ffn2, w2_in=m_w2_in, w2_out=m_w2_out, g_final=m_g_final)
    mom2 = dict(w_ada=v_w_ada, b_ada=v_b_ada, g_ffn1=v_g_ffn1, w1_in=v_w1_in, w1_out=v_w1_out, g_mix=v_g_mix,
                w_in=v_w_in, conv_w=v_conv_w, conv_b=v_conv_b, ln_a_g=v_ln_a_g, ln_a_b=v_ln_a_b, w_a_out=v_w_a_out,
                b_a_out=v_b_a_out, w_b_group=v_w_b_group, b_b_group=v_b_b_group, ls_b=v_ls_b, w_out=v_w_out,
                g_ffn2=v_g_ffn2, w2_in=v_w2_in, w2_out=v_w2_out, g_final=v_g_final)
    order = list(weights)

    s_len, d = x.shape[1], x.shape[2]
    f_dim = w1_out.shape[0] * N_CHIPS
    wc = conv_w.shape[1] * N_CHIPS
    wp = w_b_group.shape[0] * w_b_group.shape[1]
    n_groups, gi, goq = w_b_group.shape
    npc = w_in.shape[1]
    ada_c = w_ada.shape[1]
    dims = dict(S=s_len, D=d, F=f_dim)
    ts = _tile(s_len, 256)

    xi, yi, ci = lax.axis_index("x"), lax.axis_index("y"), lax.axis_index("c")
    q = 2 * xi + yi
    dev = 2 * q + ci
    q_idx = jnp.reshape(q, (1,)).astype(jnp.int32)
    c_idx = jnp.reshape(ci, (1,)).astype(jnp.int32)
    qc_idx = jnp.stack([q, ci]).astype(jnp.int32)
    _PREVIOUS.clear()

    as2d = lambda a: a.reshape(-1, a.shape[-1])
    groups = dict(ffn1=["w1_in", "w1_out"], mix=["w_in", "w_a_out", "w_b_group", "w_out"], ffn2=["w2_in", "w2_out"])
    big = [nm for grp in groups.values() for nm in grp]
    ici = {}
    for grp, names in groups.items():
        ici[grp] = _gather_ici(f"gather_{grp}_ici",
                               [_cast_into_gathered(f"cast_{nm}", as2d(weights[nm]), q_idx) for nm in names])

    def arrived(grp):
        return _gather_d2d(f"gather_{grp}_d2d", ici[grp].wait())

    def gathered(fwd, grp):
        return {nm: g.reshape(N_CHIPS, 2 * g.shape[2], g.shape[3]) for nm, g in zip(groups[grp], fwd.wait())}

    cwq = conv_w.shape[1]
    pack0 = jnp.concatenate([c.reshape(-1), conv_w.reshape(-1), b_b_group.reshape(-1)])
    n0 = -(-pack0.shape[0] // (8 * LANES)) * LANES
    pack0 = jnp.pad(pack0, (0, 8 * n0 - pack0.shape[0])).reshape(8, n0)
    g0 = _allgather_small("gather_small_in", pack0).reshape(N_DEV, 8 * n0)
    c_all = g0[:, :d]
    south = g0[0::2]
    cw_full = jnp.concatenate([south[k, d:d + CONV_K * cwq].reshape(CONV_K, cwq) for k in range(N_CHIPS)], axis=1)
    cw_pad = jnp.pad(cw_full, ((0, HALO - CONV_K), (0, 0)))
    o_bb = d + CONV_K * cwq
    bb_full = jnp.concatenate([south[k, o_bb:o_bb + n_groups * goq].reshape(n_groups, goq) for k in range(N_CHIPS)],
                              axis=1).reshape(1, d)

    b_ada_mine = lax.dynamic_slice(b_ada, (q * ada_c,), (ada_c,)).reshape(1, ada_c)
    ada_piece = _ada_fwd("ada_fwd", c_all, w_ada, b_ada_mine)
    g1 = _allgather_small("gather_ada", ada_piece).reshape(N_DEV, N_DEV, ada_c)
    ada_rows = lax.dynamic_index_in_dim(g1[0::2], dev, axis=1, keepdims=False)
    ada = ada_rows.reshape(3, 3, 1, d)
    (sh1, sc1, gt1), (sh2, sc2, gt2), (sh3, sc3, gt3) = [[ada[i, j] for j in range(3)] for i in range(3)]

    row = lambda vct: vct.reshape(1, -1)
    g1v, gmv, g2v, gfv = row(g_ffn1), row(g_mix), row(g_ffn2), row(g_final)

    x2 = x[0]
    tgt = loss_target[0]

    n1 = _norm_mod("ffn1_norm", x2, g1v, sc1, sh1, ts)
    wts = gathered(arrived("ffn1"), "ffn1")
    w1_in_g, w1_out_2d = wts["w1_in"], wts["w1_out"].reshape(f_dim, d)
    fwd = {}
    hu1, act1, f1 = _ffn_fwd("ffn1", n1, w1_in_g, w1_out_2d, dims,
                             after_swiglu=lambda: fwd.update(mix=arrived("mix")))
    h1, n2 = _residual_norm_mod("mix_norm", x2, f1, gt1, 0.5, gmv, sc2, sh2, ts)
    wts = gathered(fwd["mix"], "mix")
    w_in_g, w_out_2d = wts["w_in"], wts["w_out"].reshape(d, d)
    w_a_g = wts["w_a_out"]
    w_b_g = wts["w_b_group"]

    tm = _tile(s_len, 1024)
    tnp = npc // 2
    proj = _matmul(
        "mix_proj", n2, [w_in_g], mode="nn", grid=(s_len // tm, 8, 1),
        a_spec=pl.BlockSpec((tm, d), lambda i, j, k: (i, 0)),
        b_specs=[pl.BlockSpec((None, d, tnp), lambda i, j, k: (j // 2, 0, j % 2))],
        out_shape=_sds((N_CHIPS, s_len, npc), F32),
        out_specs=pl.BlockSpec((None, tm, tnp), lambda i, j, k: (j // 2, i, j % 2)),
        acc_shape=(tm, tnp), epilogue=_ep_store(F32))
    fwd["ffn2"] = arrived("ffn2")
    cbv, lgv, lbv = row(conv_b), row(ln_a_g), row(ln_a_b)
    a3, mixed = _mixer_mid("mix_mid", proj, cw_pad, cbv, lgv, lbv, wc, wp, ts)
    dq = d // N_CHIPS
    ya = _matmul(
        "mix_ya", a3, [w_a_g], mode="nn", grid=(s_len // tm, N_CHIPS, 1),
        a_spec=pl.BlockSpec((tm, wc), lambda i, j, k: (i, 0)),
        b_specs=[pl.BlockSpec((None, wc, dq), lambda i, j, k: (j, 0, 0))],
        out_shape=_sds((s_len, d), F32), out_specs=pl.BlockSpec((tm, dq), lambda i, j, k: (i, j)),
        acc_shape=(tm, dq), epilogue=_ep_store(F32))
    yb = _matmul(
        "mix_yb", mixed, [w_b_g], mode="nn", grid=(s_len // tm, n_groups * N_CHIPS, 1),
        a_spec=pl.BlockSpec((tm, gi), lambda i, j, k: (i, j // N_CHIPS)),
        b_specs=[pl.BlockSpec((None, gi, goq), lambda i, j, k: (j % N_CHIPS, j // N_CHIPS, 0))],
        out_shape=_sds((s_len, d), F32), out_specs=pl.BlockSpec((tm, goq), lambda i, j, k: (i, j)),
        acc_shape=(tm, goq), epilogue=_ep_store(F32))
    bav, lsv = row(b_a_out), row(ls_b)
    z = _gates_fwd("mix_gates", proj, ya, yb, bav, bb_full, lsv, wc, wp, ts)
    tn = _tile(d, 1024)
    mix = _matmul(
        "mix_out", z, [w_out_2d], mode="nn", grid=(s_len // tm, d // tn, 1),
        a_spec=pl.BlockSpec((tm, d), lambda i, j, k: (i, 0)),
        b_specs=[pl.BlockSpec((d, tn), lambda i, j, k: (0, j))],
        out_shape=_sds((s_len, d), F32), out_specs=pl.BlockSpec((tm, tn), lambda i, j, k: (i, j)),
        acc_shape=(tm, tn), epilogue=_ep_store(F32))
    h2, n3 = _residual_norm_mod("ffn2_norm", h1, mix, gt2, 1.0, g2v, sc3, sh3, ts)
    wts = gathered(fwd["ffn2"], "ffn2")
    w2_in_g, w2_out_2d = wts["w2_in"], wts["w2_out"].reshape(f_dim, d)
    hu2, act2, f3 = _ffn_fwd("ffn2", n3, w2_in_g, w2_out_2d, dims)

    dh3, df3, d_gf, d_gt3, loss_cols = _final_loss("final_loss", h2, f3, tgt, gt3, 0.5, gfv, ts)
    rs, held = {}, {}
    dn3 = _ffn_bwd(
        "ffn2", n3, hu2, act2, df3, w2_in_g, w2_out_2d, dims,
        after_dw_out=lambda g: held.update(w2_out=g),
        after_dw_in=lambda g: rs.update(ffn2=_ReduceScatter("g_ffn2", ["w2_out", "w2_in"], [held["w2_out"], g],
                                                            c_idx, qc_idx)))
    dh2, dmix, d_sh3, d_sc3, d_g2, d_gt2 = _norm_mod_bwd("ffn2_norm_bwd", h2, dn3, dh3, g2v, sc3, ts,
                                                         prev=(mix, gt2, 1.0))
    rs["ffn2"].step2()

    tk = _tile(s_len, 512)
    hq = d // (2 * N_CHIPS)
    gw_out = _matmul(
        "mix_dw_out", z, [dmix], mode="tn", grid=(N_CHIPS, d // tn, s_len // tk),
        a_spec=pl.BlockSpec((tk, 2 * hq), lambda i, j, k: (k, i)),
        b_specs=[pl.BlockSpec((tk, tn), lambda i, j, k: (k, j))],
        out_shape=_sds((2, N_CHIPS, hq, d), F32),
        out_specs=pl.BlockSpec((2, None, hq, tn), lambda i, j, k: (0, i, 0, j)),
        acc_shape=(2 * hq, tn), epilogue=_ep_halves(hq))
    dz = _matmul(
        "mix_dz", dmix, [w_out_2d], mode="nt", grid=(s_len // tm, d // tn, 1),
        a_spec=pl.BlockSpec((tm, d), lambda i, j, k: (i, 0)),
        b_specs=[pl.BlockSpec((tn, d), lambda i, j, k: (j, 0))],
        out_shape=_sds((s_len, d), F32), out_specs=pl.BlockSpec((tm, tn), lambda i, j, k: (i, j)),
        acc_shape=(tm, tn), epilogue=_ep_store(F32))
    dya, dyb, dgates, d_ba, d_ls, d_bb = _gates_bwd("mix_gates_bwd", proj, dz, ya, yb, bav, bb_full, lsv, wc, wp, ts)
    gw_a = _matmul(
        "mix_dw_a", a3, [dya], mode="tn", grid=(1, N_CHIPS, s_len // tk),
        a_spec=pl.BlockSpec((tk, wc), lambda i, j, k: (k, 0)),
        b_specs=[pl.BlockSpec((tk, dq), lambda i, j, k: (k, j))],
        out_shape=_sds((2, N_CHIPS, wc // 2, dq), F32),
        out_specs=pl.BlockSpec((2, None, wc // 2, dq), lambda i, j, k: (0, j, 0, 0)),
        acc_shape=(wc, dq), epilogue=_ep_halves(wc // 2))
    da3 = _matmul(
        "mix_da3", dya, [w_a_g], mode="nt", grid=(s_len // tm, 1, N_CHIPS),
        a_spec=pl.BlockSpec((tm, dq), lambda i, j, k: (i, k)),
        b_specs=[pl.BlockSpec((None, wc, dq), lambda i, j, k: (k, 0, 0))],
        out_shape=_sds((s_len, wc), F32), out_specs=pl.BlockSpec((tm, wc), lambda i, j, k: (i, 0)),
        acc_shape=(tm, wc), epilogue=_ep_store(F32))
    gpr = n_groups // 2
    gw_b = _matmul(
        "mix_dw_b", mixed, [dyb], mode="tn", grid=(1, n_groups * N_CHIPS, s_len // tk),
        a_spec=pl.BlockSpec((tk, gi), lambda i, j, k: (k, j // N_CHIPS)),
        b_specs=[pl.BlockSpec((tk, goq), lambda i, j, k: (k, j))],
        out_shape=_sds((2, N_CHIPS, gpr * gi, goq), F32),
        out_specs=pl.BlockSpec((None, None, gi, goq),
                               lambda i, j, k: ((j // N_CHIPS) // gpr, j % N_CHIPS, (j // N_CHIPS) % gpr, 0)),
        acc_shape=(gi, goq), epilogue=_ep_store(F32))
    dmixed = _matmul(
        "mix_dmixed", dyb, [w_b_g], mode="nt", grid=(s_len // tm, n_groups, N_CHIPS),
        a_spec=pl.BlockSpec((tm, goq), lambda i, j, k: (i, j * N_CHIPS + k)),
        b_specs=[pl.BlockSpec((None, gi, goq), lambda i, j, k: (k, j, 0))],
        out_shape=_sds((s_len, wp), F32), out_specs=pl.BlockSpec((tm, gi), lambda i, j, k: (i, j)),
        acc_shape=(tm, gi), epilogue=_ep_store(F32))
    da1, d_lg, d_lb, d_cb, d_cw = _conv_branch_bwd("mix_conv_bwd", proj, da3, cw_pad, cbv, lgv, lbv, wc, wp, ts)
    dproj = _mixer_in_bwd("mix_in_bwd", proj, da1, dmixed, dgates, cw_pad, wc, wp, ts)
    hd = d // 2
    gw_in = _matmul(
        "mix_dw_in", n2, [dproj], mode="tn", grid=(2, 8, s_len // tk),
        a_spec=pl.BlockSpec((tk, hd), lambda i, j, k: (k, i)),
        b_specs=[pl.BlockSpec((None, tk, tnp), lambda i, j, k: (j // 2, k, j % 2))],
        out_shape=_sds((2, N_CHIPS, hd, npc), F32),
        out_specs=pl.BlockSpec((None, None, hd, tnp), lambda i, j, k: (i, j // 2, 0, j % 2)),
        acc_shape=(hd, tnp), epilogue=_ep_store(F32))
    rs["mix"] = _ReduceScatter("g_mix", groups["mix"], [gw_in, gw_a, gw_b, gw_out], c_idx, qc_idx)
    rs["ffn2"].step3()
    dn2 = _matmul(
        "mix_dn", dproj, [w_in_g], mode="nt", grid=(s_len // tm, d // tn, N_CHIPS),
        a_spec=pl.BlockSpec((None, tm, npc), lambda i, j, k: (k, i, 0)),
        b_specs=[pl.BlockSpec((None, tn, npc), lambda i, j, k: (k, j, 0))],
        out_shape=_sds((s_len, d), F32), out_specs=pl.BlockSpec((tm, tn), lambda i, j, k: (i, j)),
        acc_shape=(tm, tn), epilogue=_ep_store(F32))
    dh1, df1, d_sh2, d_sc2, d_gm, d_gt1 = _norm_mod_bwd("mix_norm_bwd", h1, dn2, dh2, gmv, sc2, ts,
                                                        prev=(f1, gt1, 0.5))
    rs["mix"].step2()

    def w1_in_ready(g):
        rs["w1_in"] = _ReduceScatter("g_w1_in", ["w1_in"], [g], c_idx, qc_idx)
        rs["w1_out"].step2()
        rs["mix"].step3()

    dn1 = _ffn_bwd(
        "ffn1", n1, hu1, act1, df1, w1_in_g, w1_out_2d, dims,
        after_dw_out=lambda g: rs.update(w1_out=_ReduceScatter("g_w1_out", ["w1_out"], [g], c_idx, qc_idx)),
        after_dw_in=w1_in_ready)
    rs["w1_in"].step2()
    grad_x, d_sh1, d_sc1, d_g1 = _norm_mod_bwd("ffn1_norm_bwd", x2, dn1, dh1, g1v, sc1, ts)

    d_ada = jnp.concatenate([d_sh1, d_sc1, d_gt1, d_sh2, d_sc2, d_gt2, d_sh3, d_sc3, d_gt3], axis=1)
    small = [d_ada, d_g1, d_gm, d_cw[:CONV_K].reshape(1, -1), d_cb, d_lg, d_lb, d_ba, d_bb, d_ls, d_g2, d_gf,
             loss_cols]
    sizes = [a.shape[1] for a in small]
    pack1 = jnp.concatenate(small, axis=1).reshape(-1)
    n1p = -(-pack1.shape[0] // (8 * LANES)) * LANES
    pack1 = jnp.pad(pack1, (0, 8 * n1p - pack1.shape[0])).reshape(8, n1p)
    g2 = _allgather_small("gather_small_grads", pack1)
    total = _sum_devices("sum_small_grads", g2, 8).reshape(-1)
    offs = [0]
    for sz in sizes:
        offs.append(offs[-1] + sz)
    tot = [total[offs[k]:offs[k + 1]] for k in range(len(sizes))]
    d_ada_all = g2.reshape(N_DEV, 8 * n1p)[:, :sizes[0]]
    loss = jnp.sum(tot[12])

    grads = {}
    grads["b_ada"] = tot[0]
    grads["g_ffn1"], grads["g_mix"] = tot[1], tot[2]
    grads["conv_w"] = lax.dynamic_slice(tot[3].reshape(CONV_K, wc), (0, q * cwq), (CONV_K, cwq))
    grads["conv_b"], grads["ln_a_g"], grads["ln_a_b"], grads["b_a_out"] = tot[4], tot[5], tot[6], tot[7]
    grads["b_b_group"] = lax.dynamic_slice(tot[8].reshape(n_groups, N_CHIPS * goq), (0, q * goq), (n_groups, goq))
    grads["ls_b"], grads["g_ffn2"], grads["g_final"] = tot[9], tot[10], tot[11]

    delta, new_m, new_v = {}, {}, {}

    def adamw_group(reduced):
        for nm, g in reduced.items():
            shp = weights[nm].shape
            dl, mo, vo = _adamw(f"adamw_{nm}", as2d(weights[nm]), g, as2d(mom1[nm]), as2d(mom2[nm]))
            grads[nm], delta[nm], new_m[nm], new_v[nm] = g.reshape(shp), dl.reshape(shp), mo.reshape(shp), vo.reshape(shp)

    adamw_group(rs["ffn2"].result())
    rs["w1_out"].step3()
    adamw_group(rs["mix"].result())
    d_ada_mine = lax.dynamic_slice(d_ada_all, (0, q * ada_c), (N_DEV, ada_c))
    grads["w_ada"], delta["w_ada"], new_m["w_ada"], new_v["w_ada"] = _ada_grad_adamw(
        "adamw_w_ada", c_all.T, d_ada_mine, w_ada, m_w_ada, v_w_ada)
    rs["w1_in"].step3()
    smalls = [nm for nm in order if nm not in big and nm != "w_ada"]
    flat = lambda src: jnp.concatenate([src[nm].reshape(-1) for nm in smalls])
    n_small = sum(weights[nm].size for nm in smalls)
    rows_s = -(-n_small // (8 * LANES)) * 8
    packed = [jnp.pad(flat(src), (0, rows_s * LANES - n_small)).reshape(rows_s, LANES)
              for src in (weights, grads, mom1, mom2)]
    dl_s, mo_s, vo_s = _adamw("adamw_small", *packed)
    off = 0
    for nm in smalls:
        sz, shp = weights[nm].size, weights[nm].shape
        delta[nm] = dl_s.reshape(-1)[off:off + sz].reshape(shp)
        new_m[nm] = mo_s.reshape(-1)[off:off + sz].reshape(shp)
        new_v[nm] = vo_s.reshape(-1)[off:off + sz].reshape(shp)
        grads[nm] = grads[nm].reshape(shp)
        off += sz
    adamw_group(rs["w1_out"].result())
    adamw_group(rs["w1_in"].result())

    return (loss, grad_x[None], *[grads[nm] for nm in order], *[delta[nm] for nm in order],
            *[new_m[nm] for nm in order], *[new_v[nm] for nm in order])
```

```python
import math

import jax
import jax.numpy as jnp
from jax import lax
from jax.experimental import pallas as pl
from jax.experimental.pallas import tpu as pltpu

F32 = jnp.float32
BF16 = jnp.bfloat16
MESH = pl.DeviceIdType.MESH
ANY = pl.BlockSpec(memory_space=pl.ANY)
HBM = pl.BlockSpec(memory_space=pltpu.HBM)
SEM = pl.BlockSpec(memory_space=pltpu.SEMAPHORE)
EFFECT = pltpu.SideEffectType.DATAFLOW_SIDE_EFFECTING

EPS = 1e-6
CONV_K = 31
HALO = 32
POOL_WINDOWS = (2, 4, 8, 16)
N_CHIPS = 4
N_DEV = 8
LANES = 128

ADAM_LR = 0.001
ADAM_B1 = 0.9
ADAM_B2 = 0.999
ADAM_EPS = 1e-08
ADAM_WD = 0.01
ADAM_STEP = 10

DN = {
    "nn": (((1,), (0,)), ((), ())),
    "nt": (((1,), (1,)), ((), ())),
    "tn": (((0,), (0,)), ((), ())),
}


_PREVIOUS = []


def _ordered(call, args, n_lead, body, token=None, sources=()):
    dep = [p for p in _PREVIOUS if all(p is not a for a in (*args, *sources))]

    def wrapped(*refs):
        return body(*refs[:n_lead], *refs[n_lead + len(dep):])

    outs = call(wrapped, [ANY] * len(dep))(*args, *dep)
    seq = outs if isinstance(outs, (list, tuple)) else [outs]
    _PREVIOUS[:] = [seq[token] if token is not None else
                    next(o for o in seq if jnp.issubdtype(o.dtype, jnp.floating))]
    return outs


def _pcall(body, *, name, out_shape, grid=None, in_specs=None, out_specs=None, scratch=(), aliases=None,
           prefetch=0, vmem_mb=None):
    params = {}
    if grid is not None:
        params["dimension_semantics"] = ("arbitrary",) * len(grid)
    if vmem_mb is not None:
        params["vmem_limit_bytes"] = vmem_mb << 20
    kw = dict(name=name, out_shape=out_shape, compiler_params=pltpu.CompilerParams(**params))
    if aliases:
        kw["input_output_aliases"] = aliases

    def call(wrapped, dep_specs):
        specs = list(in_specs) + dep_specs
        if prefetch:
            return pl.pallas_call(wrapped, grid_spec=pltpu.PrefetchScalarGridSpec(
                num_scalar_prefetch=prefetch, grid=grid, in_specs=specs, out_specs=out_specs,
                scratch_shapes=list(scratch)), **kw)
        if grid is not None:
            return pl.pallas_call(wrapped, grid=grid, in_specs=specs, out_specs=out_specs,
                                  scratch_shapes=list(scratch), **kw)
        return pl.pallas_call(wrapped, in_specs=specs, out_specs=out_specs, scratch_shapes=list(scratch), **kw)

    return lambda *args: _ordered(call, args, prefetch + len(in_specs), body)


def _tile(dim, pref):
    t = min(dim, pref)
    assert dim % t == 0, (dim, pref)
    return t


def _sds(shape, dtype):
    return jax.ShapeDtypeStruct(tuple(shape), dtype)


def _sigmoid(v):
    return 1.0 / (1.0 + jnp.exp(-v))


def _vec(w):
    return pl.BlockSpec((1, w), lambda *_: (0, 0))


def _acc_rows(ref, val, i):
    @pl.when(i == 0)
    def _():
        ref[...] = jnp.zeros_like(ref)

    ref[...] += jnp.sum(val, axis=0, keepdims=True)


def _matmul(name, a, bs, *, mode, grid, a_spec, b_specs, out_shape, out_specs, acc_shape, epilogue,
            extras=(), extra_specs=(), vmem_mb=48):
    nb, ne, nk = len(bs), len(extras), grid[2]
    dn = DN[mode]

    def body(*refs):
        a_ref, b_refs, ex = refs[0], refs[1:1 + nb], refs[1 + nb:1 + nb + ne]
        if nk == 1:
            outs = refs[1 + nb + ne:]
            accs = [lax.dot_general(a_ref[...], b[...], dn, preferred_element_type=F32) for b in b_refs]
            epilogue(accs, ex, outs)
            return
        outs, acc_refs = refs[1 + nb + ne:-nb], refs[-nb:]
        k = pl.program_id(2)

        @pl.when(k == 0)
        def _():
            for acc in acc_refs:
                acc[...] = jnp.zeros_like(acc)

        for acc, b in zip(acc_refs, b_refs):
            acc[...] += lax.dot_general(a_ref[...], b[...], dn, preferred_element_type=F32)

        @pl.when(k == nk - 1)
        def _():
            epilogue([acc[...] for acc in acc_refs], ex, outs)

    scratch = [pltpu.VMEM(acc_shape, F32) for _ in range(nb)] if nk > 1 else []
    return _pcall(body, name=name, out_shape=out_shape, grid=grid,
                  in_specs=[a_spec, *b_specs, *extra_specs], out_specs=out_specs, scratch=scratch,
                  vmem_mb=vmem_mb)(a, *bs, *extras)


def _ep_store(dtype):
    def ep(accs, ex, outs):
        outs[0][...] = accs[0].astype(dtype)
    return ep


def _ep_halves(h):
    def ep(accs, ex, outs):
        outs[0][0] = accs[0][:h]
        outs[0][1] = accs[0][h:]
    return ep


def _place():
    x, y, c = lax.axis_index("x"), lax.axis_index("y"), lax.axis_index("c")
    chips = [(1 - x, y), (x, 1 - y), (1 - x, 1 - y)]
    return x, y, c, chips


def _allgather_small(name, block):
    m_per, n = block.shape

    def body(x_ref, out_ref, send_sems, recv_sems, local_sem):
        x, y, c, chips = _place()
        me, sibling = (x, y, c), (x, y, 1 - c)

        def rows(px, py, pc):
            return out_ref.at[pl.ds((4 * px + 2 * py + pc) * m_per, m_per), :]

        def copy(k, blk, to, src=None):
            return pltpu.make_async_remote_copy(
                src_ref=rows(*blk) if src is None else src, dst_ref=rows(*blk),
                send_sem=send_sems.at[k], recv_sem=recv_sems.at[k], device_id=to, device_id_type=MESH)

        mine = pltpu.make_async_copy(x_ref, rows(*me), local_sem)
        mine.start()
        first = [copy(0, me, sibling, src=x_ref)]
        first += [copy(1 + j, me, (*chip, c), src=x_ref) for j, chip in enumerate(chips)]
        for cp in first:
            cp.start()
        passed = [copy(4 + j, (*chip, c), sibling) for j, chip in enumerate(chips)]
        for j, chip in enumerate(chips):
            copy(1 + j, (*chip, c), me).wait_recv()
            passed[j].start()
        copy(0, sibling, me).wait_recv()
        for j, chip in enumerate(chips):
            copy(4 + j, (*chip, 1 - c), me).wait_recv()
        for cp in first + passed:
            cp.wait_send()
        mine.wait()

    return _pcall(
        body, name=name, out_shape=_sds((N_DEV * m_per, n), block.dtype),
        in_specs=[pl.BlockSpec(memory_space=pltpu.VMEM)], out_specs=pl.BlockSpec(memory_space=pltpu.VMEM),
        scratch=[pltpu.SemaphoreType.DMA((7,)), pltpu.SemaphoreType.DMA((7,)), pltpu.SemaphoreType.DMA],
    )(block)


class _SplitCopies:
    def __init__(self, name, arrays, plan, n_copies):
        self.name, self.plan, self.n = name, plan, len(arrays)
        n = self.n

        def body(*refs):
            send, recv, token = refs[n], refs[n + 1], refs[-1]
            for k, (src, dst, _, peer) in enumerate(plan(refs[:n])):
                pltpu.make_async_remote_copy(src_ref=src, dst_ref=dst, send_sem=send.at[k], recv_sem=recv.at[k],
                                             device_id=peer, device_id_type=MESH).start()
            token[...] = jnp.zeros_like(token)

        def call(wrapped, dep_specs):
            return pl.pallas_call(
                wrapped, name=f"{name}_start",
                out_shape=(pltpu.SemaphoreType.DMA((n_copies,)), pltpu.SemaphoreType.DMA((n_copies,)),
                           *[pltpu.HBM(a.shape, a.dtype) for a in arrays], _sds((8, LANES), F32)),
                in_specs=[HBM] * n + dep_specs,
                out_specs=(SEM, SEM, *[HBM] * n, pl.BlockSpec(memory_space=pltpu.VMEM)),
                input_output_aliases={i: 2 + i for i in range(n)},
                compiler_params=pltpu.CompilerParams(has_side_effects=EFFECT))

        outs = _ordered(call, [pltpu.with_memory_space_constraint(a, pltpu.HBM) for a in arrays], n, body, token=-1,
                        sources=arrays)
        self.send, self.recv, self.arrays = outs[0], outs[1], list(outs[2:2 + n])

    def wait(self):
        n, plan = self.n, self.plan

        def body(*refs):
            send, recv, token = refs[n], refs[n + 1], refs[-1]
            for k, (src, _, landing, peer) in enumerate(plan(refs[:n])):
                cp = pltpu.make_async_remote_copy(src_ref=src, dst_ref=landing, send_sem=send.at[k],
                                                  recv_sem=recv.at[k], device_id=peer, device_id_type=MESH)
                cp.wait_send()
                cp.wait_recv()
            token[...] = jnp.zeros_like(token)

        def call(wrapped, dep_specs):
            return pl.pallas_call(
                wrapped, name=f"{self.name}_wait",
                out_shape=(*[pltpu.HBM(a.shape, a.dtype) for a in self.arrays], _sds((8, LANES), F32)),
                in_specs=[HBM] * n + [SEM, SEM] + dep_specs,
                out_specs=(*[HBM] * n, pl.BlockSpec(memory_space=pltpu.VMEM)),
                input_output_aliases={i: i for i in range(n)},
                compiler_params=pltpu.CompilerParams(has_side_effects=EFFECT))

        return list(_ordered(call, [*self.arrays, self.send, self.recv], n + 2, body, token=-1))[:n]


def _gather_ici(name, gathered):
    def plan(refs):
        x, y, c, chips = _place()
        q = 2 * x + y
        return [(g.at[q, c], g.at[q, c], g.at[2 * px + py, c], (px, py, c)) for g in refs for px, py in chips]

    return _SplitCopies(name, gathered, plan, 3 * len(gathered))


def _gather_d2d(name, gathered):
    def plan(refs):
        x, y, c, chips = _place()
        return [(g.at[2 * px + py, c], g.at[2 * px + py, c], g.at[2 * px + py, 1 - c], (x, y, 1 - c))
                for g in refs for px, py in chips]

    return _SplitCopies(name, gathered, plan, 3 * len(gathered))


def _scatter_sibling(name, grads):
    n = len(grads)

    def plan(refs):
        x, y, c, _ = _place()
        return [(refs[w].at[1 - c], refs[n + w], refs[n + w], (x, y, 1 - c)) for w in range(n)]

    landing = [lax.empty(g.shape[1:], g.dtype) for g in grads]
    return _SplitCopies(name, [*grads, *landing], plan, n)


def _scatter_chips(name, sums):
    n = len(sums)

    def plan(refs):
        x, y, c, chips = _place()
        return [(refs[w].at[2 * px + py], refs[n + w].at[j], refs[n + w].at[j], (px, py, c))
                for w in range(n) for j, (px, py) in enumerate(chips)]

    landing = [lax.empty((3, *s.shape[1:]), s.dtype) for s in sums]
    return _SplitCopies(name, [*sums, *landing], plan, 3 * n)


def _share_final(name, finals):
    def plan(refs):
        x, y, c, _ = _place()
        return [(f.at[c], f.at[c], f.at[1 - c], (x, y, 1 - c)) for f in refs]

    return _SplitCopies(name, finals, plan, len(finals))


def _row_tile(rows, cols, budget_elems=393216):
    best = 8
    for t in range(8, rows + 1, 8):
        if rows % t == 0 and t * cols <= budget_elems:
            best = t
    return best if rows % best == 0 else rows


def _sum_with_sibling(name, grad, recv, c_idx):
    _, _, h, cols = grad.shape
    tr = _row_tile(h, cols)

    def body(s_ref, g_ref, r_ref, p_ref, pb_ref):
        p = g_ref[...] + r_ref[...]
        p_ref[...] = p
        pb_ref[...] = p.astype(BF16)

    blk = pl.BlockSpec((None, tr, cols), lambda k, r, s: (k, r, 0))
    return _pcall(
        body, name=name, out_shape=[_sds((N_CHIPS, h, cols), F32), _sds((N_CHIPS, h, cols), BF16)],
        grid=(N_CHIPS, h // tr), prefetch=1,
        in_specs=[pl.BlockSpec((None, None, tr, cols), lambda k, r, s: (s[0], k, r, 0)), blk],
        out_specs=[blk, blk], vmem_mb=32,
    )(c_idx, grad, recv)


def _sum_chips(name, own, recv, qc_idx):
    _, h, cols = own.shape
    tr = _row_tile(h, cols)

    def body(s_ref, p_ref, t_ref, o_ref):
        o_ref[...] = ((p_ref[...] + t_ref[0].astype(F32)) + t_ref[1].astype(F32)) + t_ref[2].astype(F32)

    return _pcall(
        body, name=name, out_shape=_sds((2, h, cols), F32), grid=(h // tr,), prefetch=1,
        in_specs=[pl.BlockSpec((None, tr, cols), lambda r, s: (s[0], r, 0)),
                  pl.BlockSpec((3, tr, cols), lambda r, s: (0, r, 0))],
        out_specs=pl.BlockSpec((None, tr, cols), lambda r, s: (s[1], r, 0)), vmem_mb=32,
    )(qc_idx, own, recv)


class _ReduceScatter:
    def __init__(self, tag, names, grads, c_idx, qc_idx):
        self.tag, self.names, self.n, self.c_idx, self.qc_idx = tag, names, len(grads), c_idx, qc_idx
        self.copies = _scatter_sibling(f"{tag}_rs_sibling", grads)

    def step2(self):
        n = self.n
        arrs = self.copies.wait()
        sums = [_sum_with_sibling(f"{nm}_sum_sibling", arrs[w], arrs[n + w], self.c_idx)
                for w, nm in enumerate(self.names)]
        self.own = [s[0] for s in sums]
        self.copies = _scatter_chips(f"{self.tag}_rs_chips", [s[1] for s in sums])

    def step3(self):
        n = self.n
        arrs = self.copies.wait()
        finals = [_sum_chips(f"{nm}_sum_chips", self.own[w], arrs[n + w], self.qc_idx)
                  for w, nm in enumerate(self.names)]
        self.copies = _share_final(f"{self.tag}_rs_final", finals)

    def result(self):
        return {nm: f.reshape(2 * f.shape[1], f.shape[2]) for nm, f in zip(self.names, self.copies.wait())}


def _cast_into_gathered(name, w, q_idx):
    rows, cols = w.shape
    h = rows // 2
    tr = _row_tile(h, cols, 1 << 20)
    nr = h // tr

    def body(s_ref, w_ref, o_ref):
        o_ref[...] = w_ref[...].astype(BF16)

    return _pcall(body, name=name, out_shape=_sds((N_CHIPS, 2, h, cols), BF16), grid=(2, nr), prefetch=1,
                  in_specs=[pl.BlockSpec((tr, cols), lambda hf, r, s: (hf * nr + r, 0))],
                  out_specs=pl.BlockSpec((None, None, tr, cols), lambda hf, r, s: (s[0], hf, r, 0)),
                  vmem_mb=32)(q_idx, w)


def _rms(h):
    r = lax.rsqrt(jnp.mean(h * h, axis=-1, keepdims=True) + EPS)
    return r, h * r


def _norm_mod(name, h, g, sc, sh, ts):
    s_len, d = h.shape

    def body(h_ref, g_ref, sc_ref, sh_ref, n_ref):
        _, xhat = _rms(h_ref[...])
        n_ref[...] = ((xhat * g_ref[...]) * (1.0 + sc_ref[...]) + sh_ref[...]).astype(BF16)

    row = pl.BlockSpec((ts, d), lambda i: (i, 0))
    return _pcall(body, name=name, out_shape=_sds((s_len, d), BF16), grid=(s_len // ts,),
                  in_specs=[row, _vec(d), _vec(d), _vec(d)], out_specs=row, vmem_mb=32)(h, g, sc, sh)


def _residual_norm_mod(name, h, f, gate, cmul, g, sc, sh, ts):
    s_len, d = h.shape

    def body(h_ref, f_ref, gt_ref, g_ref, sc_ref, sh_ref, ho_ref, n_ref):
        hn = h_ref[...] + (cmul * gt_ref[...]) * f_ref[...]
        ho_ref[...] = hn
        _, xhat = _rms(hn)
        n_ref[...] = ((xhat * g_ref[...]) * (1.0 + sc_ref[...]) + sh_ref[...]).astype(BF16)

    row = pl.BlockSpec((ts, d), lambda i: (i, 0))
    return _pcall(body, name=name, out_shape=[_sds((s_len, d), F32), _sds((s_len, d), BF16)],
                  grid=(s_len // ts,), in_specs=[row, row, _vec(d), _vec(d), _vec(d), _vec(d)],
                  out_specs=[row, row], vmem_mb=32)(h, f, gate, g, sc, sh)


def _final_loss(name, h, f, tgt, gate, cmul, g, ts):
    s_len, d = h.shape

    def body(h_ref, f_ref, t_ref, gt_ref, g_ref, dh_ref, df_ref, dg_ref, dgt_ref, loss_ref):
        i = pl.program_id(0)
        fv = f_ref[...]
        coef = cmul * gt_ref[...]
        hn = h_ref[...] + coef * fv
        r, xhat = _rms(hn)
        err = xhat * g_ref[...] - t_ref[...]
        _acc_rows(loss_ref, (0.5 / d) * (err * err), i)
        dy = err * (1.0 / d)
        _acc_rows(dg_ref, dy * xhat, i)
        dxhat = dy * g_ref[...]
        dh = r * (dxhat - xhat * jnp.mean(dxhat * xhat, axis=-1, keepdims=True))
        dh_ref[...] = dh
        _acc_rows(dgt_ref, cmul * (dh * fv), i)
        df_ref[...] = (coef * dh).astype(BF16)

    row = pl.BlockSpec((ts, d), lambda i: (i, 0))
    return _pcall(body, name=name,
                  out_shape=[_sds((s_len, d), F32), _sds((s_len, d), BF16)] + [_sds((1, d), F32)] * 3,
                  grid=(s_len // ts,), in_specs=[row, row, row, _vec(d), _vec(d)],
                  out_specs=[row, row, _vec(d), _vec(d), _vec(d)], vmem_mb=40)(h, f, tgt, gate, g)


def _norm_mod_bwd(name, h, dn, dh_next, g, sc, ts, prev=None):
    s_len, d = h.shape
    has_prev = prev is not None
    cmul = prev[2] if has_prev else None

    def body(*refs):
        if has_prev:
            h_ref, dn_ref, dhn_ref, f_ref, g_ref, sc_ref, gt_ref, dh_ref, df_ref, dsh_ref, dsc_ref, dg_ref, dgt_ref = refs
        else:
            h_ref, dn_ref, dhn_ref, g_ref, sc_ref, dh_ref, dsh_ref, dsc_ref, dg_ref = refs
        i = pl.program_id(0)
        r, xhat = _rms(h_ref[...])
        dn_v = dn_ref[...]
        gv = g_ref[...]
        _acc_rows(dsh_ref, dn_v, i)
        _acc_rows(dsc_ref, dn_v * (xhat * gv), i)
        dnrm = dn_v * (1.0 + sc_ref[...])
        _acc_rows(dg_ref, dnrm * xhat, i)
        dxhat = dnrm * gv
        dh = dhn_ref[...] + r * (dxhat - xhat * jnp.mean(dxhat * xhat, axis=-1, keepdims=True))
        dh_ref[...] = dh
        if has_prev:
            _acc_rows(dgt_ref, cmul * (dh * f_ref[...]), i)
            df_ref[...] = ((cmul * gt_ref[...]) * dh).astype(BF16)

    row = pl.BlockSpec((ts, d), lambda i: (i, 0))
    if has_prev:
        ins, in_specs = [h, dn, dh_next, prev[0], g, sc, prev[1]], [row, row, row, row, _vec(d), _vec(d), _vec(d)]
        out_shape = [_sds((s_len, d), F32), _sds((s_len, d), BF16)] + [_sds((1, d), F32)] * 4
        out_specs = [row, row] + [_vec(d)] * 4
    else:
        ins, in_specs = [h, dn, dh_next, g, sc], [row, row, row, _vec(d), _vec(d)]
        out_shape = [_sds((s_len, d), F32)] + [_sds((1, d), F32)] * 3
        out_specs = [row] + [_vec(d)] * 3
    return _pcall(body, name=name, out_shape=out_shape, grid=(s_len // ts,), in_specs=in_specs,
                  out_specs=out_specs, vmem_mb=40)(*ins)


def _cols(ref, lo, hi, npc):
    parts = []
    while lo < hi:
        q, o = divmod(lo, npc)
        n = min(hi - lo, npc - o)
        parts.append(ref[q, :, o:o + n])
        lo += n
    return parts[0] if len(parts) == 1 else jnp.concatenate(parts, axis=-1)


def _store_cols(ref, lo, val, npc):
    off, width = 0, val.shape[-1]
    while off < width:
        q, o = divmod(lo + off, npc)
        n = min(width - off, npc - o)
        ref[q, :, o:o + n] = val[:, off:off + n]
        off += n


def _chips_covering(cols, npc):
    return -(-cols // npc)


def _conv_ln(a0s_ref, cw_ref, cb_ref, lg_ref, lb_ref, ts):
    a1 = cb_ref[...] + cw_ref[0:1, :] * a0s_ref[pl.ds(HALO - CONV_K + 1, ts), :]
    for k in range(1, CONV_K):
        a1 = a1 + cw_ref[k:k + 1, :] * a0s_ref[pl.ds(HALO - CONV_K + 1 + k, ts), :]
    mu = jnp.mean(a1, axis=-1, keepdims=True)
    ctr = a1 - mu
    rstd = lax.rsqrt(jnp.mean(ctr * ctr, axis=-1, keepdims=True) + EPS)
    xh = ctr * rstd
    return xh, rstd, xh * lg_ref[...] + lb_ref[...]


def _stage_glu(p_ref, ph_ref, a0s_ref, i, wc, npc, ts):
    a0 = _cols(p_ref, 0, wc, npc) * _sigmoid(_cols(p_ref, wc, 2 * wc, npc))
    a0h = _cols(ph_ref, 0, wc, npc) * _sigmoid(_cols(ph_ref, wc, 2 * wc, npc))
    a0s_ref[0:HALO, :] = jnp.where(i > 0, a0h, 0.0)
    a0s_ref[HALO:HALO + ts, :] = a0


def _mixer_mid(name, proj, cw, cb, lg, lb, wc, wp, ts):
    _, s_len, npc = proj.shape
    nq = _chips_covering(2 * wc + wp, npc)
    gi = wp // len(POOL_WINDOWS)
    hb = ts // HALO

    def body(p_ref, ph_ref, cw_ref, cb_ref, lg_ref, lb_ref, a3_ref, mx_ref, a0s_ref, vs_ref):
        i = pl.program_id(0)
        _stage_glu(p_ref, ph_ref, a0s_ref, i, wc, npc, ts)
        vs_ref[0:HALO, :] = jnp.where(i > 0, _cols(ph_ref, 2 * wc, 2 * wc + wp, npc), 0.0)
        vs_ref[HALO:HALO + ts, :] = _cols(p_ref, 2 * wc, 2 * wc + wp, npc)
        _, _, a2 = _conv_ln(a0s_ref, cw_ref, cb_ref, lg_ref, lb_ref, ts)
        a3_ref[...] = (a2 * _sigmoid(a2)).astype(BF16)
        t_abs = i * ts + lax.broadcasted_iota(jnp.int32, (ts, 1), 0)
        for g, win in enumerate(POOL_WINDOWS):
            cs = slice(g * gi, (g + 1) * gi)
            acc = vs_ref[pl.ds(HALO, ts), cs]
            for dlt in range(1, win):
                acc = acc + vs_ref[pl.ds(HALO - dlt, ts), cs]
            cnt = jnp.minimum(t_abs + 1, win).astype(F32)
            mx_ref[:, cs] = (acc / cnt - vs_ref[pl.ds(HALO, ts), cs]).astype(BF16)

    return _pcall(
        body, name=name, out_shape=[_sds((s_len, wc), BF16), _sds((s_len, wp), BF16)], grid=(s_len // ts,),
        in_specs=[pl.BlockSpec((nq, ts, npc), lambda i: (0, i, 0)),
                  pl.BlockSpec((nq, HALO, npc), lambda i: (0, jnp.maximum(i * hb - 1, 0), 0)),
                  pl.BlockSpec((HALO, wc), lambda i: (0, 0)), _vec(wc), _vec(wc), _vec(wc)],
        out_specs=[pl.BlockSpec((ts, wc), lambda i: (i, 0)), pl.BlockSpec((ts, wp), lambda i: (i, 0))],
        scratch=[pltpu.VMEM((HALO + ts, wc), F32), pltpu.VMEM((HALO + ts, wp), F32)], vmem_mb=48,
    )(proj, proj, cw, cb, lg, lb)


def _gates_fwd(name, proj, ya, yb, b_a, b_b, ls, wc, wp, ts):
    _, s_len, npc = proj.shape
    d = ya.shape[1]
    g0 = 2 * wc + wp

    def body(p_ref, ya_ref, yb_ref, ba_ref, bb_ref, ls_ref, z_ref):
        ga = _sigmoid(_cols(p_ref, g0, g0 + d, npc))
        gb = _sigmoid(_cols(p_ref, g0 + d, g0 + 2 * d, npc))
        z = ga * (ya_ref[...] + ba_ref[...]) + gb * ((yb_ref[...] + bb_ref[...]) * ls_ref[...])
        z_ref[...] = z.astype(BF16)

    row = pl.BlockSpec((ts, d), lambda i: (i, 0))
    return _pcall(body, name=name, out_shape=_sds((s_len, d), BF16), grid=(s_len // ts,),
                  in_specs=[pl.BlockSpec((N_CHIPS, ts, npc), lambda i: (0, i, 0)), row, row, _vec(d), _vec(d), _vec(d)],
                  out_specs=row, vmem_mb=48)(proj, ya, yb, b_a, b_b, ls)


def _gates_bwd(name, proj, dz, ya, yb, b_a, b_b, ls, wc, wp, ts):
    _, s_len, npc = proj.shape
    d = ya.shape[1]
    g0 = 2 * wc + wp

    def body(p_ref, dz_ref, ya_ref, yb_ref, ba_ref, bb_ref, ls_ref, dya_ref, dyb_ref, dgt_ref, dba_ref, dls_ref,
             dbb_ref):
        i = pl.program_id(0)
        ga = _sigmoid(_cols(p_ref, g0, g0 + d, npc))
        gb = _sigmoid(_cols(p_ref, g0 + d, g0 + 2 * d, npc))
        dz_v = dz_ref[...]
        y_a = ya_ref[...] + ba_ref[...]
        y_b0 = yb_ref[...] + bb_ref[...]
        ls_v = ls_ref[...]
        dya = dz_v * ga
        dya_ref[...] = dya.astype(BF16)
        _acc_rows(dba_ref, dya, i)
        t = dz_v * gb
        _acc_rows(dls_ref, t * y_b0, i)
        dyb = t * ls_v
        dyb_ref[...] = dyb.astype(BF16)
        _acc_rows(dbb_ref, dyb, i)
        dgt_ref[:, 0:d] = (dz_v * y_a * ga * (1.0 - ga)).astype(BF16)
        dgt_ref[:, d:2 * d] = (dz_v * (y_b0 * ls_v) * gb * (1.0 - gb)).astype(BF16)

    row = pl.BlockSpec((ts, d), lambda i: (i, 0))
    return _pcall(
        body, name=name,
        out_shape=[_sds((s_len, d), BF16), _sds((s_len, d), BF16), _sds((s_len, 2 * d), BF16)] + [_sds((1, d), F32)] * 3,
        grid=(s_len // ts,),
        in_specs=[pl.BlockSpec((N_CHIPS, ts, npc), lambda i: (0, i, 0)), row, row, row, _vec(d), _vec(d), _vec(d)],
        out_specs=[row, row, pl.BlockSpec((ts, 2 * d), lambda i: (i, 0))] + [_vec(d)] * 3, vmem_mb=48,
    )(proj, dz, ya, yb, b_a, b_b, ls)


def _conv_branch_bwd(name, proj, da3, cw, cb, lg, lb, wc, wp, ts):
    _, s_len, npc = proj.shape
    nq = _chips_covering(2 * wc, npc)
    hb = ts // HALO

    def body(p_ref, ph_ref, da3_ref, cw_ref, cb_ref, lg_ref, lb_ref, da1_ref, dlg_ref, dlb_ref, dcb_ref, dcw_ref,
             a0s_ref):
        i = pl.program_id(0)
        _stage_glu(p_ref, ph_ref, a0s_ref, i, wc, npc, ts)
        xh, rstd, a2 = _conv_ln(a0s_ref, cw_ref, cb_ref, lg_ref, lb_ref, ts)
        sig = _sigmoid(a2)
        da2 = da3_ref[...] * (sig * (1.0 + a2 * (1.0 - sig)))
        _acc_rows(dlg_ref, da2 * xh, i)
        _acc_rows(dlb_ref, da2, i)
        dxh = da2 * lg_ref[...]
        da1 = rstd * (dxh - jnp.mean(dxh, axis=-1, keepdims=True)
                      - xh * jnp.mean(dxh * xh, axis=-1, keepdims=True))
        da1_ref[...] = da1
        _acc_rows(dcb_ref, da1, i)

        @pl.when(i == 0)
        def _():
            dcw_ref[...] = jnp.zeros_like(dcw_ref)

        for k in range(CONV_K):
            dcw_ref[k:k + 1, :] += jnp.sum(da1 * a0s_ref[pl.ds(HALO - CONV_K + 1 + k, ts), :], axis=0,
                                           keepdims=True)

    return _pcall(
        body, name=name,
        out_shape=[_sds((s_len, wc), F32)] + [_sds((1, wc), F32)] * 3 + [_sds((HALO, wc), F32)],
        grid=(s_len // ts,),
        in_specs=[pl.BlockSpec((nq, ts, npc), lambda i: (0, i, 0)),
                  pl.BlockSpec((nq, HALO, npc), lambda i: (0, jnp.maximum(i * hb - 1, 0), 0)),
                  pl.BlockSpec((ts, wc), lambda i: (i, 0)),
                  pl.BlockSpec((HALO, wc), lambda i: (0, 0)), _vec(wc), _vec(wc), _vec(wc)],
        out_specs=[pl.BlockSpec((ts, wc), lambda i: (i, 0)), _vec(wc), _vec(wc), _vec(wc),
                   pl.BlockSpec((HALO, wc), lambda i: (0, 0))],
        scratch=[pltpu.VMEM((HALO + ts, wc), F32)], vmem_mb=48,
    )(proj, proj, da3, cw, cb, lg, lb)


def _mixer_in_bwd(name, proj, da1, dmixed, dgates, cw, wc, wp, ts):
    _, s_len, npc = proj.shape
    nq = _chips_covering(2 * wc, npc)
    gi = wp // len(POOL_WINDOWS)
    hb = ts // HALO
    n_tiles = s_len // ts
    last_hb = s_len // HALO - 1
    d2 = dgates.shape[1]

    def body(p_ref, d1_ref, d1n_ref, dm_ref, dmn_ref, dgt_ref, cw_ref, o_ref, d1s_ref, es_ref):
        i = pl.program_id(0)
        more = i < n_tiles - 1
        d1s_ref[0:ts, :] = d1_ref[...]
        d1s_ref[ts:ts + HALO, :] = jnp.where(more, d1n_ref[...], 0.0)
        da0 = cw_ref[0:1, :] * d1s_ref[pl.ds(CONV_K - 1, ts), :]
        for k in range(1, CONV_K):
            da0 = da0 + cw_ref[k:k + 1, :] * d1s_ref[pl.ds(CONV_K - 1 - k, ts), :]
        glu_a = _cols(p_ref, 0, wc, npc)
        sig = _sigmoid(_cols(p_ref, wc, 2 * wc, npc))
        _store_cols(o_ref, 0, (da0 * sig).astype(BF16), npc)
        _store_cols(o_ref, wc, (da0 * glu_a * sig * (1.0 - sig)).astype(BF16), npc)

        t_abs = i * ts + lax.broadcasted_iota(jnp.int32, (ts + HALO, 1), 0)
        dm = dm_ref[...]
        dm_ext = jnp.concatenate([dm, jnp.where(more, dmn_ref[...], 0.0)], axis=0)
        for g, win in enumerate(POOL_WINDOWS):
            cs = slice(g * gi, (g + 1) * gi)
            cnt = jnp.minimum(t_abs + 1, win).astype(F32)
            es_ref[:, cs] = dm_ext[:, cs] / cnt
        parts = []
        for g, win in enumerate(POOL_WINDOWS):
            cs = slice(g * gi, (g + 1) * gi)
            acc = es_ref[pl.ds(0, ts), cs]
            for dlt in range(1, win):
                acc = acc + es_ref[pl.ds(dlt, ts), cs]
            parts.append(acc - dm[:, cs])
        _store_cols(o_ref, 2 * wc, jnp.concatenate(parts, axis=-1).astype(BF16), npc)
        _store_cols(o_ref, 2 * wc + wp, dgt_ref[...], npc)

    nxt = lambda i: (jnp.minimum((i + 1) * hb, last_hb), 0)
    return _pcall(
        body, name=name, out_shape=_sds((N_CHIPS, s_len, npc), BF16), grid=(n_tiles,),
        in_specs=[pl.BlockSpec((nq, ts, npc), lambda i: (0, i, 0)),
                  pl.BlockSpec((ts, wc), lambda i: (i, 0)), pl.BlockSpec((HALO, wc), nxt),
                  pl.BlockSpec((ts, wp), lambda i: (i, 0)), pl.BlockSpec((HALO, wp), nxt),
                  pl.BlockSpec((ts, d2), lambda i: (i, 0)),
                  pl.BlockSpec((HALO, wc), lambda i: (0, 0))],
        out_specs=pl.BlockSpec((N_CHIPS, ts, npc), lambda i: (0, i, 0)),
        scratch=[pltpu.VMEM((ts + HALO, wc), F32), pltpu.VMEM((ts + HALO, wp), F32)], vmem_mb=48,
    )(proj, da1, da1, dmixed, dmixed, dgates, cw)


def _ada_fwd(name, c_all, w, b):
    d, cols = w.shape
    tn = 512 if cols % 512 == 0 else cols

    def body(c_ref, w_ref, b_ref, o_ref):
        cv = c_ref[...]
        sc = (cv * _sigmoid(cv)).astype(BF16)
        o_ref[...] = jnp.dot(sc, w_ref[...].astype(BF16), preferred_element_type=F32) + b_ref[...]

    return _pcall(body, name=name, out_shape=_sds((N_DEV, cols), F32), grid=(cols // tn,),
                  in_specs=[pl.BlockSpec((N_DEV, d), lambda j: (0, 0)), pl.BlockSpec((d, tn), lambda j: (0, j)),
                            pl.BlockSpec((1, tn), lambda j: (0, j))],
                  out_specs=pl.BlockSpec((N_DEV, tn), lambda j: (0, j)), vmem_mb=32)(c_all, w, b)


def _adam_math(w, g, m, v):
    m_new = ADAM_B1 * m + (1.0 - ADAM_B1) * g
    v_new = ADAM_B2 * v + (1.0 - ADAM_B2) * (g * g)
    m_hat = m_new / (1.0 - ADAM_B1 ** ADAM_STEP)
    v_hat = v_new / (1.0 - ADAM_B2 ** ADAM_STEP)
    delta = -ADAM_LR * (m_hat / (jnp.sqrt(v_hat) + ADAM_EPS) + ADAM_WD * w)
    return delta, m_new, v_new


def _adamw(name, w, g, m, v):
    rows, cols = w.shape
    tr = _row_tile(rows, cols, 262144)

    def body(w_ref, g_ref, m_ref, v_ref, go_ref, d_ref, mo_ref, vo_ref):
        g = g_ref[...]
        go_ref[...] = g
        d_ref[...], mo_ref[...], vo_ref[...] = _adam_math(w_ref[...], g, m_ref[...], v_ref[...])

    spec = pl.BlockSpec((tr, cols), lambda i: (i, 0))
    return _pcall(body, name=name, out_shape=[_sds(w.shape, F32)] * 4, grid=(rows // tr,), in_specs=[spec] * 4,
                  out_specs=[spec] * 4, vmem_mb=40)(w, g, m, v)


def _ada_grad_adamw(name, c_t, d_ada, w, m, v):
    rows, cols = w.shape
    tr = _tile(rows, 256)
    tc = _tile(cols, 1536) if cols % 1536 == 0 else cols

    def body(c_ref, da_ref, w_ref, m_ref, v_ref, g_ref, d_ref, mo_ref, vo_ref):
        cv = c_ref[...]
        sc = cv * _sigmoid(cv)
        g = sc[:, 0:1] * da_ref[0:1, :]
        for b in range(1, N_DEV):
            g = g + sc[:, b:b + 1] * da_ref[b:b + 1, :]
        g_ref[...] = g
        d_ref[...], mo_ref[...], vo_ref[...] = _adam_math(w_ref[...], g, m_ref[...], v_ref[...])

    spec = pl.BlockSpec((tr, tc), lambda i, j: (i, j))
    return _pcall(body, name=name, out_shape=[_sds(w.shape, F32)] * 4, grid=(rows // tr, cols // tc),
                  in_specs=[pl.BlockSpec((tr, N_DEV), lambda i, j: (i, 0)),
                            pl.BlockSpec((N_DEV, tc), lambda i, j: (0, j)), spec, spec, spec],
                  out_specs=[spec] * 4, vmem_mb=40)(c_t, d_ada, w, m, v)


def _sum_devices(name, gathered, m_per):
    n = gathered.shape[1]

    def body(g_ref, o_ref):
        acc = g_ref[0:m_per, :]
        for dev in range(1, N_DEV):
            acc = acc + g_ref[dev * m_per:(dev + 1) * m_per, :]
        o_ref[...] = acc

    return _pcall(body, name=name, out_shape=_sds((m_per, n), F32),
                  in_specs=[pl.BlockSpec(memory_space=pltpu.VMEM)],
                  out_specs=pl.BlockSpec(memory_space=pltpu.VMEM))(gathered)


def _ffn_fwd(tag, n, w_in_g, w_out2d, dims, after_swiglu=lambda: None):
    s_len, d, f_dim = dims["S"], dims["D"], dims["F"]
    p = f_dim // 2
    tm, tn = _tile(s_len, 1024), math.gcd(p, 256)
    nbp = p // tn

    def ep(accs, ex, outs):
        hh, uu = accs
        outs[0][0] = hh.astype(BF16)
        outs[0][1] = uu.astype(BF16)
        outs[1][...] = (hh * _sigmoid(hh) * uu).astype(BF16)

    hu, act = _matmul(
        f"{tag}_swiglu", n, [w_in_g, w_in_g], mode="nn", grid=(s_len // tm, f_dim // tn, 1),
        a_spec=pl.BlockSpec((tm, d), lambda i, j, k: (i, 0)),
        b_specs=[pl.BlockSpec((None, d, tn), lambda i, j, k: (j // nbp, 0, j % nbp)),
                 pl.BlockSpec((None, d, tn), lambda i, j, k: (2 + j // nbp, 0, j % nbp))],
        out_shape=[_sds((2, s_len, f_dim), BF16), _sds((s_len, f_dim), BF16)],
        out_specs=[pl.BlockSpec((2, tm, tn), lambda i, j, k: (0, i, j)),
                   pl.BlockSpec((tm, tn), lambda i, j, k: (i, j))],
        acc_shape=(tm, tn), epilogue=ep)
    after_swiglu()
    tf = f_dim // 4
    tn2 = _tile(d, 1024)
    f = _matmul(
        f"{tag}_down", act, [w_out2d], mode="nn", grid=(s_len // tm, d // tn2, 4),
        a_spec=pl.BlockSpec((tm, tf), lambda i, j, k: (i, k)),
        b_specs=[pl.BlockSpec((tf, tn2), lambda i, j, k: (k, j))],
        out_shape=_sds((s_len, d), F32), out_specs=pl.BlockSpec((tm, tn2), lambda i, j, k: (i, j)),
        acc_shape=(tm, tn2), epilogue=_ep_store(F32))
    return hu, act, f


def _ffn_bwd(tag, n, hu, act, df, w_in_g, w_out2d, dims, after_dw_out, after_dw_in):
    s_len, d, f_dim = dims["S"], dims["D"], dims["F"]
    tf = f_dim // 4
    tk = _tile(s_len, 512)
    tn = _tile(d, 1024)
    g_out = _matmul(
        f"{tag}_dw_out", act, [df], mode="tn", grid=(4, d // tn, s_len // tk),
        a_spec=pl.BlockSpec((tk, tf), lambda i, j, k: (k, i)),
        b_specs=[pl.BlockSpec((tk, tn), lambda i, j, k: (k, j))],
        out_shape=_sds((2, 4, tf // 2, d), F32),
        out_specs=pl.BlockSpec((2, None, tf // 2, tn), lambda i, j, k: (0, i, 0, j)),
        acc_shape=(tf, tn), epilogue=_ep_halves(tf // 2))
    after_dw_out(g_out)

    def ep_dhu(accs, ex, outs):
        da = accs[0]
        hh, uu = ex[0][0].astype(F32), ex[0][1].astype(F32)
        sig = _sigmoid(hh)
        outs[0][0] = (da * uu * (sig * (1.0 + hh * (1.0 - sig)))).astype(BF16)
        outs[0][1] = (da * (hh * sig)).astype(BF16)

    tm = _tile(s_len, 512)
    hu_spec = pl.BlockSpec((2, tm, tf), lambda i, j, k: (0, i, j))
    dhu = _matmul(
        f"{tag}_dhu", df, [w_out2d], mode="nt", grid=(s_len // tm, 4, 1),
        a_spec=pl.BlockSpec((tm, d), lambda i, j, k: (i, 0)),
        b_specs=[pl.BlockSpec((tf, d), lambda i, j, k: (j, 0))],
        extras=[hu], extra_specs=[hu_spec],
        out_shape=_sds((2, s_len, f_dim), BF16), out_specs=hu_spec, acc_shape=(tm, tf), epilogue=ep_dhu)

    hd = d // 2
    g_in = _matmul(
        f"{tag}_dw_in", n, [dhu], mode="tn", grid=(2, 8, s_len // tk),
        a_spec=pl.BlockSpec((tk, hd), lambda i, j, k: (k, i)),
        b_specs=[pl.BlockSpec((None, tk, tf), lambda i, j, k: (j // 4, k, j % 4))],
        out_shape=_sds((2, 4, hd, f_dim // 2), F32),
        out_specs=pl.BlockSpec((None, None, hd, tf), lambda i, j, k: (i, j // 2, 0, j % 2)),
        acc_shape=(hd, tf), epilogue=_ep_store(F32))
    after_dw_in(g_in)

    tm2 = _tile(s_len, 1024)
    dn = _matmul(
        f"{tag}_dn", dhu, [w_in_g], mode="nt", grid=(s_len // tm2, d // tn, 8),
        a_spec=pl.BlockSpec((None, tm2, tf), lambda i, j, k: (k // 4, i, k % 4)),
        b_specs=[pl.BlockSpec((None, tn, tf), lambda i, j, k: (k // 2, j, k % 2))],
        out_shape=_sds((s_len, d), F32), out_specs=pl.BlockSpec((tm2, tn), lambda i, j, k: (i, j)),
        acc_shape=(tm2, tn), epilogue=_ep_store(F32))
    return dn


def kernel(x, c, w_ada, b_ada, g_ffn1, w1_in, w1_out, g_mix, w_in, conv_w, conv_b, ln_a_g, ln_a_b, w_a_out, b_a_out, w_b_group, b_b_group, ls_b, w_out, g_ffn2, w2_in, w2_out, g_final, loss_target, m_w_ada, m_b_ada, m_g_ffn1, m_w1_in, m_w1_out, m_g_mix, m_w_in, m_conv_w, m_conv_b, m_ln_a_g, m_ln_a_b, m_w_a_out, m_b_a_out, m_w_b_group, m_b_b_group, m_ls_b, m_w_out, m_g_ffn2, m_w2_in, m_w2_out, m_g_final, v_w_ada, v_b_ada, v_g_ffn1, v_w1_in, v_w1_out, v_g_mix, v_w_in, v_conv_w, v_conv_b, v_ln_a_g, v_ln_a_b, v_w_a_out, v_b_a_out, v_w_b_group, v_b_b_group, v_ls_b, v_w_out, v_g_ffn2, v_w2_in, v_w2_out, v_g_final):
    weights = dict(w_ada=w_ada, b_ada=b_ada, g_ffn1=g_ffn1, w1_in=w1_in, w1_out=w1_out, g_mix=g_mix, w_in=w_in,
                   conv_w=conv_w, conv_b=conv_b, ln_a_g=ln_a_g, ln_a_b=ln_a_b, w_a_out=w_a_out, b_a_out=b_a_out,
                   w_b_group=w_b_group, b_b_group=b_b_group, ls_b=ls_b, w_out=w_out, g_ffn2=g_ffn2, w2_in=w2_in,
                   w2_out=w2_out, g_final=g_final)
    mom1 = dict(w_ada=m_w_ada, b_ada=m_b_ada, g_ffn1=m_g_ffn1, w1_in=m_w1_in, w1_out=m_w1_out, g_mix=m_g_mix,
                w_in=m_w_in, conv_w=m_conv_w, conv_b=m_conv_b, ln_a_g=m_ln_a_g, ln_a_b=m_ln_a_b, w_a_out=m_w_a_out,
                b_a_out=m_b_a_out, w_b_group=m_w_b_group, b_b_group=m_b_b_group, ls_b=m_ls_b, w_out=m_w_out,
                g_ffn2=m_g_ffn2, w2_in=m_w2_in, w2_out=m_w2_out, g_final=m_g_final)
    mom2 = dict(w_ada=v_w_ada, b_ada=v_b_ada, g_ffn1=v_g_ffn1, w1_in=v_w1_in, w1_out=v_w1_out, g_mix=v_g_mix,
                w_in=v_w_in, conv_w=v_conv_w, conv_b=v_conv_b, ln_a_g=v_ln_a_g, ln_a_b=v_ln_a_b, w_a_out=v_w_a_out,
                b_a_out=v_b_a_out, w_b_group=v_w_b_group, b_b_group=v_b_b_group, ls_b=v_ls_b, w_out=v_w_out,
                g_ffn2=v_g_ffn2, w2_in=v_w2_in, w2_out=v_w2_out, g_final=v_g_final)
    order = list(weights)

    s_len, d = x.shape[1], x.shape[2]
    f_dim = w1_out.shape[0] * N_CHIPS
    wc = conv_w.shape[1] * N_CHIPS
    wp = w_b_group.shape[0] * w_b_group.shape[1]
    n_groups, gi, goq = w_b_group.shape
    npc = w_in.shape[1]
    ada_c = w_ada.shape[1]
    dims = dict(S=s_len, D=d, F=f_dim)
    ts = _tile(s_len, 256)

    xi, yi, ci = lax.axis_index("x"), lax.axis_index("y"), lax.axis_index("c")
    q = 2 * xi + yi
    dev = 2 * q + ci
    q_idx = jnp.reshape(q, (1,)).astype(jnp.int32)
    c_idx = jnp.reshape(ci, (1,)).astype(jnp.int32)
    qc_idx = jnp.stack([q, ci]).astype(jnp.int32)
    _PREVIOUS.clear()

    cwq = conv_w.shape[1]
    pack0 = jnp.concatenate([c.reshape(-1), conv_w.reshape(-1), b_b_group.reshape(-1)])
    n0 = -(-pack0.shape[0] // (8 * LANES)) * LANES
    pack0 = jnp.pad(pack0, (0, 8 * n0 - pack0.shape[0])).reshape(8, n0)
    g0 = _allgather_small("gather_small_in", pack0).reshape(N_DEV, 8 * n0)
    c_all = g0[:, :d]
    south = g0[0::2]
    cw_full = jnp.concatenate([south[k, d:d + CONV_K * cwq].reshape(CONV_K, cwq) for k in range(N_CHIPS)], axis=1)
    cw_pad = jnp.pad(cw_full, ((0, HALO - CONV_K), (0, 0)))
    o_bb = d + CONV_K * cwq
    bb_full = jnp.concatenate([south[k, o_bb:o_bb + n_groups * goq].reshape(n_groups, goq) for k in range(N_CHIPS)],
                              axis=1).reshape(1, d)

    b_ada_mine = lax.dynamic_slice(b_ada, (q * ada_c,), (ada_c,)).reshape(1, ada_c)
    ada_piece = _ada_fwd("ada_fwd", c_all, w_ada, b_ada_mine)
    g1 = _allgather_small("gather_ada", ada_piece).reshape(N_DEV, N_DEV, ada_c)
    ada_rows = lax.dynamic_index_in_dim(g1[0::2], dev, axis=1, keepdims=False)
    ada = ada_rows.reshape(3, 3, 1, d)
    (sh1, sc1, gt1), (sh2, sc2, gt2), (sh3, sc3, gt3) = [[ada[i, j] for j in range(3)] for i in range(3)]

    row = lambda vct: vct.reshape(1, -1)
    g1v, gmv, g2v, gfv = row(g_ffn1), row(g_mix), row(g_ffn2), row(g_final)

    as2d = lambda a: a.reshape(-1, a.shape[-1])
    groups = dict(ffn1=["w1_in", "w1_out"], mix=["w_in", "w_a_out", "w_b_group", "w_out"], ffn2=["w2_in", "w2_out"])
    big = [nm for grp in groups.values() for nm in grp]
    ici = {}
    for grp, names in groups.items():
        ici[grp] = _gather_ici(f"gather_{grp}_ici",
                               [_cast_into_gathered(f"cast_{nm}", as2d(weights[nm]), q_idx) for nm in names])

    def arrived(grp):
        return _gather_d2d(f"gather_{grp}_d2d", ici[grp].wait())

    def gathered(fwd, grp):
        return {nm: g.reshape(N_CHIPS, 2 * g.shape[2], g.shape[3]) for nm, g in zip(groups[grp], fwd.wait())}

    x2 = x[0]
    tgt = loss_target[0]

    n1 = _norm_mod("ffn1_norm", x2, g1v, sc1, sh1, ts)
    wts = gathered(arrived("ffn1"), "ffn1")
    w1_in_g, w1_out_2d = wts["w1_in"], wts["w1_out"].reshape(f_dim, d)
    fwd = {}
    hu1, act1, f1 = _ffn_fwd("ffn1", n1, w1_in_g, w1_out_2d, dims,
                             after_swiglu=lambda: fwd.update(mix=arrived("mix")))
    h1, n2 = _residual_norm_mod("mix_norm", x2, f1, gt1, 0.5, gmv, sc2, sh2, ts)
    wts = gathered(fwd["mix"], "mix")
    w_in_g, w_out_2d = wts["w_in"], wts["w_out"].reshape(d, d)
    w_a_g = wts["w_a_out"]
    w_b_g = wts["w_b_group"]

    tm = _tile(s_len, 1024)
    tnp = npc // 2
    proj = _matmul(
        "mix_proj", n2, [w_in_g], mode="nn", grid=(s_len // tm, 8, 1),
        a_spec=pl.BlockSpec((tm, d), lambda i, j, k: (i, 0)),
        b_specs=[pl.BlockSpec((None, d, tnp), lambda i, j, k: (j // 2, 0, j % 2))],
        out_shape=_sds((N_CHIPS, s_len, npc), F32),
        out_specs=pl.BlockSpec((None, tm, tnp), lambda i, j, k: (j // 2, i, j % 2)),
        acc_shape=(tm, tnp), epilogue=_ep_store(F32))
    fwd["ffn2"] = arrived("ffn2")
    cbv, lgv, lbv = row(conv_b), row(ln_a_g), row(ln_a_b)
    a3, mixed = _mixer_mid("mix_mid", proj, cw_pad, cbv, lgv, lbv, wc, wp, ts)
    dq = d // N_CHIPS
    ya = _matmul(
        "mix_ya", a3, [w_a_g], mode="nn", grid=(s_len // tm, N_CHIPS, 1),
        a_spec=pl.BlockSpec((tm, wc), lambda i, j, k: (i, 0)),
        b_specs=[pl.BlockSpec((None, wc, dq), lambda i, j, k: (j, 0, 0))],
        out_shape=_sds((s_len, d), F32), out_specs=pl.BlockSpec((tm, dq), lambda i, j, k: (i, j)),
        acc_shape=(tm, dq), epilogue=_ep_store(F32))
    yb = _matmul(
        "mix_yb", mixed, [w_b_g], mode="nn", grid=(s_len // tm, n_groups * N_CHIPS, 1),
        a_spec=pl.BlockSpec((tm, gi), lambda i, j, k: (i, j // N_CHIPS)),
        b_specs=[pl.BlockSpec((None, gi, goq), lambda i, j, k: (j % N_CHIPS, j // N_CHIPS, 0))],
        out_shape=_sds((s_len, d), F32), out_specs=pl.BlockSpec((tm, goq), lambda i, j, k: (i, j)),
        acc_shape=(tm, goq), epilogue=_ep_store(F32))
    bav, lsv = row(b_a_out), row(ls_b)
    z = _gates_fwd("mix_gates", proj, ya, yb, bav, bb_full, lsv, wc, wp, ts)
    tn = _tile(d, 1024)
    mix = _matmul(
        "mix_out", z, [w_out_2d], mode="nn", grid=(s_len // tm, d // tn, 1),
        a_spec=pl.BlockSpec((tm, d), lambda i, j, k: (i, 0)),
        b_specs=[pl.BlockSpec((d, tn), lambda i, j, k: (0, j))],
        out_shape=_sds((s_len, d), F32), out_specs=pl.BlockSpec((tm, tn), lambda i, j, k: (i, j)),
        acc_shape=(tm, tn), epilogue=_ep_store(F32))
    h2, n3 = _residual_norm_mod("ffn2_norm", h1, mix, gt2, 1.0, g2v, sc3, sh3, ts)
    wts = gathered(fwd["ffn2"], "ffn2")
    w2_in_g, w2_out_2d = wts["w2_in"], wts["w2_out"].reshape(f_dim, d)
    hu2, act2, f3 = _ffn_fwd("ffn2", n3, w2_in_g, w2_out_2d, dims)

    dh3, df3, d_gf, d_gt3, loss_cols = _final_loss("final_loss", h2, f3, tgt, gt3, 0.5, gfv, ts)
    rs, held = {}, {}
    dn3 = _ffn_bwd(
        "ffn2", n3, hu2, act2, df3, w2_in_g, w2_out_2d, dims,
        after_dw_out=lambda g: held.update(w2_out=g),
        after_dw_in=lambda g: rs.update(ffn2=_ReduceScatter("g_ffn2", ["w2_out", "w2_in"], [held["w2_out"], g],
                                                            c_idx, qc_idx)))
    dh2, dmix, d_sh3, d_sc3, d_g2, d_gt2 = _norm_mod_bwd("ffn2_norm_bwd", h2, dn3, dh3, g2v, sc3, ts,
                                                         prev=(mix, gt2, 1.0))
    rs["ffn2"].step2()

    tk = _tile(s_len, 512)
    hq = d // (2 * N_CHIPS)
    gw_out = _matmul(
        "mix_dw_out", z, [dmix], mode="tn", grid=(N_CHIPS, d // tn, s_len // tk),
        a_spec=pl.BlockSpec((tk, 2 * hq), lambda i, j, k: (k, i)),
        b_specs=[pl.BlockSpec((tk, tn), lambda i, j, k: (k, j))],
        out_shape=_sds((2, N_CHIPS, hq, d), F32),
        out_specs=pl.BlockSpec((2, None, hq, tn), lambda i, j, k: (0, i, 0, j)),
        acc_shape=(2 * hq, tn), epilogue=_ep_halves(hq))
    dz = _matmul(
        "mix_dz", dmix, [w_out_2d], mode="nt", grid=(s_len // tm, d // tn, 1),
        a_spec=pl.BlockSpec((tm, d), lambda i, j, k: (i, 0)),
        b_specs=[pl.BlockSpec((tn, d), lambda i, j, k: (j, 0))],
        out_shape=_sds((s_len, d), F32), out_specs=pl.BlockSpec((tm, tn), lambda i, j, k: (i, j)),
        acc_shape=(tm, tn), epilogue=_ep_store(F32))
    dya, dyb, dgates, d_ba, d_ls, d_bb = _gates_bwd("mix_gates_bwd", proj, dz, ya, yb, bav, bb_full, lsv, wc, wp, ts)
    gw_a = _matmul(
        "mix_dw_a", a3, [dya], mode="tn", grid=(1, N_CHIPS, s_len // tk),
        a_spec=pl.BlockSpec((tk, wc), lambda i, j, k: (k, 0)),
        b_specs=[pl.BlockSpec((tk, dq), lambda i, j, k: (k, j))],
        out_shape=_sds((2, N_CHIPS, wc // 2, dq), F32),
        out_specs=pl.BlockSpec((2, None, wc // 2, dq), lambda i, j, k: (0, j, 0, 0)),
        acc_shape=(wc, dq), epilogue=_ep_halves(wc // 2))
    da3 = _matmul(
        "mix_da3", dya, [w_a_g], mode="nt", grid=(s_len // tm, 1, N_CHIPS),
        a_spec=pl.BlockSpec((tm, dq), lambda i, j, k: (i, k)),
        b_specs=[pl.BlockSpec((None, wc, dq), lambda i, j, k: (k, 0, 0))],
        out_shape=_sds((s_len, wc), F32), out_specs=pl.BlockSpec((tm, wc), lambda i, j, k: (i, 0)),
        acc_shape=(tm, wc), epilogue=_ep_store(F32))
    gpr = n_groups // 2
    gw_b = _matmul(
        "mix_dw_b", mixed, [dyb], mode="tn", grid=(1, n_groups * N_CHIPS, s_len // tk),
        a_spec=pl.BlockSpec((tk, gi), lambda i, j, k: (k, j // N_CHIPS)),
        b_specs=[pl.BlockSpec((tk, goq), lambda i, j, k: (k, j))],
        out_shape=_sds((2, N_CHIPS, gpr * gi, goq), F32),
        out_specs=pl.BlockSpec((None, None, gi, goq),
                               lambda i, j, k: ((j // N_CHIPS) // gpr, j % N_CHIPS, (j // N_CHIPS) % gpr, 0)),
        acc_shape=(gi, goq), epilogue=_ep_store(F32))
    dmixed = _matmul(
        "mix_dmixed", dyb, [w_b_g], mode="nt", grid=(s_len // tm, n_groups, N_CHIPS),
        a_spec=pl.BlockSpec((tm, goq), lambda i, j, k: (i, j * N_CHIPS + k)),
        b_specs=[pl.BlockSpec((None, gi, goq), lambda i, j, k: (k, j, 0))],
        out_shape=_sds((s_len, wp), F32), out_specs=pl.BlockSpec((tm, gi), lambda i, j, k: (i, j)),
        acc_shape=(tm, gi), epilogue=_ep_store(F32))
    da1, d_lg, d_lb, d_cb, d_cw = _conv_branch_bwd("mix_conv_bwd", proj, da3, cw_pad, cbv, lgv, lbv, wc, wp, ts)
    dproj = _mixer_in_bwd("mix_in_bwd", proj, da1, dmixed, dgates, cw_pad, wc, wp, ts)
    hd = d // 2
    gw_in = _matmul(
        "mix_dw_in", n2, [dproj], mode="tn", grid=(2, 8, s_len // tk),
        a_spec=pl.BlockSpec((tk, hd), lambda i, j, k: (k, i)),
        b_specs=[pl.BlockSpec((None, tk, tnp), lambda i, j, k: (j // 2, k, j % 2))],
        out_shape=_sds((2, N_CHIPS, hd, npc), F32),
        out_specs=pl.BlockSpec((None, None, hd, tnp), lambda i, j, k: (i, j // 2, 0, j % 2)),
        acc_shape=(hd, tnp), epilogue=_ep_store(F32))
    rs["mix"] = _ReduceScatter("g_mix", groups["mix"], [gw_in, gw_a, gw_b, gw_out], c_idx, qc_idx)
    rs["ffn2"].step3()
    dn2 = _matmul(
        "mix_dn", dproj, [w_in_g], mode="nt", grid=(s_len // tm, d // tn, N_CHIPS),
        a_spec=pl.BlockSpec((None, tm, npc), lambda i, j, k: (k, i, 0)),
        b_specs=[pl.BlockSpec((None, tn, npc), lambda i, j, k: (k, j, 0))],
        out_shape=_sds((s_len, d), F32), out_specs=pl.BlockSpec((tm, tn), lambda i, j, k: (i, j)),
        acc_shape=(tm, tn), epilogue=_ep_store(F32))
    dh1, df1, d_sh2, d_sc2, d_gm, d_gt1 = _norm_mod_bwd("mix_norm_bwd", h1, dn2, dh2, gmv, sc2, ts,
                                                        prev=(f1, gt1, 0.5))
    rs["mix"].step2()

    def w1_in_ready(g):
        rs["w1_in"] = _ReduceScatter("g_w1_in", ["w1_in"], [g], c_idx, qc_idx)
        rs["w1_out"].step2()
        rs["mix"].step3()

    dn1 = _ffn_bwd(
        "ffn1", n1, hu1, act1, df1, w1_in_g, w1_out_2d, dims,
        after_dw_out=lambda g: rs.update(w1_out=_ReduceScatter("g_w1_out", ["w1_out"], [g], c_idx, qc_idx)),
        after_dw_in=w1_in_ready)
    grad_x, d_sh1, d_sc1, d_g1 = _norm_mod_bwd("ffn1_norm_bwd", x2, dn1, dh1, g1v, sc1, ts)

    d_ada = jnp.concatenate([d_sh1, d_sc1, d_gt1, d_sh2, d_sc2, d_gt2, d_sh3, d_sc3, d_gt3], axis=1)
    small = [d_ada, d_g1, d_gm, d_cw[:CONV_K].reshape(1, -1), d_cb, d_lg, d_lb, d_ba, d_bb, d_ls, d_g2, d_gf,
             loss_cols]
    sizes = [a.shape[1] for a in small]
    pack1 = jnp.concatenate(small, axis=1).reshape(-1)
    n1p = -(-pack1.shape[0] // (8 * LANES)) * LANES
    pack1 = jnp.pad(pack1, (0, 8 * n1p - pack1.shape[0])).reshape(8, n1p)
    g2 = _allgather_small("gather_small_grads", pack1)
    rs["w1_in"].step2()
    total = _sum_devices("sum_small_grads", g2, 8).reshape(-1)
    offs = [0]
    for sz in sizes:
        offs.append(offs[-1] + sz)
    tot = [total[offs[k]:offs[k + 1]] for k in range(len(sizes))]
    d_ada_all = g2.reshape(N_DEV, 8 * n1p)[:, :sizes[0]]
    loss = jnp.sum(tot[12])

    grads = {}
    grads["b_ada"] = tot[0]
    grads["g_ffn1"], grads["g_mix"] = tot[1], tot[2]
    grads["conv_w"] = lax.dynamic_slice(tot[3].reshape(CONV_K, wc), (0, q * cwq), (CONV_K, cwq))
    grads["conv_b"], grads["ln_a_g"], grads["ln_a_b"], grads["b_a_out"] = tot[4], tot[5], tot[6], tot[7]
    grads["b_b_group"] = lax.dynamic_slice(tot[8].reshape(n_groups, N_CHIPS * goq), (0, q * goq), (n_groups, goq))
    grads["ls_b"], grads["g_ffn2"], grads["g_final"] = tot[9], tot[10], tot[11]

    delta, new_m, new_v = {}, {}, {}

    def adamw_group(reduced):
        for nm, g in reduced.items():
            shp = weights[nm].shape
            go, dl, mo, vo = _adamw(f"adamw_{nm}", as2d(weights[nm]), g, as2d(mom1[nm]), as2d(mom2[nm]))
            grads[nm], delta[nm], new_m[nm], new_v[nm] = go.reshape(shp), dl.reshape(shp), mo.reshape(shp), vo.reshape(shp)

    adamw_group(rs["ffn2"].result())
    rs["w1_out"].step3()
    adamw_group(rs["mix"].result())
    d_ada_mine = lax.dynamic_slice(d_ada_all, (0, q * ada_c), (N_DEV, ada_c))
    grads["w_ada"], delta["w_ada"], new_m["w_ada"], new_v["w_ada"] = _ada_grad_adamw(
        "adamw_w_ada", c_all.T, d_ada_mine, w_ada, m_w_ada, v_w_ada)
    rs["w1_in"].step3()
    smalls = [nm for nm in order if nm not in big and nm != "w_ada"]
    flat = lambda src: jnp.concatenate([src[nm].reshape(-1) for nm in smalls])
    n_small = sum(weights[nm].size for nm in smalls)
    rows_s = -(-n_small // (8 * LANES)) * 8
    packed = [jnp.pad(flat(src), (0, rows_s * LANES - n_small)).reshape(rows_s, LANES)
              for src in (weights, grads, mom1, mom2)]
    _, dl_s, mo_s, vo_s = _adamw("adamw_small", *packed)
    off = 0
    for nm in smalls:
        sz, shp = weights[nm].size, weights[nm].shape
        delta[nm] = dl_s.reshape(-1)[off:off + sz].reshape(shp)
        new_m[nm] = mo_s.reshape(-1)[off:off + sz].reshape(shp)
        new_v[nm] = vo_s.reshape(-1)[off:off + sz].reshape(shp)
        grads[nm] = grads[nm].reshape(shp)
        off += sz
    adamw_group(rs["w1_out"].result())
    adamw_group(rs["w1_in"].result())

    return (loss, grad_x[None], *[grads[nm] for nm in order], *[delta[nm] for nm in order],
            *[new_m[nm] for nm in order], *[new_v[nm] for nm in order])
```

```python
import math

import jax
import jax.numpy as jnp
from jax import lax
from jax.experimental import pallas as pl
from jax.experimental.pallas import tpu as pltpu

F32 = jnp.float32
BF16 = jnp.bfloat16
MESH = pl.DeviceIdType.MESH
ANY = pl.BlockSpec(memory_space=pl.ANY)
HBM = pl.BlockSpec(memory_space=pltpu.HBM)
SEM = pl.BlockSpec(memory_space=pltpu.SEMAPHORE)
EFFECT = pltpu.SideEffectType.DATAFLOW_SIDE_EFFECTING

EPS = 1e-6
CONV_K = 31
HALO = 32
POOL_WINDOWS = (2, 4, 8, 16)
N_CHIPS = 4
N_DEV = 8
LANES = 128

ADAM_LR = 0.001
ADAM_B1 = 0.9
ADAM_B2 = 0.999
ADAM_EPS = 1e-08
ADAM_WD = 0.01
ADAM_STEP = 10

DN = {
    "nn": (((1,), (0,)), ((), ())),
    "nt": (((1,), (1,)), ((), ())),
    "tn": (((0,), (0,)), ((), ())),
}


_PREVIOUS = []


def _ordered(call, args, n_lead, body, token=None, sources=()):
    dep = [p for p in _PREVIOUS if all(p is not a for a in (*args, *sources))]

    def wrapped(*refs):
        return body(*refs[:n_lead], *refs[n_lead + len(dep):])

    outs = call(wrapped, [ANY] * len(dep))(*args, *dep)
    seq = outs if isinstance(outs, (list, tuple)) else [outs]
    _PREVIOUS[:] = [seq[token] if token is not None else
                    next(o for o in seq if jnp.issubdtype(o.dtype, jnp.floating))]
    return outs


def _pcall(body, *, name, out_shape, grid=None, in_specs=None, out_specs=None, scratch=(), aliases=None,
           prefetch=0, vmem_mb=None):
    params = {}
    if grid is not None:
        params["dimension_semantics"] = ("arbitrary",) * len(grid)
    if vmem_mb is not None:
        params["vmem_limit_bytes"] = vmem_mb << 20
    kw = dict(name=name, out_shape=out_shape, compiler_params=pltpu.CompilerParams(**params))
    if aliases:
        kw["input_output_aliases"] = aliases

    def call(wrapped, dep_specs):
        specs = list(in_specs) + dep_specs
        if prefetch:
            return pl.pallas_call(wrapped, grid_spec=pltpu.PrefetchScalarGridSpec(
                num_scalar_prefetch=prefetch, grid=grid, in_specs=specs, out_specs=out_specs,
                scratch_shapes=list(scratch)), **kw)
        if grid is not None:
            return pl.pallas_call(wrapped, grid=grid, in_specs=specs, out_specs=out_specs,
                                  scratch_shapes=list(scratch), **kw)
        return pl.pallas_call(wrapped, in_specs=specs, out_specs=out_specs, scratch_shapes=list(scratch), **kw)

    def run(*args):
        specs = [None] * prefetch + list(in_specs)
        placed = [pltpu.with_memory_space_constraint(a, pltpu.HBM)
                  if a.size * a.dtype.itemsize >= (1 << 20) and getattr(s, "memory_space", None) != pltpu.VMEM else a
                  for a, s in zip(args, specs)]
        return _ordered(call, placed, prefetch + len(in_specs), body, sources=args)

    return run


def _tile(dim, pref):
    t = min(dim, pref)
    assert dim % t == 0, (dim, pref)
    return t


def _sds(shape, dtype):
    return jax.ShapeDtypeStruct(tuple(shape), dtype)


def _sigmoid(v):
    return 1.0 / (1.0 + jnp.exp(-v))


def _vec(w):
    return pl.BlockSpec((1, w), lambda *_: (0, 0))


def _acc_rows(ref, val, i):
    @pl.when(i == 0)
    def _():
        ref[...] = jnp.zeros_like(ref)

    ref[...] += jnp.sum(val, axis=0, keepdims=True)


def _matmul(name, a, bs, *, mode, grid, a_spec, b_specs, out_shape, out_specs, acc_shape, epilogue,
            extras=(), extra_specs=(), vmem_mb=56):
    nb, ne, nk = len(bs), len(extras), grid[2]
    dn = DN[mode]

    def body(*refs):
        a_ref, b_refs, ex = refs[0], refs[1:1 + nb], refs[1 + nb:1 + nb + ne]
        if nk == 1:
            outs = refs[1 + nb + ne:]
            accs = [lax.dot_general(a_ref[...], b[...], dn, preferred_element_type=F32) for b in b_refs]
            epilogue(accs, ex, outs)
            return
        outs, acc_refs = refs[1 + nb + ne:-nb], refs[-nb:]
        k = pl.program_id(2)

        @pl.when(k == 0)
        def _():
            for acc in acc_refs:
                acc[...] = jnp.zeros_like(acc)

        for acc, b in zip(acc_refs, b_refs):
            acc[...] += lax.dot_general(a_ref[...], b[...], dn, preferred_element_type=F32)

        @pl.when(k == nk - 1)
        def _():
            epilogue([acc[...] for acc in acc_refs], ex, outs)

    scratch = [pltpu.VMEM(acc_shape, F32) for _ in range(nb)] if nk > 1 else []
    return _pcall(body, name=name, out_shape=out_shape, grid=grid,
                  in_specs=[a_spec, *b_specs, *extra_specs], out_specs=out_specs, scratch=scratch,
                  vmem_mb=vmem_mb)(a, *bs, *extras)


def _ep_store(dtype):
    def ep(accs, ex, outs):
        outs[0][...] = accs[0].astype(dtype)
    return ep


def _ep_halves(h):
    def ep(accs, ex, outs):
        outs[0][0] = accs[0][:h]
        outs[0][1] = accs[0][h:]
    return ep


def _place():
    x, y, c = lax.axis_index("x"), lax.axis_index("y"), lax.axis_index("c")
    chips = [(1 - x, y), (x, 1 - y), (1 - x, 1 - y)]
    return x, y, c, chips


def _allgather_small(name, block):
    m_per, n = block.shape

    def body(x_ref, out_ref, send_sems, recv_sems, local_sem):
        x, y, c, chips = _place()
        me, sibling = (x, y, c), (x, y, 1 - c)

        def rows(px, py, pc):
            return out_ref.at[pl.ds((4 * px + 2 * py + pc) * m_per, m_per), :]

        def copy(k, blk, to, src=None):
            return pltpu.make_async_remote_copy(
                src_ref=rows(*blk) if src is None else src, dst_ref=rows(*blk),
                send_sem=send_sems.at[k], recv_sem=recv_sems.at[k], device_id=to, device_id_type=MESH)

        mine = pltpu.make_async_copy(x_ref, rows(*me), local_sem)
        mine.start()
        first = [copy(0, me, sibling, src=x_ref)]
        first += [copy(1 + j, me, (*chip, c), src=x_ref) for j, chip in enumerate(chips)]
        for cp in first:
            cp.start()
        passed = [copy(4 + j, (*chip, c), sibling) for j, chip in enumerate(chips)]
        for j, chip in enumerate(chips):
            copy(1 + j, (*chip, c), me).wait_recv()
            passed[j].start()
        copy(0, sibling, me).wait_recv()
        for j, chip in enumerate(chips):
            copy(4 + j, (*chip, 1 - c), me).wait_recv()
        for cp in first + passed:
            cp.wait_send()
        mine.wait()

    return _pcall(
        body, name=name, out_shape=_sds((N_DEV * m_per, n), block.dtype),
        in_specs=[pl.BlockSpec(memory_space=pltpu.VMEM)], out_specs=pl.BlockSpec(memory_space=pltpu.VMEM),
        scratch=[pltpu.SemaphoreType.DMA((7,)), pltpu.SemaphoreType.DMA((7,)), pltpu.SemaphoreType.DMA],
    )(block)


class _SplitCopies:
    def __init__(self, name, arrays, plan, n_copies):
        self.name, self.plan, self.n = name, plan, len(arrays)
        n = self.n

        def body(*refs):
            send, recv, token = refs[n], refs[n + 1], refs[-1]
            for k, (src, dst, _, peer) in enumerate(plan(refs[:n])):
                pltpu.make_async_remote_copy(src_ref=src, dst_ref=dst, send_sem=send.at[k], recv_sem=recv.at[k],
                                             device_id=peer, device_id_type=MESH).start()
            token[...] = jnp.zeros_like(token)

        def call(wrapped, dep_specs):
            return pl.pallas_call(
                wrapped, name=f"{name}_start",
                out_shape=(pltpu.SemaphoreType.DMA((n_copies,)), pltpu.SemaphoreType.DMA((n_copies,)),
                           *[pltpu.HBM(a.shape, a.dtype) for a in arrays], _sds((8, LANES), F32)),
                in_specs=[HBM] * n + dep_specs,
                out_specs=(SEM, SEM, *[HBM] * n, pl.BlockSpec(memory_space=pltpu.VMEM)),
                input_output_aliases={i: 2 + i for i in range(n)},
                compiler_params=pltpu.CompilerParams(has_side_effects=EFFECT))

        outs = _ordered(call, [pltpu.with_memory_space_constraint(a, pltpu.HBM) for a in arrays], n, body, token=-1,
                        sources=arrays)
        self.send, self.recv, self.arrays = outs[0], outs[1], list(outs[2:2 + n])

    def wait(self):
        n, plan = self.n, self.plan

        def body(*refs):
            send, recv, token = refs[n], refs[n + 1], refs[-1]
            for k, (src, _, landing, peer) in enumerate(plan(refs[:n])):
                cp = pltpu.make_async_remote_copy(src_ref=src, dst_ref=landing, send_sem=send.at[k],
                                                  recv_sem=recv.at[k], device_id=peer, device_id_type=MESH)
                cp.wait_send()
                cp.wait_recv()
            token[...] = jnp.zeros_like(token)

        def call(wrapped, dep_specs):
            return pl.pallas_call(
                wrapped, name=f"{self.name}_wait",
                out_shape=(*[pltpu.HBM(a.shape, a.dtype) for a in self.arrays], _sds((8, LANES), F32)),
                in_specs=[HBM] * n + [SEM, SEM] + dep_specs,
                out_specs=(*[HBM] * n, pl.BlockSpec(memory_space=pltpu.VMEM)),
                input_output_aliases={i: i for i in range(n)},
                compiler_params=pltpu.CompilerParams(has_side_effects=EFFECT))

        return list(_ordered(call, [*self.arrays, self.send, self.recv], n + 2, body, token=-1))[:n]


def _gather_ici(name, gathered):
    def plan(refs):
        x, y, c, chips = _place()
        q = 2 * x + y
        return [(g.at[q, c], g.at[q, c], g.at[2 * px + py, c], (px, py, c)) for g in refs for px, py in chips]

    return _SplitCopies(name, gathered, plan, 3 * len(gathered))


def _gather_d2d(name, gathered):
    def plan(refs):
        x, y, c, chips = _place()
        return [(g.at[2 * px + py, c], g.at[2 * px + py, c], g.at[2 * px + py, 1 - c], (x, y, 1 - c))
                for g in refs for px, py in chips]

    return _SplitCopies(name, gathered, plan, 3 * len(gathered))


def _scatter_sibling(name, grads):
    n = len(grads)

    def plan(refs):
        x, y, c, _ = _place()
        return [(refs[w].at[1 - c], refs[n + w], refs[n + w], (x, y, 1 - c)) for w in range(n)]

    landing = [lax.empty(g.shape[1:], g.dtype) for g in grads]
    return _SplitCopies(name, [*grads, *landing], plan, n)


def _scatter_chips(name, sums):
    n = len(sums)

    def plan(refs):
        x, y, c, chips = _place()
        return [(refs[w].at[2 * px + py], refs[n + w].at[j], refs[n + w].at[j], (px, py, c))
                for w in range(n) for j, (px, py) in enumerate(chips)]

    landing = [lax.empty((3, *s.shape[1:]), s.dtype) for s in sums]
    return _SplitCopies(name, [*sums, *landing], plan, 3 * n)


def _share_final(name, finals):
    def plan(refs):
        x, y, c, _ = _place()
        return [(f.at[c], f.at[c], f.at[1 - c], (x, y, 1 - c)) for f in refs]

    return _SplitCopies(name, finals, plan, len(finals))


def _row_tile(rows, cols, budget_elems=393216):
    best = 8
    for t in range(8, rows + 1, 8):
        if rows % t == 0 and t * cols <= budget_elems:
            best = t
    return best if rows % best == 0 else rows


def _sum_with_sibling(name, grad, recv, c_idx):
    _, _, h, cols = grad.shape
    tr = _row_tile(h, cols)

    def body(s_ref, g_ref, r_ref, p_ref, pb_ref):
        p = g_ref[...] + r_ref[...]
        p_ref[...] = p
        pb_ref[...] = p.astype(BF16)

    blk = pl.BlockSpec((None, tr, cols), lambda k, r, s: (k, r, 0))
    return _pcall(
        body, name=name, out_shape=[_sds((N_CHIPS, h, cols), F32), _sds((N_CHIPS, h, cols), BF16)],
        grid=(N_CHIPS, h // tr), prefetch=1,
        in_specs=[pl.BlockSpec((None, None, tr, cols), lambda k, r, s: (s[0], k, r, 0)), blk],
        out_specs=[blk, blk], vmem_mb=32,
    )(c_idx, grad, recv)


def _sum_chips(name, own, recv, qc_idx):
    _, h, cols = own.shape
    tr = _row_tile(h, cols)

    def body(s_ref, p_ref, t_ref, o_ref):
        o_ref[...] = ((p_ref[...] + t_ref[0].astype(F32)) + t_ref[1].astype(F32)) + t_ref[2].astype(F32)

    return _pcall(
        body, name=name, out_shape=_sds((2, h, cols), F32), grid=(h // tr,), prefetch=1,
        in_specs=[pl.BlockSpec((None, tr, cols), lambda r, s: (s[0], r, 0)),
                  pl.BlockSpec((3, tr, cols), lambda r, s: (0, r, 0))],
        out_specs=pl.BlockSpec((None, tr, cols), lambda r, s: (s[1], r, 0)), vmem_mb=32,
    )(qc_idx, own, recv)


class _ReduceScatter:
    def __init__(self, tag, names, grads, c_idx, qc_idx):
        self.tag, self.names, self.n, self.c_idx, self.qc_idx = tag, names, len(grads), c_idx, qc_idx
        self.copies = _scatter_sibling(f"{tag}_rs_sibling", grads)

    def step2(self):
        n = self.n
        arrs = self.copies.wait()
        sums = [_sum_with_sibling(f"{nm}_sum_sibling", arrs[w], arrs[n + w], self.c_idx)
                for w, nm in enumerate(self.names)]
        self.own = [s[0] for s in sums]
        self.copies = _scatter_chips(f"{self.tag}_rs_chips", [s[1] for s in sums])

    def step3(self):
        n = self.n
        arrs = self.copies.wait()
        finals = [_sum_chips(f"{nm}_sum_chips", self.own[w], arrs[n + w], self.qc_idx)
                  for w, nm in enumerate(self.names)]
        self.copies = _share_final(f"{self.tag}_rs_final", finals)

    def result(self):
        return {nm: f.reshape(2 * f.shape[1], f.shape[2]) for nm, f in zip(self.names, self.copies.wait())}


def _cast_into_gathered(name, w, q_idx):
    rows, cols = w.shape
    h = rows // 2
    tr = _row_tile(h, cols, 1 << 20)
    nr = h // tr

    def body(s_ref, w_ref, o_ref):
        o_ref[...] = w_ref[...].astype(BF16)

    return _pcall(body, name=name, out_shape=_sds((N_CHIPS, 2, h, cols), BF16), grid=(2, nr), prefetch=1,
                  in_specs=[pl.BlockSpec((tr, cols), lambda hf, r, s: (hf * nr + r, 0))],
                  out_specs=pl.BlockSpec((None, None, tr, cols), lambda hf, r, s: (s[0], hf, r, 0)),
                  vmem_mb=32)(q_idx, w)


def _rms(h):
    r = lax.rsqrt(jnp.mean(h * h, axis=-1, keepdims=True) + EPS)
    return r, h * r


def _norm_mod(name, h, g, sc, sh, ts):
    s_len, d = h.shape

    def body(h_ref, g_ref, sc_ref, sh_ref, n_ref):
        _, xhat = _rms(h_ref[...])
        n_ref[...] = ((xhat * g_ref[...]) * (1.0 + sc_ref[...]) + sh_ref[...]).astype(BF16)

    row = pl.BlockSpec((ts, d), lambda i: (i, 0))
    return _pcall(body, name=name, out_shape=_sds((s_len, d), BF16), grid=(s_len // ts,),
                  in_specs=[row, _vec(d), _vec(d), _vec(d)], out_specs=row, vmem_mb=32)(h, g, sc, sh)


def _residual_norm_mod(name, h, f, gate, cmul, g, sc, sh, ts):
    s_len, d = h.shape

    def body(h_ref, f_ref, gt_ref, g_ref, sc_ref, sh_ref, ho_ref, n_ref):
        hn = h_ref[...] + (cmul * gt_ref[...]) * f_ref[...]
        ho_ref[...] = hn
        _, xhat = _rms(hn)
        n_ref[...] = ((xhat * g_ref[...]) * (1.0 + sc_ref[...]) + sh_ref[...]).astype(BF16)

    row = pl.BlockSpec((ts, d), lambda i: (i, 0))
    return _pcall(body, name=name, out_shape=[_sds((s_len, d), F32), _sds((s_len, d), BF16)],
                  grid=(s_len // ts,), in_specs=[row, row, _vec(d), _vec(d), _vec(d), _vec(d)],
                  out_specs=[row, row], vmem_mb=32)(h, f, gate, g, sc, sh)


def _final_loss(name, h, f, tgt, gate, cmul, g, ts):
    s_len, d = h.shape

    def body(h_ref, f_ref, t_ref, gt_ref, g_ref, dh_ref, df_ref, dg_ref, dgt_ref, loss_ref):
        i = pl.program_id(0)
        fv = f_ref[...]
        coef = cmul * gt_ref[...]
        hn = h_ref[...] + coef * fv
        r, xhat = _rms(hn)
        err = xhat * g_ref[...] - t_ref[...]
        _acc_rows(loss_ref, (0.5 / d) * (err * err), i)
        dy = err * (1.0 / d)
        _acc_rows(dg_ref, dy * xhat, i)
        dxhat = dy * g_ref[...]
        dh = r * (dxhat - xhat * jnp.mean(dxhat * xhat, axis=-1, keepdims=True))
        dh_ref[...] = dh
        _acc_rows(dgt_ref, cmul * (dh * fv), i)
        df_ref[...] = (coef * dh).astype(BF16)

    row = pl.BlockSpec((ts, d), lambda i: (i, 0))
    return _pcall(body, name=name,
                  out_shape=[_sds((s_len, d), F32), _sds((s_len, d), BF16)] + [_sds((1, d), F32)] * 3,
                  grid=(s_len // ts,), in_specs=[row, row, row, _vec(d), _vec(d)],
                  out_specs=[row, row, _vec(d), _vec(d), _vec(d)], vmem_mb=40)(h, f, tgt, gate, g)


def _norm_mod_bwd(name, h, dn, dh_next, g, sc, ts, prev=None):
    s_len, d = h.shape
    has_prev = prev is not None
    cmul = prev[2] if has_prev else None

    def body(*refs):
        if has_prev:
            h_ref, dn_ref, dhn_ref, f_ref, g_ref, sc_ref, gt_ref, dh_ref, df_ref, dsh_ref, dsc_ref, dg_ref, dgt_ref = refs
        else:
            h_ref, dn_ref, dhn_ref, g_ref, sc_ref, dh_ref, dsh_ref, dsc_ref, dg_ref = refs
        i = pl.program_id(0)
        r, xhat = _rms(h_ref[...])
        dn_v = dn_ref[...]
        gv = g_ref[...]
        _acc_rows(dsh_ref, dn_v, i)
        _acc_rows(dsc_ref, dn_v * (xhat * gv), i)
        dnrm = dn_v * (1.0 + sc_ref[...])
        _acc_rows(dg_ref, dnrm * xhat, i)
        dxhat = dnrm * gv
        dh = dhn_ref[...] + r * (dxhat - xhat * jnp.mean(dxhat * xhat, axis=-1, keepdims=True))
        dh_ref[...] = dh
        if has_prev:
            _acc_rows(dgt_ref, cmul * (dh * f_ref[...]), i)
            df_ref[...] = ((cmul * gt_ref[...]) * dh).astype(BF16)

    row = pl.BlockSpec((ts, d), lambda i: (i, 0))
    if has_prev:
        ins, in_specs = [h, dn, dh_next, prev[0], g, sc, prev[1]], [row, row, row, row, _vec(d), _vec(d), _vec(d)]
        out_shape = [_sds((s_len, d), F32), _sds((s_len, d), BF16)] + [_sds((1, d), F32)] * 4
        out_specs = [row, row] + [_vec(d)] * 4
    else:
        ins, in_specs = [h, dn, dh_next, g, sc], [row, row, row, _vec(d), _vec(d)]
        out_shape = [_sds((s_len, d), F32)] + [_sds((1, d), F32)] * 3
        out_specs = [row] + [_vec(d)] * 3
    return _pcall(body, name=name, out_shape=out_shape, grid=(s_len // ts,), in_specs=in_specs,
                  out_specs=out_specs, vmem_mb=40)(*ins)


def _cols(ref, lo, hi, npc, rows=slice(None)):
    parts = []
    while lo < hi:
        q, o = divmod(lo, npc)
        n = min(hi - lo, npc - o)
        parts.append(ref[q, rows, o:o + n])
        lo += n
    return parts[0] if len(parts) == 1 else jnp.concatenate(parts, axis=-1)


def _store_cols(ref, lo, val, npc, rows=slice(None)):
    off, width = 0, val.shape[-1]
    while off < width:
        q, o = divmod(lo + off, npc)
        n = min(width - off, npc - o)
        ref[q, rows, o:o + n] = val[:, off:off + n]
        off += n


def _chips_covering(cols, npc):
    return -(-cols // npc)


SUBLANES = 8
ROW_CHUNK = 32


def _make_phases(src_ref, ph_ref):
    rows = src_ref.shape[0] - SUBLANES
    for b in range(1, SUBLANES):
        ph_ref[b - 1] = src_ref[pl.ds(b, rows), :]


def _window(src_ref, ph_ref, off, r0, cols=slice(None)):
    a, b = divmod(off, SUBLANES)
    start = pl.multiple_of(r0 + SUBLANES * a, SUBLANES)
    if b == 0:
        return src_ref[pl.ds(start, ROW_CHUNK), cols]
    return ph_ref[b - 1, pl.ds(start, ROW_CHUNK), cols]


def _phase_scratch(rows, width):
    return pltpu.VMEM((SUBLANES - 1, rows - SUBLANES, width), F32)


def _conv_ln(a0s_ref, a0p_ref, cw_ref, cb_ref, lg_ref, lb_ref, r0):
    a1 = cb_ref[...] + cw_ref[0:1, :] * _window(a0s_ref, a0p_ref, HALO - CONV_K + 1, r0)
    for k in range(1, CONV_K):
        a1 = a1 + cw_ref[k:k + 1, :] * _window(a0s_ref, a0p_ref, HALO - CONV_K + 1 + k, r0)
    mu = jnp.mean(a1, axis=-1, keepdims=True)
    ctr = a1 - mu
    rstd = lax.rsqrt(jnp.mean(ctr * ctr, axis=-1, keepdims=True) + EPS)
    xh = ctr * rstd
    return xh, rstd, xh * lg_ref[...] + lb_ref[...]


def _for_chunks(ts, fn):
    def step(ci, carry):
        fn(pl.multiple_of(ci * ROW_CHUNK, ROW_CHUNK))
        return carry

    lax.fori_loop(0, ts // ROW_CHUNK, step, 0)


def _stage_glu(p_ref, ph_ref, a0s_ref, i, wc, npc, ts):
    a0 = _cols(p_ref, 0, wc, npc) * _sigmoid(_cols(p_ref, wc, 2 * wc, npc))
    a0h = _cols(ph_ref, 0, wc, npc) * _sigmoid(_cols(ph_ref, wc, 2 * wc, npc))
    a0s_ref[0:HALO, :] = jnp.where(i > 0, a0h, 0.0)
    a0s_ref[HALO:HALO + ts, :] = a0


def _mixer_mid(name, proj, cw, cb, lg, lb, wc, wp, ts):
    _, s_len, npc = proj.shape
    nq = _chips_covering(2 * wc + wp, npc)
    gi = wp // len(POOL_WINDOWS)
    hb = ts // HALO

    def body(p_ref, ph_ref, cw_ref, cb_ref, lg_ref, lb_ref, a3_ref, mx_ref, a0s_ref, vs_ref, a0p_ref, vp_ref):
        i = pl.program_id(0)
        _stage_glu(p_ref, ph_ref, a0s_ref, i, wc, npc, ts)
        vs_ref[0:HALO, :] = jnp.where(i > 0, _cols(ph_ref, 2 * wc, 2 * wc + wp, npc), 0.0)
        vs_ref[HALO:HALO + ts, :] = _cols(p_ref, 2 * wc, 2 * wc + wp, npc)
        _make_phases(a0s_ref, a0p_ref)
        _make_phases(vs_ref, vp_ref)

        def chunk(r0):
            rows = pl.ds(r0, ROW_CHUNK)
            _, _, a2 = _conv_ln(a0s_ref, a0p_ref, cw_ref, cb_ref, lg_ref, lb_ref, r0)
            a3_ref[rows, :] = (a2 * _sigmoid(a2)).astype(BF16)
            t_abs = i * ts + r0 + lax.broadcasted_iota(jnp.int32, (ROW_CHUNK, 1), 0)
            for g, win in enumerate(POOL_WINDOWS):
                cs = slice(g * gi, (g + 1) * gi)
                v_now = _window(vs_ref, vp_ref, HALO, r0, cs)
                acc = v_now
                for dlt in range(1, win):
                    acc = acc + _window(vs_ref, vp_ref, HALO - dlt, r0, cs)
                cnt = jnp.minimum(t_abs + 1, win).astype(F32)
                mx_ref[rows, cs] = (acc / cnt - v_now).astype(BF16)

        _for_chunks(ts, chunk)

    return _pcall(
        body, name=name, out_shape=[_sds((s_len, wc), BF16), _sds((s_len, wp), BF16)], grid=(s_len // ts,),
        in_specs=[pl.BlockSpec((nq, ts, npc), lambda i: (0, i, 0)),
                  pl.BlockSpec((nq, HALO, npc), lambda i: (0, jnp.maximum(i * hb - 1, 0), 0)),
                  pl.BlockSpec((HALO, wc), lambda i: (0, 0)), _vec(wc), _vec(wc), _vec(wc)],
        out_specs=[pl.BlockSpec((ts, wc), lambda i: (i, 0)), pl.BlockSpec((ts, wp), lambda i: (i, 0))],
        scratch=[pltpu.VMEM((HALO + ts, wc), F32), pltpu.VMEM((HALO + ts, wp), F32),
                 _phase_scratch(HALO + ts, wc), _phase_scratch(HALO + ts, wp)], vmem_mb=56,
    )(proj, proj, cw, cb, lg, lb)


def _gates_fwd(name, proj, ya, yb, b_a, b_b, ls, wc, wp, ts):
    _, s_len, npc = proj.shape
    d = ya.shape[1]
    g0 = 2 * wc + wp

    def body(p_ref, ya_ref, yb_ref, ba_ref, bb_ref, ls_ref, z_ref):
        ga = _sigmoid(_cols(p_ref, g0, g0 + d, npc))
        gb = _sigmoid(_cols(p_ref, g0 + d, g0 + 2 * d, npc))
        z = ga * (ya_ref[...] + ba_ref[...]) + gb * ((yb_ref[...] + bb_ref[...]) * ls_ref[...])
        z_ref[...] = z.astype(BF16)

    row = pl.BlockSpec((ts, d), lambda i: (i, 0))
    return _pcall(body, name=name, out_shape=_sds((s_len, d), BF16), grid=(s_len // ts,),
                  in_specs=[pl.BlockSpec((N_CHIPS, ts, npc), lambda i: (0, i, 0)), row, row, _vec(d), _vec(d), _vec(d)],
                  out_specs=row, vmem_mb=48)(proj, ya, yb, b_a, b_b, ls)


def _gates_bwd(name, proj, dz, ya, yb, b_a, b_b, ls, wc, wp, ts):
    _, s_len, npc = proj.shape
    d = ya.shape[1]
    g0 = 2 * wc + wp

    def body(p_ref, dz_ref, ya_ref, yb_ref, ba_ref, bb_ref, ls_ref, dya_ref, dyb_ref, dgt_ref, dba_ref, dls_ref,
             dbb_ref):
        i = pl.program_id(0)
        ga = _sigmoid(_cols(p_ref, g0, g0 + d, npc))
        gb = _sigmoid(_cols(p_ref, g0 + d, g0 + 2 * d, npc))
        dz_v = dz_ref[...]
        y_a = ya_ref[...] + ba_ref[...]
        y_b0 = yb_ref[...] + bb_ref[...]
        ls_v = ls_ref[...]
        dya = dz_v * ga
        dya_ref[...] = dya.astype(BF16)
        _acc_rows(dba_ref, dya, i)
        t = dz_v * gb
        _acc_rows(dls_ref, t * y_b0, i)
        dyb = t * ls_v
        dyb_ref[...] = dyb.astype(BF16)
        _acc_rows(dbb_ref, dyb, i)
        dgt_ref[:, 0:d] = (dz_v * y_a * ga * (1.0 - ga)).astype(BF16)
        dgt_ref[:, d:2 * d] = (dz_v * (y_b0 * ls_v) * gb * (1.0 - gb)).astype(BF16)

    row = pl.BlockSpec((ts, d), lambda i: (i, 0))
    return _pcall(
        body, name=name,
        out_shape=[_sds((s_len, d), BF16), _sds((s_len, d), BF16), _sds((s_len, 2 * d), BF16)] + [_sds((1, d), F32)] * 3,
        grid=(s_len // ts,),
        in_specs=[pl.BlockSpec((N_CHIPS, ts, npc), lambda i: (0, i, 0)), row, row, row, _vec(d), _vec(d), _vec(d)],
        out_specs=[row, row, pl.BlockSpec((ts, 2 * d), lambda i: (i, 0))] + [_vec(d)] * 3, vmem_mb=48,
    )(proj, dz, ya, yb, b_a, b_b, ls)


def _conv_branch_bwd(name, proj, da3, cw, cb, lg, lb, wc, wp, ts):
    _, s_len, npc = proj.shape
    nq = _chips_covering(2 * wc, npc)
    hb = ts // HALO

    n_tiles = s_len // ts

    def fold(v):
        return jnp.sum(v.reshape(ROW_CHUNK // SUBLANES, SUBLANES, v.shape[-1]), axis=0)

    def body(p_ref, ph_ref, da3_ref, cw_ref, cb_ref, lg_ref, lb_ref, da1_ref, dlg_ref, dlb_ref, dcb_ref, dcw_ref,
             a0s_ref, a0p_ref, vec8_ref, dcw8_ref):
        i = pl.program_id(0)
        _stage_glu(p_ref, ph_ref, a0s_ref, i, wc, npc, ts)
        _make_phases(a0s_ref, a0p_ref)

        @pl.when(i == 0)
        def _():
            vec8_ref[...] = jnp.zeros_like(vec8_ref)
            dcw8_ref[...] = jnp.zeros_like(dcw8_ref)

        def chunk(r0):
            rows = pl.ds(r0, ROW_CHUNK)
            xh, rstd, a2 = _conv_ln(a0s_ref, a0p_ref, cw_ref, cb_ref, lg_ref, lb_ref, r0)
            sig = _sigmoid(a2)
            da2 = da3_ref[rows, :] * (sig * (1.0 + a2 * (1.0 - sig)))
            vec8_ref[0] += fold(da2 * xh)
            vec8_ref[1] += fold(da2)
            dxh = da2 * lg_ref[...]
            da1 = rstd * (dxh - jnp.mean(dxh, axis=-1, keepdims=True)
                          - xh * jnp.mean(dxh * xh, axis=-1, keepdims=True))
            da1_ref[rows, :] = da1
            vec8_ref[2] += fold(da1)
            for k in range(CONV_K):
                dcw8_ref[k] += fold(da1 * _window(a0s_ref, a0p_ref, HALO - CONV_K + 1 + k, r0))

        _for_chunks(ts, chunk)

        @pl.when(i == n_tiles - 1)
        def _():
            dlg_ref[...] = jnp.sum(vec8_ref[0], axis=0, keepdims=True)
            dlb_ref[...] = jnp.sum(vec8_ref[1], axis=0, keepdims=True)
            dcb_ref[...] = jnp.sum(vec8_ref[2], axis=0, keepdims=True)
            dcw_ref[...] = jnp.sum(dcw8_ref[...], axis=1)

    return _pcall(
        body, name=name,
        out_shape=[_sds((s_len, wc), F32)] + [_sds((1, wc), F32)] * 3 + [_sds((HALO, wc), F32)],
        grid=(s_len // ts,),
        in_specs=[pl.BlockSpec((nq, ts, npc), lambda i: (0, i, 0)),
                  pl.BlockSpec((nq, HALO, npc), lambda i: (0, jnp.maximum(i * hb - 1, 0), 0)),
                  pl.BlockSpec((ts, wc), lambda i: (i, 0)),
                  pl.BlockSpec((HALO, wc), lambda i: (0, 0)), _vec(wc), _vec(wc), _vec(wc)],
        out_specs=[pl.BlockSpec((ts, wc), lambda i: (i, 0)), _vec(wc), _vec(wc), _vec(wc),
                   pl.BlockSpec((HALO, wc), lambda i: (0, 0))],
        scratch=[pltpu.VMEM((HALO + ts, wc), F32), _phase_scratch(HALO + ts, wc),
                 pltpu.VMEM((3, SUBLANES, wc), F32), pltpu.VMEM((HALO, SUBLANES, wc), F32)], vmem_mb=56,
    )(proj, proj, da3, cw, cb, lg, lb)


def _mixer_in_bwd(name, proj, da1, dmixed, dgates, cw, wc, wp, ts):
    _, s_len, npc = proj.shape
    nq = _chips_covering(2 * wc, npc)
    gi = wp // len(POOL_WINDOWS)
    hb = ts // HALO
    n_tiles = s_len // ts
    last_hb = s_len // HALO - 1
    d2 = dgates.shape[1]

    def body(p_ref, d1_ref, d1n_ref, dm_ref, dmn_ref, dgt_ref, cw_ref, o_ref, d1s_ref, es_ref, d1p_ref, ep_ref):
        i = pl.program_id(0)
        more = i < n_tiles - 1
        d1s_ref[0:ts, :] = d1_ref[...]
        d1s_ref[ts:ts + HALO, :] = jnp.where(more, d1n_ref[...], 0.0)
        t_abs = i * ts + lax.broadcasted_iota(jnp.int32, (ts + HALO, 1), 0)
        dm_ext = jnp.concatenate([dm_ref[...], jnp.where(more, dmn_ref[...], 0.0)], axis=0)
        for g, win in enumerate(POOL_WINDOWS):
            cs = slice(g * gi, (g + 1) * gi)
            es_ref[:, cs] = dm_ext[:, cs] / jnp.minimum(t_abs + 1, win).astype(F32)
        _make_phases(d1s_ref, d1p_ref)
        _make_phases(es_ref, ep_ref)

        def chunk(r0):
            rows = pl.ds(r0, ROW_CHUNK)
            da0 = cw_ref[0:1, :] * _window(d1s_ref, d1p_ref, CONV_K - 1, r0)
            for k in range(1, CONV_K):
                da0 = da0 + cw_ref[k:k + 1, :] * _window(d1s_ref, d1p_ref, CONV_K - 1 - k, r0)
            glu_a = _cols(p_ref, 0, wc, npc, rows)
            sig = _sigmoid(_cols(p_ref, wc, 2 * wc, npc, rows))
            _store_cols(o_ref, 0, (da0 * sig).astype(BF16), npc, rows)
            _store_cols(o_ref, wc, (da0 * glu_a * sig * (1.0 - sig)).astype(BF16), npc, rows)
            parts = []
            for g, win in enumerate(POOL_WINDOWS):
                cs = slice(g * gi, (g + 1) * gi)
                acc = _window(es_ref, ep_ref, 0, r0, cs)
                for dlt in range(1, win):
                    acc = acc + _window(es_ref, ep_ref, dlt, r0, cs)
                parts.append(acc - dm_ref[rows, cs])
            _store_cols(o_ref, 2 * wc, jnp.concatenate(parts, axis=-1).astype(BF16), npc, rows)

        _for_chunks(ts, chunk)
        _store_cols(o_ref, 2 * wc + wp, dgt_ref[...], npc)

    nxt = lambda i: (jnp.minimum((i + 1) * hb, last_hb), 0)
    return _pcall(
        body, name=name, out_shape=_sds((N_CHIPS, s_len, npc), BF16), grid=(n_tiles,),
        in_specs=[pl.BlockSpec((nq, ts, npc), lambda i: (0, i, 0)),
                  pl.BlockSpec((ts, wc), lambda i: (i, 0)), pl.BlockSpec((HALO, wc), nxt),
                  pl.BlockSpec((ts, wp), lambda i: (i, 0)), pl.BlockSpec((HALO, wp), nxt),
                  pl.BlockSpec((ts, d2), lambda i: (i, 0)),
                  pl.BlockSpec((HALO, wc), lambda i: (0, 0))],
        out_specs=pl.BlockSpec((N_CHIPS, ts, npc), lambda i: (0, i, 0)),
        scratch=[pltpu.VMEM((ts + HALO, wc), F32), pltpu.VMEM((ts + HALO, wp), F32),
                 _phase_scratch(ts + HALO, wc), _phase_scratch(ts + HALO, wp)], vmem_mb=56,
    )(proj, da1, da1, dmixed, dmixed, dgates, cw)


def _ada_fwd(name, c_all, w, b):
    d, cols = w.shape
    tn = 512 if cols % 512 == 0 else cols

    def body(c_ref, w_ref, b_ref, o_ref):
        cv = c_ref[...]
        sc = (cv * _sigmoid(cv)).astype(BF16)
        o_ref[...] = jnp.dot(sc, w_ref[...].astype(BF16), preferred_element_type=F32) + b_ref[...]

    return _pcall(body, name=name, out_shape=_sds((N_DEV, cols), F32), grid=(cols // tn,),
                  in_specs=[pl.BlockSpec((N_DEV, d), lambda j: (0, 0)), pl.BlockSpec((d, tn), lambda j: (0, j)),
                            pl.BlockSpec((1, tn), lambda j: (0, j))],
                  out_specs=pl.BlockSpec((N_DEV, tn), lambda j: (0, j)), vmem_mb=32)(c_all, w, b)


def _adam_math(w, g, m, v):
    m_new = ADAM_B1 * m + (1.0 - ADAM_B1) * g
    v_new = ADAM_B2 * v + (1.0 - ADAM_B2) * (g * g)
    m_hat = m_new / (1.0 - ADAM_B1 ** ADAM_STEP)
    v_hat = v_new / (1.0 - ADAM_B2 ** ADAM_STEP)
    delta = -ADAM_LR * (m_hat / (jnp.sqrt(v_hat) + ADAM_EPS) + ADAM_WD * w)
    return delta, m_new, v_new


def _adamw(name, w, g, m, v):
    rows, cols = w.shape
    tr = _row_tile(rows, cols, 262144)

    def body(w_ref, g_ref, m_ref, v_ref, go_ref, d_ref, mo_ref, vo_ref):
        g = g_ref[...]
        go_ref[...] = g
        d_ref[...], mo_ref[...], vo_ref[...] = _adam_math(w_ref[...], g, m_ref[...], v_ref[...])

    spec = pl.BlockSpec((tr, cols), lambda i: (i, 0))
    return _pcall(body, name=name, out_shape=[_sds(w.shape, F32)] * 4, grid=(rows // tr,), in_specs=[spec] * 4,
                  out_specs=[spec] * 4, vmem_mb=40)(w, g, m, v)


def _ada_grad_adamw(name, c_t, d_ada, w, m, v):
    rows, cols = w.shape
    tr = _tile(rows, 256)
    tc = _tile(cols, 1536) if cols % 1536 == 0 else cols

    def body(c_ref, da_ref, w_ref, m_ref, v_ref, g_ref, d_ref, mo_ref, vo_ref):
        cv = c_ref[...]
        sc = cv * _sigmoid(cv)
        g = sc[:, 0:1] * da_ref[0:1, :]
        for b in range(1, N_DEV):
            g = g + sc[:, b:b + 1] * da_ref[b:b + 1, :]
        g_ref[...] = g
        d_ref[...], mo_ref[...], vo_ref[...] = _adam_math(w_ref[...], g, m_ref[...], v_ref[...])

    spec = pl.BlockSpec((tr, tc), lambda i, j: (i, j))
    return _pcall(body, name=name, out_shape=[_sds(w.shape, F32)] * 4, grid=(rows // tr, cols // tc),
                  in_specs=[pl.BlockSpec((tr, N_DEV), lambda i, j: (i, 0)),
                            pl.BlockSpec((N_DEV, tc), lambda i, j: (0, j)), spec, spec, spec],
                  out_specs=[spec] * 4, vmem_mb=40)(c_t, d_ada, w, m, v)


def _sum_devices(name, gathered, m_per):
    n = gathered.shape[1]

    def body(g_ref, o_ref):
        acc = g_ref[0:m_per, :]
        for dev in range(1, N_DEV):
            acc = acc + g_ref[dev * m_per:(dev + 1) * m_per, :]
        o_ref[...] = acc

    return _pcall(body, name=name, out_shape=_sds((m_per, n), F32),
                  in_specs=[pl.BlockSpec(memory_space=pltpu.VMEM)],
                  out_specs=pl.BlockSpec(memory_space=pltpu.VMEM))(gathered)


def _ffn_fwd(tag, n, w_in_g, w_out2d, dims, after_swiglu=lambda: None):
    s_len, d, f_dim = dims["S"], dims["D"], dims["F"]
    p = f_dim // 2
    tm, tn = _tile(s_len, 1024), math.gcd(p, 256)
    nbp = p // tn

    def ep(accs, ex, outs):
        hh, uu = accs
        outs[0][0] = hh.astype(BF16)
        outs[0][1] = uu.astype(BF16)
        outs[1][...] = (hh * _sigmoid(hh) * uu).astype(BF16)

    hu, act = _matmul(
        f"{tag}_swiglu", n, [w_in_g, w_in_g], mode="nn", grid=(s_len // tm, f_dim // tn, 1),
        a_spec=pl.BlockSpec((tm, d), lambda i, j, k: (i, 0)),
        b_specs=[pl.BlockSpec((None, d, tn), lambda i, j, k: (j // nbp, 0, j % nbp)),
                 pl.BlockSpec((None, d, tn), lambda i, j, k: (2 + j // nbp, 0, j % nbp))],
        out_shape=[_sds((2, s_len, f_dim), BF16), _sds((s_len, f_dim), BF16)],
        out_specs=[pl.BlockSpec((2, tm, tn), lambda i, j, k: (0, i, j)),
                   pl.BlockSpec((tm, tn), lambda i, j, k: (i, j))],
        acc_shape=(tm, tn), epilogue=ep)
    after_swiglu()
    tf = f_dim // 4
    tn2 = _tile(d, 1024)
    f = _matmul(
        f"{tag}_down", act, [w_out2d], mode="nn", grid=(s_len // tm, d // tn2, 2),
        a_spec=pl.BlockSpec((tm, 2 * tf), lambda i, j, k: (i, k)),
        b_specs=[pl.BlockSpec((2 * tf, tn2), lambda i, j, k: (k, j))],
        out_shape=_sds((s_len, d), F32), out_specs=pl.BlockSpec((tm, tn2), lambda i, j, k: (i, j)),
        acc_shape=(tm, tn2), epilogue=_ep_store(F32))
    return hu, act, f


def _ffn_bwd(tag, n, hu, act, df, w_in_g, w_out2d, dims, after_dw_out, after_dw_in):
    s_len, d, f_dim = dims["S"], dims["D"], dims["F"]
    tf = f_dim // 4
    tk = _tile(s_len, 2048)
    tn = _tile(d, 1024)
    g_out = _matmul(
        f"{tag}_dw_out", act, [df], mode="tn", grid=(4, d // tn, s_len // tk),
        a_spec=pl.BlockSpec((tk, tf), lambda i, j, k: (k, i)),
        b_specs=[pl.BlockSpec((tk, tn), lambda i, j, k: (k, j))],
        out_shape=_sds((2, 4, tf // 2, d), F32),
        out_specs=pl.BlockSpec((2, None, tf // 2, tn), lambda i, j, k: (0, i, 0, j)),
        acc_shape=(tf, tn), epilogue=_ep_halves(tf // 2))
    after_dw_out(g_out)

    def ep_dhu(accs, ex, outs):
        da = accs[0]
        hh, uu = ex[0][0].astype(F32), ex[0][1].astype(F32)
        sig = _sigmoid(hh)
        outs[0][0] = (da * uu * (sig * (1.0 + hh * (1.0 - sig)))).astype(BF16)
        outs[0][1] = (da * (hh * sig)).astype(BF16)

    tm = _tile(s_len, 512)
    hu_spec = pl.BlockSpec((2, tm, tf), lambda i, j, k: (0, i, j))
    dhu = _matmul(
        f"{tag}_dhu", df, [w_out2d], mode="nt", grid=(s_len // tm, 4, 1),
        a_spec=pl.BlockSpec((tm, d), lambda i, j, k: (i, 0)),
        b_specs=[pl.BlockSpec((tf, d), lambda i, j, k: (j, 0))],
        extras=[hu], extra_specs=[hu_spec],
        out_shape=_sds((2, s_len, f_dim), BF16), out_specs=hu_spec, acc_shape=(tm, tf), epilogue=ep_dhu)

    hd = d // 2
    g_in = _matmul(
        f"{tag}_dw_in", n, [dhu], mode="tn", grid=(2, 8, s_len // tk),
        a_spec=pl.BlockSpec((tk, hd), lambda i, j, k: (k, i)),
        b_specs=[pl.BlockSpec((None, tk, tf), lambda i, j, k: (j // 4, k, j % 4))],
        out_shape=_sds((2, 4, hd, f_dim // 2), F32),
        out_specs=pl.BlockSpec((None, None, hd, tf), lambda i, j, k: (i, j // 2, 0, j % 2)),
        acc_shape=(hd, tf), epilogue=_ep_store(F32))
    after_dw_in(g_in)

    tm2 = _tile(s_len, 1024)
    dn = _matmul(
        f"{tag}_dn", dhu, [w_in_g], mode="nt", grid=(s_len // tm2, d // tn, 4),
        a_spec=pl.BlockSpec((None, tm2, 2 * tf), lambda i, j, k: (k // 2, i, k % 2)),
        b_specs=[pl.BlockSpec((None, tn, 2 * tf), lambda i, j, k: (k, j, 0))],
        out_shape=_sds((s_len, d), F32), out_specs=pl.BlockSpec((tm2, tn), lambda i, j, k: (i, j)),
        acc_shape=(tm2, tn), epilogue=_ep_store(F32))
    return dn


def kernel(x, c, w_ada, b_ada, g_ffn1, w1_in, w1_out, g_mix, w_in, conv_w, conv_b, ln_a_g, ln_a_b, w_a_out, b_a_out, w_b_group, b_b_group, ls_b, w_out, g_ffn2, w2_in, w2_out, g_final, loss_target, m_w_ada, m_b_ada, m_g_ffn1, m_w1_in, m_w1_out, m_g_mix, m_w_in, m_conv_w, m_conv_b, m_ln_a_g, m_ln_a_b, m_w_a_out, m_b_a_out, m_w_b_group, m_b_b_group, m_ls_b, m_w_out, m_g_ffn2, m_w2_in, m_w2_out, m_g_final, v_w_ada, v_b_ada, v_g_ffn1, v_w1_in, v_w1_out, v_g_mix, v_w_in, v_conv_w, v_conv_b, v_ln_a_g, v_ln_a_b, v_w_a_out, v_b_a_out, v_w_b_group, v_b_b_group, v_ls_b, v_w_out, v_g_ffn2, v_w2_in, v_w2_out, v_g_final):
    weights = dict(w_ada=w_ada, b_ada=b_ada, g_ffn1=g_ffn1, w1_in=w1_in, w1_out=w1_out, g_mix=g_mix, w_in=w_in,
                   conv_w=conv_w, conv_b=conv_b, ln_a_g=ln_a_g, ln_a_b=ln_a_b, w_a_out=w_a_out, b_a_out=b_a_out,
                   w_b_group=w_b_group, b_b_group=b_b_group, ls_b=ls_b, w_out=w_out, g_ffn2=g_ffn2, w2_in=w2_in,
                   w2_out=w2_out, g_final=g_final)
    mom1 = dict(w_ada=m_w_ada, b_ada=m_b_ada, g_ffn1=m_g_ffn1, w1_in=m_w1_in, w1_out=m_w1_out, g_mix=m_g_mix,
                w_in=m_w_in, conv_w=m_conv_w, conv_b=m_conv_b, ln_a_g=m_ln_a_g, ln_a_b=m_ln_a_b, w_a_out=m_w_a_out,
                b_a_out=m_b_a_out, w_b_group=m_w_b_group, b_b_group=m_b_b_group, ls_b=m_ls_b, w_out=m_w_out,
                g_ffn2=m_g_ffn2, w2_in=m_w2_in, w2_out=m_w2_out, g_final=m_g_final)
    mom2 = dict(w_ada=v_w_ada, b_ada=v_b_ada, g_ffn1=v_g_ffn1, w1_in=v_w1_in, w1_out=v_w1_out, g_mix=v_g_mix,
                w_in=v_w_in, conv_w=v_conv_w, conv_b=v_conv_b, ln_a_g=v_ln_a_g, ln_a_b=v_ln_a_b, w_a_out=v_w_a_out,
                b_a_out=v_b_a_out, w_b_group=v_w_b_group, b_b_group=v_b_b_group, ls_b=v_ls_b, w_out=v_w_out,
                g_ffn2=v_g_ffn2, w2_in=v_w2_in, w2_out=v_w2_out, g_final=v_g_final)
    order = list(weights)

    s_len, d = x.shape[1], x.shape[2]
    f_dim = w1_out.shape[0] * N_CHIPS
    wc = conv_w.shape[1] * N_CHIPS
    wp = w_b_group.shape[0] * w_b_group.shape[1]
    n_groups, gi, goq = w_b_group.shape
    npc = w_in.shape[1]
    ada_c = w_ada.shape[1]
    dims = dict(S=s_len, D=d, F=f_dim)
    ts = _tile(s_len, 256)

    xi, yi, ci = lax.axis_index("x"), lax.axis_index("y"), lax.axis_index("c")
    q = 2 * xi + yi
    dev = 2 * q + ci
    q_idx = jnp.reshape(q, (1,)).astype(jnp.int32)
    c_idx = jnp.reshape(ci, (1,)).astype(jnp.int32)
    qc_idx = jnp.stack([q, ci]).astype(jnp.int32)
    _PREVIOUS.clear()

    cwq = conv_w.shape[1]
    pack0 = jnp.concatenate([c.reshape(-1), conv_w.reshape(-1), b_b_group.reshape(-1)])
    n0 = -(-pack0.shape[0] // (8 * LANES)) * LANES
    pack0 = jnp.pad(pack0, (0, 8 * n0 - pack0.shape[0])).reshape(8, n0)
    g0 = _allgather_small("gather_small_in", pack0).reshape(N_DEV, 8 * n0)
    c_all = g0[:, :d]
    south = g0[0::2]
    cw_full = jnp.concatenate([south[k, d:d + CONV_K * cwq].reshape(CONV_K, cwq) for k in range(N_CHIPS)], axis=1)
    cw_pad = jnp.pad(cw_full, ((0, HALO - CONV_K), (0, 0)))
    o_bb = d + CONV_K * cwq
    bb_full = jnp.concatenate([south[k, o_bb:o_bb + n_groups * goq].reshape(n_groups, goq) for k in range(N_CHIPS)],
                              axis=1).reshape(1, d)

    b_ada_mine = lax.dynamic_slice(b_ada, (q * ada_c,), (ada_c,)).reshape(1, ada_c)
    ada_piece = _ada_fwd("ada_fwd", c_all, w_ada, b_ada_mine)
    g1 = _allgather_small("gather_ada", ada_piece).reshape(N_DEV, N_DEV, ada_c)
    ada_rows = lax.dynamic_index_in_dim(g1[0::2], dev, axis=1, keepdims=False)
    ada = ada_rows.reshape(3, 3, 1, d)
    (sh1, sc1, gt1), (sh2, sc2, gt2), (sh3, sc3, gt3) = [[ada[i, j] for j in range(3)] for i in range(3)]

    row = lambda vct: vct.reshape(1, -1)
    g1v, gmv, g2v, gfv = row(g_ffn1), row(g_mix), row(g_ffn2), row(g_final)

    as2d = lambda a: a.reshape(-1, a.shape[-1])
    groups = dict(ffn1=["w1_in", "w1_out"], mix=["w_in", "w_a_out", "w_b_group", "w_out"], ffn2=["w2_in", "w2_out"])
    big = [nm for grp in groups.values() for nm in grp]
    ici = {}
    for grp, names in groups.items():
        ici[grp] = _gather_ici(f"gather_{grp}_ici",
                               [_cast_into_gathered(f"cast_{nm}", as2d(weights[nm]), q_idx) for nm in names])

    def arrived(grp):
        return _gather_d2d(f"gather_{grp}_d2d", ici[grp].wait())

    def gathered(fwd, grp):
        return {nm: g.reshape(N_CHIPS, 2 * g.shape[2], g.shape[3]) for nm, g in zip(groups[grp], fwd.wait())}

    x2 = x[0]
    tgt = loss_target[0]

    n1 = _norm_mod("ffn1_norm", x2, g1v, sc1, sh1, ts)
    wts = gathered(arrived("ffn1"), "ffn1")
    w1_in_g, w1_out_2d = wts["w1_in"], wts["w1_out"].reshape(f_dim, d)
    fwd = {}
    hu1, act1, f1 = _ffn_fwd("ffn1", n1, w1_in_g, w1_out_2d, dims,
                             after_swiglu=lambda: fwd.update(mix=arrived("mix")))
    h1, n2 = _residual_norm_mod("mix_norm", x2, f1, gt1, 0.5, gmv, sc2, sh2, ts)
    wts = gathered(fwd["mix"], "mix")
    w_in_g, w_out_2d = wts["w_in"], wts["w_out"].reshape(d, d)
    w_a_g = wts["w_a_out"]
    w_b_g = wts["w_b_group"]

    tm = _tile(s_len, 1024)
    tnp = npc // 2
    proj = _matmul(
        "mix_proj", n2, [w_in_g], mode="nn", grid=(s_len // tm, 8, 1),
        a_spec=pl.BlockSpec((tm, d), lambda i, j, k: (i, 0)),
        b_specs=[pl.BlockSpec((None, d, tnp), lambda i, j, k: (j // 2, 0, j % 2))],
        out_shape=_sds((N_CHIPS, s_len, npc), F32),
        out_specs=pl.BlockSpec((None, tm, tnp), lambda i, j, k: (j // 2, i, j % 2)),
        acc_shape=(tm, tnp), epilogue=_ep_store(F32))
    fwd["ffn2"] = arrived("ffn2")
    cbv, lgv, lbv = row(conv_b), row(ln_a_g), row(ln_a_b)
    a3, mixed = _mixer_mid("mix_mid", proj, cw_pad, cbv, lgv, lbv, wc, wp, ts)
    dq = d // N_CHIPS
    ya = _matmul(
        "mix_ya", a3, [w_a_g], mode="nn", grid=(s_len // tm, N_CHIPS, 1),
        a_spec=pl.BlockSpec((tm, wc), lambda i, j, k: (i, 0)),
        b_specs=[pl.BlockSpec((None, wc, dq), lambda i, j, k: (j, 0, 0))],
        out_shape=_sds((s_len, d), F32), out_specs=pl.BlockSpec((tm, dq), lambda i, j, k: (i, j)),
        acc_shape=(tm, dq), epilogue=_ep_store(F32))
    yb = _matmul(
        "mix_yb", mixed, [w_b_g], mode="nn", grid=(s_len // tm, n_groups * N_CHIPS, 1),
        a_spec=pl.BlockSpec((tm, gi), lambda i, j, k: (i, j // N_CHIPS)),
        b_specs=[pl.BlockSpec((None, gi, goq), lambda i, j, k: (j % N_CHIPS, j // N_CHIPS, 0))],
        out_shape=_sds((s_len, d), F32), out_specs=pl.BlockSpec((tm, goq), lambda i, j, k: (i, j)),
        acc_shape=(tm, goq), epilogue=_ep_store(F32))
    bav, lsv = row(b_a_out), row(ls_b)
    z = _gates_fwd("mix_gates", proj, ya, yb, bav, bb_full, lsv, wc, wp, ts)
    tn = _tile(d, 1024)
    mix = _matmul(
        "mix_out", z, [w_out_2d], mode="nn", grid=(s_len // tm, d // tn, 1),
        a_spec=pl.BlockSpec((tm, d), lambda i, j, k: (i, 0)),
        b_specs=[pl.BlockSpec((d, tn), lambda i, j, k: (0, j))],
        out_shape=_sds((s_len, d), F32), out_specs=pl.BlockSpec((tm, tn), lambda i, j, k: (i, j)),
        acc_shape=(tm, tn), epilogue=_ep_store(F32))
    h2, n3 = _residual_norm_mod("ffn2_norm", h1, mix, gt2, 1.0, g2v, sc3, sh3, ts)
    wts = gathered(fwd["ffn2"], "ffn2")
    w2_in_g, w2_out_2d = wts["w2_in"], wts["w2_out"].reshape(f_dim, d)
    hu2, act2, f3 = _ffn_fwd("ffn2", n3, w2_in_g, w2_out_2d, dims)

    dh3, df3, d_gf, d_gt3, loss_cols = _final_loss("final_loss", h2, f3, tgt, gt3, 0.5, gfv, ts)
    rs, held = {}, {}
    dn3 = _ffn_bwd(
        "ffn2", n3, hu2, act2, df3, w2_in_g, w2_out_2d, dims,
        after_dw_out=lambda g: held.update(w2_out=g),
        after_dw_in=lambda g: rs.update(ffn2=_ReduceScatter("g_ffn2", ["w2_out", "w2_in"], [held["w2_out"], g],
                                                            c_idx, qc_idx)))
    dh2, dmix, d_sh3, d_sc3, d_g2, d_gt2 = _norm_mod_bwd("ffn2_norm_bwd", h2, dn3, dh3, g2v, sc3, ts,
                                                         prev=(mix, gt2, 1.0))
    rs["ffn2"].step2()

    tk = s_len
    hq = d // (2 * N_CHIPS)
    gw_out = _matmul(
        "mix_dw_out", z, [dmix], mode="tn", grid=(N_CHIPS, d // tn, s_len // tk),
        a_spec=pl.BlockSpec((tk, 2 * hq), lambda i, j, k: (k, i)),
        b_specs=[pl.BlockSpec((tk, tn), lambda i, j, k: (k, j))],
        out_shape=_sds((2, N_CHIPS, hq, d), F32),
        out_specs=pl.BlockSpec((2, None, hq, tn), lambda i, j, k: (0, i, 0, j)),
        acc_shape=(2 * hq, tn), epilogue=_ep_halves(hq))
    dz = _matmul(
        "mix_dz", dmix, [w_out_2d], mode="nt", grid=(s_len // tm, d // tn, 1),
        a_spec=pl.BlockSpec((tm, d), lambda i, j, k: (i, 0)),
        b_specs=[pl.BlockSpec((tn, d), lambda i, j, k: (j, 0))],
        out_shape=_sds((s_len, d), F32), out_specs=pl.BlockSpec((tm, tn), lambda i, j, k: (i, j)),
        acc_shape=(tm, tn), epilogue=_ep_store(F32))
    dya, dyb, dgates, d_ba, d_ls, d_bb = _gates_bwd("mix_gates_bwd", proj, dz, ya, yb, bav, bb_full, lsv, wc, wp, ts)
    gw_a = _matmul(
        "mix_dw_a", a3, [dya], mode="tn", grid=(1, N_CHIPS, s_len // tk),
        a_spec=pl.BlockSpec((tk, wc), lambda i, j, k: (k, 0)),
        b_specs=[pl.BlockSpec((tk, dq), lambda i, j, k: (k, j))],
        out_shape=_sds((2, N_CHIPS, wc // 2, dq), F32),
        out_specs=pl.BlockSpec((2, None, wc // 2, dq), lambda i, j, k: (0, j, 0, 0)),
        acc_shape=(wc, dq), epilogue=_ep_halves(wc // 2))
    da3 = _matmul(
        "mix_da3", dya, [w_a_g], mode="nt", grid=(s_len // tm, 1, N_CHIPS),
        a_spec=pl.BlockSpec((tm, dq), lambda i, j, k: (i, k)),
        b_specs=[pl.BlockSpec((None, wc, dq), lambda i, j, k: (k, 0, 0))],
        out_shape=_sds((s_len, wc), F32), out_specs=pl.BlockSpec((tm, wc), lambda i, j, k: (i, 0)),
        acc_shape=(tm, wc), epilogue=_ep_store(F32))
    gpr = n_groups // 2
    gw_b = _matmul(
        "mix_dw_b", mixed, [dyb], mode="tn", grid=(1, n_groups * N_CHIPS, s_len // tk),
        a_spec=pl.BlockSpec((tk, gi), lambda i, j, k: (k, j // N_CHIPS)),
        b_specs=[pl.BlockSpec((tk, goq), lambda i, j, k: (k, j))],
        out_shape=_sds((2, N_CHIPS, gpr * gi, goq), F32),
        out_specs=pl.BlockSpec((None, None, gi, goq),
                               lambda i, j, k: ((j // N_CHIPS) // gpr, j % N_CHIPS, (j // N_CHIPS) % gpr, 0)),
        acc_shape=(gi, goq), epilogue=_ep_store(F32))
    dmixed = _matmul(
        "mix_dmixed", dyb, [w_b_g], mode="nt", grid=(s_len // tm, n_groups, N_CHIPS),
        a_spec=pl.BlockSpec((tm, goq), lambda i, j, k: (i, j * N_CHIPS + k)),
        b_specs=[pl.BlockSpec((None, gi, goq), lambda i, j, k: (k, j, 0))],
        out_shape=_sds((s_len, wp), F32), out_specs=pl.BlockSpec((tm, gi), lambda i, j, k: (i, j)),
        acc_shape=(tm, gi), epilogue=_ep_store(F32))
    da1, d_lg, d_lb, d_cb, d_cw = _conv_branch_bwd("mix_conv_bwd", proj, da3, cw_pad, cbv, lgv, lbv, wc, wp, ts)
    dproj = _mixer_in_bwd("mix_in_bwd", proj, da1, dmixed, dgates, cw_pad, wc, wp, ts)
    hd = d // 2
    gw_in = _matmul(
        "mix_dw_in", n2, [dproj], mode="tn", grid=(2, 8, s_len // tk),
        a_spec=pl.BlockSpec((tk, hd), lambda i, j, k: (k, i)),
        b_specs=[pl.BlockSpec((None, tk, tnp), lambda i, j, k: (j // 2, k, j % 2))],
        out_shape=_sds((2, N_CHIPS, hd, npc), F32),
        out_specs=pl.BlockSpec((None, None, hd, tnp), lambda i, j, k: (i, j // 2, 0, j % 2)),
        acc_shape=(hd, tnp), epilogue=_ep_store(F32))
    rs["mix"] = _ReduceScatter("g_mix", groups["mix"], [gw_in, gw_a, gw_b, gw_out], c_idx, qc_idx)
    rs["ffn2"].step3()
    dn2 = _matmul(
        "mix_dn", dproj, [w_in_g], mode="nt", grid=(s_len // tm, d // tn, N_CHIPS),
        a_spec=pl.BlockSpec((None, tm, npc), lambda i, j, k: (k, i, 0)),
        b_specs=[pl.BlockSpec((None, tn, npc), lambda i, j, k: (k, j, 0))],
        out_shape=_sds((s_len, d), F32), out_specs=pl.BlockSpec((tm, tn), lambda i, j, k: (i, j)),
        acc_shape=(tm, tn), epilogue=_ep_store(F32))
    dh1, df1, d_sh2, d_sc2, d_gm, d_gt1 = _norm_mod_bwd("mix_norm_bwd", h1, dn2, dh2, gmv, sc2, ts,
                                                        prev=(f1, gt1, 0.5))
    rs["mix"].step2()

    def w1_in_ready(g):
        rs["w1_in"] = _ReduceScatter("g_w1_in", ["w1_in"], [g], c_idx, qc_idx)
        rs["w1_out"].step2()
        rs["mix"].step3()

    dn1 = _ffn_bwd(
        "ffn1", n1, hu1, act1, df1, w1_in_g, w1_out_2d, dims,
        after_dw_out=lambda g: rs.update(w1_out=_ReduceScatter("g_w1_out", ["w1_out"], [g], c_idx, qc_idx)),
        after_dw_in=w1_in_ready)
    grad_x, d_sh1, d_sc1, d_g1 = _norm_mod_bwd("ffn1_norm_bwd", x2, dn1, dh1, g1v, sc1, ts)

    d_ada = jnp.concatenate([d_sh1, d_sc1, d_gt1, d_sh2, d_sc2, d_gt2, d_sh3, d_sc3, d_gt3], axis=1)
    small = [d_ada, d_g1, d_gm, d_cw[:CONV_K].reshape(1, -1), d_cb, d_lg, d_lb, d_ba, d_bb, d_ls, d_g2, d_gf,
             loss_cols]
    sizes = [a.shape[1] for a in small]
    pack1 = jnp.concatenate(small, axis=1).reshape(-1)
    n1p = -(-pack1.shape[0] // (8 * LANES)) * LANES
    pack1 = jnp.pad(pack1, (0, 8 * n1p - pack1.shape[0])).reshape(8, n1p)
    g2 = _allgather_small("gather_small_grads", pack1)
    rs["w1_in"].step2()
    total = _sum_devices("sum_small_grads", g2, 8).reshape(-1)
    offs = [0]
    for sz in sizes:
        offs.append(offs[-1] + sz)
    tot = [total[offs[k]:offs[k + 1]] for k in range(len(sizes))]
    d_ada_all = g2.reshape(N_DEV, 8 * n1p)[:, :sizes[0]]
    loss = jnp.sum(tot[12])

    grads = {}
    grads["b_ada"] = tot[0]
    grads["g_ffn1"], grads["g_mix"] = tot[1], tot[2]
    grads["conv_w"] = lax.dynamic_slice(tot[3].reshape(CONV_K, wc), (0, q * cwq), (CONV_K, cwq))
    grads["conv_b"], grads["ln_a_g"], grads["ln_a_b"], grads["b_a_out"] = tot[4], tot[5], tot[6], tot[7]
    grads["b_b_group"] = lax.dynamic_slice(tot[8].reshape(n_groups, N_CHIPS * goq), (0, q * goq), (n_groups, goq))
    grads["ls_b"], grads["g_ffn2"], grads["g_final"] = tot[9], tot[10], tot[11]

    delta, new_m, new_v = {}, {}, {}

    def adamw_group(reduced):
        for nm, g in reduced.items():
            shp = weights[nm].shape
            go, dl, mo, vo = _adamw(f"adamw_{nm}", as2d(weights[nm]), g, as2d(mom1[nm]), as2d(mom2[nm]))
            grads[nm], delta[nm], new_m[nm], new_v[nm] = go.reshape(shp), dl.reshape(shp), mo.reshape(shp), vo.reshape(shp)

    adamw_group(rs["ffn2"].result())
    rs["w1_out"].step3()
    adamw_group(rs["mix"].result())
    d_ada_mine = lax.dynamic_slice(d_ada_all, (0, q * ada_c), (N_DEV, ada_c))
    grads["w_ada"], delta["w_ada"], new_m["w_ada"], new_v["w_ada"] = _ada_grad_adamw(
        "adamw_w_ada", c_all.T, d_ada_mine, w_ada, m_w_ada, v_w_ada)
    rs["w1_in"].step3()
    smalls = [nm for nm in order if nm not in big and nm != "w_ada"]
    flat = lambda src: jnp.concatenate([src[nm].reshape(-1) for nm in smalls])
    n_small = sum(weights[nm].size for nm in smalls)
    rows_s = -(-n_small // (8 * LANES)) * 8
    packed = [jnp.pad(flat(src), (0, rows_s * LANES - n_small)).reshape(rows_s, LANES)
              for src in (weights, grads, mom1, mom2)]
    _, dl_s, mo_s, vo_s = _adamw("adamw_small", *packed)
    off = 0
    for nm in smalls:
        sz, shp = weights[nm].size, weights[nm].shape
        delta[nm] = dl_s.reshape(-1)[off:off + sz].reshape(shp)
        new_m[nm] = mo_s.reshape(-1)[off:off + sz].reshape(shp)
        new_v[nm] = vo_s.reshape(-1)[off:off + sz].reshape(shp)
        grads[nm] = grads[nm].reshape(shp)
        off += sz
    adamw_group(rs["w1_out"].result())
    adamw_group(rs["w1_in"].result())

    return (loss, grad_x[None], *[grads[nm] for nm in order], *[delta[nm] for nm in order],
            *[new_m[nm] for nm in order], *[new_v[nm] for nm in order])
```

```python
import jax
import jax.numpy as jnp
from jax import lax
from jax.experimental import pallas as pl
from jax.experimental.pallas import tpu as pltpu

F32 = jnp.float32
BF16 = jnp.bfloat16
MESH = pl.DeviceIdType.MESH
ANY = pl.BlockSpec(memory_space=pl.ANY)
HBM = pl.BlockSpec(memory_space=pltpu.HBM)
SEM = pl.BlockSpec(memory_space=pltpu.SEMAPHORE)
EFFECT = pltpu.SideEffectType.DATAFLOW_SIDE_EFFECTING

EPS = 1e-6
CONV_K = 31
HALO = 32
POOL_WINDOWS = (2, 4, 8, 16)
N_CHIPS = 4
N_DEV = 8
LANES = 128

ADAM_LR = 0.001
ADAM_B1 = 0.9
ADAM_B2 = 0.999
ADAM_EPS = 1e-08
ADAM_WD = 0.01
ADAM_STEP = 10

DN = {
    "nn": (((1,), (0,)), ((), ())),
    "nt": (((1,), (1,)), ((), ())),
    "tn": (((0,), (0,)), ((), ())),
}


_PREVIOUS = []


def _ordered(call, args, n_lead, body, token=None, sources=()):
    dep = [p for p in _PREVIOUS if all(p is not a for a in (*args, *sources))]

    def wrapped(*refs):
        return body(*refs[:n_lead], *refs[n_lead + len(dep):])

    outs = call(wrapped, [ANY] * len(dep))(*args, *dep)
    seq = outs if isinstance(outs, (list, tuple)) else [outs]
    _PREVIOUS[:] = [seq[token] if token is not None else
                    next(o for o in seq if jnp.issubdtype(o.dtype, jnp.floating))]
    return outs


def _pcall(body, *, name, out_shape, grid=None, in_specs=None, out_specs=None, scratch=(), aliases=None,
           prefetch=0, vmem_mb=None):
    params = {}
    if grid is not None:
        params["dimension_semantics"] = ("arbitrary",) * len(grid)
    if vmem_mb is not None:
        params["vmem_limit_bytes"] = vmem_mb << 20
    kw = dict(name=name, out_shape=out_shape, compiler_params=pltpu.CompilerParams(**params))
    if aliases:
        kw["input_output_aliases"] = aliases

    def call(wrapped, dep_specs):
        specs = list(in_specs) + dep_specs
        if prefetch:
            return pl.pallas_call(wrapped, grid_spec=pltpu.PrefetchScalarGridSpec(
                num_scalar_prefetch=prefetch, grid=grid, in_specs=specs, out_specs=out_specs,
                scratch_shapes=list(scratch)), **kw)
        if grid is not None:
            return pl.pallas_call(wrapped, grid=grid, in_specs=specs, out_specs=out_specs,
                                  scratch_shapes=list(scratch), **kw)
        return pl.pallas_call(wrapped, in_specs=specs, out_specs=out_specs, scratch_shapes=list(scratch), **kw)

    def run(*args):
        specs = [None] * prefetch + list(in_specs)
        placed = [pltpu.with_memory_space_constraint(a, pltpu.HBM)
                  if a.size * a.dtype.itemsize >= (1 << 20) and getattr(s, "memory_space", None) != pltpu.VMEM else a
                  for a, s in zip(args, specs)]
        return _ordered(call, placed, prefetch + len(in_specs), body, sources=args)

    return run


def _tile(dim, pref):
    t = min(dim, pref)
    assert dim % t == 0, (dim, pref)
    return t


def _sds(shape, dtype):
    return jax.ShapeDtypeStruct(tuple(shape), dtype)


def _sigmoid(v):
    return 1.0 / (1.0 + jnp.exp(-v))


def _vec(w):
    return pl.BlockSpec((1, w), lambda *_: (0, 0))


def _acc_rows(ref, val, i):
    @pl.when(i == 0)
    def _():
        ref[...] = jnp.zeros_like(ref)

    ref[...] += jnp.sum(val, axis=0, keepdims=True)


def _matmul(name, a, bs, *, mode, grid, a_spec, b_specs, out_shape, out_specs, acc_shape, epilogue,
            extras=(), extra_specs=(), vmem_mb=56, col_blocks=None):
    nb, ne, nk = len(bs), len(extras), grid[2]
    dn = DN[mode]

    def body(*refs):
        a_ref, b_refs, ex = refs[0], refs[1:1 + nb], refs[1 + nb:1 + nb + ne]
        if nk == 1 and col_blocks:
            outs = refs[1 + nb + ne:]
            av = a_ref[...]
            for lo, width in col_blocks:
                cs = slice(lo, lo + width)
                accs = [lax.dot_general(av, b[:, cs] if mode == "nn" else b[cs, :], dn, preferred_element_type=F32)
                        for b in b_refs]
                epilogue(accs, ex, outs, cs)
            return
        if nk == 1:
            outs = refs[1 + nb + ne:]
            accs = [lax.dot_general(a_ref[...], b[...], dn, preferred_element_type=F32) for b in b_refs]
            epilogue(accs, ex, outs)
            return
        outs, acc_refs = refs[1 + nb + ne:-nb], refs[-nb:]
        k = pl.program_id(2)

        @pl.when(k == 0)
        def _():
            for acc in acc_refs:
                acc[...] = jnp.zeros_like(acc)

        for acc, b in zip(acc_refs, b_refs):
            acc[...] += lax.dot_general(a_ref[...], b[...], dn, preferred_element_type=F32)

        @pl.when(k == nk - 1)
        def _():
            epilogue([acc[...] for acc in acc_refs], ex, outs)

    scratch = [pltpu.VMEM(acc_shape, F32) for _ in range(nb)] if nk > 1 else []
    return _pcall(body, name=name, out_shape=out_shape, grid=grid,
                  in_specs=[a_spec, *b_specs, *extra_specs], out_specs=out_specs, scratch=scratch,
                  vmem_mb=vmem_mb)(a, *bs, *extras)


MXU_COLS = 256


def _col_blocks(width, block=2 * MXU_COLS):
    return [(lo, min(block, width - lo)) for lo in range(0, width, block)]


def _ep_store(dtype):
    def ep(accs, ex, outs):
        outs[0][...] = accs[0].astype(dtype)
    return ep


def _ep_halves(h):
    def ep(accs, ex, outs):
        outs[0][0] = accs[0][:h]
        outs[0][1] = accs[0][h:]
    return ep


def _place():
    x, y, c = lax.axis_index("x"), lax.axis_index("y"), lax.axis_index("c")
    chips = [(1 - x, y), (x, 1 - y), (1 - x, 1 - y)]
    return x, y, c, chips


def _allgather_small(name, block):
    m_per, n = block.shape

    def body(x_ref, out_ref, send_sems, recv_sems, local_sem):
        x, y, c, chips = _place()
        me, sibling = (x, y, c), (x, y, 1 - c)

        def rows(px, py, pc):
            return out_ref.at[pl.ds((4 * px + 2 * py + pc) * m_per, m_per), :]

        def copy(k, blk, to, src=None):
            return pltpu.make_async_remote_copy(
                src_ref=rows(*blk) if src is None else src, dst_ref=rows(*blk),
                send_sem=send_sems.at[k], recv_sem=recv_sems.at[k], device_id=to, device_id_type=MESH)

        mine = pltpu.make_async_copy(x_ref, rows(*me), local_sem)
        mine.start()
        first = [copy(0, me, sibling, src=x_ref)]
        first += [copy(1 + j, me, (*chip, c), src=x_ref) for j, chip in enumerate(chips)]
        for cp in first:
            cp.start()
        passed = [copy(4 + j, (*chip, c), sibling) for j, chip in enumerate(chips)]
        for j, chip in enumerate(chips):
            copy(1 + j, (*chip, c), me).wait_recv()
            passed[j].start()
        copy(0, sibling, me).wait_recv()
        for j, chip in enumerate(chips):
            copy(4 + j, (*chip, 1 - c), me).wait_recv()
        for cp in first + passed:
            cp.wait_send()
        mine.wait()

    return _pcall(
        body, name=name, out_shape=_sds((N_DEV * m_per, n), block.dtype),
        in_specs=[pl.BlockSpec(memory_space=pltpu.VMEM)], out_specs=pl.BlockSpec(memory_space=pltpu.VMEM),
        scratch=[pltpu.SemaphoreType.DMA((7,)), pltpu.SemaphoreType.DMA((7,)), pltpu.SemaphoreType.DMA],
    )(block)


class _SplitCopies:
    def __init__(self, name, arrays, plan, n_copies):
        self.name, self.plan, self.n = name, plan, len(arrays)
        n = self.n

        def body(*refs):
            send, recv, token = refs[n], refs[n + 1], refs[-1]
            for k, (src, dst, _, peer) in enumerate(plan(refs[:n])):
                pltpu.make_async_remote_copy(src_ref=src, dst_ref=dst, send_sem=send.at[k], recv_sem=recv.at[k],
                                             device_id=peer, device_id_type=MESH).start()
            token[...] = jnp.zeros_like(token)

        def call(wrapped, dep_specs):
            return pl.pallas_call(
                wrapped, name=f"{name}_start",
                out_shape=(pltpu.SemaphoreType.DMA((n_copies,)), pltpu.SemaphoreType.DMA((n_copies,)),
                           *[pltpu.HBM(a.shape, a.dtype) for a in arrays], _sds((8, LANES), F32)),
                in_specs=[HBM] * n + dep_specs,
                out_specs=(SEM, SEM, *[HBM] * n, pl.BlockSpec(memory_space=pltpu.VMEM)),
                input_output_aliases={i: 2 + i for i in range(n)},
                compiler_params=pltpu.CompilerParams(has_side_effects=EFFECT))

        outs = _ordered(call, [pltpu.with_memory_space_constraint(a, pltpu.HBM) for a in arrays], n, body, token=-1,
                        sources=arrays)
        self.send, self.recv, self.arrays = outs[0], outs[1], list(outs[2:2 + n])

    def wait(self):
        n, plan = self.n, self.plan

        def body(*refs):
            send, recv, token = refs[n], refs[n + 1], refs[-1]
            for k, (src, _, landing, peer) in enumerate(plan(refs[:n])):
                cp = pltpu.make_async_remote_copy(src_ref=src, dst_ref=landing, send_sem=send.at[k],
                                                  recv_sem=recv.at[k], device_id=peer, device_id_type=MESH)
                cp.wait_send()
                cp.wait_recv()
            token[...] = jnp.zeros_like(token)

        def call(wrapped, dep_specs):
            return pl.pallas_call(
                wrapped, name=f"{self.name}_wait",
                out_shape=(*[pltpu.HBM(a.shape, a.dtype) for a in self.arrays], _sds((8, LANES), F32)),
                in_specs=[HBM] * n + [SEM, SEM] + dep_specs,
                out_specs=(*[HBM] * n, pl.BlockSpec(memory_space=pltpu.VMEM)),
                input_output_aliases={i: i for i in range(n)},
                compiler_params=pltpu.CompilerParams(has_side_effects=EFFECT))

        return list(_ordered(call, [*self.arrays, self.send, self.recv], n + 2, body, token=-1))[:n]


def _gather_ici(name, gathered):
    def plan(refs):
        x, y, c, chips = _place()
        q = 2 * x + y
        return [(g.at[q, c], g.at[q, c], g.at[2 * px + py, c], (px, py, c)) for g in refs for px, py in chips]

    return _SplitCopies(name, gathered, plan, 3 * len(gathered))


def _gather_d2d(name, gathered):
    def plan(refs):
        x, y, c, chips = _place()
        return [(g.at[2 * px + py, c], g.at[2 * px + py, c], g.at[2 * px + py, 1 - c], (x, y, 1 - c))
                for g in refs for px, py in chips]

    return _SplitCopies(name, gathered, plan, 3 * len(gathered))


def _scatter_sibling(name, grads):
    n = len(grads)

    def plan(refs):
        x, y, c, _ = _place()
        return [(refs[w].at[1 - c], refs[n + w], refs[n + w], (x, y, 1 - c)) for w in range(n)]

    landing = [lax.empty(g.shape[1:], g.dtype) for g in grads]
    return _SplitCopies(name, [*grads, *landing], plan, n)


def _scatter_chips(name, sums):
    n = len(sums)

    def plan(refs):
        x, y, c, chips = _place()
        return [(refs[w].at[2 * px + py], refs[n + w].at[j], refs[n + w].at[j], (px, py, c))
                for w in range(n) for j, (px, py) in enumerate(chips)]

    landing = [lax.empty((3, *s.shape[1:]), s.dtype) for s in sums]
    return _SplitCopies(name, [*sums, *landing], plan, 3 * n)


def _share_final(name, finals):
    def plan(refs):
        x, y, c, _ = _place()
        return [(f.at[c], f.at[c], f.at[1 - c], (x, y, 1 - c)) for f in refs]

    return _SplitCopies(name, finals, plan, len(finals))


def _row_tile(rows, cols, budget_elems=393216):
    best = 8
    for t in range(8, rows + 1, 8):
        if rows % t == 0 and t * cols <= budget_elems:
            best = t
    return best if rows % best == 0 else rows


def _sum_with_sibling(name, grad, recv, qc_idx):
    _, _, h, cols = grad.shape
    tr = _row_tile(h, cols)

    def body(s_ref, g_ref, r_ref, own_ref, pb_ref):
        p = g_ref[...] + r_ref[...]
        pb_ref[...] = p.astype(BF16)

        @pl.when(pl.program_id(1) == s_ref[0])
        def _():
            own_ref[...] = p

    blk = pl.BlockSpec((None, tr, cols), lambda r, k, s: (k, r, 0))
    return _pcall(
        body, name=name, out_shape=[_sds((h, cols), F32), _sds((N_CHIPS, h, cols), BF16)],
        grid=(h // tr, N_CHIPS), prefetch=1,
        in_specs=[pl.BlockSpec((None, None, tr, cols), lambda r, k, s: (s[1], k, r, 0)), blk],
        out_specs=[pl.BlockSpec((tr, cols), lambda r, k, s: (r, 0)), blk], vmem_mb=32,
    )(qc_idx, grad, recv)


def _sum_chips(name, own, recv, qc_idx):
    h, cols = own.shape
    tr = _row_tile(h, cols)

    def body(s_ref, p_ref, t_ref, o_ref):
        o_ref[...] = ((p_ref[...] + t_ref[0].astype(F32)) + t_ref[1].astype(F32)) + t_ref[2].astype(F32)

    return _pcall(
        body, name=name, out_shape=_sds((2, h, cols), F32), grid=(h // tr,), prefetch=1,
        in_specs=[pl.BlockSpec((tr, cols), lambda r, s: (r, 0)),
                  pl.BlockSpec((3, tr, cols), lambda r, s: (0, r, 0))],
        out_specs=pl.BlockSpec((None, tr, cols), lambda r, s: (s[1], r, 0)), vmem_mb=32,
    )(qc_idx, own, recv)


class _ReduceScatter:
    def __init__(self, tag, names, grads, qc_idx):
        self.tag, self.names, self.n, self.qc_idx = tag, names, len(grads), qc_idx
        self.copies = _scatter_sibling(f"{tag}_rs_sibling", grads)

    def step2(self):
        n = self.n
        arrs = self.copies.wait()
        sums = [_sum_with_sibling(f"{nm}_sum_sibling", arrs[w], arrs[n + w], self.qc_idx)
                for w, nm in enumerate(self.names)]
        self.own = [s[0] for s in sums]
        self.copies = _scatter_chips(f"{self.tag}_rs_chips", [s[1] for s in sums])

    def step3(self):
        n = self.n
        arrs = self.copies.wait()
        finals = [_sum_chips(f"{nm}_sum_chips", self.own[w], arrs[n + w], self.qc_idx)
                  for w, nm in enumerate(self.names)]
        self.copies = _share_final(f"{self.tag}_rs_final", finals)

    def result(self):
        return {nm: f.reshape(2 * f.shape[1], f.shape[2]) for nm, f in zip(self.names, self.copies.wait())}


def _cast_into_gathered(name, w, q_idx):
    rows, cols = w.shape
    h = rows // 2
    tr = _row_tile(h, cols, 1 << 20)
    nr = h // tr

    def body(s_ref, w_ref, o_ref):
        o_ref[...] = w_ref[...].astype(BF16)

    return _pcall(body, name=name, out_shape=_sds((N_CHIPS, 2, h, cols), BF16), grid=(2, nr), prefetch=1,
                  in_specs=[pl.BlockSpec((tr, cols), lambda hf, r, s: (hf * nr + r, 0))],
                  out_specs=pl.BlockSpec((None, None, tr, cols), lambda hf, r, s: (s[0], hf, r, 0)),
                  vmem_mb=32)(q_idx, w)


def _rms(h):
    r = lax.rsqrt(jnp.mean(h * h, axis=-1, keepdims=True) + EPS)
    return r, h * r


def _norm_mod(name, h, g, sc, sh, ts):
    s_len, d = h.shape

    def body(h_ref, g_ref, sc_ref, sh_ref, n_ref):
        _, xhat = _rms(h_ref[...])
        n_ref[...] = ((xhat * g_ref[...]) * (1.0 + sc_ref[...]) + sh_ref[...]).astype(BF16)

    row = pl.BlockSpec((ts, d), lambda i: (i, 0))
    return _pcall(body, name=name, out_shape=_sds((s_len, d), BF16), grid=(s_len // ts,),
                  in_specs=[row, _vec(d), _vec(d), _vec(d)], out_specs=row, vmem_mb=32)(h, g, sc, sh)


def _residual_norm_mod(name, h, f, gate, cmul, g, sc, sh, ts):
    s_len, d = h.shape

    def body(h_ref, f_ref, gt_ref, g_ref, sc_ref, sh_ref, ho_ref, n_ref):
        hn = h_ref[...] + (cmul * gt_ref[...]) * f_ref[...]
        ho_ref[...] = hn
        _, xhat = _rms(hn)
        n_ref[...] = ((xhat * g_ref[...]) * (1.0 + sc_ref[...]) + sh_ref[...]).astype(BF16)

    row = pl.BlockSpec((ts, d), lambda i: (i, 0))
    return _pcall(body, name=name, out_shape=[_sds((s_len, d), F32), _sds((s_len, d), BF16)],
                  grid=(s_len // ts,), in_specs=[row, row, _vec(d), _vec(d), _vec(d), _vec(d)],
                  out_specs=[row, row], vmem_mb=32)(h, f, gate, g, sc, sh)


def _final_loss(name, h, f, tgt, gate, cmul, g, ts):
    s_len, d = h.shape

    def body(h_ref, f_ref, t_ref, gt_ref, g_ref, dh_ref, df_ref, dg_ref, dgt_ref, loss_ref):
        i = pl.program_id(0)
        fv = f_ref[...]
        coef = cmul * gt_ref[...]
        hn = h_ref[...] + coef * fv
        r, xhat = _rms(hn)
        err = xhat * g_ref[...] - t_ref[...]
        _acc_rows(loss_ref, (0.5 / d) * (err * err), i)
        dy = err * (1.0 / d)
        _acc_rows(dg_ref, dy * xhat, i)
        dxhat = dy * g_ref[...]
        dh = r * (dxhat - xhat * jnp.mean(dxhat * xhat, axis=-1, keepdims=True))
        dh_ref[...] = dh
        _acc_rows(dgt_ref, cmul * (dh * fv), i)
        df_ref[...] = (coef * dh).astype(BF16)

    row = pl.BlockSpec((ts, d), lambda i: (i, 0))
    return _pcall(body, name=name,
                  out_shape=[_sds((s_len, d), F32), _sds((s_len, d), BF16)] + [_sds((1, d), F32)] * 3,
                  grid=(s_len // ts,), in_specs=[row, row, row, _vec(d), _vec(d)],
                  out_specs=[row, row, _vec(d), _vec(d), _vec(d)], vmem_mb=40)(h, f, tgt, gate, g)


def _norm_mod_bwd(name, h, dn, dh_next, g, sc, ts, prev=None):
    s_len, d = h.shape
    has_prev = prev is not None
    cmul = prev[2] if has_prev else None

    def body(*refs):
        if has_prev:
            h_ref, dn_ref, dhn_ref, f_ref, g_ref, sc_ref, gt_ref, dh_ref, df_ref, dsh_ref, dsc_ref, dg_ref, dgt_ref = refs
        else:
            h_ref, dn_ref, dhn_ref, g_ref, sc_ref, dh_ref, dsh_ref, dsc_ref, dg_ref = refs
        i = pl.program_id(0)
        r, xhat = _rms(h_ref[...])
        dn_v = dn_ref[...]
        gv = g_ref[...]
        _acc_rows(dsh_ref, dn_v, i)
        _acc_rows(dsc_ref, dn_v * (xhat * gv), i)
        dnrm = dn_v * (1.0 + sc_ref[...])
        _acc_rows(dg_ref, dnrm * xhat, i)
        dxhat = dnrm * gv
        dh = dhn_ref[...] + r * (dxhat - xhat * jnp.mean(dxhat * xhat, axis=-1, keepdims=True))
        dh_ref[...] = dh
        if has_prev:
            _acc_rows(dgt_ref, cmul * (dh * f_ref[...]), i)
            df_ref[...] = ((cmul * gt_ref[...]) * dh).astype(BF16)

    row = pl.BlockSpec((ts, d), lambda i: (i, 0))
    if has_prev:
        ins, in_specs = [h, dn, dh_next, prev[0], g, sc, prev[1]], [row, row, row, row, _vec(d), _vec(d), _vec(d)]
        out_shape = [_sds((s_len, d), F32), _sds((s_len, d), BF16)] + [_sds((1, d), F32)] * 4
        out_specs = [row, row] + [_vec(d)] * 4
    else:
        ins, in_specs = [h, dn, dh_next, g, sc], [row, row, row, _vec(d), _vec(d)]
        out_shape = [_sds((s_len, d), F32)] + [_sds((1, d), F32)] * 3
        out_specs = [row] + [_vec(d)] * 3
    return _pcall(body, name=name, out_shape=out_shape, grid=(s_len // ts,), in_specs=in_specs,
                  out_specs=out_specs, vmem_mb=40)(*ins)


def _cols(ref, lo, hi, npc, rows=slice(None)):
    parts = []
    while lo < hi:
        q, o = divmod(lo, npc)
        n = min(hi - lo, npc - o)
        parts.append(ref[q, rows, o:o + n].astype(F32))
        lo += n
    return parts[0] if len(parts) == 1 else jnp.concatenate(parts, axis=-1)


def _store_cols(ref, lo, val, npc, rows=slice(None)):
    off, width = 0, val.shape[-1]
    while off < width:
        q, o = divmod(lo + off, npc)
        n = min(width - off, npc - o)
        ref[q, rows, o:o + n] = val[:, off:off + n]
        off += n


def _chips_covering(cols, npc):
    return -(-cols // npc)


SUBLANES = 8
ROW_CHUNK = 32


def _make_phases(src_ref, ph_ref):
    rows = src_ref.shape[0] - SUBLANES
    for b in range(1, SUBLANES):
        ph_ref[b - 1] = src_ref[pl.ds(b, rows), :]


def _window(src_ref, ph_ref, off, r0, cols=slice(None)):
    a, b = divmod(off, SUBLANES)
    start = pl.multiple_of(r0 + SUBLANES * a, SUBLANES)
    if b == 0:
        return src_ref[pl.ds(start, ROW_CHUNK), cols]
    return ph_ref[b - 1, pl.ds(start, ROW_CHUNK), cols]


def _phase_scratch(rows, width):
    return pltpu.VMEM((SUBLANES - 1, rows - SUBLANES, width), F32)


def _conv_ln(a0s_ref, a0p_ref, cw_ref, cb_ref, lg_ref, lb_ref, r0):
    a1 = cb_ref[...] + cw_ref[0:1, :] * _window(a0s_ref, a0p_ref, HALO - CONV_K + 1, r0)
    for k in range(1, CONV_K):
        a1 = a1 + cw_ref[k:k + 1, :] * _window(a0s_ref, a0p_ref, HALO - CONV_K + 1 + k, r0)
    mu = jnp.mean(a1, axis=-1, keepdims=True)
    ctr = a1 - mu
    rstd = lax.rsqrt(jnp.mean(ctr * ctr, axis=-1, keepdims=True) + EPS)
    xh = ctr * rstd
    return xh, rstd, xh * lg_ref[...] + lb_ref[...]


def _for_chunks(ts, fn):
    def step(ci, carry):
        fn(pl.multiple_of(ci * ROW_CHUNK, ROW_CHUNK))
        return carry

    lax.fori_loop(0, ts // ROW_CHUNK, step, 0)


def _stage_glu(p_ref, ph_ref, a0s_ref, i, wc, npc, ts):
    a0 = _cols(p_ref, 0, wc, npc) * _sigmoid(_cols(p_ref, wc, 2 * wc, npc))
    a0h = _cols(ph_ref, 0, wc, npc) * _sigmoid(_cols(ph_ref, wc, 2 * wc, npc))
    a0s_ref[0:HALO, :] = jnp.where(i > 0, a0h, 0.0)
    a0s_ref[HALO:HALO + ts, :] = a0


def _mixer_mid(name, proj, cw, cb, lg, lb, wc, wp, ts):
    _, s_len, npc = proj.shape
    nq = _chips_covering(2 * wc + wp, npc)
    gi = wp // len(POOL_WINDOWS)
    hb = ts // HALO

    def body(p_ref, ph_ref, cw_ref, cb_ref, lg_ref, lb_ref, a3_ref, mx_ref, a0s_ref, vs_ref, a0p_ref, vp_ref):
        i = pl.program_id(0)
        _stage_glu(p_ref, ph_ref, a0s_ref, i, wc, npc, ts)
        vs_ref[0:HALO, :] = jnp.where(i > 0, _cols(ph_ref, 2 * wc, 2 * wc + wp, npc), 0.0)
        vs_ref[HALO:HALO + ts, :] = _cols(p_ref, 2 * wc, 2 * wc + wp, npc)
        _make_phases(a0s_ref, a0p_ref)
        _make_phases(vs_ref, vp_ref)

        def chunk(r0):
            rows = pl.ds(r0, ROW_CHUNK)
            _, _, a2 = _conv_ln(a0s_ref, a0p_ref, cw_ref, cb_ref, lg_ref, lb_ref, r0)
            a3_ref[rows, :] = (a2 * _sigmoid(a2)).astype(BF16)
            t_abs = i * ts + r0 + lax.broadcasted_iota(jnp.int32, (ROW_CHUNK, 1), 0)
            for g, win in enumerate(POOL_WINDOWS):
                cs = slice(g * gi, (g + 1) * gi)
                v_now = _window(vs_ref, vp_ref, HALO, r0, cs)
                acc = v_now
                for dlt in range(1, win):
                    acc = acc + _window(vs_ref, vp_ref, HALO - dlt, r0, cs)
                cnt = jnp.minimum(t_abs + 1, win).astype(F32)
                mx_ref[rows, cs] = (acc / cnt - v_now).astype(BF16)

        _for_chunks(ts, chunk)

    return _pcall(
        body, name=name, out_shape=[_sds((s_len, wc), BF16), _sds((s_len, wp), BF16)], grid=(s_len // ts,),
        in_specs=[pl.BlockSpec((nq, ts, npc), lambda i: (0, i, 0)),
                  pl.BlockSpec((nq, HALO, npc), lambda i: (0, jnp.maximum(i * hb - 1, 0), 0)),
                  pl.BlockSpec((HALO, wc), lambda i: (0, 0)), _vec(wc), _vec(wc), _vec(wc)],
        out_specs=[pl.BlockSpec((ts, wc), lambda i: (i, 0)), pl.BlockSpec((ts, wp), lambda i: (i, 0))],
        scratch=[pltpu.VMEM((HALO + ts, wc), F32), pltpu.VMEM((HALO + ts, wp), F32),
                 _phase_scratch(HALO + ts, wc), _phase_scratch(HALO + ts, wp)], vmem_mb=56,
    )(proj, proj, cw, cb, lg, lb)


def _gates_fwd(name, proj, ya, yb, b_a, b_b, ls, wc, wp, ts):
    _, s_len, npc = proj.shape
    d = ya.shape[1]
    g0 = 2 * wc + wp

    def body(p_ref, ya_ref, yb_ref, ba_ref, bb_ref, ls_ref, z_ref):
        ga = _sigmoid(_cols(p_ref, g0, g0 + d, npc))
        gb = _sigmoid(_cols(p_ref, g0 + d, g0 + 2 * d, npc))
        z = ga * (ya_ref[...] + ba_ref[...]) + gb * ((yb_ref[...] + bb_ref[...]) * ls_ref[...])
        z_ref[...] = z.astype(BF16)

    row = pl.BlockSpec((ts, d), lambda i: (i, 0))
    return _pcall(body, name=name, out_shape=_sds((s_len, d), BF16), grid=(s_len // ts,),
                  in_specs=[pl.BlockSpec((N_CHIPS, ts, npc), lambda i: (0, i, 0)), row, row, _vec(d), _vec(d), _vec(d)],
                  out_specs=row, vmem_mb=48)(proj, ya, yb, b_a, b_b, ls)


def _gates_bwd(name, proj, dz, ya, yb, b_a, b_b, ls, wc, wp, ts):
    _, s_len, npc = proj.shape
    d = ya.shape[1]
    g0 = 2 * wc + wp

    def body(p_ref, dz_ref, ya_ref, yb_ref, ba_ref, bb_ref, ls_ref, dya_ref, dyb_ref, dgt_ref, dba_ref, dls_ref,
             dbb_ref):
        i = pl.program_id(0)
        ga = _sigmoid(_cols(p_ref, g0, g0 + d, npc))
        gb = _sigmoid(_cols(p_ref, g0 + d, g0 + 2 * d, npc))
        dz_v = dz_ref[...]
        y_a = ya_ref[...] + ba_ref[...]
        y_b0 = yb_ref[...] + bb_ref[...]
        ls_v = ls_ref[...]
        dya = dz_v * ga
        dya_ref[...] = dya.astype(BF16)
        _acc_rows(dba_ref, dya, i)
        t = dz_v * gb
        _acc_rows(dls_ref, t * y_b0, i)
        dyb = t * ls_v
        dyb_ref[...] = dyb.astype(BF16)
        _acc_rows(dbb_ref, dyb, i)
        dgt_ref[:, 0:d] = (dz_v * y_a * ga * (1.0 - ga)).astype(BF16)
        dgt_ref[:, d:2 * d] = (dz_v * (y_b0 * ls_v) * gb * (1.0 - gb)).astype(BF16)

    row = pl.BlockSpec((ts, d), lambda i: (i, 0))
    return _pcall(
        body, name=name,
        out_shape=[_sds((s_len, d), BF16), _sds((s_len, d), BF16), _sds((s_len, 2 * d), BF16)] + [_sds((1, d), F32)] * 3,
        grid=(s_len // ts,),
        in_specs=[pl.BlockSpec((N_CHIPS, ts, npc), lambda i: (0, i, 0)), row, row, row, _vec(d), _vec(d), _vec(d)],
        out_specs=[row, row, pl.BlockSpec((ts, 2 * d), lambda i: (i, 0))] + [_vec(d)] * 3, vmem_mb=48,
    )(proj, dz, ya, yb, b_a, b_b, ls)


def _conv_branch_bwd(name, proj, da3, cw, cb, lg, lb, wc, wp, ts):
    _, s_len, npc = proj.shape
    nq = _chips_covering(2 * wc, npc)
    hb = ts // HALO

    n_tiles = s_len // ts

    def fold(v):
        return jnp.sum(v.reshape(ROW_CHUNK // SUBLANES, SUBLANES, v.shape[-1]), axis=0)

    def body(p_ref, ph_ref, da3_ref, cw_ref, cb_ref, lg_ref, lb_ref, da1_ref, dlg_ref, dlb_ref, dcb_ref, dcw_ref,
             a0s_ref, a0p_ref, vec8_ref, dcw8_ref):
        i = pl.program_id(0)
        _stage_glu(p_ref, ph_ref, a0s_ref, i, wc, npc, ts)
        _make_phases(a0s_ref, a0p_ref)

        @pl.when(i == 0)
        def _():
            vec8_ref[...] = jnp.zeros_like(vec8_ref)
            dcw8_ref[...] = jnp.zeros_like(dcw8_ref)

        def chunk(r0):
            rows = pl.ds(r0, ROW_CHUNK)
            xh, rstd, a2 = _conv_ln(a0s_ref, a0p_ref, cw_ref, cb_ref, lg_ref, lb_ref, r0)
            sig = _sigmoid(a2)
            da2 = da3_ref[rows, :] * (sig * (1.0 + a2 * (1.0 - sig)))
            vec8_ref[0] += fold(da2 * xh)
            vec8_ref[1] += fold(da2)
            dxh = da2 * lg_ref[...]
            da1 = rstd * (dxh - jnp.mean(dxh, axis=-1, keepdims=True)
                          - xh * jnp.mean(dxh * xh, axis=-1, keepdims=True))
            da1_ref[rows, :] = da1
            vec8_ref[2] += fold(da1)
            for k in range(CONV_K):
                dcw8_ref[k] += fold(da1 * _window(a0s_ref, a0p_ref, HALO - CONV_K + 1 + k, r0))

        _for_chunks(ts, chunk)

        @pl.when(i == n_tiles - 1)
        def _():
            dlg_ref[...] = jnp.sum(vec8_ref[0], axis=0, keepdims=True)
            dlb_ref[...] = jnp.sum(vec8_ref[1], axis=0, keepdims=True)
            dcb_ref[...] = jnp.sum(vec8_ref[2], axis=0, keepdims=True)
            dcw_ref[...] = jnp.sum(dcw8_ref[...], axis=1)

    return _pcall(
        body, name=name,
        out_shape=[_sds((s_len, wc), F32)] + [_sds((1, wc), F32)] * 3 + [_sds((HALO, wc), F32)],
        grid=(s_len // ts,),
        in_specs=[pl.BlockSpec((nq, ts, npc), lambda i: (0, i, 0)),
                  pl.BlockSpec((nq, HALO, npc), lambda i: (0, jnp.maximum(i * hb - 1, 0), 0)),
                  pl.BlockSpec((ts, wc), lambda i: (i, 0)),
                  pl.BlockSpec((HALO, wc), lambda i: (0, 0)), _vec(wc), _vec(wc), _vec(wc)],
        out_specs=[pl.BlockSpec((ts, wc), lambda i: (i, 0)), _vec(wc), _vec(wc), _vec(wc),
                   pl.BlockSpec((HALO, wc), lambda i: (0, 0))],
        scratch=[pltpu.VMEM((HALO + ts, wc), F32), _phase_scratch(HALO + ts, wc),
                 pltpu.VMEM((3, SUBLANES, wc), F32), pltpu.VMEM((HALO, SUBLANES, wc), F32)], vmem_mb=56,
    )(proj, proj, da3, cw, cb, lg, lb)


def _mixer_in_bwd(name, proj, da1, dmixed, dgates, cw, wc, wp, ts):
    _, s_len, npc = proj.shape
    nq = _chips_covering(2 * wc, npc)
    gi = wp // len(POOL_WINDOWS)
    hb = ts // HALO
    n_tiles = s_len // ts
    last_hb = s_len // HALO - 1
    d2 = dgates.shape[1]

    def body(p_ref, d1_ref, d1n_ref, dm_ref, dmn_ref, dgt_ref, cw_ref, o_ref, d1s_ref, es_ref, d1p_ref, ep_ref):
        i = pl.program_id(0)
        more = i < n_tiles - 1
        d1s_ref[0:ts, :] = d1_ref[...]
        d1s_ref[ts:ts + HALO, :] = jnp.where(more, d1n_ref[...], 0.0)
        t_abs = i * ts + lax.broadcasted_iota(jnp.int32, (ts + HALO, 1), 0)
        dm_ext = jnp.concatenate([dm_ref[...], jnp.where(more, dmn_ref[...], 0.0)], axis=0)
        for g, win in enumerate(POOL_WINDOWS):
            cs = slice(g * gi, (g + 1) * gi)
            es_ref[:, cs] = dm_ext[:, cs] / jnp.minimum(t_abs + 1, win).astype(F32)
        _make_phases(d1s_ref, d1p_ref)
        _make_phases(es_ref, ep_ref)

        def chunk(r0):
            rows = pl.ds(r0, ROW_CHUNK)
            da0 = cw_ref[0:1, :] * _window(d1s_ref, d1p_ref, CONV_K - 1, r0)
            for k in range(1, CONV_K):
                da0 = da0 + cw_ref[k:k + 1, :] * _window(d1s_ref, d1p_ref, CONV_K - 1 - k, r0)
            glu_a = _cols(p_ref, 0, wc, npc, rows)
            sig = _sigmoid(_cols(p_ref, wc, 2 * wc, npc, rows))
            _store_cols(o_ref, 0, (da0 * sig).astype(BF16), npc, rows)
            _store_cols(o_ref, wc, (da0 * glu_a * sig * (1.0 - sig)).astype(BF16), npc, rows)
            parts = []
            for g, win in enumerate(POOL_WINDOWS):
                cs = slice(g * gi, (g + 1) * gi)
                acc = _window(es_ref, ep_ref, 0, r0, cs)
                for dlt in range(1, win):
                    acc = acc + _window(es_ref, ep_ref, dlt, r0, cs)
                parts.append(acc - dm_ref[rows, cs])
            _store_cols(o_ref, 2 * wc, jnp.concatenate(parts, axis=-1).astype(BF16), npc, rows)

        _for_chunks(ts, chunk)
        _store_cols(o_ref, 2 * wc + wp, dgt_ref[...], npc)

    nxt = lambda i: (jnp.minimum((i + 1) * hb, last_hb), 0)
    return _pcall(
        body, name=name, out_shape=_sds((N_CHIPS, s_len, npc), BF16), grid=(n_tiles,),
        in_specs=[pl.BlockSpec((nq, ts, npc), lambda i: (0, i, 0)),
                  pl.BlockSpec((ts, wc), lambda i: (i, 0)), pl.BlockSpec((HALO, wc), nxt),
                  pl.BlockSpec((ts, wp), lambda i: (i, 0)), pl.BlockSpec((HALO, wp), nxt),
                  pl.BlockSpec((ts, d2), lambda i: (i, 0)),
                  pl.BlockSpec((HALO, wc), lambda i: (0, 0))],
        out_specs=pl.BlockSpec((N_CHIPS, ts, npc), lambda i: (0, i, 0)),
        scratch=[pltpu.VMEM((ts + HALO, wc), F32), pltpu.VMEM((ts + HALO, wp), F32),
                 _phase_scratch(ts + HALO, wc), _phase_scratch(ts + HALO, wp)], vmem_mb=56,
    )(proj, da1, da1, dmixed, dmixed, dgates, cw)


def _ada_fwd(name, c_all, w, b):
    d, cols = w.shape
    tn = 512 if cols % 512 == 0 else cols

    def body(c_ref, w_ref, b_ref, o_ref):
        cv = c_ref[...]
        sc = (cv * _sigmoid(cv)).astype(BF16)
        o_ref[...] = jnp.dot(sc, w_ref[...].astype(BF16), preferred_element_type=F32) + b_ref[...]

    return _pcall(body, name=name, out_shape=_sds((N_DEV, cols), F32), grid=(cols // tn,),
                  in_specs=[pl.BlockSpec((N_DEV, d), lambda j: (0, 0)), pl.BlockSpec((d, tn), lambda j: (0, j)),
                            pl.BlockSpec((1, tn), lambda j: (0, j))],
                  out_specs=pl.BlockSpec((N_DEV, tn), lambda j: (0, j)), vmem_mb=32)(c_all, w, b)


def _adam_math(w, g, m, v):
    m_new = ADAM_B1 * m + (1.0 - ADAM_B1) * g
    v_new = ADAM_B2 * v + (1.0 - ADAM_B2) * (g * g)
    m_hat = m_new / (1.0 - ADAM_B1 ** ADAM_STEP)
    v_hat = v_new / (1.0 - ADAM_B2 ** ADAM_STEP)
    delta = -ADAM_LR * (m_hat / (jnp.sqrt(v_hat) + ADAM_EPS) + ADAM_WD * w)
    return delta, m_new, v_new


def _adamw(name, w, g, m, v):
    rows, cols = w.shape
    tr = _row_tile(rows, cols, 262144)

    def body(w_ref, g_ref, m_ref, v_ref, go_ref, d_ref, mo_ref, vo_ref):
        g = g_ref[...]
        go_ref[...] = g
        d_ref[...], mo_ref[...], vo_ref[...] = _adam_math(w_ref[...], g, m_ref[...], v_ref[...])

    spec = pl.BlockSpec((tr, cols), lambda i: (i, 0))
    return _pcall(body, name=name, out_shape=[_sds(w.shape, F32)] * 4, grid=(rows // tr,), in_specs=[spec] * 4,
                  out_specs=[spec] * 4, vmem_mb=40)(w, g, m, v)


def _ada_grad_adamw(name, c_t, d_ada, w, m, v):
    rows, cols = w.shape
    tr = _tile(rows, 256)
    tc = _tile(cols, 1536) if cols % 1536 == 0 else cols

    def body(c_ref, da_ref, w_ref, m_ref, v_ref, g_ref, d_ref, mo_ref, vo_ref):
        cv = c_ref[...]
        sc = cv * _sigmoid(cv)
        g = sc[:, 0:1] * da_ref[0:1, :]
        for b in range(1, N_DEV):
            g = g + sc[:, b:b + 1] * da_ref[b:b + 1, :]
        g_ref[...] = g
        d_ref[...], mo_ref[...], vo_ref[...] = _adam_math(w_ref[...], g, m_ref[...], v_ref[...])

    spec = pl.BlockSpec((tr, tc), lambda i, j: (i, j))
    return _pcall(body, name=name, out_shape=[_sds(w.shape, F32)] * 4, grid=(rows // tr, cols // tc),
                  in_specs=[pl.BlockSpec((tr, N_DEV), lambda i, j: (i, 0)),
                            pl.BlockSpec((N_DEV, tc), lambda i, j: (0, j)), spec, spec, spec],
                  out_specs=[spec] * 4, vmem_mb=40)(c_t, d_ada, w, m, v)


def _sum_devices(name, gathered, m_per):
    n = gathered.shape[1]

    def body(g_ref, o_ref):
        acc = g_ref[0:m_per, :]
        for dev in range(1, N_DEV):
            acc = acc + g_ref[dev * m_per:(dev + 1) * m_per, :]
        o_ref[...] = acc

    return _pcall(body, name=name, out_shape=_sds((m_per, n), F32),
                  in_specs=[pl.BlockSpec(memory_space=pltpu.VMEM)],
                  out_specs=pl.BlockSpec(memory_space=pltpu.VMEM))(gathered)


def _ffn_fwd(tag, n, w_in_g, w_out_after_swiglu, dims):
    s_len, d, f_dim = dims["S"], dims["D"], dims["F"]
    p = f_dim // 2
    tf = f_dim // 4
    tm0, tm = _tile(s_len, 512), _tile(s_len, 1024)
    nbp = p // tf

    def ep(accs, ex, outs, cs):
        hh, uu = accs
        outs[0][0, :, cs] = hh.astype(BF16)
        outs[0][1, :, cs] = uu.astype(BF16)
        outs[1][:, cs] = (hh * _sigmoid(hh) * uu).astype(BF16)

    hu, act = _matmul(
        f"{tag}_swiglu", n, [w_in_g, w_in_g], mode="nn", grid=(s_len // tm0, f_dim // tf, 1),
        a_spec=pl.BlockSpec((tm0, d), lambda i, j, k: (i, 0)),
        b_specs=[pl.BlockSpec((None, d, tf), lambda i, j, k: (j // nbp, 0, j % nbp)),
                 pl.BlockSpec((None, d, tf), lambda i, j, k: (2 + j // nbp, 0, j % nbp))],
        out_shape=[_sds((2, s_len, f_dim), BF16), _sds((s_len, f_dim), BF16)],
        out_specs=[pl.BlockSpec((2, tm0, tf), lambda i, j, k: (0, i, j)),
                   pl.BlockSpec((tm0, tf), lambda i, j, k: (i, j))],
        acc_shape=(tm0, tf), epilogue=ep, col_blocks=_col_blocks(tf))
    w_out2d = w_out_after_swiglu()
    tn2 = _tile(d, 1024)
    f = _matmul(
        f"{tag}_down", act, [w_out2d], mode="nn", grid=(s_len // tm, d // tn2, 2),
        a_spec=pl.BlockSpec((tm, 2 * tf), lambda i, j, k: (i, k)),
        b_specs=[pl.BlockSpec((2 * tf, tn2), lambda i, j, k: (k, j))],
        out_shape=_sds((s_len, d), F32), out_specs=pl.BlockSpec((tm, tn2), lambda i, j, k: (i, j)),
        acc_shape=(tm, tn2), epilogue=_ep_store(F32))
    return hu, act, f, w_out2d


def _ffn_bwd(tag, n, hu, act, df, w_in_g, w_out2d, dims, after_dw_out, after_dw_in):
    s_len, d, f_dim = dims["S"], dims["D"], dims["F"]
    tf = f_dim // 4
    tk = _tile(s_len, 2048)
    tn = _tile(d, 1024)
    g_out = _matmul(
        f"{tag}_dw_out", act, [df], mode="tn", grid=(4, d // tn, s_len // tk),
        a_spec=pl.BlockSpec((tk, tf), lambda i, j, k: (k, i)),
        b_specs=[pl.BlockSpec((tk, tn), lambda i, j, k: (k, j))],
        out_shape=_sds((2, 4, tf // 2, d), F32),
        out_specs=pl.BlockSpec((2, None, tf // 2, tn), lambda i, j, k: (0, i, 0, j)),
        acc_shape=(tf, tn), epilogue=_ep_halves(tf // 2))
    after_dw_out(g_out)

    def ep_dhu(accs, ex, outs, cs):
        da = accs[0]
        hh, uu = ex[0][0, :, cs].astype(F32), ex[0][1, :, cs].astype(F32)
        sig = _sigmoid(hh)
        outs[0][0, :, cs] = (da * uu * (sig * (1.0 + hh * (1.0 - sig)))).astype(BF16)
        outs[0][1, :, cs] = (da * (hh * sig)).astype(BF16)

    tm = _tile(s_len, 512)
    hu_spec = pl.BlockSpec((2, tm, tf), lambda i, j, k: (0, i, j))
    dhu = _matmul(
        f"{tag}_dhu", df, [w_out2d], mode="nt", grid=(s_len // tm, 4, 1),
        a_spec=pl.BlockSpec((tm, d), lambda i, j, k: (i, 0)),
        b_specs=[pl.BlockSpec((tf, d), lambda i, j, k: (j, 0))],
        extras=[hu], extra_specs=[hu_spec],
        out_shape=_sds((2, s_len, f_dim), BF16), out_specs=hu_spec, acc_shape=(tm, tf), epilogue=ep_dhu,
        col_blocks=_col_blocks(tf))

    hd = d // 2
    g_in = _matmul(
        f"{tag}_dw_in", n, [dhu], mode="tn", grid=(2, 8, s_len // tk),
        a_spec=pl.BlockSpec((tk, hd), lambda i, j, k: (k, i)),
        b_specs=[pl.BlockSpec((None, tk, tf), lambda i, j, k: (j // 4, k, j % 4))],
        out_shape=_sds((2, 4, hd, f_dim // 2), F32),
        out_specs=pl.BlockSpec((None, None, hd, tf), lambda i, j, k: (i, j // 2, 0, j % 2)),
        acc_shape=(hd, tf), epilogue=_ep_store(F32))
    after_dw_in(g_in)

    tm2 = _tile(s_len, 1024)
    dn = _matmul(
        f"{tag}_dn", dhu, [w_in_g], mode="nt", grid=(s_len // tm2, d // tn, 4),
        a_spec=pl.BlockSpec((None, tm2, 2 * tf), lambda i, j, k: (k // 2, i, k % 2)),
        b_specs=[pl.BlockSpec((None, tn, 2 * tf), lambda i, j, k: (k, j, 0))],
        out_shape=_sds((s_len, d), F32), out_specs=pl.BlockSpec((tm2, tn), lambda i, j, k: (i, j)),
        acc_shape=(tm2, tn), epilogue=_ep_store(F32))
    return dn


def kernel(x, c, w_ada, b_ada, g_ffn1, w1_in, w1_out, g_mix, w_in, conv_w, conv_b, ln_a_g, ln_a_b, w_a_out, b_a_out, w_b_group, b_b_group, ls_b, w_out, g_ffn2, w2_in, w2_out, g_final, loss_target, m_w_ada, m_b_ada, m_g_ffn1, m_w1_in, m_w1_out, m_g_mix, m_w_in, m_conv_w, m_conv_b, m_ln_a_g, m_ln_a_b, m_w_a_out, m_b_a_out, m_w_b_group, m_b_b_group, m_ls_b, m_w_out, m_g_ffn2, m_w2_in, m_w2_out, m_g_final, v_w_ada, v_b_ada, v_g_ffn1, v_w1_in, v_w1_out, v_g_mix, v_w_in, v_conv_w, v_conv_b, v_ln_a_g, v_ln_a_b, v_w_a_out, v_b_a_out, v_w_b_group, v_b_b_group, v_ls_b, v_w_out, v_g_ffn2, v_w2_in, v_w2_out, v_g_final):
    weights = dict(w_ada=w_ada, b_ada=b_ada, g_ffn1=g_ffn1, w1_in=w1_in, w1_out=w1_out, g_mix=g_mix, w_in=w_in,
                   conv_w=conv_w, conv_b=conv_b, ln_a_g=ln_a_g, ln_a_b=ln_a_b, w_a_out=w_a_out, b_a_out=b_a_out,
                   w_b_group=w_b_group, b_b_group=b_b_group, ls_b=ls_b, w_out=w_out, g_ffn2=g_ffn2, w2_in=w2_in,
                   w2_out=w2_out, g_final=g_final)
    mom1 = dict(w_ada=m_w_ada, b_ada=m_b_ada, g_ffn1=m_g_ffn1, w1_in=m_w1_in, w1_out=m_w1_out, g_mix=m_g_mix,
                w_in=m_w_in, conv_w=m_conv_w, conv_b=m_conv_b, ln_a_g=m_ln_a_g, ln_a_b=m_ln_a_b, w_a_out=m_w_a_out,
                b_a_out=m_b_a_out, w_b_group=m_w_b_group, b_b_group=m_b_b_group, ls_b=m_ls_b, w_out=m_w_out,
                g_ffn2=m_g_ffn2, w2_in=m_w2_in, w2_out=m_w2_out, g_final=m_g_final)
    mom2 = dict(w_ada=v_w_ada, b_ada=v_b_ada, g_ffn1=v_g_ffn1, w1_in=v_w1_in, w1_out=v_w1_out, g_mix=v_g_mix,
                w_in=v_w_in, conv_w=v_conv_w, conv_b=v_conv_b, ln_a_g=v_ln_a_g, ln_a_b=v_ln_a_b, w_a_out=v_w_a_out,
                b_a_out=v_b_a_out, w_b_group=v_w_b_group, b_b_group=v_b_b_group, ls_b=v_ls_b, w_out=v_w_out,
                g_ffn2=v_g_ffn2, w2_in=v_w2_in, w2_out=v_w2_out, g_final=v_g_final)
    order = list(weights)

    s_len, d = x.shape[1], x.shape[2]
    f_dim = w1_out.shape[0] * N_CHIPS
    wc = conv_w.shape[1] * N_CHIPS
    wp = w_b_group.shape[0] * w_b_group.shape[1]
    n_groups, gi, goq = w_b_group.shape
    npc = w_in.shape[1]
    ada_c = w_ada.shape[1]
    dims = dict(S=s_len, D=d, F=f_dim)
    ts = _tile(s_len, 256)

    xi, yi, ci = lax.axis_index("x"), lax.axis_index("y"), lax.axis_index("c")
    q = 2 * xi + yi
    dev = 2 * q + ci
    q_idx = jnp.reshape(q, (1,)).astype(jnp.int32)
    qc_idx = jnp.stack([q, ci]).astype(jnp.int32)
    _PREVIOUS.clear()

    cwq = conv_w.shape[1]
    pack0 = jnp.concatenate([c.reshape(-1), conv_w.reshape(-1), b_b_group.reshape(-1)])
    n0 = -(-pack0.shape[0] // (8 * LANES)) * LANES
    pack0 = jnp.pad(pack0, (0, 8 * n0 - pack0.shape[0])).reshape(8, n0)
    g0 = _allgather_small("gather_small_in", pack0).reshape(N_DEV, 8 * n0)
    c_all = g0[:, :d]
    south = g0[0::2]
    cw_full = jnp.concatenate([south[k, d:d + CONV_K * cwq].reshape(CONV_K, cwq) for k in range(N_CHIPS)], axis=1)
    cw_pad = jnp.pad(cw_full, ((0, HALO - CONV_K), (0, 0)))
    o_bb = d + CONV_K * cwq
    bb_full = jnp.concatenate([south[k, o_bb:o_bb + n_groups * goq].reshape(n_groups, goq) for k in range(N_CHIPS)],
                              axis=1).reshape(1, d)

    as2d = lambda a: a.reshape(-1, a.shape[-1])
    groups = dict(w1_in=["w1_in"], w1_out=["w1_out"], mix=["w_in", "w_a_out", "w_b_group", "w_out"],
                  ffn2=["w2_in", "w2_out"])
    big = [nm for grp in groups.values() for nm in grp]
    casts = {nm: _cast_into_gathered(f"cast_{nm}", as2d(weights[nm]), q_idx) for nm in groups["w1_in"]}
    ici = dict(w1_in=_gather_ici("gather_w1_in_ici", [casts[nm] for nm in groups["w1_in"]]))

    b_ada_mine = lax.dynamic_slice(b_ada, (q * ada_c,), (ada_c,)).reshape(1, ada_c)
    ada_piece = _ada_fwd("ada_fwd", c_all, w_ada, b_ada_mine)
    casts.update({nm: _cast_into_gathered(f"cast_{nm}", as2d(weights[nm]), q_idx) for nm in big[1:]})
    g1 = _allgather_small("gather_ada", ada_piece).reshape(N_DEV, N_DEV, ada_c)
    for grp in ("w1_out", "mix", "ffn2"):
        ici[grp] = _gather_ici(f"gather_{grp}_ici", [casts[nm] for nm in groups[grp]])
    ada_rows = lax.dynamic_index_in_dim(g1[0::2], dev, axis=1, keepdims=False)
    ada = ada_rows.reshape(3, 3, 1, d)
    (sh1, sc1, gt1), (sh2, sc2, gt2), (sh3, sc3, gt3) = [[ada[i, j] for j in range(3)] for i in range(3)]

    row = lambda vct: vct.reshape(1, -1)
    g1v, gmv, g2v, gfv = row(g_ffn1), row(g_mix), row(g_ffn2), row(g_final)

    def arrived(grp):
        return _gather_d2d(f"gather_{grp}_d2d", ici[grp].wait())

    def gathered(fwd, grp):
        return {nm: g.reshape(N_CHIPS, 2 * g.shape[2], g.shape[3]) for nm, g in zip(groups[grp], fwd.wait())}

    x2 = x[0]
    tgt = loss_target[0]

    n1 = _norm_mod("ffn1_norm", x2, g1v, sc1, sh1, ts)
    w1_in_g = gathered(arrived("w1_in"), "w1_in")["w1_in"]
    fwd = {}

    def w1_out_after_swiglu():
        fwd["w1_out"] = arrived("w1_out")
        fwd["mix"] = arrived("mix")
        return gathered(fwd["w1_out"], "w1_out")["w1_out"].reshape(f_dim, d)

    hu1, act1, f1, w1_out_2d = _ffn_fwd("ffn1", n1, w1_in_g, w1_out_after_swiglu, dims)
    h1, n2 = _residual_norm_mod("mix_norm", x2, f1, gt1, 0.5, gmv, sc2, sh2, ts)
    wts = gathered(fwd["mix"], "mix")
    w_in_g, w_out_2d = wts["w_in"], wts["w_out"].reshape(d, d)
    w_a_g = wts["w_a_out"]
    w_b_g = wts["w_b_group"]

    tm = _tile(s_len, 1024)
    tnp = npc // 2
    proj = _matmul(
        "mix_proj", n2, [w_in_g], mode="nn", grid=(s_len // tm, 8, 1),
        a_spec=pl.BlockSpec((tm, d), lambda i, j, k: (i, 0)),
        b_specs=[pl.BlockSpec((None, d, tnp), lambda i, j, k: (j // 2, 0, j % 2))],
        out_shape=_sds((N_CHIPS, s_len, npc), BF16),
        out_specs=pl.BlockSpec((None, tm, tnp), lambda i, j, k: (j // 2, i, j % 2)),
        acc_shape=(tm, tnp), epilogue=_ep_store(BF16))
    fwd["ffn2"] = arrived("ffn2")
    cbv, lgv, lbv = row(conv_b), row(ln_a_g), row(ln_a_b)
    a3, mixed = _mixer_mid("mix_mid", proj, cw_pad, cbv, lgv, lbv, wc, wp, ts)
    dq = d // N_CHIPS
    ya = _matmul(
        "mix_ya", a3, [w_a_g], mode="nn", grid=(s_len // tm, N_CHIPS, 1),
        a_spec=pl.BlockSpec((tm, wc), lambda i, j, k: (i, 0)),
        b_specs=[pl.BlockSpec((None, wc, dq), lambda i, j, k: (j, 0, 0))],
        out_shape=_sds((s_len, d), F32), out_specs=pl.BlockSpec((tm, dq), lambda i, j, k: (i, j)),
        acc_shape=(tm, dq), epilogue=_ep_store(F32))
    yb = _matmul(
        "mix_yb", mixed, [w_b_g], mode="nn", grid=(s_len // tm, n_groups * N_CHIPS, 1),
        a_spec=pl.BlockSpec((tm, gi), lambda i, j, k: (i, j // N_CHIPS)),
        b_specs=[pl.BlockSpec((None, gi, goq), lambda i, j, k: (j % N_CHIPS, j // N_CHIPS, 0))],
        out_shape=_sds((s_len, d), F32), out_specs=pl.BlockSpec((tm, goq), lambda i, j, k: (i, j)),
        acc_shape=(tm, goq), epilogue=_ep_store(F32))
    bav, lsv = row(b_a_out), row(ls_b)
    z = _gates_fwd("mix_gates", proj, ya, yb, bav, bb_full, lsv, wc, wp, ts)
    tn = _tile(d, 1024)
    mix = _matmul(
        "mix_out", z, [w_out_2d], mode="nn", grid=(s_len // tm, d // tn, 1),
        a_spec=pl.BlockSpec((tm, d), lambda i, j, k: (i, 0)),
        b_specs=[pl.BlockSpec((d, tn), lambda i, j, k: (0, j))],
        out_shape=_sds((s_len, d), F32), out_specs=pl.BlockSpec((tm, tn), lambda i, j, k: (i, j)),
        acc_shape=(tm, tn), epilogue=_ep_store(F32))
    h2, n3 = _residual_norm_mod("ffn2_norm", h1, mix, gt2, 1.0, g2v, sc3, sh3, ts)
    wts = gathered(fwd["ffn2"], "ffn2")
    w2_in_g = wts["w2_in"]
    hu2, act2, f3, w2_out_2d = _ffn_fwd("ffn2", n3, w2_in_g, lambda: wts["w2_out"].reshape(f_dim, d), dims)

    dh3, df3, d_gf, d_gt3, loss_cols = _final_loss("final_loss", h2, f3, tgt, gt3, 0.5, gfv, ts)
    rs, held = {}, {}
    dn3 = _ffn_bwd(
        "ffn2", n3, hu2, act2, df3, w2_in_g, w2_out_2d, dims,
        after_dw_out=lambda g: held.update(w2_out=g),
        after_dw_in=lambda g: rs.update(ffn2=_ReduceScatter("g_ffn2", ["w2_out", "w2_in"], [held["w2_out"], g],
                                                            qc_idx)))
    dh2, dmix, d_sh3, d_sc3, d_g2, d_gt2 = _norm_mod_bwd("ffn2_norm_bwd", h2, dn3, dh3, g2v, sc3, ts,
                                                         prev=(mix, gt2, 1.0))
    rs["ffn2"].step2()

    tk = s_len
    hq = d // (2 * N_CHIPS)
    gw_out = _matmul(
        "mix_dw_out", z, [dmix], mode="tn", grid=(N_CHIPS, d // tn, s_len // tk),
        a_spec=pl.BlockSpec((tk, 2 * hq), lambda i, j, k: (k, i)),
        b_specs=[pl.BlockSpec((tk, tn), lambda i, j, k: (k, j))],
        out_shape=_sds((2, N_CHIPS, hq, d), F32),
        out_specs=pl.BlockSpec((2, None, hq, tn), lambda i, j, k: (0, i, 0, j)),
        acc_shape=(2 * hq, tn), epilogue=_ep_halves(hq))
    dz = _matmul(
        "mix_dz", dmix, [w_out_2d], mode="nt", grid=(s_len // tm, d // tn, 1),
        a_spec=pl.BlockSpec((tm, d), lambda i, j, k: (i, 0)),
        b_specs=[pl.BlockSpec((tn, d), lambda i, j, k: (j, 0))],
        out_shape=_sds((s_len, d), F32), out_specs=pl.BlockSpec((tm, tn), lambda i, j, k: (i, j)),
        acc_shape=(tm, tn), epilogue=_ep_store(F32))
    dya, dyb, dgates, d_ba, d_ls, d_bb = _gates_bwd("mix_gates_bwd", proj, dz, ya, yb, bav, bb_full, lsv, wc, wp, ts)
    gw_a = _matmul(
        "mix_dw_a", a3, [dya], mode="tn", grid=(1, N_CHIPS, s_len // tk),
        a_spec=pl.BlockSpec((tk, wc), lambda i, j, k: (k, 0)),
        b_specs=[pl.BlockSpec((tk, dq), lambda i, j, k: (k, j))],
        out_shape=_sds((2, N_CHIPS, wc // 2, dq), F32),
        out_specs=pl.BlockSpec((2, None, wc // 2, dq), lambda i, j, k: (0, j, 0, 0)),
        acc_shape=(wc, dq), epilogue=_ep_halves(wc // 2))
    da3 = _matmul(
        "mix_da3", dya, [w_a_g], mode="nt", grid=(s_len // tm, 1, N_CHIPS),
        a_spec=pl.BlockSpec((tm, dq), lambda i, j, k: (i, k)),
        b_specs=[pl.BlockSpec((None, wc, dq), lambda i, j, k: (k, 0, 0))],
        out_shape=_sds((s_len, wc), F32), out_specs=pl.BlockSpec((tm, wc), lambda i, j, k: (i, 0)),
        acc_shape=(tm, wc), epilogue=_ep_store(F32))
    gpr = n_groups // 2
    gw_b = _matmul(
        "mix_dw_b", mixed, [dyb], mode="tn", grid=(1, n_groups * N_CHIPS, s_len // tk),
        a_spec=pl.BlockSpec((tk, gi), lambda i, j, k: (k, j // N_CHIPS)),
        b_specs=[pl.BlockSpec((tk, goq), lambda i, j, k: (k, j))],
        out_shape=_sds((2, N_CHIPS, gpr * gi, goq), F32),
        out_specs=pl.BlockSpec((None, None, gi, goq),
                               lambda i, j, k: ((j // N_CHIPS) // gpr, j % N_CHIPS, (j // N_CHIPS) % gpr, 0)),
        acc_shape=(gi, goq), epilogue=_ep_store(F32))
    dmixed = _matmul(
        "mix_dmixed", dyb, [w_b_g], mode="nt", grid=(s_len // tm, n_groups, N_CHIPS),
        a_spec=pl.BlockSpec((tm, goq), lambda i, j, k: (i, j * N_CHIPS + k)),
        b_specs=[pl.BlockSpec((None, gi, goq), lambda i, j, k: (k, j, 0))],
        out_shape=_sds((s_len, wp), F32), out_specs=pl.BlockSpec((tm, gi), lambda i, j, k: (i, j)),
        acc_shape=(tm, gi), epilogue=_ep_store(F32))
    da1, d_lg, d_lb, d_cb, d_cw = _conv_branch_bwd("mix_conv_bwd", proj, da3, cw_pad, cbv, lgv, lbv, wc, wp, ts)
    dproj = _mixer_in_bwd("mix_in_bwd", proj, da1, dmixed, dgates, cw_pad, wc, wp, ts)
    hd = d // 2
    gw_in = _matmul(
        "mix_dw_in", n2, [dproj], mode="tn", grid=(2, 8, s_len // tk),
        a_spec=pl.BlockSpec((tk, hd), lambda i, j, k: (k, i)),
        b_specs=[pl.BlockSpec((None, tk, tnp), lambda i, j, k: (j // 2, k, j % 2))],
        out_shape=_sds((2, N_CHIPS, hd, npc), F32),
        out_specs=pl.BlockSpec((None, None, hd, tnp), lambda i, j, k: (i, j // 2, 0, j % 2)),
        acc_shape=(hd, tnp), epilogue=_ep_store(F32))
    rs["mix"] = _ReduceScatter("g_mix", groups["mix"], [gw_in, gw_a, gw_b, gw_out], qc_idx)
    rs["ffn2"].step3()
    dn2 = _matmul(
        "mix_dn", dproj, [w_in_g], mode="nt", grid=(s_len // tm, d // tn, N_CHIPS),
        a_spec=pl.BlockSpec((None, tm, npc), lambda i, j, k: (k, i, 0)),
        b_specs=[pl.BlockSpec((None, tn, npc), lambda i, j, k: (k, j, 0))],
        out_shape=_sds((s_len, d), F32), out_specs=pl.BlockSpec((tm, tn), lambda i, j, k: (i, j)),
        acc_shape=(tm, tn), epilogue=_ep_store(F32))
    dh1, df1, d_sh2, d_sc2, d_gm, d_gt1 = _norm_mod_bwd("mix_norm_bwd", h1, dn2, dh2, gmv, sc2, ts,
                                                        prev=(f1, gt1, 0.5))
    rs["mix"].step2()

    def w1_in_ready(g):
        rs["w1_in"] = _ReduceScatter("g_w1_in", ["w1_in"], [g], qc_idx)
        rs["w1_out"].step2()
        rs["mix"].step3()

    dn1 = _ffn_bwd(
        "ffn1", n1, hu1, act1, df1, w1_in_g, w1_out_2d, dims,
        after_dw_out=lambda g: rs.update(w1_out=_ReduceScatter("g_w1_out", ["w1_out"], [g], qc_idx)),
        after_dw_in=w1_in_ready)
    grad_x, d_sh1, d_sc1, d_g1 = _norm_mod_bwd("ffn1_norm_bwd", x2, dn1, dh1, g1v, sc1, ts)

    d_ada = jnp.concatenate([d_sh1, d_sc1, d_gt1, d_sh2, d_sc2, d_gt2, d_sh3, d_sc3, d_gt3], axis=1)
    small = [d_ada, d_g1, d_gm, d_cw[:CONV_K].reshape(1, -1), d_cb, d_lg, d_lb, d_ba, d_bb, d_ls, d_g2, d_gf,
             loss_cols]
    sizes = [a.shape[1] for a in small]
    pack1 = jnp.concatenate(small, axis=1).reshape(-1)
    n1p = -(-pack1.shape[0] // (8 * LANES)) * LANES
    pack1 = jnp.pad(pack1, (0, 8 * n1p - pack1.shape[0])).reshape(8, n1p)
    g2 = _allgather_small("gather_small_grads", pack1)
    rs["w1_in"].step2()
    total = _sum_devices("sum_small_grads", g2, 8).reshape(-1)
    offs = [0]
    for sz in sizes:
        offs.append(offs[-1] + sz)
    tot = [total[offs[k]:offs[k + 1]] for k in range(len(sizes))]
    d_ada_all = g2.reshape(N_DEV, 8 * n1p)[:, :sizes[0]]
    loss = jnp.sum(tot[12])

    grads = {}
    grads["b_ada"] = tot[0]
    grads["g_ffn1"], grads["g_mix"] = tot[1], tot[2]
    grads["conv_w"] = lax.dynamic_slice(tot[3].reshape(CONV_K, wc), (0, q * cwq), (CONV_K, cwq))
    grads["conv_b"], grads["ln_a_g"], grads["ln_a_b"], grads["b_a_out"] = tot[4], tot[5], tot[6], tot[7]
    grads["b_b_group"] = lax.dynamic_slice(tot[8].reshape(n_groups, N_CHIPS * goq), (0, q * goq), (n_groups, goq))
    grads["ls_b"], grads["g_ffn2"], grads["g_final"] = tot[9], tot[10], tot[11]

    delta, new_m, new_v = {}, {}, {}

    def adamw_group(reduced):
        for nm, g in reduced.items():
            shp = weights[nm].shape
            go, dl, mo, vo = _adamw(f"adamw_{nm}", as2d(weights[nm]), g, as2d(mom1[nm]), as2d(mom2[nm]))
            grads[nm], delta[nm], new_m[nm], new_v[nm] = go.reshape(shp), dl.reshape(shp), mo.reshape(shp), vo.reshape(shp)

    adamw_group(rs["ffn2"].result())
    rs["w1_out"].step3()
    adamw_group(rs["mix"].result())
    d_ada_mine = lax.dynamic_slice(d_ada_all, (0, q * ada_c), (N_DEV, ada_c))
    grads["w_ada"], delta["w_ada"], new_m["w_ada"], new_v["w_ada"] = _ada_grad_adamw(
        "adamw_w_ada", c_all.T, d_ada_mine, w_ada, m_w_ada, v_w_ada)
    rs["w1_in"].step3()
    smalls = [nm for nm in order if nm not in big and nm != "w_ada"]
    flat = lambda src: jnp.concatenate([src[nm].reshape(-1) for nm in smalls])
    n_small = sum(weights[nm].size for nm in smalls)
    rows_s = -(-n_small // (8 * LANES)) * 8
    packed = [jnp.pad(flat(src), (0, rows_s * LANES - n_small)).reshape(rows_s, LANES)
              for src in (weights, grads, mom1, mom2)]
    _, dl_s, mo_s, vo_s = _adamw("adamw_small", *packed)
    off = 0
    for nm in smalls:
        sz, shp = weights[nm].size, weights[nm].shape
        delta[nm] = dl_s.reshape(-1)[off:off + sz].reshape(shp)
        new_m[nm] = mo_s.reshape(-1)[off:off + sz].reshape(shp)
        new_v[nm] = vo_s.reshape(-1)[off:off + sz].reshape(shp)
        grads[nm] = grads[nm].reshape(shp)
        off += sz
    adamw_group(rs["w1_out"].result())
    adamw_group(rs["w1_in"].result())

    return (loss, grad_x[None], *[grads[nm] for nm in order], *[delta[nm] for nm in order],
            *[new_m[nm] for nm in order], *[new_v[nm] for nm in order])
```

```python
import jax
import jax.numpy as jnp
from jax import lax
from jax.experimental import pallas as pl
from jax.experimental.pallas import tpu as pltpu

F32 = jnp.float32
BF16 = jnp.bfloat16
MESH = pl.DeviceIdType.MESH
ANY = pl.BlockSpec(memory_space=pl.ANY)
HBM = pl.BlockSpec(memory_space=pltpu.HBM)
SEM = pl.BlockSpec(memory_space=pltpu.SEMAPHORE)
EFFECT = pltpu.SideEffectType.DATAFLOW_SIDE_EFFECTING

EPS = 1e-6
CONV_K = 31
HALO = 32
POOL_WINDOWS = (2, 4, 8, 16)
N_CHIPS = 4
N_DEV = 8
LANES = 128

ADAM_LR = 0.001
ADAM_B1 = 0.9
ADAM_B2 = 0.999
ADAM_EPS = 1e-08
ADAM_WD = 0.01
ADAM_STEP = 10

DN = {
    "nn": (((1,), (0,)), ((), ())),
    "nt": (((1,), (1,)), ((), ())),
    "tn": (((0,), (0,)), ((), ())),
}


_PREVIOUS = []


def _ordered(call, args, n_lead, body, token=None, sources=()):
    dep = [pltpu.with_memory_space_constraint(p, pltpu.HBM) if p.size * p.dtype.itemsize >= (1 << 20) else p
           for p in _PREVIOUS if all(p is not a for a in (*args, *sources))]

    def wrapped(*refs):
        return body(*refs[:n_lead], *refs[n_lead + len(dep):])

    outs = call(wrapped, [ANY] * len(dep))(*args, *dep)
    seq = outs if isinstance(outs, (list, tuple)) else [outs]
    _PREVIOUS[:] = [seq[token] if token is not None else
                    next(o for o in seq if jnp.issubdtype(o.dtype, jnp.floating))]
    return outs


def _pcall(body, *, name, out_shape, grid=None, in_specs=None, out_specs=None, scratch=(), aliases=None,
           prefetch=0, vmem_mb=None):
    params = {}
    if grid is not None:
        params["dimension_semantics"] = ("arbitrary",) * len(grid)
    if vmem_mb is not None:
        params["vmem_limit_bytes"] = vmem_mb << 20
    kw = dict(name=name, out_shape=out_shape, compiler_params=pltpu.CompilerParams(**params))
    if aliases:
        kw["input_output_aliases"] = aliases

    def call(wrapped, dep_specs):
        specs = list(in_specs) + dep_specs
        if prefetch:
            return pl.pallas_call(wrapped, grid_spec=pltpu.PrefetchScalarGridSpec(
                num_scalar_prefetch=prefetch, grid=grid, in_specs=specs, out_specs=out_specs,
                scratch_shapes=list(scratch)), **kw)
        if grid is not None:
            return pl.pallas_call(wrapped, grid=grid, in_specs=specs, out_specs=out_specs,
                                  scratch_shapes=list(scratch), **kw)
        return pl.pallas_call(wrapped, in_specs=specs, out_specs=out_specs, scratch_shapes=list(scratch), **kw)

    def run(*args):
        specs = [None] * prefetch + list(in_specs)
        placed = [pltpu.with_memory_space_constraint(a, pltpu.HBM)
                  if a.size * a.dtype.itemsize >= (1 << 20) and getattr(s, "memory_space", None) != pltpu.VMEM else a
                  for a, s in zip(args, specs)]
        return _ordered(call, placed, prefetch + len(in_specs), body, sources=args)

    return run


def _tile(dim, pref):
    t = min(dim, pref)
    assert dim % t == 0, (dim, pref)
    return t


def _sds(shape, dtype):
    return jax.ShapeDtypeStruct(tuple(shape), dtype)


def _sigmoid(v):
    return 0.5 * jnp.tanh(0.5 * v) + 0.5


def _vec(w):
    return pl.BlockSpec((1, w), lambda *_: (0, 0))


def _acc_rows(ref, val, i):
    @pl.when(i == 0)
    def _():
        ref[...] = jnp.zeros_like(ref)

    ref[...] += jnp.sum(val, axis=0, keepdims=True)


def _matmul(name, a, bs, *, mode, grid, a_spec, b_specs, out_shape, out_specs, acc_shape, epilogue,
            extras=(), extra_specs=(), vmem_mb=56):
    nb, ne, nk = len(bs), len(extras), grid[2]
    dn = DN[mode]

    def body(*refs):
        a_ref, b_refs, ex = refs[0], refs[1:1 + nb], refs[1 + nb:1 + nb + ne]
        if nk == 1:
            outs = refs[1 + nb + ne:]
            accs = [lax.dot_general(a_ref[...], b[...], dn, preferred_element_type=F32) for b in b_refs]
            epilogue(accs, ex, outs)
            return
        outs, acc_refs = refs[1 + nb + ne:-nb], refs[-nb:]
        k = pl.program_id(2)

        @pl.when(k == 0)
        def _():
            for acc in acc_refs:
                acc[...] = jnp.zeros_like(acc)

        for acc, b in zip(acc_refs, b_refs):
            acc[...] += lax.dot_general(a_ref[...], b[...], dn, preferred_element_type=F32)

        @pl.when(k == nk - 1)
        def _():
            epilogue([acc[...] for acc in acc_refs], ex, outs)

    scratch = [pltpu.VMEM(acc_shape, F32) for _ in range(nb)] if nk > 1 else []
    return _pcall(body, name=name, out_shape=out_shape, grid=grid,
                  in_specs=[a_spec, *b_specs, *extra_specs], out_specs=out_specs, scratch=scratch,
                  vmem_mb=vmem_mb)(a, *bs, *extras)


def _ep_store(dtype):
    def ep(accs, ex, outs):
        outs[0][...] = accs[0].astype(dtype)
    return ep


def _ep_halves(h):
    def ep(accs, ex, outs):
        outs[0][0] = accs[0][:h]
        outs[0][1] = accs[0][h:]
    return ep


def _place():
    x, y, c = lax.axis_index("x"), lax.axis_index("y"), lax.axis_index("c")
    chips = [(1 - x, y), (x, 1 - y), (1 - x, 1 - y)]
    return x, y, c, chips


def _allgather_small(name, block):
    m_per, n = block.shape

    def body(x_ref, out_ref, send_sems, recv_sems, local_sem):
        x, y, c, chips = _place()
        me, sibling = (x, y, c), (x, y, 1 - c)

        def rows(px, py, pc):
            return out_ref.at[pl.ds((4 * px + 2 * py + pc) * m_per, m_per), :]

        def copy(k, blk, to, src=None):
            return pltpu.make_async_remote_copy(
                src_ref=rows(*blk) if src is None else src, dst_ref=rows(*blk),
                send_sem=send_sems.at[k], recv_sem=recv_sems.at[k], device_id=to, device_id_type=MESH)

        mine = pltpu.make_async_copy(x_ref, rows(*me), local_sem)
        mine.start()
        first = [copy(0, me, sibling, src=x_ref)]
        first += [copy(1 + j, me, (*chip, c), src=x_ref) for j, chip in enumerate(chips)]
        for cp in first:
            cp.start()
        passed = [copy(4 + j, (*chip, c), sibling) for j, chip in enumerate(chips)]
        for j, chip in enumerate(chips):
            copy(1 + j, (*chip, c), me).wait_recv()
            passed[j].start()
        copy(0, sibling, me).wait_recv()
        for j, chip in enumerate(chips):
            copy(4 + j, (*chip, 1 - c), me).wait_recv()
        for cp in first + passed:
            cp.wait_send()
        mine.wait()

    return _pcall(
        body, name=name, out_shape=_sds((N_DEV * m_per, n), block.dtype),
        in_specs=[pl.BlockSpec(memory_space=pltpu.VMEM)], out_specs=pl.BlockSpec(memory_space=pltpu.VMEM),
        scratch=[pltpu.SemaphoreType.DMA((7,)), pltpu.SemaphoreType.DMA((7,)), pltpu.SemaphoreType.DMA],
    )(block)


class _SplitCopies:
    def __init__(self, name, arrays, plan, n_copies):
        self.name, self.plan, self.n = name, plan, len(arrays)
        n = self.n

        def body(*refs):
            send, recv, token = refs[n], refs[n + 1], refs[-1]
            for k, (src, dst, _, peer) in enumerate(plan(refs[:n])):
                pltpu.make_async_remote_copy(src_ref=src, dst_ref=dst, send_sem=send.at[k], recv_sem=recv.at[k],
                                             device_id=peer, device_id_type=MESH).start()
            token[...] = jnp.zeros_like(token)

        def call(wrapped, dep_specs):
            return pl.pallas_call(
                wrapped, name=f"{name}_start",
                out_shape=(pltpu.SemaphoreType.DMA((n_copies,)), pltpu.SemaphoreType.DMA((n_copies,)),
                           *[pltpu.HBM(a.shape, a.dtype) for a in arrays], _sds((8, LANES), F32)),
                in_specs=[HBM] * n + dep_specs,
                out_specs=(SEM, SEM, *[HBM] * n, pl.BlockSpec(memory_space=pltpu.VMEM)),
                input_output_aliases={i: 2 + i for i in range(n)},
                compiler_params=pltpu.CompilerParams(has_side_effects=EFFECT))

        outs = _ordered(call, [pltpu.with_memory_space_constraint(a, pltpu.HBM) for a in arrays], n, body, token=-1,
                        sources=arrays)
        self.send, self.recv, self.arrays = outs[0], outs[1], list(outs[2:2 + n])

    def wait(self):
        n, plan = self.n, self.plan

        def body(*refs):
            send, recv, token = refs[n], refs[n + 1], refs[-1]
            for k, (src, _, landing, peer) in enumerate(plan(refs[:n])):
                cp = pltpu.make_async_remote_copy(src_ref=src, dst_ref=landing, send_sem=send.at[k],
                                                  recv_sem=recv.at[k], device_id=peer, device_id_type=MESH)
                cp.wait_send()
                cp.wait_recv()
            token[...] = jnp.zeros_like(token)

        def call(wrapped, dep_specs):
            return pl.pallas_call(
                wrapped, name=f"{self.name}_wait",
                out_shape=(*[pltpu.HBM(a.shape, a.dtype) for a in self.arrays], _sds((8, LANES), F32)),
                in_specs=[HBM] * n + [SEM, SEM] + dep_specs,
                out_specs=(*[HBM] * n, pl.BlockSpec(memory_space=pltpu.VMEM)),
                input_output_aliases={i: i for i in range(n)},
                compiler_params=pltpu.CompilerParams(has_side_effects=EFFECT))

        return list(_ordered(call, [*self.arrays, self.send, self.recv], n + 2, body, token=-1))[:n]


def _gather_ici(name, gathered):
    def plan(refs):
        x, y, c, chips = _place()
        q = 2 * x + y
        return [(g.at[q, c], g.at[q, c], g.at[2 * px + py, c], (px, py, c)) for g in refs for px, py in chips]

    return _SplitCopies(name, gathered, plan, 3 * len(gathered))


def _gather_d2d(name, gathered):
    def plan(refs):
        x, y, c, chips = _place()
        return [(g.at[2 * px + py, c], g.at[2 * px + py, c], g.at[2 * px + py, 1 - c], (x, y, 1 - c))
                for g in refs for px, py in chips]

    return _SplitCopies(name, gathered, plan, 3 * len(gathered))


def _scatter_sibling(name, grads):
    n = len(grads)

    def plan(refs):
        x, y, c, _ = _place()
        return [(refs[w].at[1 - c], refs[n + w], refs[n + w], (x, y, 1 - c)) for w in range(n)]

    landing = [lax.empty(g.shape[1:], g.dtype) for g in grads]
    return _SplitCopies(name, [*grads, *landing], plan, n)


def _scatter_chips(name, sums):
    n = len(sums)

    def plan(refs):
        x, y, c, chips = _place()
        return [(refs[w].at[2 * px + py], refs[n + w].at[j], refs[n + w].at[j], (px, py, c))
                for w in range(n) for j, (px, py) in enumerate(chips)]

    landing = [lax.empty((3, *s.shape[1:]), s.dtype) for s in sums]
    return _SplitCopies(name, [*sums, *landing], plan, 3 * n)


def _share_final(name, finals):
    def plan(refs):
        x, y, c, _ = _place()
        return [(f.at[c], f.at[c], f.at[1 - c], (x, y, 1 - c)) for f in refs]

    return _SplitCopies(name, finals, plan, len(finals))


def _row_tile(rows, cols, budget_elems=393216):
    best = 8
    for t in range(8, rows + 1, 8):
        if rows % t == 0 and t * cols <= budget_elems:
            best = t
    return best if rows % best == 0 else rows


def _sum_with_sibling(name, grad, recv, qc_idx):
    _, _, h, cols = grad.shape
    tr = _row_tile(h, cols)

    def body(s_ref, g_ref, r_ref, own_ref, pb_ref):
        p = g_ref[...] + r_ref[...]
        pb_ref[...] = p.astype(BF16)

        @pl.when(pl.program_id(1) == s_ref[0])
        def _():
            own_ref[...] = p

    blk = pl.BlockSpec((None, tr, cols), lambda r, k, s: (k, r, 0))
    return _pcall(
        body, name=name, out_shape=[_sds((h, cols), F32), _sds((N_CHIPS, h, cols), BF16)],
        grid=(h // tr, N_CHIPS), prefetch=1,
        in_specs=[pl.BlockSpec((None, None, tr, cols), lambda r, k, s: (s[1], k, r, 0)), blk],
        out_specs=[pl.BlockSpec((tr, cols), lambda r, k, s: (r, 0)), blk], vmem_mb=32,
    )(qc_idx, grad, recv)


def _sum_chips(name, own, recv, qc_idx):
    h, cols = own.shape
    tr = _row_tile(h, cols)

    def body(s_ref, p_ref, t_ref, o_ref):
        o_ref[...] = ((p_ref[...] + t_ref[0].astype(F32)) + t_ref[1].astype(F32)) + t_ref[2].astype(F32)

    return _pcall(
        body, name=name, out_shape=_sds((2, h, cols), F32), grid=(h // tr,), prefetch=1,
        in_specs=[pl.BlockSpec((tr, cols), lambda r, s: (r, 0)),
                  pl.BlockSpec((3, tr, cols), lambda r, s: (0, r, 0))],
        out_specs=pl.BlockSpec((None, tr, cols), lambda r, s: (s[1], r, 0)), vmem_mb=32,
    )(qc_idx, own, recv)


class _ReduceScatter:
    def __init__(self, tag, names, grads, qc_idx):
        self.tag, self.names, self.n, self.qc_idx = tag, names, len(grads), qc_idx
        self.copies = _scatter_sibling(f"{tag}_rs_sibling", grads)

    def step2(self):
        n = self.n
        arrs = self.copies.wait()
        sums = [_sum_with_sibling(f"{nm}_sum_sibling", arrs[w], arrs[n + w], self.qc_idx)
                for w, nm in enumerate(self.names)]
        self.own = [s[0] for s in sums]
        self.copies = _scatter_chips(f"{self.tag}_rs_chips", [s[1] for s in sums])

    def step3(self):
        n = self.n
        arrs = self.copies.wait()
        finals = [_sum_chips(f"{nm}_sum_chips", self.own[w], arrs[n + w], self.qc_idx)
                  for w, nm in enumerate(self.names)]
        self.copies = _share_final(f"{self.tag}_rs_final", finals)

    def result(self):
        return {nm: f.reshape(2 * f.shape[1], f.shape[2]) for nm, f in zip(self.names, self.copies.wait())}


def _cast_into_gathered(name, w, q_idx):
    rows, cols = w.shape
    h = rows // 2
    tr = _row_tile(h, cols, 1 << 20)
    nr = h // tr

    def body(s_ref, w_ref, o_ref):
        o_ref[...] = w_ref[...].astype(BF16)

    return _pcall(body, name=name, out_shape=_sds((N_CHIPS, 2, h, cols), BF16), grid=(2, nr), prefetch=1,
                  in_specs=[pl.BlockSpec((tr, cols), lambda hf, r, s: (hf * nr + r, 0))],
                  out_specs=pl.BlockSpec((None, None, tr, cols), lambda hf, r, s: (s[0], hf, r, 0)),
                  vmem_mb=32)(q_idx, w)


def _rms(h):
    r = lax.rsqrt(jnp.mean(h * h, axis=-1, keepdims=True) + EPS)
    return r, h * r


def _norm_mod(name, h, g, sc, sh, ts):
    s_len, d = h.shape

    def body(h_ref, g_ref, sc_ref, sh_ref, n_ref):
        _, xhat = _rms(h_ref[...])
        n_ref[...] = ((xhat * g_ref[...]) * (1.0 + sc_ref[...]) + sh_ref[...]).astype(BF16)

    row = pl.BlockSpec((ts, d), lambda i: (i, 0))
    return _pcall(body, name=name, out_shape=_sds((s_len, d), BF16), grid=(s_len // ts,),
                  in_specs=[row, _vec(d), _vec(d), _vec(d)], out_specs=row, vmem_mb=32)(h, g, sc, sh)


def _residual_norm_mod(name, h, f, gate, cmul, g, sc, sh, ts):
    s_len, d = h.shape

    def body(h_ref, f_ref, gt_ref, g_ref, sc_ref, sh_ref, ho_ref, n_ref):
        hn = h_ref[...] + (cmul * gt_ref[...]) * f_ref[...]
        ho_ref[...] = hn
        _, xhat = _rms(hn)
        n_ref[...] = ((xhat * g_ref[...]) * (1.0 + sc_ref[...]) + sh_ref[...]).astype(BF16)

    row = pl.BlockSpec((ts, d), lambda i: (i, 0))
    return _pcall(body, name=name, out_shape=[_sds((s_len, d), F32), _sds((s_len, d), BF16)],
                  grid=(s_len // ts,), in_specs=[row, row, _vec(d), _vec(d), _vec(d), _vec(d)],
                  out_specs=[row, row], vmem_mb=32)(h, f, gate, g, sc, sh)


def _final_loss(name, h, f, tgt, gate, cmul, g, ts):
    s_len, d = h.shape

    def body(h_ref, f_ref, t_ref, gt_ref, g_ref, dh_ref, df_ref, dg_ref, dgt_ref, loss_ref):
        i = pl.program_id(0)
        fv = f_ref[...]
        coef = cmul * gt_ref[...]
        hn = h_ref[...] + coef * fv
        r, xhat = _rms(hn)
        err = xhat * g_ref[...] - t_ref[...]
        _acc_rows(loss_ref, (0.5 / d) * (err * err), i)
        dy = err * (1.0 / d)
        _acc_rows(dg_ref, dy * xhat, i)
        dxhat = dy * g_ref[...]
        dh = r * (dxhat - xhat * jnp.mean(dxhat * xhat, axis=-1, keepdims=True))
        dh_ref[...] = dh
        _acc_rows(dgt_ref, cmul * (dh * fv), i)
        df_ref[...] = (coef * dh).astype(BF16)

    row = pl.BlockSpec((ts, d), lambda i: (i, 0))
    return _pcall(body, name=name,
                  out_shape=[_sds((s_len, d), F32), _sds((s_len, d), BF16)] + [_sds((1, d), F32)] * 3,
                  grid=(s_len // ts,), in_specs=[row, row, row, _vec(d), _vec(d)],
                  out_specs=[row, row, _vec(d), _vec(d), _vec(d)], vmem_mb=40)(h, f, tgt, gate, g)


def _norm_mod_bwd(name, h, dn, dh_next, g, sc, ts, prev=None):
    s_len, d = h.shape
    has_prev = prev is not None
    cmul = prev[2] if has_prev else None

    def body(*refs):
        if has_prev:
            h_ref, dn_ref, dhn_ref, f_ref, g_ref, sc_ref, gt_ref, dh_ref, df_ref, dsh_ref, dsc_ref, dg_ref, dgt_ref = refs
        else:
            h_ref, dn_ref, dhn_ref, g_ref, sc_ref, dh_ref, dsh_ref, dsc_ref, dg_ref = refs
        i = pl.program_id(0)
        r, xhat = _rms(h_ref[...])
        dn_v = dn_ref[...]
        gv = g_ref[...]
        _acc_rows(dsh_ref, dn_v, i)
        _acc_rows(dsc_ref, dn_v * (xhat * gv), i)
        dnrm = dn_v * (1.0 + sc_ref[...])
        _acc_rows(dg_ref, dnrm * xhat, i)
        dxhat = dnrm * gv
        dh = dhn_ref[...] + r * (dxhat - xhat * jnp.mean(dxhat * xhat, axis=-1, keepdims=True))
        dh_ref[...] = dh
        if has_prev:
            _acc_rows(dgt_ref, cmul * (dh * f_ref[...]), i)
            df_ref[...] = ((cmul * gt_ref[...]) * dh).astype(BF16)

    row = pl.BlockSpec((ts, d), lambda i: (i, 0))
    if has_prev:
        ins, in_specs = [h, dn, dh_next, prev[0], g, sc, prev[1]], [row, row, row, row, _vec(d), _vec(d), _vec(d)]
        out_shape = [_sds((s_len, d), F32), _sds((s_len, d), BF16)] + [_sds((1, d), F32)] * 4
        out_specs = [row, row] + [_vec(d)] * 4
    else:
        ins, in_specs = [h, dn, dh_next, g, sc], [row, row, row, _vec(d), _vec(d)]
        out_shape = [_sds((s_len, d), F32)] + [_sds((1, d), F32)] * 3
        out_specs = [row] + [_vec(d)] * 3
    return _pcall(body, name=name, out_shape=out_shape, grid=(s_len // ts,), in_specs=in_specs,
                  out_specs=out_specs, vmem_mb=40)(*ins)


def _cols(ref, lo, hi, npc, rows=slice(None)):
    parts = []
    while lo < hi:
        q, o = divmod(lo, npc)
        n = min(hi - lo, npc - o)
        parts.append(ref[q, rows, o:o + n].astype(F32))
        lo += n
    return parts[0] if len(parts) == 1 else jnp.concatenate(parts, axis=-1)


def _store_cols(ref, lo, val, npc, rows=slice(None)):
    off, width = 0, val.shape[-1]
    while off < width:
        q, o = divmod(lo + off, npc)
        n = min(width - off, npc - o)
        ref[q, rows, o:o + n] = val[:, off:off + n]
        off += n


def _chips_covering(cols, npc):
    return -(-cols // npc)


SUBLANES = 8
ROW_CHUNK = 32


def _make_phases(src_ref, ph_ref):
    rows = src_ref.shape[0] - SUBLANES
    for b in range(1, SUBLANES):
        ph_ref[b - 1] = src_ref[pl.ds(b, rows), :]


def _window(src_ref, ph_ref, off, r0, cols=slice(None)):
    a, b = divmod(off, SUBLANES)
    start = pl.multiple_of(r0 + SUBLANES * a, SUBLANES)
    if b == 0:
        return src_ref[pl.ds(start, ROW_CHUNK), cols]
    return ph_ref[b - 1, pl.ds(start, ROW_CHUNK), cols]


def _phase_scratch(rows, width):
    return pltpu.VMEM((SUBLANES - 1, rows - SUBLANES, width), F32)


def _conv_ln(a0s_ref, a0p_ref, cw_ref, cb_ref, lg_ref, lb_ref, r0):
    a1 = cb_ref[...] + cw_ref[0:1, :] * _window(a0s_ref, a0p_ref, HALO - CONV_K + 1, r0)
    for k in range(1, CONV_K):
        a1 = a1 + cw_ref[k:k + 1, :] * _window(a0s_ref, a0p_ref, HALO - CONV_K + 1 + k, r0)
    mu = jnp.mean(a1, axis=-1, keepdims=True)
    ctr = a1 - mu
    rstd = lax.rsqrt(jnp.mean(ctr * ctr, axis=-1, keepdims=True) + EPS)
    xh = ctr * rstd
    return xh, rstd, xh * lg_ref[...] + lb_ref[...]


def _for_chunks(ts, fn):
    def step(ci, carry):
        fn(pl.multiple_of(ci * ROW_CHUNK, ROW_CHUNK))
        return carry

    lax.fori_loop(0, ts // ROW_CHUNK, step, 0)


def _stage_glu(p_ref, ph_ref, a0s_ref, i, wc, npc, ts):
    a0 = _cols(p_ref, 0, wc, npc) * _sigmoid(_cols(p_ref, wc, 2 * wc, npc))
    a0h = _cols(ph_ref, 0, wc, npc) * _sigmoid(_cols(ph_ref, wc, 2 * wc, npc))
    a0s_ref[0:HALO, :] = jnp.where(i > 0, a0h, 0.0)
    a0s_ref[HALO:HALO + ts, :] = a0


def _mixer_mid(name, proj, cw, cb, lg, lb, wc, wp, ts):
    _, s_len, npc = proj.shape
    nq = _chips_covering(2 * wc + wp, npc)
    gi = wp // len(POOL_WINDOWS)
    hb = ts // HALO

    def body(p_ref, ph_ref, cw_ref, cb_ref, lg_ref, lb_ref, a3_ref, mx_ref, a0s_ref, vs_ref, a0p_ref, vp_ref):
        i = pl.program_id(0)
        _stage_glu(p_ref, ph_ref, a0s_ref, i, wc, npc, ts)
        vs_ref[0:HALO, :] = jnp.where(i > 0, _cols(ph_ref, 2 * wc, 2 * wc + wp, npc), 0.0)
        vs_ref[HALO:HALO + ts, :] = _cols(p_ref, 2 * wc, 2 * wc + wp, npc)
        _make_phases(a0s_ref, a0p_ref)
        _make_phases(vs_ref, vp_ref)

        def chunk(r0):
            rows = pl.ds(r0, ROW_CHUNK)
            _, _, a2 = _conv_ln(a0s_ref, a0p_ref, cw_ref, cb_ref, lg_ref, lb_ref, r0)
            a3_ref[rows, :] = (a2 * _sigmoid(a2)).astype(BF16)
            t_abs = i * ts + r0 + lax.broadcasted_iota(jnp.int32, (ROW_CHUNK, 1), 0)
            for g, win in enumerate(POOL_WINDOWS):
                cs = slice(g * gi, (g + 1) * gi)
                v_now = _window(vs_ref, vp_ref, HALO, r0, cs)
                acc = v_now
                for dlt in range(1, win):
                    acc = acc + _window(vs_ref, vp_ref, HALO - dlt, r0, cs)
                cnt = jnp.minimum(t_abs + 1, win).astype(F32)
                mx_ref[rows, cs] = (acc / cnt - v_now).astype(BF16)

        _for_chunks(ts, chunk)

    return _pcall(
        body, name=name, out_shape=[_sds((s_len, wc), BF16), _sds((s_len, wp), BF16)], grid=(s_len // ts,),
        in_specs=[pl.BlockSpec((nq, ts, npc), lambda i: (0, i, 0)),
                  pl.BlockSpec((nq, HALO, npc), lambda i: (0, jnp.maximum(i * hb - 1, 0), 0)),
                  pl.BlockSpec((HALO, wc), lambda i: (0, 0)), _vec(wc), _vec(wc), _vec(wc)],
        out_specs=[pl.BlockSpec((ts, wc), lambda i: (i, 0)), pl.BlockSpec((ts, wp), lambda i: (i, 0))],
        scratch=[pltpu.VMEM((HALO + ts, wc), F32), pltpu.VMEM((HALO + ts, wp), F32),
                 _phase_scratch(HALO + ts, wc), _phase_scratch(HALO + ts, wp)], vmem_mb=56,
    )(proj, proj, cw, cb, lg, lb)


def _gates_fwd(name, proj, ya, yb, b_a, b_b, ls, wc, wp, ts):
    _, s_len, npc = proj.shape
    d = ya.shape[1]
    g0 = 2 * wc + wp

    def body(p_ref, ya_ref, yb_ref, ba_ref, bb_ref, ls_ref, z_ref):
        ga = _sigmoid(_cols(p_ref, g0, g0 + d, npc))
        gb = _sigmoid(_cols(p_ref, g0 + d, g0 + 2 * d, npc))
        z = ga * (ya_ref[...] + ba_ref[...]) + gb * ((yb_ref[...] + bb_ref[...]) * ls_ref[...])
        z_ref[...] = z.astype(BF16)

    row = pl.BlockSpec((ts, d), lambda i: (i, 0))
    return _pcall(body, name=name, out_shape=_sds((s_len, d), BF16), grid=(s_len // ts,),
                  in_specs=[pl.BlockSpec((N_CHIPS, ts, npc), lambda i: (0, i, 0)), row, row, _vec(d), _vec(d), _vec(d)],
                  out_specs=row, vmem_mb=48)(proj, ya, yb, b_a, b_b, ls)


def _gates_bwd(name, proj, dz, ya, yb, b_a, b_b, ls, wc, wp, ts):
    _, s_len, npc = proj.shape
    d = ya.shape[1]
    g0 = 2 * wc + wp

    def body(p_ref, dz_ref, ya_ref, yb_ref, ba_ref, bb_ref, ls_ref, dya_ref, dyb_ref, dgt_ref, dba_ref, dls_ref,
             dbb_ref):
        i = pl.program_id(0)
        ga = _sigmoid(_cols(p_ref, g0, g0 + d, npc))
        gb = _sigmoid(_cols(p_ref, g0 + d, g0 + 2 * d, npc))
        dz_v = dz_ref[...]
        y_a = ya_ref[...] + ba_ref[...]
        y_b0 = yb_ref[...] + bb_ref[...]
        ls_v = ls_ref[...]
        dya = dz_v * ga
        dya_ref[...] = dya.astype(BF16)
        _acc_rows(dba_ref, dya, i)
        t = dz_v * gb
        _acc_rows(dls_ref, t * y_b0, i)
        dyb = t * ls_v
        dyb_ref[...] = dyb.astype(BF16)
        _acc_rows(dbb_ref, dyb, i)
        dgt_ref[:, 0:d] = (dz_v * y_a * ga * (1.0 - ga)).astype(BF16)
        dgt_ref[:, d:2 * d] = (dz_v * (y_b0 * ls_v) * gb * (1.0 - gb)).astype(BF16)

    row = pl.BlockSpec((ts, d), lambda i: (i, 0))
    return _pcall(
        body, name=name,
        out_shape=[_sds((s_len, d), BF16), _sds((s_len, d), BF16), _sds((s_len, 2 * d), BF16)] + [_sds((1, d), F32)] * 3,
        grid=(s_len // ts,),
        in_specs=[pl.BlockSpec((N_CHIPS, ts, npc), lambda i: (0, i, 0)), row, row, row, _vec(d), _vec(d), _vec(d)],
        out_specs=[row, row, pl.BlockSpec((ts, 2 * d), lambda i: (i, 0))] + [_vec(d)] * 3, vmem_mb=48,
    )(proj, dz, ya, yb, b_a, b_b, ls)


def _conv_branch_bwd(name, proj, da3, cw, cb, lg, lb, wc, wp, ts):
    _, s_len, npc = proj.shape
    nq = _chips_covering(2 * wc, npc)
    hb = ts // HALO

    n_tiles = s_len // ts

    def fold(v):
        return jnp.sum(v.reshape(ROW_CHUNK // SUBLANES, SUBLANES, v.shape[-1]), axis=0)

    def body(p_ref, ph_ref, da3_ref, cw_ref, cb_ref, lg_ref, lb_ref, da1_ref, dlg_ref, dlb_ref, dcb_ref, dcw_ref,
             a0s_ref, a0p_ref, vec8_ref, dcw8_ref):
        i = pl.program_id(0)
        _stage_glu(p_ref, ph_ref, a0s_ref, i, wc, npc, ts)
        _make_phases(a0s_ref, a0p_ref)

        @pl.when(i == 0)
        def _():
            vec8_ref[...] = jnp.zeros_like(vec8_ref)
            dcw8_ref[...] = jnp.zeros_like(dcw8_ref)

        def chunk(r0):
            rows = pl.ds(r0, ROW_CHUNK)
            xh, rstd, a2 = _conv_ln(a0s_ref, a0p_ref, cw_ref, cb_ref, lg_ref, lb_ref, r0)
            sig = _sigmoid(a2)
            da2 = da3_ref[rows, :] * (sig * (1.0 + a2 * (1.0 - sig)))
            vec8_ref[0] += fold(da2 * xh)
            vec8_ref[1] += fold(da2)
            dxh = da2 * lg_ref[...]
            da1 = rstd * (dxh - jnp.mean(dxh, axis=-1, keepdims=True)
                          - xh * jnp.mean(dxh * xh, axis=-1, keepdims=True))
            da1_ref[rows, :] = da1
            vec8_ref[2] += fold(da1)
            for k in range(CONV_K):
                dcw8_ref[k] += fold(da1 * _window(a0s_ref, a0p_ref, HALO - CONV_K + 1 + k, r0))

        _for_chunks(ts, chunk)

        @pl.when(i == n_tiles - 1)
        def _():
            dlg_ref[...] = jnp.sum(vec8_ref[0], axis=0, keepdims=True)
            dlb_ref[...] = jnp.sum(vec8_ref[1], axis=0, keepdims=True)
            dcb_ref[...] = jnp.sum(vec8_ref[2], axis=0, keepdims=True)
            dcw_ref[...] = jnp.sum(dcw8_ref[...], axis=1)

    return _pcall(
        body, name=name,
        out_shape=[_sds((s_len, wc), F32)] + [_sds((1, wc), F32)] * 3 + [_sds((HALO, wc), F32)],
        grid=(s_len // ts,),
        in_specs=[pl.BlockSpec((nq, ts, npc), lambda i: (0, i, 0)),
                  pl.BlockSpec((nq, HALO, npc), lambda i: (0, jnp.maximum(i * hb - 1, 0), 0)),
                  pl.BlockSpec((ts, wc), lambda i: (i, 0)),
                  pl.BlockSpec((HALO, wc), lambda i: (0, 0)), _vec(wc), _vec(wc), _vec(wc)],
        out_specs=[pl.BlockSpec((ts, wc), lambda i: (i, 0)), _vec(wc), _vec(wc), _vec(wc),
                   pl.BlockSpec((HALO, wc), lambda i: (0, 0))],
        scratch=[pltpu.VMEM((HALO + ts, wc), F32), _phase_scratch(HALO + ts, wc),
                 pltpu.VMEM((3, SUBLANES, wc), F32), pltpu.VMEM((HALO, SUBLANES, wc), F32)], vmem_mb=56,
    )(proj, proj, da3, cw, cb, lg, lb)


def _mixer_in_bwd(name, proj, da1, dmixed, dgates, cw, wc, wp, ts):
    _, s_len, npc = proj.shape
    nq = _chips_covering(2 * wc, npc)
    gi = wp // len(POOL_WINDOWS)
    hb = ts // HALO
    n_tiles = s_len // ts
    last_hb = s_len // HALO - 1
    d2 = dgates.shape[1]

    def body(p_ref, d1_ref, d1n_ref, dm_ref, dmn_ref, dgt_ref, cw_ref, o_ref, d1s_ref, es_ref, d1p_ref, ep_ref):
        i = pl.program_id(0)
        more = i < n_tiles - 1
        d1s_ref[0:ts, :] = d1_ref[...]
        d1s_ref[ts:ts + HALO, :] = jnp.where(more, d1n_ref[...], 0.0)
        t_abs = i * ts + lax.broadcasted_iota(jnp.int32, (ts + HALO, 1), 0)
        dm_ext = jnp.concatenate([dm_ref[...], jnp.where(more, dmn_ref[...], 0.0)], axis=0)
        for g, win in enumerate(POOL_WINDOWS):
            cs = slice(g * gi, (g + 1) * gi)
            es_ref[:, cs] = dm_ext[:, cs] / jnp.minimum(t_abs + 1, win).astype(F32)
        _make_phases(d1s_ref, d1p_ref)
        _make_phases(es_ref, ep_ref)

        def chunk(r0):
            rows = pl.ds(r0, ROW_CHUNK)
            da0 = cw_ref[0:1, :] * _window(d1s_ref, d1p_ref, CONV_K - 1, r0)
            for k in range(1, CONV_K):
                da0 = da0 + cw_ref[k:k + 1, :] * _window(d1s_ref, d1p_ref, CONV_K - 1 - k, r0)
            glu_a = _cols(p_ref, 0, wc, npc, rows)
            sig = _sigmoid(_cols(p_ref, wc, 2 * wc, npc, rows))
            _store_cols(o_ref, 0, (da0 * sig).astype(BF16), npc, rows)
            _store_cols(o_ref, wc, (da0 * glu_a * sig * (1.0 - sig)).astype(BF16), npc, rows)
            parts = []
            for g, win in enumerate(POOL_WINDOWS):
                cs = slice(g * gi, (g + 1) * gi)
                acc = _window(es_ref, ep_ref, 0, r0, cs)
                for dlt in range(1, win):
                    acc = acc + _window(es_ref, ep_ref, dlt, r0, cs)
                parts.append(acc - dm_ref[rows, cs])
            _store_cols(o_ref, 2 * wc, jnp.concatenate(parts, axis=-1).astype(BF16), npc, rows)

        _for_chunks(ts, chunk)
        _store_cols(o_ref, 2 * wc + wp, dgt_ref[...], npc)

    nxt = lambda i: (jnp.minimum((i + 1) * hb, last_hb), 0)
    return _pcall(
        body, name=name, out_shape=_sds((N_CHIPS, s_len, npc), BF16), grid=(n_tiles,),
        in_specs=[pl.BlockSpec((nq, ts, npc), lambda i: (0, i, 0)),
                  pl.BlockSpec((ts, wc), lambda i: (i, 0)), pl.BlockSpec((HALO, wc), nxt),
                  pl.BlockSpec((ts, wp), lambda i: (i, 0)), pl.BlockSpec((HALO, wp), nxt),
                  pl.BlockSpec((ts, d2), lambda i: (i, 0)),
                  pl.BlockSpec((HALO, wc), lambda i: (0, 0))],
        out_specs=pl.BlockSpec((N_CHIPS, ts, npc), lambda i: (0, i, 0)),
        scratch=[pltpu.VMEM((ts + HALO, wc), F32), pltpu.VMEM((ts + HALO, wp), F32),
                 _phase_scratch(ts + HALO, wc), _phase_scratch(ts + HALO, wp)], vmem_mb=56,
    )(proj, da1, da1, dmixed, dmixed, dgates, cw)


def _ada_fwd(name, c_all, w, b):
    d, cols = w.shape
    tn = 512 if cols % 512 == 0 else cols

    def body(c_ref, w_ref, b_ref, o_ref):
        cv = c_ref[...]
        sc = (cv * _sigmoid(cv)).astype(BF16)
        o_ref[...] = jnp.dot(sc, w_ref[...].astype(BF16), preferred_element_type=F32) + b_ref[...]

    return _pcall(body, name=name, out_shape=_sds((N_DEV, cols), F32), grid=(cols // tn,),
                  in_specs=[pl.BlockSpec((N_DEV, d), lambda j: (0, 0)), pl.BlockSpec((d, tn), lambda j: (0, j)),
                            pl.BlockSpec((1, tn), lambda j: (0, j))],
                  out_specs=pl.BlockSpec((N_DEV, tn), lambda j: (0, j)), vmem_mb=32)(c_all, w, b)


def _adam_math(w, g, m, v):
    m_new = ADAM_B1 * m + (1.0 - ADAM_B1) * g
    v_new = ADAM_B2 * v + (1.0 - ADAM_B2) * (g * g)
    m_hat = m_new / (1.0 - ADAM_B1 ** ADAM_STEP)
    v_hat = v_new / (1.0 - ADAM_B2 ** ADAM_STEP)
    delta = -ADAM_LR * (m_hat / (jnp.sqrt(v_hat) + ADAM_EPS) + ADAM_WD * w)
    return delta, m_new, v_new


def _adamw(name, w, g, m, v):
    rows, cols = w.shape
    tr = _row_tile(rows, cols, 262144)

    def body(w_ref, g_ref, m_ref, v_ref, go_ref, d_ref, mo_ref, vo_ref):
        g = g_ref[...]
        go_ref[...] = g
        d_ref[...], mo_ref[...], vo_ref[...] = _adam_math(w_ref[...], g, m_ref[...], v_ref[...])

    spec = pl.BlockSpec((tr, cols), lambda i: (i, 0))
    return _pcall(body, name=name, out_shape=[_sds(w.shape, F32)] * 4, grid=(rows // tr,), in_specs=[spec] * 4,
                  out_specs=[spec] * 4, vmem_mb=40)(w, g, m, v)


def _ada_grad_adamw(name, c_t, d_ada, w, m, v):
    rows, cols = w.shape
    tr = _tile(rows, 256)
    tc = _tile(cols, 1536) if cols % 1536 == 0 else cols

    def body(c_ref, da_ref, w_ref, m_ref, v_ref, g_ref, d_ref, mo_ref, vo_ref):
        cv = c_ref[...]
        sc = cv * _sigmoid(cv)
        g = sc[:, 0:1] * da_ref[0:1, :]
        for b in range(1, N_DEV):
            g = g + sc[:, b:b + 1] * da_ref[b:b + 1, :]
        g_ref[...] = g
        d_ref[...], mo_ref[...], vo_ref[...] = _adam_math(w_ref[...], g, m_ref[...], v_ref[...])

    spec = pl.BlockSpec((tr, tc), lambda i, j: (i, j))
    return _pcall(body, name=name, out_shape=[_sds(w.shape, F32)] * 4, grid=(rows // tr, cols // tc),
                  in_specs=[pl.BlockSpec((tr, N_DEV), lambda i, j: (i, 0)),
                            pl.BlockSpec((N_DEV, tc), lambda i, j: (0, j)), spec, spec, spec],
                  out_specs=[spec] * 4, vmem_mb=40)(c_t, d_ada, w, m, v)


def _sum_devices(name, gathered, m_per):
    n = gathered.shape[1]

    def body(g_ref, o_ref):
        acc = g_ref[0:m_per, :]
        for dev in range(1, N_DEV):
            acc = acc + g_ref[dev * m_per:(dev + 1) * m_per, :]
        o_ref[...] = acc

    return _pcall(body, name=name, out_shape=_sds((m_per, n), F32),
                  in_specs=[pl.BlockSpec(memory_space=pltpu.VMEM)],
                  out_specs=pl.BlockSpec(memory_space=pltpu.VMEM))(gathered)


def _ffn_fwd(tag, n, w_in_g, w_out_after_swiglu, dims):
    s_len, d, f_dim = dims["S"], dims["D"], dims["F"]
    p = f_dim // 2
    tf = f_dim // 4
    tm0, tm = _tile(s_len, 512), _tile(s_len, 1024)
    nbp = p // tf

    def ep(accs, ex, outs):
        hh, uu = accs
        outs[0][0] = hh.astype(BF16)
        outs[0][1] = uu.astype(BF16)
        outs[1][...] = (hh * _sigmoid(hh) * uu).astype(BF16)

    hu, act = _matmul(
        f"{tag}_swiglu", n, [w_in_g, w_in_g], mode="nn", grid=(s_len // tm0, f_dim // tf, 1),
        a_spec=pl.BlockSpec((tm0, d), lambda i, j, k: (i, 0)),
        b_specs=[pl.BlockSpec((None, d, tf), lambda i, j, k: (j // nbp, 0, j % nbp)),
                 pl.BlockSpec((None, d, tf), lambda i, j, k: (2 + j // nbp, 0, j % nbp))],
        out_shape=[_sds((2, s_len, f_dim), BF16), _sds((s_len, f_dim), BF16)],
        out_specs=[pl.BlockSpec((2, tm0, tf), lambda i, j, k: (0, i, j)),
                   pl.BlockSpec((tm0, tf), lambda i, j, k: (i, j))],
        acc_shape=(tm0, tf), epilogue=ep)
    w_out2d = w_out_after_swiglu()
    tn2 = _tile(d, 1024)
    f = _matmul(
        f"{tag}_down", act, [w_out2d], mode="nn", grid=(s_len // tm, d // tn2, 2),
        a_spec=pl.BlockSpec((tm, 2 * tf), lambda i, j, k: (i, k)),
        b_specs=[pl.BlockSpec((2 * tf, tn2), lambda i, j, k: (k, j))],
        out_shape=_sds((s_len, d), F32), out_specs=pl.BlockSpec((tm, tn2), lambda i, j, k: (i, j)),
        acc_shape=(tm, tn2), epilogue=_ep_store(F32))
    return hu, act, f, w_out2d


def _ffn_bwd(tag, n, hu, act, df, w_in_g, w_out2d, dims, after_dw_out, after_dw_in):
    s_len, d, f_dim = dims["S"], dims["D"], dims["F"]
    tf = f_dim // 4
    tk = _tile(s_len, 2048)
    tn = _tile(d, 1024)
    g_out = _matmul(
        f"{tag}_dw_out", act, [df], mode="tn", grid=(4, d // tn, s_len // tk),
        a_spec=pl.BlockSpec((tk, tf), lambda i, j, k: (k, i)),
        b_specs=[pl.BlockSpec((tk, tn), lambda i, j, k: (k, j))],
        out_shape=_sds((2, 4, tf // 2, d), F32),
        out_specs=pl.BlockSpec((2, None, tf // 2, tn), lambda i, j, k: (0, i, 0, j)),
        acc_shape=(tf, tn), epilogue=_ep_halves(tf // 2))
    after_dw_out(g_out)

    def ep_dhu(accs, ex, outs):
        da = accs[0]
        hh, uu = ex[0][0].astype(F32), ex[0][1].astype(F32)
        sig = _sigmoid(hh)
        outs[0][0] = (da * uu * (sig * (1.0 + hh * (1.0 - sig)))).astype(BF16)
        outs[0][1] = (da * (hh * sig)).astype(BF16)

    tm = _tile(s_len, 512)
    hu_spec = pl.BlockSpec((2, tm, tf), lambda i, j, k: (0, i, j))
    dhu = _matmul(
        f"{tag}_dhu", df, [w_out2d], mode="nt", grid=(s_len // tm, 4, 1),
        a_spec=pl.BlockSpec((tm, d), lambda i, j, k: (i, 0)),
        b_specs=[pl.BlockSpec((tf, d), lambda i, j, k: (j, 0))],
        extras=[hu], extra_specs=[hu_spec],
        out_shape=_sds((2, s_len, f_dim), BF16), out_specs=hu_spec, acc_shape=(tm, tf), epilogue=ep_dhu)

    hd = d // 2
    g_in = _matmul(
        f"{tag}_dw_in", n, [dhu], mode="tn", grid=(2, 8, s_len // tk),
        a_spec=pl.BlockSpec((tk, hd), lambda i, j, k: (k, i)),
        b_specs=[pl.BlockSpec((None, tk, tf), lambda i, j, k: (j // 4, k, j % 4))],
        out_shape=_sds((2, 4, hd, f_dim // 2), F32),
        out_specs=pl.BlockSpec((None, None, hd, tf), lambda i, j, k: (i, j // 2, 0, j % 2)),
        acc_shape=(hd, tf), epilogue=_ep_store(F32))
    after_dw_in(g_in)

    tm2 = _tile(s_len, 1024)
    dn = _matmul(
        f"{tag}_dn", dhu, [w_in_g], mode="nt", grid=(s_len // tm2, d // tn, 4),
        a_spec=pl.BlockSpec((None, tm2, 2 * tf), lambda i, j, k: (k // 2, i, k % 2)),
        b_specs=[pl.BlockSpec((None, tn, 2 * tf), lambda i, j, k: (k, j, 0))],
        out_shape=_sds((s_len, d), F32), out_specs=pl.BlockSpec((tm2, tn), lambda i, j, k: (i, j)),
        acc_shape=(tm2, tn), epilogue=_ep_store(F32))
    return dn


def kernel(x, c, w_ada, b_ada, g_ffn1, w1_in, w1_out, g_mix, w_in, conv_w, conv_b, ln_a_g, ln_a_b, w_a_out, b_a_out, w_b_group, b_b_group, ls_b, w_out, g_ffn2, w2_in, w2_out, g_final, loss_target, m_w_ada, m_b_ada, m_g_ffn1, m_w1_in, m_w1_out, m_g_mix, m_w_in, m_conv_w, m_conv_b, m_ln_a_g, m_ln_a_b, m_w_a_out, m_b_a_out, m_w_b_group, m_b_b_group, m_ls_b, m_w_out, m_g_ffn2, m_w2_in, m_w2_out, m_g_final, v_w_ada, v_b_ada, v_g_ffn1, v_w1_in, v_w1_out, v_g_mix, v_w_in, v_conv_w, v_conv_b, v_ln_a_g, v_ln_a_b, v_w_a_out, v_b_a_out, v_w_b_group, v_b_b_group, v_ls_b, v_w_out, v_g_ffn2, v_w2_in, v_w2_out, v_g_final):
    weights = dict(w_ada=w_ada, b_ada=b_ada, g_ffn1=g_ffn1, w1_in=w1_in, w1_out=w1_out, g_mix=g_mix, w_in=w_in,
                   conv_w=conv_w, conv_b=conv_b, ln_a_g=ln_a_g, ln_a_b=ln_a_b, w_a_out=w_a_out, b_a_out=b_a_out,
                   w_b_group=w_b_group, b_b_group=b_b_group, ls_b=ls_b, w_out=w_out, g_ffn2=g_ffn2, w2_in=w2_in,
                   w2_out=w2_out, g_final=g_final)
    mom1 = dict(w_ada=m_w_ada, b_ada=m_b_ada, g_ffn1=m_g_ffn1, w1_in=m_w1_in, w1_out=m_w1_out, g_mix=m_g_mix,
                w_in=m_w_in, conv_w=m_conv_w, conv_b=m_conv_b, ln_a_g=m_ln_a_g, ln_a_b=m_ln_a_b, w_a_out=m_w_a_out,
                b_a_out=m_b_a_out, w_b_group=m_w_b_group, b_b_group=m_b_b_group, ls_b=m_ls_b, w_out=m_w_out,
                g_ffn2=m_g_ffn2, w2_in=m_w2_in, w2_out=m_w2_out, g_final=m_g_final)
    mom2 = dict(w_ada=v_w_ada, b_ada=v_b_ada, g_ffn1=v_g_ffn1, w1_in=v_w1_in, w1_out=v_w1_out, g_mix=v_g_mix,
                w_in=v_w_in, conv_w=v_conv_w, conv_b=v_conv_b, ln_a_g=v_ln_a_g, ln_a_b=v_ln_a_b, w_a_out=v_w_a_out,
                b_a_out=v_b_a_out, w_b_group=v_w_b_group, b_b_group=v_b_b_group, ls_b=v_ls_b, w_out=v_w_out,
                g_ffn2=v_g_ffn2, w2_in=v_w2_in, w2_out=v_w2_out, g_final=v_g_final)
    order = list(weights)

    s_len, d = x.shape[1], x.shape[2]
    f_dim = w1_out.shape[0] * N_CHIPS
    wc = conv_w.shape[1] * N_CHIPS
    wp = w_b_group.shape[0] * w_b_group.shape[1]
    n_groups, gi, goq = w_b_group.shape
    npc = w_in.shape[1]
    ada_c = w_ada.shape[1]
    dims = dict(S=s_len, D=d, F=f_dim)
    ts = _tile(s_len, 256)

    xi, yi, ci = lax.axis_index("x"), lax.axis_index("y"), lax.axis_index("c")
    q = 2 * xi + yi
    dev = 2 * q + ci
    q_idx = jnp.reshape(q, (1,)).astype(jnp.int32)
    qc_idx = jnp.stack([q, ci]).astype(jnp.int32)
    _PREVIOUS.clear()

    cwq = conv_w.shape[1]
    pack0 = jnp.concatenate([c.reshape(-1), conv_w.reshape(-1), b_b_group.reshape(-1)])
    n0 = -(-pack0.shape[0] // (8 * LANES)) * LANES
    pack0 = jnp.pad(pack0, (0, 8 * n0 - pack0.shape[0])).reshape(8, n0)
    g0 = _allgather_small("gather_small_in", pack0).reshape(N_DEV, 8 * n0)
    c_all = g0[:, :d]
    south = g0[0::2]
    cw_full = jnp.concatenate([south[k, d:d + CONV_K * cwq].reshape(CONV_K, cwq) for k in range(N_CHIPS)], axis=1)
    cw_pad = jnp.pad(cw_full, ((0, HALO - CONV_K), (0, 0)))
    o_bb = d + CONV_K * cwq
    bb_full = jnp.concatenate([south[k, o_bb:o_bb + n_groups * goq].reshape(n_groups, goq) for k in range(N_CHIPS)],
                              axis=1).reshape(1, d)

    as2d = lambda a: a.reshape(-1, a.shape[-1])
    groups = dict(w1_in=["w1_in"], w1_out=["w1_out"], w_in=["w_in"], mix=["w_a_out", "w_b_group", "w_out"],
                  w2_in=["w2_in"], w2_out=["w2_out"])
    big = [nm for grp in groups.values() for nm in grp]
    casts = {nm: _cast_into_gathered(f"cast_{nm}", as2d(weights[nm]), q_idx) for nm in groups["w1_in"]}
    ici = dict(w1_in=_gather_ici("gather_w1_in_ici", [casts[nm] for nm in groups["w1_in"]]))

    b_ada_mine = lax.dynamic_slice(b_ada, (q * ada_c,), (ada_c,)).reshape(1, ada_c)
    ada_piece = _ada_fwd("ada_fwd", c_all, w_ada, b_ada_mine)
    casts.update({nm: _cast_into_gathered(f"cast_{nm}", as2d(weights[nm]), q_idx) for nm in big[1:]})
    g1 = _allgather_small("gather_ada", ada_piece).reshape(N_DEV, N_DEV, ada_c)
    for grp in list(groups)[1:]:
        ici[grp] = _gather_ici(f"gather_{grp}_ici", [casts[nm] for nm in groups[grp]])
    ada_rows = lax.dynamic_index_in_dim(g1[0::2], dev, axis=1, keepdims=False)
    ada = ada_rows.reshape(3, 3, 1, d)
    (sh1, sc1, gt1), (sh2, sc2, gt2), (sh3, sc3, gt3) = [[ada[i, j] for j in range(3)] for i in range(3)]

    row = lambda vct: vct.reshape(1, -1)
    g1v, gmv, g2v, gfv = row(g_ffn1), row(g_mix), row(g_ffn2), row(g_final)

    def arrived(grp):
        return _gather_d2d(f"gather_{grp}_d2d", ici[grp].wait())

    def gathered(fwd, grp):
        return {nm: g.reshape(N_CHIPS, 2 * g.shape[2], g.shape[3]) for nm, g in zip(groups[grp], fwd.wait())}

    x2 = x[0]
    tgt = loss_target[0]

    n1 = _norm_mod("ffn1_norm", x2, g1v, sc1, sh1, ts)
    w1_in_g = gathered(arrived("w1_in"), "w1_in")["w1_in"]
    fwd = {}

    def w1_out_after_swiglu():
        fwd["w1_out"] = arrived("w1_out")
        fwd["w_in"] = arrived("w_in")
        return gathered(fwd["w1_out"], "w1_out")["w1_out"].reshape(f_dim, d)

    hu1, act1, f1, w1_out_2d = _ffn_fwd("ffn1", n1, w1_in_g, w1_out_after_swiglu, dims)
    h1, n2 = _residual_norm_mod("mix_norm", x2, f1, gt1, 0.5, gmv, sc2, sh2, ts)
    w_in_g = gathered(fwd["w_in"], "w_in")["w_in"]

    tm = _tile(s_len, 1024)
    tnp = npc // 2
    proj = _matmul(
        "mix_proj", n2, [w_in_g], mode="nn", grid=(s_len // tm, 8, 1),
        a_spec=pl.BlockSpec((tm, d), lambda i, j, k: (i, 0)),
        b_specs=[pl.BlockSpec((None, d, tnp), lambda i, j, k: (j // 2, 0, j % 2))],
        out_shape=_sds((N_CHIPS, s_len, npc), BF16),
        out_specs=pl.BlockSpec((None, tm, tnp), lambda i, j, k: (j // 2, i, j % 2)),
        acc_shape=(tm, tnp), epilogue=_ep_store(BF16))
    fwd["mix"] = arrived("mix")
    cbv, lgv, lbv = row(conv_b), row(ln_a_g), row(ln_a_b)
    a3, mixed = _mixer_mid("mix_mid", proj, cw_pad, cbv, lgv, lbv, wc, wp, ts)
    wts = gathered(fwd["mix"], "mix")
    w_out_2d = wts["w_out"].reshape(d, d)
    w_a_g = wts["w_a_out"]
    w_b_g = wts["w_b_group"]
    dq = d // N_CHIPS
    ya = _matmul(
        "mix_ya", a3, [w_a_g], mode="nn", grid=(s_len // tm, N_CHIPS, 1),
        a_spec=pl.BlockSpec((tm, wc), lambda i, j, k: (i, 0)),
        b_specs=[pl.BlockSpec((None, wc, dq), lambda i, j, k: (j, 0, 0))],
        out_shape=_sds((s_len, d), BF16), out_specs=pl.BlockSpec((tm, dq), lambda i, j, k: (i, j)),
        acc_shape=(tm, dq), epilogue=_ep_store(BF16))
    yb = _matmul(
        "mix_yb", mixed, [w_b_g], mode="nn", grid=(s_len // tm, n_groups * N_CHIPS, 1),
        a_spec=pl.BlockSpec((tm, gi), lambda i, j, k: (i, j // N_CHIPS)),
        b_specs=[pl.BlockSpec((None, gi, goq), lambda i, j, k: (j % N_CHIPS, j // N_CHIPS, 0))],
        out_shape=_sds((s_len, d), BF16), out_specs=pl.BlockSpec((tm, goq), lambda i, j, k: (i, j)),
        acc_shape=(tm, goq), epilogue=_ep_store(BF16))
    bav, lsv = row(b_a_out), row(ls_b)
    z = _gates_fwd("mix_gates", proj, ya, yb, bav, bb_full, lsv, wc, wp, ts)
    tn = _tile(d, 1024)
    mix = _matmul(
        "mix_out", z, [w_out_2d], mode="nn", grid=(s_len // tm, d // tn, 1),
        a_spec=pl.BlockSpec((tm, d), lambda i, j, k: (i, 0)),
        b_specs=[pl.BlockSpec((d, tn), lambda i, j, k: (0, j))],
        out_shape=_sds((s_len, d), F32), out_specs=pl.BlockSpec((tm, tn), lambda i, j, k: (i, j)),
        acc_shape=(tm, tn), epilogue=_ep_store(F32))
    fwd["w2_in"] = arrived("w2_in")
    h2, n3 = _residual_norm_mod("ffn2_norm", h1, mix, gt2, 1.0, g2v, sc3, sh3, ts)
    w2_in_g = gathered(fwd["w2_in"], "w2_in")["w2_in"]
    hu2, act2, f3, w2_out_2d = _ffn_fwd(
        "ffn2", n3, w2_in_g,
        lambda: gathered(arrived("w2_out"), "w2_out")["w2_out"].reshape(f_dim, d), dims)

    dh3, df3, d_gf, d_gt3, loss_cols = _final_loss("final_loss", h2, f3, tgt, gt3, 0.5, gfv, ts)
    rs, held = {}, {}
    dn3 = _ffn_bwd(
        "ffn2", n3, hu2, act2, df3, w2_in_g, w2_out_2d, dims,
        after_dw_out=lambda g: held.update(w2_out=g),
        after_dw_in=lambda g: rs.update(ffn2=_ReduceScatter("g_ffn2", ["w2_out", "w2_in"], [held["w2_out"], g],
                                                            qc_idx)))
    dh2, dmix, d_sh3, d_sc3, d_g2, d_gt2 = _norm_mod_bwd("ffn2_norm_bwd", h2, dn3, dh3, g2v, sc3, ts,
                                                         prev=(mix, gt2, 1.0))
    rs["ffn2"].step2()

    tk = s_len
    hq = d // (2 * N_CHIPS)
    gw_out = _matmul(
        "mix_dw_out", z, [dmix], mode="tn", grid=(N_CHIPS, d // tn, s_len // tk),
        a_spec=pl.BlockSpec((tk, 2 * hq), lambda i, j, k: (k, i)),
        b_specs=[pl.BlockSpec((tk, tn), lambda i, j, k: (k, j))],
        out_shape=_sds((2, N_CHIPS, hq, d), F32),
        out_specs=pl.BlockSpec((2, None, hq, tn), lambda i, j, k: (0, i, 0, j)),
        acc_shape=(2 * hq, tn), epilogue=_ep_halves(hq))
    dz = _matmul(
        "mix_dz", dmix, [w_out_2d], mode="nt", grid=(s_len // tm, d // tn, 1),
        a_spec=pl.BlockSpec((tm, d), lambda i, j, k: (i, 0)),
        b_specs=[pl.BlockSpec((tn, d), lambda i, j, k: (j, 0))],
        out_shape=_sds((s_len, d), F32), out_specs=pl.BlockSpec((tm, tn), lambda i, j, k: (i, j)),
        acc_shape=(tm, tn), epilogue=_ep_store(F32))
    dya, dyb, dgates, d_ba, d_ls, d_bb = _gates_bwd("mix_gates_bwd", proj, dz, ya, yb, bav, bb_full, lsv, wc, wp, ts)
    gw_a = _matmul(
        "mix_dw_a", a3, [dya], mode="tn", grid=(1, N_CHIPS, s_len // tk),
        a_spec=pl.BlockSpec((tk, wc), lambda i, j, k: (k, 0)),
        b_specs=[pl.BlockSpec((tk, dq), lambda i, j, k: (k, j))],
        out_shape=_sds((2, N_CHIPS, wc // 2, dq), F32),
        out_specs=pl.BlockSpec((2, None, wc // 2, dq), lambda i, j, k: (0, j, 0, 0)),
        acc_shape=(wc, dq), epilogue=_ep_halves(wc // 2))
    da3 = _matmul(
        "mix_da3", dya, [w_a_g], mode="nt", grid=(s_len // tm, 1, N_CHIPS),
        a_spec=pl.BlockSpec((tm, dq), lambda i, j, k: (i, k)),
        b_specs=[pl.BlockSpec((None, wc, dq), lambda i, j, k: (k, 0, 0))],
        out_shape=_sds((s_len, wc), F32), out_specs=pl.BlockSpec((tm, wc), lambda i, j, k: (i, 0)),
        acc_shape=(tm, wc), epilogue=_ep_store(F32))
    gpr = n_groups // 2
    gw_b = _matmul(
        "mix_dw_b", mixed, [dyb], mode="tn", grid=(1, n_groups * N_CHIPS, s_len // tk),
        a_spec=pl.BlockSpec((tk, gi), lambda i, j, k: (k, j // N_CHIPS)),
        b_specs=[pl.BlockSpec((tk, goq), lambda i, j, k: (k, j))],
        out_shape=_sds((2, N_CHIPS, gpr * gi, goq), F32),
        out_specs=pl.BlockSpec((None, None, gi, goq),
                               lambda i, j, k: ((j // N_CHIPS) // gpr, j % N_CHIPS, (j // N_CHIPS) % gpr, 0)),
        acc_shape=(gi, goq), epilogue=_ep_store(F32))
    dmixed = _matmul(
        "mix_dmixed", dyb, [w_b_g], mode="nt", grid=(s_len // tm, n_groups, N_CHIPS),
        a_spec=pl.BlockSpec((tm, goq), lambda i, j, k: (i, j * N_CHIPS + k)),
        b_specs=[pl.BlockSpec((None, gi, goq), lambda i, j, k: (k, j, 0))],
        out_shape=_sds((s_len, wp), F32), out_specs=pl.BlockSpec((tm, gi), lambda i, j, k: (i, j)),
        acc_shape=(tm, gi), epilogue=_ep_store(F32))
    da1, d_lg, d_lb, d_cb, d_cw = _conv_branch_bwd("mix_conv_bwd", proj, da3, cw_pad, cbv, lgv, lbv, wc, wp, ts)
    dproj = _mixer_in_bwd("mix_in_bwd", proj, da1, dmixed, dgates, cw_pad, wc, wp, ts)
    hd = d // 2
    gw_in = _matmul(
        "mix_dw_in", n2, [dproj], mode="tn", grid=(2, 8, s_len // tk),
        a_spec=pl.BlockSpec((tk, hd), lambda i, j, k: (k, i)),
        b_specs=[pl.BlockSpec((None, tk, tnp), lambda i, j, k: (j // 2, k, j % 2))],
        out_shape=_sds((2, N_CHIPS, hd, npc), F32),
        out_specs=pl.BlockSpec((None, None, hd, tnp), lambda i, j, k: (i, j // 2, 0, j % 2)),
        acc_shape=(hd, tnp), epilogue=_ep_store(F32))
    rs["mix"] = _ReduceScatter("g_mix", ["w_in", "w_a_out", "w_b_group", "w_out"], [gw_in, gw_a, gw_b, gw_out],
                               qc_idx)
    rs["ffn2"].step3()
    dn2 = _matmul(
        "mix_dn", dproj, [w_in_g], mode="nt", grid=(s_len // tm, d // tn, N_CHIPS),
        a_spec=pl.BlockSpec((None, tm, npc), lambda i, j, k: (k, i, 0)),
        b_specs=[pl.BlockSpec((None, tn, npc), lambda i, j, k: (k, j, 0))],
        out_shape=_sds((s_len, d), F32), out_specs=pl.BlockSpec((tm, tn), lambda i, j, k: (i, j)),
        acc_shape=(tm, tn), epilogue=_ep_store(F32))
    dh1, df1, d_sh2, d_sc2, d_gm, d_gt1 = _norm_mod_bwd("mix_norm_bwd", h1, dn2, dh2, gmv, sc2, ts,
                                                        prev=(f1, gt1, 0.5))
    rs["mix"].step2()

    def w1_in_ready(g):
        rs["w1_in"] = _ReduceScatter("g_w1_in", ["w1_in"], [g], qc_idx)
        rs["w1_out"].step2()
        rs["mix"].step3()

    dn1 = _ffn_bwd(
        "ffn1", n1, hu1, act1, df1, w1_in_g, w1_out_2d, dims,
        after_dw_out=lambda g: rs.update(w1_out=_ReduceScatter("g_w1_out", ["w1_out"], [g], qc_idx)),
        after_dw_in=w1_in_ready)
    grad_x, d_sh1, d_sc1, d_g1 = _norm_mod_bwd("ffn1_norm_bwd", x2, dn1, dh1, g1v, sc1, ts)

    d_ada = jnp.concatenate([d_sh1, d_sc1, d_gt1, d_sh2, d_sc2, d_gt2, d_sh3, d_sc3, d_gt3], axis=1)
    small = [d_ada, d_g1, d_gm, d_cw[:CONV_K].reshape(1, -1), d_cb, d_lg, d_lb, d_ba, d_bb, d_ls, d_g2, d_gf,
             loss_cols]
    sizes = [a.shape[1] for a in small]
    pack1 = jnp.concatenate(small, axis=1).reshape(-1)
    n1p = -(-pack1.shape[0] // (8 * LANES)) * LANES
    pack1 = jnp.pad(pack1, (0, 8 * n1p - pack1.shape[0])).reshape(8, n1p)
    g2 = _allgather_small("gather_small_grads", pack1)
    rs["w1_in"].step2()
    total = _sum_devices("sum_small_grads", g2, 8).reshape(-1)
    offs = [0]
    for sz in sizes:
        offs.append(offs[-1] + sz)
    tot = [total[offs[k]:offs[k + 1]] for k in range(len(sizes))]
    d_ada_all = g2.reshape(N_DEV, 8 * n1p)[:, :sizes[0]]
    loss = jnp.sum(tot[12])

    grads = {}
    grads["b_ada"] = tot[0]
    grads["g_ffn1"], grads["g_mix"] = tot[1], tot[2]
    grads["conv_w"] = lax.dynamic_slice(tot[3].reshape(CONV_K, wc), (0, q * cwq), (CONV_K, cwq))
    grads["conv_b"], grads["ln_a_g"], grads["ln_a_b"], grads["b_a_out"] = tot[4], tot[5], tot[6], tot[7]
    grads["b_b_group"] = lax.dynamic_slice(tot[8].reshape(n_groups, N_CHIPS * goq), (0, q * goq), (n_groups, goq))
    grads["ls_b"], grads["g_ffn2"], grads["g_final"] = tot[9], tot[10], tot[11]

    delta, new_m, new_v = {}, {}, {}

    def adamw_group(reduced):
        for nm, g in reduced.items():
            shp = weights[nm].shape
            go, dl, mo, vo = _adamw(f"adamw_{nm}", as2d(weights[nm]), g, as2d(mom1[nm]), as2d(mom2[nm]))
            grads[nm], delta[nm], new_m[nm], new_v[nm] = go.reshape(shp), dl.reshape(shp), mo.reshape(shp), vo.reshape(shp)

    adamw_group(rs["ffn2"].result())
    rs["w1_out"].step3()
    adamw_group(rs["mix"].result())
    d_ada_mine = lax.dynamic_slice(d_ada_all, (0, q * ada_c), (N_DEV, ada_c))
    grads["w_ada"], delta["w_ada"], new_m["w_ada"], new_v["w_ada"] = _ada_grad_adamw(
        "adamw_w_ada", c_all.T, d_ada_mine, w_ada, m_w_ada, v_w_ada)
    rs["w1_in"].step3()
    smalls = [nm for nm in order if nm not in big and nm != "w_ada"]
    flat = lambda src: jnp.concatenate([src[nm].reshape(-1) for nm in smalls])
    n_small = sum(weights[nm].size for nm in smalls)
    rows_s = -(-n_small // (8 * LANES)) * 8
    packed = [jnp.pad(flat(src), (0, rows_s * LANES - n_small)).reshape(rows_s, LANES)
              for src in (weights, grads, mom1, mom2)]
    _, dl_s, mo_s, vo_s = _adamw("adamw_small", *packed)
    off = 0
    for nm in smalls:
        sz, shp = weights[nm].size, weights[nm].shape
        delta[nm] = dl_s.reshape(-1)[off:off + sz].reshape(shp)
        new_m[nm] = mo_s.reshape(-1)[off:off + sz].reshape(shp)
        new_v[nm] = vo_s.reshape(-1)[off:off + sz].reshape(shp)
        grads[nm] = grads[nm].reshape(shp)
        off += sz
    adamw_group(rs["w1_out"].result())
    adamw_group(rs["w1_in"].result())

    return (loss, grad_x[None], *[grads[nm] for nm in order], *[delta[nm] for nm in order],
            *[new_m[nm] for nm in order], *[new_v[nm] for nm in order])
```

```python
import jax
import jax.numpy as jnp
from jax import lax
from jax.experimental import pallas as pl
from jax.experimental.pallas import tpu as pltpu

F32 = jnp.float32
BF16 = jnp.bfloat16
MESH = pl.DeviceIdType.MESH
ANY = pl.BlockSpec(memory_space=pl.ANY)
HBM = pl.BlockSpec(memory_space=pltpu.HBM)
SEM = pl.BlockSpec(memory_space=pltpu.SEMAPHORE)
EFFECT = pltpu.SideEffectType.DATAFLOW_SIDE_EFFECTING

EPS = 1e-6
CONV_K = 31
HALO = 32
POOL_WINDOWS = (2, 4, 8, 16)
N_CHIPS = 4
N_DEV = 8
LANES = 128

ADAM_LR = 0.001
ADAM_B1 = 0.9
ADAM_B2 = 0.999
ADAM_EPS = 1e-08
ADAM_WD = 0.01
ADAM_STEP = 10

DN = {
    "nn": (((1,), (0,)), ((), ())),
    "nt": (((1,), (1,)), ((), ())),
    "tn": (((0,), (0,)), ((), ())),
}


_PREVIOUS = []


def _ordered(call, args, n_lead, body, token=None, sources=()):
    dep = [pltpu.with_memory_space_constraint(p, pltpu.HBM) if p.size * p.dtype.itemsize >= (1 << 20) else p
           for p in _PREVIOUS if all(p is not a for a in (*args, *sources))]

    def wrapped(*refs):
        return body(*refs[:n_lead], *refs[n_lead + len(dep):])

    outs = call(wrapped, [ANY] * len(dep))(*args, *dep)
    seq = outs if isinstance(outs, (list, tuple)) else [outs]
    _PREVIOUS[:] = [seq[token] if token is not None else
                    next(o for o in seq if jnp.issubdtype(o.dtype, jnp.floating))]
    return outs


def _pcall(body, *, name, out_shape, grid=None, in_specs=None, out_specs=None, scratch=(), aliases=None,
           prefetch=0, vmem_mb=None):
    params = {}
    if grid is not None:
        params["dimension_semantics"] = ("arbitrary",) * len(grid)
    if vmem_mb is not None:
        params["vmem_limit_bytes"] = vmem_mb << 20
    def in_hbm(shape, spec):
        big = shape.size * jnp.dtype(shape.dtype).itemsize >= (1 << 20)
        return pltpu.HBM(shape.shape, shape.dtype) if big and getattr(spec, "memory_space", None) != pltpu.VMEM else shape

    if isinstance(out_shape, (list, tuple)):
        out_shape = [in_hbm(s, sp) for s, sp in zip(out_shape, out_specs)]
    else:
        out_shape = in_hbm(out_shape, out_specs)
    kw = dict(name=name, out_shape=out_shape, compiler_params=pltpu.CompilerParams(**params))
    if aliases:
        kw["input_output_aliases"] = aliases

    def call(wrapped, dep_specs):
        specs = list(in_specs) + dep_specs
        if prefetch:
            return pl.pallas_call(wrapped, grid_spec=pltpu.PrefetchScalarGridSpec(
                num_scalar_prefetch=prefetch, grid=grid, in_specs=specs, out_specs=out_specs,
                scratch_shapes=list(scratch)), **kw)
        if grid is not None:
            return pl.pallas_call(wrapped, grid=grid, in_specs=specs, out_specs=out_specs,
                                  scratch_shapes=list(scratch), **kw)
        return pl.pallas_call(wrapped, in_specs=specs, out_specs=out_specs, scratch_shapes=list(scratch), **kw)

    def run(*args):
        specs = [None] * prefetch + list(in_specs)
        placed = [pltpu.with_memory_space_constraint(a, pltpu.HBM)
                  if a.size * a.dtype.itemsize >= (1 << 20) and getattr(s, "memory_space", None) != pltpu.VMEM else a
                  for a, s in zip(args, specs)]
        return _ordered(call, placed, prefetch + len(in_specs), body, sources=args)

    return run


def _tile(dim, pref):
    t = min(dim, pref)
    assert dim % t == 0, (dim, pref)
    return t


def _sds(shape, dtype):
    return jax.ShapeDtypeStruct(tuple(shape), dtype)


def _sigmoid(v):
    return 0.5 * jnp.tanh(0.5 * v) + 0.5


def _vec(w):
    return pl.BlockSpec((1, w), lambda *_: (0, 0))


def _acc_rows(ref, val, i):
    @pl.when(i == 0)
    def _():
        ref[...] = jnp.zeros_like(ref)

    ref[...] += jnp.sum(val, axis=0, keepdims=True)


def _matmul(name, a, bs, *, mode, grid, a_spec, b_specs, out_shape, out_specs, acc_shape, epilogue,
            extras=(), extra_specs=(), vmem_mb=56, carry=()):
    nb, ne, nk, nc = len(bs), len(extras), grid[2], len(carry)
    dn = DN[mode]

    def body(*all_refs):
        refs = all_refs[:1 + nb + ne] + all_refs[1 + nb + ne + nc:]
        a_ref, b_refs, ex = refs[0], refs[1:1 + nb], refs[1 + nb:1 + nb + ne]
        if nk == 1:
            outs = refs[1 + nb + ne:]
            accs = [lax.dot_general(a_ref[...], b[...], dn, preferred_element_type=F32) for b in b_refs]
            epilogue(accs, ex, outs)
            return
        outs, acc_refs = refs[1 + nb + ne:-nb], refs[-nb:]
        k = pl.program_id(2)

        @pl.when(k == 0)
        def _():
            for acc in acc_refs:
                acc[...] = jnp.zeros_like(acc)

        for acc, b in zip(acc_refs, b_refs):
            acc[...] += lax.dot_general(a_ref[...], b[...], dn, preferred_element_type=F32)

        @pl.when(k == nk - 1)
        def _():
            epilogue([acc[...] for acc in acc_refs], ex, outs)

    scratch = [pltpu.VMEM(acc_shape, F32) for _ in range(nb)] if nk > 1 else []
    return _pcall(body, name=name, out_shape=out_shape, grid=grid,
                  in_specs=[a_spec, *b_specs, *extra_specs, *[ANY] * nc], out_specs=out_specs, scratch=scratch,
                  aliases={1 + nb + ne + i: i for i in range(nc)}, vmem_mb=vmem_mb)(a, *bs, *extras, *carry)


def _ep_store(dtype):
    def ep(accs, ex, outs):
        outs[0][...] = accs[0].astype(dtype)
    return ep


def _ep_halves(h):
    def ep(accs, ex, outs):
        outs[0][0] = accs[0][:h]
        outs[0][1] = accs[0][h:]
    return ep


def _place():
    x, y, c = lax.axis_index("x"), lax.axis_index("y"), lax.axis_index("c")
    chips = [(1 - x, y), (x, 1 - y), (1 - x, 1 - y)]
    return x, y, c, chips


def _allgather_small(name, block):
    m_per, n = block.shape

    def body(x_ref, out_ref, send_sems, recv_sems, local_sem):
        x, y, c, chips = _place()
        me, sibling = (x, y, c), (x, y, 1 - c)

        def rows(px, py, pc):
            return out_ref.at[pl.ds((4 * px + 2 * py + pc) * m_per, m_per), :]

        def copy(k, blk, to, src=None):
            return pltpu.make_async_remote_copy(
                src_ref=rows(*blk) if src is None else src, dst_ref=rows(*blk),
                send_sem=send_sems.at[k], recv_sem=recv_sems.at[k], device_id=to, device_id_type=MESH)

        mine = pltpu.make_async_copy(x_ref, rows(*me), local_sem)
        mine.start()
        first = [copy(0, me, sibling, src=x_ref)]
        first += [copy(1 + j, me, (*chip, c), src=x_ref) for j, chip in enumerate(chips)]
        for cp in first:
            cp.start()
        passed = [copy(4 + j, (*chip, c), sibling) for j, chip in enumerate(chips)]
        for j, chip in enumerate(chips):
            copy(1 + j, (*chip, c), me).wait_recv()
            passed[j].start()
        copy(0, sibling, me).wait_recv()
        for j, chip in enumerate(chips):
            copy(4 + j, (*chip, 1 - c), me).wait_recv()
        for cp in first + passed:
            cp.wait_send()
        mine.wait()

    return _pcall(
        body, name=name, out_shape=_sds((N_DEV * m_per, n), block.dtype),
        in_specs=[pl.BlockSpec(memory_space=pltpu.VMEM)], out_specs=pl.BlockSpec(memory_space=pltpu.VMEM),
        scratch=[pltpu.SemaphoreType.DMA((7,)), pltpu.SemaphoreType.DMA((7,)), pltpu.SemaphoreType.DMA],
    )(block)


class _SplitCopies:
    def __init__(self, name, arrays, plan, n_copies):
        self.name, self.plan, self.n = name, plan, len(arrays)
        n = self.n

        def body(*refs):
            send, recv, token = refs[n], refs[n + 1], refs[-1]
            for k, (src, dst, _, peer) in enumerate(plan(refs[:n])):
                pltpu.make_async_remote_copy(src_ref=src, dst_ref=dst, send_sem=send.at[k], recv_sem=recv.at[k],
                                             device_id=peer, device_id_type=MESH).start()
            token[...] = jnp.zeros_like(token)

        def call(wrapped, dep_specs):
            return pl.pallas_call(
                wrapped, name=f"{name}_start",
                out_shape=(pltpu.SemaphoreType.DMA((n_copies,)), pltpu.SemaphoreType.DMA((n_copies,)),
                           *[pltpu.HBM(a.shape, a.dtype) for a in arrays], _sds((8, LANES), F32)),
                in_specs=[HBM] * n + dep_specs,
                out_specs=(SEM, SEM, *[HBM] * n, pl.BlockSpec(memory_space=pltpu.VMEM)),
                input_output_aliases={i: 2 + i for i in range(n)},
                compiler_params=pltpu.CompilerParams(has_side_effects=EFFECT))

        outs = _ordered(call, [pltpu.with_memory_space_constraint(a, pltpu.HBM) for a in arrays], n, body, token=-1,
                        sources=arrays)
        self.send, self.recv, self.arrays = outs[0], outs[1], list(outs[2:2 + n])

    def wait(self):
        n, plan = self.n, self.plan

        def body(*refs):
            send, recv, token = refs[n], refs[n + 1], refs[-1]
            for k, (src, _, landing, peer) in enumerate(plan(refs[:n])):
                cp = pltpu.make_async_remote_copy(src_ref=src, dst_ref=landing, send_sem=send.at[k],
                                                  recv_sem=recv.at[k], device_id=peer, device_id_type=MESH)
                cp.wait_send()
                cp.wait_recv()
            token[...] = jnp.zeros_like(token)

        def call(wrapped, dep_specs):
            return pl.pallas_call(
                wrapped, name=f"{self.name}_wait",
                out_shape=(*[pltpu.HBM(a.shape, a.dtype) for a in self.arrays], _sds((8, LANES), F32)),
                in_specs=[HBM] * n + [SEM, SEM] + dep_specs,
                out_specs=(*[HBM] * n, pl.BlockSpec(memory_space=pltpu.VMEM)),
                input_output_aliases={i: i for i in range(n)},
                compiler_params=pltpu.CompilerParams(has_side_effects=EFFECT))

        return list(_ordered(call, [*self.arrays, self.send, self.recv], n + 2, body, token=-1))[:n]


def _gather_ici(name, gathered):
    def plan(refs):
        x, y, c, chips = _place()
        q = 2 * x + y
        return [(g.at[q, c], g.at[q, c], g.at[2 * px + py, c], (px, py, c)) for g in refs for px, py in chips]

    return _SplitCopies(name, gathered, plan, 3 * len(gathered))


def _gather_d2d(name, gathered):
    def plan(refs):
        x, y, c, chips = _place()
        return [(g.at[2 * px + py, c], g.at[2 * px + py, c], g.at[2 * px + py, 1 - c], (x, y, 1 - c))
                for g in refs for px, py in chips]

    return _SplitCopies(name, gathered, plan, 3 * len(gathered))


def _scatter_sibling(name, grads):
    n = len(grads)

    def plan(refs):
        x, y, c, _ = _place()
        return [(refs[w].at[1 - c], refs[n + w], refs[n + w], (x, y, 1 - c)) for w in range(n)]

    landing = [lax.empty(g.shape[1:], g.dtype) for g in grads]
    return _SplitCopies(name, [*grads, *landing], plan, n)


def _scatter_chips(name, sums):
    n = len(sums)

    def plan(refs):
        x, y, c, chips = _place()
        return [(refs[w].at[2 * px + py], refs[n + w].at[j], refs[n + w].at[j], (px, py, c))
                for w in range(n) for j, (px, py) in enumerate(chips)]

    landing = [lax.empty((3, *s.shape[1:]), s.dtype) for s in sums]
    return _SplitCopies(name, [*sums, *landing], plan, 3 * n)


def _share_final(name, finals):
    def plan(refs):
        x, y, c, _ = _place()
        return [(f.at[c], f.at[c], f.at[1 - c], (x, y, 1 - c)) for f in refs]

    return _SplitCopies(name, finals, plan, len(finals))


def _row_tile(rows, cols, budget_elems=393216):
    best = 8
    for t in range(8, rows + 1, 8):
        if rows % t == 0 and t * cols <= budget_elems:
            best = t
    return best if rows % best == 0 else rows


def _sum_with_sibling(name, grad, recv, qc_idx):
    _, _, h, cols = grad.shape
    tr = _row_tile(h, cols)

    def body(s_ref, g_ref, r_ref, own_ref, pb_ref):
        p = g_ref[...] + r_ref[...]
        pb_ref[...] = p.astype(BF16)

        @pl.when(pl.program_id(1) == s_ref[0])
        def _():
            own_ref[...] = p

    blk = pl.BlockSpec((None, tr, cols), lambda r, k, s: (k, r, 0))
    return _pcall(
        body, name=name, out_shape=[_sds((h, cols), F32), _sds((N_CHIPS, h, cols), BF16)],
        grid=(h // tr, N_CHIPS), prefetch=1,
        in_specs=[pl.BlockSpec((None, None, tr, cols), lambda r, k, s: (s[1], k, r, 0)), blk],
        out_specs=[pl.BlockSpec((tr, cols), lambda r, k, s: (r, 0)), blk], vmem_mb=32,
    )(qc_idx, grad, recv)


def _sum_chips(name, own, recv, qc_idx):
    h, cols = own.shape
    tr = _row_tile(h, cols)

    def body(s_ref, p_ref, t_ref, o_ref):
        o_ref[...] = ((p_ref[...] + t_ref[0].astype(F32)) + t_ref[1].astype(F32)) + t_ref[2].astype(F32)

    return _pcall(
        body, name=name, out_shape=_sds((2, h, cols), F32), grid=(h // tr,), prefetch=1,
        in_specs=[pl.BlockSpec((tr, cols), lambda r, s: (r, 0)),
                  pl.BlockSpec((3, tr, cols), lambda r, s: (0, r, 0))],
        out_specs=pl.BlockSpec((None, tr, cols), lambda r, s: (s[1], r, 0)), vmem_mb=32,
    )(qc_idx, own, recv)


class _ReduceScatter:
    def __init__(self, tag, names, grads, qc_idx):
        self.tag, self.names, self.n, self.qc_idx = tag, names, len(grads), qc_idx
        self.copies = _scatter_sibling(f"{tag}_rs_sibling", grads)

    def step2(self):
        n = self.n
        arrs = self.copies.wait()
        sums = [_sum_with_sibling(f"{nm}_sum_sibling", arrs[w], arrs[n + w], self.qc_idx)
                for w, nm in enumerate(self.names)]
        self.own = [s[0] for s in sums]
        self.copies = _scatter_chips(f"{self.tag}_rs_chips", [s[1] for s in sums])

    def step3(self):
        n = self.n
        arrs = self.copies.wait()
        finals = [_sum_chips(f"{nm}_sum_chips", self.own[w], arrs[n + w], self.qc_idx)
                  for w, nm in enumerate(self.names)]
        self.copies = _share_final(f"{self.tag}_rs_final", finals)

    def result(self):
        return {nm: f.reshape(2 * f.shape[1], f.shape[2]) for nm, f in zip(self.names, self.copies.wait())}


def _cast_into_gathered(name, w, q_idx, part=0, n_parts=1):
    rows, cols = w.shape[0], w.shape[1] // n_parts
    h = rows // 2
    tr = _row_tile(h, cols, 1 << 20)
    nr = h // tr

    def body(s_ref, w_ref, o_ref):
        o_ref[...] = w_ref[...].astype(BF16)

    return _pcall(body, name=name, out_shape=_sds((N_CHIPS, 2, h, cols), BF16), grid=(2, nr), prefetch=1,
                  in_specs=[pl.BlockSpec((tr, cols), lambda hf, r, s: (hf * nr + r, part))],
                  out_specs=pl.BlockSpec((None, None, tr, cols), lambda hf, r, s: (s[0], hf, r, 0)),
                  vmem_mb=32)(q_idx, w)


def _rms(h):
    r = lax.rsqrt(jnp.mean(h * h, axis=-1, keepdims=True) + EPS)
    return r, h * r


def _norm_mod(name, h, g, sc, sh, ts):
    s_len, d = h.shape

    def body(h_ref, g_ref, sc_ref, sh_ref, n_ref):
        _, xhat = _rms(h_ref[...])
        n_ref[...] = ((xhat * g_ref[...]) * (1.0 + sc_ref[...]) + sh_ref[...]).astype(BF16)

    row = pl.BlockSpec((ts, d), lambda i: (i, 0))
    return _pcall(body, name=name, out_shape=_sds((s_len, d), BF16), grid=(s_len // ts,),
                  in_specs=[row, _vec(d), _vec(d), _vec(d)], out_specs=row, vmem_mb=32)(h, g, sc, sh)


def _residual_norm_mod(name, h, f, gate, cmul, g, sc, sh, ts):
    s_len, d = h.shape

    def body(h_ref, f_ref, gt_ref, g_ref, sc_ref, sh_ref, ho_ref, n_ref):
        hn = h_ref[...] + (cmul * gt_ref[...]) * f_ref[...]
        ho_ref[...] = hn
        _, xhat = _rms(hn)
        n_ref[...] = ((xhat * g_ref[...]) * (1.0 + sc_ref[...]) + sh_ref[...]).astype(BF16)

    row = pl.BlockSpec((ts, d), lambda i: (i, 0))
    return _pcall(body, name=name, out_shape=[_sds((s_len, d), F32), _sds((s_len, d), BF16)],
                  grid=(s_len // ts,), in_specs=[row, row, _vec(d), _vec(d), _vec(d), _vec(d)],
                  out_specs=[row, row], vmem_mb=32)(h, f, gate, g, sc, sh)


def _final_loss(name, h, f, tgt, gate, cmul, g, ts):
    s_len, d = h.shape

    def body(h_ref, f_ref, t_ref, gt_ref, g_ref, dh_ref, df_ref, dg_ref, dgt_ref, loss_ref):
        i = pl.program_id(0)
        fv = f_ref[...]
        coef = cmul * gt_ref[...]
        hn = h_ref[...] + coef * fv
        r, xhat = _rms(hn)
        err = xhat * g_ref[...] - t_ref[...]
        _acc_rows(loss_ref, (0.5 / d) * (err * err), i)
        dy = err * (1.0 / d)
        _acc_rows(dg_ref, dy * xhat, i)
        dxhat = dy * g_ref[...]
        dh = r * (dxhat - xhat * jnp.mean(dxhat * xhat, axis=-1, keepdims=True))
        dh_ref[...] = dh
        _acc_rows(dgt_ref, cmul * (dh * fv), i)
        df_ref[...] = (coef * dh).astype(BF16)

    row = pl.BlockSpec((ts, d), lambda i: (i, 0))
    return _pcall(body, name=name,
                  out_shape=[_sds((s_len, d), F32), _sds((s_len, d), BF16)] + [_sds((1, d), F32)] * 3,
                  grid=(s_len // ts,), in_specs=[row, row, row, _vec(d), _vec(d)],
                  out_specs=[row, row, _vec(d), _vec(d), _vec(d)], vmem_mb=40)(h, f, tgt, gate, g)


def _norm_mod_bwd(name, h, dn, dh_next, g, sc, ts, prev=None):
    s_len, d = h.shape
    has_prev = prev is not None
    cmul = prev[2] if has_prev else None

    def body(*refs):
        if has_prev:
            h_ref, dn_ref, dhn_ref, f_ref, g_ref, sc_ref, gt_ref, dh_ref, df_ref, dsh_ref, dsc_ref, dg_ref, dgt_ref = refs
        else:
            h_ref, dn_ref, dhn_ref, g_ref, sc_ref, dh_ref, dsh_ref, dsc_ref, dg_ref = refs
        i = pl.program_id(0)
        r, xhat = _rms(h_ref[...])
        dn_v = dn_ref[...]
        gv = g_ref[...]
        _acc_rows(dsh_ref, dn_v, i)
        _acc_rows(dsc_ref, dn_v * (xhat * gv), i)
        dnrm = dn_v * (1.0 + sc_ref[...])
        _acc_rows(dg_ref, dnrm * xhat, i)
        dxhat = dnrm * gv
        dh = dhn_ref[...] + r * (dxhat - xhat * jnp.mean(dxhat * xhat, axis=-1, keepdims=True))
        dh_ref[...] = dh
        if has_prev:
            _acc_rows(dgt_ref, cmul * (dh * f_ref[...]), i)
            df_ref[...] = ((cmul * gt_ref[...]) * dh).astype(BF16)

    row = pl.BlockSpec((ts, d), lambda i: (i, 0))
    if has_prev:
        ins, in_specs = [h, dn, dh_next, prev[0], g, sc, prev[1]], [row, row, row, row, _vec(d), _vec(d), _vec(d)]
        out_shape = [_sds((s_len, d), F32), _sds((s_len, d), BF16)] + [_sds((1, d), F32)] * 4
        out_specs = [row, row] + [_vec(d)] * 4
    else:
        ins, in_specs = [h, dn, dh_next, g, sc], [row, row, row, _vec(d), _vec(d)]
        out_shape = [_sds((s_len, d), F32)] + [_sds((1, d), F32)] * 3
        out_specs = [row] + [_vec(d)] * 3
    return _pcall(body, name=name, out_shape=out_shape, grid=(s_len // ts,), in_specs=in_specs,
                  out_specs=out_specs, vmem_mb=40)(*ins)


def _cols(ref, lo, hi, npc, rows=slice(None)):
    parts = []
    while lo < hi:
        q, o = divmod(lo, npc)
        n = min(hi - lo, npc - o)
        parts.append(ref[q, rows, o:o + n].astype(F32))
        lo += n
    return parts[0] if len(parts) == 1 else jnp.concatenate(parts, axis=-1)


def _store_cols(ref, lo, val, npc, rows=slice(None)):
    off, width = 0, val.shape[-1]
    while off < width:
        q, o = divmod(lo + off, npc)
        n = min(width - off, npc - o)
        ref[q, rows, o:o + n] = val[:, off:off + n]
        off += n


def _chips_covering(cols, npc):
    return -(-cols // npc)


SUBLANES = 8
ROW_CHUNK = 32


def _make_phases(src_ref, ph_ref):
    rows = src_ref.shape[0] - SUBLANES
    for b in range(1, SUBLANES):
        ph_ref[b - 1] = src_ref[pl.ds(b, rows), :]


def _window(src_ref, ph_ref, off, r0, cols=slice(None)):
    a, b = divmod(off, SUBLANES)
    start = pl.multiple_of(r0 + SUBLANES * a, SUBLANES)
    if b == 0:
        return src_ref[pl.ds(start, ROW_CHUNK), cols]
    return ph_ref[b - 1, pl.ds(start, ROW_CHUNK), cols]


def _phase_scratch(rows, width):
    return pltpu.VMEM((SUBLANES - 1, rows - SUBLANES, width), F32)


def _conv_ln(a0s_ref, a0p_ref, cw_ref, cb_ref, lg_ref, lb_ref, r0):
    a1 = cb_ref[...] + cw_ref[0:1, :] * _window(a0s_ref, a0p_ref, HALO - CONV_K + 1, r0)
    for k in range(1, CONV_K):
        a1 = a1 + cw_ref[k:k + 1, :] * _window(a0s_ref, a0p_ref, HALO - CONV_K + 1 + k, r0)
    mu = jnp.mean(a1, axis=-1, keepdims=True)
    ctr = a1 - mu
    rstd = lax.rsqrt(jnp.mean(ctr * ctr, axis=-1, keepdims=True) + EPS)
    xh = ctr * rstd
    return xh, rstd, xh * lg_ref[...] + lb_ref[...]


def _for_chunks(ts, fn):
    def step(ci, carry):
        fn(pl.multiple_of(ci * ROW_CHUNK, ROW_CHUNK))
        return carry

    lax.fori_loop(0, ts // ROW_CHUNK, step, 0)


def _stage_glu(p_ref, ph_ref, a0s_ref, i, wc, npc, ts):
    a0 = _cols(p_ref, 0, wc, npc) * _sigmoid(_cols(p_ref, wc, 2 * wc, npc))
    a0h = _cols(ph_ref, 0, wc, npc) * _sigmoid(_cols(ph_ref, wc, 2 * wc, npc))
    a0s_ref[0:HALO, :] = jnp.where(i > 0, a0h, 0.0)
    a0s_ref[HALO:HALO + ts, :] = a0


def _mixer_mid(name, proj, cw, cb, lg, lb, wc, wp, ts):
    _, s_len, npc = proj.shape
    nq = _chips_covering(2 * wc + wp, npc)
    gi = wp // len(POOL_WINDOWS)
    hb = ts // HALO

    def body(p_ref, ph_ref, cw_ref, cb_ref, lg_ref, lb_ref, a3_ref, mx_ref, a0s_ref, vs_ref, a0p_ref, vp_ref):
        i = pl.program_id(0)
        _stage_glu(p_ref, ph_ref, a0s_ref, i, wc, npc, ts)
        vs_ref[0:HALO, :] = jnp.where(i > 0, _cols(ph_ref, 2 * wc, 2 * wc + wp, npc), 0.0)
        vs_ref[HALO:HALO + ts, :] = _cols(p_ref, 2 * wc, 2 * wc + wp, npc)
        _make_phases(a0s_ref, a0p_ref)
        _make_phases(vs_ref, vp_ref)

        def chunk(r0):
            rows = pl.ds(r0, ROW_CHUNK)
            _, _, a2 = _conv_ln(a0s_ref, a0p_ref, cw_ref, cb_ref, lg_ref, lb_ref, r0)
            a3_ref[rows, :] = (a2 * _sigmoid(a2)).astype(BF16)
            t_abs = i * ts + r0 + lax.broadcasted_iota(jnp.int32, (ROW_CHUNK, 1), 0)
            for g, win in enumerate(POOL_WINDOWS):
                cs = slice(g * gi, (g + 1) * gi)
                v_now = _window(vs_ref, vp_ref, HALO, r0, cs)
                acc = v_now
                for dlt in range(1, win):
                    acc = acc + _window(vs_ref, vp_ref, HALO - dlt, r0, cs)
                cnt = jnp.minimum(t_abs + 1, win).astype(F32)
                mx_ref[rows, cs] = (acc / cnt - v_now).astype(BF16)

        _for_chunks(ts, chunk)

    return _pcall(
        body, name=name, out_shape=[_sds((s_len, wc), BF16), _sds((s_len, wp), BF16)], grid=(s_len // ts,),
        in_specs=[pl.BlockSpec((nq, ts, npc), lambda i: (0, i, 0)),
                  pl.BlockSpec((nq, HALO, npc), lambda i: (0, jnp.maximum(i * hb - 1, 0), 0)),
                  pl.BlockSpec((HALO, wc), lambda i: (0, 0)), _vec(wc), _vec(wc), _vec(wc)],
        out_specs=[pl.BlockSpec((ts, wc), lambda i: (i, 0)), pl.BlockSpec((ts, wp), lambda i: (i, 0))],
        scratch=[pltpu.VMEM((HALO + ts, wc), F32), pltpu.VMEM((HALO + ts, wp), F32),
                 _phase_scratch(HALO + ts, wc), _phase_scratch(HALO + ts, wp)], vmem_mb=56,
    )(proj, proj, cw, cb, lg, lb)


def _gates_fwd(name, proj, ya, yb, b_a, b_b, ls, wc, wp, ts):
    _, s_len, npc = proj.shape
    d = ya.shape[1]
    g0 = 2 * wc + wp

    def body(p_ref, ya_ref, yb_ref, ba_ref, bb_ref, ls_ref, z_ref):
        ga = _sigmoid(_cols(p_ref, g0, g0 + d, npc))
        gb = _sigmoid(_cols(p_ref, g0 + d, g0 + 2 * d, npc))
        z = ga * (ya_ref[...] + ba_ref[...]) + gb * ((yb_ref[...] + bb_ref[...]) * ls_ref[...])
        z_ref[...] = z.astype(BF16)

    row = pl.BlockSpec((ts, d), lambda i: (i, 0))
    return _pcall(body, name=name, out_shape=_sds((s_len, d), BF16), grid=(s_len // ts,),
                  in_specs=[pl.BlockSpec((N_CHIPS, ts, npc), lambda i: (0, i, 0)), row, row, _vec(d), _vec(d), _vec(d)],
                  out_specs=row, vmem_mb=48)(proj, ya, yb, b_a, b_b, ls)


def _gates_bwd(name, proj, dz, ya, yb, b_a, b_b, ls, wc, wp, ts):
    _, s_len, npc = proj.shape
    d = ya.shape[1]
    g0 = 2 * wc + wp

    def body(p_ref, dz_ref, ya_ref, yb_ref, ba_ref, bb_ref, ls_ref, dya_ref, dyb_ref, dgt_ref, dba_ref, dls_ref,
             dbb_ref):
        i = pl.program_id(0)
        ga = _sigmoid(_cols(p_ref, g0, g0 + d, npc))
        gb = _sigmoid(_cols(p_ref, g0 + d, g0 + 2 * d, npc))
        dz_v = dz_ref[...]
        y_a = ya_ref[...] + ba_ref[...]
        y_b0 = yb_ref[...] + bb_ref[...]
        ls_v = ls_ref[...]
        dya = dz_v * ga
        dya_ref[...] = dya.astype(BF16)
        _acc_rows(dba_ref, dya, i)
        t = dz_v * gb
        _acc_rows(dls_ref, t * y_b0, i)
        dyb = t * ls_v
        dyb_ref[...] = dyb.astype(BF16)
        _acc_rows(dbb_ref, dyb, i)
        dgt_ref[:, 0:d] = (dz_v * y_a * ga * (1.0 - ga)).astype(BF16)
        dgt_ref[:, d:2 * d] = (dz_v * (y_b0 * ls_v) * gb * (1.0 - gb)).astype(BF16)

    row = pl.BlockSpec((ts, d), lambda i: (i, 0))
    return _pcall(
        body, name=name,
        out_shape=[_sds((s_len, d), BF16), _sds((s_len, d), BF16), _sds((s_len, 2 * d), BF16)] + [_sds((1, d), F32)] * 3,
        grid=(s_len // ts,),
        in_specs=[pl.BlockSpec((N_CHIPS, ts, npc), lambda i: (0, i, 0)), row, row, row, _vec(d), _vec(d), _vec(d)],
        out_specs=[row, row, pl.BlockSpec((ts, 2 * d), lambda i: (i, 0))] + [_vec(d)] * 3, vmem_mb=48,
    )(proj, dz, ya, yb, b_a, b_b, ls)


def _conv_branch_bwd(name, proj, da3, cw, cb, lg, lb, wc, wp, ts):
    _, s_len, npc = proj.shape
    nq = _chips_covering(2 * wc, npc)
    hb = ts // HALO

    n_tiles = s_len // ts

    def fold(v):
        return jnp.sum(v.reshape(ROW_CHUNK // SUBLANES, SUBLANES, v.shape[-1]), axis=0)

    def body(p_ref, ph_ref, da3_ref, cw_ref, cb_ref, lg_ref, lb_ref, da1_ref, dlg_ref, dlb_ref, dcb_ref, dcw_ref,
             a0s_ref, a0p_ref, vec8_ref, dcw8_ref):
        i = pl.program_id(0)
        _stage_glu(p_ref, ph_ref, a0s_ref, i, wc, npc, ts)
        _make_phases(a0s_ref, a0p_ref)

        @pl.when(i == 0)
        def _():
            vec8_ref[...] = jnp.zeros_like(vec8_ref)
            dcw8_ref[...] = jnp.zeros_like(dcw8_ref)

        def chunk(r0):
            rows = pl.ds(r0, ROW_CHUNK)
            xh, rstd, a2 = _conv_ln(a0s_ref, a0p_ref, cw_ref, cb_ref, lg_ref, lb_ref, r0)
            sig = _sigmoid(a2)
            da2 = da3_ref[rows, :] * (sig * (1.0 + a2 * (1.0 - sig)))
            vec8_ref[0] += fold(da2 * xh)
            vec8_ref[1] += fold(da2)
            dxh = da2 * lg_ref[...]
            da1 = rstd * (dxh - jnp.mean(dxh, axis=-1, keepdims=True)
                          - xh * jnp.mean(dxh * xh, axis=-1, keepdims=True))
            da1_ref[rows, :] = da1
            vec8_ref[2] += fold(da1)
            for k in range(CONV_K):
                dcw8_ref[k] += fold(da1 * _window(a0s_ref, a0p_ref, HALO - CONV_K + 1 + k, r0))

        _for_chunks(ts, chunk)

        @pl.when(i == n_tiles - 1)
        def _():
            dlg_ref[...] = jnp.sum(vec8_ref[0], axis=0, keepdims=True)
            dlb_ref[...] = jnp.sum(vec8_ref[1], axis=0, keepdims=True)
            dcb_ref[...] = jnp.sum(vec8_ref[2], axis=0, keepdims=True)
            dcw_ref[...] = jnp.sum(dcw8_ref[...], axis=1)

    return _pcall(
        body, name=name,
        out_shape=[_sds((s_len, wc), F32)] + [_sds((1, wc), F32)] * 3 + [_sds((HALO, wc), F32)],
        grid=(s_len // ts,),
        in_specs=[pl.BlockSpec((nq, ts, npc), lambda i: (0, i, 0)),
                  pl.BlockSpec((nq, HALO, npc), lambda i: (0, jnp.maximum(i * hb - 1, 0), 0)),
                  pl.BlockSpec((ts, wc), lambda i: (i, 0)),
                  pl.BlockSpec((HALO, wc), lambda i: (0, 0)), _vec(wc), _vec(wc), _vec(wc)],
        out_specs=[pl.BlockSpec((ts, wc), lambda i: (i, 0)), _vec(wc), _vec(wc), _vec(wc),
                   pl.BlockSpec((HALO, wc), lambda i: (0, 0))],
        scratch=[pltpu.VMEM((HALO + ts, wc), F32), _phase_scratch(HALO + ts, wc),
                 pltpu.VMEM((3, SUBLANES, wc), F32), pltpu.VMEM((HALO, SUBLANES, wc), F32)], vmem_mb=56,
    )(proj, proj, da3, cw, cb, lg, lb)


def _mixer_in_bwd(name, proj, da1, dmixed, dgates, cw, wc, wp, ts):
    _, s_len, npc = proj.shape
    nq = _chips_covering(2 * wc, npc)
    gi = wp // len(POOL_WINDOWS)
    hb = ts // HALO
    n_tiles = s_len // ts
    last_hb = s_len // HALO - 1
    d2 = dgates.shape[1]

    def body(p_ref, d1_ref, d1n_ref, dm_ref, dmn_ref, dgt_ref, cw_ref, o_ref, d1s_ref, es_ref, d1p_ref, ep_ref):
        i = pl.program_id(0)
        more = i < n_tiles - 1
        d1s_ref[0:ts, :] = d1_ref[...]
        d1s_ref[ts:ts + HALO, :] = jnp.where(more, d1n_ref[...], 0.0)
        t_abs = i * ts + lax.broadcasted_iota(jnp.int32, (ts + HALO, 1), 0)
        dm_ext = jnp.concatenate([dm_ref[...], jnp.where(more, dmn_ref[...], 0.0)], axis=0)
        for g, win in enumerate(POOL_WINDOWS):
            cs = slice(g * gi, (g + 1) * gi)
            es_ref[:, cs] = dm_ext[:, cs] / jnp.minimum(t_abs + 1, win).astype(F32)
        _make_phases(d1s_ref, d1p_ref)
        _make_phases(es_ref, ep_ref)

        def chunk(r0):
            rows = pl.ds(r0, ROW_CHUNK)
            da0 = cw_ref[0:1, :] * _window(d1s_ref, d1p_ref, CONV_K - 1, r0)
            for k in range(1, CONV_K):
                da0 = da0 + cw_ref[k:k + 1, :] * _window(d1s_ref, d1p_ref, CONV_K - 1 - k, r0)
            glu_a = _cols(p_ref, 0, wc, npc, rows)
            sig = _sigmoid(_cols(p_ref, wc, 2 * wc, npc, rows))
            _store_cols(o_ref, 0, (da0 * sig).astype(BF16), npc, rows)
            _store_cols(o_ref, wc, (da0 * glu_a * sig * (1.0 - sig)).astype(BF16), npc, rows)
            parts = []
            for g, win in enumerate(POOL_WINDOWS):
                cs = slice(g * gi, (g + 1) * gi)
                acc = _window(es_ref, ep_ref, 0, r0, cs)
                for dlt in range(1, win):
                    acc = acc + _window(es_ref, ep_ref, dlt, r0, cs)
                parts.append(acc - dm_ref[rows, cs])
            _store_cols(o_ref, 2 * wc, jnp.concatenate(parts, axis=-1).astype(BF16), npc, rows)

        _for_chunks(ts, chunk)
        _store_cols(o_ref, 2 * wc + wp, dgt_ref[...], npc)

    nxt = lambda i: (jnp.minimum((i + 1) * hb, last_hb), 0)
    return _pcall(
        body, name=name, out_shape=_sds((N_CHIPS, s_len, npc), BF16), grid=(n_tiles,),
        in_specs=[pl.BlockSpec((nq, ts, npc), lambda i: (0, i, 0)),
                  pl.BlockSpec((ts, wc), lambda i: (i, 0)), pl.BlockSpec((HALO, wc), nxt),
                  pl.BlockSpec((ts, wp), lambda i: (i, 0)), pl.BlockSpec((HALO, wp), nxt),
                  pl.BlockSpec((ts, d2), lambda i: (i, 0)),
                  pl.BlockSpec((HALO, wc), lambda i: (0, 0))],
        out_specs=pl.BlockSpec((N_CHIPS, ts, npc), lambda i: (0, i, 0)),
        scratch=[pltpu.VMEM((ts + HALO, wc), F32), pltpu.VMEM((ts + HALO, wp), F32),
                 _phase_scratch(ts + HALO, wc), _phase_scratch(ts + HALO, wp)], vmem_mb=56,
    )(proj, da1, da1, dmixed, dmixed, dgates, cw)


def _ada_fwd(name, c_all, w, b):
    d, cols = w.shape
    tn = 512 if cols % 512 == 0 else cols

    def body(c_ref, w_ref, b_ref, o_ref):
        cv = c_ref[...]
        sc = (cv * _sigmoid(cv)).astype(BF16)
        o_ref[...] = jnp.dot(sc, w_ref[...].astype(BF16), preferred_element_type=F32) + b_ref[...]

    return _pcall(body, name=name, out_shape=_sds((N_DEV, cols), F32), grid=(cols // tn,),
                  in_specs=[pl.BlockSpec((N_DEV, d), lambda j: (0, 0)), pl.BlockSpec((d, tn), lambda j: (0, j)),
                            pl.BlockSpec((1, tn), lambda j: (0, j))],
                  out_specs=pl.BlockSpec((N_DEV, tn), lambda j: (0, j)), vmem_mb=32)(c_all, w, b)


def _adam_math(w, g, m, v):
    m_new = ADAM_B1 * m + (1.0 - ADAM_B1) * g
    v_new = ADAM_B2 * v + (1.0 - ADAM_B2) * (g * g)
    m_hat = m_new / (1.0 - ADAM_B1 ** ADAM_STEP)
    v_hat = v_new / (1.0 - ADAM_B2 ** ADAM_STEP)
    delta = -ADAM_LR * (m_hat / (jnp.sqrt(v_hat) + ADAM_EPS) + ADAM_WD * w)
    return delta, m_new, v_new


def _adamw(name, w, g, m, v):
    rows, cols = w.shape
    tr = _row_tile(rows, cols, 262144)

    def body(w_ref, g_ref, m_ref, v_ref, go_ref, d_ref, mo_ref, vo_ref):
        g = g_ref[...]
        go_ref[...] = g
        d_ref[...], mo_ref[...], vo_ref[...] = _adam_math(w_ref[...], g, m_ref[...], v_ref[...])

    spec = pl.BlockSpec((tr, cols), lambda i: (i, 0))
    return _pcall(body, name=name, out_shape=[_sds(w.shape, F32)] * 4, grid=(rows // tr,), in_specs=[spec] * 4,
                  out_specs=[spec] * 4, vmem_mb=40)(w, g, m, v)


def _ada_grad_adamw(name, c_t, d_ada, w, m, v):
    rows, cols = w.shape
    tr = _tile(rows, 256)
    tc = _tile(cols, 1536) if cols % 1536 == 0 else cols

    def body(c_ref, da_ref, w_ref, m_ref, v_ref, g_ref, d_ref, mo_ref, vo_ref):
        cv = c_ref[...]
        sc = cv * _sigmoid(cv)
        g = sc[:, 0:1] * da_ref[0:1, :]
        for b in range(1, N_DEV):
            g = g + sc[:, b:b + 1] * da_ref[b:b + 1, :]
        g_ref[...] = g
        d_ref[...], mo_ref[...], vo_ref[...] = _adam_math(w_ref[...], g, m_ref[...], v_ref[...])

    spec = pl.BlockSpec((tr, tc), lambda i, j: (i, j))
    return _pcall(body, name=name, out_shape=[_sds(w.shape, F32)] * 4, grid=(rows // tr, cols // tc),
                  in_specs=[pl.BlockSpec((tr, N_DEV), lambda i, j: (i, 0)),
                            pl.BlockSpec((N_DEV, tc), lambda i, j: (0, j)), spec, spec, spec],
                  out_specs=[spec] * 4, vmem_mb=40)(c_t, d_ada, w, m, v)


def _sum_devices(name, gathered, m_per):
    n = gathered.shape[1]

    def body(g_ref, o_ref):
        acc = g_ref[0:m_per, :]
        for dev in range(1, N_DEV):
            acc = acc + g_ref[dev * m_per:(dev + 1) * m_per, :]
        o_ref[...] = acc

    return _pcall(body, name=name, out_shape=_sds((m_per, n), F32),
                  in_specs=[pl.BlockSpec(memory_space=pltpu.VMEM)],
                  out_specs=pl.BlockSpec(memory_space=pltpu.VMEM))(gathered)


def _ffn_fwd(tag, n, w_in_parts, w_out_after_swiglu, dims):
    s_len, d, f_dim = dims["S"], dims["D"], dims["F"]
    tf = f_dim // 4
    tm0, tm = _tile(s_len, 512), _tile(s_len, 1024)
    n_parts = len(w_in_parts)
    nbp = (f_dim // 2) // tf
    nbq = nbp // n_parts

    def ep(accs, ex, outs):
        hh, uu = accs
        outs[0][0] = hh.astype(BF16)
        outs[0][1] = uu.astype(BF16)
        outs[1][...] = (hh * _sigmoid(hh) * uu).astype(BF16)

    done = ()
    for part, get_w in enumerate(w_in_parts):
        w_g = get_w()
        col = lambda j, part=part: (j // nbq) * nbp + part * nbq + j % nbq
        done = _matmul(
            f"{tag}_swiglu{part}", n, [w_g, w_g], mode="nn", grid=(s_len // tm0, 2 * nbq, 1),
            a_spec=pl.BlockSpec((tm0, d), lambda i, j, k: (i, 0)),
            b_specs=[pl.BlockSpec((None, d, tf), lambda i, j, k: (j // nbq, 0, j % nbq)),
                     pl.BlockSpec((None, d, tf), lambda i, j, k: (2 + j // nbq, 0, j % nbq))],
            out_shape=[_sds((2, s_len, f_dim), BF16), _sds((s_len, f_dim), BF16)],
            out_specs=[pl.BlockSpec((2, tm0, tf), lambda i, j, k, col=col: (0, i, col(j))),
                       pl.BlockSpec((tm0, tf), lambda i, j, k, col=col: (i, col(j)))],
            acc_shape=(tm0, tf), epilogue=ep, carry=done)
    hu, act = done
    w_out2d = w_out_after_swiglu()
    tn2 = _tile(d, 1024)
    f = _matmul(
        f"{tag}_down", act, [w_out2d], mode="nn", grid=(s_len // tm, d // tn2, 2),
        a_spec=pl.BlockSpec((tm, 2 * tf), lambda i, j, k: (i, k)),
        b_specs=[pl.BlockSpec((2 * tf, tn2), lambda i, j, k: (k, j))],
        out_shape=_sds((s_len, d), F32), out_specs=pl.BlockSpec((tm, tn2), lambda i, j, k: (i, j)),
        acc_shape=(tm, tn2), epilogue=_ep_store(F32))
    return hu, act, f, w_out2d


def _ffn_bwd(tag, n, hu, act, df, w_in_parts, w_out2d, dims, after_dw_out, after_dw_in):
    s_len, d, f_dim = dims["S"], dims["D"], dims["F"]
    tf = f_dim // 4
    tk = _tile(s_len, 2048)
    tn = _tile(d, 1024)
    g_out = _matmul(
        f"{tag}_dw_out", act, [df], mode="tn", grid=(4, d // tn, s_len // tk),
        a_spec=pl.BlockSpec((tk, tf), lambda i, j, k: (k, i)),
        b_specs=[pl.BlockSpec((tk, tn), lambda i, j, k: (k, j))],
        out_shape=_sds((2, 4, tf // 2, d), F32),
        out_specs=pl.BlockSpec((2, None, tf // 2, tn), lambda i, j, k: (0, i, 0, j)),
        acc_shape=(tf, tn), epilogue=_ep_halves(tf // 2))
    after_dw_out(g_out)

    def ep_dhu(accs, ex, outs):
        da = accs[0]
        hh, uu = ex[0][0].astype(F32), ex[0][1].astype(F32)
        sig = _sigmoid(hh)
        outs[0][0] = (da * uu * (sig * (1.0 + hh * (1.0 - sig)))).astype(BF16)
        outs[0][1] = (da * (hh * sig)).astype(BF16)

    tm = _tile(s_len, 512)
    hu_spec = pl.BlockSpec((2, tm, tf), lambda i, j, k: (0, i, j))
    dhu = _matmul(
        f"{tag}_dhu", df, [w_out2d], mode="nt", grid=(s_len // tm, 4, 1),
        a_spec=pl.BlockSpec((tm, d), lambda i, j, k: (i, 0)),
        b_specs=[pl.BlockSpec((tf, d), lambda i, j, k: (j, 0))],
        extras=[hu], extra_specs=[hu_spec],
        out_shape=_sds((2, s_len, f_dim), BF16), out_specs=hu_spec, acc_shape=(tm, tf), epilogue=ep_dhu)

    hd = d // 2
    g_in = _matmul(
        f"{tag}_dw_in", n, [dhu], mode="tn", grid=(2, 8, s_len // tk),
        a_spec=pl.BlockSpec((tk, hd), lambda i, j, k: (k, i)),
        b_specs=[pl.BlockSpec((None, tk, tf), lambda i, j, k: (j // 4, k, j % 4))],
        out_shape=_sds((2, 4, hd, f_dim // 2), F32),
        out_specs=pl.BlockSpec((None, None, hd, tf), lambda i, j, k: (i, j // 2, 0, j % 2)),
        acc_shape=(hd, tf), epilogue=_ep_store(F32))
    after_dw_in(g_in)

    def ep_add(accs, ex, outs):
        outs[0][...] = accs[0] + ex[0][...]

    tm2 = _tile(s_len, 1024)
    n_parts = len(w_in_parts)
    wpart = 2 * tf // n_parts
    tile = pl.BlockSpec((tm2, tn), lambda i, j, k: (i, j))
    dn = None
    for part, w_g in enumerate(w_in_parts):
        dn = _matmul(
            f"{tag}_dn{part}", dhu, [w_g], mode="nt", grid=(s_len // tm2, d // tn, N_CHIPS),
            a_spec=pl.BlockSpec((None, tm2, wpart), lambda i, j, k, part=part: (k // 2, i, (k % 2) * n_parts + part)),
            b_specs=[pl.BlockSpec((None, tn, wpart), lambda i, j, k: (k, j, 0))],
            extras=[] if dn is None else [dn], extra_specs=[] if dn is None else [tile],
            out_shape=_sds((s_len, d), F32), out_specs=tile, acc_shape=(tm2, tn),
            epilogue=_ep_store(F32) if dn is None else ep_add)
    return dn


def kernel(x, c, w_ada, b_ada, g_ffn1, w1_in, w1_out, g_mix, w_in, conv_w, conv_b, ln_a_g, ln_a_b, w_a_out, b_a_out, w_b_group, b_b_group, ls_b, w_out, g_ffn2, w2_in, w2_out, g_final, loss_target, m_w_ada, m_b_ada, m_g_ffn1, m_w1_in, m_w1_out, m_g_mix, m_w_in, m_conv_w, m_conv_b, m_ln_a_g, m_ln_a_b, m_w_a_out, m_b_a_out, m_w_b_group, m_b_b_group, m_ls_b, m_w_out, m_g_ffn2, m_w2_in, m_w2_out, m_g_final, v_w_ada, v_b_ada, v_g_ffn1, v_w1_in, v_w1_out, v_g_mix, v_w_in, v_conv_w, v_conv_b, v_ln_a_g, v_ln_a_b, v_w_a_out, v_b_a_out, v_w_b_group, v_b_b_group, v_ls_b, v_w_out, v_g_ffn2, v_w2_in, v_w2_out, v_g_final):
    weights = dict(w_ada=w_ada, b_ada=b_ada, g_ffn1=g_ffn1, w1_in=w1_in, w1_out=w1_out, g_mix=g_mix, w_in=w_in,
                   conv_w=conv_w, conv_b=conv_b, ln_a_g=ln_a_g, ln_a_b=ln_a_b, w_a_out=w_a_out, b_a_out=b_a_out,
                   w_b_group=w_b_group, b_b_group=b_b_group, ls_b=ls_b, w_out=w_out, g_ffn2=g_ffn2, w2_in=w2_in,
                   w2_out=w2_out, g_final=g_final)
    mom1 = dict(w_ada=m_w_ada, b_ada=m_b_ada, g_ffn1=m_g_ffn1, w1_in=m_w1_in, w1_out=m_w1_out, g_mix=m_g_mix,
                w_in=m_w_in, conv_w=m_conv_w, conv_b=m_conv_b, ln_a_g=m_ln_a_g, ln_a_b=m_ln_a_b, w_a_out=m_w_a_out,
                b_a_out=m_b_a_out, w_b_group=m_w_b_group, b_b_group=m_b_b_group, ls_b=m_ls_b, w_out=m_w_out,
                g_ffn2=m_g_ffn2, w2_in=m_w2_in, w2_out=m_w2_out, g_final=m_g_final)
    mom2 = dict(w_ada=v_w_ada, b_ada=v_b_ada, g_ffn1=v_g_ffn1, w1_in=v_w1_in, w1_out=v_w1_out, g_mix=v_g_mix,
                w_in=v_w_in, conv_w=v_conv_w, conv_b=v_conv_b, ln_a_g=v_ln_a_g, ln_a_b=v_ln_a_b, w_a_out=v_w_a_out,
                b_a_out=v_b_a_out, w_b_group=v_w_b_group, b_b_group=v_b_b_group, ls_b=v_ls_b, w_out=v_w_out,
                g_ffn2=v_g_ffn2, w2_in=v_w2_in, w2_out=v_w2_out, g_final=v_g_final)
    order = list(weights)

    s_len, d = x.shape[1], x.shape[2]
    f_dim = w1_out.shape[0] * N_CHIPS
    wc = conv_w.shape[1] * N_CHIPS
    wp = w_b_group.shape[0] * w_b_group.shape[1]
    n_groups, gi, goq = w_b_group.shape
    npc = w_in.shape[1]
    ada_c = w_ada.shape[1]
    dims = dict(S=s_len, D=d, F=f_dim)
    ts = _tile(s_len, 256)

    xi, yi, ci = lax.axis_index("x"), lax.axis_index("y"), lax.axis_index("c")
    q = 2 * xi + yi
    dev = 2 * q + ci
    q_idx = jnp.reshape(q, (1,)).astype(jnp.int32)
    qc_idx = jnp.stack([q, ci]).astype(jnp.int32)
    _PREVIOUS.clear()

    cwq = conv_w.shape[1]
    pack0 = jnp.concatenate([c.reshape(-1), conv_w.reshape(-1), b_b_group.reshape(-1)])
    n0 = -(-pack0.shape[0] // (8 * LANES)) * LANES
    pack0 = jnp.pad(pack0, (0, 8 * n0 - pack0.shape[0])).reshape(8, n0)
    g0 = _allgather_small("gather_small_in", pack0).reshape(N_DEV, 8 * n0)
    c_all = g0[:, :d]
    south = g0[0::2]
    cw_full = jnp.concatenate([south[k, d:d + CONV_K * cwq].reshape(CONV_K, cwq) for k in range(N_CHIPS)], axis=1)
    cw_pad = jnp.pad(cw_full, ((0, HALO - CONV_K), (0, 0)))
    o_bb = d + CONV_K * cwq
    bb_full = jnp.concatenate([south[k, o_bb:o_bb + n_groups * goq].reshape(n_groups, goq) for k in range(N_CHIPS)],
                              axis=1).reshape(1, d)

    as2d = lambda a: a.reshape(-1, a.shape[-1])
    groups = dict(w1_in_a=["w1_in_a"], w1_in_b=["w1_in_b"], w1_out=["w1_out"], w_in=["w_in"],
                  mix=["w_a_out", "w_b_group", "w_out"], w2_in=["w2_in"], w2_out=["w2_out"])
    big = ["w1_in", "w1_out", "w_in", "w_a_out", "w_b_group", "w_out", "w2_in", "w2_out"]
    pieces = {nm: (nm, 0, 1) for nm in big[1:]}
    pieces.update(w1_in_a=("w1_in", 0, 2), w1_in_b=("w1_in", 1, 2))

    def cast(piece):
        nm, part, n_parts = pieces[piece]
        return _cast_into_gathered(f"cast_{piece}", as2d(weights[nm]), q_idx, part, n_parts)

    first = list(groups)[0]
    casts = {piece: cast(piece) for piece in groups[first]}
    ici = {first: _gather_ici(f"gather_{first}_ici", [casts[piece] for piece in groups[first]])}

    b_ada_mine = lax.dynamic_slice(b_ada, (q * ada_c,), (ada_c,)).reshape(1, ada_c)
    ada_piece = _ada_fwd("ada_fwd", c_all, w_ada, b_ada_mine)
    for grp in list(groups)[1:]:
        casts.update({piece: cast(piece) for piece in groups[grp]})
    g1 = _allgather_small("gather_ada", ada_piece).reshape(N_DEV, N_DEV, ada_c)
    for grp in list(groups)[1:]:
        ici[grp] = _gather_ici(f"gather_{grp}_ici", [casts[piece] for piece in groups[grp]])
    ada_rows = lax.dynamic_index_in_dim(g1[0::2], dev, axis=1, keepdims=False)
    ada = ada_rows.reshape(3, 3, 1, d)
    (sh1, sc1, gt1), (sh2, sc2, gt2), (sh3, sc3, gt3) = [[ada[i, j] for j in range(3)] for i in range(3)]

    row = lambda vct: vct.reshape(1, -1)
    g1v, gmv, g2v, gfv = row(g_ffn1), row(g_mix), row(g_ffn2), row(g_final)

    def arrived(grp):
        return _gather_d2d(f"gather_{grp}_d2d", ici[grp].wait())

    def gathered(fwd, grp):
        return {nm: g.reshape(N_CHIPS, 2 * g.shape[2], g.shape[3]) for nm, g in zip(groups[grp], fwd.wait())}

    x2 = x[0]
    tgt = loss_target[0]

    n1 = _norm_mod("ffn1_norm", x2, g1v, sc1, sh1, ts)
    fwd, w1_in_parts = {}, []

    def w1_in_part(grp):
        def get():
            w1_in_parts.append(gathered(arrived(grp), grp)[grp])
            return w1_in_parts[-1]
        return get

    def w1_out_after_swiglu():
        fwd["w1_out"] = arrived("w1_out")
        fwd["w_in"] = arrived("w_in")
        return gathered(fwd["w1_out"], "w1_out")["w1_out"].reshape(f_dim, d)

    hu1, act1, f1, w1_out_2d = _ffn_fwd("ffn1", n1, [w1_in_part("w1_in_a"), w1_in_part("w1_in_b")],
                                        w1_out_after_swiglu, dims)
    h1, n2 = _residual_norm_mod("mix_norm", x2, f1, gt1, 0.5, gmv, sc2, sh2, ts)
    w_in_g = gathered(fwd["w_in"], "w_in")["w_in"]

    tm = _tile(s_len, 1024)
    tnp = npc // 2
    proj = _matmul(
        "mix_proj", n2, [w_in_g], mode="nn", grid=(s_len // tm, 8, 1),
        a_spec=pl.BlockSpec((tm, d), lambda i, j, k: (i, 0)),
        b_specs=[pl.BlockSpec((None, d, tnp), lambda i, j, k: (j // 2, 0, j % 2))],
        out_shape=_sds((N_CHIPS, s_len, npc), BF16),
        out_specs=pl.BlockSpec((None, tm, tnp), lambda i, j, k: (j // 2, i, j % 2)),
        acc_shape=(tm, tnp), epilogue=_ep_store(BF16))
    fwd["mix"] = arrived("mix")
    cbv, lgv, lbv = row(conv_b), row(ln_a_g), row(ln_a_b)
    a3, mixed = _mixer_mid("mix_mid", proj, cw_pad, cbv, lgv, lbv, wc, wp, ts)
    wts = gathered(fwd["mix"], "mix")
    w_out_2d = wts["w_out"].reshape(d, d)
    w_a_g = wts["w_a_out"]
    w_b_g = wts["w_b_group"]
    dq = d // N_CHIPS
    ya = _matmul(
        "mix_ya", a3, [w_a_g], mode="nn", grid=(s_len // tm, N_CHIPS, 1),
        a_spec=pl.BlockSpec((tm, wc), lambda i, j, k: (i, 0)),
        b_specs=[pl.BlockSpec((None, wc, dq), lambda i, j, k: (j, 0, 0))],
        out_shape=_sds((s_len, d), BF16), out_specs=pl.BlockSpec((tm, dq), lambda i, j, k: (i, j)),
        acc_shape=(tm, dq), epilogue=_ep_store(BF16))
    yb = _matmul(
        "mix_yb", mixed, [w_b_g], mode="nn", grid=(s_len // tm, n_groups * N_CHIPS, 1),
        a_spec=pl.BlockSpec((tm, gi), lambda i, j, k: (i, j // N_CHIPS)),
        b_specs=[pl.BlockSpec((None, gi, goq), lambda i, j, k: (j % N_CHIPS, j // N_CHIPS, 0))],
        out_shape=_sds((s_len, d), BF16), out_specs=pl.BlockSpec((tm, goq), lambda i, j, k: (i, j)),
        acc_shape=(tm, goq), epilogue=_ep_store(BF16))
    bav, lsv = row(b_a_out), row(ls_b)
    z = _gates_fwd("mix_gates", proj, ya, yb, bav, bb_full, lsv, wc, wp, ts)
    tn = _tile(d, 1024)
    mix = _matmul(
        "mix_out", z, [w_out_2d], mode="nn", grid=(s_len // tm, d // tn, 1),
        a_spec=pl.BlockSpec((tm, d), lambda i, j, k: (i, 0)),
        b_specs=[pl.BlockSpec((d, tn), lambda i, j, k: (0, j))],
        out_shape=_sds((s_len, d), F32), out_specs=pl.BlockSpec((tm, tn), lambda i, j, k: (i, j)),
        acc_shape=(tm, tn), epilogue=_ep_store(F32))
    fwd["w2_in"] = arrived("w2_in")
    h2, n3 = _residual_norm_mod("ffn2_norm", h1, mix, gt2, 1.0, g2v, sc3, sh3, ts)
    w2_in_g = gathered(fwd["w2_in"], "w2_in")["w2_in"]
    hu2, act2, f3, w2_out_2d = _ffn_fwd(
        "ffn2", n3, [lambda: w2_in_g],
        lambda: gathered(arrived("w2_out"), "w2_out")["w2_out"].reshape(f_dim, d), dims)

    dh3, df3, d_gf, d_gt3, loss_cols = _final_loss("final_loss", h2, f3, tgt, gt3, 0.5, gfv, ts)
    rs, held = {}, {}
    dn3 = _ffn_bwd(
        "ffn2", n3, hu2, act2, df3, [w2_in_g], w2_out_2d, dims,
        after_dw_out=lambda g: held.update(w2_out=g),
        after_dw_in=lambda g: rs.update(ffn2=_ReduceScatter("g_ffn2", ["w2_out", "w2_in"], [held["w2_out"], g],
                                                            qc_idx)))
    dh2, dmix, d_sh3, d_sc3, d_g2, d_gt2 = _norm_mod_bwd("ffn2_norm_bwd", h2, dn3, dh3, g2v, sc3, ts,
                                                         prev=(mix, gt2, 1.0))
    rs["ffn2"].step2()

    tk = s_len
    hq = d // (2 * N_CHIPS)
    gw_out = _matmul(
        "mix_dw_out", z, [dmix], mode="tn", grid=(N_CHIPS, d // tn, s_len // tk),
        a_spec=pl.BlockSpec((tk, 2 * hq), lambda i, j, k: (k, i)),
        b_specs=[pl.BlockSpec((tk, tn), lambda i, j, k: (k, j))],
        out_shape=_sds((2, N_CHIPS, hq, d), F32),
        out_specs=pl.BlockSpec((2, None, hq, tn), lambda i, j, k: (0, i, 0, j)),
        acc_shape=(2 * hq, tn), epilogue=_ep_halves(hq))
    dz = _matmul(
        "mix_dz", dmix, [w_out_2d], mode="nt", grid=(s_len // tm, d // tn, 1),
        a_spec=pl.BlockSpec((tm, d), lambda i, j, k: (i, 0)),
        b_specs=[pl.BlockSpec((tn, d), lambda i, j, k: (j, 0))],
        out_shape=_sds((s_len, d), F32), out_specs=pl.BlockSpec((tm, tn), lambda i, j, k: (i, j)),
        acc_shape=(tm, tn), epilogue=_ep_store(F32))
    dya, dyb, dgates, d_ba, d_ls, d_bb = _gates_bwd("mix_gates_bwd", proj, dz, ya, yb, bav, bb_full, lsv, wc, wp, ts)
    gw_a = _matmul(
        "mix_dw_a", a3, [dya], mode="tn", grid=(1, N_CHIPS, s_len // tk),
        a_spec=pl.BlockSpec((tk, wc), lambda i, j, k: (k, 0)),
        b_specs=[pl.BlockSpec((tk, dq), lambda i, j, k: (k, j))],
        out_shape=_sds((2, N_CHIPS, wc // 2, dq), F32),
        out_specs=pl.BlockSpec((2, None, wc // 2, dq), lambda i, j, k: (0, j, 0, 0)),
        acc_shape=(wc, dq), epilogue=_ep_halves(wc // 2))
    da3 = _matmul(
        "mix_da3", dya, [w_a_g], mode="nt", grid=(s_len // tm, 1, N_CHIPS),
        a_spec=pl.BlockSpec((tm, dq), lambda i, j, k: (i, k)),
        b_specs=[pl.BlockSpec((None, wc, dq), lambda i, j, k: (k, 0, 0))],
        out_shape=_sds((s_len, wc), F32), out_specs=pl.BlockSpec((tm, wc), lambda i, j, k: (i, 0)),
        acc_shape=(tm, wc), epilogue=_ep_store(F32))
    gpr = n_groups // 2
    gw_b = _matmul(
        "mix_dw_b", mixed, [dyb], mode="tn", grid=(1, n_groups * N_CHIPS, s_len // tk),
        a_spec=pl.BlockSpec((tk, gi), lambda i, j, k: (k, j // N_CHIPS)),
        b_specs=[pl.BlockSpec((tk, goq), lambda i, j, k: (k, j))],
        out_shape=_sds((2, N_CHIPS, gpr * gi, goq), F32),
        out_specs=pl.BlockSpec((None, None, gi, goq),
                               lambda i, j, k: ((j // N_CHIPS) // gpr, j % N_CHIPS, (j // N_CHIPS) % gpr, 0)),
        acc_shape=(gi, goq), epilogue=_ep_store(F32))
    dmixed = _matmul(
        "mix_dmixed", dyb, [w_b_g], mode="nt", grid=(s_len // tm, n_groups, N_CHIPS),
        a_spec=pl.BlockSpec((tm, goq), lambda i, j, k: (i, j * N_CHIPS + k)),
        b_specs=[pl.BlockSpec((None, gi, goq), lambda i, j, k: (k, j, 0))],
        out_shape=_sds((s_len, wp), F32), out_specs=pl.BlockSpec((tm, gi), lambda i, j, k: (i, j)),
        acc_shape=(tm, gi), epilogue=_ep_store(F32))
    da1, d_lg, d_lb, d_cb, d_cw = _conv_branch_bwd("mix_conv_bwd", proj, da3, cw_pad, cbv, lgv, lbv, wc, wp, ts)
    dproj = _mixer_in_bwd("mix_in_bwd", proj, da1, dmixed, dgates, cw_pad, wc, wp, ts)
    hd = d // 2
    gw_in = _matmul(
        "mix_dw_in", n2, [dproj], mode="tn", grid=(2, 8, s_len // tk),
        a_spec=pl.BlockSpec((tk, hd), lambda i, j, k: (k, i)),
        b_specs=[pl.BlockSpec((None, tk, tnp), lambda i, j, k: (j // 2, k, j % 2))],
        out_shape=_sds((2, N_CHIPS, hd, npc), F32),
        out_specs=pl.BlockSpec((None, None, hd, tnp), lambda i, j, k: (i, j // 2, 0, j % 2)),
        acc_shape=(hd, tnp), epilogue=_ep_store(F32))
    rs["mix"] = _ReduceScatter("g_mix", ["w_in", "w_a_out", "w_b_group", "w_out"], [gw_in, gw_a, gw_b, gw_out],
                               qc_idx)
    rs["ffn2"].step3()
    dn2 = _matmul(
        "mix_dn", dproj, [w_in_g], mode="nt", grid=(s_len // tm, d // tn, N_CHIPS),
        a_spec=pl.BlockSpec((None, tm, npc), lambda i, j, k: (k, i, 0)),
        b_specs=[pl.BlockSpec((None, tn, npc), lambda i, j, k: (k, j, 0))],
        out_shape=_sds((s_len, d), F32), out_specs=pl.BlockSpec((tm, tn), lambda i, j, k: (i, j)),
        acc_shape=(tm, tn), epilogue=_ep_store(F32))
    dh1, df1, d_sh2, d_sc2, d_gm, d_gt1 = _norm_mod_bwd("mix_norm_bwd", h1, dn2, dh2, gmv, sc2, ts,
                                                        prev=(f1, gt1, 0.5))
    rs["mix"].step2()

    def w1_in_ready(g):
        rs["w1_in"] = _ReduceScatter("g_w1_in", ["w1_in"], [g], qc_idx)
        rs["w1_out"].step2()
        rs["mix"].step3()

    dn1 = _ffn_bwd(
        "ffn1", n1, hu1, act1, df1, w1_in_parts, w1_out_2d, dims,
        after_dw_out=lambda g: rs.update(w1_out=_ReduceScatter("g_w1_out", ["w1_out"], [g], qc_idx)),
        after_dw_in=w1_in_ready)
    grad_x, d_sh1, d_sc1, d_g1 = _norm_mod_bwd("ffn1_norm_bwd", x2, dn1, dh1, g1v, sc1, ts)

    d_ada = jnp.concatenate([d_sh1, d_sc1, d_gt1, d_sh2, d_sc2, d_gt2, d_sh3, d_sc3, d_gt3], axis=1)
    small = [d_ada, d_g1, d_gm, d_cw[:CONV_K].reshape(1, -1), d_cb, d_lg, d_lb, d_ba, d_bb, d_ls, d_g2, d_gf,
             loss_cols]
    sizes = [a.shape[1] for a in small]
    pack1 = jnp.concatenate(small, axis=1).reshape(-1)
    n1p = -(-pack1.shape[0] // (8 * LANES)) * LANES
    pack1 = jnp.pad(pack1, (0, 8 * n1p - pack1.shape[0])).reshape(8, n1p)
    g2 = _allgather_small("gather_small_grads", pack1)
    rs["w1_in"].step2()
    total = _sum_devices("sum_small_grads", g2, 8).reshape(-1)
    offs = [0]
    for sz in sizes:
        offs.append(offs[-1] + sz)
    tot = [total[offs[k]:offs[k + 1]] for k in range(len(sizes))]
    d_ada_all = g2.reshape(N_DEV, 8 * n1p)[:, :sizes[0]]
    loss = jnp.sum(tot[12])

    grads = {}
    grads["b_ada"] = tot[0]
    grads["g_ffn1"], grads["g_mix"] = tot[1], tot[2]
    grads["conv_w"] = lax.dynamic_slice(tot[3].reshape(CONV_K, wc), (0, q * cwq), (CONV_K, cwq))
    grads["conv_b"], grads["ln_a_g"], grads["ln_a_b"], grads["b_a_out"] = tot[4], tot[5], tot[6], tot[7]
    grads["b_b_group"] = lax.dynamic_slice(tot[8].reshape(n_groups, N_CHIPS * goq), (0, q * goq), (n_groups, goq))
    grads["ls_b"], grads["g_ffn2"], grads["g_final"] = tot[9], tot[10], tot[11]

    delta, new_m, new_v = {}, {}, {}

    def adamw_group(reduced):
        for nm, g in reduced.items():
            shp = weights[nm].shape
            go, dl, mo, vo = _adamw(f"adamw_{nm}", as2d(weights[nm]), g, as2d(mom1[nm]), as2d(mom2[nm]))
            grads[nm], delta[nm], new_m[nm], new_v[nm] = go.reshape(shp), dl.reshape(shp), mo.reshape(shp), vo.reshape(shp)

    adamw_group(rs["ffn2"].result())
    rs["w1_out"].step3()
    adamw_group(rs["mix"].result())
    d_ada_mine = lax.dynamic_slice(d_ada_all, (0, q * ada_c), (N_DEV, ada_c))
    grads["w_ada"], delta["w_ada"], new_m["w_ada"], new_v["w_ada"] = _ada_grad_adamw(
        "adamw_w_ada", c_all.T, d_ada_mine, w_ada, m_w_ada, v_w_ada)
    rs["w1_in"].step3()
    smalls = [nm for nm in order if nm not in big and nm != "w_ada"]
    flat = lambda src: jnp.concatenate([src[nm].reshape(-1) for nm in smalls])
    n_small = sum(weights[nm].size for nm in smalls)
    rows_s = -(-n_small // (8 * LANES)) * 8
    packed = [jnp.pad(flat(src), (0, rows_s * LANES - n_small)).reshape(rows_s, LANES)
              for src in (weights, grads, mom1, mom2)]
    _, dl_s, mo_s, vo_s = _adamw("adamw_small", *packed)
    off = 0
    for nm in smalls:
        sz, shp = weights[nm].size, weights[nm].shape
        delta[nm] = dl_s.reshape(-1)[off:off + sz].reshape(shp)
        new_m[nm] = mo_s.reshape(-1)[off:off + sz].reshape(shp)
        new_v[nm] = vo_s.reshape(-1)[off:off + sz].reshape(shp)
        grads[nm] = grads[nm].reshape(shp)
        off += sz
    adamw_group(rs["w1_out"].result())
    adamw_group(rs["w1_in"].result())

    return (loss, grad_x[None], *[grads[nm] for nm in order], *[delta[nm] for nm in order],
            *[new_m[nm] for nm in order], *[new_v[nm] for nm in order])
```

```python
import jax
import jax.numpy as jnp
from jax import lax
from jax.experimental import pallas as pl
from jax.experimental.pallas import tpu as pltpu

F32 = jnp.float32
BF16 = jnp.bfloat16
MESH = pl.DeviceIdType.MESH
ANY = pl.BlockSpec(memory_space=pl.ANY)
HBM = pl.BlockSpec(memory_space=pltpu.HBM)
SEM = pl.BlockSpec(memory_space=pltpu.SEMAPHORE)
EFFECT = pltpu.SideEffectType.DATAFLOW_SIDE_EFFECTING

EPS = 1e-6
CONV_K = 31
HALO = 32
POOL_WINDOWS = (2, 4, 8, 16)
N_CHIPS = 4
N_DEV = 8
LANES = 128

ADAM_LR = 0.001
ADAM_B1 = 0.9
ADAM_B2 = 0.999
ADAM_EPS = 1e-08
ADAM_WD = 0.01
ADAM_STEP = 10

DN = {
    "nn": (((1,), (0,)), ((), ())),
    "nt": (((1,), (1,)), ((), ())),
    "tn": (((0,), (0,)), ((), ())),
}


_PREVIOUS = []


def _ordered(call, args, n_lead, body, token=None, sources=()):
    dep = [pltpu.with_memory_space_constraint(p, pltpu.HBM) if p.size * p.dtype.itemsize >= (1 << 20) else p
           for p in _PREVIOUS if all(p is not a for a in (*args, *sources))]

    def wrapped(*refs):
        return body(*refs[:n_lead], *refs[n_lead + len(dep):])

    outs = call(wrapped, [ANY] * len(dep))(*args, *dep)
    seq = outs if isinstance(outs, (list, tuple)) else [outs]
    _PREVIOUS[:] = [seq[token] if token is not None else
                    next(o for o in seq if jnp.issubdtype(o.dtype, jnp.floating))]
    return outs


def _pcall(body, *, name, out_shape, grid=None, in_specs=None, out_specs=None, scratch=(), aliases=None,
           prefetch=0, vmem_mb=None):
    params = {}
    if grid is not None:
        params["dimension_semantics"] = ("arbitrary",) * len(grid)
    if vmem_mb is not None:
        params["vmem_limit_bytes"] = vmem_mb << 20
    def in_hbm(shape, spec):
        big = shape.size * jnp.dtype(shape.dtype).itemsize >= (1 << 20)
        return pltpu.HBM(shape.shape, shape.dtype) if big and getattr(spec, "memory_space", None) != pltpu.VMEM else shape

    if isinstance(out_shape, (list, tuple)):
        out_shape = [in_hbm(s, sp) for s, sp in zip(out_shape, out_specs)]
    else:
        out_shape = in_hbm(out_shape, out_specs)
    kw = dict(name=name, out_shape=out_shape, compiler_params=pltpu.CompilerParams(**params))
    if aliases:
        kw["input_output_aliases"] = aliases

    def call(wrapped, dep_specs):
        specs = list(in_specs) + dep_specs
        if prefetch:
            return pl.pallas_call(wrapped, grid_spec=pltpu.PrefetchScalarGridSpec(
                num_scalar_prefetch=prefetch, grid=grid, in_specs=specs, out_specs=out_specs,
                scratch_shapes=list(scratch)), **kw)
        if grid is not None:
            return pl.pallas_call(wrapped, grid=grid, in_specs=specs, out_specs=out_specs,
                                  scratch_shapes=list(scratch), **kw)
        return pl.pallas_call(wrapped, in_specs=specs, out_specs=out_specs, scratch_shapes=list(scratch), **kw)

    def run(*args):
        specs = [None] * prefetch + list(in_specs)
        placed = [pltpu.with_memory_space_constraint(a, pltpu.HBM)
                  if a.size * a.dtype.itemsize >= (1 << 20) and getattr(s, "memory_space", None) != pltpu.VMEM else a
                  for a, s in zip(args, specs)]
        return _ordered(call, placed, prefetch + len(in_specs), body, sources=args)

    return run


def _tile(dim, pref):
    t = min(dim, pref)
    assert dim % t == 0, (dim, pref)
    return t


def _sds(shape, dtype):
    return jax.ShapeDtypeStruct(tuple(shape), dtype)


def _sigmoid(v):
    return 0.5 * jnp.tanh(0.5 * v) + 0.5


def _vec(w):
    return pl.BlockSpec((1, w), lambda *_: (0, 0))


def _acc_rows(ref, val, i):
    @pl.when(i == 0)
    def _():
        ref[...] = jnp.zeros_like(ref)

    ref[...] += jnp.sum(val, axis=0, keepdims=True)


def _matmul(name, a, bs, *, mode, grid, a_spec, b_specs, out_shape, out_specs, acc_shape, epilogue,
            extras=(), extra_specs=(), vmem_mb=56, carry=()):
    nb, ne, nk, nc = len(bs), len(extras), grid[2], len(carry)
    dn = DN[mode]

    def body(*all_refs):
        refs = all_refs[:1 + nb + ne] + all_refs[1 + nb + ne + nc:]
        a_ref, b_refs, ex = refs[0], refs[1:1 + nb], refs[1 + nb:1 + nb + ne]
        if nk == 1:
            outs = refs[1 + nb + ne:]
            accs = [lax.dot_general(a_ref[...], b[...], dn, preferred_element_type=F32) for b in b_refs]
            epilogue(accs, ex, outs)
            return
        outs, acc_refs = refs[1 + nb + ne:-nb], refs[-nb:]
        k = pl.program_id(2)

        @pl.when(k == 0)
        def _():
            for acc in acc_refs:
                acc[...] = jnp.zeros_like(acc)

        for acc, b in zip(acc_refs, b_refs):
            acc[...] += lax.dot_general(a_ref[...], b[...], dn, preferred_element_type=F32)

        @pl.when(k == nk - 1)
        def _():
            epilogue([acc[...] for acc in acc_refs], ex, outs)

    scratch = [pltpu.VMEM(acc_shape, F32) for _ in range(nb)] if nk > 1 else []
    return _pcall(body, name=name, out_shape=out_shape, grid=grid,
                  in_specs=[a_spec, *b_specs, *extra_specs, *[ANY] * nc], out_specs=out_specs, scratch=scratch,
                  aliases={1 + nb + ne + i: i for i in range(nc)}, vmem_mb=vmem_mb)(a, *bs, *extras, *carry)


def _ep_store(dtype):
    def ep(accs, ex, outs):
        outs[0][...] = accs[0].astype(dtype)
    return ep


def _ep_halves(h):
    def ep(accs, ex, outs):
        outs[0][0] = accs[0][:h]
        outs[0][1] = accs[0][h:]
    return ep


def _place():
    x, y, c = lax.axis_index("x"), lax.axis_index("y"), lax.axis_index("c")
    chips = [(1 - x, y), (x, 1 - y), (1 - x, 1 - y)]
    return x, y, c, chips


def _allgather_small(name, block):
    m_per, n = block.shape

    def body(x_ref, out_ref, send_sems, recv_sems, local_sem):
        x, y, c, chips = _place()
        me, sibling = (x, y, c), (x, y, 1 - c)

        def rows(px, py, pc):
            return out_ref.at[pl.ds((4 * px + 2 * py + pc) * m_per, m_per), :]

        def copy(k, blk, to, src=None):
            return pltpu.make_async_remote_copy(
                src_ref=rows(*blk) if src is None else src, dst_ref=rows(*blk),
                send_sem=send_sems.at[k], recv_sem=recv_sems.at[k], device_id=to, device_id_type=MESH)

        mine = pltpu.make_async_copy(x_ref, rows(*me), local_sem)
        mine.start()
        first = [copy(0, me, sibling, src=x_ref)]
        first += [copy(1 + j, me, (*chip, c), src=x_ref) for j, chip in enumerate(chips)]
        for cp in first:
            cp.start()
        passed = [copy(4 + j, (*chip, c), sibling) for j, chip in enumerate(chips)]
        for j, chip in enumerate(chips):
            copy(1 + j, (*chip, c), me).wait_recv()
            passed[j].start()
        copy(0, sibling, me).wait_recv()
        for j, chip in enumerate(chips):
            copy(4 + j, (*chip, 1 - c), me).wait_recv()
        for cp in first + passed:
            cp.wait_send()
        mine.wait()

    return _pcall(
        body, name=name, out_shape=_sds((N_DEV * m_per, n), block.dtype),
        in_specs=[pl.BlockSpec(memory_space=pltpu.VMEM)], out_specs=pl.BlockSpec(memory_space=pltpu.VMEM),
        scratch=[pltpu.SemaphoreType.DMA((7,)), pltpu.SemaphoreType.DMA((7,)), pltpu.SemaphoreType.DMA],
    )(block)


class _SplitCopies:
    def __init__(self, name, arrays, plan, n_copies):
        self.name, self.plan, self.n = name, plan, len(arrays)
        n = self.n

        def body(*refs):
            send, recv, token = refs[n], refs[n + 1], refs[-1]
            for k, (src, dst, _, peer) in enumerate(plan(refs[:n])):
                pltpu.make_async_remote_copy(src_ref=src, dst_ref=dst, send_sem=send.at[k], recv_sem=recv.at[k],
                                             device_id=peer, device_id_type=MESH).start()
            token[...] = jnp.zeros_like(token)

        def call(wrapped, dep_specs):
            return pl.pallas_call(
                wrapped, name=f"{name}_start",
                out_shape=(pltpu.SemaphoreType.DMA((n_copies,)), pltpu.SemaphoreType.DMA((n_copies,)),
                           *[pltpu.HBM(a.shape, a.dtype) for a in arrays], _sds((8, LANES), F32)),
                in_specs=[HBM] * n + dep_specs,
                out_specs=(SEM, SEM, *[HBM] * n, pl.BlockSpec(memory_space=pltpu.VMEM)),
                input_output_aliases={i: 2 + i for i in range(n)},
                compiler_params=pltpu.CompilerParams(has_side_effects=EFFECT))

        outs = _ordered(call, [pltpu.with_memory_space_constraint(a, pltpu.HBM) for a in arrays], n, body, token=-1,
                        sources=arrays)
        self.send, self.recv, self.arrays = outs[0], outs[1], list(outs[2:2 + n])

    def wait(self, arrays=None):
        n, plan = self.n, self.plan
        if arrays is not None:
            self.arrays = list(arrays)

        def body(*refs):
            send, recv, token = refs[n], refs[n + 1], refs[-1]
            for k, (src, _, landing, peer) in enumerate(plan(refs[:n])):
                cp = pltpu.make_async_remote_copy(src_ref=src, dst_ref=landing, send_sem=send.at[k],
                                                  recv_sem=recv.at[k], device_id=peer, device_id_type=MESH)
                cp.wait_send()
                cp.wait_recv()
            token[...] = jnp.zeros_like(token)

        def call(wrapped, dep_specs):
            return pl.pallas_call(
                wrapped, name=f"{self.name}_wait",
                out_shape=(*[pltpu.HBM(a.shape, a.dtype) for a in self.arrays], _sds((8, LANES), F32)),
                in_specs=[HBM] * n + [SEM, SEM] + dep_specs,
                out_specs=(*[HBM] * n, pl.BlockSpec(memory_space=pltpu.VMEM)),
                input_output_aliases={i: i for i in range(n)},
                compiler_params=pltpu.CompilerParams(has_side_effects=EFFECT))

        return list(_ordered(call, [*self.arrays, self.send, self.recv], n + 2, body, token=-1))[:n]


def _col_range(g, part, n_parts):
    width = g.shape[-1] // n_parts
    return (slice(None), pl.ds(part * width, width))


def _gather_ici(name, gathered, part=0, n_parts=1):
    def plan(refs):
        x, y, c, chips = _place()
        q = 2 * x + y
        return [(g.at[(q, c, *_col_range(g, part, n_parts))], g.at[(q, c, *_col_range(g, part, n_parts))],
                 g.at[(2 * px + py, c, *_col_range(g, part, n_parts))], (px, py, c))
                for g in refs for px, py in chips]

    return _SplitCopies(name, gathered, plan, 3 * len(gathered))


def _gather_d2d(name, gathered, part=0, n_parts=1):
    def plan(refs):
        x, y, c, chips = _place()
        return [(g.at[(2 * px + py, c, *_col_range(g, part, n_parts))],
                 g.at[(2 * px + py, c, *_col_range(g, part, n_parts))],
                 g.at[(2 * px + py, 1 - c, *_col_range(g, part, n_parts))], (x, y, 1 - c))
                for g in refs for px, py in chips]

    return _SplitCopies(name, gathered, plan, 3 * len(gathered))


def _scatter_sibling(name, grads):
    n = len(grads)

    def plan(refs):
        x, y, c, _ = _place()
        return [(refs[w].at[1 - c], refs[n + w], refs[n + w], (x, y, 1 - c)) for w in range(n)]

    landing = [lax.empty(g.shape[1:], g.dtype) for g in grads]
    return _SplitCopies(name, [*grads, *landing], plan, n)


def _scatter_chips(name, sums):
    n = len(sums)

    def plan(refs):
        x, y, c, chips = _place()
        return [(refs[w].at[2 * px + py], refs[n + w].at[j], refs[n + w].at[j], (px, py, c))
                for w in range(n) for j, (px, py) in enumerate(chips)]

    landing = [lax.empty((3, *s.shape[1:]), s.dtype) for s in sums]
    return _SplitCopies(name, [*sums, *landing], plan, 3 * n)


def _share_final(name, finals):
    def plan(refs):
        x, y, c, _ = _place()
        return [(f.at[c], f.at[c], f.at[1 - c], (x, y, 1 - c)) for f in refs]

    return _SplitCopies(name, finals, plan, len(finals))


def _row_tile(rows, cols, budget_elems=393216):
    best = 8
    for t in range(8, rows + 1, 8):
        if rows % t == 0 and t * cols <= budget_elems:
            best = t
    return best if rows % best == 0 else rows


def _sum_with_sibling(name, grad, recv, qc_idx):
    _, _, h, cols = grad.shape
    tr = _row_tile(h, cols)

    def body(s_ref, g_ref, r_ref, own_ref, pb_ref):
        p = g_ref[...] + r_ref[...]
        pb_ref[...] = p.astype(BF16)

        @pl.when(pl.program_id(1) == s_ref[0])
        def _():
            own_ref[...] = p

    blk = pl.BlockSpec((None, tr, cols), lambda r, k, s: (k, r, 0))
    return _pcall(
        body, name=name, out_shape=[_sds((h, cols), F32), _sds((N_CHIPS, h, cols), BF16)],
        grid=(h // tr, N_CHIPS), prefetch=1,
        in_specs=[pl.BlockSpec((None, None, tr, cols), lambda r, k, s: (s[1], k, r, 0)), blk],
        out_specs=[pl.BlockSpec((tr, cols), lambda r, k, s: (r, 0)), blk], vmem_mb=32,
    )(qc_idx, grad, recv)


def _sum_chips(name, own, recv, qc_idx):
    h, cols = own.shape
    tr = _row_tile(h, cols)

    def body(s_ref, p_ref, t_ref, o_ref):
        o_ref[...] = ((p_ref[...] + t_ref[0].astype(F32)) + t_ref[1].astype(F32)) + t_ref[2].astype(F32)

    return _pcall(
        body, name=name, out_shape=_sds((2, h, cols), F32), grid=(h // tr,), prefetch=1,
        in_specs=[pl.BlockSpec((tr, cols), lambda r, s: (r, 0)),
                  pl.BlockSpec((3, tr, cols), lambda r, s: (0, r, 0))],
        out_specs=pl.BlockSpec((None, tr, cols), lambda r, s: (s[1], r, 0)), vmem_mb=32,
    )(qc_idx, own, recv)


class _ReduceScatter:
    def __init__(self, tag, names, grads, qc_idx):
        self.tag, self.names, self.n, self.qc_idx = tag, names, len(grads), qc_idx
        self.copies = _scatter_sibling(f"{tag}_rs_sibling", grads)

    def step2(self):
        n = self.n
        arrs = self.copies.wait()
        sums = [_sum_with_sibling(f"{nm}_sum_sibling", arrs[w], arrs[n + w], self.qc_idx)
                for w, nm in enumerate(self.names)]
        self.own = [s[0] for s in sums]
        self.copies = _scatter_chips(f"{self.tag}_rs_chips", [s[1] for s in sums])

    def step3(self):
        n = self.n
        arrs = self.copies.wait()
        finals = [_sum_chips(f"{nm}_sum_chips", self.own[w], arrs[n + w], self.qc_idx)
                  for w, nm in enumerate(self.names)]
        self.copies = _share_final(f"{self.tag}_rs_final", finals)

    def result(self):
        return {nm: f.reshape(2 * f.shape[1], f.shape[2]) for nm, f in zip(self.names, self.copies.wait())}


def _cast_into_gathered(name, w, q_idx):
    rows, cols = w.shape
    h = rows // 2
    tr = _row_tile(h, cols, 1 << 20)
    nr = h // tr

    def body(s_ref, w_ref, o_ref):
        o_ref[...] = w_ref[...].astype(BF16)

    return _pcall(body, name=name, out_shape=_sds((N_CHIPS, 2, h, cols), BF16), grid=(2, nr), prefetch=1,
                  in_specs=[pl.BlockSpec((tr, cols), lambda hf, r, s: (hf * nr + r, 0))],
                  out_specs=pl.BlockSpec((None, None, tr, cols), lambda hf, r, s: (s[0], hf, r, 0)),
                  vmem_mb=32)(q_idx, w)


def _rms(h):
    r = lax.rsqrt(jnp.mean(h * h, axis=-1, keepdims=True) + EPS)
    return r, h * r


def _norm_mod(name, h, g, sc, sh, ts):
    s_len, d = h.shape

    def body(h_ref, g_ref, sc_ref, sh_ref, n_ref):
        _, xhat = _rms(h_ref[...])
        n_ref[...] = ((xhat * g_ref[...]) * (1.0 + sc_ref[...]) + sh_ref[...]).astype(BF16)

    row = pl.BlockSpec((ts, d), lambda i: (i, 0))
    return _pcall(body, name=name, out_shape=_sds((s_len, d), BF16), grid=(s_len // ts,),
                  in_specs=[row, _vec(d), _vec(d), _vec(d)], out_specs=row, vmem_mb=32)(h, g, sc, sh)


def _residual_norm_mod(name, h, f, gate, cmul, g, sc, sh, ts):
    s_len, d = h.shape

    def body(h_ref, f_ref, gt_ref, g_ref, sc_ref, sh_ref, ho_ref, n_ref):
        hn = h_ref[...] + (cmul * gt_ref[...]) * f_ref[...]
        ho_ref[...] = hn
        _, xhat = _rms(hn)
        n_ref[...] = ((xhat * g_ref[...]) * (1.0 + sc_ref[...]) + sh_ref[...]).astype(BF16)

    row = pl.BlockSpec((ts, d), lambda i: (i, 0))
    return _pcall(body, name=name, out_shape=[_sds((s_len, d), F32), _sds((s_len, d), BF16)],
                  grid=(s_len // ts,), in_specs=[row, row, _vec(d), _vec(d), _vec(d), _vec(d)],
                  out_specs=[row, row], vmem_mb=32)(h, f, gate, g, sc, sh)


def _final_loss(name, h, f, tgt, gate, cmul, g, ts):
    s_len, d = h.shape

    def body(h_ref, f_ref, t_ref, gt_ref, g_ref, dh_ref, df_ref, dg_ref, dgt_ref, loss_ref):
        i = pl.program_id(0)
        fv = f_ref[...]
        coef = cmul * gt_ref[...]
        hn = h_ref[...] + coef * fv
        r, xhat = _rms(hn)
        err = xhat * g_ref[...] - t_ref[...]
        _acc_rows(loss_ref, (0.5 / d) * (err * err), i)
        dy = err * (1.0 / d)
        _acc_rows(dg_ref, dy * xhat, i)
        dxhat = dy * g_ref[...]
        dh = r * (dxhat - xhat * jnp.mean(dxhat * xhat, axis=-1, keepdims=True))
        dh_ref[...] = dh
        _acc_rows(dgt_ref, cmul * (dh * fv), i)
        df_ref[...] = (coef * dh).astype(BF16)

    row = pl.BlockSpec((ts, d), lambda i: (i, 0))
    return _pcall(body, name=name,
                  out_shape=[_sds((s_len, d), F32), _sds((s_len, d), BF16)] + [_sds((1, d), F32)] * 3,
                  grid=(s_len // ts,), in_specs=[row, row, row, _vec(d), _vec(d)],
                  out_specs=[row, row, _vec(d), _vec(d), _vec(d)], vmem_mb=40)(h, f, tgt, gate, g)


def _norm_mod_bwd(name, h, dn, dh_next, g, sc, ts, prev=None):
    s_len, d = h.shape
    has_prev = prev is not None
    cmul = prev[2] if has_prev else None

    def body(*refs):
        if has_prev:
            h_ref, dn_ref, dhn_ref, f_ref, g_ref, sc_ref, gt_ref, dh_ref, df_ref, dsh_ref, dsc_ref, dg_ref, dgt_ref = refs
        else:
            h_ref, dn_ref, dhn_ref, g_ref, sc_ref, dh_ref, dsh_ref, dsc_ref, dg_ref = refs
        i = pl.program_id(0)
        r, xhat = _rms(h_ref[...])
        dn_v = dn_ref[...]
        gv = g_ref[...]
        _acc_rows(dsh_ref, dn_v, i)
        _acc_rows(dsc_ref, dn_v * (xhat * gv), i)
        dnrm = dn_v * (1.0 + sc_ref[...])
        _acc_rows(dg_ref, dnrm * xhat, i)
        dxhat = dnrm * gv
        dh = dhn_ref[...] + r * (dxhat - xhat * jnp.mean(dxhat * xhat, axis=-1, keepdims=True))
        dh_ref[...] = dh
        if has_prev:
            _acc_rows(dgt_ref, cmul * (dh * f_ref[...]), i)
            df_ref[...] = ((cmul * gt_ref[...]) * dh).astype(BF16)

    row = pl.BlockSpec((ts, d), lambda i: (i, 0))
    if has_prev:
        ins, in_specs = [h, dn, dh_next, prev[0], g, sc, prev[1]], [row, row, row, row, _vec(d), _vec(d), _vec(d)]
        out_shape = [_sds((s_len, d), F32), _sds((s_len, d), BF16)] + [_sds((1, d), F32)] * 4
        out_specs = [row, row] + [_vec(d)] * 4
    else:
        ins, in_specs = [h, dn, dh_next, g, sc], [row, row, row, _vec(d), _vec(d)]
        out_shape = [_sds((s_len, d), F32)] + [_sds((1, d), F32)] * 3
        out_specs = [row] + [_vec(d)] * 3
    return _pcall(body, name=name, out_shape=out_shape, grid=(s_len // ts,), in_specs=in_specs,
                  out_specs=out_specs, vmem_mb=40)(*ins)


def _cols(ref, lo, hi, npc, rows=slice(None)):
    parts = []
    while lo < hi:
        q, o = divmod(lo, npc)
        n = min(hi - lo, npc - o)
        parts.append(ref[q, rows, o:o + n].astype(F32))
        lo += n
    return parts[0] if len(parts) == 1 else jnp.concatenate(parts, axis=-1)


def _store_cols(ref, lo, val, npc, rows=slice(None)):
    off, width = 0, val.shape[-1]
    while off < width:
        q, o = divmod(lo + off, npc)
        n = min(width - off, npc - o)
        ref[q, rows, o:o + n] = val[:, off:off + n]
        off += n


def _chips_covering(cols, npc):
    return -(-cols // npc)


SUBLANES = 8
ROW_CHUNK = 32


def _make_phases(src_ref, ph_ref):
    rows = src_ref.shape[0] - SUBLANES
    for b in range(1, SUBLANES):
        ph_ref[b - 1] = src_ref[pl.ds(b, rows), :]


def _window(src_ref, ph_ref, off, r0, cols=slice(None)):
    a, b = divmod(off, SUBLANES)
    start = pl.multiple_of(r0 + SUBLANES * a, SUBLANES)
    if b == 0:
        return src_ref[pl.ds(start, ROW_CHUNK), cols]
    return ph_ref[b - 1, pl.ds(start, ROW_CHUNK), cols]


def _phase_scratch(rows, width):
    return pltpu.VMEM((SUBLANES - 1, rows - SUBLANES, width), F32)


def _conv_ln(a0s_ref, a0p_ref, cw_ref, cb_ref, lg_ref, lb_ref, r0):
    a1 = cb_ref[...] + cw_ref[0:1, :] * _window(a0s_ref, a0p_ref, HALO - CONV_K + 1, r0)
    for k in range(1, CONV_K):
        a1 = a1 + cw_ref[k:k + 1, :] * _window(a0s_ref, a0p_ref, HALO - CONV_K + 1 + k, r0)
    mu = jnp.mean(a1, axis=-1, keepdims=True)
    ctr = a1 - mu
    rstd = lax.rsqrt(jnp.mean(ctr * ctr, axis=-1, keepdims=True) + EPS)
    xh = ctr * rstd
    return xh, rstd, xh * lg_ref[...] + lb_ref[...]


def _for_chunks(ts, fn):
    def step(ci, carry):
        fn(pl.multiple_of(ci * ROW_CHUNK, ROW_CHUNK))
        return carry

    lax.fori_loop(0, ts // ROW_CHUNK, step, 0)


def _stage_glu(p_ref, ph_ref, a0s_ref, i, wc, npc, ts):
    a0 = _cols(p_ref, 0, wc, npc) * _sigmoid(_cols(p_ref, wc, 2 * wc, npc))
    a0h = _cols(ph_ref, 0, wc, npc) * _sigmoid(_cols(ph_ref, wc, 2 * wc, npc))
    a0s_ref[0:HALO, :] = jnp.where(i > 0, a0h, 0.0)
    a0s_ref[HALO:HALO + ts, :] = a0


def _mixer_mid(name, proj, cw, cb, lg, lb, wc, wp, ts):
    _, s_len, npc = proj.shape
    nq = _chips_covering(2 * wc + wp, npc)
    gi = wp // len(POOL_WINDOWS)
    hb = ts // HALO

    def body(p_ref, ph_ref, cw_ref, cb_ref, lg_ref, lb_ref, a3_ref, mx_ref, a0s_ref, vs_ref, a0p_ref, vp_ref):
        i = pl.program_id(0)
        _stage_glu(p_ref, ph_ref, a0s_ref, i, wc, npc, ts)
        vs_ref[0:HALO, :] = jnp.where(i > 0, _cols(ph_ref, 2 * wc, 2 * wc + wp, npc), 0.0)
        vs_ref[HALO:HALO + ts, :] = _cols(p_ref, 2 * wc, 2 * wc + wp, npc)
        _make_phases(a0s_ref, a0p_ref)
        _make_phases(vs_ref, vp_ref)

        def chunk(r0):
            rows = pl.ds(r0, ROW_CHUNK)
            _, _, a2 = _conv_ln(a0s_ref, a0p_ref, cw_ref, cb_ref, lg_ref, lb_ref, r0)
            a3_ref[rows, :] = (a2 * _sigmoid(a2)).astype(BF16)
            t_abs = i * ts + r0 + lax.broadcasted_iota(jnp.int32, (ROW_CHUNK, 1), 0)
            for g, win in enumerate(POOL_WINDOWS):
                cs = slice(g * gi, (g + 1) * gi)
                v_now = _window(vs_ref, vp_ref, HALO, r0, cs)
                acc = v_now
                for dlt in range(1, win):
                    acc = acc + _window(vs_ref, vp_ref, HALO - dlt, r0, cs)
                cnt = jnp.minimum(t_abs + 1, win).astype(F32)
                mx_ref[rows, cs] = (acc / cnt - v_now).astype(BF16)

        _for_chunks(ts, chunk)

    return _pcall(
        body, name=name, out_shape=[_sds((s_len, wc), BF16), _sds((s_len, wp), BF16)], grid=(s_len // ts,),
        in_specs=[pl.BlockSpec((nq, ts, npc), lambda i: (0, i, 0)),
                  pl.BlockSpec((nq, HALO, npc), lambda i: (0, jnp.maximum(i * hb - 1, 0), 0)),
                  pl.BlockSpec((HALO, wc), lambda i: (0, 0)), _vec(wc), _vec(wc), _vec(wc)],
        out_specs=[pl.BlockSpec((ts, wc), lambda i: (i, 0)), pl.BlockSpec((ts, wp), lambda i: (i, 0))],
        scratch=[pltpu.VMEM((HALO + ts, wc), F32), pltpu.VMEM((HALO + ts, wp), F32),
                 _phase_scratch(HALO + ts, wc), _phase_scratch(HALO + ts, wp)], vmem_mb=56,
    )(proj, proj, cw, cb, lg, lb)


def _gates_fwd(name, proj, ya, yb, b_a, b_b, ls, wc, wp, ts):
    _, s_len, npc = proj.shape
    d = ya.shape[1]
    g0 = 2 * wc + wp

    def body(p_ref, ya_ref, yb_ref, ba_ref, bb_ref, ls_ref, z_ref):
        ga = _sigmoid(_cols(p_ref, g0, g0 + d, npc))
        gb = _sigmoid(_cols(p_ref, g0 + d, g0 + 2 * d, npc))
        z = ga * (ya_ref[...] + ba_ref[...]) + gb * ((yb_ref[...] + bb_ref[...]) * ls_ref[...])
        z_ref[...] = z.astype(BF16)

    row = pl.BlockSpec((ts, d), lambda i: (i, 0))
    return _pcall(body, name=name, out_shape=_sds((s_len, d), BF16), grid=(s_len // ts,),
                  in_specs=[pl.BlockSpec((N_CHIPS, ts, npc), lambda i: (0, i, 0)), row, row, _vec(d), _vec(d), _vec(d)],
                  out_specs=row, vmem_mb=48)(proj, ya, yb, b_a, b_b, ls)


def _gates_bwd(name, proj, dz, ya, yb, b_a, b_b, ls, wc, wp, ts):
    _, s_len, npc = proj.shape
    d = ya.shape[1]
    g0 = 2 * wc + wp

    def body(p_ref, dz_ref, ya_ref, yb_ref, ba_ref, bb_ref, ls_ref, dya_ref, dyb_ref, dgt_ref, dba_ref, dls_ref,
             dbb_ref):
        i = pl.program_id(0)
        ga = _sigmoid(_cols(p_ref, g0, g0 + d, npc))
        gb = _sigmoid(_cols(p_ref, g0 + d, g0 + 2 * d, npc))
        dz_v = dz_ref[...]
        y_a = ya_ref[...] + ba_ref[...]
        y_b0 = yb_ref[...] + bb_ref[...]
        ls_v = ls_ref[...]
        dya = dz_v * ga
        dya_ref[...] = dya.astype(BF16)
        _acc_rows(dba_ref, dya, i)
        t = dz_v * gb
        _acc_rows(dls_ref, t * y_b0, i)
        dyb = t * ls_v
        dyb_ref[...] = dyb.astype(BF16)
        _acc_rows(dbb_ref, dyb, i)
        dgt_ref[:, 0:d] = (dz_v * y_a * ga * (1.0 - ga)).astype(BF16)
        dgt_ref[:, d:2 * d] = (dz_v * (y_b0 * ls_v) * gb * (1.0 - gb)).astype(BF16)

    row = pl.BlockSpec((ts, d), lambda i: (i, 0))
    return _pcall(
        body, name=name,
        out_shape=[_sds((s_len, d), BF16), _sds((s_len, d), BF16), _sds((s_len, 2 * d), BF16)] + [_sds((1, d), F32)] * 3,
        grid=(s_len // ts,),
        in_specs=[pl.BlockSpec((N_CHIPS, ts, npc), lambda i: (0, i, 0)), row, row, row, _vec(d), _vec(d), _vec(d)],
        out_specs=[row, row, pl.BlockSpec((ts, 2 * d), lambda i: (i, 0))] + [_vec(d)] * 3, vmem_mb=48,
    )(proj, dz, ya, yb, b_a, b_b, ls)


def _conv_branch_bwd(name, proj, da3, cw, cb, lg, lb, wc, wp, ts):
    _, s_len, npc = proj.shape
    nq = _chips_covering(2 * wc, npc)
    hb = ts // HALO

    n_tiles = s_len // ts

    def fold(v):
        return jnp.sum(v.reshape(ROW_CHUNK // SUBLANES, SUBLANES, v.shape[-1]), axis=0)

    def body(p_ref, ph_ref, da3_ref, cw_ref, cb_ref, lg_ref, lb_ref, da1_ref, dlg_ref, dlb_ref, dcb_ref, dcw_ref,
             a0s_ref, a0p_ref, vec8_ref, dcw8_ref):
        i = pl.program_id(0)
        _stage_glu(p_ref, ph_ref, a0s_ref, i, wc, npc, ts)
        _make_phases(a0s_ref, a0p_ref)

        @pl.when(i == 0)
        def _():
            vec8_ref[...] = jnp.zeros_like(vec8_ref)
            dcw8_ref[...] = jnp.zeros_like(dcw8_ref)

        def chunk(r0):
            rows = pl.ds(r0, ROW_CHUNK)
            xh, rstd, a2 = _conv_ln(a0s_ref, a0p_ref, cw_ref, cb_ref, lg_ref, lb_ref, r0)
            sig = _sigmoid(a2)
            da2 = da3_ref[rows, :] * (sig * (1.0 + a2 * (1.0 - sig)))
            vec8_ref[0] += fold(da2 * xh)
            vec8_ref[1] += fold(da2)
            dxh = da2 * lg_ref[...]
            da1 = rstd * (dxh - jnp.mean(dxh, axis=-1, keepdims=True)
                          - xh * jnp.mean(dxh * xh, axis=-1, keepdims=True))
            da1_ref[rows, :] = da1
            vec8_ref[2] += fold(da1)
            for k in range(CONV_K):
                dcw8_ref[k] += fold(da1 * _window(a0s_ref, a0p_ref, HALO - CONV_K + 1 + k, r0))

        _for_chunks(ts, chunk)

        @pl.when(i == n_tiles - 1)
        def _():
            dlg_ref[...] = jnp.sum(vec8_ref[0], axis=0, keepdims=True)
            dlb_ref[...] = jnp.sum(vec8_ref[1], axis=0, keepdims=True)
            dcb_ref[...] = jnp.sum(vec8_ref[2], axis=0, keepdims=True)
            dcw_ref[...] = jnp.sum(dcw8_ref[...], axis=1)

    return _pcall(
        body, name=name,
        out_shape=[_sds((s_len, wc), F32)] + [_sds((1, wc), F32)] * 3 + [_sds((HALO, wc), F32)],
        grid=(s_len // ts,),
        in_specs=[pl.BlockSpec((nq, ts, npc), lambda i: (0, i, 0)),
                  pl.BlockSpec((nq, HALO, npc), lambda i: (0, jnp.maximum(i * hb - 1, 0), 0)),
                  pl.BlockSpec((ts, wc), lambda i: (i, 0)),
                  pl.BlockSpec((HALO, wc), lambda i: (0, 0)), _vec(wc), _vec(wc), _vec(wc)],
        out_specs=[pl.BlockSpec((ts, wc), lambda i: (i, 0)), _vec(wc), _vec(wc), _vec(wc),
                   pl.BlockSpec((HALO, wc), lambda i: (0, 0))],
        scratch=[pltpu.VMEM((HALO + ts, wc), F32), _phase_scratch(HALO + ts, wc),
                 pltpu.VMEM((3, SUBLANES, wc), F32), pltpu.VMEM((HALO, SUBLANES, wc), F32)], vmem_mb=56,
    )(proj, proj, da3, cw, cb, lg, lb)


def _mixer_in_bwd(name, proj, da1, dmixed, dgates, cw, wc, wp, ts):
    _, s_len, npc = proj.shape
    nq = _chips_covering(2 * wc, npc)
    gi = wp // len(POOL_WINDOWS)
    hb = ts // HALO
    n_tiles = s_len // ts
    last_hb = s_len // HALO - 1
    d2 = dgates.shape[1]

    def body(p_ref, d1_ref, d1n_ref, dm_ref, dmn_ref, dgt_ref, cw_ref, o_ref, d1s_ref, es_ref, d1p_ref, ep_ref):
        i = pl.program_id(0)
        more = i < n_tiles - 1
        d1s_ref[0:ts, :] = d1_ref[...]
        d1s_ref[ts:ts + HALO, :] = jnp.where(more, d1n_ref[...], 0.0)
        t_abs = i * ts + lax.broadcasted_iota(jnp.int32, (ts + HALO, 1), 0)
        dm_ext = jnp.concatenate([dm_ref[...], jnp.where(more, dmn_ref[...], 0.0)], axis=0)
        for g, win in enumerate(POOL_WINDOWS):
            cs = slice(g * gi, (g + 1) * gi)
            es_ref[:, cs] = dm_ext[:, cs] / jnp.minimum(t_abs + 1, win).astype(F32)
        _make_phases(d1s_ref, d1p_ref)
        _make_phases(es_ref, ep_ref)

        def chunk(r0):
            rows = pl.ds(r0, ROW_CHUNK)
            da0 = cw_ref[0:1, :] * _window(d1s_ref, d1p_ref, CONV_K - 1, r0)
            for k in range(1, CONV_K):
                da0 = da0 + cw_ref[k:k + 1, :] * _window(d1s_ref, d1p_ref, CONV_K - 1 - k, r0)
            glu_a = _cols(p_ref, 0, wc, npc, rows)
            sig = _sigmoid(_cols(p_ref, wc, 2 * wc, npc, rows))
            _store_cols(o_ref, 0, (da0 * sig).astype(BF16), npc, rows)
            _store_cols(o_ref, wc, (da0 * glu_a * sig * (1.0 - sig)).astype(BF16), npc, rows)
            parts = []
            for g, win in enumerate(POOL_WINDOWS):
                cs = slice(g * gi, (g + 1) * gi)
                acc = _window(es_ref, ep_ref, 0, r0, cs)
                for dlt in range(1, win):
                    acc = acc + _window(es_ref, ep_ref, dlt, r0, cs)
                parts.append(acc - dm_ref[rows, cs])
            _store_cols(o_ref, 2 * wc, jnp.concatenate(parts, axis=-1).astype(BF16), npc, rows)

        _for_chunks(ts, chunk)
        _store_cols(o_ref, 2 * wc + wp, dgt_ref[...], npc)

    nxt = lambda i: (jnp.minimum((i + 1) * hb, last_hb), 0)
    return _pcall(
        body, name=name, out_shape=_sds((N_CHIPS, s_len, npc), BF16), grid=(n_tiles,),
        in_specs=[pl.BlockSpec((nq, ts, npc), lambda i: (0, i, 0)),
                  pl.BlockSpec((ts, wc), lambda i: (i, 0)), pl.BlockSpec((HALO, wc), nxt),
                  pl.BlockSpec((ts, wp), lambda i: (i, 0)), pl.BlockSpec((HALO, wp), nxt),
                  pl.BlockSpec((ts, d2), lambda i: (i, 0)),
                  pl.BlockSpec((HALO, wc), lambda i: (0, 0))],
        out_specs=pl.BlockSpec((N_CHIPS, ts, npc), lambda i: (0, i, 0)),
        scratch=[pltpu.VMEM((ts + HALO, wc), F32), pltpu.VMEM((ts + HALO, wp), F32),
                 _phase_scratch(ts + HALO, wc), _phase_scratch(ts + HALO, wp)], vmem_mb=56,
    )(proj, da1, da1, dmixed, dmixed, dgates, cw)


def _ada_fwd(name, c_all, w, b):
    d, cols = w.shape
    tn = 512 if cols % 512 == 0 else cols

    def body(c_ref, w_ref, b_ref, o_ref):
        cv = c_ref[...]
        sc = (cv * _sigmoid(cv)).astype(BF16)
        o_ref[...] = jnp.dot(sc, w_ref[...].astype(BF16), preferred_element_type=F32) + b_ref[...]

    return _pcall(body, name=name, out_shape=_sds((N_DEV, cols), F32), grid=(cols // tn,),
                  in_specs=[pl.BlockSpec((N_DEV, d), lambda j: (0, 0)), pl.BlockSpec((d, tn), lambda j: (0, j)),
                            pl.BlockSpec((1, tn), lambda j: (0, j))],
                  out_specs=pl.BlockSpec((N_DEV, tn), lambda j: (0, j)), vmem_mb=32)(c_all, w, b)


def _adam_math(w, g, m, v):
    m_new = ADAM_B1 * m + (1.0 - ADAM_B1) * g
    v_new = ADAM_B2 * v + (1.0 - ADAM_B2) * (g * g)
    m_hat = m_new / (1.0 - ADAM_B1 ** ADAM_STEP)
    v_hat = v_new / (1.0 - ADAM_B2 ** ADAM_STEP)
    delta = -ADAM_LR * (m_hat / (jnp.sqrt(v_hat) + ADAM_EPS) + ADAM_WD * w)
    return delta, m_new, v_new


def _adamw(name, w, g, m, v):
    rows, cols = w.shape
    tr = _row_tile(rows, cols, 262144)

    def body(w_ref, g_ref, m_ref, v_ref, go_ref, d_ref, mo_ref, vo_ref):
        g = g_ref[...]
        go_ref[...] = g
        d_ref[...], mo_ref[...], vo_ref[...] = _adam_math(w_ref[...], g, m_ref[...], v_ref[...])

    spec = pl.BlockSpec((tr, cols), lambda i: (i, 0))
    return _pcall(body, name=name, out_shape=[_sds(w.shape, F32)] * 4, grid=(rows // tr,), in_specs=[spec] * 4,
                  out_specs=[spec] * 4, vmem_mb=40)(w, g, m, v)


def _ada_grad_adamw(name, c_t, d_ada, w, m, v):
    rows, cols = w.shape
    tr = _tile(rows, 256)
    tc = _tile(cols, 1536) if cols % 1536 == 0 else cols

    def body(c_ref, da_ref, w_ref, m_ref, v_ref, g_ref, d_ref, mo_ref, vo_ref):
        cv = c_ref[...]
        sc = cv * _sigmoid(cv)
        g = sc[:, 0:1] * da_ref[0:1, :]
        for b in range(1, N_DEV):
            g = g + sc[:, b:b + 1] * da_ref[b:b + 1, :]
        g_ref[...] = g
        d_ref[...], mo_ref[...], vo_ref[...] = _adam_math(w_ref[...], g, m_ref[...], v_ref[...])

    spec = pl.BlockSpec((tr, tc), lambda i, j: (i, j))
    return _pcall(body, name=name, out_shape=[_sds(w.shape, F32)] * 4, grid=(rows // tr, cols // tc),
                  in_specs=[pl.BlockSpec((tr, N_DEV), lambda i, j: (i, 0)),
                            pl.BlockSpec((N_DEV, tc), lambda i, j: (0, j)), spec, spec, spec],
                  out_specs=[spec] * 4, vmem_mb=40)(c_t, d_ada, w, m, v)


def _sum_devices(name, gathered, m_per):
    n = gathered.shape[1]

    def body(g_ref, o_ref):
        acc = g_ref[0:m_per, :]
        for dev in range(1, N_DEV):
            acc = acc + g_ref[dev * m_per:(dev + 1) * m_per, :]
        o_ref[...] = acc

    return _pcall(body, name=name, out_shape=_sds((m_per, n), F32),
                  in_specs=[pl.BlockSpec(memory_space=pltpu.VMEM)],
                  out_specs=pl.BlockSpec(memory_space=pltpu.VMEM))(gathered)


def _ffn_fwd(tag, n, w_in_parts, w_out_after_swiglu, dims):
    s_len, d, f_dim = dims["S"], dims["D"], dims["F"]
    tf = f_dim // 4
    tm0, tm = _tile(s_len, 512), _tile(s_len, 1024)
    n_parts = len(w_in_parts)
    nbp = (f_dim // 2) // tf
    nbq = nbp // n_parts

    def ep(accs, ex, outs):
        hh, uu = accs
        outs[0][0] = hh.astype(BF16)
        outs[0][1] = uu.astype(BF16)
        outs[1][...] = (hh * _sigmoid(hh) * uu).astype(BF16)

    done = ()
    for part, get_w in enumerate(w_in_parts):
        w_g = get_w()
        col = lambda j, part=part: (j // nbq) * nbp + part * nbq + j % nbq
        done = _matmul(
            f"{tag}_swiglu{part}", n, [w_g, w_g], mode="nn", grid=(s_len // tm0, 2 * nbq, 1),
            a_spec=pl.BlockSpec((tm0, d), lambda i, j, k: (i, 0)),
            b_specs=[pl.BlockSpec((None, d, tf), lambda i, j, k, part=part: (j // nbq, 0, part * nbq + j % nbq)),
                     pl.BlockSpec((None, d, tf), lambda i, j, k, part=part: (2 + j // nbq, 0, part * nbq + j % nbq))],
            out_shape=[_sds((2, s_len, f_dim), BF16), _sds((s_len, f_dim), BF16)],
            out_specs=[pl.BlockSpec((2, tm0, tf), lambda i, j, k, col=col: (0, i, col(j))),
                       pl.BlockSpec((tm0, tf), lambda i, j, k, col=col: (i, col(j)))],
            acc_shape=(tm0, tf), epilogue=ep, carry=done)
    hu, act = done
    w_out2d = w_out_after_swiglu()
    tn2 = _tile(d, 1024)
    f = _matmul(
        f"{tag}_down", act, [w_out2d], mode="nn", grid=(s_len // tm, d // tn2, 2),
        a_spec=pl.BlockSpec((tm, 2 * tf), lambda i, j, k: (i, k)),
        b_specs=[pl.BlockSpec((2 * tf, tn2), lambda i, j, k: (k, j))],
        out_shape=_sds((s_len, d), F32), out_specs=pl.BlockSpec((tm, tn2), lambda i, j, k: (i, j)),
        acc_shape=(tm, tn2), epilogue=_ep_store(F32))
    return hu, act, f, w_out2d


def _ffn_bwd(tag, n, hu, act, df, w_in_g, w_out2d, dims, after_dw_out, after_dw_in):
    s_len, d, f_dim = dims["S"], dims["D"], dims["F"]
    tf = f_dim // 4
    tk = _tile(s_len, 2048)
    tn = _tile(d, 1024)
    g_out = _matmul(
        f"{tag}_dw_out", act, [df], mode="tn", grid=(4, d // tn, s_len // tk),
        a_spec=pl.BlockSpec((tk, tf), lambda i, j, k: (k, i)),
        b_specs=[pl.BlockSpec((tk, tn), lambda i, j, k: (k, j))],
        out_shape=_sds((2, 4, tf // 2, d), F32),
        out_specs=pl.BlockSpec((2, None, tf // 2, tn), lambda i, j, k: (0, i, 0, j)),
        acc_shape=(tf, tn), epilogue=_ep_halves(tf // 2))
    after_dw_out(g_out)

    def ep_dhu(accs, ex, outs):
        da = accs[0]
        hh, uu = ex[0][0].astype(F32), ex[0][1].astype(F32)
        sig = _sigmoid(hh)
        outs[0][0] = (da * uu * (sig * (1.0 + hh * (1.0 - sig)))).astype(BF16)
        outs[0][1] = (da * (hh * sig)).astype(BF16)

    tm = _tile(s_len, 512)
    hu_spec = pl.BlockSpec((2, tm, tf), lambda i, j, k: (0, i, j))
    dhu = _matmul(
        f"{tag}_dhu", df, [w_out2d], mode="nt", grid=(s_len // tm, 4, 1),
        a_spec=pl.BlockSpec((tm, d), lambda i, j, k: (i, 0)),
        b_specs=[pl.BlockSpec((tf, d), lambda i, j, k: (j, 0))],
        extras=[hu], extra_specs=[hu_spec],
        out_shape=_sds((2, s_len, f_dim), BF16), out_specs=hu_spec, acc_shape=(tm, tf), epilogue=ep_dhu)

    hd = d // 2
    g_in = _matmul(
        f"{tag}_dw_in", n, [dhu], mode="tn", grid=(2, 8, s_len // tk),
        a_spec=pl.BlockSpec((tk, hd), lambda i, j, k: (k, i)),
        b_specs=[pl.BlockSpec((None, tk, tf), lambda i, j, k: (j // 4, k, j % 4))],
        out_shape=_sds((2, 4, hd, f_dim // 2), F32),
        out_specs=pl.BlockSpec((None, None, hd, tf), lambda i, j, k: (i, j // 2, 0, j % 2)),
        acc_shape=(hd, tf), epilogue=_ep_store(F32))
    after_dw_in(g_in)

    tm2 = _tile(s_len, 1024)
    dn = _matmul(
        f"{tag}_dn", dhu, [w_in_g], mode="nt", grid=(s_len // tm2, d // tn, N_CHIPS),
        a_spec=pl.BlockSpec((None, tm2, 2 * tf), lambda i, j, k: (k // 2, i, k % 2)),
        b_specs=[pl.BlockSpec((None, tn, 2 * tf), lambda i, j, k: (k, j, 0))],
        out_shape=_sds((s_len, d), F32), out_specs=pl.BlockSpec((tm2, tn), lambda i, j, k: (i, j)),
        acc_shape=(tm2, tn), epilogue=_ep_store(F32))
    return dn


def kernel(x, c, w_ada, b_ada, g_ffn1, w1_in, w1_out, g_mix, w_in, conv_w, conv_b, ln_a_g, ln_a_b, w_a_out, b_a_out, w_b_group, b_b_group, ls_b, w_out, g_ffn2, w2_in, w2_out, g_final, loss_target, m_w_ada, m_b_ada, m_g_ffn1, m_w1_in, m_w1_out, m_g_mix, m_w_in, m_conv_w, m_conv_b, m_ln_a_g, m_ln_a_b, m_w_a_out, m_b_a_out, m_w_b_group, m_b_b_group, m_ls_b, m_w_out, m_g_ffn2, m_w2_in, m_w2_out, m_g_final, v_w_ada, v_b_ada, v_g_ffn1, v_w1_in, v_w1_out, v_g_mix, v_w_in, v_conv_w, v_conv_b, v_ln_a_g, v_ln_a_b, v_w_a_out, v_b_a_out, v_w_b_group, v_b_b_group, v_ls_b, v_w_out, v_g_ffn2, v_w2_in, v_w2_out, v_g_final):
    weights = dict(w_ada=w_ada, b_ada=b_ada, g_ffn1=g_ffn1, w1_in=w1_in, w1_out=w1_out, g_mix=g_mix, w_in=w_in,
                   conv_w=conv_w, conv_b=conv_b, ln_a_g=ln_a_g, ln_a_b=ln_a_b, w_a_out=w_a_out, b_a_out=b_a_out,
                   w_b_group=w_b_group, b_b_group=b_b_group, ls_b=ls_b, w_out=w_out, g_ffn2=g_ffn2, w2_in=w2_in,
                   w2_out=w2_out, g_final=g_final)
    mom1 = dict(w_ada=m_w_ada, b_ada=m_b_ada, g_ffn1=m_g_ffn1, w1_in=m_w1_in, w1_out=m_w1_out, g_mix=m_g_mix,
                w_in=m_w_in, conv_w=m_conv_w, conv_b=m_conv_b, ln_a_g=m_ln_a_g, ln_a_b=m_ln_a_b, w_a_out=m_w_a_out,
                b_a_out=m_b_a_out, w_b_group=m_w_b_group, b_b_group=m_b_b_group, ls_b=m_ls_b, w_out=m_w_out,
                g_ffn2=m_g_ffn2, w2_in=m_w2_in, w2_out=m_w2_out, g_final=m_g_final)
    mom2 = dict(w_ada=v_w_ada, b_ada=v_b_ada, g_ffn1=v_g_ffn1, w1_in=v_w1_in, w1_out=v_w1_out, g_mix=v_g_mix,
                w_in=v_w_in, conv_w=v_conv_w, conv_b=v_conv_b, ln_a_g=v_ln_a_g, ln_a_b=v_ln_a_b, w_a_out=v_w_a_out,
                b_a_out=v_b_a_out, w_b_group=v_w_b_group, b_b_group=v_b_b_group, ls_b=v_ls_b, w_out=v_w_out,
                g_ffn2=v_g_ffn2, w2_in=v_w2_in, w2_out=v_w2_out, g_final=v_g_final)
    order = list(weights)

    s_len, d = x.shape[1], x.shape[2]
    f_dim = w1_out.shape[0] * N_CHIPS
    wc = conv_w.shape[1] * N_CHIPS
    wp = w_b_group.shape[0] * w_b_group.shape[1]
    n_groups, gi, goq = w_b_group.shape
    npc = w_in.shape[1]
    ada_c = w_ada.shape[1]
    dims = dict(S=s_len, D=d, F=f_dim)
    ts = _tile(s_len, 256)

    xi, yi, ci = lax.axis_index("x"), lax.axis_index("y"), lax.axis_index("c")
    q = 2 * xi + yi
    dev = 2 * q + ci
    q_idx = jnp.reshape(q, (1,)).astype(jnp.int32)
    qc_idx = jnp.stack([q, ci]).astype(jnp.int32)
    _PREVIOUS.clear()

    cwq = conv_w.shape[1]
    pack0 = jnp.concatenate([c.reshape(-1), conv_w.reshape(-1), b_b_group.reshape(-1)])
    n0 = -(-pack0.shape[0] // (8 * LANES)) * LANES
    pack0 = jnp.pad(pack0, (0, 8 * n0 - pack0.shape[0])).reshape(8, n0)
    g0 = _allgather_small("gather_small_in", pack0).reshape(N_DEV, 8 * n0)
    c_all = g0[:, :d]
    south = g0[0::2]
    cw_full = jnp.concatenate([south[k, d:d + CONV_K * cwq].reshape(CONV_K, cwq) for k in range(N_CHIPS)], axis=1)
    cw_pad = jnp.pad(cw_full, ((0, HALO - CONV_K), (0, 0)))
    o_bb = d + CONV_K * cwq
    bb_full = jnp.concatenate([south[k, o_bb:o_bb + n_groups * goq].reshape(n_groups, goq) for k in range(N_CHIPS)],
                              axis=1).reshape(1, d)

    as2d = lambda a: a.reshape(-1, a.shape[-1])
    groups = dict(w1_out=["w1_out"], w_in=["w_in"], mix=["w_a_out", "w_b_group", "w_out"], w2_in=["w2_in"],
                  w2_out=["w2_out"])
    big = ["w1_in", *[nm for grp in groups.values() for nm in grp]]
    cast = lambda nm: _cast_into_gathered(f"cast_{nm}", as2d(weights[nm]), q_idx)
    w1_ici = [_gather_ici("gather_w1_in_a_ici", [cast("w1_in")], 0, 2)]

    b_ada_mine = lax.dynamic_slice(b_ada, (q * ada_c,), (ada_c,)).reshape(1, ada_c)
    ada_piece = _ada_fwd("ada_fwd", c_all, w_ada, b_ada_mine)
    casts = {nm: cast(nm) for nm in big[1:]}
    g1 = _allgather_small("gather_ada", ada_piece).reshape(N_DEV, N_DEV, ada_c)
    w1_ici.append(_gather_ici("gather_w1_in_b_ici", w1_ici[0].arrays, 1, 2))
    ici = {grp: _gather_ici(f"gather_{grp}_ici", [casts[nm] for nm in names]) for grp, names in groups.items()}
    ada_rows = lax.dynamic_index_in_dim(g1[0::2], dev, axis=1, keepdims=False)
    ada = ada_rows.reshape(3, 3, 1, d)
    (sh1, sc1, gt1), (sh2, sc2, gt2), (sh3, sc3, gt3) = [[ada[i, j] for j in range(3)] for i in range(3)]

    row = lambda vct: vct.reshape(1, -1)
    g1v, gmv, g2v, gfv = row(g_ffn1), row(g_mix), row(g_ffn2), row(g_final)

    def arrived(grp):
        return _gather_d2d(f"gather_{grp}_d2d", ici[grp].wait())

    def gathered(fwd, grp):
        return {nm: g.reshape(N_CHIPS, 2 * g.shape[2], g.shape[3]) for nm, g in zip(groups[grp], fwd.wait())}

    x2 = x[0]
    tgt = loss_target[0]

    n1 = _norm_mod("ffn1_norm", x2, g1v, sc1, sh1, ts)
    fwd, w1_buf = {}, [w1_ici[1].arrays]

    def w1_in_part(part):
        def get():
            here = w1_ici[part].wait(w1_buf[0])
            w1_buf[0] = _gather_d2d(f"gather_w1_in_{'ab'[part]}_d2d", here, part, 2).wait()
            g = w1_buf[0][0]
            return g.reshape(N_CHIPS, 2 * g.shape[2], g.shape[3])
        return get

    def w1_out_after_swiglu():
        fwd["w1_out"] = arrived("w1_out")
        fwd["w_in"] = arrived("w_in")
        return gathered(fwd["w1_out"], "w1_out")["w1_out"].reshape(f_dim, d)

    hu1, act1, f1, w1_out_2d = _ffn_fwd("ffn1", n1, [w1_in_part(0), w1_in_part(1)], w1_out_after_swiglu, dims)
    w1_in_g = w1_buf[0][0].reshape(N_CHIPS, 2 * w1_buf[0][0].shape[2], w1_buf[0][0].shape[3])
    h1, n2 = _residual_norm_mod("mix_norm", x2, f1, gt1, 0.5, gmv, sc2, sh2, ts)
    w_in_g = gathered(fwd["w_in"], "w_in")["w_in"]

    tm = _tile(s_len, 1024)
    tnp = npc // 2
    proj = _matmul(
        "mix_proj", n2, [w_in_g], mode="nn", grid=(s_len // tm, 8, 1),
        a_spec=pl.BlockSpec((tm, d), lambda i, j, k: (i, 0)),
        b_specs=[pl.BlockSpec((None, d, tnp), lambda i, j, k: (j // 2, 0, j % 2))],
        out_shape=_sds((N_CHIPS, s_len, npc), BF16),
        out_specs=pl.BlockSpec((None, tm, tnp), lambda i, j, k: (j // 2, i, j % 2)),
        acc_shape=(tm, tnp), epilogue=_ep_store(BF16))
    fwd["mix"] = arrived("mix")
    cbv, lgv, lbv = row(conv_b), row(ln_a_g), row(ln_a_b)
    a3, mixed = _mixer_mid("mix_mid", proj, cw_pad, cbv, lgv, lbv, wc, wp, ts)
    wts = gathered(fwd["mix"], "mix")
    w_out_2d = wts["w_out"].reshape(d, d)
    w_a_g = wts["w_a_out"]
    w_b_g = wts["w_b_group"]
    dq = d // N_CHIPS
    ya = _matmul(
        "mix_ya", a3, [w_a_g], mode="nn", grid=(s_len // tm, N_CHIPS, 1),
        a_spec=pl.BlockSpec((tm, wc), lambda i, j, k: (i, 0)),
        b_specs=[pl.BlockSpec((None, wc, dq), lambda i, j, k: (j, 0, 0))],
        out_shape=_sds((s_len, d), BF16), out_specs=pl.BlockSpec((tm, dq), lambda i, j, k: (i, j)),
        acc_shape=(tm, dq), epilogue=_ep_store(BF16))
    yb = _matmul(
        "mix_yb", mixed, [w_b_g], mode="nn", grid=(s_len // tm, n_groups * N_CHIPS, 1),
        a_spec=pl.BlockSpec((tm, gi), lambda i, j, k: (i, j // N_CHIPS)),
        b_specs=[pl.BlockSpec((None, gi, goq), lambda i, j, k: (j % N_CHIPS, j // N_CHIPS, 0))],
        out_shape=_sds((s_len, d), BF16), out_specs=pl.BlockSpec((tm, goq), lambda i, j, k: (i, j)),
        acc_shape=(tm, goq), epilogue=_ep_store(BF16))
    bav, lsv = row(b_a_out), row(ls_b)
    z = _gates_fwd("mix_gates", proj, ya, yb, bav, bb_full, lsv, wc, wp, ts)
    tn = _tile(d, 1024)
    mix = _matmul(
        "mix_out", z, [w_out_2d], mode="nn", grid=(s_len // tm, d // tn, 1),
        a_spec=pl.BlockSpec((tm, d), lambda i, j, k: (i, 0)),
        b_specs=[pl.BlockSpec((d, tn), lambda i, j, k: (0, j))],
        out_shape=_sds((s_len, d), F32), out_specs=pl.BlockSpec((tm, tn), lambda i, j, k: (i, j)),
        acc_shape=(tm, tn), epilogue=_ep_store(F32))
    fwd["w2_in"] = arrived("w2_in")
    h2, n3 = _residual_norm_mod("ffn2_norm", h1, mix, gt2, 1.0, g2v, sc3, sh3, ts)
    w2_in_g = gathered(fwd["w2_in"], "w2_in")["w2_in"]
    hu2, act2, f3, w2_out_2d = _ffn_fwd(
        "ffn2", n3, [lambda: w2_in_g],
        lambda: gathered(arrived("w2_out"), "w2_out")["w2_out"].reshape(f_dim, d), dims)

    dh3, df3, d_gf, d_gt3, loss_cols = _final_loss("final_loss", h2, f3, tgt, gt3, 0.5, gfv, ts)
    rs, held = {}, {}
    dn3 = _ffn_bwd(
        "ffn2", n3, hu2, act2, df3, w2_in_g, w2_out_2d, dims,
        after_dw_out=lambda g: held.update(w2_out=g),
        after_dw_in=lambda g: rs.update(ffn2=_ReduceScatter("g_ffn2", ["w2_out", "w2_in"], [held["w2_out"], g],
                                                            qc_idx)))
    dh2, dmix, d_sh3, d_sc3, d_g2, d_gt2 = _norm_mod_bwd("ffn2_norm_bwd", h2, dn3, dh3, g2v, sc3, ts,
                                                         prev=(mix, gt2, 1.0))
    rs["ffn2"].step2()

    tk = s_len
    hq = d // (2 * N_CHIPS)
    gw_out = _matmul(
        "mix_dw_out", z, [dmix], mode="tn", grid=(N_CHIPS, d // tn, s_len // tk),
        a_spec=pl.BlockSpec((tk, 2 * hq), lambda i, j, k: (k, i)),
        b_specs=[pl.BlockSpec((tk, tn), lambda i, j, k: (k, j))],
        out_shape=_sds((2, N_CHIPS, hq, d), F32),
        out_specs=pl.BlockSpec((2, None, hq, tn), lambda i, j, k: (0, i, 0, j)),
        acc_shape=(2 * hq, tn), epilogue=_ep_halves(hq))
    dz = _matmul(
        "mix_dz", dmix, [w_out_2d], mode="nt", grid=(s_len // tm, d // tn, 1),
        a_spec=pl.BlockSpec((tm, d), lambda i, j, k: (i, 0)),
        b_specs=[pl.BlockSpec((tn, d), lambda i, j, k: (j, 0))],
        out_shape=_sds((s_len, d), F32), out_specs=pl.BlockSpec((tm, tn), lambda i, j, k: (i, j)),
        acc_shape=(tm, tn), epilogue=_ep_store(F32))
    dya, dyb, dgates, d_ba, d_ls, d_bb = _gates_bwd("mix_gates_bwd", proj, dz, ya, yb, bav, bb_full, lsv, wc, wp, ts)
    gw_a = _matmul(
        "mix_dw_a", a3, [dya], mode="tn", grid=(1, N_CHIPS, s_len // tk),
        a_spec=pl.BlockSpec((tk, wc), lambda i, j, k: (k, 0)),
        b_specs=[pl.BlockSpec((tk, dq), lambda i, j, k: (k, j))],
        out_shape=_sds((2, N_CHIPS, wc // 2, dq), F32),
        out_specs=pl.BlockSpec((2, None, wc // 2, dq), lambda i, j, k: (0, j, 0, 0)),
        acc_shape=(wc, dq), epilogue=_ep_halves(wc // 2))
    da3 = _matmul(
        "mix_da3", dya, [w_a_g], mode="nt", grid=(s_len // tm, 1, N_CHIPS),
        a_spec=pl.BlockSpec((tm, dq), lambda i, j, k: (i, k)),
        b_specs=[pl.BlockSpec((None, wc, dq), lambda i, j, k: (k, 0, 0))],
        out_shape=_sds((s_len, wc), F32), out_specs=pl.BlockSpec((tm, wc), lambda i, j, k: (i, 0)),
        acc_shape=(tm, wc), epilogue=_ep_store(F32))
    gpr = n_groups // 2
    gw_b = _matmul(
        "mix_dw_b", mixed, [dyb], mode="tn", grid=(1, n_groups * N_CHIPS, s_len // tk),
        a_spec=pl.BlockSpec((tk, gi), lambda i, j, k: (k, j // N_CHIPS)),
        b_specs=[pl.BlockSpec((tk, goq), lambda i, j, k: (k, j))],
        out_shape=_sds((2, N_CHIPS, gpr * gi, goq), F32),
        out_specs=pl.BlockSpec((None, None, gi, goq),
                               lambda i, j, k: ((j // N_CHIPS) // gpr, j % N_CHIPS, (j // N_CHIPS) % gpr, 0)),
        acc_shape=(gi, goq), epilogue=_ep_store(F32))
    dmixed = _matmul(
        "mix_dmixed", dyb, [w_b_g], mode="nt", grid=(s_len // tm, n_groups, N_CHIPS),
        a_spec=pl.BlockSpec((tm, goq), lambda i, j, k: (i, j * N_CHIPS + k)),
        b_specs=[pl.BlockSpec((None, gi, goq), lambda i, j, k: (k, j, 0))],
        out_shape=_sds((s_len, wp), F32), out_specs=pl.BlockSpec((tm, gi), lambda i, j, k: (i, j)),
        acc_shape=(tm, gi), epilogue=_ep_store(F32))
    da1, d_lg, d_lb, d_cb, d_cw = _conv_branch_bwd("mix_conv_bwd", proj, da3, cw_pad, cbv, lgv, lbv, wc, wp, ts)
    dproj = _mixer_in_bwd("mix_in_bwd", proj, da1, dmixed, dgates, cw_pad, wc, wp, ts)
    hd = d // 2
    gw_in = _matmul(
        "mix_dw_in", n2, [dproj], mode="tn", grid=(2, 8, s_len // tk),
        a_spec=pl.BlockSpec((tk, hd), lambda i, j, k: (k, i)),
        b_specs=[pl.BlockSpec((None, tk, tnp), lambda i, j, k: (j // 2, k, j % 2))],
        out_shape=_sds((2, N_CHIPS, hd, npc), F32),
        out_specs=pl.BlockSpec((None, None, hd, tnp), lambda i, j, k: (i, j // 2, 0, j % 2)),
        acc_shape=(hd, tnp), epilogue=_ep_store(F32))
    rs["mix"] = _ReduceScatter("g_mix", ["w_in", "w_a_out", "w_b_group", "w_out"], [gw_in, gw_a, gw_b, gw_out],
                               qc_idx)
    rs["ffn2"].step3()
    dn2 = _matmul(
        "mix_dn", dproj, [w_in_g], mode="nt", grid=(s_len // tm, d // tn, N_CHIPS),
        a_spec=pl.BlockSpec((None, tm, npc), lambda i, j, k: (k, i, 0)),
        b_specs=[pl.BlockSpec((None, tn, npc), lambda i, j, k: (k, j, 0))],
        out_shape=_sds((s_len, d), F32), out_specs=pl.BlockSpec((tm, tn), lambda i, j, k: (i, j)),
        acc_shape=(tm, tn), epilogue=_ep_store(F32))
    dh1, df1, d_sh2, d_sc2, d_gm, d_gt1 = _norm_mod_bwd("mix_norm_bwd", h1, dn2, dh2, gmv, sc2, ts,
                                                        prev=(f1, gt1, 0.5))
    rs["mix"].step2()

    def w1_in_ready(g):
        rs["w1_in"] = _ReduceScatter("g_w1_in", ["w1_in"], [g], qc_idx)
        rs["w1_out"].step2()
        rs["mix"].step3()

    dn1 = _ffn_bwd(
        "ffn1", n1, hu1, act1, df1, w1_in_g, w1_out_2d, dims,
        after_dw_out=lambda g: rs.update(w1_out=_ReduceScatter("g_w1_out", ["w1_out"], [g], qc_idx)),
        after_dw_in=w1_in_ready)
    grad_x, d_sh1, d_sc1, d_g1 = _norm_mod_bwd("ffn1_norm_bwd", x2, dn1, dh1, g1v, sc1, ts)

    d_ada = jnp.concatenate([d_sh1, d_sc1, d_gt1, d_sh2, d_sc2, d_gt2, d_sh3, d_sc3, d_gt3], axis=1)
    small = [d_ada, d_g1, d_gm, d_cw[:CONV_K].reshape(1, -1), d_cb, d_lg, d_lb, d_ba, d_bb, d_ls, d_g2, d_gf,
             loss_cols]
    sizes = [a.shape[1] for a in small]
    pack1 = jnp.concatenate(small, axis=1).reshape(-1)
    n1p = -(-pack1.shape[0] // (8 * LANES)) * LANES
    pack1 = jnp.pad(pack1, (0, 8 * n1p - pack1.shape[0])).reshape(8, n1p)
    g2 = _allgather_small("gather_small_grads", pack1)
    rs["w1_in"].step2()
    total = _sum_devices("sum_small_grads", g2, 8).reshape(-1)
    offs = [0]
    for sz in sizes:
        offs.append(offs[-1] + sz)
    tot = [total[offs[k]:offs[k + 1]] for k in range(len(sizes))]
    d_ada_all = g2.reshape(N_DEV, 8 * n1p)[:, :sizes[0]]
    loss = jnp.sum(tot[12])

    grads = {}
    grads["b_ada"] = tot[0]
    grads["g_ffn1"], grads["g_mix"] = tot[1], tot[2]
    grads["conv_w"] = lax.dynamic_slice(tot[3].reshape(CONV_K, wc), (0, q * cwq), (CONV_K, cwq))
    grads["conv_b"], grads["ln_a_g"], grads["ln_a_b"], grads["b_a_out"] = tot[4], tot[5], tot[6], tot[7]
    grads["b_b_group"] = lax.dynamic_slice(tot[8].reshape(n_groups, N_CHIPS * goq), (0, q * goq), (n_groups, goq))
    grads["ls_b"], grads["g_ffn2"], grads["g_final"] = tot[9], tot[10], tot[11]

    delta, new_m, new_v = {}, {}, {}

    def adamw_group(reduced):
        for nm, g in reduced.items():
            shp = weights[nm].shape
            go, dl, mo, vo = _adamw(f"adamw_{nm}", as2d(weights[nm]), g, as2d(mom1[nm]), as2d(mom2[nm]))
            grads[nm], delta[nm], new_m[nm], new_v[nm] = go.reshape(shp), dl.reshape(shp), mo.reshape(shp), vo.reshape(shp)

    adamw_group(rs["ffn2"].result())
    rs["w1_out"].step3()
    adamw_group(rs["mix"].result())
    d_ada_mine = lax.dynamic_slice(d_ada_all, (0, q * ada_c), (N_DEV, ada_c))
    grads["w_ada"], delta["w_ada"], new_m["w_ada"], new_v["w_ada"] = _ada_grad_adamw(
        "adamw_w_ada", c_all.T, d_ada_mine, w_ada, m_w_ada, v_w_ada)
    rs["w1_in"].step3()
    smalls = [nm for nm in order if nm not in big and nm != "w_ada"]
    flat = lambda src: jnp.concatenate([src[nm].reshape(-1) for nm in smalls])
    n_small = sum(weights[nm].size for nm in smalls)
    rows_s = -(-n_small // (8 * LANES)) * 8
    packed = [jnp.pad(flat(src), (0, rows_s * LANES - n_small)).reshape(rows_s, LANES)
              for src in (weights, grads, mom1, mom2)]
    _, dl_s, mo_s, vo_s = _adamw("adamw_small", *packed)
    off = 0
    for nm in smalls:
        sz, shp = weights[nm].size, weights[nm].shape
        delta[nm] = dl_s.reshape(-1)[off:off + sz].reshape(shp)
        new_m[nm] = mo_s.reshape(-1)[off:off + sz].reshape(shp)
        new_v[nm] = vo_s.reshape(-1)[off:off + sz].reshape(shp)
        grads[nm] = grads[nm].reshape(shp)
        off += sz
    adamw_group(rs["w1_out"].result())
    adamw_group(rs["w1_in"].result())

    return (loss, grad_x[None], *[grads[nm] for nm in order], *[delta[nm] for nm in order],
            *[new_m[nm] for nm in order], *[new_v[nm] for nm in order])
```

```python
import jax
import jax.numpy as jnp
from jax import lax
from jax.experimental import pallas as pl
from jax.experimental.pallas import tpu as pltpu

F32 = jnp.float32
BF16 = jnp.bfloat16
MESH = pl.DeviceIdType.MESH
ANY = pl.BlockSpec(memory_space=pl.ANY)
HBM = pl.BlockSpec(memory_space=pltpu.HBM)
SEM = pl.BlockSpec(memory_space=pltpu.SEMAPHORE)
EFFECT = pltpu.SideEffectType.DATAFLOW_SIDE_EFFECTING

EPS = 1e-6
CONV_K = 31
HALO = 32
POOL_WINDOWS = (2, 4, 8, 16)
N_CHIPS = 4
N_DEV = 8
LANES = 128

ADAM_LR = 0.001
ADAM_B1 = 0.9
ADAM_B2 = 0.999
ADAM_EPS = 1e-08
ADAM_WD = 0.01
ADAM_STEP = 10

DN = {
    "nn": (((1,), (0,)), ((), ())),
    "nt": (((1,), (1,)), ((), ())),
    "tn": (((0,), (0,)), ((), ())),
}


_PREVIOUS = []


def _ordered(call, args, n_lead, body, token=None, sources=()):
    dep = [pltpu.with_memory_space_constraint(p, pltpu.HBM) if p.size * p.dtype.itemsize >= (1 << 20) else p
           for p in _PREVIOUS if all(p is not a for a in (*args, *sources))]

    def wrapped(*refs):
        return body(*refs[:n_lead], *refs[n_lead + len(dep):])

    outs = call(wrapped, [ANY] * len(dep))(*args, *dep)
    seq = outs if isinstance(outs, (list, tuple)) else [outs]
    _PREVIOUS[:] = [seq[token] if token is not None else
                    next(o for o in seq if jnp.issubdtype(o.dtype, jnp.floating))]
    return outs


def _pcall(body, *, name, out_shape, grid=None, in_specs=None, out_specs=None, scratch=(), aliases=None,
           prefetch=0, vmem_mb=None):
    params = {}
    if grid is not None:
        params["dimension_semantics"] = ("arbitrary",) * len(grid)
    if vmem_mb is not None:
        params["vmem_limit_bytes"] = vmem_mb << 20
    def in_hbm(shape, spec):
        big = shape.size * jnp.dtype(shape.dtype).itemsize >= (1 << 20)
        return pltpu.HBM(shape.shape, shape.dtype) if big and getattr(spec, "memory_space", None) != pltpu.VMEM else shape

    if isinstance(out_shape, (list, tuple)):
        out_shape = [in_hbm(s, sp) for s, sp in zip(out_shape, out_specs)]
    else:
        out_shape = in_hbm(out_shape, out_specs)
    kw = dict(name=name, out_shape=out_shape, compiler_params=pltpu.CompilerParams(**params))
    if aliases:
        kw["input_output_aliases"] = aliases

    def call(wrapped, dep_specs):
        specs = list(in_specs) + dep_specs
        if prefetch:
            return pl.pallas_call(wrapped, grid_spec=pltpu.PrefetchScalarGridSpec(
                num_scalar_prefetch=prefetch, grid=grid, in_specs=specs, out_specs=out_specs,
                scratch_shapes=list(scratch)), **kw)
        if grid is not None:
            return pl.pallas_call(wrapped, grid=grid, in_specs=specs, out_specs=out_specs,
                                  scratch_shapes=list(scratch), **kw)
        return pl.pallas_call(wrapped, in_specs=specs, out_specs=out_specs, scratch_shapes=list(scratch), **kw)

    def run(*args):
        specs = [None] * prefetch + list(in_specs)
        placed = [pltpu.with_memory_space_constraint(a, pltpu.HBM)
                  if a.size * a.dtype.itemsize >= (1 << 20) and getattr(s, "memory_space", None) != pltpu.VMEM else a
                  for a, s in zip(args, specs)]
        return _ordered(call, placed, prefetch + len(in_specs), body, sources=args)

    return run


def _tile(dim, pref):
    t = min(dim, pref)
    assert dim % t == 0, (dim, pref)
    return t


def _sds(shape, dtype):
    return jax.ShapeDtypeStruct(tuple(shape), dtype)


def _sigmoid(v):
    return 0.5 * jnp.tanh(0.5 * v) + 0.5


def _vec(w):
    return pl.BlockSpec((1, w), lambda *_: (0, 0))


def _acc_rows(ref, val, i):
    @pl.when(i == 0)
    def _():
        ref[...] = jnp.zeros_like(ref)

    ref[...] += jnp.sum(val, axis=0, keepdims=True)


def _matmul(name, a, bs, *, mode, grid, a_spec, b_specs, out_shape, out_specs, acc_shape, epilogue,
            extras=(), extra_specs=(), vmem_mb=56, carry=()):
    nb, ne, nk, nc = len(bs), len(extras), grid[2], len(carry)
    dn = DN[mode]

    def body(*all_refs):
        refs = all_refs[:1 + nb + ne] + all_refs[1 + nb + ne + nc:]
        a_ref, b_refs, ex = refs[0], refs[1:1 + nb], refs[1 + nb:1 + nb + ne]
        if nk == 1:
            outs = refs[1 + nb + ne:]
            accs = [lax.dot_general(a_ref[...], b[...], dn, preferred_element_type=F32) for b in b_refs]
            epilogue(accs, ex, outs)
            return
        outs, acc_refs = refs[1 + nb + ne:-nb], refs[-nb:]
        k = pl.program_id(2)

        @pl.when(k == 0)
        def _():
            for acc in acc_refs:
                acc[...] = jnp.zeros_like(acc)

        for acc, b in zip(acc_refs, b_refs):
            acc[...] += lax.dot_general(a_ref[...], b[...], dn, preferred_element_type=F32)

        @pl.when(k == nk - 1)
        def _():
            epilogue([acc[...] for acc in acc_refs], ex, outs)

    scratch = [pltpu.VMEM(acc_shape, F32) for _ in range(nb)] if nk > 1 else []
    return _pcall(body, name=name, out_shape=out_shape, grid=grid,
                  in_specs=[a_spec, *b_specs, *extra_specs, *[ANY] * nc], out_specs=out_specs, scratch=scratch,
                  aliases={1 + nb + ne + i: i for i in range(nc)}, vmem_mb=vmem_mb)(a, *bs, *extras, *carry)


def _ep_store(dtype):
    def ep(accs, ex, outs):
        outs[0][...] = accs[0].astype(dtype)
    return ep


def _ep_halves(h):
    def ep(accs, ex, outs):
        outs[0][0] = accs[0][:h]
        outs[0][1] = accs[0][h:]
    return ep


def _place():
    x, y, c = lax.axis_index("x"), lax.axis_index("y"), lax.axis_index("c")
    chips = [(1 - x, y), (x, 1 - y), (1 - x, 1 - y)]
    return x, y, c, chips


def _allgather_small(name, block):
    m_per, n = block.shape

    def body(x_ref, out_ref, send_sems, recv_sems, local_sem):
        x, y, c, chips = _place()
        me, sibling = (x, y, c), (x, y, 1 - c)

        def rows(px, py, pc):
            return out_ref.at[pl.ds((4 * px + 2 * py + pc) * m_per, m_per), :]

        def copy(k, blk, to, src=None):
            return pltpu.make_async_remote_copy(
                src_ref=rows(*blk) if src is None else src, dst_ref=rows(*blk),
                send_sem=send_sems.at[k], recv_sem=recv_sems.at[k], device_id=to, device_id_type=MESH)

        mine = pltpu.make_async_copy(x_ref, rows(*me), local_sem)
        mine.start()
        first = [copy(0, me, sibling, src=x_ref)]
        first += [copy(1 + j, me, (*chip, c), src=x_ref) for j, chip in enumerate(chips)]
        for cp in first:
            cp.start()
        passed = [copy(4 + j, (*chip, c), sibling) for j, chip in enumerate(chips)]
        for j, chip in enumerate(chips):
            copy(1 + j, (*chip, c), me).wait_recv()
            passed[j].start()
        copy(0, sibling, me).wait_recv()
        for j, chip in enumerate(chips):
            copy(4 + j, (*chip, 1 - c), me).wait_recv()
        for cp in first + passed:
            cp.wait_send()
        mine.wait()

    return _pcall(
        body, name=name, out_shape=_sds((N_DEV * m_per, n), block.dtype),
        in_specs=[pl.BlockSpec(memory_space=pltpu.VMEM)], out_specs=pl.BlockSpec(memory_space=pltpu.VMEM),
        scratch=[pltpu.SemaphoreType.DMA((7,)), pltpu.SemaphoreType.DMA((7,)), pltpu.SemaphoreType.DMA],
    )(block)


class _SplitCopies:
    def __init__(self, name, arrays, plan, n_copies):
        self.name, self.plan, self.n = name, plan, len(arrays)
        n = self.n

        def body(*refs):
            send, recv, token = refs[n], refs[n + 1], refs[-1]
            for k, (src, dst, _, peer) in enumerate(plan(refs[:n])):
                pltpu.make_async_remote_copy(src_ref=src, dst_ref=dst, send_sem=send.at[k], recv_sem=recv.at[k],
                                             device_id=peer, device_id_type=MESH).start()
            token[...] = jnp.zeros_like(token)

        def call(wrapped, dep_specs):
            return pl.pallas_call(
                wrapped, name=f"{name}_start",
                out_shape=(pltpu.SemaphoreType.DMA((n_copies,)), pltpu.SemaphoreType.DMA((n_copies,)),
                           *[pltpu.HBM(a.shape, a.dtype) for a in arrays], _sds((8, LANES), F32)),
                in_specs=[HBM] * n + dep_specs,
                out_specs=(SEM, SEM, *[HBM] * n, pl.BlockSpec(memory_space=pltpu.VMEM)),
                input_output_aliases={i: 2 + i for i in range(n)},
                compiler_params=pltpu.CompilerParams(has_side_effects=EFFECT))

        outs = _ordered(call, [pltpu.with_memory_space_constraint(a, pltpu.HBM) for a in arrays], n, body, token=-1,
                        sources=arrays)
        self.send, self.recv, self.arrays = outs[0], outs[1], list(outs[2:2 + n])

    def wait(self, arrays=None):
        n, plan = self.n, self.plan
        if arrays is not None:
            self.arrays = list(arrays)

        def body(*refs):
            send, recv, token = refs[n], refs[n + 1], refs[-1]
            for k, (src, _, landing, peer) in enumerate(plan(refs[:n])):
                cp = pltpu.make_async_remote_copy(src_ref=src, dst_ref=landing, send_sem=send.at[k],
                                                  recv_sem=recv.at[k], device_id=peer, device_id_type=MESH)
                cp.wait_send()
                cp.wait_recv()
            token[...] = jnp.zeros_like(token)

        def call(wrapped, dep_specs):
            return pl.pallas_call(
                wrapped, name=f"{self.name}_wait",
                out_shape=(*[pltpu.HBM(a.shape, a.dtype) for a in self.arrays], _sds((8, LANES), F32)),
                in_specs=[HBM] * n + [SEM, SEM] + dep_specs,
                out_specs=(*[HBM] * n, pl.BlockSpec(memory_space=pltpu.VMEM)),
                input_output_aliases={i: i for i in range(n)},
                compiler_params=pltpu.CompilerParams(has_side_effects=EFFECT))

        return list(_ordered(call, [*self.arrays, self.send, self.recv], n + 2, body, token=-1))[:n]


def _col_range(g, part, n_parts):
    width = g.shape[-1] // n_parts
    return (slice(None), pl.ds(part * width, width))


def _gather_ici(name, gathered, part=0, n_parts=1):
    def plan(refs):
        x, y, c, chips = _place()
        q = 2 * x + y
        return [(g.at[(q, c, *_col_range(g, part, n_parts))], g.at[(q, c, *_col_range(g, part, n_parts))],
                 g.at[(2 * px + py, c, *_col_range(g, part, n_parts))], (px, py, c))
                for g in refs for px, py in chips]

    return _SplitCopies(name, gathered, plan, 3 * len(gathered))


def _gather_d2d(name, gathered, part=0, n_parts=1):
    def plan(refs):
        x, y, c, chips = _place()
        return [(g.at[(2 * px + py, c, *_col_range(g, part, n_parts))],
                 g.at[(2 * px + py, c, *_col_range(g, part, n_parts))],
                 g.at[(2 * px + py, 1 - c, *_col_range(g, part, n_parts))], (x, y, 1 - c))
                for g in refs for px, py in chips]

    return _SplitCopies(name, gathered, plan, 3 * len(gathered))


def _scatter_sibling(name, grads):
    n = len(grads)

    def plan(refs):
        x, y, c, _ = _place()
        return [(refs[w].at[1 - c], refs[n + w], refs[n + w], (x, y, 1 - c)) for w in range(n)]

    landing = [lax.empty(g.shape[1:], g.dtype) for g in grads]
    return _SplitCopies(name, [*grads, *landing], plan, n)


def _scatter_chips(name, sums):
    n = len(sums)

    def plan(refs):
        x, y, c, chips = _place()
        return [(refs[w].at[2 * px + py], refs[n + w].at[j], refs[n + w].at[j], (px, py, c))
                for w in range(n) for j, (px, py) in enumerate(chips)]

    landing = [lax.empty((3, *s.shape[1:]), s.dtype) for s in sums]
    return _SplitCopies(name, [*sums, *landing], plan, 3 * n)


def _share_final(name, finals):
    def plan(refs):
        x, y, c, _ = _place()
        return [(f.at[c], f.at[c], f.at[1 - c], (x, y, 1 - c)) for f in refs]

    return _SplitCopies(name, finals, plan, len(finals))


def _row_tile(rows, cols, budget_elems=393216):
    best = 8
    for t in range(8, rows + 1, 8):
        if rows % t == 0 and t * cols <= budget_elems:
            best = t
    return best if rows % best == 0 else rows


def _sum_with_sibling(name, grad, recv, qc_idx):
    _, _, h, cols = grad.shape
    tr = _row_tile(h, cols)

    def body(s_ref, g_ref, r_ref, own_ref, pb_ref):
        p = g_ref[...] + r_ref[...]
        pb_ref[...] = p.astype(BF16)

        @pl.when(pl.program_id(1) == s_ref[0])
        def _():
            own_ref[...] = p

    blk = pl.BlockSpec((None, tr, cols), lambda r, k, s: (k, r, 0))
    return _pcall(
        body, name=name, out_shape=[_sds((h, cols), F32), _sds((N_CHIPS, h, cols), BF16)],
        grid=(h // tr, N_CHIPS), prefetch=1,
        in_specs=[pl.BlockSpec((None, None, tr, cols), lambda r, k, s: (s[1], k, r, 0)), blk],
        out_specs=[pl.BlockSpec((tr, cols), lambda r, k, s: (r, 0)), blk], vmem_mb=32,
    )(qc_idx, grad, recv)


def _sum_chips(name, own, recv, qc_idx):
    h, cols = own.shape
    tr = _row_tile(h, cols)

    def body(s_ref, p_ref, t_ref, o_ref):
        o_ref[...] = ((p_ref[...] + t_ref[0].astype(F32)) + t_ref[1].astype(F32)) + t_ref[2].astype(F32)

    return _pcall(
        body, name=name, out_shape=_sds((2, h, cols), F32), grid=(h // tr,), prefetch=1,
        in_specs=[pl.BlockSpec((tr, cols), lambda r, s: (r, 0)),
                  pl.BlockSpec((3, tr, cols), lambda r, s: (0, r, 0))],
        out_specs=pl.BlockSpec((None, tr, cols), lambda r, s: (s[1], r, 0)), vmem_mb=32,
    )(qc_idx, own, recv)


class _ReduceScatter:
    def __init__(self, tag, names, grads, qc_idx):
        self.tag, self.names, self.n, self.qc_idx = tag, names, len(grads), qc_idx
        self.copies = _scatter_sibling(f"{tag}_rs_sibling", grads)

    def step2(self):
        n = self.n
        arrs = self.copies.wait()
        sums = [_sum_with_sibling(f"{nm}_sum_sibling", arrs[w], arrs[n + w], self.qc_idx)
                for w, nm in enumerate(self.names)]
        self.own = [s[0] for s in sums]
        self.copies = _scatter_chips(f"{self.tag}_rs_chips", [s[1] for s in sums])

    def step3(self):
        n = self.n
        arrs = self.copies.wait()
        finals = [_sum_chips(f"{nm}_sum_chips", self.own[w], arrs[n + w], self.qc_idx)
                  for w, nm in enumerate(self.names)]
        self.copies = _share_final(f"{self.tag}_rs_final", finals)

    def result(self):
        return {nm: f.reshape(2 * f.shape[1], f.shape[2]) for nm, f in zip(self.names, self.copies.wait())}


def _cast_into_gathered(name, w, q_idx):
    rows, cols = w.shape
    h = rows // 2
    tr = _row_tile(h, cols, 1 << 20)
    nr = h // tr

    def body(s_ref, w_ref, o_ref):
        o_ref[...] = w_ref[...].astype(BF16)

    return _pcall(body, name=name, out_shape=_sds((N_CHIPS, 2, h, cols), BF16), grid=(2, nr), prefetch=1,
                  in_specs=[pl.BlockSpec((tr, cols), lambda hf, r, s: (hf * nr + r, 0))],
                  out_specs=pl.BlockSpec((None, None, tr, cols), lambda hf, r, s: (s[0], hf, r, 0)),
                  vmem_mb=32)(q_idx, w)


def _regroup(name, w, n_groups):
    n_chips, rows, goq = w.shape
    gi = rows // n_groups

    def body(w_ref, o_ref):
        o_ref[...] = w_ref[...]

    return _pcall(body, name=name, out_shape=_sds((n_groups, gi, n_chips * goq), w.dtype), grid=(n_groups, n_chips),
                  in_specs=[pl.BlockSpec((None, gi, goq), lambda g, k: (k, g, 0))],
                  out_specs=pl.BlockSpec((None, gi, goq), lambda g, k: (g, 0, k)), vmem_mb=32)(w)


def _rms(h):
    r = lax.rsqrt(jnp.mean(h * h, axis=-1, keepdims=True) + EPS)
    return r, h * r


def _norm_mod(name, h, g, sc, sh, ts):
    s_len, d = h.shape

    def body(h_ref, g_ref, sc_ref, sh_ref, n_ref):
        _, xhat = _rms(h_ref[...])
        n_ref[...] = ((xhat * g_ref[...]) * (1.0 + sc_ref[...]) + sh_ref[...]).astype(BF16)

    row = pl.BlockSpec((ts, d), lambda i: (i, 0))
    return _pcall(body, name=name, out_shape=_sds((s_len, d), BF16), grid=(s_len // ts,),
                  in_specs=[row, _vec(d), _vec(d), _vec(d)], out_specs=row, vmem_mb=32)(h, g, sc, sh)


def _residual_norm_mod(name, h, f, gate, cmul, g, sc, sh, ts):
    s_len, d = h.shape

    def body(h_ref, f_ref, gt_ref, g_ref, sc_ref, sh_ref, ho_ref, n_ref):
        hn = h_ref[...] + (cmul * gt_ref[...]) * f_ref[...]
        ho_ref[...] = hn
        _, xhat = _rms(hn)
        n_ref[...] = ((xhat * g_ref[...]) * (1.0 + sc_ref[...]) + sh_ref[...]).astype(BF16)

    row = pl.BlockSpec((ts, d), lambda i: (i, 0))
    return _pcall(body, name=name, out_shape=[_sds((s_len, d), F32), _sds((s_len, d), BF16)],
                  grid=(s_len // ts,), in_specs=[row, row, _vec(d), _vec(d), _vec(d), _vec(d)],
                  out_specs=[row, row], vmem_mb=32)(h, f, gate, g, sc, sh)


def _final_loss(name, h, f, tgt, gate, cmul, g, ts):
    s_len, d = h.shape

    def body(h_ref, f_ref, t_ref, gt_ref, g_ref, dh_ref, df_ref, dg_ref, dgt_ref, loss_ref):
        i = pl.program_id(0)
        fv = f_ref[...]
        coef = cmul * gt_ref[...]
        hn = h_ref[...] + coef * fv
        r, xhat = _rms(hn)
        err = xhat * g_ref[...] - t_ref[...]
        _acc_rows(loss_ref, (0.5 / d) * (err * err), i)
        dy = err * (1.0 / d)
        _acc_rows(dg_ref, dy * xhat, i)
        dxhat = dy * g_ref[...]
        dh = r * (dxhat - xhat * jnp.mean(dxhat * xhat, axis=-1, keepdims=True))
        dh_ref[...] = dh
        _acc_rows(dgt_ref, cmul * (dh * fv), i)
        df_ref[...] = (coef * dh).astype(BF16)

    row = pl.BlockSpec((ts, d), lambda i: (i, 0))
    return _pcall(body, name=name,
                  out_shape=[_sds((s_len, d), F32), _sds((s_len, d), BF16)] + [_sds((1, d), F32)] * 3,
                  grid=(s_len // ts,), in_specs=[row, row, row, _vec(d), _vec(d)],
                  out_specs=[row, row, _vec(d), _vec(d), _vec(d)], vmem_mb=40)(h, f, tgt, gate, g)


def _norm_mod_bwd(name, h, dn, dh_next, g, sc, ts, prev=None):
    s_len, d = h.shape
    has_prev = prev is not None
    cmul = prev[2] if has_prev else None

    def body(*refs):
        if has_prev:
            h_ref, dn_ref, dhn_ref, f_ref, g_ref, sc_ref, gt_ref, dh_ref, df_ref, dsh_ref, dsc_ref, dg_ref, dgt_ref = refs
        else:
            h_ref, dn_ref, dhn_ref, g_ref, sc_ref, dh_ref, dsh_ref, dsc_ref, dg_ref = refs
        i = pl.program_id(0)
        r, xhat = _rms(h_ref[...])
        dn_v = dn_ref[...]
        gv = g_ref[...]
        _acc_rows(dsh_ref, dn_v, i)
        _acc_rows(dsc_ref, dn_v * (xhat * gv), i)
        dnrm = dn_v * (1.0 + sc_ref[...])
        _acc_rows(dg_ref, dnrm * xhat, i)
        dxhat = dnrm * gv
        dh = dhn_ref[...] + r * (dxhat - xhat * jnp.mean(dxhat * xhat, axis=-1, keepdims=True))
        dh_ref[...] = dh
        if has_prev:
            _acc_rows(dgt_ref, cmul * (dh * f_ref[...]), i)
            df_ref[...] = ((cmul * gt_ref[...]) * dh).astype(BF16)

    row = pl.BlockSpec((ts, d), lambda i: (i, 0))
    if has_prev:
        ins, in_specs = [h, dn, dh_next, prev[0], g, sc, prev[1]], [row, row, row, row, _vec(d), _vec(d), _vec(d)]
        out_shape = [_sds((s_len, d), F32), _sds((s_len, d), BF16)] + [_sds((1, d), F32)] * 4
        out_specs = [row, row] + [_vec(d)] * 4
    else:
        ins, in_specs = [h, dn, dh_next, g, sc], [row, row, row, _vec(d), _vec(d)]
        out_shape = [_sds((s_len, d), F32)] + [_sds((1, d), F32)] * 3
        out_specs = [row] + [_vec(d)] * 3
    return _pcall(body, name=name, out_shape=out_shape, grid=(s_len // ts,), in_specs=in_specs,
                  out_specs=out_specs, vmem_mb=40)(*ins)


def _cols(ref, lo, hi, npc, rows=slice(None)):
    parts = []
    while lo < hi:
        q, o = divmod(lo, npc)
        n = min(hi - lo, npc - o)
        parts.append(ref[q, rows, o:o + n].astype(F32))
        lo += n
    return parts[0] if len(parts) == 1 else jnp.concatenate(parts, axis=-1)


def _store_cols(ref, lo, val, npc, rows=slice(None)):
    off, width = 0, val.shape[-1]
    while off < width:
        q, o = divmod(lo + off, npc)
        n = min(width - off, npc - o)
        ref[q, rows, o:o + n] = val[:, off:off + n]
        off += n


def _chips_covering(cols, npc):
    return -(-cols // npc)


SUBLANES = 8
ROW_CHUNK = 32


def _make_phases(src_ref, ph_ref):
    rows = src_ref.shape[0] - SUBLANES
    for b in range(1, SUBLANES):
        ph_ref[b - 1] = src_ref[pl.ds(b, rows), :]


def _window(src_ref, ph_ref, off, r0, cols=slice(None)):
    a, b = divmod(off, SUBLANES)
    start = pl.multiple_of(r0 + SUBLANES * a, SUBLANES)
    if b == 0:
        return src_ref[pl.ds(start, ROW_CHUNK), cols]
    return ph_ref[b - 1, pl.ds(start, ROW_CHUNK), cols]


def _phase_scratch(rows, width):
    return pltpu.VMEM((SUBLANES - 1, rows - SUBLANES, width), F32)


def _conv(a0s_ref, a0p_ref, cw_ref, cb_ref, r0):
    a1 = cb_ref[...] + cw_ref[0:1, :] * _window(a0s_ref, a0p_ref, HALO - CONV_K + 1, r0)
    for k in range(1, CONV_K):
        a1 = a1 + cw_ref[k:k + 1, :] * _window(a0s_ref, a0p_ref, HALO - CONV_K + 1 + k, r0)
    return a1


def _layer_norm(a1, lg_ref, lb_ref):
    mu = jnp.mean(a1, axis=-1, keepdims=True)
    ctr = a1 - mu
    rstd = lax.rsqrt(jnp.mean(ctr * ctr, axis=-1, keepdims=True) + EPS)
    xh = ctr * rstd
    return xh, rstd, xh * lg_ref[...] + lb_ref[...]


def _for_chunks(ts, fn):
    def step(ci, carry):
        fn(pl.multiple_of(ci * ROW_CHUNK, ROW_CHUNK))
        return carry

    lax.fori_loop(0, ts // ROW_CHUNK, step, 0)


def _stage_glu(p_ref, ph_ref, a0s_ref, i, wc, npc, ts):
    a0 = _cols(p_ref, 0, wc, npc) * _sigmoid(_cols(p_ref, wc, 2 * wc, npc))
    a0h = _cols(ph_ref, 0, wc, npc) * _sigmoid(_cols(ph_ref, wc, 2 * wc, npc))
    a0s_ref[0:HALO, :] = jnp.where(i > 0, a0h, 0.0)
    a0s_ref[HALO:HALO + ts, :] = a0


def _mixer_mid(name, proj, cw, cb, lg, lb, wc, wp, ts):
    _, s_len, npc = proj.shape
    nq = _chips_covering(2 * wc + wp, npc)
    gi = wp // len(POOL_WINDOWS)
    hb = ts // HALO

    def body(p_ref, ph_ref, cw_ref, cb_ref, lg_ref, lb_ref, a3_ref, mx_ref, a1_ref, a0s_ref, vs_ref, a0p_ref,
             vp_ref):
        i = pl.program_id(0)
        _stage_glu(p_ref, ph_ref, a0s_ref, i, wc, npc, ts)
        vs_ref[0:HALO, :] = jnp.where(i > 0, _cols(ph_ref, 2 * wc, 2 * wc + wp, npc), 0.0)
        vs_ref[HALO:HALO + ts, :] = _cols(p_ref, 2 * wc, 2 * wc + wp, npc)
        _make_phases(a0s_ref, a0p_ref)
        _make_phases(vs_ref, vp_ref)

        def chunk(r0):
            rows = pl.ds(r0, ROW_CHUNK)
            a1 = _conv(a0s_ref, a0p_ref, cw_ref, cb_ref, r0)
            a1_ref[rows, :] = a1
            _, _, a2 = _layer_norm(a1, lg_ref, lb_ref)
            a3_ref[rows, :] = (a2 * _sigmoid(a2)).astype(BF16)
            t_abs = i * ts + r0 + lax.broadcasted_iota(jnp.int32, (ROW_CHUNK, 1), 0)
            for g, win in enumerate(POOL_WINDOWS):
                cs = slice(g * gi, (g + 1) * gi)
                v_now = _window(vs_ref, vp_ref, HALO, r0, cs)
                acc = v_now
                for dlt in range(1, win):
                    acc = acc + _window(vs_ref, vp_ref, HALO - dlt, r0, cs)
                cnt = jnp.minimum(t_abs + 1, win).astype(F32)
                mx_ref[rows, cs] = (acc / cnt - v_now).astype(BF16)

        _for_chunks(ts, chunk)

    return _pcall(
        body, name=name, out_shape=[_sds((s_len, wc), BF16), _sds((s_len, wp), BF16), _sds((s_len, wc), F32)],
        grid=(s_len // ts,),
        in_specs=[pl.BlockSpec((nq, ts, npc), lambda i: (0, i, 0)),
                  pl.BlockSpec((nq, HALO, npc), lambda i: (0, jnp.maximum(i * hb - 1, 0), 0)),
                  pl.BlockSpec((HALO, wc), lambda i: (0, 0)), _vec(wc), _vec(wc), _vec(wc)],
        out_specs=[pl.BlockSpec((ts, wc), lambda i: (i, 0)), pl.BlockSpec((ts, wp), lambda i: (i, 0)),
                   pl.BlockSpec((ts, wc), lambda i: (i, 0))],
        scratch=[pltpu.VMEM((HALO + ts, wc), F32), pltpu.VMEM((HALO + ts, wp), F32),
                 _phase_scratch(HALO + ts, wc), _phase_scratch(HALO + ts, wp)], vmem_mb=56,
    )(proj, proj, cw, cb, lg, lb)


def _gates_fwd(name, proj, ya, yb, b_a, b_b, ls, wc, wp, ts):
    _, s_len, npc = proj.shape
    d = ya.shape[1]
    g0 = 2 * wc + wp

    def body(p_ref, ya_ref, yb_ref, ba_ref, bb_ref, ls_ref, z_ref):
        ga = _sigmoid(_cols(p_ref, g0, g0 + d, npc))
        gb = _sigmoid(_cols(p_ref, g0 + d, g0 + 2 * d, npc))
        z = ga * (ya_ref[...] + ba_ref[...]) + gb * ((yb_ref[...] + bb_ref[...]) * ls_ref[...])
        z_ref[...] = z.astype(BF16)

    row = pl.BlockSpec((ts, d), lambda i: (i, 0))
    return _pcall(body, name=name, out_shape=_sds((s_len, d), BF16), grid=(s_len // ts,),
                  in_specs=[pl.BlockSpec((N_CHIPS, ts, npc), lambda i: (0, i, 0)), row, row, _vec(d), _vec(d), _vec(d)],
                  out_specs=row, vmem_mb=48)(proj, ya, yb, b_a, b_b, ls)


def _gates_bwd(name, proj, dz, ya, yb, b_a, b_b, ls, wc, wp, ts):
    _, s_len, npc = proj.shape
    d = ya.shape[1]
    g0 = 2 * wc + wp

    def body(p_ref, dz_ref, ya_ref, yb_ref, ba_ref, bb_ref, ls_ref, dya_ref, dyb_ref, dgt_ref, dba_ref, dls_ref,
             dbb_ref):
        i = pl.program_id(0)
        ga = _sigmoid(_cols(p_ref, g0, g0 + d, npc))
        gb = _sigmoid(_cols(p_ref, g0 + d, g0 + 2 * d, npc))
        dz_v = dz_ref[...]
        y_a = ya_ref[...] + ba_ref[...]
        y_b0 = yb_ref[...] + bb_ref[...]
        ls_v = ls_ref[...]
        dya = dz_v * ga
        dya_ref[...] = dya.astype(BF16)
        _acc_rows(dba_ref, dya, i)
        t = dz_v * gb
        _acc_rows(dls_ref, t * y_b0, i)
        dyb = t * ls_v
        dyb_ref[...] = dyb.astype(BF16)
        _acc_rows(dbb_ref, dyb, i)
        dgt_ref[:, 0:d] = (dz_v * y_a * ga * (1.0 - ga)).astype(BF16)
        dgt_ref[:, d:2 * d] = (dz_v * (y_b0 * ls_v) * gb * (1.0 - gb)).astype(BF16)

    row = pl.BlockSpec((ts, d), lambda i: (i, 0))
    return _pcall(
        body, name=name,
        out_shape=[_sds((s_len, d), BF16), _sds((s_len, d), BF16), _sds((s_len, 2 * d), BF16)] + [_sds((1, d), F32)] * 3,
        grid=(s_len // ts,),
        in_specs=[pl.BlockSpec((N_CHIPS, ts, npc), lambda i: (0, i, 0)), row, row, row, _vec(d), _vec(d), _vec(d)],
        out_specs=[row, row, pl.BlockSpec((ts, 2 * d), lambda i: (i, 0))] + [_vec(d)] * 3, vmem_mb=48,
    )(proj, dz, ya, yb, b_a, b_b, ls)


def _conv_branch_bwd(name, proj, a1, da3, lg, lb, wc, wp, ts):
    _, s_len, npc = proj.shape
    nq = _chips_covering(2 * wc, npc)
    hb = ts // HALO

    n_tiles = s_len // ts

    def fold(v):
        return jnp.sum(v.reshape(ROW_CHUNK // SUBLANES, SUBLANES, v.shape[-1]), axis=0)

    def body(p_ref, ph_ref, a1_ref, da3_ref, lg_ref, lb_ref, da1_ref, dlg_ref, dlb_ref, dcb_ref, dcw_ref,
             a0s_ref, a0p_ref, vec8_ref, dcw8_ref):
        i = pl.program_id(0)
        _stage_glu(p_ref, ph_ref, a0s_ref, i, wc, npc, ts)
        _make_phases(a0s_ref, a0p_ref)

        @pl.when(i == 0)
        def _():
            vec8_ref[...] = jnp.zeros_like(vec8_ref)
            dcw8_ref[...] = jnp.zeros_like(dcw8_ref)

        def chunk(r0):
            rows = pl.ds(r0, ROW_CHUNK)
            xh, rstd, a2 = _layer_norm(a1_ref[rows, :], lg_ref, lb_ref)
            sig = _sigmoid(a2)
            da2 = da3_ref[rows, :] * (sig * (1.0 + a2 * (1.0 - sig)))
            vec8_ref[0] += fold(da2 * xh)
            vec8_ref[1] += fold(da2)
            dxh = da2 * lg_ref[...]
            da1 = rstd * (dxh - jnp.mean(dxh, axis=-1, keepdims=True)
                          - xh * jnp.mean(dxh * xh, axis=-1, keepdims=True))
            da1_ref[rows, :] = da1
            vec8_ref[2] += fold(da1)
            for k in range(CONV_K):
                dcw8_ref[k] += fold(da1 * _window(a0s_ref, a0p_ref, HALO - CONV_K + 1 + k, r0))

        _for_chunks(ts, chunk)

        @pl.when(i == n_tiles - 1)
        def _():
            dlg_ref[...] = jnp.sum(vec8_ref[0], axis=0, keepdims=True)
            dlb_ref[...] = jnp.sum(vec8_ref[1], axis=0, keepdims=True)
            dcb_ref[...] = jnp.sum(vec8_ref[2], axis=0, keepdims=True)
            dcw_ref[...] = jnp.sum(dcw8_ref[...], axis=1)

    return _pcall(
        body, name=name,
        out_shape=[_sds((s_len, wc), F32)] + [_sds((1, wc), F32)] * 3 + [_sds((HALO, wc), F32)],
        grid=(s_len // ts,),
        in_specs=[pl.BlockSpec((nq, ts, npc), lambda i: (0, i, 0)),
                  pl.BlockSpec((nq, HALO, npc), lambda i: (0, jnp.maximum(i * hb - 1, 0), 0)),
                  pl.BlockSpec((ts, wc), lambda i: (i, 0)), pl.BlockSpec((ts, wc), lambda i: (i, 0)),
                  _vec(wc), _vec(wc)],
        out_specs=[pl.BlockSpec((ts, wc), lambda i: (i, 0)), _vec(wc), _vec(wc), _vec(wc),
                   pl.BlockSpec((HALO, wc), lambda i: (0, 0))],
        scratch=[pltpu.VMEM((HALO + ts, wc), F32), _phase_scratch(HALO + ts, wc),
                 pltpu.VMEM((3, SUBLANES, wc), F32), pltpu.VMEM((HALO, SUBLANES, wc), F32)], vmem_mb=56,
    )(proj, proj, a1, da3, lg, lb)


def _mixer_in_bwd(name, proj, da1, dmixed, dgates, cw, wc, wp, ts):
    _, s_len, npc = proj.shape
    nq = _chips_covering(2 * wc, npc)
    gi = wp // len(POOL_WINDOWS)
    hb = ts // HALO
    n_tiles = s_len // ts
    last_hb = s_len // HALO - 1
    d2 = dgates.shape[1]

    def body(p_ref, d1_ref, d1n_ref, dm_ref, dmn_ref, dgt_ref, cw_ref, o_ref, d1s_ref, es_ref, d1p_ref, ep_ref):
        i = pl.program_id(0)
        more = i < n_tiles - 1
        d1s_ref[0:ts, :] = d1_ref[...]
        d1s_ref[ts:ts + HALO, :] = jnp.where(more, d1n_ref[...], 0.0)
        t_abs = i * ts + lax.broadcasted_iota(jnp.int32, (ts + HALO, 1), 0)
        dm_ext = jnp.concatenate([dm_ref[...], jnp.where(more, dmn_ref[...], 0.0)], axis=0)
        for g, win in enumerate(POOL_WINDOWS):
            cs = slice(g * gi, (g + 1) * gi)
            es_ref[:, cs] = dm_ext[:, cs] / jnp.minimum(t_abs + 1, win).astype(F32)
        _make_phases(d1s_ref, d1p_ref)
        _make_phases(es_ref, ep_ref)

        def chunk(r0):
            rows = pl.ds(r0, ROW_CHUNK)
            da0 = cw_ref[0:1, :] * _window(d1s_ref, d1p_ref, CONV_K - 1, r0)
            for k in range(1, CONV_K):
                da0 = da0 + cw_ref[k:k + 1, :] * _window(d1s_ref, d1p_ref, CONV_K - 1 - k, r0)
            glu_a = _cols(p_ref, 0, wc, npc, rows)
            sig = _sigmoid(_cols(p_ref, wc, 2 * wc, npc, rows))
            _store_cols(o_ref, 0, (da0 * sig).astype(BF16), npc, rows)
            _store_cols(o_ref, wc, (da0 * glu_a * sig * (1.0 - sig)).astype(BF16), npc, rows)
            parts = []
            for g, win in enumerate(POOL_WINDOWS):
                cs = slice(g * gi, (g + 1) * gi)
                acc = _window(es_ref, ep_ref, 0, r0, cs)
                for dlt in range(1, win):
                    acc = acc + _window(es_ref, ep_ref, dlt, r0, cs)
                parts.append(acc - dm_ref[rows, cs])
            _store_cols(o_ref, 2 * wc, jnp.concatenate(parts, axis=-1).astype(BF16), npc, rows)

        _for_chunks(ts, chunk)
        _store_cols(o_ref, 2 * wc + wp, dgt_ref[...], npc)

    nxt = lambda i: (jnp.minimum((i + 1) * hb, last_hb), 0)
    return _pcall(
        body, name=name, out_shape=_sds((N_CHIPS, s_len, npc), BF16), grid=(n_tiles,),
        in_specs=[pl.BlockSpec((nq, ts, npc), lambda i: (0, i, 0)),
                  pl.BlockSpec((ts, wc), lambda i: (i, 0)), pl.BlockSpec((HALO, wc), nxt),
                  pl.BlockSpec((ts, wp), lambda i: (i, 0)), pl.BlockSpec((HALO, wp), nxt),
                  pl.BlockSpec((ts, d2), lambda i: (i, 0)),
                  pl.BlockSpec((HALO, wc), lambda i: (0, 0))],
        out_specs=pl.BlockSpec((N_CHIPS, ts, npc), lambda i: (0, i, 0)),
        scratch=[pltpu.VMEM((ts + HALO, wc), F32), pltpu.VMEM((ts + HALO, wp), F32),
                 _phase_scratch(ts + HALO, wc), _phase_scratch(ts + HALO, wp)], vmem_mb=56,
    )(proj, da1, da1, dmixed, dmixed, dgates, cw)


def _ada_fwd(name, c_all, w, b):
    d, cols = w.shape
    tn = 512 if cols % 512 == 0 else cols

    def body(c_ref, w_ref, b_ref, o_ref):
        cv = c_ref[...]
        sc = (cv * _sigmoid(cv)).astype(BF16)
        o_ref[...] = jnp.dot(sc, w_ref[...].astype(BF16), preferred_element_type=F32) + b_ref[...]

    return _pcall(body, name=name, out_shape=_sds((N_DEV, cols), F32), grid=(cols // tn,),
                  in_specs=[pl.BlockSpec((N_DEV, d), lambda j: (0, 0)), pl.BlockSpec((d, tn), lambda j: (0, j)),
                            pl.BlockSpec((1, tn), lambda j: (0, j))],
                  out_specs=pl.BlockSpec((N_DEV, tn), lambda j: (0, j)), vmem_mb=32)(c_all, w, b)


def _adam_math(w, g, m, v):
    m_new = ADAM_B1 * m + (1.0 - ADAM_B1) * g
    v_new = ADAM_B2 * v + (1.0 - ADAM_B2) * (g * g)
    m_hat = m_new / (1.0 - ADAM_B1 ** ADAM_STEP)
    v_hat = v_new / (1.0 - ADAM_B2 ** ADAM_STEP)
    delta = -ADAM_LR * (m_hat / (jnp.sqrt(v_hat) + ADAM_EPS) + ADAM_WD * w)
    return delta, m_new, v_new


def _adamw(name, w, g, m, v):
    rows, cols = w.shape
    tr = _row_tile(rows, cols, 524288)

    def body(w_ref, g_ref, m_ref, v_ref, go_ref, d_ref, mo_ref, vo_ref):
        g = g_ref[...]
        go_ref[...] = g
        d_ref[...], mo_ref[...], vo_ref[...] = _adam_math(w_ref[...], g, m_ref[...], v_ref[...])

    spec = pl.BlockSpec((tr, cols), lambda i: (i, 0))
    return _pcall(body, name=name, out_shape=[_sds(w.shape, F32)] * 4, grid=(rows // tr,), in_specs=[spec] * 4,
                  out_specs=[spec] * 4, vmem_mb=40)(w, g, m, v)


def _ada_grad_adamw(name, c_t, d_ada, w, m, v):
    rows, cols = w.shape
    tr = _tile(rows, 256)
    tc = _tile(cols, 1536) if cols % 1536 == 0 else cols

    def body(c_ref, da_ref, w_ref, m_ref, v_ref, g_ref, d_ref, mo_ref, vo_ref):
        cv = c_ref[...]
        sc = cv * _sigmoid(cv)
        g = sc[:, 0:1] * da_ref[0:1, :]
        for b in range(1, N_DEV):
            g = g + sc[:, b:b + 1] * da_ref[b:b + 1, :]
        g_ref[...] = g
        d_ref[...], mo_ref[...], vo_ref[...] = _adam_math(w_ref[...], g, m_ref[...], v_ref[...])

    spec = pl.BlockSpec((tr, tc), lambda i, j: (i, j))
    return _pcall(body, name=name, out_shape=[_sds(w.shape, F32)] * 4, grid=(rows // tr, cols // tc),
                  in_specs=[pl.BlockSpec((tr, N_DEV), lambda i, j: (i, 0)),
                            pl.BlockSpec((N_DEV, tc), lambda i, j: (0, j)), spec, spec, spec],
                  out_specs=[spec] * 4, vmem_mb=40)(c_t, d_ada, w, m, v)


def _sum_devices(name, gathered, m_per):
    n = gathered.shape[1]

    def body(g_ref, o_ref):
        acc = g_ref[0:m_per, :]
        for dev in range(1, N_DEV):
            acc = acc + g_ref[dev * m_per:(dev + 1) * m_per, :]
        o_ref[...] = acc

    return _pcall(body, name=name, out_shape=_sds((m_per, n), F32),
                  in_specs=[pl.BlockSpec(memory_space=pltpu.VMEM)],
                  out_specs=pl.BlockSpec(memory_space=pltpu.VMEM))(gathered)


def _ffn_fwd(tag, n, w_in_parts, w_out_after_swiglu, dims):
    s_len, d, f_dim = dims["S"], dims["D"], dims["F"]
    tf = f_dim // 4
    tm0, tm = _tile(s_len, 512), _tile(s_len, 1024)
    n_parts = len(w_in_parts)
    nbp = (f_dim // 2) // tf
    nbq = nbp // n_parts

    def ep(accs, ex, outs):
        hh, uu = accs
        sig = _sigmoid(hh)
        silu = hh * sig
        outs[0][0] = (uu * (sig + silu * (1.0 - sig))).astype(BF16)
        outs[0][1] = silu.astype(BF16)
        outs[1][...] = (silu * uu).astype(BF16)

    done = ()
    for part, get_w in enumerate(w_in_parts):
        w_g = get_w()
        col = lambda j, part=part: (j // nbq) * nbp + part * nbq + j % nbq
        done = _matmul(
            f"{tag}_swiglu{part}", n, [w_g, w_g], mode="nn", grid=(s_len // tm0, 2 * nbq, 1),
            a_spec=pl.BlockSpec((tm0, d), lambda i, j, k: (i, 0)),
            b_specs=[pl.BlockSpec((None, d, tf), lambda i, j, k, part=part: (j // nbq, 0, part * nbq + j % nbq)),
                     pl.BlockSpec((None, d, tf), lambda i, j, k, part=part: (2 + j // nbq, 0, part * nbq + j % nbq))],
            out_shape=[_sds((2, s_len, f_dim), BF16), _sds((s_len, f_dim), BF16)],
            out_specs=[pl.BlockSpec((2, tm0, tf), lambda i, j, k, col=col: (0, i, col(j))),
                       pl.BlockSpec((tm0, tf), lambda i, j, k, col=col: (i, col(j)))],
            acc_shape=(tm0, tf), epilogue=ep, carry=done)
    hu, act = done
    w_out2d = w_out_after_swiglu()
    tn2 = _tile(d, 1024)
    f = _matmul(
        f"{tag}_down", act, [w_out2d], mode="nn", grid=(s_len // tm, d // tn2, 2),
        a_spec=pl.BlockSpec((tm, 2 * tf), lambda i, j, k: (i, k)),
        b_specs=[pl.BlockSpec((2 * tf, tn2), lambda i, j, k: (k, j))],
        out_shape=_sds((s_len, d), F32), out_specs=pl.BlockSpec((tm, tn2), lambda i, j, k: (i, j)),
        acc_shape=(tm, tn2), epilogue=_ep_store(F32))
    return hu, act, f, w_out2d


def _ffn_bwd(tag, n, hu, act, df, w_in_g, w_out2d, dims, after_dw_out, after_dw_in):
    s_len, d, f_dim = dims["S"], dims["D"], dims["F"]
    tf = f_dim // 4
    tk = _tile(s_len, 2048)
    tn = _tile(d, 1024)
    g_out = _matmul(
        f"{tag}_dw_out", act, [df], mode="tn", grid=(4, d // tn, s_len // tk),
        a_spec=pl.BlockSpec((tk, tf), lambda i, j, k: (k, i)),
        b_specs=[pl.BlockSpec((tk, tn), lambda i, j, k: (k, j))],
        out_shape=_sds((2, 4, tf // 2, d), F32),
        out_specs=pl.BlockSpec((2, None, tf // 2, tn), lambda i, j, k: (0, i, 0, j)),
        acc_shape=(tf, tn), epilogue=_ep_halves(tf // 2))
    after_dw_out(g_out)

    def ep_dhu(accs, ex, outs):
        da = accs[0]
        outs[0][0] = (da * ex[0][0].astype(F32)).astype(BF16)
        outs[0][1] = (da * ex[0][1].astype(F32)).astype(BF16)

    tm = _tile(s_len, 512)
    hu_spec = pl.BlockSpec((2, tm, tf), lambda i, j, k: (0, i, j))
    dhu = _matmul(
        f"{tag}_dhu", df, [w_out2d], mode="nt", grid=(s_len // tm, 4, 1),
        a_spec=pl.BlockSpec((tm, d), lambda i, j, k: (i, 0)),
        b_specs=[pl.BlockSpec((tf, d), lambda i, j, k: (j, 0))],
        extras=[hu], extra_specs=[hu_spec],
        out_shape=_sds((2, s_len, f_dim), BF16), out_specs=hu_spec, acc_shape=(tm, tf), epilogue=ep_dhu)

    hd = d // 2
    g_in = _matmul(
        f"{tag}_dw_in", n, [dhu], mode="tn", grid=(2, 8, s_len // tk),
        a_spec=pl.BlockSpec((tk, hd), lambda i, j, k: (k, i)),
        b_specs=[pl.BlockSpec((None, tk, tf), lambda i, j, k: (j // 4, k, j % 4))],
        out_shape=_sds((2, 4, hd, f_dim // 2), F32),
        out_specs=pl.BlockSpec((None, None, hd, tf), lambda i, j, k: (i, j // 2, 0, j % 2)),
        acc_shape=(hd, tf), epilogue=_ep_store(F32))
    after_dw_in(g_in)

    tm2 = _tile(s_len, 1024)
    dn = _matmul(
        f"{tag}_dn", dhu, [w_in_g], mode="nt", grid=(s_len // tm2, d // tn, N_CHIPS),
        a_spec=pl.BlockSpec((None, tm2, 2 * tf), lambda i, j, k: (k // 2, i, k % 2)),
        b_specs=[pl.BlockSpec((None, tn, 2 * tf), lambda i, j, k: (k, j, 0))],
        out_shape=_sds((s_len, d), F32), out_specs=pl.BlockSpec((tm2, tn), lambda i, j, k: (i, j)),
        acc_shape=(tm2, tn), epilogue=_ep_store(F32))
    return dn


def kernel(x, c, w_ada, b_ada, g_ffn1, w1_in, w1_out, g_mix, w_in, conv_w, conv_b, ln_a_g, ln_a_b, w_a_out, b_a_out, w_b_group, b_b_group, ls_b, w_out, g_ffn2, w2_in, w2_out, g_final, loss_target, m_w_ada, m_b_ada, m_g_ffn1, m_w1_in, m_w1_out, m_g_mix, m_w_in, m_conv_w, m_conv_b, m_ln_a_g, m_ln_a_b, m_w_a_out, m_b_a_out, m_w_b_group, m_b_b_group, m_ls_b, m_w_out, m_g_ffn2, m_w2_in, m_w2_out, m_g_final, v_w_ada, v_b_ada, v_g_ffn1, v_w1_in, v_w1_out, v_g_mix, v_w_in, v_conv_w, v_conv_b, v_ln_a_g, v_ln_a_b, v_w_a_out, v_b_a_out, v_w_b_group, v_b_b_group, v_ls_b, v_w_out, v_g_ffn2, v_w2_in, v_w2_out, v_g_final):
    weights = dict(w_ada=w_ada, b_ada=b_ada, g_ffn1=g_ffn1, w1_in=w1_in, w1_out=w1_out, g_mix=g_mix, w_in=w_in,
                   conv_w=conv_w, conv_b=conv_b, ln_a_g=ln_a_g, ln_a_b=ln_a_b, w_a_out=w_a_out, b_a_out=b_a_out,
                   w_b_group=w_b_group, b_b_group=b_b_group, ls_b=ls_b, w_out=w_out, g_ffn2=g_ffn2, w2_in=w2_in,
                   w2_out=w2_out, g_final=g_final)
    mom1 = dict(w_ada=m_w_ada, b_ada=m_b_ada, g_ffn1=m_g_ffn1, w1_in=m_w1_in, w1_out=m_w1_out, g_mix=m_g_mix,
                w_in=m_w_in, conv_w=m_conv_w, conv_b=m_conv_b, ln_a_g=m_ln_a_g, ln_a_b=m_ln_a_b, w_a_out=m_w_a_out,
                b_a_out=m_b_a_out, w_b_group=m_w_b_group, b_b_group=m_b_b_group, ls_b=m_ls_b, w_out=m_w_out,
                g_ffn2=m_g_ffn2, w2_in=m_w2_in, w2_out=m_w2_out, g_final=m_g_final)
    mom2 = dict(w_ada=v_w_ada, b_ada=v_b_ada, g_ffn1=v_g_ffn1, w1_in=v_w1_in, w1_out=v_w1_out, g_mix=v_g_mix,
                w_in=v_w_in, conv_w=v_conv_w, conv_b=v_conv_b, ln_a_g=v_ln_a_g, ln_a_b=v_ln_a_b, w_a_out=v_w_a_out,
                b_a_out=v_b_a_out, w_b_group=v_w_b_group, b_b_group=v_b_b_group, ls_b=v_ls_b, w_out=v_w_out,
                g_ffn2=v_g_ffn2, w2_in=v_w2_in, w2_out=v_w2_out, g_final=v_g_final)
    order = list(weights)

    s_len, d = x.shape[1], x.shape[2]
    f_dim = w1_out.shape[0] * N_CHIPS
    wc = conv_w.shape[1] * N_CHIPS
    wp = w_b_group.shape[0] * w_b_group.shape[1]
    n_groups, gi, goq = w_b_group.shape
    npc = w_in.shape[1]
    ada_c = w_ada.shape[1]
    dims = dict(S=s_len, D=d, F=f_dim)
    ts = _tile(s_len, 256)

    xi, yi, ci = lax.axis_index("x"), lax.axis_index("y"), lax.axis_index("c")
    q = 2 * xi + yi
    dev = 2 * q + ci
    q_idx = jnp.reshape(q, (1,)).astype(jnp.int32)
    qc_idx = jnp.stack([q, ci]).astype(jnp.int32)
    _PREVIOUS.clear()

    cwq = conv_w.shape[1]
    pack0 = jnp.concatenate([c.reshape(-1), conv_w.reshape(-1), b_b_group.reshape(-1)])
    n0 = -(-pack0.shape[0] // (8 * LANES)) * LANES
    pack0 = jnp.pad(pack0, (0, 8 * n0 - pack0.shape[0])).reshape(8, n0)
    g0 = _allgather_small("gather_small_in", pack0).reshape(N_DEV, 8 * n0)
    c_all = g0[:, :d]
    south = g0[0::2]
    cw_full = jnp.concatenate([south[k, d:d + CONV_K * cwq].reshape(CONV_K, cwq) for k in range(N_CHIPS)], axis=1)
    cw_pad = jnp.pad(cw_full, ((0, HALO - CONV_K), (0, 0)))
    o_bb = d + CONV_K * cwq
    bb_full = jnp.concatenate([south[k, o_bb:o_bb + n_groups * goq].reshape(n_groups, goq) for k in range(N_CHIPS)],
                              axis=1).reshape(1, d)

    as2d = lambda a: a.reshape(-1, a.shape[-1])
    groups = dict(w1_out=["w1_out"], w_in=["w_in"], mix=["w_a_out", "w_b_group", "w_out"], w2_in=["w2_in"],
                  w2_out=["w2_out"])
    big = ["w1_in", *[nm for grp in groups.values() for nm in grp]]
    cast = lambda nm: _cast_into_gathered(f"cast_{nm}", as2d(weights[nm]), q_idx)
    w1_ici = [_gather_ici("gather_w1_in_a_ici", [cast("w1_in")], 0, 2)]

    b_ada_mine = lax.dynamic_slice(b_ada, (q * ada_c,), (ada_c,)).reshape(1, ada_c)
    ada_piece = _ada_fwd("ada_fwd", c_all, w_ada, b_ada_mine)
    casts = {nm: cast(nm) for nm in big[1:]}
    g1 = _allgather_small("gather_ada", ada_piece).reshape(N_DEV, N_DEV, ada_c)
    w1_ici.append(_gather_ici("gather_w1_in_b_ici", w1_ici[0].arrays, 1, 2))
    ici = {grp: _gather_ici(f"gather_{grp}_ici", [casts[nm] for nm in names]) for grp, names in groups.items()}
    ada_rows = lax.dynamic_index_in_dim(g1[0::2], dev, axis=1, keepdims=False)
    ada = ada_rows.reshape(3, 3, 1, d)
    (sh1, sc1, gt1), (sh2, sc2, gt2), (sh3, sc3, gt3) = [[ada[i, j] for j in range(3)] for i in range(3)]

    row = lambda vct: vct.reshape(1, -1)
    g1v, gmv, g2v, gfv = row(g_ffn1), row(g_mix), row(g_ffn2), row(g_final)

    def arrived(grp):
        return _gather_d2d(f"gather_{grp}_d2d", ici[grp].wait())

    def gathered(fwd, grp):
        return {nm: g.reshape(N_CHIPS, 2 * g.shape[2], g.shape[3]) for nm, g in zip(groups[grp], fwd.wait())}

    x2 = x[0]
    tgt = loss_target[0]

    n1 = _norm_mod("ffn1_norm", x2, g1v, sc1, sh1, ts)
    fwd, w1_buf = {}, [w1_ici[1].arrays]

    def w1_in_part(part):
        def get():
            here = w1_ici[part].wait(w1_buf[0])
            w1_buf[0] = _gather_d2d(f"gather_w1_in_{'ab'[part]}_d2d", here, part, 2).wait()
            g = w1_buf[0][0]
            return g.reshape(N_CHIPS, 2 * g.shape[2], g.shape[3])
        return get

    def w1_out_after_swiglu():
        fwd["w1_out"] = arrived("w1_out")
        fwd["w_in"] = arrived("w_in")
        return gathered(fwd["w1_out"], "w1_out")["w1_out"].reshape(f_dim, d)

    hu1, act1, f1, w1_out_2d = _ffn_fwd("ffn1", n1, [w1_in_part(0), w1_in_part(1)], w1_out_after_swiglu, dims)
    w1_in_g = w1_buf[0][0].reshape(N_CHIPS, 2 * w1_buf[0][0].shape[2], w1_buf[0][0].shape[3])
    h1, n2 = _residual_norm_mod("mix_norm", x2, f1, gt1, 0.5, gmv, sc2, sh2, ts)
    w_in_g = gathered(fwd["w_in"], "w_in")["w_in"]

    tm = _tile(s_len, 1024)
    tnp = npc // 2
    proj = _matmul(
        "mix_proj", n2, [w_in_g], mode="nn", grid=(s_len // tm, 8, 1),
        a_spec=pl.BlockSpec((tm, d), lambda i, j, k: (i, 0)),
        b_specs=[pl.BlockSpec((None, d, tnp), lambda i, j, k: (j // 2, 0, j % 2))],
        out_shape=_sds((N_CHIPS, s_len, npc), BF16),
        out_specs=pl.BlockSpec((None, tm, tnp), lambda i, j, k: (j // 2, i, j % 2)),
        acc_shape=(tm, tnp), epilogue=_ep_store(BF16))
    fwd["mix"] = arrived("mix")
    cbv, lgv, lbv = row(conv_b), row(ln_a_g), row(ln_a_b)
    a3, mixed, conv_out = _mixer_mid("mix_mid", proj, cw_pad, cbv, lgv, lbv, wc, wp, ts)
    wts = gathered(fwd["mix"], "mix")
    w_out_2d = wts["w_out"].reshape(d, d)
    w_a_g = wts["w_a_out"]
    w_b_r = _regroup("regroup_w_b", wts["w_b_group"], n_groups)
    dq = d // N_CHIPS
    ya = _matmul(
        "mix_ya", a3, [w_a_g], mode="nn", grid=(s_len // tm, N_CHIPS, 1),
        a_spec=pl.BlockSpec((tm, wc), lambda i, j, k: (i, 0)),
        b_specs=[pl.BlockSpec((None, wc, dq), lambda i, j, k: (j, 0, 0))],
        out_shape=_sds((s_len, d), BF16), out_specs=pl.BlockSpec((tm, dq), lambda i, j, k: (i, j)),
        acc_shape=(tm, dq), epilogue=_ep_store(BF16))
    yb = _matmul(
        "mix_yb", mixed, [w_b_r], mode="nn", grid=(s_len // tm, n_groups, 1),
        a_spec=pl.BlockSpec((tm, gi), lambda i, j, k: (i, j)),
        b_specs=[pl.BlockSpec((None, gi, dq), lambda i, j, k: (j, 0, 0))],
        out_shape=_sds((s_len, d), BF16), out_specs=pl.BlockSpec((tm, dq), lambda i, j, k: (i, j)),
        acc_shape=(tm, dq), epilogue=_ep_store(BF16))
    bav, lsv = row(b_a_out), row(ls_b)
    z = _gates_fwd("mix_gates", proj, ya, yb, bav, bb_full, lsv, wc, wp, ts)
    tn = _tile(d, 1024)
    mix = _matmul(
        "mix_out", z, [w_out_2d], mode="nn", grid=(s_len // tm, d // tn, 1),
        a_spec=pl.BlockSpec((tm, d), lambda i, j, k: (i, 0)),
        b_specs=[pl.BlockSpec((d, tn), lambda i, j, k: (0, j))],
        out_shape=_sds((s_len, d), F32), out_specs=pl.BlockSpec((tm, tn), lambda i, j, k: (i, j)),
        acc_shape=(tm, tn), epilogue=_ep_store(F32))
    fwd["w2_in"] = arrived("w2_in")
    h2, n3 = _residual_norm_mod("ffn2_norm", h1, mix, gt2, 1.0, g2v, sc3, sh3, ts)
    w2_in_g = gathered(fwd["w2_in"], "w2_in")["w2_in"]
    hu2, act2, f3, w2_out_2d = _ffn_fwd(
        "ffn2", n3, [lambda: w2_in_g],
        lambda: gathered(arrived("w2_out"), "w2_out")["w2_out"].reshape(f_dim, d), dims)

    dh3, df3, d_gf, d_gt3, loss_cols = _final_loss("final_loss", h2, f3, tgt, gt3, 0.5, gfv, ts)
    rs, held = {}, {}
    dn3 = _ffn_bwd(
        "ffn2", n3, hu2, act2, df3, w2_in_g, w2_out_2d, dims,
        after_dw_out=lambda g: held.update(w2_out=g),
        after_dw_in=lambda g: rs.update(ffn2=_ReduceScatter("g_ffn2", ["w2_out", "w2_in"], [held["w2_out"], g],
                                                            qc_idx)))
    dh2, dmix, d_sh3, d_sc3, d_g2, d_gt2 = _norm_mod_bwd("ffn2_norm_bwd", h2, dn3, dh3, g2v, sc3, ts,
                                                         prev=(mix, gt2, 1.0))
    rs["ffn2"].step2()

    tk = s_len
    hq = d // (2 * N_CHIPS)
    gw_out = _matmul(
        "mix_dw_out", z, [dmix], mode="tn", grid=(N_CHIPS, d // tn, s_len // tk),
        a_spec=pl.BlockSpec((tk, 2 * hq), lambda i, j, k: (k, i)),
        b_specs=[pl.BlockSpec((tk, tn), lambda i, j, k: (k, j))],
        out_shape=_sds((2, N_CHIPS, hq, d), F32),
        out_specs=pl.BlockSpec((2, None, hq, tn), lambda i, j, k: (0, i, 0, j)),
        acc_shape=(2 * hq, tn), epilogue=_ep_halves(hq))
    dz = _matmul(
        "mix_dz", dmix, [w_out_2d], mode="nt", grid=(s_len // tm, d // tn, 1),
        a_spec=pl.BlockSpec((tm, d), lambda i, j, k: (i, 0)),
        b_specs=[pl.BlockSpec((tn, d), lambda i, j, k: (j, 0))],
        out_shape=_sds((s_len, d), F32), out_specs=pl.BlockSpec((tm, tn), lambda i, j, k: (i, j)),
        acc_shape=(tm, tn), epilogue=_ep_store(F32))
    dya, dyb, dgates, d_ba, d_ls, d_bb = _gates_bwd("mix_gates_bwd", proj, dz, ya, yb, bav, bb_full, lsv, wc, wp, ts)
    gw_a = _matmul(
        "mix_dw_a", a3, [dya], mode="tn", grid=(1, N_CHIPS, s_len // tk),
        a_spec=pl.BlockSpec((tk, wc), lambda i, j, k: (k, 0)),
        b_specs=[pl.BlockSpec((tk, dq), lambda i, j, k: (k, j))],
        out_shape=_sds((2, N_CHIPS, wc // 2, dq), F32),
        out_specs=pl.BlockSpec((2, None, wc // 2, dq), lambda i, j, k: (0, j, 0, 0)),
        acc_shape=(wc, dq), epilogue=_ep_halves(wc // 2))
    da3 = _matmul(
        "mix_da3", dya, [w_a_g], mode="nt", grid=(s_len // tm, 1, N_CHIPS),
        a_spec=pl.BlockSpec((tm, dq), lambda i, j, k: (i, k)),
        b_specs=[pl.BlockSpec((None, wc, dq), lambda i, j, k: (k, 0, 0))],
        out_shape=_sds((s_len, wc), F32), out_specs=pl.BlockSpec((tm, wc), lambda i, j, k: (i, 0)),
        acc_shape=(tm, wc), epilogue=_ep_store(F32))
    gpr = n_groups // 2

    def ep_by_chip(accs, ex, outs):
        for k in range(N_CHIPS):
            outs[0][k] = accs[0][:, k * goq:(k + 1) * goq]

    gw_b = _matmul(
        "mix_dw_b", mixed, [dyb], mode="tn", grid=(1, n_groups, s_len // tk),
        a_spec=pl.BlockSpec((tk, gi), lambda i, j, k: (k, j)),
        b_specs=[pl.BlockSpec((tk, dq), lambda i, j, k: (k, j))],
        out_shape=_sds((2, N_CHIPS, gpr * gi, goq), F32),
        out_specs=pl.BlockSpec((None, N_CHIPS, gi, goq), lambda i, j, k: (j // gpr, 0, j % gpr, 0)),
        acc_shape=(gi, dq), epilogue=ep_by_chip)
    dmixed = _matmul(
        "mix_dmixed", dyb, [w_b_r], mode="nt", grid=(s_len // tm, n_groups, 1),
        a_spec=pl.BlockSpec((tm, dq), lambda i, j, k: (i, j)),
        b_specs=[pl.BlockSpec((None, gi, dq), lambda i, j, k: (j, 0, 0))],
        out_shape=_sds((s_len, wp), F32), out_specs=pl.BlockSpec((tm, gi), lambda i, j, k: (i, j)),
        acc_shape=(tm, gi), epilogue=_ep_store(F32))
    da1, d_lg, d_lb, d_cb, d_cw = _conv_branch_bwd("mix_conv_bwd", proj, conv_out, da3, lgv, lbv, wc, wp, ts)
    dproj = _mixer_in_bwd("mix_in_bwd", proj, da1, dmixed, dgates, cw_pad, wc, wp, ts)
    hd = d // 2
    gw_in = _matmul(
        "mix_dw_in", n2, [dproj], mode="tn", grid=(2, 8, s_len // tk),
        a_spec=pl.BlockSpec((tk, hd), lambda i, j, k: (k, i)),
        b_specs=[pl.BlockSpec((None, tk, tnp), lambda i, j, k: (j // 2, k, j % 2))],
        out_shape=_sds((2, N_CHIPS, hd, npc), F32),
        out_specs=pl.BlockSpec((None, None, hd, tnp), lambda i, j, k: (i, j // 2, 0, j % 2)),
        acc_shape=(hd, tnp), epilogue=_ep_store(F32))
    rs["mix"] = _ReduceScatter("g_mix", ["w_in", "w_a_out", "w_b_group", "w_out"], [gw_in, gw_a, gw_b, gw_out],
                               qc_idx)
    rs["ffn2"].step3()
    dn2 = _matmul(
        "mix_dn", dproj, [w_in_g], mode="nt", grid=(s_len // tm, d // tn, N_CHIPS),
        a_spec=pl.BlockSpec((None, tm, npc), lambda i, j, k: (k, i, 0)),
        b_specs=[pl.BlockSpec((None, tn, npc), lambda i, j, k: (k, j, 0))],
        out_shape=_sds((s_len, d), F32), out_specs=pl.BlockSpec((tm, tn), lambda i, j, k: (i, j)),
        acc_shape=(tm, tn), epilogue=_ep_store(F32))
    dh1, df1, d_sh2, d_sc2, d_gm, d_gt1 = _norm_mod_bwd("mix_norm_bwd", h1, dn2, dh2, gmv, sc2, ts,
                                                        prev=(f1, gt1, 0.5))
    rs["mix"].step2()

    def w1_in_ready(g):
        rs["w1_in"] = _ReduceScatter("g_w1_in", ["w1_in"], [g], qc_idx)
        rs["w1_out"].step2()
        rs["mix"].step3()

    dn1 = _ffn_bwd(
        "ffn1", n1, hu1, act1, df1, w1_in_g, w1_out_2d, dims,
        after_dw_out=lambda g: rs.update(w1_out=_ReduceScatter("g_w1_out", ["w1_out"], [g], qc_idx)),
        after_dw_in=w1_in_ready)
    grad_x, d_sh1, d_sc1, d_g1 = _norm_mod_bwd("ffn1_norm_bwd", x2, dn1, dh1, g1v, sc1, ts)

    d_ada = jnp.concatenate([d_sh1, d_sc1, d_gt1, d_sh2, d_sc2, d_gt2, d_sh3, d_sc3, d_gt3], axis=1)
    small = [d_ada, d_g1, d_gm, d_cw[:CONV_K].reshape(1, -1), d_cb, d_lg, d_lb, d_ba, d_bb, d_ls, d_g2, d_gf,
             loss_cols]
    sizes = [a.shape[1] for a in small]
    pack1 = jnp.concatenate(small, axis=1).reshape(-1)
    n1p = -(-pack1.shape[0] // (8 * LANES)) * LANES
    pack1 = jnp.pad(pack1, (0, 8 * n1p - pack1.shape[0])).reshape(8, n1p)
    g2 = _allgather_small("gather_small_grads", pack1)
    rs["w1_in"].step2()
    total = _sum_devices("sum_small_grads", g2, 8).reshape(-1)
    offs = [0]
    for sz in sizes:
        offs.append(offs[-1] + sz)
    tot = [total[offs[k]:offs[k + 1]] for k in range(len(sizes))]
    d_ada_all = g2.reshape(N_DEV, 8 * n1p)[:, :sizes[0]]
    loss = jnp.sum(tot[12])

    grads = {}
    grads["b_ada"] = tot[0]
    grads["g_ffn1"], grads["g_mix"] = tot[1], tot[2]
    grads["conv_w"] = lax.dynamic_slice(tot[3].reshape(CONV_K, wc), (0, q * cwq), (CONV_K, cwq))
    grads["conv_b"], grads["ln_a_g"], grads["ln_a_b"], grads["b_a_out"] = tot[4], tot[5], tot[6], tot[7]
    grads["b_b_group"] = lax.dynamic_slice(tot[8].reshape(n_groups, N_CHIPS * goq), (0, q * goq), (n_groups, goq))
    grads["ls_b"], grads["g_ffn2"], grads["g_final"] = tot[9], tot[10], tot[11]

    delta, new_m, new_v = {}, {}, {}

    def adamw_group(reduced):
        for nm, g in reduced.items():
            shp = weights[nm].shape
            go, dl, mo, vo = _adamw(f"adamw_{nm}", as2d(weights[nm]), g, as2d(mom1[nm]), as2d(mom2[nm]))
            grads[nm], delta[nm], new_m[nm], new_v[nm] = go.reshape(shp), dl.reshape(shp), mo.reshape(shp), vo.reshape(shp)

    adamw_group(rs["ffn2"].result())
    rs["w1_out"].step3()
    adamw_group(rs["mix"].result())
    d_ada_mine = lax.dynamic_slice(d_ada_all, (0, q * ada_c), (N_DEV, ada_c))
    grads["w_ada"], delta["w_ada"], new_m["w_ada"], new_v["w_ada"] = _ada_grad_adamw(
        "adamw_w_ada", c_all.T, d_ada_mine, w_ada, m_w_ada, v_w_ada)
    rs["w1_in"].step3()
    smalls = [nm for nm in order if nm not in big and nm != "w_ada"]
    flat = lambda src: jnp.concatenate([src[nm].reshape(-1) for nm in smalls])
    n_small = sum(weights[nm].size for nm in smalls)
    rows_s = -(-n_small // (8 * LANES)) * 8
    packed = [jnp.pad(flat(src), (0, rows_s * LANES - n_small)).reshape(rows_s, LANES)
              for src in (weights, grads, mom1, mom2)]
    _, dl_s, mo_s, vo_s = _adamw("adamw_small", *packed)
    off = 0
    for nm in smalls:
        sz, shp = weights[nm].size, weights[nm].shape
        delta[nm] = dl_s.reshape(-1)[off:off + sz].reshape(shp)
        new_m[nm] = mo_s.reshape(-1)[off:off + sz].reshape(shp)
        new_v[nm] = vo_s.reshape(-1)[off:off + sz].reshape(shp)
        grads[nm] = grads[nm].reshape(shp)
        off += sz
    adamw_group(rs["w1_out"].result())
    adamw_group(rs["w1_in"].result())

    return (loss, grad_x[None], *[grads[nm] for nm in order], *[delta[nm] for nm in order],
            *[new_m[nm] for nm in order], *[new_v[nm] for nm in order])
```

```python
import jax
import jax.numpy as jnp
from jax import lax
from jax.experimental import pallas as pl
from jax.experimental.pallas import tpu as pltpu

F32 = jnp.float32
BF16 = jnp.bfloat16
MESH = pl.DeviceIdType.MESH
ANY = pl.BlockSpec(memory_space=pl.ANY)
HBM = pl.BlockSpec(memory_space=pltpu.HBM)
SEM = pl.BlockSpec(memory_space=pltpu.SEMAPHORE)
EFFECT = pltpu.SideEffectType.DATAFLOW_SIDE_EFFECTING

EPS = 1e-6
CONV_K = 31
HALO = 32
POOL_WINDOWS = (2, 4, 8, 16)
N_CHIPS = 4
N_DEV = 8
LANES = 128

ADAM_LR = 0.001
ADAM_B1 = 0.9
ADAM_B2 = 0.999
ADAM_EPS = 1e-08
ADAM_WD = 0.01
ADAM_STEP = 10

DN = {
    "nn": (((1,), (0,)), ((), ())),
    "nt": (((1,), (1,)), ((), ())),
    "tn": (((0,), (0,)), ((), ())),
}


_PREVIOUS = []


def _ordered(call, args, n_lead, body, token=None, sources=()):
    dep = [pltpu.with_memory_space_constraint(p, pltpu.HBM) if p.size * p.dtype.itemsize >= (1 << 20) else p
           for p in _PREVIOUS if all(p is not a for a in (*args, *sources))]

    def wrapped(*refs):
        return body(*refs[:n_lead], *refs[n_lead + len(dep):])

    outs = call(wrapped, [ANY] * len(dep))(*args, *dep)
    seq = outs if isinstance(outs, (list, tuple)) else [outs]
    _PREVIOUS[:] = [seq[token] if token is not None else
                    next(o for o in seq if jnp.issubdtype(o.dtype, jnp.floating))]
    return outs


def _pcall(body, *, name, out_shape, grid=None, in_specs=None, out_specs=None, scratch=(), aliases=None,
           prefetch=0, vmem_mb=None):
    params = {}
    if grid is not None:
        params["dimension_semantics"] = ("arbitrary",) * len(grid)
    if vmem_mb is not None:
        params["vmem_limit_bytes"] = vmem_mb << 20
    def in_hbm(shape, spec):
        big = shape.size * jnp.dtype(shape.dtype).itemsize >= (1 << 20)
        return pltpu.HBM(shape.shape, shape.dtype) if big and getattr(spec, "memory_space", None) != pltpu.VMEM else shape

    if isinstance(out_shape, (list, tuple)):
        out_shape = [in_hbm(s, sp) for s, sp in zip(out_shape, out_specs)]
    else:
        out_shape = in_hbm(out_shape, out_specs)
    kw = dict(name=name, out_shape=out_shape, compiler_params=pltpu.CompilerParams(**params))
    if aliases:
        kw["input_output_aliases"] = aliases

    def call(wrapped, dep_specs):
        specs = list(in_specs) + dep_specs
        if prefetch:
            return pl.pallas_call(wrapped, grid_spec=pltpu.PrefetchScalarGridSpec(
                num_scalar_prefetch=prefetch, grid=grid, in_specs=specs, out_specs=out_specs,
                scratch_shapes=list(scratch)), **kw)
        if grid is not None:
            return pl.pallas_call(wrapped, grid=grid, in_specs=specs, out_specs=out_specs,
                                  scratch_shapes=list(scratch), **kw)
        return pl.pallas_call(wrapped, in_specs=specs, out_specs=out_specs, scratch_shapes=list(scratch), **kw)

    def run(*args):
        specs = [None] * prefetch + list(in_specs)
        placed = [pltpu.with_memory_space_constraint(a, pltpu.HBM)
                  if a.size * a.dtype.itemsize >= (1 << 20) and getattr(s, "memory_space", None) != pltpu.VMEM else a
                  for a, s in zip(args, specs)]
        return _ordered(call, placed, prefetch + len(in_specs), body, sources=args)

    return run


def _tile(dim, pref):
    t = min(dim, pref)
    assert dim % t == 0, (dim, pref)
    return t


def _sds(shape, dtype):
    return jax.ShapeDtypeStruct(tuple(shape), dtype)


def _sigmoid(v):
    return 0.5 * jnp.tanh(0.5 * v) + 0.5


def _vec(w):
    return pl.BlockSpec((1, w), lambda *_: (0, 0))


def _acc_rows(ref, val, i):
    @pl.when(i == 0)
    def _():
        ref[...] = jnp.zeros_like(ref)

    ref[...] += jnp.sum(val, axis=0, keepdims=True)


def _matmul(name, a, bs, *, mode, grid, a_spec, b_specs, out_shape, out_specs, acc_shape, epilogue,
            extras=(), extra_specs=(), vmem_mb=56, carry=()):
    nb, ne, nk, nc = len(bs), len(extras), grid[2], len(carry)
    dn = DN[mode]

    def body(*all_refs):
        refs = all_refs[:1 + nb + ne] + all_refs[1 + nb + ne + nc:]
        a_ref, b_refs, ex = refs[0], refs[1:1 + nb], refs[1 + nb:1 + nb + ne]
        if nk == 1:
            outs = refs[1 + nb + ne:]
            accs = [lax.dot_general(a_ref[...], b[...], dn, preferred_element_type=F32) for b in b_refs]
            epilogue(accs, ex, outs)
            return
        outs, acc_refs = refs[1 + nb + ne:-nb], refs[-nb:]
        k = pl.program_id(2)

        @pl.when(k == 0)
        def _():
            for acc in acc_refs:
                acc[...] = jnp.zeros_like(acc)

        for acc, b in zip(acc_refs, b_refs):
            acc[...] += lax.dot_general(a_ref[...], b[...], dn, preferred_element_type=F32)

        @pl.when(k == nk - 1)
        def _():
            epilogue([acc[...] for acc in acc_refs], ex, outs)

    scratch = [pltpu.VMEM(acc_shape, F32) for _ in range(nb)] if nk > 1 else []
    return _pcall(body, name=name, out_shape=out_shape, grid=grid,
                  in_specs=[a_spec, *b_specs, *extra_specs, *[ANY] * nc], out_specs=out_specs, scratch=scratch,
                  aliases={1 + nb + ne + i: i for i in range(nc)}, vmem_mb=vmem_mb)(a, *bs, *extras, *carry)


def _ep_store(dtype):
    def ep(accs, ex, outs):
        outs[0][...] = accs[0].astype(dtype)
    return ep


def _ep_halves(h):
    def ep(accs, ex, outs):
        outs[0][0] = accs[0][:h]
        outs[0][1] = accs[0][h:]
    return ep


def _place():
    x, y, c = lax.axis_index("x"), lax.axis_index("y"), lax.axis_index("c")
    chips = [(1 - x, y), (x, 1 - y), (1 - x, 1 - y)]
    return x, y, c, chips


def _allgather_small(name, block):
    m_per, n = block.shape

    def body(x_ref, out_ref, send_sems, recv_sems, local_sem):
        x, y, c, chips = _place()
        me, sibling = (x, y, c), (x, y, 1 - c)

        def rows(px, py, pc):
            return out_ref.at[pl.ds((4 * px + 2 * py + pc) * m_per, m_per), :]

        def copy(k, blk, to, src=None):
            return pltpu.make_async_remote_copy(
                src_ref=rows(*blk) if src is None else src, dst_ref=rows(*blk),
                send_sem=send_sems.at[k], recv_sem=recv_sems.at[k], device_id=to, device_id_type=MESH)

        mine = pltpu.make_async_copy(x_ref, rows(*me), local_sem)
        mine.start()
        first = [copy(0, me, sibling, src=x_ref)]
        first += [copy(1 + j, me, (*chip, c), src=x_ref) for j, chip in enumerate(chips)]
        for cp in first:
            cp.start()
        passed = [copy(4 + j, (*chip, c), sibling) for j, chip in enumerate(chips)]
        for j, chip in enumerate(chips):
            copy(1 + j, (*chip, c), me).wait_recv()
            passed[j].start()
        copy(0, sibling, me).wait_recv()
        for j, chip in enumerate(chips):
            copy(4 + j, (*chip, 1 - c), me).wait_recv()
        for cp in first + passed:
            cp.wait_send()
        mine.wait()

    return _pcall(
        body, name=name, out_shape=_sds((N_DEV * m_per, n), block.dtype),
        in_specs=[pl.BlockSpec(memory_space=pltpu.VMEM)], out_specs=pl.BlockSpec(memory_space=pltpu.VMEM),
        scratch=[pltpu.SemaphoreType.DMA((7,)), pltpu.SemaphoreType.DMA((7,)), pltpu.SemaphoreType.DMA],
    )(block)


class _SplitCopies:
    def __init__(self, name, arrays, plan, n_copies):
        self.name, self.plan, self.n = name, plan, len(arrays)
        n = self.n

        def body(*refs):
            send, recv, token = refs[n], refs[n + 1], refs[-1]
            for k, (src, dst, _, peer) in enumerate(plan(refs[:n])):
                pltpu.make_async_remote_copy(src_ref=src, dst_ref=dst, send_sem=send.at[k], recv_sem=recv.at[k],
                                             device_id=peer, device_id_type=MESH).start()
            token[...] = jnp.zeros_like(token)

        def call(wrapped, dep_specs):
            return pl.pallas_call(
                wrapped, name=f"{name}_start",
                out_shape=(pltpu.SemaphoreType.DMA((n_copies,)), pltpu.SemaphoreType.DMA((n_copies,)),
                           *[pltpu.HBM(a.shape, a.dtype) for a in arrays], _sds((8, LANES), F32)),
                in_specs=[HBM] * n + dep_specs,
                out_specs=(SEM, SEM, *[HBM] * n, pl.BlockSpec(memory_space=pltpu.VMEM)),
                input_output_aliases={i: 2 + i for i in range(n)},
                compiler_params=pltpu.CompilerParams(has_side_effects=EFFECT))

        outs = _ordered(call, [pltpu.with_memory_space_constraint(a, pltpu.HBM) for a in arrays], n, body, token=-1,
                        sources=arrays)
        self.send, self.recv, self.arrays = outs[0], outs[1], list(outs[2:2 + n])

    def wait(self, arrays=None):
        n, plan = self.n, self.plan
        if arrays is not None:
            self.arrays = list(arrays)

        def body(*refs):
            send, recv, token = refs[n], refs[n + 1], refs[-1]
            for k, (src, _, landing, peer) in enumerate(plan(refs[:n])):
                cp = pltpu.make_async_remote_copy(src_ref=src, dst_ref=landing, send_sem=send.at[k],
                                                  recv_sem=recv.at[k], device_id=peer, device_id_type=MESH)
                cp.wait_send()
                cp.wait_recv()
            token[...] = jnp.zeros_like(token)

        def call(wrapped, dep_specs):
            return pl.pallas_call(
                wrapped, name=f"{self.name}_wait",
                out_shape=(*[pltpu.HBM(a.shape, a.dtype) for a in self.arrays], _sds((8, LANES), F32)),
                in_specs=[HBM] * n + [SEM, SEM] + dep_specs,
                out_specs=(*[HBM] * n, pl.BlockSpec(memory_space=pltpu.VMEM)),
                input_output_aliases={i: i for i in range(n)},
                compiler_params=pltpu.CompilerParams(has_side_effects=EFFECT))

        return list(_ordered(call, [*self.arrays, self.send, self.recv], n + 2, body, token=-1))[:n]


def _col_range(g, part, n_parts):
    width = g.shape[-1] // n_parts
    return (slice(None), pl.ds(part * width, width))


def _gather_ici(name, gathered, part=0, n_parts=1):
    def plan(refs):
        x, y, c, chips = _place()
        q = 2 * x + y
        return [(g.at[(q, c, *_col_range(g, part, n_parts))], g.at[(q, c, *_col_range(g, part, n_parts))],
                 g.at[(2 * px + py, c, *_col_range(g, part, n_parts))], (px, py, c))
                for g in refs for px, py in chips]

    return _SplitCopies(name, gathered, plan, 3 * len(gathered))


def _gather_d2d(name, gathered, part=0, n_parts=1):
    def plan(refs):
        x, y, c, chips = _place()
        return [(g.at[(2 * px + py, c, *_col_range(g, part, n_parts))],
                 g.at[(2 * px + py, c, *_col_range(g, part, n_parts))],
                 g.at[(2 * px + py, 1 - c, *_col_range(g, part, n_parts))], (x, y, 1 - c))
                for g in refs for px, py in chips]

    return _SplitCopies(name, gathered, plan, 3 * len(gathered))


class _TwoPartGather:
    def __init__(self, name, gathered):
        self.name, self.d2d = name, {}
        self.ici = [_gather_ici(f"gather_{name}_a_ici", [gathered], 0, 2)]
        self.buf = self.ici[0].arrays

    def start_second(self):
        self.ici.append(_gather_ici(f"gather_{self.name}_b_ici", self.buf, 1, 2))
        self.buf = self.ici[1].arrays

    def arrive(self, part):
        here = self.ici[part].wait(self.buf)
        self.d2d[part] = _gather_d2d(f"gather_{self.name}_{'ab'[part]}_d2d", here, part, 2)
        self.buf = self.d2d[part].arrays

    def ready(self, part):
        self.buf = self.d2d[part].wait(self.buf)
        g = self.buf[0]
        return g.reshape(N_CHIPS, 2 * g.shape[2], g.shape[3])


def _scatter_sibling(name, grads):
    n = len(grads)

    def plan(refs):
        x, y, c, _ = _place()
        return [(refs[w].at[1 - c], refs[n + w], refs[n + w], (x, y, 1 - c)) for w in range(n)]

    landing = [lax.empty(g.shape[1:], g.dtype) for g in grads]
    return _SplitCopies(name, [*grads, *landing], plan, n)


def _scatter_chips(name, sums):
    n = len(sums)

    def plan(refs):
        x, y, c, chips = _place()
        return [(refs[w].at[2 * px + py], refs[n + w].at[j], refs[n + w].at[j], (px, py, c))
                for w in range(n) for j, (px, py) in enumerate(chips)]

    landing = [lax.empty((3, *s.shape[1:]), s.dtype) for s in sums]
    return _SplitCopies(name, [*sums, *landing], plan, 3 * n)


def _share_final(name, finals):
    def plan(refs):
        x, y, c, _ = _place()
        return [(f.at[c], f.at[c], f.at[1 - c], (x, y, 1 - c)) for f in refs]

    return _SplitCopies(name, finals, plan, len(finals))


def _row_tile(rows, cols, budget_elems=393216):
    best = 8
    for t in range(8, rows + 1, 8):
        if rows % t == 0 and t * cols <= budget_elems:
            best = t
    return best if rows % best == 0 else rows


def _sum_with_sibling(name, grad, recv, qc_idx):
    _, _, h, cols = grad.shape
    tr = _row_tile(h, cols)

    def body(s_ref, g_ref, r_ref, own_ref, pb_ref):
        p = g_ref[...] + r_ref[...]
        pb_ref[...] = p.astype(BF16)

        @pl.when(pl.program_id(1) == s_ref[0])
        def _():
            own_ref[...] = p

    blk = pl.BlockSpec((None, tr, cols), lambda r, k, s: (k, r, 0))
    return _pcall(
        body, name=name, out_shape=[_sds((h, cols), F32), _sds((N_CHIPS, h, cols), BF16)],
        grid=(h // tr, N_CHIPS), prefetch=1,
        in_specs=[pl.BlockSpec((None, None, tr, cols), lambda r, k, s: (s[1], k, r, 0)), blk],
        out_specs=[pl.BlockSpec((tr, cols), lambda r, k, s: (r, 0)), blk], vmem_mb=32,
    )(qc_idx, grad, recv)


def _sum_chips(name, own, recv, qc_idx):
    h, cols = own.shape
    tr = _row_tile(h, cols)

    def body(s_ref, p_ref, t_ref, o_ref):
        o_ref[...] = ((p_ref[...] + t_ref[0].astype(F32)) + t_ref[1].astype(F32)) + t_ref[2].astype(F32)

    return _pcall(
        body, name=name, out_shape=_sds((2, h, cols), F32), grid=(h // tr,), prefetch=1,
        in_specs=[pl.BlockSpec((tr, cols), lambda r, s: (r, 0)),
                  pl.BlockSpec((3, tr, cols), lambda r, s: (0, r, 0))],
        out_specs=pl.BlockSpec((None, tr, cols), lambda r, s: (s[1], r, 0)), vmem_mb=32,
    )(qc_idx, own, recv)


class _ReduceScatter:
    def __init__(self, tag, names, grads, qc_idx):
        self.tag, self.names, self.n, self.qc_idx = tag, names, len(grads), qc_idx
        self.copies = _scatter_sibling(f"{tag}_rs_sibling", grads)

    def step2(self):
        n = self.n
        arrs = self.copies.wait()
        sums = [_sum_with_sibling(f"{nm}_sum_sibling", arrs[w], arrs[n + w], self.qc_idx)
                for w, nm in enumerate(self.names)]
        self.own = [s[0] for s in sums]
        self.copies = _scatter_chips(f"{self.tag}_rs_chips", [s[1] for s in sums])

    def step3(self):
        n = self.n
        arrs = self.copies.wait()
        finals = [_sum_chips(f"{nm}_sum_chips", self.own[w], arrs[n + w], self.qc_idx)
                  for w, nm in enumerate(self.names)]
        self.copies = _share_final(f"{self.tag}_rs_final", finals)

    def result(self):
        return {nm: f.reshape(2 * f.shape[1], f.shape[2]) for nm, f in zip(self.names, self.copies.wait())}


def _cast_into_gathered(name, w, q_idx):
    rows, cols = w.shape
    h = rows // 2
    tr = _row_tile(h, cols, 1 << 20)
    nr = h // tr

    def body(s_ref, w_ref, o_ref):
        o_ref[...] = w_ref[...].astype(BF16)

    return _pcall(body, name=name, out_shape=_sds((N_CHIPS, 2, h, cols), BF16), grid=(2, nr), prefetch=1,
                  in_specs=[pl.BlockSpec((tr, cols), lambda hf, r, s: (hf * nr + r, 0))],
                  out_specs=pl.BlockSpec((None, None, tr, cols), lambda hf, r, s: (s[0], hf, r, 0)),
                  vmem_mb=32)(q_idx, w)


def _regroup(name, w, n_groups):
    n_chips, rows, goq = w.shape
    gi = rows // n_groups

    def body(w_ref, o_ref):
        o_ref[...] = w_ref[...]

    return _pcall(body, name=name, out_shape=_sds((n_groups, gi, n_chips * goq), w.dtype), grid=(n_groups, n_chips),
                  in_specs=[pl.BlockSpec((None, gi, goq), lambda g, k: (k, g, 0))],
                  out_specs=pl.BlockSpec((None, gi, goq), lambda g, k: (g, 0, k)), vmem_mb=32)(w)


def _rms(h):
    r = lax.rsqrt(jnp.mean(h * h, axis=-1, keepdims=True) + EPS)
    return r, h * r


def _norm_mod(name, h, g, sc, sh, ts):
    s_len, d = h.shape

    def body(h_ref, g_ref, sc_ref, sh_ref, n_ref):
        _, xhat = _rms(h_ref[...])
        n_ref[...] = ((xhat * g_ref[...]) * (1.0 + sc_ref[...]) + sh_ref[...]).astype(BF16)

    row = pl.BlockSpec((ts, d), lambda i: (i, 0))
    return _pcall(body, name=name, out_shape=_sds((s_len, d), BF16), grid=(s_len // ts,),
                  in_specs=[row, _vec(d), _vec(d), _vec(d)], out_specs=row, vmem_mb=32)(h, g, sc, sh)


def _residual_norm_mod(name, h, f, gate, cmul, g, sc, sh, ts):
    s_len, d = h.shape

    def body(h_ref, f_ref, gt_ref, g_ref, sc_ref, sh_ref, ho_ref, n_ref):
        hn = h_ref[...] + (cmul * gt_ref[...]) * f_ref[...]
        ho_ref[...] = hn
        _, xhat = _rms(hn)
        n_ref[...] = ((xhat * g_ref[...]) * (1.0 + sc_ref[...]) + sh_ref[...]).astype(BF16)

    row = pl.BlockSpec((ts, d), lambda i: (i, 0))
    return _pcall(body, name=name, out_shape=[_sds((s_len, d), F32), _sds((s_len, d), BF16)],
                  grid=(s_len // ts,), in_specs=[row, row, _vec(d), _vec(d), _vec(d), _vec(d)],
                  out_specs=[row, row], vmem_mb=32)(h, f, gate, g, sc, sh)


def _final_loss(name, h, f, tgt, gate, cmul, g, ts):
    s_len, d = h.shape

    def body(h_ref, f_ref, t_ref, gt_ref, g_ref, dh_ref, df_ref, dg_ref, dgt_ref, loss_ref):
        i = pl.program_id(0)
        fv = f_ref[...]
        coef = cmul * gt_ref[...]
        hn = h_ref[...] + coef * fv
        r, xhat = _rms(hn)
        err = xhat * g_ref[...] - t_ref[...]
        _acc_rows(loss_ref, (0.5 / d) * (err * err), i)
        dy = err * (1.0 / d)
        _acc_rows(dg_ref, dy * xhat, i)
        dxhat = dy * g_ref[...]
        dh = r * (dxhat - xhat * jnp.mean(dxhat * xhat, axis=-1, keepdims=True))
        dh_ref[...] = dh
        _acc_rows(dgt_ref, cmul * (dh * fv), i)
        df_ref[...] = (coef * dh).astype(BF16)

    row = pl.BlockSpec((ts, d), lambda i: (i, 0))
    return _pcall(body, name=name,
                  out_shape=[_sds((s_len, d), F32), _sds((s_len, d), BF16)] + [_sds((1, d), F32)] * 3,
                  grid=(s_len // ts,), in_specs=[row, row, row, _vec(d), _vec(d)],
                  out_specs=[row, row, _vec(d), _vec(d), _vec(d)], vmem_mb=40)(h, f, tgt, gate, g)


def _norm_mod_bwd(name, h, dn, dh_next, g, sc, ts, prev=None):
    s_len, d = h.shape
    has_prev = prev is not None
    cmul = prev[2] if has_prev else None

    def body(*refs):
        if has_prev:
            h_ref, dn_ref, dhn_ref, f_ref, g_ref, sc_ref, gt_ref, dh_ref, df_ref, dsh_ref, dsc_ref, dg_ref, dgt_ref = refs
        else:
            h_ref, dn_ref, dhn_ref, g_ref, sc_ref, dh_ref, dsh_ref, dsc_ref, dg_ref = refs
        i = pl.program_id(0)
        r, xhat = _rms(h_ref[...])
        dn_v = dn_ref[...]
        gv = g_ref[...]
        _acc_rows(dsh_ref, dn_v, i)
        _acc_rows(dsc_ref, dn_v * (xhat * gv), i)
        dnrm = dn_v * (1.0 + sc_ref[...])
        _acc_rows(dg_ref, dnrm * xhat, i)
        dxhat = dnrm * gv
        dh = dhn_ref[...] + r * (dxhat - xhat * jnp.mean(dxhat * xhat, axis=-1, keepdims=True))
        dh_ref[...] = dh
        if has_prev:
            _acc_rows(dgt_ref, cmul * (dh * f_ref[...]), i)
            df_ref[...] = ((cmul * gt_ref[...]) * dh).astype(BF16)

    row = pl.BlockSpec((ts, d), lambda i: (i, 0))
    if has_prev:
        ins, in_specs = [h, dn, dh_next, prev[0], g, sc, prev[1]], [row, row, row, row, _vec(d), _vec(d), _vec(d)]
        out_shape = [_sds((s_len, d), F32), _sds((s_len, d), BF16)] + [_sds((1, d), F32)] * 4
        out_specs = [row, row] + [_vec(d)] * 4
    else:
        ins, in_specs = [h, dn, dh_next, g, sc], [row, row, row, _vec(d), _vec(d)]
        out_shape = [_sds((s_len, d), F32)] + [_sds((1, d), F32)] * 3
        out_specs = [row] + [_vec(d)] * 3
    return _pcall(body, name=name, out_shape=out_shape, grid=(s_len // ts,), in_specs=in_specs,
                  out_specs=out_specs, vmem_mb=40)(*ins)


def _cols(ref, lo, hi, npc, rows=slice(None)):
    parts = []
    while lo < hi:
        q, o = divmod(lo, npc)
        n = min(hi - lo, npc - o)
        parts.append(ref[q, rows, o:o + n].astype(F32))
        lo += n
    return parts[0] if len(parts) == 1 else jnp.concatenate(parts, axis=-1)


def _store_cols(ref, lo, val, npc, rows=slice(None)):
    off, width = 0, val.shape[-1]
    while off < width:
        q, o = divmod(lo + off, npc)
        n = min(width - off, npc - o)
        ref[q, rows, o:o + n] = val[:, off:off + n]
        off += n


def _chips_covering(cols, npc):
    return -(-cols // npc)


SUBLANES = 8
ROW_CHUNK = 32


def _make_phases(src_ref, ph_ref):
    rows = src_ref.shape[0] - SUBLANES
    for b in range(1, SUBLANES):
        ph_ref[b - 1] = src_ref[pl.ds(b, rows), :]


def _window(src_ref, ph_ref, off, r0, cols=slice(None)):
    a, b = divmod(off, SUBLANES)
    start = pl.multiple_of(r0 + SUBLANES * a, SUBLANES)
    if b == 0:
        return src_ref[pl.ds(start, ROW_CHUNK), cols]
    return ph_ref[b - 1, pl.ds(start, ROW_CHUNK), cols]


def _phase_scratch(rows, width):
    return pltpu.VMEM((SUBLANES - 1, rows - SUBLANES, width), F32)


def _conv(a0s_ref, a0p_ref, cw_ref, cb_ref, r0):
    a1 = cb_ref[...] + cw_ref[0:1, :] * _window(a0s_ref, a0p_ref, HALO - CONV_K + 1, r0)
    for k in range(1, CONV_K):
        a1 = a1 + cw_ref[k:k + 1, :] * _window(a0s_ref, a0p_ref, HALO - CONV_K + 1 + k, r0)
    return a1


def _layer_norm(a1, lg_ref, lb_ref):
    mu = jnp.mean(a1, axis=-1, keepdims=True)
    ctr = a1 - mu
    rstd = lax.rsqrt(jnp.mean(ctr * ctr, axis=-1, keepdims=True) + EPS)
    xh = ctr * rstd
    return xh, rstd, xh * lg_ref[...] + lb_ref[...]


def _for_chunks(ts, fn):
    def step(ci, carry):
        fn(pl.multiple_of(ci * ROW_CHUNK, ROW_CHUNK))
        return carry

    lax.fori_loop(0, ts // ROW_CHUNK, step, 0)


def _stage_glu(p_ref, ph_ref, a0s_ref, i, wc, npc, ts):
    a0 = _cols(p_ref, 0, wc, npc) * _sigmoid(_cols(p_ref, wc, 2 * wc, npc))
    a0h = _cols(ph_ref, 0, wc, npc) * _sigmoid(_cols(ph_ref, wc, 2 * wc, npc))
    a0s_ref[0:HALO, :] = jnp.where(i > 0, a0h, 0.0)
    a0s_ref[HALO:HALO + ts, :] = a0


def _mixer_mid(name, proj, cw, cb, lg, lb, wc, wp, ts):
    _, s_len, npc = proj.shape
    nq = _chips_covering(2 * wc + wp, npc)
    gi = wp // len(POOL_WINDOWS)
    hb = ts // HALO

    def body(p_ref, ph_ref, cw_ref, cb_ref, lg_ref, lb_ref, a3_ref, mx_ref, a1_ref, a0s_ref, vs_ref, a0p_ref,
             vp_ref):
        i = pl.program_id(0)
        _stage_glu(p_ref, ph_ref, a0s_ref, i, wc, npc, ts)
        vs_ref[0:HALO, :] = jnp.where(i > 0, _cols(ph_ref, 2 * wc, 2 * wc + wp, npc), 0.0)
        vs_ref[HALO:HALO + ts, :] = _cols(p_ref, 2 * wc, 2 * wc + wp, npc)
        _make_phases(a0s_ref, a0p_ref)
        _make_phases(vs_ref, vp_ref)

        def chunk(r0):
            rows = pl.ds(r0, ROW_CHUNK)
            a1 = _conv(a0s_ref, a0p_ref, cw_ref, cb_ref, r0)
            a1_ref[rows, :] = a1
            _, _, a2 = _layer_norm(a1, lg_ref, lb_ref)
            a3_ref[rows, :] = (a2 * _sigmoid(a2)).astype(BF16)
            t_abs = i * ts + r0 + lax.broadcasted_iota(jnp.int32, (ROW_CHUNK, 1), 0)
            for g, win in enumerate(POOL_WINDOWS):
                cs = slice(g * gi, (g + 1) * gi)
                v_now = _window(vs_ref, vp_ref, HALO, r0, cs)
                acc = v_now
                for dlt in range(1, win):
                    acc = acc + _window(vs_ref, vp_ref, HALO - dlt, r0, cs)
                cnt = jnp.minimum(t_abs + 1, win).astype(F32)
                mx_ref[rows, cs] = (acc / cnt - v_now).astype(BF16)

        _for_chunks(ts, chunk)

    return _pcall(
        body, name=name, out_shape=[_sds((s_len, wc), BF16), _sds((s_len, wp), BF16), _sds((s_len, wc), F32)],
        grid=(s_len // ts,),
        in_specs=[pl.BlockSpec((nq, ts, npc), lambda i: (0, i, 0)),
                  pl.BlockSpec((nq, HALO, npc), lambda i: (0, jnp.maximum(i * hb - 1, 0), 0)),
                  pl.BlockSpec((HALO, wc), lambda i: (0, 0)), _vec(wc), _vec(wc), _vec(wc)],
        out_specs=[pl.BlockSpec((ts, wc), lambda i: (i, 0)), pl.BlockSpec((ts, wp), lambda i: (i, 0)),
                   pl.BlockSpec((ts, wc), lambda i: (i, 0))],
        scratch=[pltpu.VMEM((HALO + ts, wc), F32), pltpu.VMEM((HALO + ts, wp), F32),
                 _phase_scratch(HALO + ts, wc), _phase_scratch(HALO + ts, wp)], vmem_mb=56,
    )(proj, proj, cw, cb, lg, lb)


def _gates_fwd(name, proj, ya, yb, b_a, b_b, ls, wc, wp, ts):
    _, s_len, npc = proj.shape
    d = ya.shape[1]
    g0 = 2 * wc + wp

    def body(p_ref, ya_ref, yb_ref, ba_ref, bb_ref, ls_ref, z_ref):
        ga = _sigmoid(_cols(p_ref, g0, g0 + d, npc))
        gb = _sigmoid(_cols(p_ref, g0 + d, g0 + 2 * d, npc))
        z = ga * (ya_ref[...] + ba_ref[...]) + gb * ((yb_ref[...] + bb_ref[...]) * ls_ref[...])
        z_ref[...] = z.astype(BF16)

    row = pl.BlockSpec((ts, d), lambda i: (i, 0))
    return _pcall(body, name=name, out_shape=_sds((s_len, d), BF16), grid=(s_len // ts,),
                  in_specs=[pl.BlockSpec((N_CHIPS, ts, npc), lambda i: (0, i, 0)), row, row, _vec(d), _vec(d), _vec(d)],
                  out_specs=row, vmem_mb=48)(proj, ya, yb, b_a, b_b, ls)


def _gates_bwd(name, proj, dz, ya, yb, b_a, b_b, ls, wc, wp, ts):
    _, s_len, npc = proj.shape
    d = ya.shape[1]
    g0 = 2 * wc + wp

    def body(p_ref, dz_ref, ya_ref, yb_ref, ba_ref, bb_ref, ls_ref, dya_ref, dyb_ref, dgt_ref, dba_ref, dls_ref,
             dbb_ref):
        i = pl.program_id(0)
        ga = _sigmoid(_cols(p_ref, g0, g0 + d, npc))
        gb = _sigmoid(_cols(p_ref, g0 + d, g0 + 2 * d, npc))
        dz_v = dz_ref[...]
        y_a = ya_ref[...] + ba_ref[...]
        y_b0 = yb_ref[...] + bb_ref[...]
        ls_v = ls_ref[...]
        dya = dz_v * ga
        dya_ref[...] = dya.astype(BF16)
        _acc_rows(dba_ref, dya, i)
        t = dz_v * gb
        _acc_rows(dls_ref, t * y_b0, i)
        dyb = t * ls_v
        dyb_ref[...] = dyb.astype(BF16)
        _acc_rows(dbb_ref, dyb, i)
        dgt_ref[:, 0:d] = (dz_v * y_a * ga * (1.0 - ga)).astype(BF16)
        dgt_ref[:, d:2 * d] = (dz_v * (y_b0 * ls_v) * gb * (1.0 - gb)).astype(BF16)

    row = pl.BlockSpec((ts, d), lambda i: (i, 0))
    return _pcall(
        body, name=name,
        out_shape=[_sds((s_len, d), BF16), _sds((s_len, d), BF16), _sds((s_len, 2 * d), BF16)] + [_sds((1, d), F32)] * 3,
        grid=(s_len // ts,),
        in_specs=[pl.BlockSpec((N_CHIPS, ts, npc), lambda i: (0, i, 0)), row, row, row, _vec(d), _vec(d), _vec(d)],
        out_specs=[row, row, pl.BlockSpec((ts, 2 * d), lambda i: (i, 0))] + [_vec(d)] * 3, vmem_mb=48,
    )(proj, dz, ya, yb, b_a, b_b, ls)


def _conv_branch_bwd(name, proj, a1, da3, lg, lb, wc, wp, ts):
    _, s_len, npc = proj.shape
    nq = _chips_covering(2 * wc, npc)
    hb = ts // HALO

    n_tiles = s_len // ts

    def fold(v):
        return jnp.sum(v.reshape(ROW_CHUNK // SUBLANES, SUBLANES, v.shape[-1]), axis=0)

    def body(p_ref, ph_ref, a1_ref, da3_ref, lg_ref, lb_ref, da1_ref, dlg_ref, dlb_ref, dcb_ref, dcw_ref,
             a0s_ref, a0p_ref, vec8_ref, dcw8_ref):
        i = pl.program_id(0)
        _stage_glu(p_ref, ph_ref, a0s_ref, i, wc, npc, ts)
        _make_phases(a0s_ref, a0p_ref)

        @pl.when(i == 0)
        def _():
            vec8_ref[...] = jnp.zeros_like(vec8_ref)
            dcw8_ref[...] = jnp.zeros_like(dcw8_ref)

        def chunk(r0):
            rows = pl.ds(r0, ROW_CHUNK)
            xh, rstd, a2 = _layer_norm(a1_ref[rows, :], lg_ref, lb_ref)
            sig = _sigmoid(a2)
            da2 = da3_ref[rows, :] * (sig * (1.0 + a2 * (1.0 - sig)))
            vec8_ref[0] += fold(da2 * xh)
            vec8_ref[1] += fold(da2)
            dxh = da2 * lg_ref[...]
            da1 = rstd * (dxh - jnp.mean(dxh, axis=-1, keepdims=True)
                          - xh * jnp.mean(dxh * xh, axis=-1, keepdims=True))
            da1_ref[rows, :] = da1
            vec8_ref[2] += fold(da1)
            for k in range(CONV_K):
                dcw8_ref[k] += fold(da1 * _window(a0s_ref, a0p_ref, HALO - CONV_K + 1 + k, r0))

        _for_chunks(ts, chunk)

        @pl.when(i == n_tiles - 1)
        def _():
            dlg_ref[...] = jnp.sum(vec8_ref[0], axis=0, keepdims=True)
            dlb_ref[...] = jnp.sum(vec8_ref[1], axis=0, keepdims=True)
            dcb_ref[...] = jnp.sum(vec8_ref[2], axis=0, keepdims=True)
            dcw_ref[...] = jnp.sum(dcw8_ref[...], axis=1)

    return _pcall(
        body, name=name,
        out_shape=[_sds((s_len, wc), F32)] + [_sds((1, wc), F32)] * 3 + [_sds((HALO, wc), F32)],
        grid=(s_len // ts,),
        in_specs=[pl.BlockSpec((nq, ts, npc), lambda i: (0, i, 0)),
                  pl.BlockSpec((nq, HALO, npc), lambda i: (0, jnp.maximum(i * hb - 1, 0), 0)),
                  pl.BlockSpec((ts, wc), lambda i: (i, 0)), pl.BlockSpec((ts, wc), lambda i: (i, 0)),
                  _vec(wc), _vec(wc)],
        out_specs=[pl.BlockSpec((ts, wc), lambda i: (i, 0)), _vec(wc), _vec(wc), _vec(wc),
                   pl.BlockSpec((HALO, wc), lambda i: (0, 0))],
        scratch=[pltpu.VMEM((HALO + ts, wc), F32), _phase_scratch(HALO + ts, wc),
                 pltpu.VMEM((3, SUBLANES, wc), F32), pltpu.VMEM((HALO, SUBLANES, wc), F32)], vmem_mb=56,
    )(proj, proj, a1, da3, lg, lb)


def _mixer_in_bwd(name, proj, da1, dmixed, dgates, cw, wc, wp, ts):
    _, s_len, npc = proj.shape
    nq = _chips_covering(2 * wc, npc)
    gi = wp // len(POOL_WINDOWS)
    hb = ts // HALO
    n_tiles = s_len // ts
    last_hb = s_len // HALO - 1
    d2 = dgates.shape[1]

    def body(p_ref, d1_ref, d1n_ref, dm_ref, dmn_ref, dgt_ref, cw_ref, o_ref, d1s_ref, es_ref, d1p_ref, ep_ref):
        i = pl.program_id(0)
        more = i < n_tiles - 1
        d1s_ref[0:ts, :] = d1_ref[...]
        d1s_ref[ts:ts + HALO, :] = jnp.where(more, d1n_ref[...], 0.0)
        t_abs = i * ts + lax.broadcasted_iota(jnp.int32, (ts + HALO, 1), 0)
        dm_ext = jnp.concatenate([dm_ref[...], jnp.where(more, dmn_ref[...], 0.0)], axis=0)
        for g, win in enumerate(POOL_WINDOWS):
            cs = slice(g * gi, (g + 1) * gi)
            es_ref[:, cs] = dm_ext[:, cs] / jnp.minimum(t_abs + 1, win).astype(F32)
        _make_phases(d1s_ref, d1p_ref)
        _make_phases(es_ref, ep_ref)

        def chunk(r0):
            rows = pl.ds(r0, ROW_CHUNK)
            da0 = cw_ref[0:1, :] * _window(d1s_ref, d1p_ref, CONV_K - 1, r0)
            for k in range(1, CONV_K):
                da0 = da0 + cw_ref[k:k + 1, :] * _window(d1s_ref, d1p_ref, CONV_K - 1 - k, r0)
            glu_a = _cols(p_ref, 0, wc, npc, rows)
            sig = _sigmoid(_cols(p_ref, wc, 2 * wc, npc, rows))
            _store_cols(o_ref, 0, (da0 * sig).astype(BF16), npc, rows)
            _store_cols(o_ref, wc, (da0 * glu_a * sig * (1.0 - sig)).astype(BF16), npc, rows)
            parts = []
            for g, win in enumerate(POOL_WINDOWS):
                cs = slice(g * gi, (g + 1) * gi)
                acc = _window(es_ref, ep_ref, 0, r0, cs)
                for dlt in range(1, win):
                    acc = acc + _window(es_ref, ep_ref, dlt, r0, cs)
                parts.append(acc - dm_ref[rows, cs])
            _store_cols(o_ref, 2 * wc, jnp.concatenate(parts, axis=-1).astype(BF16), npc, rows)

        _for_chunks(ts, chunk)
        _store_cols(o_ref, 2 * wc + wp, dgt_ref[...], npc)

    nxt = lambda i: (jnp.minimum((i + 1) * hb, last_hb), 0)
    return _pcall(
        body, name=name, out_shape=_sds((N_CHIPS, s_len, npc), BF16), grid=(n_tiles,),
        in_specs=[pl.BlockSpec((nq, ts, npc), lambda i: (0, i, 0)),
                  pl.BlockSpec((ts, wc), lambda i: (i, 0)), pl.BlockSpec((HALO, wc), nxt),
                  pl.BlockSpec((ts, wp), lambda i: (i, 0)), pl.BlockSpec((HALO, wp), nxt),
                  pl.BlockSpec((ts, d2), lambda i: (i, 0)),
                  pl.BlockSpec((HALO, wc), lambda i: (0, 0))],
        out_specs=pl.BlockSpec((N_CHIPS, ts, npc), lambda i: (0, i, 0)),
        scratch=[pltpu.VMEM((ts + HALO, wc), F32), pltpu.VMEM((ts + HALO, wp), F32),
                 _phase_scratch(ts + HALO, wc), _phase_scratch(ts + HALO, wp)], vmem_mb=56,
    )(proj, da1, da1, dmixed, dmixed, dgates, cw)


def _ada_fwd(name, c_all, w, b):
    d, cols = w.shape
    tn = 512 if cols % 512 == 0 else cols

    def body(c_ref, w_ref, b_ref, o_ref):
        cv = c_ref[...]
        sc = (cv * _sigmoid(cv)).astype(BF16)
        o_ref[...] = jnp.dot(sc, w_ref[...].astype(BF16), preferred_element_type=F32) + b_ref[...]

    return _pcall(body, name=name, out_shape=_sds((N_DEV, cols), F32), grid=(cols // tn,),
                  in_specs=[pl.BlockSpec((N_DEV, d), lambda j: (0, 0)), pl.BlockSpec((d, tn), lambda j: (0, j)),
                            pl.BlockSpec((1, tn), lambda j: (0, j))],
                  out_specs=pl.BlockSpec((N_DEV, tn), lambda j: (0, j)), vmem_mb=32)(c_all, w, b)


def _adam_math(w, g, m, v):
    m_new = ADAM_B1 * m + (1.0 - ADAM_B1) * g
    v_new = ADAM_B2 * v + (1.0 - ADAM_B2) * (g * g)
    m_hat = m_new / (1.0 - ADAM_B1 ** ADAM_STEP)
    v_hat = v_new / (1.0 - ADAM_B2 ** ADAM_STEP)
    delta = -ADAM_LR * (m_hat / (jnp.sqrt(v_hat) + ADAM_EPS) + ADAM_WD * w)
    return delta, m_new, v_new


def _adamw(name, w, g, m, v):
    rows, cols = w.shape
    tr = _row_tile(rows, cols, 524288)

    def body(w_ref, g_ref, m_ref, v_ref, go_ref, d_ref, mo_ref, vo_ref):
        g = g_ref[...]
        go_ref[...] = g
        d_ref[...], mo_ref[...], vo_ref[...] = _adam_math(w_ref[...], g, m_ref[...], v_ref[...])

    spec = pl.BlockSpec((tr, cols), lambda i: (i, 0))
    return _pcall(body, name=name, out_shape=[_sds(w.shape, F32)] * 4, grid=(rows // tr,), in_specs=[spec] * 4,
                  out_specs=[spec] * 4, vmem_mb=40)(w, g, m, v)


def _ada_grad_adamw(name, c_t, d_ada, w, m, v):
    rows, cols = w.shape
    tr = _tile(rows, 256)
    tc = _tile(cols, 1536) if cols % 1536 == 0 else cols

    def body(c_ref, da_ref, w_ref, m_ref, v_ref, g_ref, d_ref, mo_ref, vo_ref):
        cv = c_ref[...]
        sc = cv * _sigmoid(cv)
        g = sc[:, 0:1] * da_ref[0:1, :]
        for b in range(1, N_DEV):
            g = g + sc[:, b:b + 1] * da_ref[b:b + 1, :]
        g_ref[...] = g
        d_ref[...], mo_ref[...], vo_ref[...] = _adam_math(w_ref[...], g, m_ref[...], v_ref[...])

    spec = pl.BlockSpec((tr, tc), lambda i, j: (i, j))
    return _pcall(body, name=name, out_shape=[_sds(w.shape, F32)] * 4, grid=(rows // tr, cols // tc),
                  in_specs=[pl.BlockSpec((tr, N_DEV), lambda i, j: (i, 0)),
                            pl.BlockSpec((N_DEV, tc), lambda i, j: (0, j)), spec, spec, spec],
                  out_specs=[spec] * 4, vmem_mb=40)(c_t, d_ada, w, m, v)


def _sum_devices(name, gathered, m_per):
    n = gathered.shape[1]

    def body(g_ref, o_ref):
        acc = g_ref[0:m_per, :]
        for dev in range(1, N_DEV):
            acc = acc + g_ref[dev * m_per:(dev + 1) * m_per, :]
        o_ref[...] = acc

    return _pcall(body, name=name, out_shape=_sds((m_per, n), F32),
                  in_specs=[pl.BlockSpec(memory_space=pltpu.VMEM)],
                  out_specs=pl.BlockSpec(memory_space=pltpu.VMEM))(gathered)


def _ffn_fwd(tag, n, w_in_parts, w_out_after_swiglu, dims):
    s_len, d, f_dim = dims["S"], dims["D"], dims["F"]
    tf = f_dim // 4
    tm0, tm = _tile(s_len, 512), _tile(s_len, 1024)
    n_parts = len(w_in_parts)
    nbp = (f_dim // 2) // tf
    nbq = nbp // n_parts

    def ep(accs, ex, outs):
        hh, uu = accs
        sig = _sigmoid(hh)
        silu = hh * sig
        outs[0][0] = (uu * (sig + silu * (1.0 - sig))).astype(BF16)
        outs[0][1] = silu.astype(BF16)
        outs[1][...] = (silu * uu).astype(BF16)

    done = ()
    for part, get_w in enumerate(w_in_parts):
        w_g = get_w()
        col = lambda j, part=part: (j // nbq) * nbp + part * nbq + j % nbq
        done = _matmul(
            f"{tag}_swiglu{part}", n, [w_g, w_g], mode="nn", grid=(2 * nbq, s_len // tm0, 1),
            a_spec=pl.BlockSpec((tm0, d), lambda j, i, k: (i, 0)),
            b_specs=[pl.BlockSpec((None, d, tf), lambda j, i, k, part=part: (j // nbq, 0, part * nbq + j % nbq)),
                     pl.BlockSpec((None, d, tf), lambda j, i, k, part=part: (2 + j // nbq, 0, part * nbq + j % nbq))],
            out_shape=[_sds((2, s_len, f_dim), BF16), _sds((s_len, f_dim), BF16)],
            out_specs=[pl.BlockSpec((2, tm0, tf), lambda j, i, k, col=col: (0, i, col(j))),
                       pl.BlockSpec((tm0, tf), lambda j, i, k, col=col: (i, col(j)))],
            acc_shape=(tm0, tf), epilogue=ep, carry=done)
    hu, act = done
    w_out2d = w_out_after_swiglu()
    tn2 = _tile(d, 1024)
    f = _matmul(
        f"{tag}_down", act, [w_out2d], mode="nn", grid=(s_len // tm, d // tn2, 2),
        a_spec=pl.BlockSpec((tm, 2 * tf), lambda i, j, k: (i, k)),
        b_specs=[pl.BlockSpec((2 * tf, tn2), lambda i, j, k: (k, j))],
        out_shape=_sds((s_len, d), F32), out_specs=pl.BlockSpec((tm, tn2), lambda i, j, k: (i, j)),
        acc_shape=(tm, tn2), epilogue=_ep_store(F32))
    return hu, act, f, w_out2d


def _ffn_bwd(tag, n, hu, act, df, w_in_g, w_out2d, dims, after_dw_out, after_dw_in):
    s_len, d, f_dim = dims["S"], dims["D"], dims["F"]
    tf = f_dim // 4
    tk = _tile(s_len, 2048)
    tn = _tile(d, 1024)
    g_out = _matmul(
        f"{tag}_dw_out", act, [df], mode="tn", grid=(4, d // tn, s_len // tk),
        a_spec=pl.BlockSpec((tk, tf), lambda i, j, k: (k, i)),
        b_specs=[pl.BlockSpec((tk, tn), lambda i, j, k: (k, j))],
        out_shape=_sds((2, 4, tf // 2, d), F32),
        out_specs=pl.BlockSpec((2, None, tf // 2, tn), lambda i, j, k: (0, i, 0, j)),
        acc_shape=(tf, tn), epilogue=_ep_halves(tf // 2))
    after_dw_out(g_out)

    def ep_dhu(accs, ex, outs):
        da = accs[0]
        outs[0][0] = (da * ex[0][0].astype(F32)).astype(BF16)
        outs[0][1] = (da * ex[0][1].astype(F32)).astype(BF16)

    tm = _tile(s_len, 512)
    hu_spec = pl.BlockSpec((2, tm, tf), lambda j, i, k: (0, i, j))
    dhu = _matmul(
        f"{tag}_dhu", df, [w_out2d], mode="nt", grid=(4, s_len // tm, 1),
        a_spec=pl.BlockSpec((tm, d), lambda j, i, k: (i, 0)),
        b_specs=[pl.BlockSpec((tf, d), lambda j, i, k: (j, 0))],
        extras=[hu], extra_specs=[hu_spec],
        out_shape=_sds((2, s_len, f_dim), BF16), out_specs=hu_spec, acc_shape=(tm, tf), epilogue=ep_dhu)

    hd = d // 2
    g_in = _matmul(
        f"{tag}_dw_in", n, [dhu], mode="tn", grid=(2, 8, s_len // tk),
        a_spec=pl.BlockSpec((tk, hd), lambda i, j, k: (k, i)),
        b_specs=[pl.BlockSpec((None, tk, tf), lambda i, j, k: (j // 4, k, j % 4))],
        out_shape=_sds((2, 4, hd, f_dim // 2), F32),
        out_specs=pl.BlockSpec((None, None, hd, tf), lambda i, j, k: (i, j // 2, 0, j % 2)),
        acc_shape=(hd, tf), epilogue=_ep_store(F32))
    after_dw_in(g_in)

    tm2 = _tile(s_len, 1024)
    dn = _matmul(
        f"{tag}_dn", dhu, [w_in_g], mode="nt", grid=(s_len // tm2, d // tn, N_CHIPS),
        a_spec=pl.BlockSpec((None, tm2, 2 * tf), lambda i, j, k: (k // 2, i, k % 2)),
        b_specs=[pl.BlockSpec((None, tn, 2 * tf), lambda i, j, k: (k, j, 0))],
        out_shape=_sds((s_len, d), F32), out_specs=pl.BlockSpec((tm2, tn), lambda i, j, k: (i, j)),
        acc_shape=(tm2, tn), epilogue=_ep_store(F32))
    return dn


def kernel(x, c, w_ada, b_ada, g_ffn1, w1_in, w1_out, g_mix, w_in, conv_w, conv_b, ln_a_g, ln_a_b, w_a_out, b_a_out, w_b_group, b_b_group, ls_b, w_out, g_ffn2, w2_in, w2_out, g_final, loss_target, m_w_ada, m_b_ada, m_g_ffn1, m_w1_in, m_w1_out, m_g_mix, m_w_in, m_conv_w, m_conv_b, m_ln_a_g, m_ln_a_b, m_w_a_out, m_b_a_out, m_w_b_group, m_b_b_group, m_ls_b, m_w_out, m_g_ffn2, m_w2_in, m_w2_out, m_g_final, v_w_ada, v_b_ada, v_g_ffn1, v_w1_in, v_w1_out, v_g_mix, v_w_in, v_conv_w, v_conv_b, v_ln_a_g, v_ln_a_b, v_w_a_out, v_b_a_out, v_w_b_group, v_b_b_group, v_ls_b, v_w_out, v_g_ffn2, v_w2_in, v_w2_out, v_g_final):
    weights = dict(w_ada=w_ada, b_ada=b_ada, g_ffn1=g_ffn1, w1_in=w1_in, w1_out=w1_out, g_mix=g_mix, w_in=w_in,
                   conv_w=conv_w, conv_b=conv_b, ln_a_g=ln_a_g, ln_a_b=ln_a_b, w_a_out=w_a_out, b_a_out=b_a_out,
                   w_b_group=w_b_group, b_b_group=b_b_group, ls_b=ls_b, w_out=w_out, g_ffn2=g_ffn2, w2_in=w2_in,
                   w2_out=w2_out, g_final=g_final)
    mom1 = dict(w_ada=m_w_ada, b_ada=m_b_ada, g_ffn1=m_g_ffn1, w1_in=m_w1_in, w1_out=m_w1_out, g_mix=m_g_mix,
                w_in=m_w_in, conv_w=m_conv_w, conv_b=m_conv_b, ln_a_g=m_ln_a_g, ln_a_b=m_ln_a_b, w_a_out=m_w_a_out,
                b_a_out=m_b_a_out, w_b_group=m_w_b_group, b_b_group=m_b_b_group, ls_b=m_ls_b, w_out=m_w_out,
                g_ffn2=m_g_ffn2, w2_in=m_w2_in, w2_out=m_w2_out, g_final=m_g_final)
    mom2 = dict(w_ada=v_w_ada, b_ada=v_b_ada, g_ffn1=v_g_ffn1, w1_in=v_w1_in, w1_out=v_w1_out, g_mix=v_g_mix,
                w_in=v_w_in, conv_w=v_conv_w, conv_b=v_conv_b, ln_a_g=v_ln_a_g, ln_a_b=v_ln_a_b, w_a_out=v_w_a_out,
                b_a_out=v_b_a_out, w_b_group=v_w_b_group, b_b_group=v_b_b_group, ls_b=v_ls_b, w_out=v_w_out,
                g_ffn2=v_g_ffn2, w2_in=v_w2_in, w2_out=v_w2_out, g_final=v_g_final)
    order = list(weights)

    s_len, d = x.shape[1], x.shape[2]
    f_dim = w1_out.shape[0] * N_CHIPS
    wc = conv_w.shape[1] * N_CHIPS
    wp = w_b_group.shape[0] * w_b_group.shape[1]
    n_groups, gi, goq = w_b_group.shape
    npc = w_in.shape[1]
    ada_c = w_ada.shape[1]
    dims = dict(S=s_len, D=d, F=f_dim)
    ts = _tile(s_len, 256)

    xi, yi, ci = lax.axis_index("x"), lax.axis_index("y"), lax.axis_index("c")
    q = 2 * xi + yi
    dev = 2 * q + ci
    q_idx = jnp.reshape(q, (1,)).astype(jnp.int32)
    qc_idx = jnp.stack([q, ci]).astype(jnp.int32)
    _PREVIOUS.clear()

    cwq = conv_w.shape[1]
    pack0 = jnp.concatenate([c.reshape(-1), conv_w.reshape(-1), b_b_group.reshape(-1)])
    n0 = -(-pack0.shape[0] // (8 * LANES)) * LANES
    pack0 = jnp.pad(pack0, (0, 8 * n0 - pack0.shape[0])).reshape(8, n0)
    g0 = _allgather_small("gather_small_in", pack0).reshape(N_DEV, 8 * n0)
    c_all = g0[:, :d]
    south = g0[0::2]
    cw_full = jnp.concatenate([south[k, d:d + CONV_K * cwq].reshape(CONV_K, cwq) for k in range(N_CHIPS)], axis=1)
    cw_pad = jnp.pad(cw_full, ((0, HALO - CONV_K), (0, 0)))
    o_bb = d + CONV_K * cwq
    bb_full = jnp.concatenate([south[k, o_bb:o_bb + n_groups * goq].reshape(n_groups, goq) for k in range(N_CHIPS)],
                              axis=1).reshape(1, d)

    as2d = lambda a: a.reshape(-1, a.shape[-1])
    groups = dict(w1_out=["w1_out"], mix=["w_a_out", "w_b_group", "w_out"], w2_in=["w2_in"], w2_out=["w2_out"])
    big = ["w1_in", "w1_out", "w_in", "w_a_out", "w_b_group", "w_out", "w2_in", "w2_out"]
    cast = lambda nm: _cast_into_gathered(f"cast_{nm}", as2d(weights[nm]), q_idx)
    w1_in_gather = _TwoPartGather("w1_in", cast("w1_in"))

    b_ada_mine = lax.dynamic_slice(b_ada, (q * ada_c,), (ada_c,)).reshape(1, ada_c)
    ada_piece = _ada_fwd("ada_fwd", c_all, w_ada, b_ada_mine)
    casts = {nm: cast(nm) for nm in big[1:]}
    g1 = _allgather_small("gather_ada", ada_piece).reshape(N_DEV, N_DEV, ada_c)
    w1_in_gather.start_second()
    ici = {}
    for grp, names in groups.items():
        ici[grp] = _gather_ici(f"gather_{grp}_ici", [casts[nm] for nm in names])
        if grp == "w1_out":
            w_in_gather = _TwoPartGather("w_in", casts["w_in"])
            w_in_gather.start_second()
    ada_rows = lax.dynamic_index_in_dim(g1[0::2], dev, axis=1, keepdims=False)
    ada = ada_rows.reshape(3, 3, 1, d)
    (sh1, sc1, gt1), (sh2, sc2, gt2), (sh3, sc3, gt3) = [[ada[i, j] for j in range(3)] for i in range(3)]

    row = lambda vct: vct.reshape(1, -1)
    g1v, gmv, g2v, gfv = row(g_ffn1), row(g_mix), row(g_ffn2), row(g_final)

    def arrived(grp):
        return _gather_d2d(f"gather_{grp}_d2d", ici[grp].wait())

    def gathered(fwd, grp):
        return {nm: g.reshape(N_CHIPS, 2 * g.shape[2], g.shape[3]) for nm, g in zip(groups[grp], fwd.wait())}

    x2 = x[0]
    tgt = loss_target[0]

    n1 = _norm_mod("ffn1_norm", x2, g1v, sc1, sh1, ts)
    fwd, w1_in_parts = {}, []

    def w1_in_part(part):
        def get():
            w1_in_gather.arrive(part)
            w1_in_parts.append(w1_in_gather.ready(part))
            return w1_in_parts[-1]
        return get

    def w1_out_after_swiglu():
        fwd["w1_out"] = arrived("w1_out")
        w_in_gather.arrive(0)
        return gathered(fwd["w1_out"], "w1_out")["w1_out"].reshape(f_dim, d)

    hu1, act1, f1, w1_out_2d = _ffn_fwd("ffn1", n1, [w1_in_part(0), w1_in_part(1)], w1_out_after_swiglu, dims)
    w1_in_g = w1_in_parts[-1]
    h1, n2 = _residual_norm_mod("mix_norm", x2, f1, gt1, 0.5, gmv, sc2, sh2, ts)

    tm = _tile(s_len, 1024)
    tnp = npc // 2
    proj = ()
    for part in range(2):
        if part:
            w_in_gather.arrive(part)
        w_in_g = w_in_gather.ready(part)
        proj = (_matmul(
            f"mix_proj{part}", n2, [w_in_g], mode="nn", grid=(s_len // tm, N_CHIPS, 1),
            a_spec=pl.BlockSpec((tm, d), lambda i, j, k: (i, 0)),
            b_specs=[pl.BlockSpec((None, d, tnp), lambda i, j, k, part=part: (j, 0, part))],
            out_shape=_sds((N_CHIPS, s_len, npc), BF16),
            out_specs=pl.BlockSpec((None, tm, tnp), lambda i, j, k, part=part: (j, i, part)),
            acc_shape=(tm, tnp), epilogue=_ep_store(BF16), carry=proj),)
    proj = proj[0]
    fwd["mix"] = arrived("mix")
    cbv, lgv, lbv = row(conv_b), row(ln_a_g), row(ln_a_b)
    a3, mixed, conv_out = _mixer_mid("mix_mid", proj, cw_pad, cbv, lgv, lbv, wc, wp, ts)
    wts = gathered(fwd["mix"], "mix")
    w_out_2d = wts["w_out"].reshape(d, d)
    w_a_g = wts["w_a_out"]
    w_b_r = _regroup("regroup_w_b", wts["w_b_group"], n_groups)
    dq = d // N_CHIPS
    ya = _matmul(
        "mix_ya", a3, [w_a_g], mode="nn", grid=(s_len // tm, N_CHIPS, 1),
        a_spec=pl.BlockSpec((tm, wc), lambda i, j, k: (i, 0)),
        b_specs=[pl.BlockSpec((None, wc, dq), lambda i, j, k: (j, 0, 0))],
        out_shape=_sds((s_len, d), BF16), out_specs=pl.BlockSpec((tm, dq), lambda i, j, k: (i, j)),
        acc_shape=(tm, dq), epilogue=_ep_store(BF16))
    yb = _matmul(
        "mix_yb", mixed, [w_b_r], mode="nn", grid=(s_len // tm, n_groups, 1),
        a_spec=pl.BlockSpec((tm, gi), lambda i, j, k: (i, j)),
        b_specs=[pl.BlockSpec((None, gi, dq), lambda i, j, k: (j, 0, 0))],
        out_shape=_sds((s_len, d), BF16), out_specs=pl.BlockSpec((tm, dq), lambda i, j, k: (i, j)),
        acc_shape=(tm, dq), epilogue=_ep_store(BF16))
    bav, lsv = row(b_a_out), row(ls_b)
    z = _gates_fwd("mix_gates", proj, ya, yb, bav, bb_full, lsv, wc, wp, ts)
    tn = _tile(d, 1024)
    mix = _matmul(
        "mix_out", z, [w_out_2d], mode="nn", grid=(s_len // tm, d // tn, 1),
        a_spec=pl.BlockSpec((tm, d), lambda i, j, k: (i, 0)),
        b_specs=[pl.BlockSpec((d, tn), lambda i, j, k: (0, j))],
        out_shape=_sds((s_len, d), F32), out_specs=pl.BlockSpec((tm, tn), lambda i, j, k: (i, j)),
        acc_shape=(tm, tn), epilogue=_ep_store(F32))
    fwd["w2_in"] = arrived("w2_in")
    h2, n3 = _residual_norm_mod("ffn2_norm", h1, mix, gt2, 1.0, g2v, sc3, sh3, ts)
    w2_in_g = gathered(fwd["w2_in"], "w2_in")["w2_in"]
    hu2, act2, f3, w2_out_2d = _ffn_fwd(
        "ffn2", n3, [lambda: w2_in_g],
        lambda: gathered(arrived("w2_out"), "w2_out")["w2_out"].reshape(f_dim, d), dims)

    dh3, df3, d_gf, d_gt3, loss_cols = _final_loss("final_loss", h2, f3, tgt, gt3, 0.5, gfv, ts)
    rs, held = {}, {}
    dn3 = _ffn_bwd(
        "ffn2", n3, hu2, act2, df3, w2_in_g, w2_out_2d, dims,
        after_dw_out=lambda g: held.update(w2_out=g),
        after_dw_in=lambda g: rs.update(ffn2=_ReduceScatter("g_ffn2", ["w2_out", "w2_in"], [held["w2_out"], g],
                                                            qc_idx)))
    dh2, dmix, d_sh3, d_sc3, d_g2, d_gt2 = _norm_mod_bwd("ffn2_norm_bwd", h2, dn3, dh3, g2v, sc3, ts,
                                                         prev=(mix, gt2, 1.0))
    rs["ffn2"].step2()

    tk = s_len
    hq = d // (2 * N_CHIPS)
    gw_out = _matmul(
        "mix_dw_out", z, [dmix], mode="tn", grid=(N_CHIPS, d // tn, s_len // tk),
        a_spec=pl.BlockSpec((tk, 2 * hq), lambda i, j, k: (k, i)),
        b_specs=[pl.BlockSpec((tk, tn), lambda i, j, k: (k, j))],
        out_shape=_sds((2, N_CHIPS, hq, d), F32),
        out_specs=pl.BlockSpec((2, None, hq, tn), lambda i, j, k: (0, i, 0, j)),
        acc_shape=(2 * hq, tn), epilogue=_ep_halves(hq))
    dz = _matmul(
        "mix_dz", dmix, [w_out_2d], mode="nt", grid=(s_len // tm, d // tn, 1),
        a_spec=pl.BlockSpec((tm, d), lambda i, j, k: (i, 0)),
        b_specs=[pl.BlockSpec((tn, d), lambda i, j, k: (j, 0))],
        out_shape=_sds((s_len, d), F32), out_specs=pl.BlockSpec((tm, tn), lambda i, j, k: (i, j)),
        acc_shape=(tm, tn), epilogue=_ep_store(F32))
    dya, dyb, dgates, d_ba, d_ls, d_bb = _gates_bwd("mix_gates_bwd", proj, dz, ya, yb, bav, bb_full, lsv, wc, wp, ts)
    gw_a = _matmul(
        "mix_dw_a", a3, [dya], mode="tn", grid=(1, N_CHIPS, s_len // tk),
        a_spec=pl.BlockSpec((tk, wc), lambda i, j, k: (k, 0)),
        b_specs=[pl.BlockSpec((tk, dq), lambda i, j, k: (k, j))],
        out_shape=_sds((2, N_CHIPS, wc // 2, dq), F32),
        out_specs=pl.BlockSpec((2, None, wc // 2, dq), lambda i, j, k: (0, j, 0, 0)),
        acc_shape=(wc, dq), epilogue=_ep_halves(wc // 2))
    da3 = _matmul(
        "mix_da3", dya, [w_a_g], mode="nt", grid=(s_len // tm, 1, N_CHIPS),
        a_spec=pl.BlockSpec((tm, dq), lambda i, j, k: (i, k)),
        b_specs=[pl.BlockSpec((None, wc, dq), lambda i, j, k: (k, 0, 0))],
        out_shape=_sds((s_len, wc), F32), out_specs=pl.BlockSpec((tm, wc), lambda i, j, k: (i, 0)),
        acc_shape=(tm, wc), epilogue=_ep_store(F32))
    gpr = n_groups // 2

    def ep_by_chip(accs, ex, outs):
        for k in range(N_CHIPS):
            outs[0][k] = accs[0][:, k * goq:(k + 1) * goq]

    gw_b = _matmul(
        "mix_dw_b", mixed, [dyb], mode="tn", grid=(1, n_groups, s_len // tk),
        a_spec=pl.BlockSpec((tk, gi), lambda i, j, k: (k, j)),
        b_specs=[pl.BlockSpec((tk, dq), lambda i, j, k: (k, j))],
        out_shape=_sds((2, N_CHIPS, gpr * gi, goq), F32),
        out_specs=pl.BlockSpec((None, N_CHIPS, gi, goq), lambda i, j, k: (j // gpr, 0, j % gpr, 0)),
        acc_shape=(gi, dq), epilogue=ep_by_chip)
    dmixed = _matmul(
        "mix_dmixed", dyb, [w_b_r], mode="nt", grid=(s_len // tm, n_groups, 1),
        a_spec=pl.BlockSpec((tm, dq), lambda i, j, k: (i, j)),
        b_specs=[pl.BlockSpec((None, gi, dq), lambda i, j, k: (j, 0, 0))],
        out_shape=_sds((s_len, wp), F32), out_specs=pl.BlockSpec((tm, gi), lambda i, j, k: (i, j)),
        acc_shape=(tm, gi), epilogue=_ep_store(F32))
    da1, d_lg, d_lb, d_cb, d_cw = _conv_branch_bwd("mix_conv_bwd", proj, conv_out, da3, lgv, lbv, wc, wp, ts)
    dproj = _mixer_in_bwd("mix_in_bwd", proj, da1, dmixed, dgates, cw_pad, wc, wp, ts)
    hd = d // 2
    gw_in = _matmul(
        "mix_dw_in", n2, [dproj], mode="tn", grid=(2, 8, s_len // tk),
        a_spec=pl.BlockSpec((tk, hd), lambda i, j, k: (k, i)),
        b_specs=[pl.BlockSpec((None, tk, tnp), lambda i, j, k: (j // 2, k, j % 2))],
        out_shape=_sds((2, N_CHIPS, hd, npc), F32),
        out_specs=pl.BlockSpec((None, None, hd, tnp), lambda i, j, k: (i, j // 2, 0, j % 2)),
        acc_shape=(hd, tnp), epilogue=_ep_store(F32))
    rs["mix"] = _ReduceScatter("g_mix", ["w_in", "w_a_out", "w_b_group", "w_out"], [gw_in, gw_a, gw_b, gw_out],
                               qc_idx)
    rs["ffn2"].step3()
    dn2 = _matmul(
        "mix_dn", dproj, [w_in_g], mode="nt", grid=(s_len // tm, d // tn, N_CHIPS),
        a_spec=pl.BlockSpec((None, tm, npc), lambda i, j, k: (k, i, 0)),
        b_specs=[pl.BlockSpec((None, tn, npc), lambda i, j, k: (k, j, 0))],
        out_shape=_sds((s_len, d), F32), out_specs=pl.BlockSpec((tm, tn), lambda i, j, k: (i, j)),
        acc_shape=(tm, tn), epilogue=_ep_store(F32))
    dh1, df1, d_sh2, d_sc2, d_gm, d_gt1 = _norm_mod_bwd("mix_norm_bwd", h1, dn2, dh2, gmv, sc2, ts,
                                                        prev=(f1, gt1, 0.5))
    rs["mix"].step2()

    def w1_in_ready(g):
        rs["w1_in"] = _ReduceScatter("g_w1_in", ["w1_in"], [g], qc_idx)
        rs["w1_out"].step2()
        rs["mix"].step3()

    dn1 = _ffn_bwd(
        "ffn1", n1, hu1, act1, df1, w1_in_g, w1_out_2d, dims,
        after_dw_out=lambda g: rs.update(w1_out=_ReduceScatter("g_w1_out", ["w1_out"], [g], qc_idx)),
        after_dw_in=w1_in_ready)
    grad_x, d_sh1, d_sc1, d_g1 = _norm_mod_bwd("ffn1_norm_bwd", x2, dn1, dh1, g1v, sc1, ts)

    d_ada = jnp.concatenate([d_sh1, d_sc1, d_gt1, d_sh2, d_sc2, d_gt2, d_sh3, d_sc3, d_gt3], axis=1)
    small = [d_ada, d_g1, d_gm, d_cw[:CONV_K].reshape(1, -1), d_cb, d_lg, d_lb, d_ba, d_bb, d_ls, d_g2, d_gf,
             loss_cols]
    sizes = [a.shape[1] for a in small]
    pack1 = jnp.concatenate(small, axis=1).reshape(-1)
    n1p = -(-pack1.shape[0] // (8 * LANES)) * LANES
    pack1 = jnp.pad(pack1, (0, 8 * n1p - pack1.shape[0])).reshape(8, n1p)
    g2 = _allgather_small("gather_small_grads", pack1)
    rs["w1_in"].step2()
    total = _sum_devices("sum_small_grads", g2, 8).reshape(-1)
    offs = [0]
    for sz in sizes:
        offs.append(offs[-1] + sz)
    tot = [total[offs[k]:offs[k + 1]] for k in range(len(sizes))]
    d_ada_all = g2.reshape(N_DEV, 8 * n1p)[:, :sizes[0]]
    loss = jnp.sum(tot[12])

    grads = {}
    grads["b_ada"] = tot[0]
    grads["g_ffn1"], grads["g_mix"] = tot[1], tot[2]
    grads["conv_w"] = lax.dynamic_slice(tot[3].reshape(CONV_K, wc), (0, q * cwq), (CONV_K, cwq))
    grads["conv_b"], grads["ln_a_g"], grads["ln_a_b"], grads["b_a_out"] = tot[4], tot[5], tot[6], tot[7]
    grads["b_b_group"] = lax.dynamic_slice(tot[8].reshape(n_groups, N_CHIPS * goq), (0, q * goq), (n_groups, goq))
    grads["ls_b"], grads["g_ffn2"], grads["g_final"] = tot[9], tot[10], tot[11]

    delta, new_m, new_v = {}, {}, {}

    def adamw_group(reduced):
        for nm, g in reduced.items():
            shp = weights[nm].shape
            go, dl, mo, vo = _adamw(f"adamw_{nm}", as2d(weights[nm]), g, as2d(mom1[nm]), as2d(mom2[nm]))
            grads[nm], delta[nm], new_m[nm], new_v[nm] = go.reshape(shp), dl.reshape(shp), mo.reshape(shp), vo.reshape(shp)

    adamw_group(rs["ffn2"].result())
    rs["w1_out"].step3()
    adamw_group(rs["mix"].result())
    d_ada_mine = lax.dynamic_slice(d_ada_all, (0, q * ada_c), (N_DEV, ada_c))
    grads["w_ada"], delta["w_ada"], new_m["w_ada"], new_v["w_ada"] = _ada_grad_adamw(
        "adamw_w_ada", c_all.T, d_ada_mine, w_ada, m_w_ada, v_w_ada)
    rs["w1_in"].step3()
    smalls = [nm for nm in order if nm not in big and nm != "w_ada"]
    flat = lambda src: jnp.concatenate([src[nm].reshape(-1) for nm in smalls])
    n_small = sum(weights[nm].size for nm in smalls)
    rows_s = -(-n_small // (8 * LANES)) * 8
    packed = [jnp.pad(flat(src), (0, rows_s * LANES - n_small)).reshape(rows_s, LANES)
              for src in (weights, grads, mom1, mom2)]
    _, dl_s, mo_s, vo_s = _adamw("adamw_small", *packed)
    off = 0
    for nm in smalls:
        sz, shp = weights[nm].size, weights[nm].shape
        delta[nm] = dl_s.reshape(-1)[off:off + sz].reshape(shp)
        new_m[nm] = mo_s.reshape(-1)[off:off + sz].reshape(shp)
        new_v[nm] = vo_s.reshape(-1)[off:off + sz].reshape(shp)
        grads[nm] = grads[nm].reshape(shp)
        off += sz
    adamw_group(rs["w1_out"].result())
    adamw_group(rs["w1_in"].result())

    return (loss, grad_x[None], *[grads[nm] for nm in order], *[delta[nm] for nm in order],
            *[new_m[nm] for nm in order], *[new_v[nm] for nm in order])
```

```python
import jax
import jax.numpy as jnp
from jax import lax
from jax.experimental import pallas as pl
from jax.experimental.pallas import tpu as pltpu

F32 = jnp.float32
BF16 = jnp.bfloat16
MESH = pl.DeviceIdType.MESH
ANY = pl.BlockSpec(memory_space=pl.ANY)
HBM = pl.BlockSpec(memory_space=pltpu.HBM)
SEM = pl.BlockSpec(memory_space=pltpu.SEMAPHORE)
EFFECT = pltpu.SideEffectType.DATAFLOW_SIDE_EFFECTING

EPS = 1e-6
CONV_K = 31
HALO = 32
POOL_WINDOWS = (2, 4, 8, 16)
N_CHIPS = 4
N_DEV = 8
LANES = 128

ADAM_LR = 0.001
ADAM_B1 = 0.9
ADAM_B2 = 0.999
ADAM_EPS = 1e-08
ADAM_WD = 0.01
ADAM_STEP = 10

DN = {
    "nn": (((1,), (0,)), ((), ())),
    "nt": (((1,), (1,)), ((), ())),
    "tn": (((0,), (0,)), ((), ())),
}


_PREVIOUS = []


def _ordered(call, args, n_lead, body, token=None, sources=()):
    dep = [pltpu.with_memory_space_constraint(p, pltpu.HBM) if p.size * p.dtype.itemsize >= (1 << 20) else p
           for p in _PREVIOUS if all(p is not a for a in (*args, *sources))]

    def wrapped(*refs):
        return body(*refs[:n_lead], *refs[n_lead + len(dep):])

    outs = call(wrapped, [ANY] * len(dep))(*args, *dep)
    seq = outs if isinstance(outs, (list, tuple)) else [outs]
    _PREVIOUS[:] = [seq[token] if token is not None else
                    next(o for o in seq if jnp.issubdtype(o.dtype, jnp.floating))]
    return outs


def _pcall(body, *, name, out_shape, grid=None, in_specs=None, out_specs=None, scratch=(), aliases=None,
           prefetch=0, vmem_mb=None):
    params = {}
    if grid is not None:
        params["dimension_semantics"] = ("arbitrary",) * len(grid)
    if vmem_mb is not None:
        params["vmem_limit_bytes"] = vmem_mb << 20
    def in_hbm(shape, spec):
        big = shape.size * jnp.dtype(shape.dtype).itemsize >= (1 << 20)
        return pltpu.HBM(shape.shape, shape.dtype) if big and getattr(spec, "memory_space", None) != pltpu.VMEM else shape

    if isinstance(out_shape, (list, tuple)):
        out_shape = [in_hbm(s, sp) for s, sp in zip(out_shape, out_specs)]
    else:
        out_shape = in_hbm(out_shape, out_specs)
    kw = dict(name=name, out_shape=out_shape, compiler_params=pltpu.CompilerParams(**params))
    if aliases:
        kw["input_output_aliases"] = aliases

    def call(wrapped, dep_specs):
        specs = list(in_specs) + dep_specs
        if prefetch:
            return pl.pallas_call(wrapped, grid_spec=pltpu.PrefetchScalarGridSpec(
                num_scalar_prefetch=prefetch, grid=grid, in_specs=specs, out_specs=out_specs,
                scratch_shapes=list(scratch)), **kw)
        if grid is not None:
            return pl.pallas_call(wrapped, grid=grid, in_specs=specs, out_specs=out_specs,
                                  scratch_shapes=list(scratch), **kw)
        return pl.pallas_call(wrapped, in_specs=specs, out_specs=out_specs, scratch_shapes=list(scratch), **kw)

    def run(*args):
        specs = [None] * prefetch + list(in_specs)
        placed = [pltpu.with_memory_space_constraint(a, pltpu.HBM)
                  if a.size * a.dtype.itemsize >= (1 << 20) and getattr(s, "memory_space", None) != pltpu.VMEM else a
                  for a, s in zip(args, specs)]
        return _ordered(call, placed, prefetch + len(in_specs), body, sources=args)

    return run


def _tile(dim, pref):
    t = min(dim, pref)
    assert dim % t == 0, (dim, pref)
    return t


def _sds(shape, dtype):
    return jax.ShapeDtypeStruct(tuple(shape), dtype)


def _sigmoid(v):
    return 0.5 * jnp.tanh(0.5 * v) + 0.5


def _vec(w):
    return pl.BlockSpec((1, w), lambda *_: (0, 0))


def _acc_rows(ref, val, i):
    @pl.when(i == 0)
    def _():
        ref[...] = jnp.zeros_like(ref)

    ref[...] += jnp.sum(val, axis=0, keepdims=True)


def _matmul(name, a, bs, *, mode, grid, a_spec, b_specs, out_shape, out_specs, acc_shape, epilogue,
            extras=(), extra_specs=(), vmem_mb=56, carry=()):
    nb, ne, nk, nc = len(bs), len(extras), grid[2], len(carry)
    dn = DN[mode]

    def body(*all_refs):
        refs = all_refs[:1 + nb + ne] + all_refs[1 + nb + ne + nc:]
        a_ref, b_refs, ex = refs[0], refs[1:1 + nb], refs[1 + nb:1 + nb + ne]
        if nk == 1:
            outs = refs[1 + nb + ne:]
            accs = [lax.dot_general(a_ref[...], b[...], dn, preferred_element_type=F32) for b in b_refs]
            epilogue(accs, ex, outs)
            return
        outs, acc_refs = refs[1 + nb + ne:-nb], refs[-nb:]
        k = pl.program_id(2)

        @pl.when(k == 0)
        def _():
            for acc in acc_refs:
                acc[...] = jnp.zeros_like(acc)

        for acc, b in zip(acc_refs, b_refs):
            acc[...] += lax.dot_general(a_ref[...], b[...], dn, preferred_element_type=F32)

        @pl.when(k == nk - 1)
        def _():
            epilogue([acc[...] for acc in acc_refs], ex, outs)

    scratch = [pltpu.VMEM(acc_shape, F32) for _ in range(nb)] if nk > 1 else []
    return _pcall(body, name=name, out_shape=out_shape, grid=grid,
                  in_specs=[a_spec, *b_specs, *extra_specs, *[ANY] * nc], out_specs=out_specs, scratch=scratch,
                  aliases={1 + nb + ne + i: i for i in range(nc)}, vmem_mb=vmem_mb)(a, *bs, *extras, *carry)


def _ep_store(dtype):
    def ep(accs, ex, outs):
        outs[0][...] = accs[0].astype(dtype)
    return ep


def _ep_halves(h):
    def ep(accs, ex, outs):
        outs[0][0] = accs[0][:h]
        outs[0][1] = accs[0][h:]
    return ep


def _place():
    x, y, c = lax.axis_index("x"), lax.axis_index("y"), lax.axis_index("c")
    chips = [(1 - x, y), (x, 1 - y), (1 - x, 1 - y)]
    return x, y, c, chips


def _allgather_small(name, block):
    m_per, n = block.shape

    def body(x_ref, out_ref, send_sems, recv_sems, local_sem):
        x, y, c, chips = _place()
        me, sibling = (x, y, c), (x, y, 1 - c)

        def rows(px, py, pc):
            return out_ref.at[pl.ds((4 * px + 2 * py + pc) * m_per, m_per), :]

        def copy(k, blk, to, src=None):
            return pltpu.make_async_remote_copy(
                src_ref=rows(*blk) if src is None else src, dst_ref=rows(*blk),
                send_sem=send_sems.at[k], recv_sem=recv_sems.at[k], device_id=to, device_id_type=MESH)

        mine = pltpu.make_async_copy(x_ref, rows(*me), local_sem)
        mine.start()
        first = [copy(0, me, sibling, src=x_ref)]
        first += [copy(1 + j, me, (*chip, c), src=x_ref) for j, chip in enumerate(chips)]
        for cp in first:
            cp.start()
        passed = [copy(4 + j, (*chip, c), sibling) for j, chip in enumerate(chips)]
        for j, chip in enumerate(chips):
            copy(1 + j, (*chip, c), me).wait_recv()
            passed[j].start()
        copy(0, sibling, me).wait_recv()
        for j, chip in enumerate(chips):
            copy(4 + j, (*chip, 1 - c), me).wait_recv()
        for cp in first + passed:
            cp.wait_send()
        mine.wait()

    return _pcall(
        body, name=name, out_shape=_sds((N_DEV * m_per, n), block.dtype),
        in_specs=[pl.BlockSpec(memory_space=pltpu.VMEM)], out_specs=pl.BlockSpec(memory_space=pltpu.VMEM),
        scratch=[pltpu.SemaphoreType.DMA((7,)), pltpu.SemaphoreType.DMA((7,)), pltpu.SemaphoreType.DMA],
    )(block)


class _SplitCopies:
    def __init__(self, name, arrays, plan, n_copies):
        self.name, self.plan, self.n = name, plan, len(arrays)
        n = self.n

        def body(*refs):
            send, recv, token = refs[n], refs[n + 1], refs[-1]
            for k, (src, dst, _, peer) in enumerate(plan(refs[:n])):
                pltpu.make_async_remote_copy(src_ref=src, dst_ref=dst, send_sem=send.at[k], recv_sem=recv.at[k],
                                             device_id=peer, device_id_type=MESH).start()
            token[...] = jnp.zeros_like(token)

        def call(wrapped, dep_specs):
            return pl.pallas_call(
                wrapped, name=f"{name}_start",
                out_shape=(pltpu.SemaphoreType.DMA((n_copies,)), pltpu.SemaphoreType.DMA((n_copies,)),
                           *[pltpu.HBM(a.shape, a.dtype) for a in arrays], _sds((8, LANES), F32)),
                in_specs=[HBM] * n + dep_specs,
                out_specs=(SEM, SEM, *[HBM] * n, pl.BlockSpec(memory_space=pltpu.VMEM)),
                input_output_aliases={i: 2 + i for i in range(n)},
                compiler_params=pltpu.CompilerParams(has_side_effects=EFFECT))

        outs = _ordered(call, [pltpu.with_memory_space_constraint(a, pltpu.HBM) for a in arrays], n, body, token=-1,
                        sources=arrays)
        self.send, self.recv, self.arrays = outs[0], outs[1], list(outs[2:2 + n])

    def wait(self, arrays=None):
        n, plan = self.n, self.plan
        if arrays is not None:
            self.arrays = list(arrays)

        def body(*refs):
            send, recv, token = refs[n], refs[n + 1], refs[-1]
            for k, (src, _, landing, peer) in enumerate(plan(refs[:n])):
                cp = pltpu.make_async_remote_copy(src_ref=src, dst_ref=landing, send_sem=send.at[k],
                                                  recv_sem=recv.at[k], device_id=peer, device_id_type=MESH)
                cp.wait_send()
                cp.wait_recv()
            token[...] = jnp.zeros_like(token)

        def call(wrapped, dep_specs):
            return pl.pallas_call(
                wrapped, name=f"{self.name}_wait",
                out_shape=(*[pltpu.HBM(a.shape, a.dtype) for a in self.arrays], _sds((8, LANES), F32)),
                in_specs=[HBM] * n + [SEM, SEM] + dep_specs,
                out_specs=(*[HBM] * n, pl.BlockSpec(memory_space=pltpu.VMEM)),
                input_output_aliases={i: i for i in range(n)},
                compiler_params=pltpu.CompilerParams(has_side_effects=EFFECT))

        return list(_ordered(call, [*self.arrays, self.send, self.recv], n + 2, body, token=-1))[:n]


def _col_range(g, part, n_parts):
    width = g.shape[-1] // n_parts
    return (slice(None), pl.ds(part * width, width))


def _gather_ici(name, gathered, part=0, n_parts=1):
    def plan(refs):
        x, y, c, chips = _place()
        q = 2 * x + y
        return [(g.at[(q, c, *_col_range(g, part, n_parts))], g.at[(q, c, *_col_range(g, part, n_parts))],
                 g.at[(2 * px + py, c, *_col_range(g, part, n_parts))], (px, py, c))
                for g in refs for px, py in chips]

    return _SplitCopies(name, gathered, plan, 3 * len(gathered))


def _gather_d2d(name, gathered, part=0, n_parts=1):
    def plan(refs):
        x, y, c, chips = _place()
        return [(g.at[(2 * px + py, c, *_col_range(g, part, n_parts))],
                 g.at[(2 * px + py, c, *_col_range(g, part, n_parts))],
                 g.at[(2 * px + py, 1 - c, *_col_range(g, part, n_parts))], (x, y, 1 - c))
                for g in refs for px, py in chips]

    return _SplitCopies(name, gathered, plan, 3 * len(gathered))


class _TwoPartGather:
    def __init__(self, name, gathered):
        self.name, self.d2d = name, {}
        self.ici = [_gather_ici(f"gather_{name}_a_ici", [gathered], 0, 2)]
        self.buf = self.ici[0].arrays

    def start_second(self):
        self.ici.append(_gather_ici(f"gather_{self.name}_b_ici", self.buf, 1, 2))
        self.buf = self.ici[1].arrays

    def arrive(self, part):
        here = self.ici[part].wait(self.buf)
        self.d2d[part] = _gather_d2d(f"gather_{self.name}_{'ab'[part]}_d2d", here, part, 2)
        self.buf = self.d2d[part].arrays

    def ready(self, part):
        self.buf = self.d2d[part].wait(self.buf)
        g = self.buf[0]
        return g.reshape(N_CHIPS, 2 * g.shape[2], g.shape[3])


def _scatter_sibling(name, grads):
    n = len(grads)

    def plan(refs):
        x, y, c, _ = _place()
        return [(refs[w].at[1 - c], refs[n + w], refs[n + w], (x, y, 1 - c)) for w in range(n)]

    landing = [lax.empty(g.shape[1:], g.dtype) for g in grads]
    return _SplitCopies(name, [*grads, *landing], plan, n)


def _scatter_chips(name, sums):
    n = len(sums)

    def plan(refs):
        x, y, c, chips = _place()
        return [(refs[w].at[2 * px + py], refs[n + w].at[j], refs[n + w].at[j], (px, py, c))
                for w in range(n) for j, (px, py) in enumerate(chips)]

    landing = [lax.empty((3, *s.shape[1:]), s.dtype) for s in sums]
    return _SplitCopies(name, [*sums, *landing], plan, 3 * n)


def _share_final(name, finals):
    def plan(refs):
        x, y, c, _ = _place()
        return [(f.at[c], f.at[c], f.at[1 - c], (x, y, 1 - c)) for f in refs]

    return _SplitCopies(name, finals, plan, len(finals))


def _row_tile(rows, cols, budget_elems=786432):
    best = 8
    for t in range(8, rows + 1, 8):
        if rows % t == 0 and t * cols <= budget_elems:
            best = t
    return best if rows % best == 0 else rows


def _sum_with_sibling(name, grad, recv, qc_idx):
    _, _, h, cols = grad.shape
    tr = _row_tile(h, cols)

    def body(s_ref, g_ref, r_ref, own_ref, pb_ref):
        p = g_ref[...] + r_ref[...]
        pb_ref[...] = p.astype(BF16)

        @pl.when(pl.program_id(1) == s_ref[0])
        def _():
            own_ref[...] = p

    blk = pl.BlockSpec((None, tr, cols), lambda r, k, s: (k, r, 0))
    return _pcall(
        body, name=name, out_shape=[_sds((h, cols), F32), _sds((N_CHIPS, h, cols), BF16)],
        grid=(h // tr, N_CHIPS), prefetch=1,
        in_specs=[pl.BlockSpec((None, None, tr, cols), lambda r, k, s: (s[1], k, r, 0)), blk],
        out_specs=[pl.BlockSpec((tr, cols), lambda r, k, s: (r, 0)), blk], vmem_mb=32,
    )(qc_idx, grad, recv)


def _sum_chips(name, own, recv, qc_idx):
    h, cols = own.shape
    tr = _row_tile(h, cols)

    def body(s_ref, p_ref, t_ref, o_ref):
        o_ref[...] = ((p_ref[...] + t_ref[0].astype(F32)) + t_ref[1].astype(F32)) + t_ref[2].astype(F32)

    return _pcall(
        body, name=name, out_shape=_sds((2, h, cols), F32), grid=(h // tr,), prefetch=1,
        in_specs=[pl.BlockSpec((tr, cols), lambda r, s: (r, 0)),
                  pl.BlockSpec((3, tr, cols), lambda r, s: (0, r, 0))],
        out_specs=pl.BlockSpec((None, tr, cols), lambda r, s: (s[1], r, 0)), vmem_mb=32,
    )(qc_idx, own, recv)


class _ReduceScatter:
    def __init__(self, tag, names, grads, qc_idx):
        self.tag, self.names, self.n, self.qc_idx = tag, names, len(grads), qc_idx
        self.copies = _scatter_sibling(f"{tag}_rs_sibling", grads)

    def step2(self):
        n = self.n
        arrs = self.copies.wait()
        sums = [_sum_with_sibling(f"{nm}_sum_sibling", arrs[w], arrs[n + w], self.qc_idx)
                for w, nm in enumerate(self.names)]
        self.own = [s[0] for s in sums]
        self.copies = _scatter_chips(f"{self.tag}_rs_chips", [s[1] for s in sums])

    def step3(self):
        n = self.n
        arrs = self.copies.wait()
        finals = [_sum_chips(f"{nm}_sum_chips", self.own[w], arrs[n + w], self.qc_idx)
                  for w, nm in enumerate(self.names)]
        self.copies = _share_final(f"{self.tag}_rs_final", finals)

    def result(self):
        return {nm: f.reshape(2 * f.shape[1], f.shape[2]) for nm, f in zip(self.names, self.copies.wait())}


def _cast_into_gathered(name, w, q_idx):
    rows, cols = w.shape
    h = rows // 2
    tr = _row_tile(h, cols, 1 << 20)
    nr = h // tr

    def body(s_ref, w_ref, o_ref):
        o_ref[...] = w_ref[...].astype(BF16)

    return _pcall(body, name=name, out_shape=_sds((N_CHIPS, 2, h, cols), BF16), grid=(2, nr), prefetch=1,
                  in_specs=[pl.BlockSpec((tr, cols), lambda hf, r, s: (hf * nr + r, 0))],
                  out_specs=pl.BlockSpec((None, None, tr, cols), lambda hf, r, s: (s[0], hf, r, 0)),
                  vmem_mb=32)(q_idx, w)


def _regroup(name, w, n_groups):
    n_chips, rows, goq = w.shape
    gi = rows // n_groups

    def body(w_ref, o_ref):
        o_ref[...] = w_ref[...]

    return _pcall(body, name=name, out_shape=_sds((n_groups, gi, n_chips * goq), w.dtype), grid=(n_groups, n_chips),
                  in_specs=[pl.BlockSpec((None, gi, goq), lambda g, k: (k, g, 0))],
                  out_specs=pl.BlockSpec((None, gi, goq), lambda g, k: (g, 0, k)), vmem_mb=32)(w)


def _rms(h):
    r = lax.rsqrt(jnp.mean(h * h, axis=-1, keepdims=True) + EPS)
    return r, h * r


def _norm_mod(name, h, g, sc, sh, ts):
    s_len, d = h.shape

    def body(h_ref, g_ref, sc_ref, sh_ref, n_ref):
        _, xhat = _rms(h_ref[...])
        n_ref[...] = ((xhat * g_ref[...]) * (1.0 + sc_ref[...]) + sh_ref[...]).astype(BF16)

    row = pl.BlockSpec((ts, d), lambda i: (i, 0))
    return _pcall(body, name=name, out_shape=_sds((s_len, d), BF16), grid=(s_len // ts,),
                  in_specs=[row, _vec(d), _vec(d), _vec(d)], out_specs=row, vmem_mb=32)(h, g, sc, sh)


def _residual_norm_mod(name, h, f, gate, cmul, g, sc, sh, ts):
    s_len, d = h.shape

    def body(h_ref, f_ref, gt_ref, g_ref, sc_ref, sh_ref, ho_ref, n_ref):
        hn = h_ref[...] + (cmul * gt_ref[...]) * f_ref[...]
        ho_ref[...] = hn
        _, xhat = _rms(hn)
        n_ref[...] = ((xhat * g_ref[...]) * (1.0 + sc_ref[...]) + sh_ref[...]).astype(BF16)

    row = pl.BlockSpec((ts, d), lambda i: (i, 0))
    return _pcall(body, name=name, out_shape=[_sds((s_len, d), F32), _sds((s_len, d), BF16)],
                  grid=(s_len // ts,), in_specs=[row, row, _vec(d), _vec(d), _vec(d), _vec(d)],
                  out_specs=[row, row], vmem_mb=32)(h, f, gate, g, sc, sh)


def _final_loss(name, h, f, tgt, gate, cmul, g, ts):
    s_len, d = h.shape

    def body(h_ref, f_ref, t_ref, gt_ref, g_ref, dh_ref, df_ref, dg_ref, dgt_ref, loss_ref):
        i = pl.program_id(0)
        fv = f_ref[...]
        coef = cmul * gt_ref[...]
        hn = h_ref[...] + coef * fv
        r, xhat = _rms(hn)
        err = xhat * g_ref[...] - t_ref[...]
        _acc_rows(loss_ref, (0.5 / d) * (err * err), i)
        dy = err * (1.0 / d)
        _acc_rows(dg_ref, dy * xhat, i)
        dxhat = dy * g_ref[...]
        dh = r * (dxhat - xhat * jnp.mean(dxhat * xhat, axis=-1, keepdims=True))
        dh_ref[...] = dh
        _acc_rows(dgt_ref, cmul * (dh * fv), i)
        df_ref[...] = (coef * dh).astype(BF16)

    row = pl.BlockSpec((ts, d), lambda i: (i, 0))
    return _pcall(body, name=name,
                  out_shape=[_sds((s_len, d), F32), _sds((s_len, d), BF16)] + [_sds((1, d), F32)] * 3,
                  grid=(s_len // ts,), in_specs=[row, row, row, _vec(d), _vec(d)],
                  out_specs=[row, row, _vec(d), _vec(d), _vec(d)], vmem_mb=40)(h, f, tgt, gate, g)


def _norm_mod_bwd(name, h, dn, dh_next, g, sc, ts, prev=None):
    s_len, d = h.shape
    has_prev = prev is not None
    cmul = prev[2] if has_prev else None

    def body(*refs):
        if has_prev:
            h_ref, dn_ref, dhn_ref, f_ref, g_ref, sc_ref, gt_ref, dh_ref, df_ref, dsh_ref, dsc_ref, dg_ref, dgt_ref = refs
        else:
            h_ref, dn_ref, dhn_ref, g_ref, sc_ref, dh_ref, dsh_ref, dsc_ref, dg_ref = refs
        i = pl.program_id(0)
        r, xhat = _rms(h_ref[...])
        dn_v = dn_ref[...]
        gv = g_ref[...]
        _acc_rows(dsh_ref, dn_v, i)
        _acc_rows(dsc_ref, dn_v * (xhat * gv), i)
        dnrm = dn_v * (1.0 + sc_ref[...])
        _acc_rows(dg_ref, dnrm * xhat, i)
        dxhat = dnrm * gv
        dh = dhn_ref[...] + r * (dxhat - xhat * jnp.mean(dxhat * xhat, axis=-1, keepdims=True))
        dh_ref[...] = dh
        if has_prev:
            _acc_rows(dgt_ref, cmul * (dh * f_ref[...]), i)
            df_ref[...] = ((cmul * gt_ref[...]) * dh).astype(BF16)

    row = pl.BlockSpec((ts, d), lambda i: (i, 0))
    if has_prev:
        ins, in_specs = [h, dn, dh_next, prev[0], g, sc, prev[1]], [row, row, row, row, _vec(d), _vec(d), _vec(d)]
        out_shape = [_sds((s_len, d), F32), _sds((s_len, d), BF16)] + [_sds((1, d), F32)] * 4
        out_specs = [row, row] + [_vec(d)] * 4
    else:
        ins, in_specs = [h, dn, dh_next, g, sc], [row, row, row, _vec(d), _vec(d)]
        out_shape = [_sds((s_len, d), F32)] + [_sds((1, d), F32)] * 3
        out_specs = [row] + [_vec(d)] * 3
    return _pcall(body, name=name, out_shape=out_shape, grid=(s_len // ts,), in_specs=in_specs,
                  out_specs=out_specs, vmem_mb=40)(*ins)


def _cols(ref, lo, hi, npc, rows=slice(None)):
    parts = []
    while lo < hi:
        q, o = divmod(lo, npc)
        n = min(hi - lo, npc - o)
        parts.append(ref[q, rows, o:o + n].astype(F32))
        lo += n
    return parts[0] if len(parts) == 1 else jnp.concatenate(parts, axis=-1)


def _store_cols(ref, lo, val, npc, rows=slice(None)):
    off, width = 0, val.shape[-1]
    while off < width:
        q, o = divmod(lo + off, npc)
        n = min(width - off, npc - o)
        ref[q, rows, o:o + n] = val[:, off:off + n]
        off += n


def _chips_covering(cols, npc):
    return -(-cols // npc)


SUBLANES = 8
ROW_CHUNK = 32


def _make_phases(src_ref, ph_ref):
    rows = src_ref.shape[0] - SUBLANES
    for b in range(1, SUBLANES):
        ph_ref[b - 1] = src_ref[pl.ds(b, rows), :]


def _window(src_ref, ph_ref, off, r0, cols=slice(None)):
    a, b = divmod(off, SUBLANES)
    start = pl.multiple_of(r0 + SUBLANES * a, SUBLANES)
    if b == 0:
        return src_ref[pl.ds(start, ROW_CHUNK), cols]
    return ph_ref[b - 1, pl.ds(start, ROW_CHUNK), cols]


def _phase_scratch(rows, width):
    return pltpu.VMEM((SUBLANES - 1, rows - SUBLANES, width), F32)


def _conv(a0s_ref, a0p_ref, cw_ref, cb_ref, r0):
    a1 = cb_ref[...] + cw_ref[0:1, :] * _window(a0s_ref, a0p_ref, HALO - CONV_K + 1, r0)
    for k in range(1, CONV_K):
        a1 = a1 + cw_ref[k:k + 1, :] * _window(a0s_ref, a0p_ref, HALO - CONV_K + 1 + k, r0)
    return a1


def _layer_norm(a1, lg_ref, lb_ref):
    mu = jnp.mean(a1, axis=-1, keepdims=True)
    ctr = a1 - mu
    rstd = lax.rsqrt(jnp.mean(ctr * ctr, axis=-1, keepdims=True) + EPS)
    xh = ctr * rstd
    return xh, rstd, xh * lg_ref[...] + lb_ref[...]


def _for_chunks(ts, fn):
    def step(ci, carry):
        fn(pl.multiple_of(ci * ROW_CHUNK, ROW_CHUNK))
        return carry

    lax.fori_loop(0, ts // ROW_CHUNK, step, 0)


def _stage_glu(p_ref, ph_ref, a0s_ref, i, wc, npc, ts):
    a0 = _cols(p_ref, 0, wc, npc) * _sigmoid(_cols(p_ref, wc, 2 * wc, npc))
    a0h = _cols(ph_ref, 0, wc, npc) * _sigmoid(_cols(ph_ref, wc, 2 * wc, npc))
    a0s_ref[0:HALO, :] = jnp.where(i > 0, a0h, 0.0)
    a0s_ref[HALO:HALO + ts, :] = a0


def _mixer_mid(name, proj, cw, cb, lg, lb, wc, wp, ts):
    _, s_len, npc = proj.shape
    nq = _chips_covering(2 * wc + wp, npc)
    gi = wp // len(POOL_WINDOWS)
    hb = ts // HALO

    def body(p_ref, ph_ref, cw_ref, cb_ref, lg_ref, lb_ref, a3_ref, mx_ref, a1_ref, a0s_ref, vs_ref, a0p_ref,
             vp_ref):
        i = pl.program_id(0)
        _stage_glu(p_ref, ph_ref, a0s_ref, i, wc, npc, ts)
        vs_ref[0:HALO, :] = jnp.where(i > 0, _cols(ph_ref, 2 * wc, 2 * wc + wp, npc), 0.0)
        vs_ref[HALO:HALO + ts, :] = _cols(p_ref, 2 * wc, 2 * wc + wp, npc)
        _make_phases(a0s_ref, a0p_ref)
        _make_phases(vs_ref, vp_ref)

        def chunk(r0):
            rows = pl.ds(r0, ROW_CHUNK)
            a1 = _conv(a0s_ref, a0p_ref, cw_ref, cb_ref, r0)
            a1_ref[rows, :] = a1
            _, _, a2 = _layer_norm(a1, lg_ref, lb_ref)
            a3_ref[rows, :] = (a2 * _sigmoid(a2)).astype(BF16)
            t_abs = i * ts + r0 + lax.broadcasted_iota(jnp.int32, (ROW_CHUNK, 1), 0)
            for g, win in enumerate(POOL_WINDOWS):
                cs = slice(g * gi, (g + 1) * gi)
                v_now = _window(vs_ref, vp_ref, HALO, r0, cs)
                acc = v_now
                for dlt in range(1, win):
                    acc = acc + _window(vs_ref, vp_ref, HALO - dlt, r0, cs)
                cnt = jnp.minimum(t_abs + 1, win).astype(F32)
                mx_ref[rows, cs] = (acc / cnt - v_now).astype(BF16)

        _for_chunks(ts, chunk)

    return _pcall(
        body, name=name, out_shape=[_sds((s_len, wc), BF16), _sds((s_len, wp), BF16), _sds((s_len, wc), F32)],
        grid=(s_len // ts,),
        in_specs=[pl.BlockSpec((nq, ts, npc), lambda i: (0, i, 0)),
                  pl.BlockSpec((nq, HALO, npc), lambda i: (0, jnp.maximum(i * hb - 1, 0), 0)),
                  pl.BlockSpec((HALO, wc), lambda i: (0, 0)), _vec(wc), _vec(wc), _vec(wc)],
        out_specs=[pl.BlockSpec((ts, wc), lambda i: (i, 0)), pl.BlockSpec((ts, wp), lambda i: (i, 0)),
                   pl.BlockSpec((ts, wc), lambda i: (i, 0))],
        scratch=[pltpu.VMEM((HALO + ts, wc), F32), pltpu.VMEM((HALO + ts, wp), F32),
                 _phase_scratch(HALO + ts, wc), _phase_scratch(HALO + ts, wp)], vmem_mb=56,
    )(proj, proj, cw, cb, lg, lb)


def _gates_fwd(name, proj, ya, yb, b_a, b_b, ls, wc, wp, ts):
    _, s_len, npc = proj.shape
    d = ya.shape[1]
    g0 = 2 * wc + wp

    def body(p_ref, ya_ref, yb_ref, ba_ref, bb_ref, ls_ref, z_ref):
        ga = _sigmoid(_cols(p_ref, g0, g0 + d, npc))
        gb = _sigmoid(_cols(p_ref, g0 + d, g0 + 2 * d, npc))
        z = ga * (ya_ref[...] + ba_ref[...]) + gb * ((yb_ref[...] + bb_ref[...]) * ls_ref[...])
        z_ref[...] = z.astype(BF16)

    row = pl.BlockSpec((ts, d), lambda i: (i, 0))
    return _pcall(body, name=name, out_shape=_sds((s_len, d), BF16), grid=(s_len // ts,),
                  in_specs=[pl.BlockSpec((N_CHIPS, ts, npc), lambda i: (0, i, 0)), row, row, _vec(d), _vec(d), _vec(d)],
                  out_specs=row, vmem_mb=48)(proj, ya, yb, b_a, b_b, ls)


def _gates_bwd(name, proj, dz, ya, yb, b_a, b_b, ls, wc, wp, ts):
    _, s_len, npc = proj.shape
    d = ya.shape[1]
    g0 = 2 * wc + wp

    def body(p_ref, dz_ref, ya_ref, yb_ref, ba_ref, bb_ref, ls_ref, dya_ref, dyb_ref, dgt_ref, dba_ref, dls_ref,
             dbb_ref):
        i = pl.program_id(0)
        ga = _sigmoid(_cols(p_ref, g0, g0 + d, npc))
        gb = _sigmoid(_cols(p_ref, g0 + d, g0 + 2 * d, npc))
        dz_v = dz_ref[...]
        y_a = ya_ref[...] + ba_ref[...]
        y_b0 = yb_ref[...] + bb_ref[...]
        ls_v = ls_ref[...]
        dya = dz_v * ga
        dya_ref[...] = dya.astype(BF16)
        _acc_rows(dba_ref, dya, i)
        t = dz_v * gb
        _acc_rows(dls_ref, t * y_b0, i)
        dyb = t * ls_v
        dyb_ref[...] = dyb.astype(BF16)
        _acc_rows(dbb_ref, dyb, i)
        dgt_ref[:, 0:d] = (dz_v * y_a * ga * (1.0 - ga)).astype(BF16)
        dgt_ref[:, d:2 * d] = (dz_v * (y_b0 * ls_v) * gb * (1.0 - gb)).astype(BF16)

    row = pl.BlockSpec((ts, d), lambda i: (i, 0))
    return _pcall(
        body, name=name,
        out_shape=[_sds((s_len, d), BF16), _sds((s_len, d), BF16), _sds((s_len, 2 * d), BF16)] + [_sds((1, d), F32)] * 3,
        grid=(s_len // ts,),
        in_specs=[pl.BlockSpec((N_CHIPS, ts, npc), lambda i: (0, i, 0)), row, row, row, _vec(d), _vec(d), _vec(d)],
        out_specs=[row, row, pl.BlockSpec((ts, 2 * d), lambda i: (i, 0))] + [_vec(d)] * 3, vmem_mb=48,
    )(proj, dz, ya, yb, b_a, b_b, ls)


def _conv_branch_bwd(name, proj, a1, da3, lg, lb, wc, wp, ts):
    _, s_len, npc = proj.shape
    nq = _chips_covering(2 * wc, npc)
    hb = ts // HALO

    n_tiles = s_len // ts

    def fold(v):
        return jnp.sum(v.reshape(ROW_CHUNK // SUBLANES, SUBLANES, v.shape[-1]), axis=0)

    def body(p_ref, ph_ref, a1_ref, da3_ref, lg_ref, lb_ref, da1_ref, dlg_ref, dlb_ref, dcb_ref, dcw_ref,
             a0s_ref, a0p_ref, vec8_ref, dcw8_ref):
        i = pl.program_id(0)
        _stage_glu(p_ref, ph_ref, a0s_ref, i, wc, npc, ts)
        _make_phases(a0s_ref, a0p_ref)

        @pl.when(i == 0)
        def _():
            vec8_ref[...] = jnp.zeros_like(vec8_ref)
            dcw8_ref[...] = jnp.zeros_like(dcw8_ref)

        def chunk(r0):
            rows = pl.ds(r0, ROW_CHUNK)
            xh, rstd, a2 = _layer_norm(a1_ref[rows, :], lg_ref, lb_ref)
            sig = _sigmoid(a2)
            da2 = da3_ref[rows, :] * (sig * (1.0 + a2 * (1.0 - sig)))
            vec8_ref[0] += fold(da2 * xh)
            vec8_ref[1] += fold(da2)
            dxh = da2 * lg_ref[...]
            da1 = rstd * (dxh - jnp.mean(dxh, axis=-1, keepdims=True)
                          - xh * jnp.mean(dxh * xh, axis=-1, keepdims=True))
            da1_ref[rows, :] = da1
            vec8_ref[2] += fold(da1)
            for k in range(CONV_K):
                dcw8_ref[k] += fold(da1 * _window(a0s_ref, a0p_ref, HALO - CONV_K + 1 + k, r0))

        _for_chunks(ts, chunk)

        @pl.when(i == n_tiles - 1)
        def _():
            dlg_ref[...] = jnp.sum(vec8_ref[0], axis=0, keepdims=True)
            dlb_ref[...] = jnp.sum(vec8_ref[1], axis=0, keepdims=True)
            dcb_ref[...] = jnp.sum(vec8_ref[2], axis=0, keepdims=True)
            dcw_ref[...] = jnp.sum(dcw8_ref[...], axis=1)

    return _pcall(
        body, name=name,
        out_shape=[_sds((s_len, wc), F32)] + [_sds((1, wc), F32)] * 3 + [_sds((HALO, wc), F32)],
        grid=(s_len // ts,),
        in_specs=[pl.BlockSpec((nq, ts, npc), lambda i: (0, i, 0)),
                  pl.BlockSpec((nq, HALO, npc), lambda i: (0, jnp.maximum(i * hb - 1, 0), 0)),
                  pl.BlockSpec((ts, wc), lambda i: (i, 0)), pl.BlockSpec((ts, wc), lambda i: (i, 0)),
                  _vec(wc), _vec(wc)],
        out_specs=[pl.BlockSpec((ts, wc), lambda i: (i, 0)), _vec(wc), _vec(wc), _vec(wc),
                   pl.BlockSpec((HALO, wc), lambda i: (0, 0))],
        scratch=[pltpu.VMEM((HALO + ts, wc), F32), _phase_scratch(HALO + ts, wc),
                 pltpu.VMEM((3, SUBLANES, wc), F32), pltpu.VMEM((HALO, SUBLANES, wc), F32)], vmem_mb=56,
    )(proj, proj, a1, da3, lg, lb)


def _mixer_in_bwd(name, proj, da1, dmixed, dgates, cw, wc, wp, ts):
    _, s_len, npc = proj.shape
    nq = _chips_covering(2 * wc, npc)
    gi = wp // len(POOL_WINDOWS)
    hb = ts // HALO
    n_tiles = s_len // ts
    last_hb = s_len // HALO - 1
    d2 = dgates.shape[1]

    def body(p_ref, d1_ref, d1n_ref, dm_ref, dmn_ref, dgt_ref, cw_ref, o_ref, d1s_ref, es_ref, d1p_ref, ep_ref):
        i = pl.program_id(0)
        more = i < n_tiles - 1
        d1s_ref[0:ts, :] = d1_ref[...]
        d1s_ref[ts:ts + HALO, :] = jnp.where(more, d1n_ref[...], 0.0)
        t_abs = i * ts + lax.broadcasted_iota(jnp.int32, (ts + HALO, 1), 0)
        dm_ext = jnp.concatenate([dm_ref[...], jnp.where(more, dmn_ref[...], 0.0)], axis=0)
        for g, win in enumerate(POOL_WINDOWS):
            cs = slice(g * gi, (g + 1) * gi)
            es_ref[:, cs] = dm_ext[:, cs] / jnp.minimum(t_abs + 1, win).astype(F32)
        _make_phases(d1s_ref, d1p_ref)
        _make_phases(es_ref, ep_ref)

        def chunk(r0):
            rows = pl.ds(r0, ROW_CHUNK)
            da0 = cw_ref[0:1, :] * _window(d1s_ref, d1p_ref, CONV_K - 1, r0)
            for k in range(1, CONV_K):
                da0 = da0 + cw_ref[k:k + 1, :] * _window(d1s_ref, d1p_ref, CONV_K - 1 - k, r0)
            glu_a = _cols(p_ref, 0, wc, npc, rows)
            sig = _sigmoid(_cols(p_ref, wc, 2 * wc, npc, rows))
            _store_cols(o_ref, 0, (da0 * sig).astype(BF16), npc, rows)
            _store_cols(o_ref, wc, (da0 * glu_a * sig * (1.0 - sig)).astype(BF16), npc, rows)
            parts = []
            for g, win in enumerate(POOL_WINDOWS):
                cs = slice(g * gi, (g + 1) * gi)
                acc = _window(es_ref, ep_ref, 0, r0, cs)
                for dlt in range(1, win):
                    acc = acc + _window(es_ref, ep_ref, dlt, r0, cs)
                parts.append(acc - dm_ref[rows, cs])
            _store_cols(o_ref, 2 * wc, jnp.concatenate(parts, axis=-1).astype(BF16), npc, rows)

        _for_chunks(ts, chunk)
        _store_cols(o_ref, 2 * wc + wp, dgt_ref[...], npc)

    nxt = lambda i: (jnp.minimum((i + 1) * hb, last_hb), 0)
    return _pcall(
        body, name=name, out_shape=_sds((N_CHIPS, s_len, npc), BF16), grid=(n_tiles,),
        in_specs=[pl.BlockSpec((nq, ts, npc), lambda i: (0, i, 0)),
                  pl.BlockSpec((ts, wc), lambda i: (i, 0)), pl.BlockSpec((HALO, wc), nxt),
                  pl.BlockSpec((ts, wp), lambda i: (i, 0)), pl.BlockSpec((HALO, wp), nxt),
                  pl.BlockSpec((ts, d2), lambda i: (i, 0)),
                  pl.BlockSpec((HALO, wc), lambda i: (0, 0))],
        out_specs=pl.BlockSpec((N_CHIPS, ts, npc), lambda i: (0, i, 0)),
        scratch=[pltpu.VMEM((ts + HALO, wc), F32), pltpu.VMEM((ts + HALO, wp), F32),
                 _phase_scratch(ts + HALO, wc), _phase_scratch(ts + HALO, wp)], vmem_mb=56,
    )(proj, da1, da1, dmixed, dmixed, dgates, cw)


def _ada_fwd(name, c_all, w, b):
    d, cols = w.shape
    tn = 512 if cols % 512 == 0 else cols

    def body(c_ref, w_ref, b_ref, o_ref):
        cv = c_ref[...]
        sc = (cv * _sigmoid(cv)).astype(BF16)
        o_ref[...] = jnp.dot(sc, w_ref[...].astype(BF16), preferred_element_type=F32) + b_ref[...]

    return _pcall(body, name=name, out_shape=_sds((N_DEV, cols), F32), grid=(cols // tn,),
                  in_specs=[pl.BlockSpec((N_DEV, d), lambda j: (0, 0)), pl.BlockSpec((d, tn), lambda j: (0, j)),
                            pl.BlockSpec((1, tn), lambda j: (0, j))],
                  out_specs=pl.BlockSpec((N_DEV, tn), lambda j: (0, j)), vmem_mb=32)(c_all, w, b)


def _adam_math(w, g, m, v):
    m_new = ADAM_B1 * m + (1.0 - ADAM_B1) * g
    v_new = ADAM_B2 * v + (1.0 - ADAM_B2) * (g * g)
    m_hat = m_new / (1.0 - ADAM_B1 ** ADAM_STEP)
    v_hat = v_new / (1.0 - ADAM_B2 ** ADAM_STEP)
    delta = -ADAM_LR * (m_hat / (jnp.sqrt(v_hat) + ADAM_EPS) + ADAM_WD * w)
    return delta, m_new, v_new


def _adamw(name, w, g, m, v):
    rows, cols = w.shape
    tr = _row_tile(rows, cols, 524288)

    def body(w_ref, g_ref, m_ref, v_ref, go_ref, d_ref, mo_ref, vo_ref):
        g = g_ref[...]
        go_ref[...] = g
        d_ref[...], mo_ref[...], vo_ref[...] = _adam_math(w_ref[...], g, m_ref[...], v_ref[...])

    spec = pl.BlockSpec((tr, cols), lambda i: (i, 0))
    return _pcall(body, name=name, out_shape=[_sds(w.shape, F32)] * 4, grid=(rows // tr,), in_specs=[spec] * 4,
                  out_specs=[spec] * 4, vmem_mb=40)(w, g, m, v)


def _ada_grad_adamw(name, c_t, d_ada, w, m, v):
    rows, cols = w.shape
    tr = _tile(rows, 256)
    tc = _tile(cols, 1536) if cols % 1536 == 0 else cols

    def body(c_ref, da_ref, w_ref, m_ref, v_ref, g_ref, d_ref, mo_ref, vo_ref):
        cv = c_ref[...]
        sc = cv * _sigmoid(cv)
        g = sc[:, 0:1] * da_ref[0:1, :]
        for b in range(1, N_DEV):
            g = g + sc[:, b:b + 1] * da_ref[b:b + 1, :]
        g_ref[...] = g
        d_ref[...], mo_ref[...], vo_ref[...] = _adam_math(w_ref[...], g, m_ref[...], v_ref[...])

    spec = pl.BlockSpec((tr, tc), lambda i, j: (i, j))
    return _pcall(body, name=name, out_shape=[_sds(w.shape, F32)] * 4, grid=(rows // tr, cols // tc),
                  in_specs=[pl.BlockSpec((tr, N_DEV), lambda i, j: (i, 0)),
                            pl.BlockSpec((N_DEV, tc), lambda i, j: (0, j)), spec, spec, spec],
                  out_specs=[spec] * 4, vmem_mb=40)(c_t, d_ada, w, m, v)


def _sum_devices(name, gathered, m_per):
    n = gathered.shape[1]

    def body(g_ref, o_ref):
        acc = g_ref[0:m_per, :]
        for dev in range(1, N_DEV):
            acc = acc + g_ref[dev * m_per:(dev + 1) * m_per, :]
        o_ref[...] = acc

    return _pcall(body, name=name, out_shape=_sds((m_per, n), F32),
                  in_specs=[pl.BlockSpec(memory_space=pltpu.VMEM)],
                  out_specs=pl.BlockSpec(memory_space=pltpu.VMEM))(gathered)


def _ffn_fwd(tag, n, w_in_parts, w_out_after_swiglu, dims):
    s_len, d, f_dim = dims["S"], dims["D"], dims["F"]
    tf = f_dim // 4
    tm0, tm = _tile(s_len, 512), _tile(s_len, 1024)
    n_parts = len(w_in_parts)
    nbp = (f_dim // 2) // tf
    nbq = nbp // n_parts

    def ep(accs, ex, outs):
        hh, uu = accs
        sig = _sigmoid(hh)
        silu = hh * sig
        outs[0][0] = (uu * (sig + silu * (1.0 - sig))).astype(BF16)
        outs[0][1] = silu.astype(BF16)
        outs[1][...] = (silu * uu).astype(BF16)

    done = ()
    for part, get_w in enumerate(w_in_parts):
        w_g = get_w()
        col = lambda j, part=part: (j // nbq) * nbp + part * nbq + j % nbq
        done = _matmul(
            f"{tag}_swiglu{part}", n, [w_g, w_g], mode="nn", grid=(2 * nbq, s_len // tm0, 1),
            a_spec=pl.BlockSpec((tm0, d), lambda j, i, k: (i, 0)),
            b_specs=[pl.BlockSpec((None, d, tf), lambda j, i, k, part=part: (j // nbq, 0, part * nbq + j % nbq)),
                     pl.BlockSpec((None, d, tf), lambda j, i, k, part=part: (2 + j // nbq, 0, part * nbq + j % nbq))],
            out_shape=[_sds((2, s_len, f_dim), BF16), _sds((s_len, f_dim), BF16)],
            out_specs=[pl.BlockSpec((2, tm0, tf), lambda j, i, k, col=col: (0, i, col(j))),
                       pl.BlockSpec((tm0, tf), lambda j, i, k, col=col: (i, col(j)))],
            acc_shape=(tm0, tf), epilogue=ep, carry=done)
    hu, act = done
    w_out2d = w_out_after_swiglu()
    tn2 = _tile(d, 1024)
    f = _matmul(
        f"{tag}_down", act, [w_out2d], mode="nn", grid=(s_len // tm, d // tn2, 2),
        a_spec=pl.BlockSpec((tm, 2 * tf), lambda i, j, k: (i, k)),
        b_specs=[pl.BlockSpec((2 * tf, tn2), lambda i, j, k: (k, j))],
        out_shape=_sds((s_len, d), F32), out_specs=pl.BlockSpec((tm, tn2), lambda i, j, k: (i, j)),
        acc_shape=(tm, tn2), epilogue=_ep_store(F32))
    return hu, act, f, w_out2d


def _ffn_bwd(tag, n, hu, act, df, w_in_g, w_out2d, dims, after_dw_out, after_dw_in):
    s_len, d, f_dim = dims["S"], dims["D"], dims["F"]
    tf = f_dim // 4
    tk = _tile(s_len, 2048)
    tn = _tile(d, 1024)
    g_out = _matmul(
        f"{tag}_dw_out", act, [df], mode="tn", grid=(4, d // tn, s_len // tk),
        a_spec=pl.BlockSpec((tk, tf), lambda i, j, k: (k, i)),
        b_specs=[pl.BlockSpec((tk, tn), lambda i, j, k: (k, j))],
        out_shape=_sds((2, 4, tf // 2, d), F32),
        out_specs=pl.BlockSpec((2, None, tf // 2, tn), lambda i, j, k: (0, i, 0, j)),
        acc_shape=(tf, tn), epilogue=_ep_halves(tf // 2))
    after_dw_out(g_out)

    def ep_dhu(accs, ex, outs):
        da = accs[0]
        outs[0][0] = (da * ex[0][0].astype(F32)).astype(BF16)
        outs[0][1] = (da * ex[0][1].astype(F32)).astype(BF16)

    tm = _tile(s_len, 512)
    hu_spec = pl.BlockSpec((2, tm, tf), lambda j, i, k: (0, i, j))
    dhu = _matmul(
        f"{tag}_dhu", df, [w_out2d], mode="nt", grid=(4, s_len // tm, 1),
        a_spec=pl.BlockSpec((tm, d), lambda j, i, k: (i, 0)),
        b_specs=[pl.BlockSpec((tf, d), lambda j, i, k: (j, 0))],
        extras=[hu], extra_specs=[hu_spec],
        out_shape=_sds((2, s_len, f_dim), BF16), out_specs=hu_spec, acc_shape=(tm, tf), epilogue=ep_dhu)

    hd = d // 2
    g_in = _matmul(
        f"{tag}_dw_in", n, [dhu], mode="tn", grid=(2, 8, s_len // tk),
        a_spec=pl.BlockSpec((tk, hd), lambda i, j, k: (k, i)),
        b_specs=[pl.BlockSpec((None, tk, tf), lambda i, j, k: (j // 4, k, j % 4))],
        out_shape=_sds((2, 4, hd, f_dim // 2), F32),
        out_specs=pl.BlockSpec((None, None, hd, tf), lambda i, j, k: (i, j // 2, 0, j % 2)),
        acc_shape=(hd, tf), epilogue=_ep_store(F32))
    after_dw_in(g_in)

    tm2 = _tile(s_len, 1024)
    dn = _matmul(
        f"{tag}_dn", dhu, [w_in_g], mode="nt", grid=(s_len // tm2, d // tn, N_CHIPS),
        a_spec=pl.BlockSpec((None, tm2, 2 * tf), lambda i, j, k: (k // 2, i, k % 2)),
        b_specs=[pl.BlockSpec((None, tn, 2 * tf), lambda i, j, k: (k, j, 0))],
        out_shape=_sds((s_len, d), F32), out_specs=pl.BlockSpec((tm2, tn), lambda i, j, k: (i, j)),
        acc_shape=(tm2, tn), epilogue=_ep_store(F32))
    return dn


def kernel(x, c, w_ada, b_ada, g_ffn1, w1_in, w1_out, g_mix, w_in, conv_w, conv_b, ln_a_g, ln_a_b, w_a_out, b_a_out, w_b_group, b_b_group, ls_b, w_out, g_ffn2, w2_in, w2_out, g_final, loss_target, m_w_ada, m_b_ada, m_g_ffn1, m_w1_in, m_w1_out, m_g_mix, m_w_in, m_conv_w, m_conv_b, m_ln_a_g, m_ln_a_b, m_w_a_out, m_b_a_out, m_w_b_group, m_b_b_group, m_ls_b, m_w_out, m_g_ffn2, m_w2_in, m_w2_out, m_g_final, v_w_ada, v_b_ada, v_g_ffn1, v_w1_in, v_w1_out, v_g_mix, v_w_in, v_conv_w, v_conv_b, v_ln_a_g, v_ln_a_b, v_w_a_out, v_b_a_out, v_w_b_group, v_b_b_group, v_ls_b, v_w_out, v_g_ffn2, v_w2_in, v_w2_out, v_g_final):
    weights = dict(w_ada=w_ada, b_ada=b_ada, g_ffn1=g_ffn1, w1_in=w1_in, w1_out=w1_out, g_mix=g_mix, w_in=w_in,
                   conv_w=conv_w, conv_b=conv_b, ln_a_g=ln_a_g, ln_a_b=ln_a_b, w_a_out=w_a_out, b_a_out=b_a_out,
                   w_b_group=w_b_group, b_b_group=b_b_group, ls_b=ls_b, w_out=w_out, g_ffn2=g_ffn2, w2_in=w2_in,
                   w2_out=w2_out, g_final=g_final)
    mom1 = dict(w_ada=m_w_ada, b_ada=m_b_ada, g_ffn1=m_g_ffn1, w1_in=m_w1_in, w1_out=m_w1_out, g_mix=m_g_mix,
                w_in=m_w_in, conv_w=m_conv_w, conv_b=m_conv_b, ln_a_g=m_ln_a_g, ln_a_b=m_ln_a_b, w_a_out=m_w_a_out,
                b_a_out=m_b_a_out, w_b_group=m_w_b_group, b_b_group=m_b_b_group, ls_b=m_ls_b, w_out=m_w_out,
                g_ffn2=m_g_ffn2, w2_in=m_w2_in, w2_out=m_w2_out, g_final=m_g_final)
    mom2 = dict(w_ada=v_w_ada, b_ada=v_b_ada, g_ffn1=v_g_ffn1, w1_in=v_w1_in, w1_out=v_w1_out, g_mix=v_g_mix,
                w_in=v_w_in, conv_w=v_conv_w, conv_b=v_conv_b, ln_a_g=v_ln_a_g, ln_a_b=v_ln_a_b, w_a_out=v_w_a_out,
                b_a_out=v_b_a_out, w_b_group=v_w_b_group, b_b_group=v_b_b_group, ls_b=v_ls_b, w_out=v_w_out,
                g_ffn2=v_g_ffn2, w2_in=v_w2_in, w2_out=v_w2_out, g_final=v_g_final)
    order = list(weights)

    s_len, d = x.shape[1], x.shape[2]
    f_dim = w1_out.shape[0] * N_CHIPS
    wc = conv_w.shape[1] * N_CHIPS
    wp = w_b_group.shape[0] * w_b_group.shape[1]
    n_groups, gi, goq = w_b_group.shape
    npc = w_in.shape[1]
    ada_c = w_ada.shape[1]
    dims = dict(S=s_len, D=d, F=f_dim)
    ts = _tile(s_len, 256)

    xi, yi, ci = lax.axis_index("x"), lax.axis_index("y"), lax.axis_index("c")
    q = 2 * xi + yi
    dev = 2 * q + ci
    q_idx = jnp.reshape(q, (1,)).astype(jnp.int32)
    qc_idx = jnp.stack([q, ci]).astype(jnp.int32)
    _PREVIOUS.clear()

    cwq = conv_w.shape[1]
    pack0 = jnp.concatenate([c.reshape(-1), conv_w.reshape(-1), b_b_group.reshape(-1)])
    n0 = -(-pack0.shape[0] // (8 * LANES)) * LANES
    pack0 = jnp.pad(pack0, (0, 8 * n0 - pack0.shape[0])).reshape(8, n0)
    g0 = _allgather_small("gather_small_in", pack0).reshape(N_DEV, 8 * n0)
    c_all = g0[:, :d]
    south = g0[0::2]
    cw_full = jnp.concatenate([south[k, d:d + CONV_K * cwq].reshape(CONV_K, cwq) for k in range(N_CHIPS)], axis=1)
    cw_pad = jnp.pad(cw_full, ((0, HALO - CONV_K), (0, 0)))
    o_bb = d + CONV_K * cwq
    bb_full = jnp.concatenate([south[k, o_bb:o_bb + n_groups * goq].reshape(n_groups, goq) for k in range(N_CHIPS)],
                              axis=1).reshape(1, d)

    as2d = lambda a: a.reshape(-1, a.shape[-1])
    groups = dict(w1_out=["w1_out"], mix=["w_a_out", "w_b_group", "w_out"], w2_in=["w2_in"], w2_out=["w2_out"])
    big = ["w1_in", "w1_out", "w_in", "w_a_out", "w_b_group", "w_out", "w2_in", "w2_out"]
    cast = lambda nm: _cast_into_gathered(f"cast_{nm}", as2d(weights[nm]), q_idx)
    w1_in_cast = cast("w1_in")

    b_ada_mine = lax.dynamic_slice(b_ada, (q * ada_c,), (ada_c,)).reshape(1, ada_c)
    ada_piece = _ada_fwd("ada_fwd", c_all, w_ada, b_ada_mine)
    g1 = _allgather_small("gather_ada", ada_piece).reshape(N_DEV, N_DEV, ada_c)
    w1_in_gather = _TwoPartGather("w1_in", w1_in_cast)
    w1_in_gather.start_second()
    casts = {nm: cast(nm) for nm in big[1:]}
    ici = {}
    for grp, names in groups.items():
        ici[grp] = _gather_ici(f"gather_{grp}_ici", [casts[nm] for nm in names])
        if grp == "w1_out":
            w_in_gather = _TwoPartGather("w_in", casts["w_in"])
            w_in_gather.start_second()
    ada_rows = lax.dynamic_index_in_dim(g1[0::2], dev, axis=1, keepdims=False)
    ada = ada_rows.reshape(3, 3, 1, d)
    (sh1, sc1, gt1), (sh2, sc2, gt2), (sh3, sc3, gt3) = [[ada[i, j] for j in range(3)] for i in range(3)]

    row = lambda vct: vct.reshape(1, -1)
    g1v, gmv, g2v, gfv = row(g_ffn1), row(g_mix), row(g_ffn2), row(g_final)

    def arrived(grp):
        return _gather_d2d(f"gather_{grp}_d2d", ici[grp].wait())

    def gathered(fwd, grp):
        return {nm: g.reshape(N_CHIPS, 2 * g.shape[2], g.shape[3]) for nm, g in zip(groups[grp], fwd.wait())}

    x2 = x[0]
    tgt = loss_target[0]

    n1 = _norm_mod("ffn1_norm", x2, g1v, sc1, sh1, ts)
    fwd, w1_in_parts = {}, []

    def w1_in_part(part):
        def get():
            w1_in_gather.arrive(part)
            w1_in_parts.append(w1_in_gather.ready(part))
            return w1_in_parts[-1]
        return get

    def w1_out_after_swiglu():
        fwd["w1_out"] = arrived("w1_out")
        w_in_gather.arrive(0)
        return gathered(fwd["w1_out"], "w1_out")["w1_out"].reshape(f_dim, d)

    hu1, act1, f1, w1_out_2d = _ffn_fwd("ffn1", n1, [w1_in_part(0), w1_in_part(1)], w1_out_after_swiglu, dims)
    w1_in_g = w1_in_parts[-1]
    h1, n2 = _residual_norm_mod("mix_norm", x2, f1, gt1, 0.5, gmv, sc2, sh2, ts)

    tm = _tile(s_len, 1024)
    tnp = npc // 2
    proj = ()
    for part in range(2):
        if part:
            w_in_gather.arrive(part)
        w_in_g = w_in_gather.ready(part)
        proj = (_matmul(
            f"mix_proj{part}", n2, [w_in_g], mode="nn", grid=(s_len // tm, N_CHIPS, 1),
            a_spec=pl.BlockSpec((tm, d), lambda i, j, k: (i, 0)),
            b_specs=[pl.BlockSpec((None, d, tnp), lambda i, j, k, part=part: (j, 0, part))],
            out_shape=_sds((N_CHIPS, s_len, npc), BF16),
            out_specs=pl.BlockSpec((None, tm, tnp), lambda i, j, k, part=part: (j, i, part)),
            acc_shape=(tm, tnp), epilogue=_ep_store(BF16), carry=proj),)
    proj = proj[0]
    fwd["mix"] = arrived("mix")
    cbv, lgv, lbv = row(conv_b), row(ln_a_g), row(ln_a_b)
    a3, mixed, conv_out = _mixer_mid("mix_mid", proj, cw_pad, cbv, lgv, lbv, wc, wp, ts)
    wts = gathered(fwd["mix"], "mix")
    w_out_2d = wts["w_out"].reshape(d, d)
    w_a_g = wts["w_a_out"]
    w_b_r = _regroup("regroup_w_b", wts["w_b_group"], n_groups)
    dq = d // N_CHIPS
    ya = _matmul(
        "mix_ya", a3, [w_a_g], mode="nn", grid=(s_len // tm, N_CHIPS, 1),
        a_spec=pl.BlockSpec((tm, wc), lambda i, j, k: (i, 0)),
        b_specs=[pl.BlockSpec((None, wc, dq), lambda i, j, k: (j, 0, 0))],
        out_shape=_sds((s_len, d), BF16), out_specs=pl.BlockSpec((tm, dq), lambda i, j, k: (i, j)),
        acc_shape=(tm, dq), epilogue=_ep_store(BF16))
    yb = _matmul(
        "mix_yb", mixed, [w_b_r], mode="nn", grid=(s_len // tm, n_groups, 1),
        a_spec=pl.BlockSpec((tm, gi), lambda i, j, k: (i, j)),
        b_specs=[pl.BlockSpec((None, gi, dq), lambda i, j, k: (j, 0, 0))],
        out_shape=_sds((s_len, d), BF16), out_specs=pl.BlockSpec((tm, dq), lambda i, j, k: (i, j)),
        acc_shape=(tm, dq), epilogue=_ep_store(BF16))
    bav, lsv = row(b_a_out), row(ls_b)
    z = _gates_fwd("mix_gates", proj, ya, yb, bav, bb_full, lsv, wc, wp, ts)
    tn = _tile(d, 1024)
    mix = _matmul(
        "mix_out", z, [w_out_2d], mode="nn", grid=(s_len // tm, d // tn, 1),
        a_spec=pl.BlockSpec((tm, d), lambda i, j, k: (i, 0)),
        b_specs=[pl.BlockSpec((d, tn), lambda i, j, k: (0, j))],
        out_shape=_sds((s_len, d), F32), out_specs=pl.BlockSpec((tm, tn), lambda i, j, k: (i, j)),
        acc_shape=(tm, tn), epilogue=_ep_store(F32))
    fwd["w2_in"] = arrived("w2_in")
    h2, n3 = _residual_norm_mod("ffn2_norm", h1, mix, gt2, 1.0, g2v, sc3, sh3, ts)
    w2_in_g = gathered(fwd["w2_in"], "w2_in")["w2_in"]
    hu2, act2, f3, w2_out_2d = _ffn_fwd(
        "ffn2", n3, [lambda: w2_in_g],
        lambda: gathered(arrived("w2_out"), "w2_out")["w2_out"].reshape(f_dim, d), dims)

    dh3, df3, d_gf, d_gt3, loss_cols = _final_loss("final_loss", h2, f3, tgt, gt3, 0.5, gfv, ts)
    rs, held = {}, {}
    dn3 = _ffn_bwd(
        "ffn2", n3, hu2, act2, df3, w2_in_g, w2_out_2d, dims,
        after_dw_out=lambda g: held.update(w2_out=g),
        after_dw_in=lambda g: rs.update(ffn2=_ReduceScatter("g_ffn2", ["w2_out", "w2_in"], [held["w2_out"], g],
                                                            qc_idx)))
    dh2, dmix, d_sh3, d_sc3, d_g2, d_gt2 = _norm_mod_bwd("ffn2_norm_bwd", h2, dn3, dh3, g2v, sc3, ts,
                                                         prev=(mix, gt2, 1.0))
    rs["ffn2"].step2()

    tk = s_len
    hq = d // (2 * N_CHIPS)
    gw_out = _matmul(
        "mix_dw_out", z, [dmix], mode="tn", grid=(N_CHIPS, d // tn, s_len // tk),
        a_spec=pl.BlockSpec((tk, 2 * hq), lambda i, j, k: (k, i)),
        b_specs=[pl.BlockSpec((tk, tn), lambda i, j, k: (k, j))],
        out_shape=_sds((2, N_CHIPS, hq, d), F32),
        out_specs=pl.BlockSpec((2, None, hq, tn), lambda i, j, k: (0, i, 0, j)),
        acc_shape=(2 * hq, tn), epilogue=_ep_halves(hq))
    dz = _matmul(
        "mix_dz", dmix, [w_out_2d], mode="nt", grid=(s_len // tm, d // tn, 1),
        a_spec=pl.BlockSpec((tm, d), lambda i, j, k: (i, 0)),
        b_specs=[pl.BlockSpec((tn, d), lambda i, j, k: (j, 0))],
        out_shape=_sds((s_len, d), F32), out_specs=pl.BlockSpec((tm, tn), lambda i, j, k: (i, j)),
        acc_shape=(tm, tn), epilogue=_ep_store(F32))
    dya, dyb, dgates, d_ba, d_ls, d_bb = _gates_bwd("mix_gates_bwd", proj, dz, ya, yb, bav, bb_full, lsv, wc, wp, ts)
    gw_a = _matmul(
        "mix_dw_a", a3, [dya], mode="tn", grid=(1, N_CHIPS, s_len // tk),
        a_spec=pl.BlockSpec((tk, wc), lambda i, j, k: (k, 0)),
        b_specs=[pl.BlockSpec((tk, dq), lambda i, j, k: (k, j))],
        out_shape=_sds((2, N_CHIPS, wc // 2, dq), F32),
        out_specs=pl.BlockSpec((2, None, wc // 2, dq), lambda i, j, k: (0, j, 0, 0)),
        acc_shape=(wc, dq), epilogue=_ep_halves(wc // 2))
    da3 = _matmul(
        "mix_da3", dya, [w_a_g], mode="nt", grid=(s_len // tm, 1, N_CHIPS),
        a_spec=pl.BlockSpec((tm, dq), lambda i, j, k: (i, k)),
        b_specs=[pl.BlockSpec((None, wc, dq), lambda i, j, k: (k, 0, 0))],
        out_shape=_sds((s_len, wc), F32), out_specs=pl.BlockSpec((tm, wc), lambda i, j, k: (i, 0)),
        acc_shape=(tm, wc), epilogue=_ep_store(F32))
    gpr = n_groups // 2

    def ep_by_chip(accs, ex, outs):
        for k in range(N_CHIPS):
            outs[0][k] = accs[0][:, k * goq:(k + 1) * goq]

    gw_b = _matmul(
        "mix_dw_b", mixed, [dyb], mode="tn", grid=(1, n_groups, s_len // tk),
        a_spec=pl.BlockSpec((tk, gi), lambda i, j, k: (k, j)),
        b_specs=[pl.BlockSpec((tk, dq), lambda i, j, k: (k, j))],
        out_shape=_sds((2, N_CHIPS, gpr * gi, goq), F32),
        out_specs=pl.BlockSpec((None, N_CHIPS, gi, goq), lambda i, j, k: (j // gpr, 0, j % gpr, 0)),
        acc_shape=(gi, dq), epilogue=ep_by_chip)
    dmixed = _matmul(
        "mix_dmixed", dyb, [w_b_r], mode="nt", grid=(s_len // tm, n_groups, 1),
        a_spec=pl.BlockSpec((tm, dq), lambda i, j, k: (i, j)),
        b_specs=[pl.BlockSpec((None, gi, dq), lambda i, j, k: (j, 0, 0))],
        out_shape=_sds((s_len, wp), F32), out_specs=pl.BlockSpec((tm, gi), lambda i, j, k: (i, j)),
        acc_shape=(tm, gi), epilogue=_ep_store(F32))
    da1, d_lg, d_lb, d_cb, d_cw = _conv_branch_bwd("mix_conv_bwd", proj, conv_out, da3, lgv, lbv, wc, wp, ts)
    dproj = _mixer_in_bwd("mix_in_bwd", proj, da1, dmixed, dgates, cw_pad, wc, wp, ts)
    hd = d // 2
    gw_in = _matmul(
        "mix_dw_in", n2, [dproj], mode="tn", grid=(2, 8, s_len // tk),
        a_spec=pl.BlockSpec((tk, hd), lambda i, j, k: (k, i)),
        b_specs=[pl.BlockSpec((None, tk, tnp), lambda i, j, k: (j // 2, k, j % 2))],
        out_shape=_sds((2, N_CHIPS, hd, npc), F32),
        out_specs=pl.BlockSpec((None, None, hd, tnp), lambda i, j, k: (i, j // 2, 0, j % 2)),
        acc_shape=(hd, tnp), epilogue=_ep_store(F32))
    rs["mix"] = _ReduceScatter("g_mix", ["w_in", "w_a_out", "w_b_group", "w_out"], [gw_in, gw_a, gw_b, gw_out],
                               qc_idx)
    rs["ffn2"].step3()
    dn2 = _matmul(
        "mix_dn", dproj, [w_in_g], mode="nt", grid=(s_len // tm, d // tn, N_CHIPS),
        a_spec=pl.BlockSpec((None, tm, npc), lambda i, j, k: (k, i, 0)),
        b_specs=[pl.BlockSpec((None, tn, npc), lambda i, j, k: (k, j, 0))],
        out_shape=_sds((s_len, d), F32), out_specs=pl.BlockSpec((tm, tn), lambda i, j, k: (i, j)),
        acc_shape=(tm, tn), epilogue=_ep_store(F32))
    dh1, df1, d_sh2, d_sc2, d_gm, d_gt1 = _norm_mod_bwd("mix_norm_bwd", h1, dn2, dh2, gmv, sc2, ts,
                                                        prev=(f1, gt1, 0.5))
    rs["mix"].step2()

    def w1_in_ready(g):
        rs["w1_in"] = _ReduceScatter("g_w1_in", ["w1_in"], [g], qc_idx)
        rs["w1_out"].step2()
        rs["mix"].step3()

    dn1 = _ffn_bwd(
        "ffn1", n1, hu1, act1, df1, w1_in_g, w1_out_2d, dims,
        after_dw_out=lambda g: rs.update(w1_out=_ReduceScatter("g_w1_out", ["w1_out"], [g], qc_idx)),
        after_dw_in=w1_in_ready)
    grad_x, d_sh1, d_sc1, d_g1 = _norm_mod_bwd("ffn1_norm_bwd", x2, dn1, dh1, g1v, sc1, ts)

    d_ada = jnp.concatenate([d_sh1, d_sc1, d_gt1, d_sh2, d_sc2, d_gt2, d_sh3, d_sc3, d_gt3], axis=1)
    small = [d_ada, d_g1, d_gm, d_cw[:CONV_K].reshape(1, -1), d_cb, d_lg, d_lb, d_ba, d_bb, d_ls, d_g2, d_gf,
             loss_cols]
    sizes = [a.shape[1] for a in small]
    pack1 = jnp.concatenate(small, axis=1).reshape(-1)
    n1p = -(-pack1.shape[0] // (8 * LANES)) * LANES
    pack1 = jnp.pad(pack1, (0, 8 * n1p - pack1.shape[0])).reshape(8, n1p)
    g2 = _allgather_small("gather_small_grads", pack1)
    rs["w1_in"].step2()
    total = _sum_devices("sum_small_grads", g2, 8).reshape(-1)
    offs = [0]
    for sz in sizes:
        offs.append(offs[-1] + sz)
    tot = [total[offs[k]:offs[k + 1]] for k in range(len(sizes))]
    d_ada_all = g2.reshape(N_DEV, 8 * n1p)[:, :sizes[0]]
    loss = jnp.sum(tot[12])

    grads = {}
    grads["b_ada"] = tot[0]
    grads["g_ffn1"], grads["g_mix"] = tot[1], tot[2]
    grads["conv_w"] = lax.dynamic_slice(tot[3].reshape(CONV_K, wc), (0, q * cwq), (CONV_K, cwq))
    grads["conv_b"], grads["ln_a_g"], grads["ln_a_b"], grads["b_a_out"] = tot[4], tot[5], tot[6], tot[7]
    grads["b_b_group"] = lax.dynamic_slice(tot[8].reshape(n_groups, N_CHIPS * goq), (0, q * goq), (n_groups, goq))
    grads["ls_b"], grads["g_ffn2"], grads["g_final"] = tot[9], tot[10], tot[11]

    delta, new_m, new_v = {}, {}, {}

    def adamw_group(reduced):
        for nm, g in reduced.items():
            shp = weights[nm].shape
            go, dl, mo, vo = _adamw(f"adamw_{nm}", as2d(weights[nm]), g, as2d(mom1[nm]), as2d(mom2[nm]))
            grads[nm], delta[nm], new_m[nm], new_v[nm] = go.reshape(shp), dl.reshape(shp), mo.reshape(shp), vo.reshape(shp)

    adamw_group(rs["ffn2"].result())
    rs["w1_out"].step3()
    adamw_group(rs["mix"].result())
    d_ada_mine = lax.dynamic_slice(d_ada_all, (0, q * ada_c), (N_DEV, ada_c))
    grads["w_ada"], delta["w_ada"], new_m["w_ada"], new_v["w_ada"] = _ada_grad_adamw(
        "adamw_w_ada", c_all.T, d_ada_mine, w_ada, m_w_ada, v_w_ada)
    rs["w1_in"].step3()
    smalls = [nm for nm in order if nm not in big and nm != "w_ada"]
    flat = lambda src: jnp.concatenate([src[nm].reshape(-1) for nm in smalls])
    n_small = sum(weights[nm].size for nm in smalls)
    rows_s = -(-n_small // (8 * LANES)) * 8
    packed = [jnp.pad(flat(src), (0, rows_s * LANES - n_small)).reshape(rows_s, LANES)
              for src in (weights, grads, mom1, mom2)]
    _, dl_s, mo_s, vo_s = _adamw("adamw_small", *packed)
    off = 0
    for nm in smalls:
        sz, shp = weights[nm].size, weights[nm].shape
        delta[nm] = dl_s.reshape(-1)[off:off + sz].reshape(shp)
        new_m[nm] = mo_s.reshape(-1)[off:off + sz].reshape(shp)
        new_v[nm] = vo_s.reshape(-1)[off:off + sz].reshape(shp)
        grads[nm] = grads[nm].reshape(shp)
        off += sz
    adamw_group(rs["w1_out"].result())
    adamw_group(rs["w1_in"].result())

    return (loss, grad_x[None], *[grads[nm] for nm in order], *[delta[nm] for nm in order],
            *[new_m[nm] for nm in order], *[new_v[nm] for nm in order])
```

```python
import jax
import jax.numpy as jnp
from jax import lax
from jax.experimental import pallas as pl
from jax.experimental.pallas import tpu as pltpu

F32 = jnp.float32
BF16 = jnp.bfloat16
MESH = pl.DeviceIdType.MESH
ANY = pl.BlockSpec(memory_space=pl.ANY)
HBM = pl.BlockSpec(memory_space=pltpu.HBM)
SEM = pl.BlockSpec(memory_space=pltpu.SEMAPHORE)
EFFECT = pltpu.SideEffectType.DATAFLOW_SIDE_EFFECTING

EPS = 1e-6
CONV_K = 31
HALO = 32
POOL_WINDOWS = (2, 4, 8, 16)
N_CHIPS = 4
N_DEV = 8
LANES = 128

ADAM_LR = 0.001
ADAM_B1 = 0.9
ADAM_B2 = 0.999
ADAM_EPS = 1e-08
ADAM_WD = 0.01
ADAM_STEP = 10

DN = {
    "nn": (((1,), (0,)), ((), ())),
    "nt": (((1,), (1,)), ((), ())),
    "tn": (((0,), (0,)), ((), ())),
}


_PREVIOUS = []


def _ordered(call, args, n_lead, body, token=None, sources=()):
    dep = [pltpu.with_memory_space_constraint(p, pltpu.HBM) if p.size * p.dtype.itemsize >= (1 << 20) else p
           for p in _PREVIOUS if all(p is not a for a in (*args, *sources))]

    def wrapped(*refs):
        return body(*refs[:n_lead], *refs[n_lead + len(dep):])

    outs = call(wrapped, [ANY] * len(dep))(*args, *dep)
    seq = outs if isinstance(outs, (list, tuple)) else [outs]
    _PREVIOUS[:] = [seq[token] if token is not None else
                    next(o for o in seq if jnp.issubdtype(o.dtype, jnp.floating))]
    return outs


def _pcall(body, *, name, out_shape, grid=None, in_specs=None, out_specs=None, scratch=(), aliases=None,
           prefetch=0, vmem_mb=None):
    params = {}
    if grid is not None:
        params["dimension_semantics"] = ("arbitrary",) * len(grid)
    if vmem_mb is not None:
        params["vmem_limit_bytes"] = vmem_mb << 20
    def in_hbm(shape, spec):
        big = shape.size * jnp.dtype(shape.dtype).itemsize >= (1 << 20)
        return pltpu.HBM(shape.shape, shape.dtype) if big and getattr(spec, "memory_space", None) != pltpu.VMEM else shape

    if isinstance(out_shape, (list, tuple)):
        out_shape = [in_hbm(s, sp) for s, sp in zip(out_shape, out_specs)]
    else:
        out_shape = in_hbm(out_shape, out_specs)
    kw = dict(name=name, out_shape=out_shape, compiler_params=pltpu.CompilerParams(**params))
    if aliases:
        kw["input_output_aliases"] = aliases

    def call(wrapped, dep_specs):
        specs = list(in_specs) + dep_specs
        if prefetch:
            return pl.pallas_call(wrapped, grid_spec=pltpu.PrefetchScalarGridSpec(
                num_scalar_prefetch=prefetch, grid=grid, in_specs=specs, out_specs=out_specs,
                scratch_shapes=list(scratch)), **kw)
        if grid is not None:
            return pl.pallas_call(wrapped, grid=grid, in_specs=specs, out_specs=out_specs,
                                  scratch_shapes=list(scratch), **kw)
        return pl.pallas_call(wrapped, in_specs=specs, out_specs=out_specs, scratch_shapes=list(scratch), **kw)

    def run(*args):
        specs = [None] * prefetch + list(in_specs)
        placed = [pltpu.with_memory_space_constraint(a, pltpu.HBM)
                  if a.size * a.dtype.itemsize >= (1 << 20) and getattr(s, "memory_space", None) != pltpu.VMEM else a
                  for a, s in zip(args, specs)]
        return _ordered(call, placed, prefetch + len(in_specs), body, sources=args)

    return run


def _tile(dim, pref):
    t = min(dim, pref)
    assert dim % t == 0, (dim, pref)
    return t


def _sds(shape, dtype):
    return jax.ShapeDtypeStruct(tuple(shape), dtype)


def _sigmoid(v):
    return 0.5 * jnp.tanh(0.5 * v) + 0.5


def _vec(w):
    return pl.BlockSpec((1, w), lambda *_: (0, 0))


def _acc_rows(ref, val, i):
    @pl.when(i == 0)
    def _():
        ref[...] = jnp.zeros_like(ref)

    ref[...] += jnp.sum(val, axis=0, keepdims=True)


def _matmul(name, a, bs, *, mode, grid, a_spec, b_specs, out_shape, out_specs, acc_shape, epilogue,
            extras=(), extra_specs=(), vmem_mb=56, carry=()):
    nb, ne, nk, nc = len(bs), len(extras), grid[2], len(carry)
    dn = DN[mode]

    def body(*all_refs):
        refs = all_refs[:1 + nb + ne] + all_refs[1 + nb + ne + nc:]
        a_ref, b_refs, ex = refs[0], refs[1:1 + nb], refs[1 + nb:1 + nb + ne]
        if nk == 1:
            outs = refs[1 + nb + ne:]
            accs = [lax.dot_general(a_ref[...], b[...], dn, preferred_element_type=F32) for b in b_refs]
            epilogue(accs, ex, outs)
            return
        outs, acc_refs = refs[1 + nb + ne:-nb], refs[-nb:]
        k = pl.program_id(2)

        @pl.when(k == 0)
        def _():
            for acc in acc_refs:
                acc[...] = jnp.zeros_like(acc)

        for acc, b in zip(acc_refs, b_refs):
            acc[...] += lax.dot_general(a_ref[...], b[...], dn, preferred_element_type=F32)

        @pl.when(k == nk - 1)
        def _():
            epilogue([acc[...] for acc in acc_refs], ex, outs)

    scratch = [pltpu.VMEM(acc_shape, F32) for _ in range(nb)] if nk > 1 else []
    return _pcall(body, name=name, out_shape=out_shape, grid=grid,
                  in_specs=[a_spec, *b_specs, *extra_specs, *[ANY] * nc], out_specs=out_specs, scratch=scratch,
                  aliases={1 + nb + ne + i: i for i in range(nc)}, vmem_mb=vmem_mb)(a, *bs, *extras, *carry)


def _ep_store(dtype):
    def ep(accs, ex, outs):
        outs[0][...] = accs[0].astype(dtype)
    return ep


def _ep_halves(h):
    def ep(accs, ex, outs):
        outs[0][0] = accs[0][:h]
        outs[0][1] = accs[0][h:]
    return ep


def _place():
    x, y, c = lax.axis_index("x"), lax.axis_index("y"), lax.axis_index("c")
    chips = [(1 - x, y), (x, 1 - y), (1 - x, 1 - y)]
    return x, y, c, chips


def _allgather_small(name, block):
    m_per, n = block.shape

    def body(x_ref, out_ref, send_sems, recv_sems, local_sem):
        x, y, c, chips = _place()
        me, sibling = (x, y, c), (x, y, 1 - c)

        def rows(px, py, pc):
            return out_ref.at[pl.ds((4 * px + 2 * py + pc) * m_per, m_per), :]

        def copy(k, blk, to, src=None):
            return pltpu.make_async_remote_copy(
                src_ref=rows(*blk) if src is None else src, dst_ref=rows(*blk),
                send_sem=send_sems.at[k], recv_sem=recv_sems.at[k], device_id=to, device_id_type=MESH)

        mine = pltpu.make_async_copy(x_ref, rows(*me), local_sem)
        mine.start()
        first = [copy(0, me, sibling, src=x_ref)]
        first += [copy(1 + j, me, (*chip, c), src=x_ref) for j, chip in enumerate(chips)]
        for cp in first:
            cp.start()
        passed = [copy(4 + j, (*chip, c), sibling) for j, chip in enumerate(chips)]
        for j, chip in enumerate(chips):
            copy(1 + j, (*chip, c), me).wait_recv()
            passed[j].start()
        copy(0, sibling, me).wait_recv()
        for j, chip in enumerate(chips):
            copy(4 + j, (*chip, 1 - c), me).wait_recv()
        for cp in first + passed:
            cp.wait_send()
        mine.wait()

    return _pcall(
        body, name=name, out_shape=_sds((N_DEV * m_per, n), block.dtype),
        in_specs=[pl.BlockSpec(memory_space=pltpu.VMEM)], out_specs=pl.BlockSpec(memory_space=pltpu.VMEM),
        scratch=[pltpu.SemaphoreType.DMA((7,)), pltpu.SemaphoreType.DMA((7,)), pltpu.SemaphoreType.DMA],
    )(block)


class _SplitCopies:
    def __init__(self, name, arrays, plan, n_copies):
        self.name, self.plan, self.n = name, plan, len(arrays)
        n = self.n

        def body(*refs):
            send, recv, token = refs[n], refs[n + 1], refs[-1]
            for k, (src, dst, _, peer) in enumerate(plan(refs[:n])):
                pltpu.make_async_remote_copy(src_ref=src, dst_ref=dst, send_sem=send.at[k], recv_sem=recv.at[k],
                                             device_id=peer, device_id_type=MESH).start()
            token[...] = jnp.zeros_like(token)

        def call(wrapped, dep_specs):
            return pl.pallas_call(
                wrapped, name=f"{name}_start",
                out_shape=(pltpu.SemaphoreType.DMA((n_copies,)), pltpu.SemaphoreType.DMA((n_copies,)),
                           *[pltpu.HBM(a.shape, a.dtype) for a in arrays], _sds((8, LANES), F32)),
                in_specs=[HBM] * n + dep_specs,
                out_specs=(SEM, SEM, *[HBM] * n, pl.BlockSpec(memory_space=pltpu.VMEM)),
                input_output_aliases={i: 2 + i for i in range(n)},
                compiler_params=pltpu.CompilerParams(has_side_effects=EFFECT))

        outs = _ordered(call, [pltpu.with_memory_space_constraint(a, pltpu.HBM) for a in arrays], n, body, token=-1,
                        sources=arrays)
        self.send, self.recv, self.arrays = outs[0], outs[1], list(outs[2:2 + n])

    def wait(self, arrays=None):
        n, plan = self.n, self.plan
        if arrays is not None:
            self.arrays = list(arrays)

        def body(*refs):
            send, recv, token = refs[n], refs[n + 1], refs[-1]
            for k, (src, _, landing, peer) in enumerate(plan(refs[:n])):
                cp = pltpu.make_async_remote_copy(src_ref=src, dst_ref=landing, send_sem=send.at[k],
                                                  recv_sem=recv.at[k], device_id=peer, device_id_type=MESH)
                cp.wait_send()
                cp.wait_recv()
            token[...] = jnp.zeros_like(token)

        def call(wrapped, dep_specs):
            return pl.pallas_call(
                wrapped, name=f"{self.name}_wait",
                out_shape=(*[pltpu.HBM(a.shape, a.dtype) for a in self.arrays], _sds((8, LANES), F32)),
                in_specs=[HBM] * n + [SEM, SEM] + dep_specs,
                out_specs=(*[HBM] * n, pl.BlockSpec(memory_space=pltpu.VMEM)),
                input_output_aliases={i: i for i in range(n)},
                compiler_params=pltpu.CompilerParams(has_side_effects=EFFECT))

        return list(_ordered(call, [*self.arrays, self.send, self.recv], n + 2, body, token=-1))[:n]


def _col_range(g, part, n_parts):
    width = g.shape[-1] // n_parts
    return (slice(None), pl.ds(part * width, width))


def _gather_ici(name, gathered, part=0, n_parts=1):
    def plan(refs):
        x, y, c, chips = _place()
        q = 2 * x + y
        return [(g.at[(q, c, *_col_range(g, part, n_parts))], g.at[(q, c, *_col_range(g, part, n_parts))],
                 g.at[(2 * px + py, c, *_col_range(g, part, n_parts))], (px, py, c))
                for g in refs for px, py in chips]

    return _SplitCopies(name, gathered, plan, 3 * len(gathered))


def _gather_d2d(name, gathered, part=0, n_parts=1):
    def plan(refs):
        x, y, c, chips = _place()
        return [(g.at[(2 * px + py, c, *_col_range(g, part, n_parts))],
                 g.at[(2 * px + py, c, *_col_range(g, part, n_parts))],
                 g.at[(2 * px + py, 1 - c, *_col_range(g, part, n_parts))], (x, y, 1 - c))
                for g in refs for px, py in chips]

    return _SplitCopies(name, gathered, plan, 3 * len(gathered))


class _TwoPartGather:
    def __init__(self, name, gathered):
        self.name, self.d2d = name, {}
        self.ici = [_gather_ici(f"gather_{name}_a_ici", [gathered], 0, 2)]
        self.buf = self.ici[0].arrays

    def start_second(self):
        self.ici.append(_gather_ici(f"gather_{self.name}_b_ici", self.buf, 1, 2))
        self.buf = self.ici[1].arrays

    def arrive(self, part):
        here = self.ici[part].wait(self.buf)
        self.d2d[part] = _gather_d2d(f"gather_{self.name}_{'ab'[part]}_d2d", here, part, 2)
        self.buf = self.d2d[part].arrays

    def ready(self, part):
        self.buf = self.d2d[part].wait(self.buf)
        g = self.buf[0]
        return g.reshape(N_CHIPS, 2 * g.shape[2], g.shape[3])


def _scatter_sibling(name, grads):
    n = len(grads)

    def plan(refs):
        x, y, c, _ = _place()
        return [(refs[w].at[1 - c], refs[n + w], refs[n + w], (x, y, 1 - c)) for w in range(n)]

    landing = [lax.empty(g.shape[1:], g.dtype) for g in grads]
    return _SplitCopies(name, [*grads, *landing], plan, n)


def _scatter_chips(name, sums):
    n = len(sums)

    def plan(refs):
        x, y, c, chips = _place()
        return [(refs[w].at[2 * px + py], refs[n + w].at[j], refs[n + w].at[j], (px, py, c))
                for w in range(n) for j, (px, py) in enumerate(chips)]

    landing = [lax.empty((3, *s.shape[1:]), s.dtype) for s in sums]
    return _SplitCopies(name, [*sums, *landing], plan, 3 * n)


def _share_final(name, finals):
    def plan(refs):
        x, y, c, _ = _place()
        return [(f.at[c], f.at[c], f.at[1 - c], (x, y, 1 - c)) for f in refs]

    return _SplitCopies(name, finals, plan, len(finals))


def _row_tile(rows, cols, budget_elems=786432):
    best = 8
    for t in range(8, rows + 1, 8):
        if rows % t == 0 and t * cols <= budget_elems:
            best = t
    return best if rows % best == 0 else rows


def _sum_with_sibling(name, grad, recv, qc_idx):
    _, _, h, cols = grad.shape
    tr = _row_tile(h, cols)

    def body(s_ref, g_ref, r_ref, own_ref, pb_ref):
        p = g_ref[...] + r_ref[...]
        pb_ref[...] = p.astype(BF16)

        @pl.when(pl.program_id(1) == s_ref[0])
        def _():
            own_ref[...] = p

    blk = pl.BlockSpec((None, tr, cols), lambda r, k, s: (k, r, 0))
    return _pcall(
        body, name=name, out_shape=[_sds((h, cols), F32), _sds((N_CHIPS, h, cols), BF16)],
        grid=(h // tr, N_CHIPS), prefetch=1,
        in_specs=[pl.BlockSpec((None, None, tr, cols), lambda r, k, s: (s[1], k, r, 0)), blk],
        out_specs=[pl.BlockSpec((tr, cols), lambda r, k, s: (r, 0)), blk], vmem_mb=32,
    )(qc_idx, grad, recv)


def _sum_chips(name, own, recv, qc_idx):
    h, cols = own.shape
    tr = _row_tile(h, cols)

    def body(s_ref, p_ref, t_ref, o_ref):
        o_ref[...] = ((p_ref[...] + t_ref[0].astype(F32)) + t_ref[1].astype(F32)) + t_ref[2].astype(F32)

    return _pcall(
        body, name=name, out_shape=_sds((2, h, cols), F32), grid=(h // tr,), prefetch=1,
        in_specs=[pl.BlockSpec((tr, cols), lambda r, s: (r, 0)),
                  pl.BlockSpec((3, tr, cols), lambda r, s: (0, r, 0))],
        out_specs=pl.BlockSpec((None, tr, cols), lambda r, s: (s[1], r, 0)), vmem_mb=32,
    )(qc_idx, own, recv)


class _ReduceScatter:
    def __init__(self, tag, names, grads, qc_idx):
        self.tag, self.names, self.n, self.qc_idx = tag, names, len(grads), qc_idx
        self.copies = _scatter_sibling(f"{tag}_rs_sibling", grads)

    def step2(self):
        n = self.n
        arrs = self.copies.wait()
        sums = [_sum_with_sibling(f"{nm}_sum_sibling", arrs[w], arrs[n + w], self.qc_idx)
                for w, nm in enumerate(self.names)]
        self.own = [s[0] for s in sums]
        self.copies = _scatter_chips(f"{self.tag}_rs_chips", [s[1] for s in sums])

    def step3(self):
        n = self.n
        arrs = self.copies.wait()
        finals = [_sum_chips(f"{nm}_sum_chips", self.own[w], arrs[n + w], self.qc_idx)
                  for w, nm in enumerate(self.names)]
        self.copies = _share_final(f"{self.tag}_rs_final", finals)

    def result(self):
        return {nm: f.reshape(2 * f.shape[1], f.shape[2]) for nm, f in zip(self.names, self.copies.wait())}


def _cast_into_gathered(name, w, q_idx):
    rows, cols = w.shape
    h = rows // 2
    tr = _row_tile(h, cols, 1 << 20)
    nr = h // tr

    def body(s_ref, w_ref, o_ref):
        o_ref[...] = w_ref[...].astype(BF16)

    return _pcall(body, name=name, out_shape=_sds((N_CHIPS, 2, h, cols), BF16), grid=(2, nr), prefetch=1,
                  in_specs=[pl.BlockSpec((tr, cols), lambda hf, r, s: (hf * nr + r, 0))],
                  out_specs=pl.BlockSpec((None, None, tr, cols), lambda hf, r, s: (s[0], hf, r, 0)),
                  vmem_mb=32)(q_idx, w)


def _regroup(name, w, n_groups):
    n_chips, rows, goq = w.shape
    gi = rows // n_groups

    def body(w_ref, o_ref):
        o_ref[...] = w_ref[...]

    return _pcall(body, name=name, out_shape=_sds((n_groups, gi, n_chips * goq), w.dtype), grid=(n_groups, n_chips),
                  in_specs=[pl.BlockSpec((None, gi, goq), lambda g, k: (k, g, 0))],
                  out_specs=pl.BlockSpec((None, gi, goq), lambda g, k: (g, 0, k)), vmem_mb=32)(w)


def _rms(h):
    r = lax.rsqrt(jnp.mean(h * h, axis=-1, keepdims=True) + EPS)
    return r, h * r


def _norm_mod(name, h, g, sc, sh, ts):
    s_len, d = h.shape

    def body(h_ref, g_ref, sc_ref, sh_ref, n_ref):
        _, xhat = _rms(h_ref[...])
        n_ref[...] = ((xhat * g_ref[...]) * (1.0 + sc_ref[...]) + sh_ref[...]).astype(BF16)

    row = pl.BlockSpec((ts, d), lambda i: (i, 0))
    return _pcall(body, name=name, out_shape=_sds((s_len, d), BF16), grid=(s_len // ts,),
                  in_specs=[row, _vec(d), _vec(d), _vec(d)], out_specs=row, vmem_mb=32)(h, g, sc, sh)


def _residual_norm_mod(name, h, f, gate, cmul, g, sc, sh, ts):
    s_len, d = h.shape

    def body(h_ref, f_ref, gt_ref, g_ref, sc_ref, sh_ref, ho_ref, n_ref):
        hn = h_ref[...] + (cmul * gt_ref[...]) * f_ref[...]
        ho_ref[...] = hn
        _, xhat = _rms(hn)
        n_ref[...] = ((xhat * g_ref[...]) * (1.0 + sc_ref[...]) + sh_ref[...]).astype(BF16)

    row = pl.BlockSpec((ts, d), lambda i: (i, 0))
    return _pcall(body, name=name, out_shape=[_sds((s_len, d), F32), _sds((s_len, d), BF16)],
                  grid=(s_len // ts,), in_specs=[row, row, _vec(d), _vec(d), _vec(d), _vec(d)],
                  out_specs=[row, row], vmem_mb=32)(h, f, gate, g, sc, sh)


def _final_loss(name, h, f, tgt, gate, cmul, g, ts):
    s_len, d = h.shape

    def body(h_ref, f_ref, t_ref, gt_ref, g_ref, dh_ref, df_ref, dg_ref, dgt_ref, loss_ref):
        i = pl.program_id(0)
        fv = f_ref[...]
        coef = cmul * gt_ref[...]
        hn = h_ref[...] + coef * fv
        r, xhat = _rms(hn)
        err = xhat * g_ref[...] - t_ref[...]
        _acc_rows(loss_ref, (0.5 / d) * (err * err), i)
        dy = err * (1.0 / d)
        _acc_rows(dg_ref, dy * xhat, i)
        dxhat = dy * g_ref[...]
        dh = r * (dxhat - xhat * jnp.mean(dxhat * xhat, axis=-1, keepdims=True))
        dh_ref[...] = dh
        _acc_rows(dgt_ref, cmul * (dh * fv), i)
        df_ref[...] = (coef * dh).astype(BF16)

    row = pl.BlockSpec((ts, d), lambda i: (i, 0))
    return _pcall(body, name=name,
                  out_shape=[_sds((s_len, d), F32), _sds((s_len, d), BF16)] + [_sds((1, d), F32)] * 3,
                  grid=(s_len // ts,), in_specs=[row, row, row, _vec(d), _vec(d)],
                  out_specs=[row, row, _vec(d), _vec(d), _vec(d)], vmem_mb=40)(h, f, tgt, gate, g)


def _norm_mod_bwd(name, h, dn, dh_next, g, sc, ts, prev=None):
    s_len, d = h.shape
    has_prev = prev is not None
    cmul = prev[2] if has_prev else None

    def body(*refs):
        if has_prev:
            h_ref, dn_ref, dhn_ref, f_ref, g_ref, sc_ref, gt_ref, dh_ref, df_ref, dsh_ref, dsc_ref, dg_ref, dgt_ref = refs
        else:
            h_ref, dn_ref, dhn_ref, g_ref, sc_ref, dh_ref, dsh_ref, dsc_ref, dg_ref = refs
        i = pl.program_id(0)
        r, xhat = _rms(h_ref[...])
        dn_v = dn_ref[...]
        gv = g_ref[...]
        _acc_rows(dsh_ref, dn_v, i)
        _acc_rows(dsc_ref, dn_v * (xhat * gv), i)
        dnrm = dn_v * (1.0 + sc_ref[...])
        _acc_rows(dg_ref, dnrm * xhat, i)
        dxhat = dnrm * gv
        dh = dhn_ref[...] + r * (dxhat - xhat * jnp.mean(dxhat * xhat, axis=-1, keepdims=True))
        dh_ref[...] = dh
        if has_prev:
            _acc_rows(dgt_ref, cmul * (dh * f_ref[...]), i)
            df_ref[...] = ((cmul * gt_ref[...]) * dh).astype(BF16)

    row = pl.BlockSpec((ts, d), lambda i: (i, 0))
    if has_prev:
        ins, in_specs = [h, dn, dh_next, prev[0], g, sc, prev[1]], [row, row, row, row, _vec(d), _vec(d), _vec(d)]
        out_shape = [_sds((s_len, d), F32), _sds((s_len, d), BF16)] + [_sds((1, d), F32)] * 4
        out_specs = [row, row] + [_vec(d)] * 4
    else:
        ins, in_specs = [h, dn, dh_next, g, sc], [row, row, row, _vec(d), _vec(d)]
        out_shape = [_sds((s_len, d), F32)] + [_sds((1, d), F32)] * 3
        out_specs = [row] + [_vec(d)] * 3
    return _pcall(body, name=name, out_shape=out_shape, grid=(s_len // ts,), in_specs=in_specs,
                  out_specs=out_specs, vmem_mb=40)(*ins)


def _cols(ref, lo, hi, npc, rows=slice(None)):
    parts = []
    while lo < hi:
        q, o = divmod(lo, npc)
        n = min(hi - lo, npc - o)
        parts.append(ref[q, rows, o:o + n].astype(F32))
        lo += n
    return parts[0] if len(parts) == 1 else jnp.concatenate(parts, axis=-1)


def _store_cols(ref, lo, val, npc, rows=slice(None)):
    off, width = 0, val.shape[-1]
    while off < width:
        q, o = divmod(lo + off, npc)
        n = min(width - off, npc - o)
        ref[q, rows, o:o + n] = val[:, off:off + n]
        off += n


def _chips_covering(cols, npc):
    return -(-cols // npc)


SUBLANES = 8
ROW_CHUNK = 32


def _make_phases(src_ref, ph_ref):
    rows = src_ref.shape[0] - SUBLANES
    for b in range(1, SUBLANES):
        ph_ref[b - 1] = src_ref[pl.ds(b, rows), :]


def _window(src_ref, ph_ref, off, r0, cols=slice(None)):
    a, b = divmod(off, SUBLANES)
    start = pl.multiple_of(r0 + SUBLANES * a, SUBLANES)
    if b == 0:
        return src_ref[pl.ds(start, ROW_CHUNK), cols]
    return ph_ref[b - 1, pl.ds(start, ROW_CHUNK), cols]


def _phase_scratch(rows, width):
    return pltpu.VMEM((SUBLANES - 1, rows - SUBLANES, width), F32)


def _conv(a0s_ref, a0p_ref, cw_ref, cb_ref, r0):
    a1 = cb_ref[...] + cw_ref[0:1, :] * _window(a0s_ref, a0p_ref, HALO - CONV_K + 1, r0)
    for k in range(1, CONV_K):
        a1 = a1 + cw_ref[k:k + 1, :] * _window(a0s_ref, a0p_ref, HALO - CONV_K + 1 + k, r0)
    return a1


def _layer_norm(a1, lg_ref, lb_ref):
    mu = jnp.mean(a1, axis=-1, keepdims=True)
    ctr = a1 - mu
    rstd = lax.rsqrt(jnp.mean(ctr * ctr, axis=-1, keepdims=True) + EPS)
    xh = ctr * rstd
    return xh, rstd, xh * lg_ref[...] + lb_ref[...]


def _for_chunks(ts, fn):
    def step(ci, carry):
        fn(pl.multiple_of(ci * ROW_CHUNK, ROW_CHUNK))
        return carry

    lax.fori_loop(0, ts // ROW_CHUNK, step, 0)


def _stage_glu(p_ref, ph_ref, a0s_ref, i, wc, npc, ts):
    a0 = _cols(p_ref, 0, wc, npc) * _sigmoid(_cols(p_ref, wc, 2 * wc, npc))
    a0h = _cols(ph_ref, 0, wc, npc) * _sigmoid(_cols(ph_ref, wc, 2 * wc, npc))
    a0s_ref[0:HALO, :] = jnp.where(i > 0, a0h, 0.0)
    a0s_ref[HALO:HALO + ts, :] = a0


def _mixer_mid(name, proj, cw, cb, lg, lb, wc, wp, ts):
    _, s_len, npc = proj.shape
    nq = _chips_covering(2 * wc + wp, npc)
    gi = wp // len(POOL_WINDOWS)
    hb = ts // HALO

    def body(p_ref, ph_ref, cw_ref, cb_ref, lg_ref, lb_ref, a3_ref, mx_ref, a1_ref, a0s_ref, vs_ref, a0p_ref,
             vp_ref):
        i = pl.program_id(0)
        _stage_glu(p_ref, ph_ref, a0s_ref, i, wc, npc, ts)
        vs_ref[0:HALO, :] = jnp.where(i > 0, _cols(ph_ref, 2 * wc, 2 * wc + wp, npc), 0.0)
        vs_ref[HALO:HALO + ts, :] = _cols(p_ref, 2 * wc, 2 * wc + wp, npc)
        _make_phases(a0s_ref, a0p_ref)
        _make_phases(vs_ref, vp_ref)

        def chunk(r0):
            rows = pl.ds(r0, ROW_CHUNK)
            a1 = _conv(a0s_ref, a0p_ref, cw_ref, cb_ref, r0)
            a1_ref[rows, :] = a1
            _, _, a2 = _layer_norm(a1, lg_ref, lb_ref)
            a3_ref[rows, :] = (a2 * _sigmoid(a2)).astype(BF16)
            t_abs = i * ts + r0 + lax.broadcasted_iota(jnp.int32, (ROW_CHUNK, 1), 0)
            for g, win in enumerate(POOL_WINDOWS):
                cs = slice(g * gi, (g + 1) * gi)
                v_now = _window(vs_ref, vp_ref, HALO, r0, cs)
                acc = v_now
                for dlt in range(1, win):
                    acc = acc + _window(vs_ref, vp_ref, HALO - dlt, r0, cs)
                cnt = jnp.minimum(t_abs + 1, win).astype(F32)
                mx_ref[rows, cs] = (acc / cnt - v_now).astype(BF16)

        _for_chunks(ts, chunk)

    return _pcall(
        body, name=name, out_shape=[_sds((s_len, wc), BF16), _sds((s_len, wp), BF16), _sds((s_len, wc), F32)],
        grid=(s_len // ts,),
        in_specs=[pl.BlockSpec((nq, ts, npc), lambda i: (0, i, 0)),
                  pl.BlockSpec((nq, HALO, npc), lambda i: (0, jnp.maximum(i * hb - 1, 0), 0)),
                  pl.BlockSpec((HALO, wc), lambda i: (0, 0)), _vec(wc), _vec(wc), _vec(wc)],
        out_specs=[pl.BlockSpec((ts, wc), lambda i: (i, 0)), pl.BlockSpec((ts, wp), lambda i: (i, 0)),
                   pl.BlockSpec((ts, wc), lambda i: (i, 0))],
        scratch=[pltpu.VMEM((HALO + ts, wc), F32), pltpu.VMEM((HALO + ts, wp), F32),
                 _phase_scratch(HALO + ts, wc), _phase_scratch(HALO + ts, wp)], vmem_mb=56,
    )(proj, proj, cw, cb, lg, lb)


def _gates_fwd(name, proj, ya, yb, b_a, b_b, ls, wc, wp, ts):
    _, s_len, npc = proj.shape
    d = ya.shape[1]
    g0 = 2 * wc + wp

    def body(p_ref, ya_ref, yb_ref, ba_ref, bb_ref, ls_ref, z_ref):
        ga = _sigmoid(_cols(p_ref, g0, g0 + d, npc))
        gb = _sigmoid(_cols(p_ref, g0 + d, g0 + 2 * d, npc))
        z = ga * (ya_ref[...] + ba_ref[...]) + gb * ((yb_ref[...] + bb_ref[...]) * ls_ref[...])
        z_ref[...] = z.astype(BF16)

    row = pl.BlockSpec((ts, d), lambda i: (i, 0))
    return _pcall(body, name=name, out_shape=_sds((s_len, d), BF16), grid=(s_len // ts,),
                  in_specs=[pl.BlockSpec((N_CHIPS, ts, npc), lambda i: (0, i, 0)), row, row, _vec(d), _vec(d), _vec(d)],
                  out_specs=row, vmem_mb=48)(proj, ya, yb, b_a, b_b, ls)


def _gates_bwd(name, proj, dz, ya, yb, b_a, b_b, ls, wc, wp, ts):
    _, s_len, npc = proj.shape
    d = ya.shape[1]
    g0 = 2 * wc + wp

    def body(p_ref, dz_ref, ya_ref, yb_ref, ba_ref, bb_ref, ls_ref, dya_ref, dyb_ref, dgt_ref, dba_ref, dls_ref,
             dbb_ref):
        i = pl.program_id(0)
        ga = _sigmoid(_cols(p_ref, g0, g0 + d, npc))
        gb = _sigmoid(_cols(p_ref, g0 + d, g0 + 2 * d, npc))
        dz_v = dz_ref[...]
        y_a = ya_ref[...] + ba_ref[...]
        y_b0 = yb_ref[...] + bb_ref[...]
        ls_v = ls_ref[...]
        dya = dz_v * ga
        dya_ref[...] = dya.astype(BF16)
        _acc_rows(dba_ref, dya, i)
        t = dz_v * gb
        _acc_rows(dls_ref, t * y_b0, i)
        dyb = t * ls_v
        dyb_ref[...] = dyb.astype(BF16)
        _acc_rows(dbb_ref, dyb, i)
        dgt_ref[:, 0:d] = (dz_v * y_a * ga * (1.0 - ga)).astype(BF16)
        dgt_ref[:, d:2 * d] = (dz_v * (y_b0 * ls_v) * gb * (1.0 - gb)).astype(BF16)

    row = pl.BlockSpec((ts, d), lambda i: (i, 0))
    return _pcall(
        body, name=name,
        out_shape=[_sds((s_len, d), BF16), _sds((s_len, d), BF16), _sds((s_len, 2 * d), BF16)] + [_sds((1, d), F32)] * 3,
        grid=(s_len // ts,),
        in_specs=[pl.BlockSpec((N_CHIPS, ts, npc), lambda i: (0, i, 0)), row, row, row, _vec(d), _vec(d), _vec(d)],
        out_specs=[row, row, pl.BlockSpec((ts, 2 * d), lambda i: (i, 0))] + [_vec(d)] * 3, vmem_mb=48,
    )(proj, dz, ya, yb, b_a, b_b, ls)


def _conv_branch_bwd(name, proj, a1, da3, lg, lb, wc, wp, ts):
    _, s_len, npc = proj.shape
    nq = _chips_covering(2 * wc, npc)
    hb = ts // HALO

    n_tiles = s_len // ts

    def fold(v):
        return jnp.sum(v.reshape(ROW_CHUNK // SUBLANES, SUBLANES, v.shape[-1]), axis=0)

    def body(p_ref, ph_ref, a1_ref, da3_ref, lg_ref, lb_ref, da1_ref, dlg_ref, dlb_ref, dcb_ref, dcw_ref,
             a0s_ref, a0p_ref, vec8_ref, dcw8_ref):
        i = pl.program_id(0)
        _stage_glu(p_ref, ph_ref, a0s_ref, i, wc, npc, ts)
        _make_phases(a0s_ref, a0p_ref)

        @pl.when(i == 0)
        def _():
            vec8_ref[...] = jnp.zeros_like(vec8_ref)
            dcw8_ref[...] = jnp.zeros_like(dcw8_ref)

        def chunk(r0):
            rows = pl.ds(r0, ROW_CHUNK)
            xh, rstd, a2 = _layer_norm(a1_ref[rows, :], lg_ref, lb_ref)
            sig = _sigmoid(a2)
            da2 = da3_ref[rows, :] * (sig * (1.0 + a2 * (1.0 - sig)))
            vec8_ref[0] += fold(da2 * xh)
            vec8_ref[1] += fold(da2)
            dxh = da2 * lg_ref[...]
            da1 = rstd * (dxh - jnp.mean(dxh, axis=-1, keepdims=True)
                          - xh * jnp.mean(dxh * xh, axis=-1, keepdims=True))
            da1_ref[rows, :] = da1
            vec8_ref[2] += fold(da1)
            for k in range(CONV_K):
                dcw8_ref[k] += fold(da1 * _window(a0s_ref, a0p_ref, HALO - CONV_K + 1 + k, r0))

        _for_chunks(ts, chunk)

        @pl.when(i == n_tiles - 1)
        def _():
            dlg_ref[...] = jnp.sum(vec8_ref[0], axis=0, keepdims=True)
            dlb_ref[...] = jnp.sum(vec8_ref[1], axis=0, keepdims=True)
            dcb_ref[...] = jnp.sum(vec8_ref[2], axis=0, keepdims=True)
            dcw_ref[...] = jnp.sum(dcw8_ref[...], axis=1)

    return _pcall(
        body, name=name,
        out_shape=[_sds((s_len, wc), F32)] + [_sds((1, wc), F32)] * 3 + [_sds((HALO, wc), F32)],
        grid=(s_len // ts,),
        in_specs=[pl.BlockSpec((nq, ts, npc), lambda i: (0, i, 0)),
                  pl.BlockSpec((nq, HALO, npc), lambda i: (0, jnp.maximum(i * hb - 1, 0), 0)),
                  pl.BlockSpec((ts, wc), lambda i: (i, 0)), pl.BlockSpec((ts, wc), lambda i: (i, 0)),
                  _vec(wc), _vec(wc)],
        out_specs=[pl.BlockSpec((ts, wc), lambda i: (i, 0)), _vec(wc), _vec(wc), _vec(wc),
                   pl.BlockSpec((HALO, wc), lambda i: (0, 0))],
        scratch=[pltpu.VMEM((HALO + ts, wc), F32), _phase_scratch(HALO + ts, wc),
                 pltpu.VMEM((3, SUBLANES, wc), F32), pltpu.VMEM((HALO, SUBLANES, wc), F32)], vmem_mb=56,
    )(proj, proj, a1, da3, lg, lb)


def _mixer_in_bwd(name, proj, da1, dmixed, dgates, cw, wc, wp, ts):
    _, s_len, npc = proj.shape
    nq = _chips_covering(2 * wc, npc)
    gi = wp // len(POOL_WINDOWS)
    hb = ts // HALO
    n_tiles = s_len // ts
    last_hb = s_len // HALO - 1
    d2 = dgates.shape[1]

    def body(p_ref, d1_ref, d1n_ref, dm_ref, dmn_ref, dgt_ref, cw_ref, o_ref, d1s_ref, es_ref, d1p_ref, ep_ref):
        i = pl.program_id(0)
        more = i < n_tiles - 1
        d1s_ref[0:ts, :] = d1_ref[...]
        d1s_ref[ts:ts + HALO, :] = jnp.where(more, d1n_ref[...], 0.0)
        t_abs = i * ts + lax.broadcasted_iota(jnp.int32, (ts + HALO, 1), 0)
        dm_ext = jnp.concatenate([dm_ref[...], jnp.where(more, dmn_ref[...], 0.0)], axis=0)
        for g, win in enumerate(POOL_WINDOWS):
            cs = slice(g * gi, (g + 1) * gi)
            es_ref[:, cs] = dm_ext[:, cs] / jnp.minimum(t_abs + 1, win).astype(F32)
        _make_phases(d1s_ref, d1p_ref)
        _make_phases(es_ref, ep_ref)

        def chunk(r0):
            rows = pl.ds(r0, ROW_CHUNK)
            da0 = cw_ref[0:1, :] * _window(d1s_ref, d1p_ref, CONV_K - 1, r0)
            for k in range(1, CONV_K):
                da0 = da0 + cw_ref[k:k + 1, :] * _window(d1s_ref, d1p_ref, CONV_K - 1 - k, r0)
            glu_a = _cols(p_ref, 0, wc, npc, rows)
            sig = _sigmoid(_cols(p_ref, wc, 2 * wc, npc, rows))
            _store_cols(o_ref, 0, (da0 * sig).astype(BF16), npc, rows)
            _store_cols(o_ref, wc, (da0 * glu_a * sig * (1.0 - sig)).astype(BF16), npc, rows)
            parts = []
            for g, win in enumerate(POOL_WINDOWS):
                cs = slice(g * gi, (g + 1) * gi)
                acc = _window(es_ref, ep_ref, 0, r0, cs)
                for dlt in range(1, win):
                    acc = acc + _window(es_ref, ep_ref, dlt, r0, cs)
                parts.append(acc - dm_ref[rows, cs])
            _store_cols(o_ref, 2 * wc, jnp.concatenate(parts, axis=-1).astype(BF16), npc, rows)

        _for_chunks(ts, chunk)
        _store_cols(o_ref, 2 * wc + wp, dgt_ref[...], npc)

    nxt = lambda i: (jnp.minimum((i + 1) * hb, last_hb), 0)
    return _pcall(
        body, name=name, out_shape=_sds((N_CHIPS, s_len, npc), BF16), grid=(n_tiles,),
        in_specs=[pl.BlockSpec((nq, ts, npc), lambda i: (0, i, 0)),
                  pl.BlockSpec((ts, wc), lambda i: (i, 0)), pl.BlockSpec((HALO, wc), nxt),
                  pl.BlockSpec((ts, wp), lambda i: (i, 0)), pl.BlockSpec((HALO, wp), nxt),
                  pl.BlockSpec((ts, d2), lambda i: (i, 0)),
                  pl.BlockSpec((HALO, wc), lambda i: (0, 0))],
        out_specs=pl.BlockSpec((N_CHIPS, ts, npc), lambda i: (0, i, 0)),
        scratch=[pltpu.VMEM((ts + HALO, wc), F32), pltpu.VMEM((ts + HALO, wp), F32),
                 _phase_scratch(ts + HALO, wc), _phase_scratch(ts + HALO, wp)], vmem_mb=56,
    )(proj, da1, da1, dmixed, dmixed, dgates, cw)


def _ada_fwd(name, c_all, w, b):
    d, cols = w.shape
    tn = 512 if cols % 512 == 0 else cols

    def body(c_ref, w_ref, b_ref, o_ref):
        cv = c_ref[...]
        sc = (cv * _sigmoid(cv)).astype(BF16)
        o_ref[...] = jnp.dot(sc, w_ref[...].astype(BF16), preferred_element_type=F32) + b_ref[...]

    return _pcall(body, name=name, out_shape=_sds((N_DEV, cols), F32), grid=(cols // tn,),
                  in_specs=[pl.BlockSpec((N_DEV, d), lambda j: (0, 0)), pl.BlockSpec((d, tn), lambda j: (0, j)),
                            pl.BlockSpec((1, tn), lambda j: (0, j))],
                  out_specs=pl.BlockSpec((N_DEV, tn), lambda j: (0, j)), vmem_mb=32)(c_all, w, b)


def _adam_math(w, g, m, v):
    m_new = ADAM_B1 * m + (1.0 - ADAM_B1) * g
    v_new = ADAM_B2 * v + (1.0 - ADAM_B2) * (g * g)
    m_hat = m_new / (1.0 - ADAM_B1 ** ADAM_STEP)
    v_hat = v_new / (1.0 - ADAM_B2 ** ADAM_STEP)
    delta = -ADAM_LR * (m_hat / (jnp.sqrt(v_hat) + ADAM_EPS) + ADAM_WD * w)
    return delta, m_new, v_new


def _adamw(name, w, g, m, v):
    rows, cols = w.shape
    tr = _row_tile(rows, cols, 524288)

    def body(w_ref, g_ref, m_ref, v_ref, go_ref, d_ref, mo_ref, vo_ref):
        g = g_ref[...]
        go_ref[...] = g
        d_ref[...], mo_ref[...], vo_ref[...] = _adam_math(w_ref[...], g, m_ref[...], v_ref[...])

    spec = pl.BlockSpec((tr, cols), lambda i: (i, 0))
    return _pcall(body, name=name, out_shape=[_sds(w.shape, F32)] * 4, grid=(rows // tr,), in_specs=[spec] * 4,
                  out_specs=[spec] * 4, vmem_mb=40)(w, g, m, v)


def _ada_grad_adamw(name, c_t, d_ada, w, m, v):
    rows, cols = w.shape
    tr = _tile(rows, 256)
    tc = _tile(cols, 1536) if cols % 1536 == 0 else cols

    def body(c_ref, da_ref, w_ref, m_ref, v_ref, g_ref, d_ref, mo_ref, vo_ref):
        cv = c_ref[...]
        sc = cv * _sigmoid(cv)
        g = sc[:, 0:1] * da_ref[0:1, :]
        for b in range(1, N_DEV):
            g = g + sc[:, b:b + 1] * da_ref[b:b + 1, :]
        g_ref[...] = g
        d_ref[...], mo_ref[...], vo_ref[...] = _adam_math(w_ref[...], g, m_ref[...], v_ref[...])

    spec = pl.BlockSpec((tr, tc), lambda i, j: (i, j))
    return _pcall(body, name=name, out_shape=[_sds(w.shape, F32)] * 4, grid=(rows // tr, cols // tc),
                  in_specs=[pl.BlockSpec((tr, N_DEV), lambda i, j: (i, 0)),
                            pl.BlockSpec((N_DEV, tc), lambda i, j: (0, j)), spec, spec, spec],
                  out_specs=[spec] * 4, vmem_mb=40)(c_t, d_ada, w, m, v)


def _sum_devices(name, gathered, m_per):
    n = gathered.shape[1]

    def body(g_ref, o_ref):
        acc = g_ref[0:m_per, :]
        for dev in range(1, N_DEV):
            acc = acc + g_ref[dev * m_per:(dev + 1) * m_per, :]
        o_ref[...] = acc

    return _pcall(body, name=name, out_shape=_sds((m_per, n), F32),
                  in_specs=[pl.BlockSpec(memory_space=pltpu.VMEM)],
                  out_specs=pl.BlockSpec(memory_space=pltpu.VMEM))(gathered)


def _ffn_fwd(tag, n, w_in_parts, w_out_after_swiglu, dims):
    s_len, d, f_dim = dims["S"], dims["D"], dims["F"]
    tf = f_dim // 4
    tm0, tm = _tile(s_len, 512), _tile(s_len, 1024)
    n_parts = len(w_in_parts)
    nbp = (f_dim // 2) // tf
    nbq = nbp // n_parts

    def ep(accs, ex, outs):
        hh, uu = accs
        sig = _sigmoid(hh)
        silu = hh * sig
        outs[0][0] = (uu * (sig + silu * (1.0 - sig))).astype(BF16)
        outs[0][1] = silu.astype(BF16)
        outs[1][...] = (silu * uu).astype(BF16)

    done = ()
    for part, get_w in enumerate(w_in_parts):
        w_g = get_w()
        col = lambda j, part=part: (j // nbq) * nbp + part * nbq + j % nbq
        done = _matmul(
            f"{tag}_swiglu{part}", n, [w_g, w_g], mode="nn", grid=(2 * nbq, s_len // tm0, 1),
            a_spec=pl.BlockSpec((tm0, d), lambda j, i, k: (i, 0)),
            b_specs=[pl.BlockSpec((None, d, tf), lambda j, i, k, part=part: (j // nbq, 0, part * nbq + j % nbq)),
                     pl.BlockSpec((None, d, tf), lambda j, i, k, part=part: (2 + j // nbq, 0, part * nbq + j % nbq))],
            out_shape=[_sds((2, s_len, f_dim), BF16), _sds((s_len, f_dim), BF16)],
            out_specs=[pl.BlockSpec((2, tm0, tf), lambda j, i, k, col=col: (0, i, col(j))),
                       pl.BlockSpec((tm0, tf), lambda j, i, k, col=col: (i, col(j)))],
            acc_shape=(tm0, tf), epilogue=ep, carry=done)
    hu, act = done
    w_out2d = w_out_after_swiglu()
    tn2 = _tile(d, 1024)
    f = _matmul(
        f"{tag}_down", act, [w_out2d], mode="nn", grid=(s_len // tm, d // tn2, 2),
        a_spec=pl.BlockSpec((tm, 2 * tf), lambda i, j, k: (i, k)),
        b_specs=[pl.BlockSpec((2 * tf, tn2), lambda i, j, k: (k, j))],
        out_shape=_sds((s_len, d), F32), out_specs=pl.BlockSpec((tm, tn2), lambda i, j, k: (i, j)),
        acc_shape=(tm, tn2), epilogue=_ep_store(F32))
    return hu, act, f, w_out2d


def _ffn_bwd(tag, n, hu, act, df, w_in_g, w_out2d, dims, after_dw_out, after_dw_in):
    s_len, d, f_dim = dims["S"], dims["D"], dims["F"]
    tf = f_dim // 4
    tk = _tile(s_len, 2048)
    tn = _tile(d, 1024)
    g_out = _matmul(
        f"{tag}_dw_out", act, [df], mode="tn", grid=(4, d // tn, s_len // tk),
        a_spec=pl.BlockSpec((tk, tf), lambda i, j, k: (k, i)),
        b_specs=[pl.BlockSpec((tk, tn), lambda i, j, k: (k, j))],
        out_shape=_sds((2, 4, tf // 2, d), F32),
        out_specs=pl.BlockSpec((2, None, tf // 2, tn), lambda i, j, k: (0, i, 0, j)),
        acc_shape=(tf, tn), epilogue=_ep_halves(tf // 2))
    after_dw_out(g_out)

    def ep_dhu(accs, ex, outs):
        da = accs[0]
        outs[0][0] = (da * ex[0][0].astype(F32)).astype(BF16)
        outs[0][1] = (da * ex[0][1].astype(F32)).astype(BF16)

    tm = _tile(s_len, 512)
    hu_spec = pl.BlockSpec((2, tm, tf), lambda j, i, k: (0, i, j))
    dhu = _matmul(
        f"{tag}_dhu", df, [w_out2d], mode="nt", grid=(4, s_len // tm, 1),
        a_spec=pl.BlockSpec((tm, d), lambda j, i, k: (i, 0)),
        b_specs=[pl.BlockSpec((tf, d), lambda j, i, k: (j, 0))],
        extras=[hu], extra_specs=[hu_spec],
        out_shape=_sds((2, s_len, f_dim), BF16), out_specs=hu_spec, acc_shape=(tm, tf), epilogue=ep_dhu)

    hd = d // 2
    rt = hd // 2
    g_in = _matmul(
        f"{tag}_dw_in", n, [dhu], mode="tn", grid=(8, 4, 1),
        a_spec=pl.BlockSpec((s_len, rt), lambda j, i, k: (0, i)),
        b_specs=[pl.BlockSpec((None, s_len, tf), lambda j, i, k: (j // 4, 0, j % 4))],
        out_shape=_sds((2, 4, hd, f_dim // 2), F32),
        out_specs=pl.BlockSpec((None, None, rt, tf), lambda j, i, k: (i // 2, j // 2, i % 2, j % 2)),
        acc_shape=(rt, tf), epilogue=_ep_store(F32))
    after_dw_in(g_in)

    tm2 = _tile(s_len, 1024)
    dn = _matmul(
        f"{tag}_dn", dhu, [w_in_g], mode="nt", grid=(s_len // tm2, d // tn, N_CHIPS),
        a_spec=pl.BlockSpec((None, tm2, 2 * tf), lambda i, j, k: (k // 2, i, k % 2)),
        b_specs=[pl.BlockSpec((None, tn, 2 * tf), lambda i, j, k: (k, j, 0))],
        out_shape=_sds((s_len, d), F32), out_specs=pl.BlockSpec((tm2, tn), lambda i, j, k: (i, j)),
        acc_shape=(tm2, tn), epilogue=_ep_store(F32))
    return dn


def kernel(x, c, w_ada, b_ada, g_ffn1, w1_in, w1_out, g_mix, w_in, conv_w, conv_b, ln_a_g, ln_a_b, w_a_out, b_a_out, w_b_group, b_b_group, ls_b, w_out, g_ffn2, w2_in, w2_out, g_final, loss_target, m_w_ada, m_b_ada, m_g_ffn1, m_w1_in, m_w1_out, m_g_mix, m_w_in, m_conv_w, m_conv_b, m_ln_a_g, m_ln_a_b, m_w_a_out, m_b_a_out, m_w_b_group, m_b_b_group, m_ls_b, m_w_out, m_g_ffn2, m_w2_in, m_w2_out, m_g_final, v_w_ada, v_b_ada, v_g_ffn1, v_w1_in, v_w1_out, v_g_mix, v_w_in, v_conv_w, v_conv_b, v_ln_a_g, v_ln_a_b, v_w_a_out, v_b_a_out, v_w_b_group, v_b_b_group, v_ls_b, v_w_out, v_g_ffn2, v_w2_in, v_w2_out, v_g_final):
    weights = dict(w_ada=w_ada, b_ada=b_ada, g_ffn1=g_ffn1, w1_in=w1_in, w1_out=w1_out, g_mix=g_mix, w_in=w_in,
                   conv_w=conv_w, conv_b=conv_b, ln_a_g=ln_a_g, ln_a_b=ln_a_b, w_a_out=w_a_out, b_a_out=b_a_out,
                   w_b_group=w_b_group, b_b_group=b_b_group, ls_b=ls_b, w_out=w_out, g_ffn2=g_ffn2, w2_in=w2_in,
                   w2_out=w2_out, g_final=g_final)
    mom1 = dict(w_ada=m_w_ada, b_ada=m_b_ada, g_ffn1=m_g_ffn1, w1_in=m_w1_in, w1_out=m_w1_out, g_mix=m_g_mix,
                w_in=m_w_in, conv_w=m_conv_w, conv_b=m_conv_b, ln_a_g=m_ln_a_g, ln_a_b=m_ln_a_b, w_a_out=m_w_a_out,
                b_a_out=m_b_a_out, w_b_group=m_w_b_group, b_b_group=m_b_b_group, ls_b=m_ls_b, w_out=m_w_out,
                g_ffn2=m_g_ffn2, w2_in=m_w2_in, w2_out=m_w2_out, g_final=m_g_final)
    mom2 = dict(w_ada=v_w_ada, b_ada=v_b_ada, g_ffn1=v_g_ffn1, w1_in=v_w1_in, w1_out=v_w1_out, g_mix=v_g_mix,
                w_in=v_w_in, conv_w=v_conv_w, conv_b=v_conv_b, ln_a_g=v_ln_a_g, ln_a_b=v_ln_a_b, w_a_out=v_w_a_out,
                b_a_out=v_b_a_out, w_b_group=v_w_b_group, b_b_group=v_b_b_group, ls_b=v_ls_b, w_out=v_w_out,
                g_ffn2=v_g_ffn2, w2_in=v_w2_in, w2_out=v_w2_out, g_final=v_g_final)
    order = list(weights)

    s_len, d = x.shape[1], x.shape[2]
    f_dim = w1_out.shape[0] * N_CHIPS
    wc = conv_w.shape[1] * N_CHIPS
    wp = w_b_group.shape[0] * w_b_group.shape[1]
    n_groups, gi, goq = w_b_group.shape
    npc = w_in.shape[1]
    ada_c = w_ada.shape[1]
    dims = dict(S=s_len, D=d, F=f_dim)
    ts = _tile(s_len, 256)

    xi, yi, ci = lax.axis_index("x"), lax.axis_index("y"), lax.axis_index("c")
    q = 2 * xi + yi
    dev = 2 * q + ci
    q_idx = jnp.reshape(q, (1,)).astype(jnp.int32)
    qc_idx = jnp.stack([q, ci]).astype(jnp.int32)
    _PREVIOUS.clear()

    cwq = conv_w.shape[1]
    pack0 = jnp.concatenate([c.reshape(-1), conv_w.reshape(-1), b_b_group.reshape(-1)])
    n0 = -(-pack0.shape[0] // (8 * LANES)) * LANES
    pack0 = jnp.pad(pack0, (0, 8 * n0 - pack0.shape[0])).reshape(8, n0)
    g0 = _allgather_small("gather_small_in", pack0).reshape(N_DEV, 8 * n0)
    c_all = g0[:, :d]
    south = g0[0::2]
    cw_full = jnp.concatenate([south[k, d:d + CONV_K * cwq].reshape(CONV_K, cwq) for k in range(N_CHIPS)], axis=1)
    cw_pad = jnp.pad(cw_full, ((0, HALO - CONV_K), (0, 0)))
    o_bb = d + CONV_K * cwq
    bb_full = jnp.concatenate([south[k, o_bb:o_bb + n_groups * goq].reshape(n_groups, goq) for k in range(N_CHIPS)],
                              axis=1).reshape(1, d)

    as2d = lambda a: a.reshape(-1, a.shape[-1])
    groups = dict(w1_out=["w1_out"], mix=["w_a_out", "w_b_group", "w_out"], w2_in=["w2_in"], w2_out=["w2_out"])
    big = ["w1_in", "w1_out", "w_in", "w_a_out", "w_b_group", "w_out", "w2_in", "w2_out"]
    cast = lambda nm: _cast_into_gathered(f"cast_{nm}", as2d(weights[nm]), q_idx)
    w1_in_gather = _TwoPartGather("w1_in", cast("w1_in"))

    b_ada_mine = lax.dynamic_slice(b_ada, (q * ada_c,), (ada_c,)).reshape(1, ada_c)
    ada_piece = _ada_fwd("ada_fwd", c_all, w_ada, b_ada_mine)
    casts = {nm: cast(nm) for nm in big[1:]}
    g1 = _allgather_small("gather_ada", ada_piece).reshape(N_DEV, N_DEV, ada_c)
    w1_in_gather.start_second()
    ici = {}
    for grp, names in groups.items():
        ici[grp] = _gather_ici(f"gather_{grp}_ici", [casts[nm] for nm in names])
        if grp == "w1_out":
            w_in_gather = _TwoPartGather("w_in", casts["w_in"])
            w_in_gather.start_second()
    ada_rows = lax.dynamic_index_in_dim(g1[0::2], dev, axis=1, keepdims=False)
    ada = ada_rows.reshape(3, 3, 1, d)
    (sh1, sc1, gt1), (sh2, sc2, gt2), (sh3, sc3, gt3) = [[ada[i, j] for j in range(3)] for i in range(3)]

    row = lambda vct: vct.reshape(1, -1)
    g1v, gmv, g2v, gfv = row(g_ffn1), row(g_mix), row(g_ffn2), row(g_final)

    def arrived(grp):
        return _gather_d2d(f"gather_{grp}_d2d", ici[grp].wait())

    def gathered(fwd, grp):
        return {nm: g.reshape(N_CHIPS, 2 * g.shape[2], g.shape[3]) for nm, g in zip(groups[grp], fwd.wait())}

    x2 = x[0]
    tgt = loss_target[0]

    n1 = _norm_mod("ffn1_norm", x2, g1v, sc1, sh1, ts)
    fwd, w1_in_parts = {}, []

    def w1_in_part(part):
        def get():
            w1_in_gather.arrive(part)
            w1_in_parts.append(w1_in_gather.ready(part))
            return w1_in_parts[-1]
        return get

    def w1_out_after_swiglu():
        fwd["w1_out"] = arrived("w1_out")
        w_in_gather.arrive(0)
        return gathered(fwd["w1_out"], "w1_out")["w1_out"].reshape(f_dim, d)

    hu1, act1, f1, w1_out_2d = _ffn_fwd("ffn1", n1, [w1_in_part(0), w1_in_part(1)], w1_out_after_swiglu, dims)
    w1_in_g = w1_in_parts[-1]
    h1, n2 = _residual_norm_mod("mix_norm", x2, f1, gt1, 0.5, gmv, sc2, sh2, ts)

    tm = _tile(s_len, 1024)
    tnp = npc // 2
    proj = ()
    for part in range(2):
        if part:
            w_in_gather.arrive(part)
        w_in_g = w_in_gather.ready(part)
        proj = (_matmul(
            f"mix_proj{part}", n2, [w_in_g], mode="nn", grid=(s_len // tm, N_CHIPS, 1),
            a_spec=pl.BlockSpec((tm, d), lambda i, j, k: (i, 0)),
            b_specs=[pl.BlockSpec((None, d, tnp), lambda i, j, k, part=part: (j, 0, part))],
            out_shape=_sds((N_CHIPS, s_len, npc), BF16),
            out_specs=pl.BlockSpec((None, tm, tnp), lambda i, j, k, part=part: (j, i, part)),
            acc_shape=(tm, tnp), epilogue=_ep_store(BF16), carry=proj),)
    proj = proj[0]
    fwd["mix"] = arrived("mix")
    cbv, lgv, lbv = row(conv_b), row(ln_a_g), row(ln_a_b)
    a3, mixed, conv_out = _mixer_mid("mix_mid", proj, cw_pad, cbv, lgv, lbv, wc, wp, ts)
    wts = gathered(fwd["mix"], "mix")
    w_out_2d = wts["w_out"].reshape(d, d)
    w_a_g = wts["w_a_out"]
    w_b_r = _regroup("regroup_w_b", wts["w_b_group"], n_groups)
    dq = d // N_CHIPS
    ya = _matmul(
        "mix_ya", a3, [w_a_g], mode="nn", grid=(s_len // tm, N_CHIPS, 1),
        a_spec=pl.BlockSpec((tm, wc), lambda i, j, k: (i, 0)),
        b_specs=[pl.BlockSpec((None, wc, dq), lambda i, j, k: (j, 0, 0))],
        out_shape=_sds((s_len, d), BF16), out_specs=pl.BlockSpec((tm, dq), lambda i, j, k: (i, j)),
        acc_shape=(tm, dq), epilogue=_ep_store(BF16))
    yb = _matmul(
        "mix_yb", mixed, [w_b_r], mode="nn", grid=(s_len // tm, n_groups, 1),
        a_spec=pl.BlockSpec((tm, gi), lambda i, j, k: (i, j)),
        b_specs=[pl.BlockSpec((None, gi, dq), lambda i, j, k: (j, 0, 0))],
        out_shape=_sds((s_len, d), BF16), out_specs=pl.BlockSpec((tm, dq), lambda i, j, k: (i, j)),
        acc_shape=(tm, dq), epilogue=_ep_store(BF16))
    bav, lsv = row(b_a_out), row(ls_b)
    z = _gates_fwd("mix_gates", proj, ya, yb, bav, bb_full, lsv, wc, wp, ts)
    tn = _tile(d, 1024)
    mix = _matmul(
        "mix_out", z, [w_out_2d], mode="nn", grid=(s_len // tm, d // tn, 1),
        a_spec=pl.BlockSpec((tm, d), lambda i, j, k: (i, 0)),
        b_specs=[pl.BlockSpec((d, tn), lambda i, j, k: (0, j))],
        out_shape=_sds((s_len, d), F32), out_specs=pl.BlockSpec((tm, tn), lambda i, j, k: (i, j)),
        acc_shape=(tm, tn), epilogue=_ep_store(F32))
    fwd["w2_in"] = arrived("w2_in")
    h2, n3 = _residual_norm_mod("ffn2_norm", h1, mix, gt2, 1.0, g2v, sc3, sh3, ts)
    w2_in_g = gathered(fwd["w2_in"], "w2_in")["w2_in"]
    hu2, act2, f3, w2_out_2d = _ffn_fwd(
        "ffn2", n3, [lambda: w2_in_g],
        lambda: gathered(arrived("w2_out"), "w2_out")["w2_out"].reshape(f_dim, d), dims)

    dh3, df3, d_gf, d_gt3, loss_cols = _final_loss("final_loss", h2, f3, tgt, gt3, 0.5, gfv, ts)
    rs, held = {}, {}
    dn3 = _ffn_bwd(
        "ffn2", n3, hu2, act2, df3, w2_in_g, w2_out_2d, dims,
        after_dw_out=lambda g: held.update(w2_out=g),
        after_dw_in=lambda g: rs.update(ffn2=_ReduceScatter("g_ffn2", ["w2_out", "w2_in"], [held["w2_out"], g],
                                                            qc_idx)))
    dh2, dmix, d_sh3, d_sc3, d_g2, d_gt2 = _norm_mod_bwd("ffn2_norm_bwd", h2, dn3, dh3, g2v, sc3, ts,
                                                         prev=(mix, gt2, 1.0))
    rs["ffn2"].step2()

    tk = s_len
    hq = d // (2 * N_CHIPS)
    gw_out = _matmul(
        "mix_dw_out", z, [dmix], mode="tn", grid=(N_CHIPS, d // tn, s_len // tk),
        a_spec=pl.BlockSpec((tk, 2 * hq), lambda i, j, k: (k, i)),
        b_specs=[pl.BlockSpec((tk, tn), lambda i, j, k: (k, j))],
        out_shape=_sds((2, N_CHIPS, hq, d), F32),
        out_specs=pl.BlockSpec((2, None, hq, tn), lambda i, j, k: (0, i, 0, j)),
        acc_shape=(2 * hq, tn), epilogue=_ep_halves(hq))
    dz = _matmul(
        "mix_dz", dmix, [w_out_2d], mode="nt", grid=(s_len // tm, d // tn, 1),
        a_spec=pl.BlockSpec((tm, d), lambda i, j, k: (i, 0)),
        b_specs=[pl.BlockSpec((tn, d), lambda i, j, k: (j, 0))],
        out_shape=_sds((s_len, d), F32), out_specs=pl.BlockSpec((tm, tn), lambda i, j, k: (i, j)),
        acc_shape=(tm, tn), epilogue=_ep_store(F32))
    dya, dyb, dgates, d_ba, d_ls, d_bb = _gates_bwd("mix_gates_bwd", proj, dz, ya, yb, bav, bb_full, lsv, wc, wp, ts)
    gw_a = _matmul(
        "mix_dw_a", a3, [dya], mode="tn", grid=(1, N_CHIPS, s_len // tk),
        a_spec=pl.BlockSpec((tk, wc), lambda i, j, k: (k, 0)),
        b_specs=[pl.BlockSpec((tk, dq), lambda i, j, k: (k, j))],
        out_shape=_sds((2, N_CHIPS, wc // 2, dq), F32),
        out_specs=pl.BlockSpec((2, None, wc // 2, dq), lambda i, j, k: (0, j, 0, 0)),
        acc_shape=(wc, dq), epilogue=_ep_halves(wc // 2))
    da3 = _matmul(
        "mix_da3", dya, [w_a_g], mode="nt", grid=(s_len // tm, 1, N_CHIPS),
        a_spec=pl.BlockSpec((tm, dq), lambda i, j, k: (i, k)),
        b_specs=[pl.BlockSpec((None, wc, dq), lambda i, j, k: (k, 0, 0))],
        out_shape=_sds((s_len, wc), F32), out_specs=pl.BlockSpec((tm, wc), lambda i, j, k: (i, 0)),
        acc_shape=(tm, wc), epilogue=_ep_store(F32))
    gpr = n_groups // 2

    def ep_by_chip(accs, ex, outs):
        for k in range(N_CHIPS):
            outs[0][k] = accs[0][:, k * goq:(k + 1) * goq]

    gw_b = _matmul(
        "mix_dw_b", mixed, [dyb], mode="tn", grid=(1, n_groups, s_len // tk),
        a_spec=pl.BlockSpec((tk, gi), lambda i, j, k: (k, j)),
        b_specs=[pl.BlockSpec((tk, dq), lambda i, j, k: (k, j))],
        out_shape=_sds((2, N_CHIPS, gpr * gi, goq), F32),
        out_specs=pl.BlockSpec((None, N_CHIPS, gi, goq), lambda i, j, k: (j // gpr, 0, j % gpr, 0)),
        acc_shape=(gi, dq), epilogue=ep_by_chip)
    dmixed = _matmul(
        "mix_dmixed", dyb, [w_b_r], mode="nt", grid=(s_len // tm, n_groups, 1),
        a_spec=pl.BlockSpec((tm, dq), lambda i, j, k: (i, j)),
        b_specs=[pl.BlockSpec((None, gi, dq), lambda i, j, k: (j, 0, 0))],
        out_shape=_sds((s_len, wp), F32), out_specs=pl.BlockSpec((tm, gi), lambda i, j, k: (i, j)),
        acc_shape=(tm, gi), epilogue=_ep_store(F32))
    da1, d_lg, d_lb, d_cb, d_cw = _conv_branch_bwd("mix_conv_bwd", proj, conv_out, da3, lgv, lbv, wc, wp, ts)
    dproj = _mixer_in_bwd("mix_in_bwd", proj, da1, dmixed, dgates, cw_pad, wc, wp, ts)
    hd = d // 2
    gw_in = _matmul(
        "mix_dw_in", n2, [dproj], mode="tn", grid=(2, 8, s_len // tk),
        a_spec=pl.BlockSpec((tk, hd), lambda i, j, k: (k, i)),
        b_specs=[pl.BlockSpec((None, tk, tnp), lambda i, j, k: (j // 2, k, j % 2))],
        out_shape=_sds((2, N_CHIPS, hd, npc), F32),
        out_specs=pl.BlockSpec((None, None, hd, tnp), lambda i, j, k: (i, j // 2, 0, j % 2)),
        acc_shape=(hd, tnp), epilogue=_ep_store(F32))
    rs["mix"] = _ReduceScatter("g_mix", ["w_in", "w_a_out", "w_b_group", "w_out"], [gw_in, gw_a, gw_b, gw_out],
                               qc_idx)
    rs["ffn2"].step3()
    dn2 = _matmul(
        "mix_dn", dproj, [w_in_g], mode="nt", grid=(s_len // tm, d // tn, N_CHIPS),
        a_spec=pl.BlockSpec((None, tm, npc), lambda i, j, k: (k, i, 0)),
        b_specs=[pl.BlockSpec((None, tn, npc), lambda i, j, k: (k, j, 0))],
        out_shape=_sds((s_len, d), F32), out_specs=pl.BlockSpec((tm, tn), lambda i, j, k: (i, j)),
        acc_shape=(tm, tn), epilogue=_ep_store(F32))
    dh1, df1, d_sh2, d_sc2, d_gm, d_gt1 = _norm_mod_bwd("mix_norm_bwd", h1, dn2, dh2, gmv, sc2, ts,
                                                        prev=(f1, gt1, 0.5))
    rs["mix"].step2()

    def w1_in_ready(g):
        rs["w1_in"] = _ReduceScatter("g_w1_in", ["w1_in"], [g], qc_idx)
        rs["w1_out"].step2()
        rs["mix"].step3()

    dn1 = _ffn_bwd(
        "ffn1", n1, hu1, act1, df1, w1_in_g, w1_out_2d, dims,
        after_dw_out=lambda g: rs.update(w1_out=_ReduceScatter("g_w1_out", ["w1_out"], [g], qc_idx)),
        after_dw_in=w1_in_ready)
    grad_x, d_sh1, d_sc1, d_g1 = _norm_mod_bwd("ffn1_norm_bwd", x2, dn1, dh1, g1v, sc1, ts)

    d_ada = jnp.concatenate([d_sh1, d_sc1, d_gt1, d_sh2, d_sc2, d_gt2, d_sh3, d_sc3, d_gt3], axis=1)
    small = [d_ada, d_g1, d_gm, d_cw[:CONV_K].reshape(1, -1), d_cb, d_lg, d_lb, d_ba, d_bb, d_ls, d_g2, d_gf,
             loss_cols]
    sizes = [a.shape[1] for a in small]
    pack1 = jnp.concatenate(small, axis=1).reshape(-1)
    n1p = -(-pack1.shape[0] // (8 * LANES)) * LANES
    pack1 = jnp.pad(pack1, (0, 8 * n1p - pack1.shape[0])).reshape(8, n1p)
    g2 = _allgather_small("gather_small_grads", pack1)
    rs["w1_in"].step2()
    total = _sum_devices("sum_small_grads", g2, 8).reshape(-1)
    offs = [0]
    for sz in sizes:
        offs.append(offs[-1] + sz)
    tot = [total[offs[k]:offs[k + 1]] for k in range(len(sizes))]
    d_ada_all = g2.reshape(N_DEV, 8 * n1p)[:, :sizes[0]]
    loss = jnp.sum(tot[12])

    grads = {}
    grads["b_ada"] = tot[0]
    grads["g_ffn1"], grads["g_mix"] = tot[1], tot[2]
    grads["conv_w"] = lax.dynamic_slice(tot[3].reshape(CONV_K, wc), (0, q * cwq), (CONV_K, cwq))
    grads["conv_b"], grads["ln_a_g"], grads["ln_a_b"], grads["b_a_out"] = tot[4], tot[5], tot[6], tot[7]
    grads["b_b_group"] = lax.dynamic_slice(tot[8].reshape(n_groups, N_CHIPS * goq), (0, q * goq), (n_groups, goq))
    grads["ls_b"], grads["g_ffn2"], grads["g_final"] = tot[9], tot[10], tot[11]

    delta, new_m, new_v = {}, {}, {}

    def adamw_group(reduced):
        for nm, g in reduced.items():
            shp = weights[nm].shape
            go, dl, mo, vo = _adamw(f"adamw_{nm}", as2d(weights[nm]), g, as2d(mom1[nm]), as2d(mom2[nm]))
            grads[nm], delta[nm], new_m[nm], new_v[nm] = go.reshape(shp), dl.reshape(shp), mo.reshape(shp), vo.reshape(shp)

    adamw_group(rs["ffn2"].result())
    rs["w1_out"].step3()
    adamw_group(rs["mix"].result())
    d_ada_mine = lax.dynamic_slice(d_ada_all, (0, q * ada_c), (N_DEV, ada_c))
    grads["w_ada"], delta["w_ada"], new_m["w_ada"], new_v["w_ada"] = _ada_grad_adamw(
        "adamw_w_ada", c_all.T, d_ada_mine, w_ada, m_w_ada, v_w_ada)
    rs["w1_in"].step3()
    smalls = [nm for nm in order if nm not in big and nm != "w_ada"]
    flat = lambda src: jnp.concatenate([src[nm].reshape(-1) for nm in smalls])
    n_small = sum(weights[nm].size for nm in smalls)
    rows_s = -(-n_small // (8 * LANES)) * 8
    packed = [jnp.pad(flat(src), (0, rows_s * LANES - n_small)).reshape(rows_s, LANES)
              for src in (weights, grads, mom1, mom2)]
    _, dl_s, mo_s, vo_s = _adamw("adamw_small", *packed)
    off = 0
    for nm in smalls:
        sz, shp = weights[nm].size, weights[nm].shape
        delta[nm] = dl_s.reshape(-1)[off:off + sz].reshape(shp)
        new_m[nm] = mo_s.reshape(-1)[off:off + sz].reshape(shp)
        new_v[nm] = vo_s.reshape(-1)[off:off + sz].reshape(shp)
        grads[nm] = grads[nm].reshape(shp)
        off += sz
    adamw_group(rs["w1_out"].result())
    adamw_group(rs["w1_in"].result())

    return (loss, grad_x[None], *[grads[nm] for nm in order], *[delta[nm] for nm in order],
            *[new_m[nm] for nm in order], *[new_v[nm] for nm in order])
```

```python
import jax
import jax.numpy as jnp
from jax import lax
from jax.experimental import pallas as pl
from jax.experimental.pallas import tpu as pltpu

F32 = jnp.float32
BF16 = jnp.bfloat16
MESH = pl.DeviceIdType.MESH
ANY = pl.BlockSpec(memory_space=pl.ANY)
HBM = pl.BlockSpec(memory_space=pltpu.HBM)
SEM = pl.BlockSpec(memory_space=pltpu.SEMAPHORE)
EFFECT = pltpu.SideEffectType.DATAFLOW_SIDE_EFFECTING

EPS = 1e-6
CONV_K = 31
HALO = 32
POOL_WINDOWS = (2, 4, 8, 16)
N_CHIPS = 4
N_DEV = 8
LANES = 128

ADAM_LR = 0.001
ADAM_B1 = 0.9
ADAM_B2 = 0.999
ADAM_EPS = 1e-08
ADAM_WD = 0.01
ADAM_STEP = 10

DN = {
    "nn": (((1,), (0,)), ((), ())),
    "nt": (((1,), (1,)), ((), ())),
    "tn": (((0,), (0,)), ((), ())),
}


_PREVIOUS = []


def _ordered(call, args, n_lead, body, token=None, sources=()):
    dep = [pltpu.with_memory_space_constraint(p, pltpu.HBM) if p.size * p.dtype.itemsize >= (1 << 20) else p
           for p in _PREVIOUS if all(p is not a for a in (*args, *sources))]

    def wrapped(*refs):
        return body(*refs[:n_lead], *refs[n_lead + len(dep):])

    outs = call(wrapped, [ANY] * len(dep))(*args, *dep)
    seq = outs if isinstance(outs, (list, tuple)) else [outs]
    _PREVIOUS[:] = [seq[token] if token is not None else
                    next(o for o in seq if jnp.issubdtype(o.dtype, jnp.floating))]
    return outs


def _pcall(body, *, name, out_shape, grid=None, in_specs=None, out_specs=None, scratch=(), aliases=None,
           prefetch=0, vmem_mb=None):
    params = {}
    if grid is not None:
        params["dimension_semantics"] = ("arbitrary",) * len(grid)
    if vmem_mb is not None:
        params["vmem_limit_bytes"] = vmem_mb << 20
    def in_hbm(shape, spec):
        big = shape.size * jnp.dtype(shape.dtype).itemsize >= (1 << 20)
        return pltpu.HBM(shape.shape, shape.dtype) if big and getattr(spec, "memory_space", None) != pltpu.VMEM else shape

    if isinstance(out_shape, (list, tuple)):
        out_shape = [in_hbm(s, sp) for s, sp in zip(out_shape, out_specs)]
    else:
        out_shape = in_hbm(out_shape, out_specs)
    kw = dict(name=name, out_shape=out_shape, compiler_params=pltpu.CompilerParams(**params))
    if aliases:
        kw["input_output_aliases"] = aliases

    def call(wrapped, dep_specs):
        specs = list(in_specs) + dep_specs
        if prefetch:
            return pl.pallas_call(wrapped, grid_spec=pltpu.PrefetchScalarGridSpec(
                num_scalar_prefetch=prefetch, grid=grid, in_specs=specs, out_specs=out_specs,
                scratch_shapes=list(scratch)), **kw)
        if grid is not None:
            return pl.pallas_call(wrapped, grid=grid, in_specs=specs, out_specs=out_specs,
                                  scratch_shapes=list(scratch), **kw)
        return pl.pallas_call(wrapped, in_specs=specs, out_specs=out_specs, scratch_shapes=list(scratch), **kw)

    def run(*args):
        specs = [None] * prefetch + list(in_specs)
        placed = [pltpu.with_memory_space_constraint(a, pltpu.HBM)
                  if a.size * a.dtype.itemsize >= (1 << 20) and getattr(s, "memory_space", None) != pltpu.VMEM else a
                  for a, s in zip(args, specs)]
        return _ordered(call, placed, prefetch + len(in_specs), body, sources=args)

    return run


def _tile(dim, pref):
    t = min(dim, pref)
    assert dim % t == 0, (dim, pref)
    return t


def _sds(shape, dtype):
    return jax.ShapeDtypeStruct(tuple(shape), dtype)


def _sigmoid(v):
    return 0.5 * jnp.tanh(0.5 * v) + 0.5


def _vec(w):
    return pl.BlockSpec((1, w), lambda *_: (0, 0))


def _acc_rows(ref, val, i):
    @pl.when(i == 0)
    def _():
        ref[...] = jnp.zeros_like(ref)

    ref[...] += jnp.sum(val, axis=0, keepdims=True)


def _matmul(name, a, bs, *, mode, grid, a_spec, b_specs, out_shape, out_specs, acc_shape, epilogue,
            extras=(), extra_specs=(), vmem_mb=56, carry=()):
    nb, ne, nk, nc = len(bs), len(extras), grid[2], len(carry)
    dn = DN[mode]

    def body(*all_refs):
        refs = all_refs[:1 + nb + ne] + all_refs[1 + nb + ne + nc:]
        a_ref, b_refs, ex = refs[0], refs[1:1 + nb], refs[1 + nb:1 + nb + ne]
        if nk == 1:
            outs = refs[1 + nb + ne:]
            accs = [lax.dot_general(a_ref[...], b[...], dn, preferred_element_type=F32) for b in b_refs]
            epilogue(accs, ex, outs)
            return
        outs, acc_refs = refs[1 + nb + ne:-nb], refs[-nb:]
        k = pl.program_id(2)

        @pl.when(k == 0)
        def _():
            for acc in acc_refs:
                acc[...] = jnp.zeros_like(acc)

        for acc, b in zip(acc_refs, b_refs):
            acc[...] += lax.dot_general(a_ref[...], b[...], dn, preferred_element_type=F32)

        @pl.when(k == nk - 1)
        def _():
            epilogue([acc[...] for acc in acc_refs], ex, outs)

    scratch = [pltpu.VMEM(acc_shape, F32) for _ in range(nb)] if nk > 1 else []
    return _pcall(body, name=name, out_shape=out_shape, grid=grid,
                  in_specs=[a_spec, *b_specs, *extra_specs, *[ANY] * nc], out_specs=out_specs, scratch=scratch,
                  aliases={1 + nb + ne + i: i for i in range(nc)}, vmem_mb=vmem_mb)(a, *bs, *extras, *carry)


def _ep_store(dtype):
    def ep(accs, ex, outs):
        outs[0][...] = accs[0].astype(dtype)
    return ep


def _ep_halves(h):
    def ep(accs, ex, outs):
        outs[0][0] = accs[0][:h]
        outs[0][1] = accs[0][h:]
    return ep


def _place():
    x, y, c = lax.axis_index("x"), lax.axis_index("y"), lax.axis_index("c")
    chips = [(1 - x, y), (x, 1 - y), (1 - x, 1 - y)]
    return x, y, c, chips


def _allgather_small(name, block):
    m_per, n = block.shape

    def body(x_ref, out_ref, send_sems, recv_sems, local_sem):
        x, y, c, chips = _place()
        me, sibling = (x, y, c), (x, y, 1 - c)

        def rows(px, py, pc):
            return out_ref.at[pl.ds((4 * px + 2 * py + pc) * m_per, m_per), :]

        def copy(k, blk, to, src=None):
            return pltpu.make_async_remote_copy(
                src_ref=rows(*blk) if src is None else src, dst_ref=rows(*blk),
                send_sem=send_sems.at[k], recv_sem=recv_sems.at[k], device_id=to, device_id_type=MESH)

        mine = pltpu.make_async_copy(x_ref, rows(*me), local_sem)
        mine.start()
        first = [copy(0, me, sibling, src=x_ref)]
        first += [copy(1 + j, me, (*chip, c), src=x_ref) for j, chip in enumerate(chips)]
        for cp in first:
            cp.start()
        passed = [copy(4 + j, (*chip, c), sibling) for j, chip in enumerate(chips)]
        for j, chip in enumerate(chips):
            copy(1 + j, (*chip, c), me).wait_recv()
            passed[j].start()
        copy(0, sibling, me).wait_recv()
        for j, chip in enumerate(chips):
            copy(4 + j, (*chip, 1 - c), me).wait_recv()
        for cp in first + passed:
            cp.wait_send()
        mine.wait()

    return _pcall(
        body, name=name, out_shape=_sds((N_DEV * m_per, n), block.dtype),
        in_specs=[pl.BlockSpec(memory_space=pltpu.VMEM)], out_specs=pl.BlockSpec(memory_space=pltpu.VMEM),
        scratch=[pltpu.SemaphoreType.DMA((7,)), pltpu.SemaphoreType.DMA((7,)), pltpu.SemaphoreType.DMA],
    )(block)


class _SplitCopies:
    def __init__(self, name, arrays, plan, n_copies):
        self.name, self.plan, self.n = name, plan, len(arrays)
        n = self.n

        def body(*refs):
            send, recv, token = refs[n], refs[n + 1], refs[-1]
            for k, (src, dst, _, peer) in enumerate(plan(refs[:n])):
                pltpu.make_async_remote_copy(src_ref=src, dst_ref=dst, send_sem=send.at[k], recv_sem=recv.at[k],
                                             device_id=peer, device_id_type=MESH).start()
            token[...] = jnp.zeros_like(token)

        def call(wrapped, dep_specs):
            return pl.pallas_call(
                wrapped, name=f"{name}_start",
                out_shape=(pltpu.SemaphoreType.DMA((n_copies,)), pltpu.SemaphoreType.DMA((n_copies,)),
                           *[pltpu.HBM(a.shape, a.dtype) for a in arrays], _sds((8, LANES), F32)),
                in_specs=[HBM] * n + dep_specs,
                out_specs=(SEM, SEM, *[HBM] * n, pl.BlockSpec(memory_space=pltpu.VMEM)),
                input_output_aliases={i: 2 + i for i in range(n)},
                compiler_params=pltpu.CompilerParams(has_side_effects=EFFECT))

        outs = _ordered(call, [pltpu.with_memory_space_constraint(a, pltpu.HBM) for a in arrays], n, body, token=-1,
                        sources=arrays)
        self.send, self.recv, self.arrays = outs[0], outs[1], list(outs[2:2 + n])

    def wait(self, arrays=None):
        n, plan = self.n, self.plan
        if arrays is not None:
            self.arrays = list(arrays)

        def body(*refs):
            send, recv, token = refs[n], refs[n + 1], refs[-1]
            for k, (src, _, landing, peer) in enumerate(plan(refs[:n])):
                cp = pltpu.make_async_remote_copy(src_ref=src, dst_ref=landing, send_sem=send.at[k],
                                                  recv_sem=recv.at[k], device_id=peer, device_id_type=MESH)
                cp.wait_send()
                cp.wait_recv()
            token[...] = jnp.zeros_like(token)

        def call(wrapped, dep_specs):
            return pl.pallas_call(
                wrapped, name=f"{self.name}_wait",
                out_shape=(*[pltpu.HBM(a.shape, a.dtype) for a in self.arrays], _sds((8, LANES), F32)),
                in_specs=[HBM] * n + [SEM, SEM] + dep_specs,
                out_specs=(*[HBM] * n, pl.BlockSpec(memory_space=pltpu.VMEM)),
                input_output_aliases={i: i for i in range(n)},
                compiler_params=pltpu.CompilerParams(has_side_effects=EFFECT))

        return list(_ordered(call, [*self.arrays, self.send, self.recv], n + 2, body, token=-1))[:n]


def _col_range(g, part, n_parts):
    width = g.shape[-1] // n_parts
    return (slice(None), pl.ds(part * width, width))


def _gather_ici(name, gathered, part=0, n_parts=1):
    def plan(refs):
        x, y, c, chips = _place()
        q = 2 * x + y
        return [(g.at[(q, c, *_col_range(g, part, n_parts))], g.at[(q, c, *_col_range(g, part, n_parts))],
                 g.at[(2 * px + py, c, *_col_range(g, part, n_parts))], (px, py, c))
                for g in refs for px, py in chips]

    return _SplitCopies(name, gathered, plan, 3 * len(gathered))


def _gather_d2d(name, gathered, part=0, n_parts=1):
    def plan(refs):
        x, y, c, chips = _place()
        return [(g.at[(2 * px + py, c, *_col_range(g, part, n_parts))],
                 g.at[(2 * px + py, c, *_col_range(g, part, n_parts))],
                 g.at[(2 * px + py, 1 - c, *_col_range(g, part, n_parts))], (x, y, 1 - c))
                for g in refs for px, py in chips]

    return _SplitCopies(name, gathered, plan, 3 * len(gathered))


class _TwoPartGather:
    def __init__(self, name, gathered):
        self.name, self.d2d = name, {}
        self.ici = [_gather_ici(f"gather_{name}_a_ici", [gathered], 0, 2)]
        self.buf = self.ici[0].arrays

    def start_second(self):
        self.ici.append(_gather_ici(f"gather_{self.name}_b_ici", self.buf, 1, 2))
        self.buf = self.ici[1].arrays

    def arrive(self, part):
        here = self.ici[part].wait(self.buf)
        self.d2d[part] = _gather_d2d(f"gather_{self.name}_{'ab'[part]}_d2d", here, part, 2)
        self.buf = self.d2d[part].arrays

    def ready(self, part):
        self.buf = self.d2d[part].wait(self.buf)
        g = self.buf[0]
        return g.reshape(N_CHIPS, 2 * g.shape[2], g.shape[3])


def _scatter_sibling(name, grads):
    n = len(grads)

    def plan(refs):
        x, y, c, _ = _place()
        return [(refs[w].at[1 - c], refs[n + w], refs[n + w], (x, y, 1 - c)) for w in range(n)]

    landing = [lax.empty(g.shape[1:], g.dtype) for g in grads]
    return _SplitCopies(name, [*grads, *landing], plan, n)


def _scatter_chips(name, sums):
    n = len(sums)

    def plan(refs):
        x, y, c, chips = _place()
        return [(refs[w].at[2 * px + py], refs[n + w].at[j], refs[n + w].at[j], (px, py, c))
                for w in range(n) for j, (px, py) in enumerate(chips)]

    landing = [lax.empty((3, *s.shape[1:]), s.dtype) for s in sums]
    return _SplitCopies(name, [*sums, *landing], plan, 3 * n)


def _share_final(name, finals):
    def plan(refs):
        x, y, c, _ = _place()
        return [(f.at[c], f.at[c], f.at[1 - c], (x, y, 1 - c)) for f in refs]

    return _SplitCopies(name, finals, plan, len(finals))


def _row_tile(rows, cols, budget_elems=786432):
    best = 8
    for t in range(8, rows + 1, 8):
        if rows % t == 0 and t * cols <= budget_elems:
            best = t
    return best if rows % best == 0 else rows


def _sum_with_sibling(name, grad, recv, qc_idx):
    _, _, h, cols = grad.shape
    tr = _row_tile(h, cols)

    def body(s_ref, g_ref, r_ref, own_ref, pb_ref):
        p = g_ref[...] + r_ref[...]
        pb_ref[...] = p.astype(BF16)

        @pl.when(pl.program_id(1) == s_ref[0])
        def _():
            own_ref[...] = p

    blk = pl.BlockSpec((None, tr, cols), lambda r, k, s: (k, r, 0))
    return _pcall(
        body, name=name, out_shape=[_sds((h, cols), F32), _sds((N_CHIPS, h, cols), BF16)],
        grid=(h // tr, N_CHIPS), prefetch=1,
        in_specs=[pl.BlockSpec((None, None, tr, cols), lambda r, k, s: (s[1], k, r, 0)), blk],
        out_specs=[pl.BlockSpec((tr, cols), lambda r, k, s: (r, 0)), blk], vmem_mb=32,
    )(qc_idx, grad, recv)


def _sum_chips(name, own, recv, qc_idx):
    h, cols = own.shape
    tr = _row_tile(h, cols)

    def body(s_ref, p_ref, t_ref, o_ref):
        o_ref[...] = ((p_ref[...] + t_ref[0].astype(F32)) + t_ref[1].astype(F32)) + t_ref[2].astype(F32)

    return _pcall(
        body, name=name, out_shape=_sds((2, h, cols), F32), grid=(h // tr,), prefetch=1,
        in_specs=[pl.BlockSpec((tr, cols), lambda r, s: (r, 0)),
                  pl.BlockSpec((3, tr, cols), lambda r, s: (0, r, 0))],
        out_specs=pl.BlockSpec((None, tr, cols), lambda r, s: (s[1], r, 0)), vmem_mb=32,
    )(qc_idx, own, recv)


class _ReduceScatter:
    def __init__(self, tag, names, grads, qc_idx):
        self.tag, self.names, self.n, self.qc_idx = tag, names, len(grads), qc_idx
        self.copies = _scatter_sibling(f"{tag}_rs_sibling", grads)

    def step2(self):
        n = self.n
        arrs = self.copies.wait()
        sums = [_sum_with_sibling(f"{nm}_sum_sibling", arrs[w], arrs[n + w], self.qc_idx)
                for w, nm in enumerate(self.names)]
        self.own = [s[0] for s in sums]
        self.copies = _scatter_chips(f"{self.tag}_rs_chips", [s[1] for s in sums])

    def step3(self):
        n = self.n
        arrs = self.copies.wait()
        finals = [_sum_chips(f"{nm}_sum_chips", self.own[w], arrs[n + w], self.qc_idx)
                  for w, nm in enumerate(self.names)]
        self.copies = _share_final(f"{self.tag}_rs_final", finals)

    def result(self):
        return {nm: f.reshape(2 * f.shape[1], f.shape[2]) for nm, f in zip(self.names, self.copies.wait())}


def _cast_into_gathered(name, w, q_idx):
    rows, cols = w.shape
    h = rows // 2
    tr = _row_tile(h, cols, 1 << 20)
    nr = h // tr

    def body(s_ref, w_ref, o_ref):
        o_ref[...] = w_ref[...].astype(BF16)

    return _pcall(body, name=name, out_shape=_sds((N_CHIPS, 2, h, cols), BF16), grid=(2, nr), prefetch=1,
                  in_specs=[pl.BlockSpec((tr, cols), lambda hf, r, s: (hf * nr + r, 0))],
                  out_specs=pl.BlockSpec((None, None, tr, cols), lambda hf, r, s: (s[0], hf, r, 0)),
                  vmem_mb=32)(q_idx, w)


def _regroup(name, w, n_groups):
    n_chips, rows, goq = w.shape
    gi = rows // n_groups

    def body(w_ref, o_ref):
        o_ref[...] = w_ref[...]

    return _pcall(body, name=name, out_shape=_sds((n_groups, gi, n_chips * goq), w.dtype), grid=(n_groups, n_chips),
                  in_specs=[pl.BlockSpec((None, gi, goq), lambda g, k: (k, g, 0))],
                  out_specs=pl.BlockSpec((None, gi, goq), lambda g, k: (g, 0, k)), vmem_mb=32)(w)


def _rms(h):
    r = lax.rsqrt(jnp.mean(h * h, axis=-1, keepdims=True) + EPS)
    return r, h * r


def _norm_mod(name, h, g, sc, sh, ts):
    s_len, d = h.shape

    def body(h_ref, g_ref, sc_ref, sh_ref, n_ref):
        _, xhat = _rms(h_ref[...])
        n_ref[...] = ((xhat * g_ref[...]) * (1.0 + sc_ref[...]) + sh_ref[...]).astype(BF16)

    row = pl.BlockSpec((ts, d), lambda i: (i, 0))
    return _pcall(body, name=name, out_shape=_sds((s_len, d), BF16), grid=(s_len // ts,),
                  in_specs=[row, _vec(d), _vec(d), _vec(d)], out_specs=row, vmem_mb=32)(h, g, sc, sh)


def _residual_norm_mod(name, h, f, gate, cmul, g, sc, sh, ts):
    s_len, d = h.shape

    def body(h_ref, f_ref, gt_ref, g_ref, sc_ref, sh_ref, ho_ref, n_ref):
        hn = h_ref[...] + (cmul * gt_ref[...]) * f_ref[...]
        ho_ref[...] = hn
        _, xhat = _rms(hn)
        n_ref[...] = ((xhat * g_ref[...]) * (1.0 + sc_ref[...]) + sh_ref[...]).astype(BF16)

    row = pl.BlockSpec((ts, d), lambda i: (i, 0))
    return _pcall(body, name=name, out_shape=[_sds((s_len, d), F32), _sds((s_len, d), BF16)],
                  grid=(s_len // ts,), in_specs=[row, row, _vec(d), _vec(d), _vec(d), _vec(d)],
                  out_specs=[row, row], vmem_mb=32)(h, f, gate, g, sc, sh)


def _final_loss(name, h, f, tgt, gate, cmul, g, ts):
    s_len, d = h.shape

    def body(h_ref, f_ref, t_ref, gt_ref, g_ref, dh_ref, df_ref, dg_ref, dgt_ref, loss_ref):
        i = pl.program_id(0)
        fv = f_ref[...]
        coef = cmul * gt_ref[...]
        hn = h_ref[...] + coef * fv
        r, xhat = _rms(hn)
        err = xhat * g_ref[...] - t_ref[...]
        _acc_rows(loss_ref, (0.5 / d) * (err * err), i)
        dy = err * (1.0 / d)
        _acc_rows(dg_ref, dy * xhat, i)
        dxhat = dy * g_ref[...]
        dh = r * (dxhat - xhat * jnp.mean(dxhat * xhat, axis=-1, keepdims=True))
        dh_ref[...] = dh
        _acc_rows(dgt_ref, cmul * (dh * fv), i)
        df_ref[...] = (coef * dh).astype(BF16)

    row = pl.BlockSpec((ts, d), lambda i: (i, 0))
    return _pcall(body, name=name,
                  out_shape=[_sds((s_len, d), F32), _sds((s_len, d), BF16)] + [_sds((1, d), F32)] * 3,
                  grid=(s_len // ts,), in_specs=[row, row, row, _vec(d), _vec(d)],
                  out_specs=[row, row, _vec(d), _vec(d), _vec(d)], vmem_mb=40)(h, f, tgt, gate, g)


def _norm_mod_bwd(name, h, dn, dh_next, g, sc, ts, prev=None):
    s_len, d = h.shape
    has_prev = prev is not None
    cmul = prev[2] if has_prev else None

    def body(*refs):
        if has_prev:
            h_ref, dn_ref, dhn_ref, f_ref, g_ref, sc_ref, gt_ref, dh_ref, df_ref, dsh_ref, dsc_ref, dg_ref, dgt_ref = refs
        else:
            h_ref, dn_ref, dhn_ref, g_ref, sc_ref, dh_ref, dsh_ref, dsc_ref, dg_ref = refs
        i = pl.program_id(0)
        r, xhat = _rms(h_ref[...])
        dn_v = dn_ref[...].astype(F32)
        gv = g_ref[...]
        _acc_rows(dsh_ref, dn_v, i)
        _acc_rows(dsc_ref, dn_v * (xhat * gv), i)
        dnrm = dn_v * (1.0 + sc_ref[...])
        _acc_rows(dg_ref, dnrm * xhat, i)
        dxhat = dnrm * gv
        dh = dhn_ref[...] + r * (dxhat - xhat * jnp.mean(dxhat * xhat, axis=-1, keepdims=True))
        dh_ref[...] = dh
        if has_prev:
            _acc_rows(dgt_ref, cmul * (dh * f_ref[...]), i)
            df_ref[...] = ((cmul * gt_ref[...]) * dh).astype(BF16)

    row = pl.BlockSpec((ts, d), lambda i: (i, 0))
    if has_prev:
        ins, in_specs = [h, dn, dh_next, prev[0], g, sc, prev[1]], [row, row, row, row, _vec(d), _vec(d), _vec(d)]
        out_shape = [_sds((s_len, d), F32), _sds((s_len, d), BF16)] + [_sds((1, d), F32)] * 4
        out_specs = [row, row] + [_vec(d)] * 4
    else:
        ins, in_specs = [h, dn, dh_next, g, sc], [row, row, row, _vec(d), _vec(d)]
        out_shape = [_sds((s_len, d), F32)] + [_sds((1, d), F32)] * 3
        out_specs = [row] + [_vec(d)] * 3
    return _pcall(body, name=name, out_shape=out_shape, grid=(s_len // ts,), in_specs=in_specs,
                  out_specs=out_specs, vmem_mb=40)(*ins)


def _cols(ref, lo, hi, npc, rows=slice(None)):
    parts = []
    while lo < hi:
        q, o = divmod(lo, npc)
        n = min(hi - lo, npc - o)
        parts.append(ref[q, rows, o:o + n].astype(F32))
        lo += n
    return parts[0] if len(parts) == 1 else jnp.concatenate(parts, axis=-1)


def _store_cols(ref, lo, val, npc, rows=slice(None)):
    off, width = 0, val.shape[-1]
    while off < width:
        q, o = divmod(lo + off, npc)
        n = min(width - off, npc - o)
        ref[q, rows, o:o + n] = val[:, off:off + n]
        off += n


def _chips_covering(cols, npc):
    return -(-cols // npc)


SUBLANES = 8
ROW_CHUNK = 32


def _make_phases(src_ref, ph_ref):
    rows = src_ref.shape[0] - SUBLANES
    for b in range(1, SUBLANES):
        ph_ref[b - 1] = src_ref[pl.ds(b, rows), :]


def _window(src_ref, ph_ref, off, r0, cols=slice(None)):
    a, b = divmod(off, SUBLANES)
    start = pl.multiple_of(r0 + SUBLANES * a, SUBLANES)
    if b == 0:
        return src_ref[pl.ds(start, ROW_CHUNK), cols]
    return ph_ref[b - 1, pl.ds(start, ROW_CHUNK), cols]


def _phase_scratch(rows, width):
    return pltpu.VMEM((SUBLANES - 1, rows - SUBLANES, width), F32)


def _conv(a0s_ref, a0p_ref, cw_ref, cb_ref, r0):
    a1 = cb_ref[...] + cw_ref[0:1, :] * _window(a0s_ref, a0p_ref, HALO - CONV_K + 1, r0)
    for k in range(1, CONV_K):
        a1 = a1 + cw_ref[k:k + 1, :] * _window(a0s_ref, a0p_ref, HALO - CONV_K + 1 + k, r0)
    return a1


def _layer_norm(a1, lg_ref, lb_ref):
    mu = jnp.mean(a1, axis=-1, keepdims=True)
    ctr = a1 - mu
    rstd = lax.rsqrt(jnp.mean(ctr * ctr, axis=-1, keepdims=True) + EPS)
    xh = ctr * rstd
    return xh, rstd, xh * lg_ref[...] + lb_ref[...]


def _for_chunks(ts, fn):
    def step(ci, carry):
        fn(pl.multiple_of(ci * ROW_CHUNK, ROW_CHUNK))
        return carry

    lax.fori_loop(0, ts // ROW_CHUNK, step, 0)


def _stage_glu(p_ref, ph_ref, a0s_ref, i, wc, npc, ts):
    a0 = _cols(p_ref, 0, wc, npc) * _sigmoid(_cols(p_ref, wc, 2 * wc, npc))
    a0h = _cols(ph_ref, 0, wc, npc) * _sigmoid(_cols(ph_ref, wc, 2 * wc, npc))
    a0s_ref[0:HALO, :] = jnp.where(i > 0, a0h, 0.0)
    a0s_ref[HALO:HALO + ts, :] = a0


def _mixer_mid(name, proj, cw, cb, lg, lb, wc, wp, ts):
    _, s_len, npc = proj.shape
    nq = _chips_covering(2 * wc + wp, npc)
    gi = wp // len(POOL_WINDOWS)
    hb = ts // HALO

    def body(p_ref, ph_ref, cw_ref, cb_ref, lg_ref, lb_ref, a3_ref, mx_ref, a1_ref, a0s_ref, vs_ref, a0p_ref,
             vp_ref):
        i = pl.program_id(0)
        _stage_glu(p_ref, ph_ref, a0s_ref, i, wc, npc, ts)
        vs_ref[0:HALO, :] = jnp.where(i > 0, _cols(ph_ref, 2 * wc, 2 * wc + wp, npc), 0.0)
        vs_ref[HALO:HALO + ts, :] = _cols(p_ref, 2 * wc, 2 * wc + wp, npc)
        _make_phases(a0s_ref, a0p_ref)
        _make_phases(vs_ref, vp_ref)

        def chunk(r0):
            rows = pl.ds(r0, ROW_CHUNK)
            a1 = _conv(a0s_ref, a0p_ref, cw_ref, cb_ref, r0)
            a1_ref[rows, :] = a1
            _, _, a2 = _layer_norm(a1, lg_ref, lb_ref)
            a3_ref[rows, :] = (a2 * _sigmoid(a2)).astype(BF16)
            t_abs = i * ts + r0 + lax.broadcasted_iota(jnp.int32, (ROW_CHUNK, 1), 0)
            for g, win in enumerate(POOL_WINDOWS):
                cs = slice(g * gi, (g + 1) * gi)
                v_now = _window(vs_ref, vp_ref, HALO, r0, cs)
                acc = v_now
                for dlt in range(1, win):
                    acc = acc + _window(vs_ref, vp_ref, HALO - dlt, r0, cs)
                cnt = jnp.minimum(t_abs + 1, win).astype(F32)
                mx_ref[rows, cs] = (acc / cnt - v_now).astype(BF16)

        _for_chunks(ts, chunk)

    return _pcall(
        body, name=name, out_shape=[_sds((s_len, wc), BF16), _sds((s_len, wp), BF16), _sds((s_len, wc), F32)],
        grid=(s_len // ts,),
        in_specs=[pl.BlockSpec((nq, ts, npc), lambda i: (0, i, 0)),
                  pl.BlockSpec((nq, HALO, npc), lambda i: (0, jnp.maximum(i * hb - 1, 0), 0)),
                  pl.BlockSpec((HALO, wc), lambda i: (0, 0)), _vec(wc), _vec(wc), _vec(wc)],
        out_specs=[pl.BlockSpec((ts, wc), lambda i: (i, 0)), pl.BlockSpec((ts, wp), lambda i: (i, 0)),
                   pl.BlockSpec((ts, wc), lambda i: (i, 0))],
        scratch=[pltpu.VMEM((HALO + ts, wc), F32), pltpu.VMEM((HALO + ts, wp), F32),
                 _phase_scratch(HALO + ts, wc), _phase_scratch(HALO + ts, wp)], vmem_mb=56,
    )(proj, proj, cw, cb, lg, lb)


def _gates_fwd(name, proj, ya, yb, b_a, b_b, ls, wc, wp, ts):
    _, s_len, npc = proj.shape
    d = ya.shape[1]
    g0 = 2 * wc + wp

    def body(p_ref, ya_ref, yb_ref, ba_ref, bb_ref, ls_ref, z_ref):
        ga = _sigmoid(_cols(p_ref, g0, g0 + d, npc))
        gb = _sigmoid(_cols(p_ref, g0 + d, g0 + 2 * d, npc))
        z = ga * (ya_ref[...] + ba_ref[...]) + gb * ((yb_ref[...] + bb_ref[...]) * ls_ref[...])
        z_ref[...] = z.astype(BF16)

    row = pl.BlockSpec((ts, d), lambda i: (i, 0))
    return _pcall(body, name=name, out_shape=_sds((s_len, d), BF16), grid=(s_len // ts,),
                  in_specs=[pl.BlockSpec((N_CHIPS, ts, npc), lambda i: (0, i, 0)), row, row, _vec(d), _vec(d), _vec(d)],
                  out_specs=row, vmem_mb=48)(proj, ya, yb, b_a, b_b, ls)


def _gates_bwd(name, proj, dz, ya, yb, b_a, b_b, ls, wc, wp, ts):
    _, s_len, npc = proj.shape
    d = ya.shape[1]
    g0 = 2 * wc + wp

    def body(p_ref, dz_ref, ya_ref, yb_ref, ba_ref, bb_ref, ls_ref, dya_ref, dyb_ref, dgt_ref, dba_ref, dls_ref,
             dbb_ref):
        i = pl.program_id(0)
        ga = _sigmoid(_cols(p_ref, g0, g0 + d, npc))
        gb = _sigmoid(_cols(p_ref, g0 + d, g0 + 2 * d, npc))
        dz_v = dz_ref[...].astype(F32)
        y_a = ya_ref[...] + ba_ref[...]
        y_b0 = yb_ref[...] + bb_ref[...]
        ls_v = ls_ref[...]
        dya = dz_v * ga
        dya_ref[...] = dya.astype(BF16)
        _acc_rows(dba_ref, dya, i)
        t = dz_v * gb
        _acc_rows(dls_ref, t * y_b0, i)
        dyb = t * ls_v
        dyb_ref[...] = dyb.astype(BF16)
        _acc_rows(dbb_ref, dyb, i)
        dgt_ref[:, 0:d] = (dz_v * y_a * ga * (1.0 - ga)).astype(BF16)
        dgt_ref[:, d:2 * d] = (dz_v * (y_b0 * ls_v) * gb * (1.0 - gb)).astype(BF16)

    row = pl.BlockSpec((ts, d), lambda i: (i, 0))
    return _pcall(
        body, name=name,
        out_shape=[_sds((s_len, d), BF16), _sds((s_len, d), BF16), _sds((s_len, 2 * d), BF16)] + [_sds((1, d), F32)] * 3,
        grid=(s_len // ts,),
        in_specs=[pl.BlockSpec((N_CHIPS, ts, npc), lambda i: (0, i, 0)), row, row, row, _vec(d), _vec(d), _vec(d)],
        out_specs=[row, row, pl.BlockSpec((ts, 2 * d), lambda i: (i, 0))] + [_vec(d)] * 3, vmem_mb=48,
    )(proj, dz, ya, yb, b_a, b_b, ls)


def _conv_branch_bwd(name, proj, a1, da3, lg, lb, wc, wp, ts):
    _, s_len, npc = proj.shape
    nq = _chips_covering(2 * wc, npc)
    hb = ts // HALO

    n_tiles = s_len // ts

    def fold(v):
        return jnp.sum(v.reshape(ROW_CHUNK // SUBLANES, SUBLANES, v.shape[-1]), axis=0)

    def body(p_ref, ph_ref, a1_ref, da3_ref, lg_ref, lb_ref, da1_ref, dlg_ref, dlb_ref, dcb_ref, dcw_ref,
             a0s_ref, a0p_ref, vec8_ref, dcw8_ref):
        i = pl.program_id(0)
        _stage_glu(p_ref, ph_ref, a0s_ref, i, wc, npc, ts)
        _make_phases(a0s_ref, a0p_ref)

        @pl.when(i == 0)
        def _():
            vec8_ref[...] = jnp.zeros_like(vec8_ref)
            dcw8_ref[...] = jnp.zeros_like(dcw8_ref)

        def chunk(r0):
            rows = pl.ds(r0, ROW_CHUNK)
            xh, rstd, a2 = _layer_norm(a1_ref[rows, :], lg_ref, lb_ref)
            sig = _sigmoid(a2)
            da2 = da3_ref[rows, :] * (sig * (1.0 + a2 * (1.0 - sig)))
            vec8_ref[0] += fold(da2 * xh)
            vec8_ref[1] += fold(da2)
            dxh = da2 * lg_ref[...]
            da1 = rstd * (dxh - jnp.mean(dxh, axis=-1, keepdims=True)
                          - xh * jnp.mean(dxh * xh, axis=-1, keepdims=True))
            da1_ref[rows, :] = da1
            vec8_ref[2] += fold(da1)
            for k in range(CONV_K):
                dcw8_ref[k] += fold(da1 * _window(a0s_ref, a0p_ref, HALO - CONV_K + 1 + k, r0))

        _for_chunks(ts, chunk)

        @pl.when(i == n_tiles - 1)
        def _():
            dlg_ref[...] = jnp.sum(vec8_ref[0], axis=0, keepdims=True)
            dlb_ref[...] = jnp.sum(vec8_ref[1], axis=0, keepdims=True)
            dcb_ref[...] = jnp.sum(vec8_ref[2], axis=0, keepdims=True)
            dcw_ref[...] = jnp.sum(dcw8_ref[...], axis=1)

    return _pcall(
        body, name=name,
        out_shape=[_sds((s_len, wc), F32)] + [_sds((1, wc), F32)] * 3 + [_sds((HALO, wc), F32)],
        grid=(s_len // ts,),
        in_specs=[pl.BlockSpec((nq, ts, npc), lambda i: (0, i, 0)),
                  pl.BlockSpec((nq, HALO, npc), lambda i: (0, jnp.maximum(i * hb - 1, 0), 0)),
                  pl.BlockSpec((ts, wc), lambda i: (i, 0)), pl.BlockSpec((ts, wc), lambda i: (i, 0)),
                  _vec(wc), _vec(wc)],
        out_specs=[pl.BlockSpec((ts, wc), lambda i: (i, 0)), _vec(wc), _vec(wc), _vec(wc),
                   pl.BlockSpec((HALO, wc), lambda i: (0, 0))],
        scratch=[pltpu.VMEM((HALO + ts, wc), F32), _phase_scratch(HALO + ts, wc),
                 pltpu.VMEM((3, SUBLANES, wc), F32), pltpu.VMEM((HALO, SUBLANES, wc), F32)], vmem_mb=56,
    )(proj, proj, a1, da3, lg, lb)


def _mixer_in_bwd(name, proj, da1, dmixed, dgates, cw, wc, wp, ts):
    _, s_len, npc = proj.shape
    nq = _chips_covering(2 * wc, npc)
    gi = wp // len(POOL_WINDOWS)
    hb = ts // HALO
    n_tiles = s_len // ts
    last_hb = s_len // HALO - 1
    d2 = dgates.shape[1]

    def body(p_ref, d1_ref, d1n_ref, dm_ref, dmn_ref, dgt_ref, cw_ref, o_ref, d1s_ref, es_ref, d1p_ref, ep_ref):
        i = pl.program_id(0)
        more = i < n_tiles - 1
        d1s_ref[0:ts, :] = d1_ref[...]
        d1s_ref[ts:ts + HALO, :] = jnp.where(more, d1n_ref[...], 0.0)
        t_abs = i * ts + lax.broadcasted_iota(jnp.int32, (ts + HALO, 1), 0)
        dm_ext = jnp.concatenate([dm_ref[...], jnp.where(more, dmn_ref[...], 0.0)], axis=0)
        for g, win in enumerate(POOL_WINDOWS):
            cs = slice(g * gi, (g + 1) * gi)
            es_ref[:, cs] = dm_ext[:, cs] / jnp.minimum(t_abs + 1, win).astype(F32)
        _make_phases(d1s_ref, d1p_ref)
        _make_phases(es_ref, ep_ref)

        def chunk(r0):
            rows = pl.ds(r0, ROW_CHUNK)
            da0 = cw_ref[0:1, :] * _window(d1s_ref, d1p_ref, CONV_K - 1, r0)
            for k in range(1, CONV_K):
                da0 = da0 + cw_ref[k:k + 1, :] * _window(d1s_ref, d1p_ref, CONV_K - 1 - k, r0)
            glu_a = _cols(p_ref, 0, wc, npc, rows)
            sig = _sigmoid(_cols(p_ref, wc, 2 * wc, npc, rows))
            _store_cols(o_ref, 0, (da0 * sig).astype(BF16), npc, rows)
            _store_cols(o_ref, wc, (da0 * glu_a * sig * (1.0 - sig)).astype(BF16), npc, rows)
            parts = []
            for g, win in enumerate(POOL_WINDOWS):
                cs = slice(g * gi, (g + 1) * gi)
                acc = _window(es_ref, ep_ref, 0, r0, cs)
                for dlt in range(1, win):
                    acc = acc + _window(es_ref, ep_ref, dlt, r0, cs)
                parts.append(acc - dm_ref[rows, cs])
            _store_cols(o_ref, 2 * wc, jnp.concatenate(parts, axis=-1).astype(BF16), npc, rows)

        _for_chunks(ts, chunk)
        _store_cols(o_ref, 2 * wc + wp, dgt_ref[...], npc)

    nxt = lambda i: (jnp.minimum((i + 1) * hb, last_hb), 0)
    return _pcall(
        body, name=name, out_shape=_sds((N_CHIPS, s_len, npc), BF16), grid=(n_tiles,),
        in_specs=[pl.BlockSpec((nq, ts, npc), lambda i: (0, i, 0)),
                  pl.BlockSpec((ts, wc), lambda i: (i, 0)), pl.BlockSpec((HALO, wc), nxt),
                  pl.BlockSpec((ts, wp), lambda i: (i, 0)), pl.BlockSpec((HALO, wp), nxt),
                  pl.BlockSpec((ts, d2), lambda i: (i, 0)),
                  pl.BlockSpec((HALO, wc), lambda i: (0, 0))],
        out_specs=pl.BlockSpec((N_CHIPS, ts, npc), lambda i: (0, i, 0)),
        scratch=[pltpu.VMEM((ts + HALO, wc), F32), pltpu.VMEM((ts + HALO, wp), F32),
                 _phase_scratch(ts + HALO, wc), _phase_scratch(ts + HALO, wp)], vmem_mb=56,
    )(proj, da1, da1, dmixed, dmixed, dgates, cw)


def _ada_fwd(name, c_all, w, b):
    d, cols = w.shape
    tn = 512 if cols % 512 == 0 else cols

    def body(c_ref, w_ref, b_ref, o_ref):
        cv = c_ref[...]
        sc = (cv * _sigmoid(cv)).astype(BF16)
        o_ref[...] = jnp.dot(sc, w_ref[...].astype(BF16), preferred_element_type=F32) + b_ref[...]

    return _pcall(body, name=name, out_shape=_sds((N_DEV, cols), F32), grid=(cols // tn,),
                  in_specs=[pl.BlockSpec((N_DEV, d), lambda j: (0, 0)), pl.BlockSpec((d, tn), lambda j: (0, j)),
                            pl.BlockSpec((1, tn), lambda j: (0, j))],
                  out_specs=pl.BlockSpec((N_DEV, tn), lambda j: (0, j)), vmem_mb=32)(c_all, w, b)


def _adam_math(w, g, m, v):
    m_new = ADAM_B1 * m + (1.0 - ADAM_B1) * g
    v_new = ADAM_B2 * v + (1.0 - ADAM_B2) * (g * g)
    m_hat = m_new / (1.0 - ADAM_B1 ** ADAM_STEP)
    v_hat = v_new / (1.0 - ADAM_B2 ** ADAM_STEP)
    delta = -ADAM_LR * (m_hat / (jnp.sqrt(v_hat) + ADAM_EPS) + ADAM_WD * w)
    return delta, m_new, v_new


def _adamw(name, w, g, m, v):
    rows, cols = w.shape
    tr = _row_tile(rows, cols, 524288)

    def body(w_ref, g_ref, m_ref, v_ref, go_ref, d_ref, mo_ref, vo_ref):
        g = g_ref[...]
        go_ref[...] = g
        d_ref[...], mo_ref[...], vo_ref[...] = _adam_math(w_ref[...], g, m_ref[...], v_ref[...])

    spec = pl.BlockSpec((tr, cols), lambda i: (i, 0))
    return _pcall(body, name=name, out_shape=[_sds(w.shape, F32)] * 4, grid=(rows // tr,), in_specs=[spec] * 4,
                  out_specs=[spec] * 4, vmem_mb=40)(w, g, m, v)


def _ada_grad_adamw(name, c_t, d_ada, w, m, v):
    rows, cols = w.shape
    tr = _tile(rows, 256)
    tc = _tile(cols, 1536) if cols % 1536 == 0 else cols

    def body(c_ref, da_ref, w_ref, m_ref, v_ref, g_ref, d_ref, mo_ref, vo_ref):
        cv = c_ref[...]
        sc = cv * _sigmoid(cv)
        g = sc[:, 0:1] * da_ref[0:1, :]
        for b in range(1, N_DEV):
            g = g + sc[:, b:b + 1] * da_ref[b:b + 1, :]
        g_ref[...] = g
        d_ref[...], mo_ref[...], vo_ref[...] = _adam_math(w_ref[...], g, m_ref[...], v_ref[...])

    spec = pl.BlockSpec((tr, tc), lambda i, j: (i, j))
    return _pcall(body, name=name, out_shape=[_sds(w.shape, F32)] * 4, grid=(rows // tr, cols // tc),
                  in_specs=[pl.BlockSpec((tr, N_DEV), lambda i, j: (i, 0)),
                            pl.BlockSpec((N_DEV, tc), lambda i, j: (0, j)), spec, spec, spec],
                  out_specs=[spec] * 4, vmem_mb=40)(c_t, d_ada, w, m, v)


def _sum_devices(name, gathered, m_per):
    n = gathered.shape[1]

    def body(g_ref, o_ref):
        acc = g_ref[0:m_per, :]
        for dev in range(1, N_DEV):
            acc = acc + g_ref[dev * m_per:(dev + 1) * m_per, :]
        o_ref[...] = acc

    return _pcall(body, name=name, out_shape=_sds((m_per, n), F32),
                  in_specs=[pl.BlockSpec(memory_space=pltpu.VMEM)],
                  out_specs=pl.BlockSpec(memory_space=pltpu.VMEM))(gathered)


def _ffn_fwd(tag, n, w_in_parts, w_out_after_swiglu, dims):
    s_len, d, f_dim = dims["S"], dims["D"], dims["F"]
    tf = f_dim // 4
    tm0, tm = _tile(s_len, 512), _tile(s_len, 1024)
    n_parts = len(w_in_parts)
    nbp = (f_dim // 2) // tf
    nbq = nbp // n_parts

    def ep(accs, ex, outs):
        hh, uu = accs
        sig = _sigmoid(hh)
        silu = hh * sig
        outs[0][0] = (uu * (sig + silu * (1.0 - sig))).astype(BF16)
        outs[0][1] = silu.astype(BF16)
        outs[1][...] = (silu * uu).astype(BF16)

    done = ()
    for part, get_w in enumerate(w_in_parts):
        w_g = get_w()
        col = lambda j, part=part: (j // nbq) * nbp + part * nbq + j % nbq
        done = _matmul(
            f"{tag}_swiglu{part}", n, [w_g, w_g], mode="nn", grid=(2 * nbq, s_len // tm0, 1),
            a_spec=pl.BlockSpec((tm0, d), lambda j, i, k: (i, 0)),
            b_specs=[pl.BlockSpec((None, d, tf), lambda j, i, k, part=part: (j // nbq, 0, part * nbq + j % nbq)),
                     pl.BlockSpec((None, d, tf), lambda j, i, k, part=part: (2 + j // nbq, 0, part * nbq + j % nbq))],
            out_shape=[_sds((2, s_len, f_dim), BF16), _sds((s_len, f_dim), BF16)],
            out_specs=[pl.BlockSpec((2, tm0, tf), lambda j, i, k, col=col: (0, i, col(j))),
                       pl.BlockSpec((tm0, tf), lambda j, i, k, col=col: (i, col(j)))],
            acc_shape=(tm0, tf), epilogue=ep, carry=done)
    hu, act = done
    w_out2d = w_out_after_swiglu()
    tn2 = _tile(d, 1024)
    f = _matmul(
        f"{tag}_down", act, [w_out2d], mode="nn", grid=(s_len // tm, d // tn2, 2),
        a_spec=pl.BlockSpec((tm, 2 * tf), lambda i, j, k: (i, k)),
        b_specs=[pl.BlockSpec((2 * tf, tn2), lambda i, j, k: (k, j))],
        out_shape=_sds((s_len, d), F32), out_specs=pl.BlockSpec((tm, tn2), lambda i, j, k: (i, j)),
        acc_shape=(tm, tn2), epilogue=_ep_store(F32))
    return hu, act, f, w_out2d


def _ffn_bwd(tag, n, hu, act, df, w_in_g, w_out2d, dims, after_dw_out, after_dw_in):
    s_len, d, f_dim = dims["S"], dims["D"], dims["F"]
    tf = f_dim // 4
    tk = _tile(s_len, 2048)
    tn = _tile(d, 1024)
    g_out = _matmul(
        f"{tag}_dw_out", act, [df], mode="tn", grid=(4, d // tn, s_len // tk),
        a_spec=pl.BlockSpec((tk, tf), lambda i, j, k: (k, i)),
        b_specs=[pl.BlockSpec((tk, tn), lambda i, j, k: (k, j))],
        out_shape=_sds((2, 4, tf // 2, d), F32),
        out_specs=pl.BlockSpec((2, None, tf // 2, tn), lambda i, j, k: (0, i, 0, j)),
        acc_shape=(tf, tn), epilogue=_ep_halves(tf // 2))
    after_dw_out(g_out)

    def ep_dhu(accs, ex, outs):
        da = accs[0]
        outs[0][0] = (da * ex[0][0].astype(F32)).astype(BF16)
        outs[0][1] = (da * ex[0][1].astype(F32)).astype(BF16)

    tm = _tile(s_len, 1024)
    hu_spec = pl.BlockSpec((2, tm, tf), lambda j, i, k: (0, i, j))
    dhu = _matmul(
        f"{tag}_dhu", df, [w_out2d], mode="nt", grid=(4, s_len // tm, 1),
        a_spec=pl.BlockSpec((tm, d), lambda j, i, k: (i, 0)),
        b_specs=[pl.BlockSpec((tf, d), lambda j, i, k: (j, 0))],
        extras=[hu], extra_specs=[hu_spec],
        out_shape=_sds((2, s_len, f_dim), BF16), out_specs=hu_spec, acc_shape=(tm, tf), epilogue=ep_dhu)

    hd = d // 2
    rt = hd // 2
    g_in = _matmul(
        f"{tag}_dw_in", n, [dhu], mode="tn", grid=(8, 4, 1),
        a_spec=pl.BlockSpec((s_len, rt), lambda j, i, k: (0, i)),
        b_specs=[pl.BlockSpec((None, s_len, tf), lambda j, i, k: (j // 4, 0, j % 4))],
        out_shape=_sds((2, 4, hd, f_dim // 2), F32),
        out_specs=pl.BlockSpec((None, None, rt, tf), lambda j, i, k: (i // 2, j // 2, i % 2, j % 2)),
        acc_shape=(rt, tf), epilogue=_ep_store(F32))
    after_dw_in(g_in)

    tm2 = _tile(s_len, 1024)
    dn = _matmul(
        f"{tag}_dn", dhu, [w_in_g], mode="nt", grid=(s_len // tm2, d // tn, N_CHIPS),
        a_spec=pl.BlockSpec((None, tm2, 2 * tf), lambda i, j, k: (k // 2, i, k % 2)),
        b_specs=[pl.BlockSpec((None, tn, 2 * tf), lambda i, j, k: (k, j, 0))],
        out_shape=_sds((s_len, d), BF16), out_specs=pl.BlockSpec((tm2, tn), lambda i, j, k: (i, j)),
        acc_shape=(tm2, tn), epilogue=_ep_store(BF16))
    return dn


def kernel(x, c, w_ada, b_ada, g_ffn1, w1_in, w1_out, g_mix, w_in, conv_w, conv_b, ln_a_g, ln_a_b, w_a_out, b_a_out, w_b_group, b_b_group, ls_b, w_out, g_ffn2, w2_in, w2_out, g_final, loss_target, m_w_ada, m_b_ada, m_g_ffn1, m_w1_in, m_w1_out, m_g_mix, m_w_in, m_conv_w, m_conv_b, m_ln_a_g, m_ln_a_b, m_w_a_out, m_b_a_out, m_w_b_group, m_b_b_group, m_ls_b, m_w_out, m_g_ffn2, m_w2_in, m_w2_out, m_g_final, v_w_ada, v_b_ada, v_g_ffn1, v_w1_in, v_w1_out, v_g_mix, v_w_in, v_conv_w, v_conv_b, v_ln_a_g, v_ln_a_b, v_w_a_out, v_b_a_out, v_w_b_group, v_b_b_group, v_ls_b, v_w_out, v_g_ffn2, v_w2_in, v_w2_out, v_g_final):
    weights = dict(w_ada=w_ada, b_ada=b_ada, g_ffn1=g_ffn1, w1_in=w1_in, w1_out=w1_out, g_mix=g_mix, w_in=w_in,
                   conv_w=conv_w, conv_b=conv_b, ln_a_g=ln_a_g, ln_a_b=ln_a_b, w_a_out=w_a_out, b_a_out=b_a_out,
                   w_b_group=w_b_group, b_b_group=b_b_group, ls_b=ls_b, w_out=w_out, g_ffn2=g_ffn2, w2_in=w2_in,
                   w2_out=w2_out, g_final=g_final)
    mom1 = dict(w_ada=m_w_ada, b_ada=m_b_ada, g_ffn1=m_g_ffn1, w1_in=m_w1_in, w1_out=m_w1_out, g_mix=m_g_mix,
                w_in=m_w_in, conv_w=m_conv_w, conv_b=m_conv_b, ln_a_g=m_ln_a_g, ln_a_b=m_ln_a_b, w_a_out=m_w_a_out,
                b_a_out=m_b_a_out, w_b_group=m_w_b_group, b_b_group=m_b_b_group, ls_b=m_ls_b, w_out=m_w_out,
                g_ffn2=m_g_ffn2, w2_in=m_w2_in, w2_out=m_w2_out, g_final=m_g_final)
    mom2 = dict(w_ada=v_w_ada, b_ada=v_b_ada, g_ffn1=v_g_ffn1, w1_in=v_w1_in, w1_out=v_w1_out, g_mix=v_g_mix,
                w_in=v_w_in, conv_w=v_conv_w, conv_b=v_conv_b, ln_a_g=v_ln_a_g, ln_a_b=v_ln_a_b, w_a_out=v_w_a_out,
                b_a_out=v_b_a_out, w_b_group=v_w_b_group, b_b_group=v_b_b_group, ls_b=v_ls_b, w_out=v_w_out,
                g_ffn2=v_g_ffn2, w2_in=v_w2_in, w2_out=v_w2_out, g_final=v_g_final)
    order = list(weights)

    s_len, d = x.shape[1], x.shape[2]
    f_dim = w1_out.shape[0] * N_CHIPS
    wc = conv_w.shape[1] * N_CHIPS
    wp = w_b_group.shape[0] * w_b_group.shape[1]
    n_groups, gi, goq = w_b_group.shape
    npc = w_in.shape[1]
    ada_c = w_ada.shape[1]
    dims = dict(S=s_len, D=d, F=f_dim)
    ts = _tile(s_len, 256)

    xi, yi, ci = lax.axis_index("x"), lax.axis_index("y"), lax.axis_index("c")
    q = 2 * xi + yi
    dev = 2 * q + ci
    q_idx = jnp.reshape(q, (1,)).astype(jnp.int32)
    qc_idx = jnp.stack([q, ci]).astype(jnp.int32)
    _PREVIOUS.clear()

    cwq = conv_w.shape[1]
    pack0 = jnp.concatenate([c.reshape(-1), conv_w.reshape(-1), b_b_group.reshape(-1)])
    n0 = -(-pack0.shape[0] // (8 * LANES)) * LANES
    pack0 = jnp.pad(pack0, (0, 8 * n0 - pack0.shape[0])).reshape(8, n0)
    g0 = _allgather_small("gather_small_in", pack0).reshape(N_DEV, 8 * n0)
    c_all = g0[:, :d]
    south = g0[0::2]
    cw_full = jnp.concatenate([south[k, d:d + CONV_K * cwq].reshape(CONV_K, cwq) for k in range(N_CHIPS)], axis=1)
    cw_pad = jnp.pad(cw_full, ((0, HALO - CONV_K), (0, 0)))
    o_bb = d + CONV_K * cwq
    bb_full = jnp.concatenate([south[k, o_bb:o_bb + n_groups * goq].reshape(n_groups, goq) for k in range(N_CHIPS)],
                              axis=1).reshape(1, d)

    as2d = lambda a: a.reshape(-1, a.shape[-1])
    groups = dict(w1_out=["w1_out"], mix=["w_a_out", "w_b_group", "w_out"], w2_in=["w2_in"], w2_out=["w2_out"])
    big = ["w1_in", "w1_out", "w_in", "w_a_out", "w_b_group", "w_out", "w2_in", "w2_out"]
    cast = lambda nm: _cast_into_gathered(f"cast_{nm}", as2d(weights[nm]), q_idx)
    w1_in_gather = _TwoPartGather("w1_in", cast("w1_in"))

    b_ada_mine = lax.dynamic_slice(b_ada, (q * ada_c,), (ada_c,)).reshape(1, ada_c)
    ada_piece = _ada_fwd("ada_fwd", c_all, w_ada, b_ada_mine)
    casts = {nm: cast(nm) for nm in big[1:]}
    g1 = _allgather_small("gather_ada", ada_piece).reshape(N_DEV, N_DEV, ada_c)
    w1_in_gather.start_second()
    ici = {}
    for grp, names in groups.items():
        ici[grp] = _gather_ici(f"gather_{grp}_ici", [casts[nm] for nm in names])
        if grp == "w1_out":
            w_in_gather = _TwoPartGather("w_in", casts["w_in"])
            w_in_gather.start_second()
    ada_rows = lax.dynamic_index_in_dim(g1[0::2], dev, axis=1, keepdims=False)
    ada = ada_rows.reshape(3, 3, 1, d)
    (sh1, sc1, gt1), (sh2, sc2, gt2), (sh3, sc3, gt3) = [[ada[i, j] for j in range(3)] for i in range(3)]

    row = lambda vct: vct.reshape(1, -1)
    g1v, gmv, g2v, gfv = row(g_ffn1), row(g_mix), row(g_ffn2), row(g_final)

    def arrived(grp):
        return _gather_d2d(f"gather_{grp}_d2d", ici[grp].wait())

    def gathered(fwd, grp):
        return {nm: g.reshape(N_CHIPS, 2 * g.shape[2], g.shape[3]) for nm, g in zip(groups[grp], fwd.wait())}

    x2 = x[0]
    tgt = loss_target[0]

    n1 = _norm_mod("ffn1_norm", x2, g1v, sc1, sh1, ts)
    fwd, w1_in_parts = {}, []

    def w1_in_part(part):
        def get():
            w1_in_gather.arrive(part)
            w1_in_parts.append(w1_in_gather.ready(part))
            return w1_in_parts[-1]
        return get

    def w1_out_after_swiglu():
        fwd["w1_out"] = arrived("w1_out")
        w_in_gather.arrive(0)
        return gathered(fwd["w1_out"], "w1_out")["w1_out"].reshape(f_dim, d)

    hu1, act1, f1, w1_out_2d = _ffn_fwd("ffn1", n1, [w1_in_part(0), w1_in_part(1)], w1_out_after_swiglu, dims)
    w1_in_g = w1_in_parts[-1]
    h1, n2 = _residual_norm_mod("mix_norm", x2, f1, gt1, 0.5, gmv, sc2, sh2, ts)

    tm = _tile(s_len, 1024)
    tnp = npc // 2
    proj = ()
    for part in range(2):
        if part:
            w_in_gather.arrive(part)
        w_in_g = w_in_gather.ready(part)
        proj = (_matmul(
            f"mix_proj{part}", n2, [w_in_g], mode="nn", grid=(s_len // tm, N_CHIPS, 1),
            a_spec=pl.BlockSpec((tm, d), lambda i, j, k: (i, 0)),
            b_specs=[pl.BlockSpec((None, d, tnp), lambda i, j, k, part=part: (j, 0, part))],
            out_shape=_sds((N_CHIPS, s_len, npc), BF16),
            out_specs=pl.BlockSpec((None, tm, tnp), lambda i, j, k, part=part: (j, i, part)),
            acc_shape=(tm, tnp), epilogue=_ep_store(BF16), carry=proj),)
    proj = proj[0]
    fwd["mix"] = arrived("mix")
    cbv, lgv, lbv = row(conv_b), row(ln_a_g), row(ln_a_b)
    a3, mixed, conv_out = _mixer_mid("mix_mid", proj, cw_pad, cbv, lgv, lbv, wc, wp, ts)
    wts = gathered(fwd["mix"], "mix")
    w_out_2d = wts["w_out"].reshape(d, d)
    w_a_g = wts["w_a_out"]
    w_b_r = _regroup("regroup_w_b", wts["w_b_group"], n_groups)
    dq = d // N_CHIPS
    ya = _matmul(
        "mix_ya", a3, [w_a_g], mode="nn", grid=(s_len // tm, N_CHIPS, 1),
        a_spec=pl.BlockSpec((tm, wc), lambda i, j, k: (i, 0)),
        b_specs=[pl.BlockSpec((None, wc, dq), lambda i, j, k: (j, 0, 0))],
        out_shape=_sds((s_len, d), BF16), out_specs=pl.BlockSpec((tm, dq), lambda i, j, k: (i, j)),
        acc_shape=(tm, dq), epilogue=_ep_store(BF16))
    yb = _matmul(
        "mix_yb", mixed, [w_b_r], mode="nn", grid=(s_len // tm, n_groups, 1),
        a_spec=pl.BlockSpec((tm, gi), lambda i, j, k: (i, j)),
        b_specs=[pl.BlockSpec((None, gi, dq), lambda i, j, k: (j, 0, 0))],
        out_shape=_sds((s_len, d), BF16), out_specs=pl.BlockSpec((tm, dq), lambda i, j, k: (i, j)),
        acc_shape=(tm, dq), epilogue=_ep_store(BF16))
    bav, lsv = row(b_a_out), row(ls_b)
    z = _gates_fwd("mix_gates", proj, ya, yb, bav, bb_full, lsv, wc, wp, ts)
    tn = _tile(d, 1024)
    mix = _matmul(
        "mix_out", z, [w_out_2d], mode="nn", grid=(s_len // tm, d // tn, 1),
        a_spec=pl.BlockSpec((tm, d), lambda i, j, k: (i, 0)),
        b_specs=[pl.BlockSpec((d, tn), lambda i, j, k: (0, j))],
        out_shape=_sds((s_len, d), F32), out_specs=pl.BlockSpec((tm, tn), lambda i, j, k: (i, j)),
        acc_shape=(tm, tn), epilogue=_ep_store(F32))
    fwd["w2_in"] = arrived("w2_in")
    h2, n3 = _residual_norm_mod("ffn2_norm", h1, mix, gt2, 1.0, g2v, sc3, sh3, ts)
    w2_in_g = gathered(fwd["w2_in"], "w2_in")["w2_in"]
    hu2, act2, f3, w2_out_2d = _ffn_fwd(
        "ffn2", n3, [lambda: w2_in_g],
        lambda: gathered(arrived("w2_out"), "w2_out")["w2_out"].reshape(f_dim, d), dims)

    dh3, df3, d_gf, d_gt3, loss_cols = _final_loss("final_loss", h2, f3, tgt, gt3, 0.5, gfv, ts)
    rs, held = {}, {}
    dn3 = _ffn_bwd(
        "ffn2", n3, hu2, act2, df3, w2_in_g, w2_out_2d, dims,
        after_dw_out=lambda g: held.update(w2_out=g),
        after_dw_in=lambda g: rs.update(ffn2=_ReduceScatter("g_ffn2", ["w2_out", "w2_in"], [held["w2_out"], g],
                                                            qc_idx)))
    dh2, dmix, d_sh3, d_sc3, d_g2, d_gt2 = _norm_mod_bwd("ffn2_norm_bwd", h2, dn3, dh3, g2v, sc3, ts,
                                                         prev=(mix, gt2, 1.0))
    rs["ffn2"].step2()

    tk = s_len
    hq = d // (2 * N_CHIPS)
    gw_out = _matmul(
        "mix_dw_out", z, [dmix], mode="tn", grid=(N_CHIPS, d // tn, s_len // tk),
        a_spec=pl.BlockSpec((tk, 2 * hq), lambda i, j, k: (k, i)),
        b_specs=[pl.BlockSpec((tk, tn), lambda i, j, k: (k, j))],
        out_shape=_sds((2, N_CHIPS, hq, d), F32),
        out_specs=pl.BlockSpec((2, None, hq, tn), lambda i, j, k: (0, i, 0, j)),
        acc_shape=(2 * hq, tn), epilogue=_ep_halves(hq))
    dz = _matmul(
        "mix_dz", dmix, [w_out_2d], mode="nt", grid=(s_len // tm, d // tn, 1),
        a_spec=pl.BlockSpec((tm, d), lambda i, j, k: (i, 0)),
        b_specs=[pl.BlockSpec((tn, d), lambda i, j, k: (j, 0))],
        out_shape=_sds((s_len, d), BF16), out_specs=pl.BlockSpec((tm, tn), lambda i, j, k: (i, j)),
        acc_shape=(tm, tn), epilogue=_ep_store(BF16))
    dya, dyb, dgates, d_ba, d_ls, d_bb = _gates_bwd("mix_gates_bwd", proj, dz, ya, yb, bav, bb_full, lsv, wc, wp, ts)
    gw_a = _matmul(
        "mix_dw_a", a3, [dya], mode="tn", grid=(1, N_CHIPS, s_len // tk),
        a_spec=pl.BlockSpec((tk, wc), lambda i, j, k: (k, 0)),
        b_specs=[pl.BlockSpec((tk, dq), lambda i, j, k: (k, j))],
        out_shape=_sds((2, N_CHIPS, wc // 2, dq), F32),
        out_specs=pl.BlockSpec((2, None, wc // 2, dq), lambda i, j, k: (0, j, 0, 0)),
        acc_shape=(wc, dq), epilogue=_ep_halves(wc // 2))
    da3 = _matmul(
        "mix_da3", dya, [w_a_g], mode="nt", grid=(s_len // tm, 1, N_CHIPS),
        a_spec=pl.BlockSpec((tm, dq), lambda i, j, k: (i, k)),
        b_specs=[pl.BlockSpec((None, wc, dq), lambda i, j, k: (k, 0, 0))],
        out_shape=_sds((s_len, wc), F32), out_specs=pl.BlockSpec((tm, wc), lambda i, j, k: (i, 0)),
        acc_shape=(tm, wc), epilogue=_ep_store(F32))
    gpr = n_groups // 2

    def ep_by_chip(accs, ex, outs):
        for k in range(N_CHIPS):
            outs[0][k] = accs[0][:, k * goq:(k + 1) * goq]

    gw_b = _matmul(
        "mix_dw_b", mixed, [dyb], mode="tn", grid=(1, n_groups, s_len // tk),
        a_spec=pl.BlockSpec((tk, gi), lambda i, j, k: (k, j)),
        b_specs=[pl.BlockSpec((tk, dq), lambda i, j, k: (k, j))],
        out_shape=_sds((2, N_CHIPS, gpr * gi, goq), F32),
        out_specs=pl.BlockSpec((None, N_CHIPS, gi, goq), lambda i, j, k: (j // gpr, 0, j % gpr, 0)),
        acc_shape=(gi, dq), epilogue=ep_by_chip)
    dmixed = _matmul(
        "mix_dmixed", dyb, [w_b_r], mode="nt", grid=(s_len // tm, n_groups, 1),
        a_spec=pl.BlockSpec((tm, dq), lambda i, j, k: (i, j)),
        b_specs=[pl.BlockSpec((None, gi, dq), lambda i, j, k: (j, 0, 0))],
        out_shape=_sds((s_len, wp), F32), out_specs=pl.BlockSpec((tm, gi), lambda i, j, k: (i, j)),
        acc_shape=(tm, gi), epilogue=_ep_store(F32))
    da1, d_lg, d_lb, d_cb, d_cw = _conv_branch_bwd("mix_conv_bwd", proj, conv_out, da3, lgv, lbv, wc, wp, ts)
    dproj = _mixer_in_bwd("mix_in_bwd", proj, da1, dmixed, dgates, cw_pad, wc, wp, ts)
    hd = d // 2
    gw_in = _matmul(
        "mix_dw_in", n2, [dproj], mode="tn", grid=(2, 8, s_len // tk),
        a_spec=pl.BlockSpec((tk, hd), lambda i, j, k: (k, i)),
        b_specs=[pl.BlockSpec((None, tk, tnp), lambda i, j, k: (j // 2, k, j % 2))],
        out_shape=_sds((2, N_CHIPS, hd, npc), F32),
        out_specs=pl.BlockSpec((None, None, hd, tnp), lambda i, j, k: (i, j // 2, 0, j % 2)),
        acc_shape=(hd, tnp), epilogue=_ep_store(F32))
    rs["mix"] = _ReduceScatter("g_mix", ["w_in", "w_a_out", "w_b_group", "w_out"], [gw_in, gw_a, gw_b, gw_out],
                               qc_idx)
    rs["ffn2"].step3()
    dn2 = _matmul(
        "mix_dn", dproj, [w_in_g], mode="nt", grid=(s_len // tm, d // tn, N_CHIPS),
        a_spec=pl.BlockSpec((None, tm, npc), lambda i, j, k: (k, i, 0)),
        b_specs=[pl.BlockSpec((None, tn, npc), lambda i, j, k: (k, j, 0))],
        out_shape=_sds((s_len, d), BF16), out_specs=pl.BlockSpec((tm, tn), lambda i, j, k: (i, j)),
        acc_shape=(tm, tn), epilogue=_ep_store(BF16))
    dh1, df1, d_sh2, d_sc2, d_gm, d_gt1 = _norm_mod_bwd("mix_norm_bwd", h1, dn2, dh2, gmv, sc2, ts,
                                                        prev=(f1, gt1, 0.5))
    rs["mix"].step2()

    def w1_in_ready(g):
        rs["w1_in"] = _ReduceScatter("g_w1_in", ["w1_in"], [g], qc_idx)
        rs["w1_out"].step2()
        rs["mix"].step3()

    dn1 = _ffn_bwd(
        "ffn1", n1, hu1, act1, df1, w1_in_g, w1_out_2d, dims,
        after_dw_out=lambda g: rs.update(w1_out=_ReduceScatter("g_w1_out", ["w1_out"], [g], qc_idx)),
        after_dw_in=w1_in_ready)
    grad_x, d_sh1, d_sc1, d_g1 = _norm_mod_bwd("ffn1_norm_bwd", x2, dn1, dh1, g1v, sc1, ts)

    d_ada = jnp.concatenate([d_sh1, d_sc1, d_gt1, d_sh2, d_sc2, d_gt2, d_sh3, d_sc3, d_gt3], axis=1)
    small = [d_ada, d_g1, d_gm, d_cw[:CONV_K].reshape(1, -1), d_cb, d_lg, d_lb, d_ba, d_bb, d_ls, d_g2, d_gf,
             loss_cols]
    sizes = [a.shape[1] for a in small]
    pack1 = jnp.concatenate(small, axis=1).reshape(-1)
    n1p = -(-pack1.shape[0] // (8 * LANES)) * LANES
    pack1 = jnp.pad(pack1, (0, 8 * n1p - pack1.shape[0])).reshape(8, n1p)
    g2 = _allgather_small("gather_small_grads", pack1)
    rs["w1_in"].step2()
    total = _sum_devices("sum_small_grads", g2, 8).reshape(-1)
    offs = [0]
    for sz in sizes:
        offs.append(offs[-1] + sz)
    tot = [total[offs[k]:offs[k + 1]] for k in range(len(sizes))]
    d_ada_all = g2.reshape(N_DEV, 8 * n1p)[:, :sizes[0]]
    loss = jnp.sum(tot[12])

    grads = {}
    grads["b_ada"] = tot[0]
    grads["g_ffn1"], grads["g_mix"] = tot[1], tot[2]
    grads["conv_w"] = lax.dynamic_slice(tot[3].reshape(CONV_K, wc), (0, q * cwq), (CONV_K, cwq))
    grads["conv_b"], grads["ln_a_g"], grads["ln_a_b"], grads["b_a_out"] = tot[4], tot[5], tot[6], tot[7]
    grads["b_b_group"] = lax.dynamic_slice(tot[8].reshape(n_groups, N_CHIPS * goq), (0, q * goq), (n_groups, goq))
    grads["ls_b"], grads["g_ffn2"], grads["g_final"] = tot[9], tot[10], tot[11]

    delta, new_m, new_v = {}, {}, {}

    def adamw_group(reduced):
        for nm, g in reduced.items():
            shp = weights[nm].shape
            go, dl, mo, vo = _adamw(f"adamw_{nm}", as2d(weights[nm]), g, as2d(mom1[nm]), as2d(mom2[nm]))
            grads[nm], delta[nm], new_m[nm], new_v[nm] = go.reshape(shp), dl.reshape(shp), mo.reshape(shp), vo.reshape(shp)

    adamw_group(rs["ffn2"].result())
    rs["w1_out"].step3()
    adamw_group(rs["mix"].result())
    d_ada_mine = lax.dynamic_slice(d_ada_all, (0, q * ada_c), (N_DEV, ada_c))
    grads["w_ada"], delta["w_ada"], new_m["w_ada"], new_v["w_ada"] = _ada_grad_adamw(
        "adamw_w_ada", c_all.T, d_ada_mine, w_ada, m_w_ada, v_w_ada)
    rs["w1_in"].step3()
    smalls = [nm for nm in order if nm not in big and nm != "w_ada"]
    flat = lambda src: jnp.concatenate([src[nm].reshape(-1) for nm in smalls])
    n_small = sum(weights[nm].size for nm in smalls)
    rows_s = -(-n_small // (8 * LANES)) * 8
    packed = [jnp.pad(flat(src), (0, rows_s * LANES - n_small)).reshape(rows_s, LANES)
              for src in (weights, grads, mom1, mom2)]
    _, dl_s, mo_s, vo_s = _adamw("adamw_small", *packed)
    off = 0
    for nm in smalls:
        sz, shp = weights[nm].size, weights[nm].shape
        delta[nm] = dl_s.reshape(-1)[off:off + sz].reshape(shp)
        new_m[nm] = mo_s.reshape(-1)[off:off + sz].reshape(shp)
        new_v[nm] = vo_s.reshape(-1)[off:off + sz].reshape(shp)
        grads[nm] = grads[nm].reshape(shp)
        off += sz
    adamw_group(rs["w1_out"].result())
    adamw_group(rs["w1_in"].result())

    return (loss, grad_x[None], *[grads[nm] for nm in order], *[delta[nm] for nm in order],
            *[new_m[nm] for nm in order], *[new_v[nm] for nm in order])
```

```python
import jax
import jax.numpy as jnp
from jax import lax
from jax.experimental import pallas as pl
from jax.experimental.pallas import tpu as pltpu

F32 = jnp.float32
BF16 = jnp.bfloat16
MESH = pl.DeviceIdType.MESH
ANY = pl.BlockSpec(memory_space=pl.ANY)
HBM = pl.BlockSpec(memory_space=pltpu.HBM)
SEM = pl.BlockSpec(memory_space=pltpu.SEMAPHORE)
EFFECT = pltpu.SideEffectType.DATAFLOW_SIDE_EFFECTING

EPS = 1e-6
CONV_K = 31
HALO = 32
POOL_WINDOWS = (2, 4, 8, 16)
N_CHIPS = 4
N_DEV = 8
LANES = 128

ADAM_LR = 0.001
ADAM_B1 = 0.9
ADAM_B2 = 0.999
ADAM_EPS = 1e-08
ADAM_WD = 0.01
ADAM_STEP = 10

DN = {
    "nn": (((1,), (0,)), ((), ())),
    "nt": (((1,), (1,)), ((), ())),
    "tn": (((0,), (0,)), ((), ())),
}


_PREVIOUS = []


def _ordered(call, args, n_lead, body, token=None, sources=()):
    dep = [pltpu.with_memory_space_constraint(p, pltpu.HBM) if p.size * p.dtype.itemsize >= (1 << 20) else p
           for p in _PREVIOUS if all(p is not a for a in (*args, *sources))]

    def wrapped(*refs):
        return body(*refs[:n_lead], *refs[n_lead + len(dep):])

    outs = call(wrapped, [ANY] * len(dep))(*args, *dep)
    seq = outs if isinstance(outs, (list, tuple)) else [outs]
    _PREVIOUS[:] = [seq[token] if token is not None else
                    next(o for o in seq if jnp.issubdtype(o.dtype, jnp.floating))]
    return outs


def _pcall(body, *, name, out_shape, grid=None, in_specs=None, out_specs=None, scratch=(), aliases=None,
           prefetch=0, vmem_mb=None):
    params = {}
    if grid is not None:
        params["dimension_semantics"] = ("arbitrary",) * len(grid)
    if vmem_mb is not None:
        params["vmem_limit_bytes"] = vmem_mb << 20
    def in_hbm(shape, spec):
        big = shape.size * jnp.dtype(shape.dtype).itemsize >= (1 << 20)
        return pltpu.HBM(shape.shape, shape.dtype) if big and getattr(spec, "memory_space", None) != pltpu.VMEM else shape

    if isinstance(out_shape, (list, tuple)):
        out_shape = [in_hbm(s, sp) for s, sp in zip(out_shape, out_specs)]
    else:
        out_shape = in_hbm(out_shape, out_specs)
    kw = dict(name=name, out_shape=out_shape, compiler_params=pltpu.CompilerParams(**params))
    if aliases:
        kw["input_output_aliases"] = aliases

    def call(wrapped, dep_specs):
        specs = list(in_specs) + dep_specs
        if prefetch:
            return pl.pallas_call(wrapped, grid_spec=pltpu.PrefetchScalarGridSpec(
                num_scalar_prefetch=prefetch, grid=grid, in_specs=specs, out_specs=out_specs,
                scratch_shapes=list(scratch)), **kw)
        if grid is not None:
            return pl.pallas_call(wrapped, grid=grid, in_specs=specs, out_specs=out_specs,
                                  scratch_shapes=list(scratch), **kw)
        return pl.pallas_call(wrapped, in_specs=specs, out_specs=out_specs, scratch_shapes=list(scratch), **kw)

    def run(*args):
        specs = [None] * prefetch + list(in_specs)
        placed = [pltpu.with_memory_space_constraint(a, pltpu.HBM)
                  if a.size * a.dtype.itemsize >= (1 << 20) and getattr(s, "memory_space", None) != pltpu.VMEM else a
                  for a, s in zip(args, specs)]
        return _ordered(call, placed, prefetch + len(in_specs), body, sources=args)

    return run


def _tile(dim, pref):
    t = min(dim, pref)
    assert dim % t == 0, (dim, pref)
    return t


def _sds(shape, dtype):
    return jax.ShapeDtypeStruct(tuple(shape), dtype)


def _sigmoid(v):
    return 0.5 * jnp.tanh(0.5 * v) + 0.5


def _vec(w):
    return pl.BlockSpec((1, w), lambda *_: (0, 0))


def _acc_rows(ref, val, i):
    @pl.when(i == 0)
    def _():
        ref[...] = jnp.zeros_like(ref)

    ref[...] += jnp.sum(val, axis=0, keepdims=True)


def _matmul(name, a, bs, *, mode, grid, a_spec, b_specs, out_shape, out_specs, acc_shape, epilogue,
            extras=(), extra_specs=(), vmem_mb=56, carry=()):
    nb, ne, nk, nc = len(bs), len(extras), grid[2], len(carry)
    dn = DN[mode]

    def body(*all_refs):
        refs = all_refs[:1 + nb + ne] + all_refs[1 + nb + ne + nc:]
        a_ref, b_refs, ex = refs[0], refs[1:1 + nb], refs[1 + nb:1 + nb + ne]
        if nk == 1:
            outs = refs[1 + nb + ne:]
            accs = [lax.dot_general(a_ref[...], b[...], dn, preferred_element_type=F32) for b in b_refs]
            epilogue(accs, ex, outs)
            return
        outs, acc_refs = refs[1 + nb + ne:-nb], refs[-nb:]
        k = pl.program_id(2)

        @pl.when(k == 0)
        def _():
            for acc in acc_refs:
                acc[...] = jnp.zeros_like(acc)

        for acc, b in zip(acc_refs, b_refs):
            acc[...] += lax.dot_general(a_ref[...], b[...], dn, preferred_element_type=F32)

        @pl.when(k == nk - 1)
        def _():
            epilogue([acc[...] for acc in acc_refs], ex, outs)

    scratch = [pltpu.VMEM(acc_shape, F32) for _ in range(nb)] if nk > 1 else []
    return _pcall(body, name=name, out_shape=out_shape, grid=grid,
                  in_specs=[a_spec, *b_specs, *extra_specs, *[ANY] * nc], out_specs=out_specs, scratch=scratch,
                  aliases={1 + nb + ne + i: i for i in range(nc)}, vmem_mb=vmem_mb)(a, *bs, *extras, *carry)


def _ep_store(dtype):
    def ep(accs, ex, outs):
        outs[0][...] = accs[0].astype(dtype)
    return ep


def _ep_halves(h):
    def ep(accs, ex, outs):
        outs[0][0] = accs[0][:h]
        outs[0][1] = accs[0][h:]
    return ep


def _place():
    x, y, c = lax.axis_index("x"), lax.axis_index("y"), lax.axis_index("c")
    chips = [(1 - x, y), (x, 1 - y), (1 - x, 1 - y)]
    return x, y, c, chips


def _allgather_small(name, block):
    m_per, n = block.shape

    def body(x_ref, out_ref, send_sems, recv_sems, local_sem):
        x, y, c, chips = _place()
        me, sibling = (x, y, c), (x, y, 1 - c)

        def rows(px, py, pc):
            return out_ref.at[pl.ds((4 * px + 2 * py + pc) * m_per, m_per), :]

        def copy(k, blk, to, src=None):
            return pltpu.make_async_remote_copy(
                src_ref=rows(*blk) if src is None else src, dst_ref=rows(*blk),
                send_sem=send_sems.at[k], recv_sem=recv_sems.at[k], device_id=to, device_id_type=MESH)

        mine = pltpu.make_async_copy(x_ref, rows(*me), local_sem)
        mine.start()
        first = [copy(0, me, sibling, src=x_ref)]
        first += [copy(1 + j, me, (*chip, c), src=x_ref) for j, chip in enumerate(chips)]
        for cp in first:
            cp.start()
        passed = [copy(4 + j, (*chip, c), sibling) for j, chip in enumerate(chips)]
        for j, chip in enumerate(chips):
            copy(1 + j, (*chip, c), me).wait_recv()
            passed[j].start()
        copy(0, sibling, me).wait_recv()
        for j, chip in enumerate(chips):
            copy(4 + j, (*chip, 1 - c), me).wait_recv()
        for cp in first + passed:
            cp.wait_send()
        mine.wait()

    return _pcall(
        body, name=name, out_shape=_sds((N_DEV * m_per, n), block.dtype),
        in_specs=[pl.BlockSpec(memory_space=pltpu.VMEM)], out_specs=pl.BlockSpec(memory_space=pltpu.VMEM),
        scratch=[pltpu.SemaphoreType.DMA((7,)), pltpu.SemaphoreType.DMA((7,)), pltpu.SemaphoreType.DMA],
    )(block)


class _SplitCopies:
    def __init__(self, name, arrays, plan, n_copies):
        self.name, self.plan, self.n = name, plan, len(arrays)
        n = self.n

        def body(*refs):
            send, recv, token = refs[n], refs[n + 1], refs[-1]
            for k, (src, dst, _, peer) in enumerate(plan(refs[:n])):
                pltpu.make_async_remote_copy(src_ref=src, dst_ref=dst, send_sem=send.at[k], recv_sem=recv.at[k],
                                             device_id=peer, device_id_type=MESH).start()
            token[...] = jnp.zeros_like(token)

        def call(wrapped, dep_specs):
            return pl.pallas_call(
                wrapped, name=f"{name}_start",
                out_shape=(pltpu.SemaphoreType.DMA((n_copies,)), pltpu.SemaphoreType.DMA((n_copies,)),
                           *[pltpu.HBM(a.shape, a.dtype) for a in arrays], _sds((8, LANES), F32)),
                in_specs=[HBM] * n + dep_specs,
                out_specs=(SEM, SEM, *[HBM] * n, pl.BlockSpec(memory_space=pltpu.VMEM)),
                input_output_aliases={i: 2 + i for i in range(n)},
                compiler_params=pltpu.CompilerParams(has_side_effects=EFFECT))

        outs = _ordered(call, [pltpu.with_memory_space_constraint(a, pltpu.HBM) for a in arrays], n, body, token=-1,
                        sources=arrays)
        self.send, self.recv, self.arrays = outs[0], outs[1], list(outs[2:2 + n])

    def wait(self, arrays=None):
        n, plan = self.n, self.plan
        if arrays is not None:
            self.arrays = list(arrays)

        def body(*refs):
            send, recv, token = refs[n], refs[n + 1], refs[-1]
            for k, (src, _, landing, peer) in enumerate(plan(refs[:n])):
                cp = pltpu.make_async_remote_copy(src_ref=src, dst_ref=landing, send_sem=send.at[k],
                                                  recv_sem=recv.at[k], device_id=peer, device_id_type=MESH)
                cp.wait_send()
                cp.wait_recv()
            token[...] = jnp.zeros_like(token)

        def call(wrapped, dep_specs):
            return pl.pallas_call(
                wrapped, name=f"{self.name}_wait",
                out_shape=(*[pltpu.HBM(a.shape, a.dtype) for a in self.arrays], _sds((8, LANES), F32)),
                in_specs=[HBM] * n + [SEM, SEM] + dep_specs,
                out_specs=(*[HBM] * n, pl.BlockSpec(memory_space=pltpu.VMEM)),
                input_output_aliases={i: i for i in range(n)},
                compiler_params=pltpu.CompilerParams(has_side_effects=EFFECT))

        return list(_ordered(call, [*self.arrays, self.send, self.recv], n + 2, body, token=-1))[:n]


def _col_range(g, part, n_parts):
    width = g.shape[-1] // n_parts
    return (slice(None), pl.ds(part * width, width))


def _gather_ici(name, gathered, part=0, n_parts=1):
    def plan(refs):
        x, y, c, chips = _place()
        q = 2 * x + y
        return [(g.at[(q, c, *_col_range(g, part, n_parts))], g.at[(q, c, *_col_range(g, part, n_parts))],
                 g.at[(2 * px + py, c, *_col_range(g, part, n_parts))], (px, py, c))
                for g in refs for px, py in chips]

    return _SplitCopies(name, gathered, plan, 3 * len(gathered))


def _gather_d2d(name, gathered, part=0, n_parts=1):
    def plan(refs):
        x, y, c, chips = _place()
        return [(g.at[(2 * px + py, c, *_col_range(g, part, n_parts))],
                 g.at[(2 * px + py, c, *_col_range(g, part, n_parts))],
                 g.at[(2 * px + py, 1 - c, *_col_range(g, part, n_parts))], (x, y, 1 - c))
                for g in refs for px, py in chips]

    return _SplitCopies(name, gathered, plan, 3 * len(gathered))


class _TwoPartGather:
    def __init__(self, name, gathered):
        self.name, self.d2d = name, {}
        self.ici = [_gather_ici(f"gather_{name}_a_ici", [gathered], 0, 2)]
        self.buf = self.ici[0].arrays

    def start_second(self):
        self.ici.append(_gather_ici(f"gather_{self.name}_b_ici", self.buf, 1, 2))
        self.buf = self.ici[1].arrays

    def arrive(self, part):
        here = self.ici[part].wait(self.buf)
        self.d2d[part] = _gather_d2d(f"gather_{self.name}_{'ab'[part]}_d2d", here, part, 2)
        self.buf = self.d2d[part].arrays

    def ready(self, part):
        self.buf = self.d2d[part].wait(self.buf)
        g = self.buf[0]
        return g.reshape(N_CHIPS, 2 * g.shape[2], g.shape[3])


def _scatter_sibling(name, grads):
    n = len(grads)

    def plan(refs):
        x, y, c, _ = _place()
        return [(refs[w].at[1 - c], refs[n + w], refs[n + w], (x, y, 1 - c)) for w in range(n)]

    landing = [lax.empty(g.shape[1:], g.dtype) for g in grads]
    return _SplitCopies(name, [*grads, *landing], plan, n)


def _scatter_chips(name, sums):
    n = len(sums)

    def plan(refs):
        x, y, c, chips = _place()
        return [(refs[w].at[2 * px + py], refs[n + w].at[j], refs[n + w].at[j], (px, py, c))
                for w in range(n) for j, (px, py) in enumerate(chips)]

    landing = [lax.empty((3, *s.shape[1:]), s.dtype) for s in sums]
    return _SplitCopies(name, [*sums, *landing], plan, 3 * n)


def _share_final(name, finals):
    def plan(refs):
        x, y, c, _ = _place()
        return [(f.at[c], f.at[c], f.at[1 - c], (x, y, 1 - c)) for f in refs]

    return _SplitCopies(name, finals, plan, len(finals))


def _row_tile(rows, cols, budget_elems=786432):
    best = 8
    for t in range(8, rows + 1, 8):
        if rows % t == 0 and t * cols <= budget_elems:
            best = t
    return best if rows % best == 0 else rows


def _sum_with_sibling(name, grad, recv, qc_idx):
    _, _, h, cols = grad.shape
    tr = _row_tile(h, cols)

    def body(s_ref, g_ref, r_ref, own_ref, pb_ref):
        p = g_ref[...] + r_ref[...]
        pb_ref[...] = p.astype(BF16)

        @pl.when(pl.program_id(1) == s_ref[0])
        def _():
            own_ref[...] = p

    blk = pl.BlockSpec((None, tr, cols), lambda r, k, s: (k, r, 0))
    return _pcall(
        body, name=name, out_shape=[_sds((h, cols), F32), _sds((N_CHIPS, h, cols), BF16)],
        grid=(h // tr, N_CHIPS), prefetch=1,
        in_specs=[pl.BlockSpec((None, None, tr, cols), lambda r, k, s: (s[1], k, r, 0)), blk],
        out_specs=[pl.BlockSpec((tr, cols), lambda r, k, s: (r, 0)), blk], vmem_mb=32,
    )(qc_idx, grad, recv)


def _sum_chips(name, own, recv, qc_idx):
    h, cols = own.shape
    tr = _row_tile(h, cols)

    def body(s_ref, p_ref, t_ref, o_ref):
        o_ref[...] = ((p_ref[...] + t_ref[0].astype(F32)) + t_ref[1].astype(F32)) + t_ref[2].astype(F32)

    return _pcall(
        body, name=name, out_shape=_sds((2, h, cols), F32), grid=(h // tr,), prefetch=1,
        in_specs=[pl.BlockSpec((tr, cols), lambda r, s: (r, 0)),
                  pl.BlockSpec((3, tr, cols), lambda r, s: (0, r, 0))],
        out_specs=pl.BlockSpec((None, tr, cols), lambda r, s: (s[1], r, 0)), vmem_mb=32,
    )(qc_idx, own, recv)


class _ReduceScatter:
    def __init__(self, tag, names, grads, qc_idx):
        self.tag, self.names, self.n, self.qc_idx = tag, names, len(grads), qc_idx
        self.copies = _scatter_sibling(f"{tag}_rs_sibling", grads)

    def step2(self):
        n = self.n
        arrs = self.copies.wait()
        sums = [_sum_with_sibling(f"{nm}_sum_sibling", arrs[w], arrs[n + w], self.qc_idx)
                for w, nm in enumerate(self.names)]
        self.own = [s[0] for s in sums]
        self.copies = _scatter_chips(f"{self.tag}_rs_chips", [s[1] for s in sums])

    def step3(self):
        n = self.n
        arrs = self.copies.wait()
        finals = [_sum_chips(f"{nm}_sum_chips", self.own[w], arrs[n + w], self.qc_idx)
                  for w, nm in enumerate(self.names)]
        self.copies = _share_final(f"{self.tag}_rs_final", finals)

    def result(self):
        return {nm: f.reshape(2 * f.shape[1], f.shape[2]) for nm, f in zip(self.names, self.copies.wait())}


def _cast_into_gathered(name, w, q_idx):
    rows, cols = w.shape
    h = rows // 2
    tr = _row_tile(h, cols, 1 << 20)
    nr = h // tr

    def body(s_ref, w_ref, o_ref):
        o_ref[...] = w_ref[...].astype(BF16)

    return _pcall(body, name=name, out_shape=_sds((N_CHIPS, 2, h, cols), BF16), grid=(2, nr), prefetch=1,
                  in_specs=[pl.BlockSpec((tr, cols), lambda hf, r, s: (hf * nr + r, 0))],
                  out_specs=pl.BlockSpec((None, None, tr, cols), lambda hf, r, s: (s[0], hf, r, 0)),
                  vmem_mb=32)(q_idx, w)


def _regroup(name, w, n_groups):
    n_chips, rows, goq = w.shape
    gi = rows // n_groups

    def body(w_ref, o_ref):
        o_ref[...] = w_ref[...]

    return _pcall(body, name=name, out_shape=_sds((n_groups, gi, n_chips * goq), w.dtype), grid=(n_groups, n_chips),
                  in_specs=[pl.BlockSpec((None, gi, goq), lambda g, k: (k, g, 0))],
                  out_specs=pl.BlockSpec((None, gi, goq), lambda g, k: (g, 0, k)), vmem_mb=32)(w)


def _rms(h):
    r = lax.rsqrt(jnp.mean(h * h, axis=-1, keepdims=True) + EPS)
    return r, h * r


def _norm_mod(name, h, g, sc, sh, ts):
    s_len, d = h.shape

    def body(h_ref, g_ref, sc_ref, sh_ref, n_ref):
        _, xhat = _rms(h_ref[...])
        n_ref[...] = ((xhat * g_ref[...]) * (1.0 + sc_ref[...]) + sh_ref[...]).astype(BF16)

    row = pl.BlockSpec((ts, d), lambda i: (i, 0))
    return _pcall(body, name=name, out_shape=_sds((s_len, d), BF16), grid=(s_len // ts,),
                  in_specs=[row, _vec(d), _vec(d), _vec(d)], out_specs=row, vmem_mb=32)(h, g, sc, sh)


def _residual_norm_mod(name, h, f, gate, cmul, g, sc, sh, ts):
    s_len, d = h.shape

    def body(h_ref, f_ref, gt_ref, g_ref, sc_ref, sh_ref, ho_ref, n_ref):
        hn = h_ref[...] + (cmul * gt_ref[...]) * f_ref[...]
        ho_ref[...] = hn
        _, xhat = _rms(hn)
        n_ref[...] = ((xhat * g_ref[...]) * (1.0 + sc_ref[...]) + sh_ref[...]).astype(BF16)

    row = pl.BlockSpec((ts, d), lambda i: (i, 0))
    return _pcall(body, name=name, out_shape=[_sds((s_len, d), F32), _sds((s_len, d), BF16)],
                  grid=(s_len // ts,), in_specs=[row, row, _vec(d), _vec(d), _vec(d), _vec(d)],
                  out_specs=[row, row], vmem_mb=32)(h, f, gate, g, sc, sh)


def _final_loss(name, h, f, tgt, gate, cmul, g, ts):
    s_len, d = h.shape

    def body(h_ref, f_ref, t_ref, gt_ref, g_ref, dh_ref, df_ref, dg_ref, dgt_ref, loss_ref):
        i = pl.program_id(0)
        fv = f_ref[...]
        coef = cmul * gt_ref[...]
        hn = h_ref[...] + coef * fv
        r, xhat = _rms(hn)
        err = xhat * g_ref[...] - t_ref[...]
        _acc_rows(loss_ref, (0.5 / d) * (err * err), i)
        dy = err * (1.0 / d)
        _acc_rows(dg_ref, dy * xhat, i)
        dxhat = dy * g_ref[...]
        dh = r * (dxhat - xhat * jnp.mean(dxhat * xhat, axis=-1, keepdims=True))
        dh_ref[...] = dh
        _acc_rows(dgt_ref, cmul * (dh * fv), i)
        df_ref[...] = (coef * dh).astype(BF16)

    row = pl.BlockSpec((ts, d), lambda i: (i, 0))
    return _pcall(body, name=name,
                  out_shape=[_sds((s_len, d), F32), _sds((s_len, d), BF16)] + [_sds((1, d), F32)] * 3,
                  grid=(s_len // ts,), in_specs=[row, row, row, _vec(d), _vec(d)],
                  out_specs=[row, row, _vec(d), _vec(d), _vec(d)], vmem_mb=40)(h, f, tgt, gate, g)


def _norm_mod_bwd(name, h, dn, dh_next, g, sc, ts, prev=None):
    s_len, d = h.shape
    has_prev = prev is not None
    cmul = prev[2] if has_prev else None

    def body(*refs):
        if has_prev:
            h_ref, dn_ref, dhn_ref, f_ref, g_ref, sc_ref, gt_ref, dh_ref, df_ref, dsh_ref, dsc_ref, dg_ref, dgt_ref = refs
        else:
            h_ref, dn_ref, dhn_ref, g_ref, sc_ref, dh_ref, dsh_ref, dsc_ref, dg_ref = refs
        i = pl.program_id(0)
        r, xhat = _rms(h_ref[...])
        dn_v = dn_ref[...].astype(F32)
        gv = g_ref[...]
        _acc_rows(dsh_ref, dn_v, i)
        _acc_rows(dsc_ref, dn_v * (xhat * gv), i)
        dnrm = dn_v * (1.0 + sc_ref[...])
        _acc_rows(dg_ref, dnrm * xhat, i)
        dxhat = dnrm * gv
        dh = dhn_ref[...] + r * (dxhat - xhat * jnp.mean(dxhat * xhat, axis=-1, keepdims=True))
        dh_ref[...] = dh
        if has_prev:
            _acc_rows(dgt_ref, cmul * (dh * f_ref[...]), i)
            df_ref[...] = ((cmul * gt_ref[...]) * dh).astype(BF16)

    row = pl.BlockSpec((ts, d), lambda i: (i, 0))
    if has_prev:
        ins, in_specs = [h, dn, dh_next, prev[0], g, sc, prev[1]], [row, row, row, row, _vec(d), _vec(d), _vec(d)]
        out_shape = [_sds((s_len, d), F32), _sds((s_len, d), BF16)] + [_sds((1, d), F32)] * 4
        out_specs = [row, row] + [_vec(d)] * 4
    else:
        ins, in_specs = [h, dn, dh_next, g, sc], [row, row, row, _vec(d), _vec(d)]
        out_shape = [_sds((s_len, d), F32)] + [_sds((1, d), F32)] * 3
        out_specs = [row] + [_vec(d)] * 3
    return _pcall(body, name=name, out_shape=out_shape, grid=(s_len // ts,), in_specs=in_specs,
                  out_specs=out_specs, vmem_mb=40)(*ins)


def _cols(ref, lo, hi, npc, rows=slice(None)):
    parts = []
    while lo < hi:
        q, o = divmod(lo, npc)
        n = min(hi - lo, npc - o)
        parts.append(ref[q, rows, o:o + n].astype(F32))
        lo += n
    return parts[0] if len(parts) == 1 else jnp.concatenate(parts, axis=-1)


def _store_cols(ref, lo, val, npc, rows=slice(None)):
    off, width = 0, val.shape[-1]
    while off < width:
        q, o = divmod(lo + off, npc)
        n = min(width - off, npc - o)
        ref[q, rows, o:o + n] = val[:, off:off + n]
        off += n


def _chips_covering(cols, npc):
    return -(-cols // npc)


SUBLANES = 8
ROW_CHUNK = 32


def _make_phases(src_ref, ph_ref):
    rows = src_ref.shape[0] - SUBLANES
    for b in range(1, SUBLANES):
        ph_ref[b - 1] = src_ref[pl.ds(b, rows), :]


def _window(src_ref, ph_ref, off, r0, cols=slice(None)):
    a, b = divmod(off, SUBLANES)
    start = pl.multiple_of(r0 + SUBLANES * a, SUBLANES)
    if b == 0:
        return src_ref[pl.ds(start, ROW_CHUNK), cols]
    return ph_ref[b - 1, pl.ds(start, ROW_CHUNK), cols]


def _phase_scratch(rows, width):
    return pltpu.VMEM((SUBLANES - 1, rows - SUBLANES, width), F32)


def _conv(a0s_ref, a0p_ref, cw_ref, cb_ref, r0):
    a1 = cb_ref[...] + cw_ref[0:1, :] * _window(a0s_ref, a0p_ref, HALO - CONV_K + 1, r0)
    for k in range(1, CONV_K):
        a1 = a1 + cw_ref[k:k + 1, :] * _window(a0s_ref, a0p_ref, HALO - CONV_K + 1 + k, r0)
    return a1


def _layer_norm(a1, lg_ref, lb_ref):
    mu = jnp.mean(a1, axis=-1, keepdims=True)
    ctr = a1 - mu
    rstd = lax.rsqrt(jnp.mean(ctr * ctr, axis=-1, keepdims=True) + EPS)
    xh = ctr * rstd
    return xh, rstd, xh * lg_ref[...] + lb_ref[...]


def _for_chunks(ts, fn):
    def step(ci, carry):
        fn(pl.multiple_of(ci * ROW_CHUNK, ROW_CHUNK))
        return carry

    lax.fori_loop(0, ts // ROW_CHUNK, step, 0)


def _stage_glu(p_ref, ph_ref, a0s_ref, i, wc, npc, ts):
    a0 = _cols(p_ref, 0, wc, npc) * _sigmoid(_cols(p_ref, wc, 2 * wc, npc))
    a0h = _cols(ph_ref, 0, wc, npc) * _sigmoid(_cols(ph_ref, wc, 2 * wc, npc))
    a0s_ref[0:HALO, :] = jnp.where(i > 0, a0h, 0.0)
    a0s_ref[HALO:HALO + ts, :] = a0


def _mixer_mid(name, proj, cw, cb, lg, lb, wc, wp, ts):
    _, s_len, npc = proj.shape
    nq = _chips_covering(2 * wc + wp, npc)
    gi = wp // len(POOL_WINDOWS)
    hb = ts // HALO

    def body(p_ref, ph_ref, cw_ref, cb_ref, lg_ref, lb_ref, a3_ref, mx_ref, a1_ref, a0s_ref, vs_ref, a0p_ref,
             vp_ref):
        i = pl.program_id(0)
        _stage_glu(p_ref, ph_ref, a0s_ref, i, wc, npc, ts)
        vs_ref[0:HALO, :] = jnp.where(i > 0, _cols(ph_ref, 2 * wc, 2 * wc + wp, npc), 0.0)
        vs_ref[HALO:HALO + ts, :] = _cols(p_ref, 2 * wc, 2 * wc + wp, npc)
        _make_phases(a0s_ref, a0p_ref)
        _make_phases(vs_ref, vp_ref)

        def chunk(r0):
            rows = pl.ds(r0, ROW_CHUNK)
            a1 = _conv(a0s_ref, a0p_ref, cw_ref, cb_ref, r0)
            a1_ref[rows, :] = a1
            _, _, a2 = _layer_norm(a1, lg_ref, lb_ref)
            a3_ref[rows, :] = (a2 * _sigmoid(a2)).astype(BF16)
            t_abs = i * ts + r0 + lax.broadcasted_iota(jnp.int32, (ROW_CHUNK, 1), 0)
            for g, win in enumerate(POOL_WINDOWS):
                cs = slice(g * gi, (g + 1) * gi)
                v_now = _window(vs_ref, vp_ref, HALO, r0, cs)
                acc = v_now
                for dlt in range(1, win):
                    acc = acc + _window(vs_ref, vp_ref, HALO - dlt, r0, cs)
                cnt = jnp.minimum(t_abs + 1, win).astype(F32)
                mx_ref[rows, cs] = (acc / cnt - v_now).astype(BF16)

        _for_chunks(ts, chunk)

    return _pcall(
        body, name=name, out_shape=[_sds((s_len, wc), BF16), _sds((s_len, wp), BF16), _sds((s_len, wc), F32)],
        grid=(s_len // ts,),
        in_specs=[pl.BlockSpec((nq, ts, npc), lambda i: (0, i, 0)),
                  pl.BlockSpec((nq, HALO, npc), lambda i: (0, jnp.maximum(i * hb - 1, 0), 0)),
                  pl.BlockSpec((HALO, wc), lambda i: (0, 0)), _vec(wc), _vec(wc), _vec(wc)],
        out_specs=[pl.BlockSpec((ts, wc), lambda i: (i, 0)), pl.BlockSpec((ts, wp), lambda i: (i, 0)),
                   pl.BlockSpec((ts, wc), lambda i: (i, 0))],
        scratch=[pltpu.VMEM((HALO + ts, wc), F32), pltpu.VMEM((HALO + ts, wp), F32),
                 _phase_scratch(HALO + ts, wc), _phase_scratch(HALO + ts, wp)], vmem_mb=56,
    )(proj, proj, cw, cb, lg, lb)


def _gates_fwd(name, proj, ya, yb, b_a, b_b, ls, wc, wp, ts):
    _, s_len, npc = proj.shape
    d = ya.shape[1]
    g0 = 2 * wc + wp

    def body(p_ref, ya_ref, yb_ref, ba_ref, bb_ref, ls_ref, z_ref):
        ga = _sigmoid(_cols(p_ref, g0, g0 + d, npc))
        gb = _sigmoid(_cols(p_ref, g0 + d, g0 + 2 * d, npc))
        z = ga * (ya_ref[...] + ba_ref[...]) + gb * ((yb_ref[...] + bb_ref[...]) * ls_ref[...])
        z_ref[...] = z.astype(BF16)

    row = pl.BlockSpec((ts, d), lambda i: (i, 0))
    return _pcall(body, name=name, out_shape=_sds((s_len, d), BF16), grid=(s_len // ts,),
                  in_specs=[pl.BlockSpec((N_CHIPS, ts, npc), lambda i: (0, i, 0)), row, row, _vec(d), _vec(d), _vec(d)],
                  out_specs=row, vmem_mb=48)(proj, ya, yb, b_a, b_b, ls)


def _gates_bwd(name, proj, dz, ya, yb, b_a, b_b, ls, wc, wp, ts):
    _, s_len, npc = proj.shape
    d = ya.shape[1]
    g0 = 2 * wc + wp

    def body(p_ref, dz_ref, ya_ref, yb_ref, ba_ref, bb_ref, ls_ref, dya_ref, dyb_ref, dgt_ref, dba_ref, dls_ref,
             dbb_ref):
        i = pl.program_id(0)
        ga = _sigmoid(_cols(p_ref, g0, g0 + d, npc))
        gb = _sigmoid(_cols(p_ref, g0 + d, g0 + 2 * d, npc))
        dz_v = dz_ref[...].astype(F32)
        y_a = ya_ref[...] + ba_ref[...]
        y_b0 = yb_ref[...] + bb_ref[...]
        ls_v = ls_ref[...]
        dya = dz_v * ga
        dya_ref[...] = dya.astype(BF16)
        _acc_rows(dba_ref, dya, i)
        t = dz_v * gb
        _acc_rows(dls_ref, t * y_b0, i)
        dyb = t * ls_v
        dyb_ref[...] = dyb.astype(BF16)
        _acc_rows(dbb_ref, dyb, i)
        dgt_ref[:, 0:d] = (dz_v * y_a * ga * (1.0 - ga)).astype(BF16)
        dgt_ref[:, d:2 * d] = (dz_v * (y_b0 * ls_v) * gb * (1.0 - gb)).astype(BF16)

    row = pl.BlockSpec((ts, d), lambda i: (i, 0))
    return _pcall(
        body, name=name,
        out_shape=[_sds((s_len, d), BF16), _sds((s_len, d), BF16), _sds((s_len, 2 * d), BF16)] + [_sds((1, d), F32)] * 3,
        grid=(s_len // ts,),
        in_specs=[pl.BlockSpec((N_CHIPS, ts, npc), lambda i: (0, i, 0)), row, row, row, _vec(d), _vec(d), _vec(d)],
        out_specs=[row, row, pl.BlockSpec((ts, 2 * d), lambda i: (i, 0))] + [_vec(d)] * 3, vmem_mb=48,
    )(proj, dz, ya, yb, b_a, b_b, ls)


def _conv_branch_bwd(name, proj, a1, da3, lg, lb, wc, wp, ts):
    _, s_len, npc = proj.shape
    nq = _chips_covering(2 * wc, npc)
    hb = ts // HALO

    n_tiles = s_len // ts

    def fold(v):
        return jnp.sum(v.reshape(ROW_CHUNK // SUBLANES, SUBLANES, v.shape[-1]), axis=0)

    def body(p_ref, ph_ref, a1_ref, da3_ref, lg_ref, lb_ref, da1_ref, dlg_ref, dlb_ref, dcb_ref, dcw_ref,
             a0s_ref, a0p_ref, vec8_ref, dcw8_ref):
        i = pl.program_id(0)
        _stage_glu(p_ref, ph_ref, a0s_ref, i, wc, npc, ts)
        _make_phases(a0s_ref, a0p_ref)

        @pl.when(i == 0)
        def _():
            vec8_ref[...] = jnp.zeros_like(vec8_ref)
            dcw8_ref[...] = jnp.zeros_like(dcw8_ref)

        def chunk(r0):
            rows = pl.ds(r0, ROW_CHUNK)
            xh, rstd, a2 = _layer_norm(a1_ref[rows, :], lg_ref, lb_ref)
            sig = _sigmoid(a2)
            da2 = da3_ref[rows, :].astype(F32) * (sig * (1.0 + a2 * (1.0 - sig)))
            vec8_ref[0] += fold(da2 * xh)
            vec8_ref[1] += fold(da2)
            dxh = da2 * lg_ref[...]
            da1 = rstd * (dxh - jnp.mean(dxh, axis=-1, keepdims=True)
                          - xh * jnp.mean(dxh * xh, axis=-1, keepdims=True))
            da1_ref[rows, :] = da1
            vec8_ref[2] += fold(da1)
            for k in range(CONV_K):
                dcw8_ref[k] += fold(da1 * _window(a0s_ref, a0p_ref, HALO - CONV_K + 1 + k, r0))

        _for_chunks(ts, chunk)

        @pl.when(i == n_tiles - 1)
        def _():
            dlg_ref[...] = jnp.sum(vec8_ref[0], axis=0, keepdims=True)
            dlb_ref[...] = jnp.sum(vec8_ref[1], axis=0, keepdims=True)
            dcb_ref[...] = jnp.sum(vec8_ref[2], axis=0, keepdims=True)
            dcw_ref[...] = jnp.sum(dcw8_ref[...], axis=1)

    return _pcall(
        body, name=name,
        out_shape=[_sds((s_len, wc), F32)] + [_sds((1, wc), F32)] * 3 + [_sds((HALO, wc), F32)],
        grid=(s_len // ts,),
        in_specs=[pl.BlockSpec((nq, ts, npc), lambda i: (0, i, 0)),
                  pl.BlockSpec((nq, HALO, npc), lambda i: (0, jnp.maximum(i * hb - 1, 0), 0)),
                  pl.BlockSpec((ts, wc), lambda i: (i, 0)), pl.BlockSpec((ts, wc), lambda i: (i, 0)),
                  _vec(wc), _vec(wc)],
        out_specs=[pl.BlockSpec((ts, wc), lambda i: (i, 0)), _vec(wc), _vec(wc), _vec(wc),
                   pl.BlockSpec((HALO, wc), lambda i: (0, 0))],
        scratch=[pltpu.VMEM((HALO + ts, wc), F32), _phase_scratch(HALO + ts, wc),
                 pltpu.VMEM((3, SUBLANES, wc), F32), pltpu.VMEM((HALO, SUBLANES, wc), F32)], vmem_mb=56,
    )(proj, proj, a1, da3, lg, lb)


def _mixer_in_bwd(name, proj, da1, dmixed, dgates, cw, wc, wp, ts):
    _, s_len, npc = proj.shape
    nq = _chips_covering(2 * wc, npc)
    gi = wp // len(POOL_WINDOWS)
    hb = ts // HALO
    n_tiles = s_len // ts
    last_hb = s_len // HALO - 1
    d2 = dgates.shape[1]

    def body(p_ref, d1_ref, d1n_ref, dm_ref, dmn_ref, dgt_ref, cw_ref, o_ref, d1s_ref, es_ref, d1p_ref, ep_ref):
        i = pl.program_id(0)
        more = i < n_tiles - 1
        d1s_ref[0:ts, :] = d1_ref[...]
        d1s_ref[ts:ts + HALO, :] = jnp.where(more, d1n_ref[...], 0.0)
        t_abs = i * ts + lax.broadcasted_iota(jnp.int32, (ts + HALO, 1), 0)
        dm_ext = jnp.concatenate([dm_ref[...].astype(F32), jnp.where(more, dmn_ref[...].astype(F32), 0.0)], axis=0)
        for g, win in enumerate(POOL_WINDOWS):
            cs = slice(g * gi, (g + 1) * gi)
            es_ref[:, cs] = dm_ext[:, cs] / jnp.minimum(t_abs + 1, win).astype(F32)
        _make_phases(d1s_ref, d1p_ref)
        _make_phases(es_ref, ep_ref)

        def chunk(r0):
            rows = pl.ds(r0, ROW_CHUNK)
            da0 = cw_ref[0:1, :] * _window(d1s_ref, d1p_ref, CONV_K - 1, r0)
            for k in range(1, CONV_K):
                da0 = da0 + cw_ref[k:k + 1, :] * _window(d1s_ref, d1p_ref, CONV_K - 1 - k, r0)
            glu_a = _cols(p_ref, 0, wc, npc, rows)
            sig = _sigmoid(_cols(p_ref, wc, 2 * wc, npc, rows))
            _store_cols(o_ref, 0, (da0 * sig).astype(BF16), npc, rows)
            _store_cols(o_ref, wc, (da0 * glu_a * sig * (1.0 - sig)).astype(BF16), npc, rows)
            parts = []
            for g, win in enumerate(POOL_WINDOWS):
                cs = slice(g * gi, (g + 1) * gi)
                acc = _window(es_ref, ep_ref, 0, r0, cs)
                for dlt in range(1, win):
                    acc = acc + _window(es_ref, ep_ref, dlt, r0, cs)
                parts.append(acc - dm_ref[rows, cs].astype(F32))
            _store_cols(o_ref, 2 * wc, jnp.concatenate(parts, axis=-1).astype(BF16), npc, rows)

        _for_chunks(ts, chunk)
        _store_cols(o_ref, 2 * wc + wp, dgt_ref[...], npc)

    nxt = lambda i: (jnp.minimum((i + 1) * hb, last_hb), 0)
    return _pcall(
        body, name=name, out_shape=_sds((N_CHIPS, s_len, npc), BF16), grid=(n_tiles,),
        in_specs=[pl.BlockSpec((nq, ts, npc), lambda i: (0, i, 0)),
                  pl.BlockSpec((ts, wc), lambda i: (i, 0)), pl.BlockSpec((HALO, wc), nxt),
                  pl.BlockSpec((ts, wp), lambda i: (i, 0)), pl.BlockSpec((HALO, wp), nxt),
                  pl.BlockSpec((ts, d2), lambda i: (i, 0)),
                  pl.BlockSpec((HALO, wc), lambda i: (0, 0))],
        out_specs=pl.BlockSpec((N_CHIPS, ts, npc), lambda i: (0, i, 0)),
        scratch=[pltpu.VMEM((ts + HALO, wc), F32), pltpu.VMEM((ts + HALO, wp), F32),
                 _phase_scratch(ts + HALO, wc), _phase_scratch(ts + HALO, wp)], vmem_mb=56,
    )(proj, da1, da1, dmixed, dmixed, dgates, cw)


def _ada_fwd(name, c_all, w, b):
    d, cols = w.shape
    tn = 512 if cols % 512 == 0 else cols

    def body(c_ref, w_ref, b_ref, o_ref):
        cv = c_ref[...]
        sc = (cv * _sigmoid(cv)).astype(BF16)
        o_ref[...] = jnp.dot(sc, w_ref[...].astype(BF16), preferred_element_type=F32) + b_ref[...]

    return _pcall(body, name=name, out_shape=_sds((N_DEV, cols), F32), grid=(cols // tn,),
                  in_specs=[pl.BlockSpec((N_DEV, d), lambda j: (0, 0)), pl.BlockSpec((d, tn), lambda j: (0, j)),
                            pl.BlockSpec((1, tn), lambda j: (0, j))],
                  out_specs=pl.BlockSpec((N_DEV, tn), lambda j: (0, j)), vmem_mb=32)(c_all, w, b)


def _adam_math(w, g, m, v):
    m_new = ADAM_B1 * m + (1.0 - ADAM_B1) * g
    v_new = ADAM_B2 * v + (1.0 - ADAM_B2) * (g * g)
    m_hat = m_new / (1.0 - ADAM_B1 ** ADAM_STEP)
    v_hat = v_new / (1.0 - ADAM_B2 ** ADAM_STEP)
    delta = -ADAM_LR * (m_hat / (jnp.sqrt(v_hat) + ADAM_EPS) + ADAM_WD * w)
    return delta, m_new, v_new


def _adamw(name, w, g, m, v):
    rows, cols = w.shape
    tr = _row_tile(rows, cols, 524288)

    def body(w_ref, g_ref, m_ref, v_ref, go_ref, d_ref, mo_ref, vo_ref):
        g = g_ref[...]
        go_ref[...] = g
        d_ref[...], mo_ref[...], vo_ref[...] = _adam_math(w_ref[...], g, m_ref[...], v_ref[...])

    spec = pl.BlockSpec((tr, cols), lambda i: (i, 0))
    return _pcall(body, name=name, out_shape=[_sds(w.shape, F32)] * 4, grid=(rows // tr,), in_specs=[spec] * 4,
                  out_specs=[spec] * 4, vmem_mb=40)(w, g, m, v)


def _ada_grad_adamw(name, c_t, d_ada, w, m, v):
    rows, cols = w.shape
    tr = _tile(rows, 256)
    tc = _tile(cols, 1536) if cols % 1536 == 0 else cols

    def body(c_ref, da_ref, w_ref, m_ref, v_ref, g_ref, d_ref, mo_ref, vo_ref):
        cv = c_ref[...]
        sc = cv * _sigmoid(cv)
        g = sc[:, 0:1] * da_ref[0:1, :]
        for b in range(1, N_DEV):
            g = g + sc[:, b:b + 1] * da_ref[b:b + 1, :]
        g_ref[...] = g
        d_ref[...], mo_ref[...], vo_ref[...] = _adam_math(w_ref[...], g, m_ref[...], v_ref[...])

    spec = pl.BlockSpec((tr, tc), lambda i, j: (i, j))
    return _pcall(body, name=name, out_shape=[_sds(w.shape, F32)] * 4, grid=(rows // tr, cols // tc),
                  in_specs=[pl.BlockSpec((tr, N_DEV), lambda i, j: (i, 0)),
                            pl.BlockSpec((N_DEV, tc), lambda i, j: (0, j)), spec, spec, spec],
                  out_specs=[spec] * 4, vmem_mb=40)(c_t, d_ada, w, m, v)


def _sum_devices(name, gathered, m_per):
    n = gathered.shape[1]

    def body(g_ref, o_ref):
        acc = g_ref[0:m_per, :]
        for dev in range(1, N_DEV):
            acc = acc + g_ref[dev * m_per:(dev + 1) * m_per, :]
        o_ref[...] = acc

    return _pcall(body, name=name, out_shape=_sds((m_per, n), F32),
                  in_specs=[pl.BlockSpec(memory_space=pltpu.VMEM)],
                  out_specs=pl.BlockSpec(memory_space=pltpu.VMEM))(gathered)


def _ffn_fwd(tag, n, w_in_parts, w_out_after_swiglu, dims):
    s_len, d, f_dim = dims["S"], dims["D"], dims["F"]
    tf = f_dim // 4
    tm0, tm = _tile(s_len, 512), _tile(s_len, 1024)
    n_parts = len(w_in_parts)
    nbp = (f_dim // 2) // tf
    nbq = nbp // n_parts

    def ep(accs, ex, outs):
        hh, uu = accs
        sig = _sigmoid(hh)
        silu = hh * sig
        outs[0][0] = (uu * (sig + silu * (1.0 - sig))).astype(BF16)
        outs[0][1] = silu.astype(BF16)
        outs[1][...] = (silu * uu).astype(BF16)

    done = ()
    for part, get_w in enumerate(w_in_parts):
        w_g = get_w()
        col = lambda j, part=part: (j // nbq) * nbp + part * nbq + j % nbq
        done = _matmul(
            f"{tag}_swiglu{part}", n, [w_g, w_g], mode="nn", grid=(2 * nbq, s_len // tm0, 1),
            a_spec=pl.BlockSpec((tm0, d), lambda j, i, k: (i, 0)),
            b_specs=[pl.BlockSpec((None, d, tf), lambda j, i, k, part=part: (j // nbq, 0, part * nbq + j % nbq)),
                     pl.BlockSpec((None, d, tf), lambda j, i, k, part=part: (2 + j // nbq, 0, part * nbq + j % nbq))],
            out_shape=[_sds((2, s_len, f_dim), BF16), _sds((s_len, f_dim), BF16)],
            out_specs=[pl.BlockSpec((2, tm0, tf), lambda j, i, k, col=col: (0, i, col(j))),
                       pl.BlockSpec((tm0, tf), lambda j, i, k, col=col: (i, col(j)))],
            acc_shape=(tm0, tf), epilogue=ep, carry=done)
    hu, act = done
    w_out2d = w_out_after_swiglu()
    tn2 = _tile(d, 1024)
    f = _matmul(
        f"{tag}_down", act, [w_out2d], mode="nn", grid=(s_len // tm, d // tn2, 2),
        a_spec=pl.BlockSpec((tm, 2 * tf), lambda i, j, k: (i, k)),
        b_specs=[pl.BlockSpec((2 * tf, tn2), lambda i, j, k: (k, j))],
        out_shape=_sds((s_len, d), F32), out_specs=pl.BlockSpec((tm, tn2), lambda i, j, k: (i, j)),
        acc_shape=(tm, tn2), epilogue=_ep_store(F32))
    return hu, act, f, w_out2d


def _ffn_bwd(tag, n, hu, act, df, w_in_g, w_out2d, dims, after_dw_out, after_dw_in):
    s_len, d, f_dim = dims["S"], dims["D"], dims["F"]
    tf = f_dim // 4
    tk = _tile(s_len, 2048)
    tn = _tile(d, 1024)
    g_out = _matmul(
        f"{tag}_dw_out", act, [df], mode="tn", grid=(4, d // tn, s_len // tk),
        a_spec=pl.BlockSpec((tk, tf), lambda i, j, k: (k, i)),
        b_specs=[pl.BlockSpec((tk, tn), lambda i, j, k: (k, j))],
        out_shape=_sds((2, 4, tf // 2, d), F32),
        out_specs=pl.BlockSpec((2, None, tf // 2, tn), lambda i, j, k: (0, i, 0, j)),
        acc_shape=(tf, tn), epilogue=_ep_halves(tf // 2))
    after_dw_out(g_out)

    def ep_dhu(accs, ex, outs):
        da = accs[0]
        outs[0][0] = (da * ex[0][0].astype(F32)).astype(BF16)
        outs[0][1] = (da * ex[0][1].astype(F32)).astype(BF16)

    tm = _tile(s_len, 1024)
    hu_spec = pl.BlockSpec((2, tm, tf), lambda j, i, k: (0, i, j))
    dhu = _matmul(
        f"{tag}_dhu", df, [w_out2d], mode="nt", grid=(4, s_len // tm, 1),
        a_spec=pl.BlockSpec((tm, d), lambda j, i, k: (i, 0)),
        b_specs=[pl.BlockSpec((tf, d), lambda j, i, k: (j, 0))],
        extras=[hu], extra_specs=[hu_spec],
        out_shape=_sds((2, s_len, f_dim), BF16), out_specs=hu_spec, acc_shape=(tm, tf), epilogue=ep_dhu)

    hd = d // 2
    rt = hd // 2
    g_in = _matmul(
        f"{tag}_dw_in", n, [dhu], mode="tn", grid=(8, 4, 1),
        a_spec=pl.BlockSpec((s_len, rt), lambda j, i, k: (0, i)),
        b_specs=[pl.BlockSpec((None, s_len, tf), lambda j, i, k: (j // 4, 0, j % 4))],
        out_shape=_sds((2, 4, hd, f_dim // 2), F32),
        out_specs=pl.BlockSpec((None, None, rt, tf), lambda j, i, k: (i // 2, j // 2, i % 2, j % 2)),
        acc_shape=(rt, tf), epilogue=_ep_store(F32))
    after_dw_in(g_in)

    tm2 = _tile(s_len, 1024)
    dn = _matmul(
        f"{tag}_dn", dhu, [w_in_g], mode="nt", grid=(s_len // tm2, d // tn, N_CHIPS),
        a_spec=pl.BlockSpec((None, tm2, 2 * tf), lambda i, j, k: (k // 2, i, k % 2)),
        b_specs=[pl.BlockSpec((None, tn, 2 * tf), lambda i, j, k: (k, j, 0))],
        out_shape=_sds((s_len, d), BF16), out_specs=pl.BlockSpec((tm2, tn), lambda i, j, k: (i, j)),
        acc_shape=(tm2, tn), epilogue=_ep_store(BF16))
    return dn


def kernel(x, c, w_ada, b_ada, g_ffn1, w1_in, w1_out, g_mix, w_in, conv_w, conv_b, ln_a_g, ln_a_b, w_a_out, b_a_out, w_b_group, b_b_group, ls_b, w_out, g_ffn2, w2_in, w2_out, g_final, loss_target, m_w_ada, m_b_ada, m_g_ffn1, m_w1_in, m_w1_out, m_g_mix, m_w_in, m_conv_w, m_conv_b, m_ln_a_g, m_ln_a_b, m_w_a_out, m_b_a_out, m_w_b_group, m_b_b_group, m_ls_b, m_w_out, m_g_ffn2, m_w2_in, m_w2_out, m_g_final, v_w_ada, v_b_ada, v_g_ffn1, v_w1_in, v_w1_out, v_g_mix, v_w_in, v_conv_w, v_conv_b, v_ln_a_g, v_ln_a_b, v_w_a_out, v_b_a_out, v_w_b_group, v_b_b_group, v_ls_b, v_w_out, v_g_ffn2, v_w2_in, v_w2_out, v_g_final):
    weights = dict(w_ada=w_ada, b_ada=b_ada, g_ffn1=g_ffn1, w1_in=w1_in, w1_out=w1_out, g_mix=g_mix, w_in=w_in,
                   conv_w=conv_w, conv_b=conv_b, ln_a_g=ln_a_g, ln_a_b=ln_a_b, w_a_out=w_a_out, b_a_out=b_a_out,
                   w_b_group=w_b_group, b_b_group=b_b_group, ls_b=ls_b, w_out=w_out, g_ffn2=g_ffn2, w2_in=w2_in,
                   w2_out=w2_out, g_final=g_final)
    mom1 = dict(w_ada=m_w_ada, b_ada=m_b_ada, g_ffn1=m_g_ffn1, w1_in=m_w1_in, w1_out=m_w1_out, g_mix=m_g_mix,
                w_in=m_w_in, conv_w=m_conv_w, conv_b=m_conv_b, ln_a_g=m_ln_a_g, ln_a_b=m_ln_a_b, w_a_out=m_w_a_out,
                b_a_out=m_b_a_out, w_b_group=m_w_b_group, b_b_group=m_b_b_group, ls_b=m_ls_b, w_out=m_w_out,
                g_ffn2=m_g_ffn2, w2_in=m_w2_in, w2_out=m_w2_out, g_final=m_g_final)
    mom2 = dict(w_ada=v_w_ada, b_ada=v_b_ada, g_ffn1=v_g_ffn1, w1_in=v_w1_in, w1_out=v_w1_out, g_mix=v_g_mix,
                w_in=v_w_in, conv_w=v_conv_w, conv_b=v_conv_b, ln_a_g=v_ln_a_g, ln_a_b=v_ln_a_b, w_a_out=v_w_a_out,
                b_a_out=v_b_a_out, w_b_group=v_w_b_group, b_b_group=v_b_b_group, ls_b=v_ls_b, w_out=v_w_out,
                g_ffn2=v_g_ffn2, w2_in=v_w2_in, w2_out=v_w2_out, g_final=v_g_final)
    order = list(weights)

    s_len, d = x.shape[1], x.shape[2]
    f_dim = w1_out.shape[0] * N_CHIPS
    wc = conv_w.shape[1] * N_CHIPS
    wp = w_b_group.shape[0] * w_b_group.shape[1]
    n_groups, gi, goq = w_b_group.shape
    npc = w_in.shape[1]
    ada_c = w_ada.shape[1]
    dims = dict(S=s_len, D=d, F=f_dim)
    ts = _tile(s_len, 256)

    xi, yi, ci = lax.axis_index("x"), lax.axis_index("y"), lax.axis_index("c")
    q = 2 * xi + yi
    dev = 2 * q + ci
    q_idx = jnp.reshape(q, (1,)).astype(jnp.int32)
    qc_idx = jnp.stack([q, ci]).astype(jnp.int32)
    _PREVIOUS.clear()

    cwq = conv_w.shape[1]
    pack0 = jnp.concatenate([c.reshape(-1), conv_w.reshape(-1), b_b_group.reshape(-1)])
    n0 = -(-pack0.shape[0] // (8 * LANES)) * LANES
    pack0 = jnp.pad(pack0, (0, 8 * n0 - pack0.shape[0])).reshape(8, n0)
    g0 = _allgather_small("gather_small_in", pack0).reshape(N_DEV, 8 * n0)
    c_all = g0[:, :d]
    south = g0[0::2]
    cw_full = jnp.concatenate([south[k, d:d + CONV_K * cwq].reshape(CONV_K, cwq) for k in range(N_CHIPS)], axis=1)
    cw_pad = jnp.pad(cw_full, ((0, HALO - CONV_K), (0, 0)))
    o_bb = d + CONV_K * cwq
    bb_full = jnp.concatenate([south[k, o_bb:o_bb + n_groups * goq].reshape(n_groups, goq) for k in range(N_CHIPS)],
                              axis=1).reshape(1, d)

    as2d = lambda a: a.reshape(-1, a.shape[-1])
    groups = dict(w1_out=["w1_out"], mix=["w_a_out", "w_b_group", "w_out"], w2_in=["w2_in"], w2_out=["w2_out"])
    big = ["w1_in", "w1_out", "w_in", "w_a_out", "w_b_group", "w_out", "w2_in", "w2_out"]
    cast = lambda nm: _cast_into_gathered(f"cast_{nm}", as2d(weights[nm]), q_idx)
    w1_in_gather = _TwoPartGather("w1_in", cast("w1_in"))

    b_ada_mine = lax.dynamic_slice(b_ada, (q * ada_c,), (ada_c,)).reshape(1, ada_c)
    ada_piece = _ada_fwd("ada_fwd", c_all, w_ada, b_ada_mine)
    casts = {nm: cast(nm) for nm in big[1:]}
    g1 = _allgather_small("gather_ada", ada_piece).reshape(N_DEV, N_DEV, ada_c)
    w1_in_gather.start_second()
    ici = {}
    for grp, names in groups.items():
        ici[grp] = _gather_ici(f"gather_{grp}_ici", [casts[nm] for nm in names])
        if grp == "w1_out":
            w_in_gather = _TwoPartGather("w_in", casts["w_in"])
            w_in_gather.start_second()
    ada_rows = lax.dynamic_index_in_dim(g1[0::2], dev, axis=1, keepdims=False)
    ada = ada_rows.reshape(3, 3, 1, d)
    (sh1, sc1, gt1), (sh2, sc2, gt2), (sh3, sc3, gt3) = [[ada[i, j] for j in range(3)] for i in range(3)]

    row = lambda vct: vct.reshape(1, -1)
    g1v, gmv, g2v, gfv = row(g_ffn1), row(g_mix), row(g_ffn2), row(g_final)

    def arrived(grp):
        return _gather_d2d(f"gather_{grp}_d2d", ici[grp].wait())

    def gathered(fwd, grp):
        return {nm: g.reshape(N_CHIPS, 2 * g.shape[2], g.shape[3]) for nm, g in zip(groups[grp], fwd.wait())}

    x2 = x[0]
    tgt = loss_target[0]

    n1 = _norm_mod("ffn1_norm", x2, g1v, sc1, sh1, ts)
    fwd, w1_in_parts = {}, []

    def w1_in_part(part):
        def get():
            w1_in_gather.arrive(part)
            w1_in_parts.append(w1_in_gather.ready(part))
            return w1_in_parts[-1]
        return get

    def w1_out_after_swiglu():
        fwd["w1_out"] = arrived("w1_out")
        w_in_gather.arrive(0)
        return gathered(fwd["w1_out"], "w1_out")["w1_out"].reshape(f_dim, d)

    hu1, act1, f1, w1_out_2d = _ffn_fwd("ffn1", n1, [w1_in_part(0), w1_in_part(1)], w1_out_after_swiglu, dims)
    w1_in_g = w1_in_parts[-1]
    h1, n2 = _residual_norm_mod("mix_norm", x2, f1, gt1, 0.5, gmv, sc2, sh2, ts)

    tm = _tile(s_len, 1024)
    tnp = npc // 2
    proj = ()
    for part in range(2):
        if part:
            w_in_gather.arrive(part)
        w_in_g = w_in_gather.ready(part)
        proj = (_matmul(
            f"mix_proj{part}", n2, [w_in_g], mode="nn", grid=(s_len // tm, N_CHIPS, 1),
            a_spec=pl.BlockSpec((tm, d), lambda i, j, k: (i, 0)),
            b_specs=[pl.BlockSpec((None, d, tnp), lambda i, j, k, part=part: (j, 0, part))],
            out_shape=_sds((N_CHIPS, s_len, npc), BF16),
            out_specs=pl.BlockSpec((None, tm, tnp), lambda i, j, k, part=part: (j, i, part)),
            acc_shape=(tm, tnp), epilogue=_ep_store(BF16), carry=proj),)
    proj = proj[0]
    fwd["mix"] = arrived("mix")
    cbv, lgv, lbv = row(conv_b), row(ln_a_g), row(ln_a_b)
    a3, mixed, conv_out = _mixer_mid("mix_mid", proj, cw_pad, cbv, lgv, lbv, wc, wp, ts)
    wts = gathered(fwd["mix"], "mix")
    w_out_2d = wts["w_out"].reshape(d, d)
    w_a_g = wts["w_a_out"]
    w_b_r = _regroup("regroup_w_b", wts["w_b_group"], n_groups)
    dq = d // N_CHIPS
    ya = _matmul(
        "mix_ya", a3, [w_a_g], mode="nn", grid=(s_len // tm, N_CHIPS, 1),
        a_spec=pl.BlockSpec((tm, wc), lambda i, j, k: (i, 0)),
        b_specs=[pl.BlockSpec((None, wc, dq), lambda i, j, k: (j, 0, 0))],
        out_shape=_sds((s_len, d), BF16), out_specs=pl.BlockSpec((tm, dq), lambda i, j, k: (i, j)),
        acc_shape=(tm, dq), epilogue=_ep_store(BF16))
    yb = _matmul(
        "mix_yb", mixed, [w_b_r], mode="nn", grid=(s_len // tm, n_groups, 1),
        a_spec=pl.BlockSpec((tm, gi), lambda i, j, k: (i, j)),
        b_specs=[pl.BlockSpec((None, gi, dq), lambda i, j, k: (j, 0, 0))],
        out_shape=_sds((s_len, d), BF16), out_specs=pl.BlockSpec((tm, dq), lambda i, j, k: (i, j)),
        acc_shape=(tm, dq), epilogue=_ep_store(BF16))
    bav, lsv = row(b_a_out), row(ls_b)
    z = _gates_fwd("mix_gates", proj, ya, yb, bav, bb_full, lsv, wc, wp, ts)
    tn = _tile(d, 1024)
    mix = _matmul(
        "mix_out", z, [w_out_2d], mode="nn", grid=(s_len // tm, d // tn, 1),
        a_spec=pl.BlockSpec((tm, d), lambda i, j, k: (i, 0)),
        b_specs=[pl.BlockSpec((d, tn), lambda i, j, k: (0, j))],
        out_shape=_sds((s_len, d), F32), out_specs=pl.BlockSpec((tm, tn), lambda i, j, k: (i, j)),
        acc_shape=(tm, tn), epilogue=_ep_store(F32))
    fwd["w2_in"] = arrived("w2_in")
    h2, n3 = _residual_norm_mod("ffn2_norm", h1, mix, gt2, 1.0, g2v, sc3, sh3, ts)
    w2_in_g = gathered(fwd["w2_in"], "w2_in")["w2_in"]
    hu2, act2, f3, w2_out_2d = _ffn_fwd(
        "ffn2", n3, [lambda: w2_in_g],
        lambda: gathered(arrived("w2_out"), "w2_out")["w2_out"].reshape(f_dim, d), dims)

    dh3, df3, d_gf, d_gt3, loss_cols = _final_loss("final_loss", h2, f3, tgt, gt3, 0.5, gfv, ts)
    rs, held = {}, {}
    dn3 = _ffn_bwd(
        "ffn2", n3, hu2, act2, df3, w2_in_g, w2_out_2d, dims,
        after_dw_out=lambda g: held.update(w2_out=g),
        after_dw_in=lambda g: rs.update(ffn2=_ReduceScatter("g_ffn2", ["w2_out", "w2_in"], [held["w2_out"], g],
                                                            qc_idx)))
    dh2, dmix, d_sh3, d_sc3, d_g2, d_gt2 = _norm_mod_bwd("ffn2_norm_bwd", h2, dn3, dh3, g2v, sc3, ts,
                                                         prev=(mix, gt2, 1.0))
    rs["ffn2"].step2()

    tk = s_len
    hq = d // (2 * N_CHIPS)
    gw_out = _matmul(
        "mix_dw_out", z, [dmix], mode="tn", grid=(N_CHIPS, d // tn, s_len // tk),
        a_spec=pl.BlockSpec((tk, 2 * hq), lambda i, j, k: (k, i)),
        b_specs=[pl.BlockSpec((tk, tn), lambda i, j, k: (k, j))],
        out_shape=_sds((2, N_CHIPS, hq, d), F32),
        out_specs=pl.BlockSpec((2, None, hq, tn), lambda i, j, k: (0, i, 0, j)),
        acc_shape=(2 * hq, tn), epilogue=_ep_halves(hq))
    dz = _matmul(
        "mix_dz", dmix, [w_out_2d], mode="nt", grid=(s_len // tm, d // tn, 1),
        a_spec=pl.BlockSpec((tm, d), lambda i, j, k: (i, 0)),
        b_specs=[pl.BlockSpec((tn, d), lambda i, j, k: (j, 0))],
        out_shape=_sds((s_len, d), BF16), out_specs=pl.BlockSpec((tm, tn), lambda i, j, k: (i, j)),
        acc_shape=(tm, tn), epilogue=_ep_store(BF16))
    dya, dyb, dgates, d_ba, d_ls, d_bb = _gates_bwd("mix_gates_bwd", proj, dz, ya, yb, bav, bb_full, lsv, wc, wp, ts)
    gw_a = _matmul(
        "mix_dw_a", a3, [dya], mode="tn", grid=(1, N_CHIPS, s_len // tk),
        a_spec=pl.BlockSpec((tk, wc), lambda i, j, k: (k, 0)),
        b_specs=[pl.BlockSpec((tk, dq), lambda i, j, k: (k, j))],
        out_shape=_sds((2, N_CHIPS, wc // 2, dq), F32),
        out_specs=pl.BlockSpec((2, None, wc // 2, dq), lambda i, j, k: (0, j, 0, 0)),
        acc_shape=(wc, dq), epilogue=_ep_halves(wc // 2))
    da3 = _matmul(
        "mix_da3", dya, [w_a_g], mode="nt", grid=(s_len // tm, 1, N_CHIPS),
        a_spec=pl.BlockSpec((tm, dq), lambda i, j, k: (i, k)),
        b_specs=[pl.BlockSpec((None, wc, dq), lambda i, j, k: (k, 0, 0))],
        out_shape=_sds((s_len, wc), BF16), out_specs=pl.BlockSpec((tm, wc), lambda i, j, k: (i, 0)),
        acc_shape=(tm, wc), epilogue=_ep_store(BF16))
    gpr = n_groups // 2

    def ep_by_chip(accs, ex, outs):
        for k in range(N_CHIPS):
            outs[0][k] = accs[0][:, k * goq:(k + 1) * goq]

    gw_b = _matmul(
        "mix_dw_b", mixed, [dyb], mode="tn", grid=(1, n_groups, s_len // tk),
        a_spec=pl.BlockSpec((tk, gi), lambda i, j, k: (k, j)),
        b_specs=[pl.BlockSpec((tk, dq), lambda i, j, k: (k, j))],
        out_shape=_sds((2, N_CHIPS, gpr * gi, goq), F32),
        out_specs=pl.BlockSpec((None, N_CHIPS, gi, goq), lambda i, j, k: (j // gpr, 0, j % gpr, 0)),
        acc_shape=(gi, dq), epilogue=ep_by_chip)
    dmixed = _matmul(
        "mix_dmixed", dyb, [w_b_r], mode="nt", grid=(s_len // tm, n_groups, 1),
        a_spec=pl.BlockSpec((tm, dq), lambda i, j, k: (i, j)),
        b_specs=[pl.BlockSpec((None, gi, dq), lambda i, j, k: (j, 0, 0))],
        out_shape=_sds((s_len, wp), BF16), out_specs=pl.BlockSpec((tm, gi), lambda i, j, k: (i, j)),
        acc_shape=(tm, gi), epilogue=_ep_store(BF16))
    da1, d_lg, d_lb, d_cb, d_cw = _conv_branch_bwd("mix_conv_bwd", proj, conv_out, da3, lgv, lbv, wc, wp, ts)
    dproj = _mixer_in_bwd("mix_in_bwd", proj, da1, dmixed, dgates, cw_pad, wc, wp, ts)
    hd = d // 2
    rt = hd // 2
    gw_in = _matmul(
        "mix_dw_in", n2, [dproj], mode="tn", grid=(N_CHIPS, 4, 1),
        a_spec=pl.BlockSpec((s_len, rt), lambda j, i, k: (0, i)),
        b_specs=[pl.BlockSpec((None, s_len, npc), lambda j, i, k: (j, 0, 0))],
        out_shape=_sds((2, N_CHIPS, hd, npc), F32),
        out_specs=pl.BlockSpec((None, None, rt, npc), lambda j, i, k: (i // 2, j, i % 2, 0)),
        acc_shape=(rt, npc), epilogue=_ep_store(F32))
    rs["mix"] = _ReduceScatter("g_mix", ["w_in", "w_a_out", "w_b_group", "w_out"], [gw_in, gw_a, gw_b, gw_out],
                               qc_idx)
    rs["ffn2"].step3()
    dn2 = _matmul(
        "mix_dn", dproj, [w_in_g], mode="nt", grid=(s_len // tm, d // tn, N_CHIPS),
        a_spec=pl.BlockSpec((None, tm, npc), lambda i, j, k: (k, i, 0)),
        b_specs=[pl.BlockSpec((None, tn, npc), lambda i, j, k: (k, j, 0))],
        out_shape=_sds((s_len, d), BF16), out_specs=pl.BlockSpec((tm, tn), lambda i, j, k: (i, j)),
        acc_shape=(tm, tn), epilogue=_ep_store(BF16))
    dh1, df1, d_sh2, d_sc2, d_gm, d_gt1 = _norm_mod_bwd("mix_norm_bwd", h1, dn2, dh2, gmv, sc2, ts,
                                                        prev=(f1, gt1, 0.5))
    rs["mix"].step2()

    def w1_in_ready(g):
        rs["w1_in"] = _ReduceScatter("g_w1_in", ["w1_in"], [g], qc_idx)
        rs["w1_out"].step2()
        rs["mix"].step3()

    dn1 = _ffn_bwd(
        "ffn1", n1, hu1, act1, df1, w1_in_g, w1_out_2d, dims,
        after_dw_out=lambda g: rs.update(w1_out=_ReduceScatter("g_w1_out", ["w1_out"], [g], qc_idx)),
        after_dw_in=w1_in_ready)
    grad_x, d_sh1, d_sc1, d_g1 = _norm_mod_bwd("ffn1_norm_bwd", x2, dn1, dh1, g1v, sc1, ts)

    d_ada = jnp.concatenate([d_sh1, d_sc1, d_gt1, d_sh2, d_sc2, d_gt2, d_sh3, d_sc3, d_gt3], axis=1)
    small = [d_ada, d_g1, d_gm, d_cw[:CONV_K].reshape(1, -1), d_cb, d_lg, d_lb, d_ba, d_bb, d_ls, d_g2, d_gf,
             loss_cols]
    sizes = [a.shape[1] for a in small]
    pack1 = jnp.concatenate(small, axis=1).reshape(-1)
    n1p = -(-pack1.shape[0] // (8 * LANES)) * LANES
    pack1 = jnp.pad(pack1, (0, 8 * n1p - pack1.shape[0])).reshape(8, n1p)
    g2 = _allgather_small("gather_small_grads", pack1)
    rs["w1_in"].step2()
    total = _sum_devices("sum_small_grads", g2, 8).reshape(-1)
    offs = [0]
    for sz in sizes:
        offs.append(offs[-1] + sz)
    tot = [total[offs[k]:offs[k + 1]] for k in range(len(sizes))]
    d_ada_all = g2.reshape(N_DEV, 8 * n1p)[:, :sizes[0]]
    loss = jnp.sum(tot[12])

    grads = {}
    grads["b_ada"] = tot[0]
    grads["g_ffn1"], grads["g_mix"] = tot[1], tot[2]
    grads["conv_w"] = lax.dynamic_slice(tot[3].reshape(CONV_K, wc), (0, q * cwq), (CONV_K, cwq))
    grads["conv_b"], grads["ln_a_g"], grads["ln_a_b"], grads["b_a_out"] = tot[4], tot[5], tot[6], tot[7]
    grads["b_b_group"] = lax.dynamic_slice(tot[8].reshape(n_groups, N_CHIPS * goq), (0, q * goq), (n_groups, goq))
    grads["ls_b"], grads["g_ffn2"], grads["g_final"] = tot[9], tot[10], tot[11]

    delta, new_m, new_v = {}, {}, {}

    def adamw_group(reduced):
        for nm, g in reduced.items():
            shp = weights[nm].shape
            go, dl, mo, vo = _adamw(f"adamw_{nm}", as2d(weights[nm]), g, as2d(mom1[nm]), as2d(mom2[nm]))
            grads[nm], delta[nm], new_m[nm], new_v[nm] = go.reshape(shp), dl.reshape(shp), mo.reshape(shp), vo.reshape(shp)

    adamw_group(rs["ffn2"].result())
    rs["w1_out"].step3()
    adamw_group(rs["mix"].result())
    d_ada_mine = lax.dynamic_slice(d_ada_all, (0, q * ada_c), (N_DEV, ada_c))
    grads["w_ada"], delta["w_ada"], new_m["w_ada"], new_v["w_ada"] = _ada_grad_adamw(
        "adamw_w_ada", c_all.T, d_ada_mine, w_ada, m_w_ada, v_w_ada)
    rs["w1_in"].step3()
    smalls = [nm for nm in order if nm not in big and nm != "w_ada"]
    flat = lambda src: jnp.concatenate([src[nm].reshape(-1) for nm in smalls])
    n_small = sum(weights[nm].size for nm in smalls)
    rows_s = -(-n_small // (8 * LANES)) * 8
    packed = [jnp.pad(flat(src), (0, rows_s * LANES - n_small)).reshape(rows_s, LANES)
              for src in (weights, grads, mom1, mom2)]
    _, dl_s, mo_s, vo_s = _adamw("adamw_small", *packed)
    off = 0
    for nm in smalls:
        sz, shp = weights[nm].size, weights[nm].shape
        delta[nm] = dl_s.reshape(-1)[off:off + sz].reshape(shp)
        new_m[nm] = mo_s.reshape(-1)[off:off + sz].reshape(shp)
        new_v[nm] = vo_s.reshape(-1)[off:off + sz].reshape(shp)
        grads[nm] = grads[nm].reshape(shp)
        off += sz
    adamw_group(rs["w1_out"].result())
    adamw_group(rs["w1_in"].result())

    return (loss, grad_x[None], *[grads[nm] for nm in order], *[delta[nm] for nm in order],
            *[new_m[nm] for nm in order], *[new_v[nm] for nm in order])
```

```python
import jax
import jax.numpy as jnp
from jax import lax
from jax.experimental import pallas as pl
from jax.experimental.pallas import tpu as pltpu

F32 = jnp.float32
BF16 = jnp.bfloat16
MESH = pl.DeviceIdType.MESH
ANY = pl.BlockSpec(memory_space=pl.ANY)
HBM = pl.BlockSpec(memory_space=pltpu.HBM)
SEM = pl.BlockSpec(memory_space=pltpu.SEMAPHORE)
EFFECT = pltpu.SideEffectType.DATAFLOW_SIDE_EFFECTING

EPS = 1e-6
CONV_K = 31
HALO = 32
POOL_WINDOWS = (2, 4, 8, 16)
N_CHIPS = 4
N_DEV = 8
LANES = 128

ADAM_LR = 0.001
ADAM_B1 = 0.9
ADAM_B2 = 0.999
ADAM_EPS = 1e-08
ADAM_WD = 0.01
ADAM_STEP = 10

DN = {
    "nn": (((1,), (0,)), ((), ())),
    "nt": (((1,), (1,)), ((), ())),
    "tn": (((0,), (0,)), ((), ())),
}


_PREVIOUS = []


def _ordered(call, args, n_lead, body, token=None, sources=()):
    dep = [pltpu.with_memory_space_constraint(p, pltpu.HBM) if p.size * p.dtype.itemsize >= (1 << 20) else p
           for p in _PREVIOUS if all(p is not a for a in (*args, *sources))]

    def wrapped(*refs):
        return body(*refs[:n_lead], *refs[n_lead + len(dep):])

    outs = call(wrapped, [ANY] * len(dep))(*args, *dep)
    seq = outs if isinstance(outs, (list, tuple)) else [outs]
    _PREVIOUS[:] = [seq[token] if token is not None else
                    next(o for o in seq if jnp.issubdtype(o.dtype, jnp.floating))]
    return outs


def _pcall(body, *, name, out_shape, grid=None, in_specs=None, out_specs=None, scratch=(), aliases=None,
           prefetch=0, vmem_mb=None):
    params = {}
    if grid is not None:
        params["dimension_semantics"] = ("arbitrary",) * len(grid)
    if vmem_mb is not None:
        params["vmem_limit_bytes"] = vmem_mb << 20
    def in_hbm(shape, spec):
        big = shape.size * jnp.dtype(shape.dtype).itemsize >= (1 << 20)
        return pltpu.HBM(shape.shape, shape.dtype) if big and getattr(spec, "memory_space", None) != pltpu.VMEM else shape

    if isinstance(out_shape, (list, tuple)):
        out_shape = [in_hbm(s, sp) for s, sp in zip(out_shape, out_specs)]
    else:
        out_shape = in_hbm(out_shape, out_specs)
    kw = dict(name=name, out_shape=out_shape, compiler_params=pltpu.CompilerParams(**params))
    if aliases:
        kw["input_output_aliases"] = aliases

    def call(wrapped, dep_specs):
        specs = list(in_specs) + dep_specs
        if prefetch:
            return pl.pallas_call(wrapped, grid_spec=pltpu.PrefetchScalarGridSpec(
                num_scalar_prefetch=prefetch, grid=grid, in_specs=specs, out_specs=out_specs,
                scratch_shapes=list(scratch)), **kw)
        if grid is not None:
            return pl.pallas_call(wrapped, grid=grid, in_specs=specs, out_specs=out_specs,
                                  scratch_shapes=list(scratch), **kw)
        return pl.pallas_call(wrapped, in_specs=specs, out_specs=out_specs, scratch_shapes=list(scratch), **kw)

    def run(*args):
        specs = [None] * prefetch + list(in_specs)
        placed = [pltpu.with_memory_space_constraint(a, pltpu.HBM)
                  if a.size * a.dtype.itemsize >= (1 << 20) and getattr(s, "memory_space", None) != pltpu.VMEM else a
                  for a, s in zip(args, specs)]
        return _ordered(call, placed, prefetch + len(in_specs), body, sources=args)

    return run


def _tile(dim, pref):
    t = min(dim, pref)
    assert dim % t == 0, (dim, pref)
    return t


def _sds(shape, dtype):
    return jax.ShapeDtypeStruct(tuple(shape), dtype)


def _sigmoid(v):
    return 0.5 * jnp.tanh(0.5 * v) + 0.5


def _vec(w):
    return pl.BlockSpec((1, w), lambda *_: (0, 0))


def _acc_rows(ref, val, i):
    @pl.when(i == 0)
    def _():
        ref[...] = jnp.zeros_like(ref)

    ref[...] += jnp.sum(val, axis=0, keepdims=True)


def _matmul(name, a, bs, *, mode, grid, a_spec, b_specs, out_shape, out_specs, acc_shape, epilogue,
            extras=(), extra_specs=(), vmem_mb=56, carry=()):
    nb, ne, nk, nc = len(bs), len(extras), grid[2], len(carry)
    dn = DN[mode]

    def body(*all_refs):
        refs = all_refs[:1 + nb + ne] + all_refs[1 + nb + ne + nc:]
        a_ref, b_refs, ex = refs[0], refs[1:1 + nb], refs[1 + nb:1 + nb + ne]
        if nk == 1:
            outs = refs[1 + nb + ne:]
            accs = [lax.dot_general(a_ref[...], b[...], dn, preferred_element_type=F32) for b in b_refs]
            epilogue(accs, ex, outs)
            return
        outs, acc_refs = refs[1 + nb + ne:-nb], refs[-nb:]
        k = pl.program_id(2)

        @pl.when(k == 0)
        def _():
            for acc in acc_refs:
                acc[...] = jnp.zeros_like(acc)

        for acc, b in zip(acc_refs, b_refs):
            acc[...] += lax.dot_general(a_ref[...], b[...], dn, preferred_element_type=F32)

        @pl.when(k == nk - 1)
        def _():
            epilogue([acc[...] for acc in acc_refs], ex, outs)

    scratch = [pltpu.VMEM(acc_shape, F32) for _ in range(nb)] if nk > 1 else []
    return _pcall(body, name=name, out_shape=out_shape, grid=grid,
                  in_specs=[a_spec, *b_specs, *extra_specs, *[ANY] * nc], out_specs=out_specs, scratch=scratch,
                  aliases={1 + nb + ne + i: i for i in range(nc)}, vmem_mb=vmem_mb)(a, *bs, *extras, *carry)


def _ep_store(dtype):
    def ep(accs, ex, outs):
        outs[0][...] = accs[0].astype(dtype)
    return ep


def _ep_halves(h):
    def ep(accs, ex, outs):
        outs[0][0] = accs[0][:h]
        outs[0][1] = accs[0][h:]
    return ep


def _place():
    x, y, c = lax.axis_index("x"), lax.axis_index("y"), lax.axis_index("c")
    chips = [(1 - x, y), (x, 1 - y), (1 - x, 1 - y)]
    return x, y, c, chips


def _allgather_small(name, block):
    m_per, n = block.shape

    def body(x_ref, out_ref, send_sems, recv_sems, local_sem):
        x, y, c, chips = _place()
        me, sibling = (x, y, c), (x, y, 1 - c)

        def rows(px, py, pc):
            return out_ref.at[pl.ds((4 * px + 2 * py + pc) * m_per, m_per), :]

        def copy(k, blk, to, src=None):
            return pltpu.make_async_remote_copy(
                src_ref=rows(*blk) if src is None else src, dst_ref=rows(*blk),
                send_sem=send_sems.at[k], recv_sem=recv_sems.at[k], device_id=to, device_id_type=MESH)

        mine = pltpu.make_async_copy(x_ref, rows(*me), local_sem)
        mine.start()
        first = [copy(0, me, sibling, src=x_ref)]
        first += [copy(1 + j, me, (*chip, c), src=x_ref) for j, chip in enumerate(chips)]
        for cp in first:
            cp.start()
        passed = [copy(4 + j, (*chip, c), sibling) for j, chip in enumerate(chips)]
        for j, chip in enumerate(chips):
            copy(1 + j, (*chip, c), me).wait_recv()
            passed[j].start()
        copy(0, sibling, me).wait_recv()
        for j, chip in enumerate(chips):
            copy(4 + j, (*chip, 1 - c), me).wait_recv()
        for cp in first + passed:
            cp.wait_send()
        mine.wait()

    return _pcall(
        body, name=name, out_shape=_sds((N_DEV * m_per, n), block.dtype),
        in_specs=[pl.BlockSpec(memory_space=pltpu.VMEM)], out_specs=pl.BlockSpec(memory_space=pltpu.VMEM),
        scratch=[pltpu.SemaphoreType.DMA((7,)), pltpu.SemaphoreType.DMA((7,)), pltpu.SemaphoreType.DMA],
    )(block)


class _SplitCopies:
    def __init__(self, name, arrays, plan, n_copies):
        self.name, self.plan, self.n = name, plan, len(arrays)
        n = self.n

        def body(*refs):
            send, recv, token = refs[n], refs[n + 1], refs[-1]
            for k, (src, dst, _, peer) in enumerate(plan(refs[:n])):
                pltpu.make_async_remote_copy(src_ref=src, dst_ref=dst, send_sem=send.at[k], recv_sem=recv.at[k],
                                             device_id=peer, device_id_type=MESH).start()
            token[...] = jnp.zeros_like(token)

        def call(wrapped, dep_specs):
            return pl.pallas_call(
                wrapped, name=f"{name}_start",
                out_shape=(pltpu.SemaphoreType.DMA((n_copies,)), pltpu.SemaphoreType.DMA((n_copies,)),
                           *[pltpu.HBM(a.shape, a.dtype) for a in arrays], _sds((8, LANES), F32)),
                in_specs=[HBM] * n + dep_specs,
                out_specs=(SEM, SEM, *[HBM] * n, pl.BlockSpec(memory_space=pltpu.VMEM)),
                input_output_aliases={i: 2 + i for i in range(n)},
                compiler_params=pltpu.CompilerParams(has_side_effects=EFFECT))

        outs = _ordered(call, [pltpu.with_memory_space_constraint(a, pltpu.HBM) for a in arrays], n, body, token=-1,
                        sources=arrays)
        self.send, self.recv, self.arrays = outs[0], outs[1], list(outs[2:2 + n])

    def wait(self, arrays=None):
        n, plan = self.n, self.plan
        if arrays is not None:
            self.arrays = list(arrays)

        def body(*refs):
            send, recv, token = refs[n], refs[n + 1], refs[-1]
            for k, (src, _, landing, peer) in enumerate(plan(refs[:n])):
                cp = pltpu.make_async_remote_copy(src_ref=src, dst_ref=landing, send_sem=send.at[k],
                                                  recv_sem=recv.at[k], device_id=peer, device_id_type=MESH)
                cp.wait_send()
                cp.wait_recv()
            token[...] = jnp.zeros_like(token)

        def call(wrapped, dep_specs):
            return pl.pallas_call(
                wrapped, name=f"{self.name}_wait",
                out_shape=(*[pltpu.HBM(a.shape, a.dtype) for a in self.arrays], _sds((8, LANES), F32)),
                in_specs=[HBM] * n + [SEM, SEM] + dep_specs,
                out_specs=(*[HBM] * n, pl.BlockSpec(memory_space=pltpu.VMEM)),
                input_output_aliases={i: i for i in range(n)},
                compiler_params=pltpu.CompilerParams(has_side_effects=EFFECT))

        return list(_ordered(call, [*self.arrays, self.send, self.recv], n + 2, body, token=-1))[:n]


def _col_range(g, part, n_parts):
    width = g.shape[-1] // n_parts
    return (slice(None), pl.ds(part * width, width))


def _gather_ici(name, gathered, part=0, n_parts=1):
    def plan(refs):
        x, y, c, chips = _place()
        q = 2 * x + y
        return [(g.at[(q, c, *_col_range(g, part, n_parts))], g.at[(q, c, *_col_range(g, part, n_parts))],
                 g.at[(2 * px + py, c, *_col_range(g, part, n_parts))], (px, py, c))
                for g in refs for px, py in chips]

    return _SplitCopies(name, gathered, plan, 3 * len(gathered))


def _gather_d2d(name, gathered, part=0, n_parts=1):
    def plan(refs):
        x, y, c, chips = _place()
        return [(g.at[(2 * px + py, c, *_col_range(g, part, n_parts))],
                 g.at[(2 * px + py, c, *_col_range(g, part, n_parts))],
                 g.at[(2 * px + py, 1 - c, *_col_range(g, part, n_parts))], (x, y, 1 - c))
                for g in refs for px, py in chips]

    return _SplitCopies(name, gathered, plan, 3 * len(gathered))


class _TwoPartGather:
    def __init__(self, name, gathered):
        self.name, self.d2d = name, {}
        self.ici = [_gather_ici(f"gather_{name}_a_ici", [gathered], 0, 2)]
        self.buf = self.ici[0].arrays

    def start_second(self):
        self.ici.append(_gather_ici(f"gather_{self.name}_b_ici", self.buf, 1, 2))
        self.buf = self.ici[1].arrays

    def arrive(self, part):
        here = self.ici[part].wait(self.buf)
        self.d2d[part] = _gather_d2d(f"gather_{self.name}_{'ab'[part]}_d2d", here, part, 2)
        self.buf = self.d2d[part].arrays

    def ready(self, part):
        self.buf = self.d2d[part].wait(self.buf)
        g = self.buf[0]
        return g.reshape(N_CHIPS, 2 * g.shape[2], g.shape[3])


def _scatter_sibling(name, grads):
    n = len(grads)

    def plan(refs):
        x, y, c, _ = _place()
        return [(refs[w].at[1 - c], refs[n + w], refs[n + w], (x, y, 1 - c)) for w in range(n)]

    landing = [lax.empty(g.shape[1:], g.dtype) for g in grads]
    return _SplitCopies(name, [*grads, *landing], plan, n)


def _scatter_chips(name, sums):
    n = len(sums)

    def plan(refs):
        x, y, c, chips = _place()
        return [(refs[w].at[2 * px + py], refs[n + w].at[j], refs[n + w].at[j], (px, py, c))
                for w in range(n) for j, (px, py) in enumerate(chips)]

    landing = [lax.empty((3, *s.shape[1:]), s.dtype) for s in sums]
    return _SplitCopies(name, [*sums, *landing], plan, 3 * n)


def _share_final(name, finals):
    def plan(refs):
        x, y, c, _ = _place()
        return [(f.at[c], f.at[c], f.at[1 - c], (x, y, 1 - c)) for f in refs]

    return _SplitCopies(name, finals, plan, len(finals))


def _row_tile(rows, cols, budget_elems=786432):
    best = 8
    for t in range(8, rows + 1, 8):
        if rows % t == 0 and t * cols <= budget_elems:
            best = t
    return best if rows % best == 0 else rows


def _sum_with_sibling(name, grad, recv, qc_idx):
    _, _, h, cols = grad.shape
    tr = _row_tile(h, cols)

    def body(s_ref, g_ref, r_ref, own_ref, pb_ref):
        p = g_ref[...] + r_ref[...]
        pb_ref[...] = p.astype(BF16)

        @pl.when(pl.program_id(1) == s_ref[0])
        def _():
            own_ref[...] = p

    blk = pl.BlockSpec((None, tr, cols), lambda r, k, s: (k, r, 0))
    return _pcall(
        body, name=name, out_shape=[_sds((h, cols), F32), _sds((N_CHIPS, h, cols), BF16)],
        grid=(h // tr, N_CHIPS), prefetch=1,
        in_specs=[pl.BlockSpec((None, None, tr, cols), lambda r, k, s: (s[1], k, r, 0)), blk],
        out_specs=[pl.BlockSpec((tr, cols), lambda r, k, s: (r, 0)), blk], vmem_mb=32,
    )(qc_idx, grad, recv)


def _sum_chips(name, own, recv, qc_idx):
    h, cols = own.shape
    tr = _row_tile(h, cols)

    def body(s_ref, p_ref, t_ref, o_ref):
        o_ref[...] = ((p_ref[...] + t_ref[0].astype(F32)) + t_ref[1].astype(F32)) + t_ref[2].astype(F32)

    return _pcall(
        body, name=name, out_shape=_sds((2, h, cols), F32), grid=(h // tr,), prefetch=1,
        in_specs=[pl.BlockSpec((tr, cols), lambda r, s: (r, 0)),
                  pl.BlockSpec((3, tr, cols), lambda r, s: (0, r, 0))],
        out_specs=pl.BlockSpec((None, tr, cols), lambda r, s: (s[1], r, 0)), vmem_mb=32,
    )(qc_idx, own, recv)


class _ReduceScatter:
    def __init__(self, tag, names, grads, qc_idx):
        self.tag, self.names, self.n, self.qc_idx = tag, names, len(grads), qc_idx
        self.copies = _scatter_sibling(f"{tag}_rs_sibling", grads)

    def step2(self):
        n = self.n
        arrs = self.copies.wait()
        sums = [_sum_with_sibling(f"{nm}_sum_sibling", arrs[w], arrs[n + w], self.qc_idx)
                for w, nm in enumerate(self.names)]
        self.own = [s[0] for s in sums]
        self.copies = _scatter_chips(f"{self.tag}_rs_chips", [s[1] for s in sums])

    def step3(self):
        n = self.n
        arrs = self.copies.wait()
        finals = [_sum_chips(f"{nm}_sum_chips", self.own[w], arrs[n + w], self.qc_idx)
                  for w, nm in enumerate(self.names)]
        self.copies = _share_final(f"{self.tag}_rs_final", finals)

    def result(self):
        return {nm: f.reshape(2 * f.shape[1], f.shape[2]) for nm, f in zip(self.names, self.copies.wait())}


def _cast_into_gathered(name, w, q_idx):
    rows, cols = w.shape
    h = rows // 2
    tr = _row_tile(h, cols, 1 << 20)
    nr = h // tr

    def body(s_ref, w_ref, o_ref):
        o_ref[...] = w_ref[...].astype(BF16)

    return _pcall(body, name=name, out_shape=_sds((N_CHIPS, 2, h, cols), BF16), grid=(2, nr), prefetch=1,
                  in_specs=[pl.BlockSpec((tr, cols), lambda hf, r, s: (hf * nr + r, 0))],
                  out_specs=pl.BlockSpec((None, None, tr, cols), lambda hf, r, s: (s[0], hf, r, 0)),
                  vmem_mb=32)(q_idx, w)


def _regroup(name, w, n_groups):
    n_chips, rows, goq = w.shape
    gi = rows // n_groups

    def body(w_ref, o_ref):
        o_ref[...] = w_ref[...]

    return _pcall(body, name=name, out_shape=_sds((n_groups, gi, n_chips * goq), w.dtype), grid=(n_groups, n_chips),
                  in_specs=[pl.BlockSpec((None, gi, goq), lambda g, k: (k, g, 0))],
                  out_specs=pl.BlockSpec((None, gi, goq), lambda g, k: (g, 0, k)), vmem_mb=32)(w)


def _rms(h):
    r = lax.rsqrt(jnp.mean(h * h, axis=-1, keepdims=True) + EPS)
    return r, h * r


def _norm_mod(name, h, g, sc, sh, ts):
    s_len, d = h.shape

    def body(h_ref, g_ref, sc_ref, sh_ref, n_ref):
        _, xhat = _rms(h_ref[...])
        n_ref[...] = ((xhat * g_ref[...]) * (1.0 + sc_ref[...]) + sh_ref[...]).astype(BF16)

    row = pl.BlockSpec((ts, d), lambda i: (i, 0))
    return _pcall(body, name=name, out_shape=_sds((s_len, d), BF16), grid=(s_len // ts,),
                  in_specs=[row, _vec(d), _vec(d), _vec(d)], out_specs=row, vmem_mb=32)(h, g, sc, sh)


def _residual_norm_mod(name, h, f, gate, cmul, g, sc, sh, ts):
    s_len, d = h.shape

    def body(h_ref, f_ref, gt_ref, g_ref, sc_ref, sh_ref, ho_ref, n_ref):
        hn = h_ref[...] + (cmul * gt_ref[...]) * f_ref[...]
        ho_ref[...] = hn
        _, xhat = _rms(hn)
        n_ref[...] = ((xhat * g_ref[...]) * (1.0 + sc_ref[...]) + sh_ref[...]).astype(BF16)

    row = pl.BlockSpec((ts, d), lambda i: (i, 0))
    return _pcall(body, name=name, out_shape=[_sds((s_len, d), F32), _sds((s_len, d), BF16)],
                  grid=(s_len // ts,), in_specs=[row, row, _vec(d), _vec(d), _vec(d), _vec(d)],
                  out_specs=[row, row], vmem_mb=32)(h, f, gate, g, sc, sh)


def _final_loss(name, h, f, tgt, gate, cmul, g, ts):
    s_len, d = h.shape

    def body(h_ref, f_ref, t_ref, gt_ref, g_ref, dh_ref, df_ref, dg_ref, dgt_ref, loss_ref):
        i = pl.program_id(0)
        fv = f_ref[...]
        coef = cmul * gt_ref[...]
        hn = h_ref[...] + coef * fv
        r, xhat = _rms(hn)
        err = xhat * g_ref[...] - t_ref[...]
        _acc_rows(loss_ref, (0.5 / d) * (err * err), i)
        dy = err * (1.0 / d)
        _acc_rows(dg_ref, dy * xhat, i)
        dxhat = dy * g_ref[...]
        dh = r * (dxhat - xhat * jnp.mean(dxhat * xhat, axis=-1, keepdims=True))
        dh_ref[...] = dh
        _acc_rows(dgt_ref, cmul * (dh * fv), i)
        df_ref[...] = (coef * dh).astype(BF16)

    row = pl.BlockSpec((ts, d), lambda i: (i, 0))
    return _pcall(body, name=name,
                  out_shape=[_sds((s_len, d), F32), _sds((s_len, d), BF16)] + [_sds((1, d), F32)] * 3,
                  grid=(s_len // ts,), in_specs=[row, row, row, _vec(d), _vec(d)],
                  out_specs=[row, row, _vec(d), _vec(d), _vec(d)], vmem_mb=40)(h, f, tgt, gate, g)


def _norm_mod_bwd(name, h, dn, dh_next, g, sc, ts, prev=None):
    s_len, d = h.shape
    has_prev = prev is not None
    cmul = prev[2] if has_prev else None

    def body(*refs):
        if has_prev:
            h_ref, dn_ref, dhn_ref, f_ref, g_ref, sc_ref, gt_ref, dh_ref, df_ref, dsh_ref, dsc_ref, dg_ref, dgt_ref = refs
        else:
            h_ref, dn_ref, dhn_ref, g_ref, sc_ref, dh_ref, dsh_ref, dsc_ref, dg_ref = refs
        i = pl.program_id(0)
        r, xhat = _rms(h_ref[...])
        dn_v = dn_ref[...].astype(F32)
        gv = g_ref[...]
        _acc_rows(dsh_ref, dn_v, i)
        _acc_rows(dsc_ref, dn_v * (xhat * gv), i)
        dnrm = dn_v * (1.0 + sc_ref[...])
        _acc_rows(dg_ref, dnrm * xhat, i)
        dxhat = dnrm * gv
        dh = dhn_ref[...] + r * (dxhat - xhat * jnp.mean(dxhat * xhat, axis=-1, keepdims=True))
        dh_ref[...] = dh
        if has_prev:
            _acc_rows(dgt_ref, cmul * (dh * f_ref[...]), i)
            df_ref[...] = ((cmul * gt_ref[...]) * dh).astype(BF16)

    row = pl.BlockSpec((ts, d), lambda i: (i, 0))
    if has_prev:
        ins, in_specs = [h, dn, dh_next, prev[0], g, sc, prev[1]], [row, row, row, row, _vec(d), _vec(d), _vec(d)]
        out_shape = [_sds((s_len, d), F32), _sds((s_len, d), BF16)] + [_sds((1, d), F32)] * 4
        out_specs = [row, row] + [_vec(d)] * 4
    else:
        ins, in_specs = [h, dn, dh_next, g, sc], [row, row, row, _vec(d), _vec(d)]
        out_shape = [_sds((s_len, d), F32)] + [_sds((1, d), F32)] * 3
        out_specs = [row] + [_vec(d)] * 3
    return _pcall(body, name=name, out_shape=out_shape, grid=(s_len // ts,), in_specs=in_specs,
                  out_specs=out_specs, vmem_mb=40)(*ins)


def _cols(ref, lo, hi, npc, rows=slice(None)):
    parts = []
    while lo < hi:
        q, o = divmod(lo, npc)
        n = min(hi - lo, npc - o)
        parts.append(ref[q, rows, o:o + n].astype(F32))
        lo += n
    return parts[0] if len(parts) == 1 else jnp.concatenate(parts, axis=-1)


def _store_cols(ref, lo, val, npc, rows=slice(None)):
    off, width = 0, val.shape[-1]
    while off < width:
        q, o = divmod(lo + off, npc)
        n = min(width - off, npc - o)
        ref[q, rows, o:o + n] = val[:, off:off + n]
        off += n


def _chips_covering(cols, npc):
    return -(-cols // npc)


SUBLANES = 8
ROW_CHUNK = 32


def _make_phases(src_ref, ph_ref):
    rows = src_ref.shape[0] - SUBLANES
    for b in range(1, SUBLANES):
        ph_ref[b - 1] = src_ref[pl.ds(b, rows), :]


def _window(src_ref, ph_ref, off, r0, cols=slice(None)):
    a, b = divmod(off, SUBLANES)
    start = pl.multiple_of(r0 + SUBLANES * a, SUBLANES)
    if b == 0:
        return src_ref[pl.ds(start, ROW_CHUNK), cols]
    return ph_ref[b - 1, pl.ds(start, ROW_CHUNK), cols]


def _phase_scratch(rows, width):
    return pltpu.VMEM((SUBLANES - 1, rows - SUBLANES, width), F32)


def _conv(a0s_ref, a0p_ref, cw_ref, cb_ref, r0):
    a1 = cb_ref[...] + cw_ref[0:1, :] * _window(a0s_ref, a0p_ref, HALO - CONV_K + 1, r0)
    for k in range(1, CONV_K):
        a1 = a1 + cw_ref[k:k + 1, :] * _window(a0s_ref, a0p_ref, HALO - CONV_K + 1 + k, r0)
    return a1


def _layer_norm(a1, lg_ref, lb_ref):
    mu = jnp.mean(a1, axis=-1, keepdims=True)
    ctr = a1 - mu
    rstd = lax.rsqrt(jnp.mean(ctr * ctr, axis=-1, keepdims=True) + EPS)
    xh = ctr * rstd
    return xh, rstd, xh * lg_ref[...] + lb_ref[...]


def _for_chunks(ts, fn):
    def step(ci, carry):
        fn(pl.multiple_of(ci * ROW_CHUNK, ROW_CHUNK))
        return carry

    lax.fori_loop(0, ts // ROW_CHUNK, step, 0)


def _stage_glu(p_ref, ph_ref, a0s_ref, i, wc, npc, ts):
    a0 = _cols(p_ref, 0, wc, npc) * _sigmoid(_cols(p_ref, wc, 2 * wc, npc))
    a0h = _cols(ph_ref, 0, wc, npc) * _sigmoid(_cols(ph_ref, wc, 2 * wc, npc))
    a0s_ref[0:HALO, :] = jnp.where(i > 0, a0h, 0.0)
    a0s_ref[HALO:HALO + ts, :] = a0


def _mixer_mid(name, proj, cw, cb, lg, lb, wc, wp, ts):
    _, s_len, npc = proj.shape
    nq = _chips_covering(2 * wc + wp, npc)
    gi = wp // len(POOL_WINDOWS)
    hb = ts // HALO

    def body(p_ref, ph_ref, cw_ref, cb_ref, lg_ref, lb_ref, a3_ref, mx_ref, a1_ref, a0s_ref, vs_ref, a0p_ref,
             vp_ref):
        i = pl.program_id(0)
        _stage_glu(p_ref, ph_ref, a0s_ref, i, wc, npc, ts)
        vs_ref[0:HALO, :] = jnp.where(i > 0, _cols(ph_ref, 2 * wc, 2 * wc + wp, npc), 0.0)
        vs_ref[HALO:HALO + ts, :] = _cols(p_ref, 2 * wc, 2 * wc + wp, npc)
        _make_phases(a0s_ref, a0p_ref)
        _make_phases(vs_ref, vp_ref)

        def chunk(r0):
            rows = pl.ds(r0, ROW_CHUNK)
            a1 = _conv(a0s_ref, a0p_ref, cw_ref, cb_ref, r0)
            a1_ref[rows, :] = a1
            _, _, a2 = _layer_norm(a1, lg_ref, lb_ref)
            a3_ref[rows, :] = (a2 * _sigmoid(a2)).astype(BF16)
            t_abs = i * ts + r0 + lax.broadcasted_iota(jnp.int32, (ROW_CHUNK, 1), 0)
            for g, win in enumerate(POOL_WINDOWS):
                cs = slice(g * gi, (g + 1) * gi)
                v_now = _window(vs_ref, vp_ref, HALO, r0, cs)
                acc = v_now
                for dlt in range(1, win):
                    acc = acc + _window(vs_ref, vp_ref, HALO - dlt, r0, cs)
                cnt = jnp.minimum(t_abs + 1, win).astype(F32)
                mx_ref[rows, cs] = (acc / cnt - v_now).astype(BF16)

        _for_chunks(ts, chunk)

    return _pcall(
        body, name=name, out_shape=[_sds((s_len, wc), BF16), _sds((s_len, wp), BF16), _sds((s_len, wc), F32)],
        grid=(s_len // ts,),
        in_specs=[pl.BlockSpec((nq, ts, npc), lambda i: (0, i, 0)),
                  pl.BlockSpec((nq, HALO, npc), lambda i: (0, jnp.maximum(i * hb - 1, 0), 0)),
                  pl.BlockSpec((HALO, wc), lambda i: (0, 0)), _vec(wc), _vec(wc), _vec(wc)],
        out_specs=[pl.BlockSpec((ts, wc), lambda i: (i, 0)), pl.BlockSpec((ts, wp), lambda i: (i, 0)),
                   pl.BlockSpec((ts, wc), lambda i: (i, 0))],
        scratch=[pltpu.VMEM((HALO + ts, wc), F32), pltpu.VMEM((HALO + ts, wp), F32),
                 _phase_scratch(HALO + ts, wc), _phase_scratch(HALO + ts, wp)], vmem_mb=56,
    )(proj, proj, cw, cb, lg, lb)


def _gates_fwd(name, proj, ya, yb, b_a, b_b, ls, wc, wp, ts):
    _, s_len, npc = proj.shape
    d = ya.shape[1]
    g0 = 2 * wc + wp

    def body(p_ref, ya_ref, yb_ref, ba_ref, bb_ref, ls_ref, z_ref):
        ga = _sigmoid(_cols(p_ref, g0, g0 + d, npc))
        gb = _sigmoid(_cols(p_ref, g0 + d, g0 + 2 * d, npc))
        z = ga * (ya_ref[...] + ba_ref[...]) + gb * ((yb_ref[...] + bb_ref[...]) * ls_ref[...])
        z_ref[...] = z.astype(BF16)

    row = pl.BlockSpec((ts, d), lambda i: (i, 0))
    return _pcall(body, name=name, out_shape=_sds((s_len, d), BF16), grid=(s_len // ts,),
                  in_specs=[pl.BlockSpec((N_CHIPS, ts, npc), lambda i: (0, i, 0)), row, row, _vec(d), _vec(d), _vec(d)],
                  out_specs=row, vmem_mb=48)(proj, ya, yb, b_a, b_b, ls)


def _gates_bwd(name, proj, dz, ya, yb, b_a, b_b, ls, wc, wp, ts):
    _, s_len, npc = proj.shape
    d = ya.shape[1]
    g0 = 2 * wc + wp

    def body(p_ref, dz_ref, ya_ref, yb_ref, ba_ref, bb_ref, ls_ref, dya_ref, dyb_ref, dgt_ref, dba_ref, dls_ref,
             dbb_ref):
        i = pl.program_id(0)
        ga = _sigmoid(_cols(p_ref, g0, g0 + d, npc))
        gb = _sigmoid(_cols(p_ref, g0 + d, g0 + 2 * d, npc))
        dz_v = dz_ref[...].astype(F32)
        y_a = ya_ref[...] + ba_ref[...]
        y_b0 = yb_ref[...] + bb_ref[...]
        ls_v = ls_ref[...]
        dya = dz_v * ga
        dya_ref[...] = dya.astype(BF16)
        _acc_rows(dba_ref, dya, i)
        t = dz_v * gb
        _acc_rows(dls_ref, t * y_b0, i)
        dyb = t * ls_v
        dyb_ref[...] = dyb.astype(BF16)
        _acc_rows(dbb_ref, dyb, i)
        dgt_ref[:, 0:d] = (dz_v * y_a * ga * (1.0 - ga)).astype(BF16)
        dgt_ref[:, d:2 * d] = (dz_v * (y_b0 * ls_v) * gb * (1.0 - gb)).astype(BF16)

    row = pl.BlockSpec((ts, d), lambda i: (i, 0))
    return _pcall(
        body, name=name,
        out_shape=[_sds((s_len, d), BF16), _sds((s_len, d), BF16), _sds((s_len, 2 * d), BF16)] + [_sds((1, d), F32)] * 3,
        grid=(s_len // ts,),
        in_specs=[pl.BlockSpec((N_CHIPS, ts, npc), lambda i: (0, i, 0)), row, row, row, _vec(d), _vec(d), _vec(d)],
        out_specs=[row, row, pl.BlockSpec((ts, 2 * d), lambda i: (i, 0))] + [_vec(d)] * 3, vmem_mb=48,
    )(proj, dz, ya, yb, b_a, b_b, ls)


def _conv_branch_bwd(name, proj, a1, da3, lg, lb, wc, wp, ts):
    _, s_len, npc = proj.shape
    nq = _chips_covering(2 * wc, npc)
    hb = ts // HALO

    n_tiles = s_len // ts

    def fold(v):
        return jnp.sum(v.reshape(ROW_CHUNK // SUBLANES, SUBLANES, v.shape[-1]), axis=0)

    def body(p_ref, ph_ref, a1_ref, da3_ref, lg_ref, lb_ref, da1_ref, dlg_ref, dlb_ref, dcb_ref, dcw_ref,
             a0s_ref, a0p_ref, vec8_ref, dcw8_ref):
        i = pl.program_id(0)
        _stage_glu(p_ref, ph_ref, a0s_ref, i, wc, npc, ts)
        _make_phases(a0s_ref, a0p_ref)

        @pl.when(i == 0)
        def _():
            vec8_ref[...] = jnp.zeros_like(vec8_ref)
            dcw8_ref[...] = jnp.zeros_like(dcw8_ref)

        def chunk(r0):
            rows = pl.ds(r0, ROW_CHUNK)
            xh, rstd, a2 = _layer_norm(a1_ref[rows, :], lg_ref, lb_ref)
            sig = _sigmoid(a2)
            da2 = da3_ref[rows, :].astype(F32) * (sig * (1.0 + a2 * (1.0 - sig)))
            vec8_ref[0] += fold(da2 * xh)
            vec8_ref[1] += fold(da2)
            dxh = da2 * lg_ref[...]
            da1 = rstd * (dxh - jnp.mean(dxh, axis=-1, keepdims=True)
                          - xh * jnp.mean(dxh * xh, axis=-1, keepdims=True))
            da1_ref[rows, :] = da1
            vec8_ref[2] += fold(da1)
            for k in range(CONV_K):
                dcw8_ref[k] += fold(da1 * _window(a0s_ref, a0p_ref, HALO - CONV_K + 1 + k, r0))

        _for_chunks(ts, chunk)

        @pl.when(i == n_tiles - 1)
        def _():
            dlg_ref[...] = jnp.sum(vec8_ref[0], axis=0, keepdims=True)
            dlb_ref[...] = jnp.sum(vec8_ref[1], axis=0, keepdims=True)
            dcb_ref[...] = jnp.sum(vec8_ref[2], axis=0, keepdims=True)
            dcw_ref[...] = jnp.sum(dcw8_ref[...], axis=1)

    return _pcall(
        body, name=name,
        out_shape=[_sds((s_len, wc), F32)] + [_sds((1, wc), F32)] * 3 + [_sds((HALO, wc), F32)],
        grid=(s_len // ts,),
        in_specs=[pl.BlockSpec((nq, ts, npc), lambda i: (0, i, 0)),
                  pl.BlockSpec((nq, HALO, npc), lambda i: (0, jnp.maximum(i * hb - 1, 0), 0)),
                  pl.BlockSpec((ts, wc), lambda i: (i, 0)), pl.BlockSpec((ts, wc), lambda i: (i, 0)),
                  _vec(wc), _vec(wc)],
        out_specs=[pl.BlockSpec((ts, wc), lambda i: (i, 0)), _vec(wc), _vec(wc), _vec(wc),
                   pl.BlockSpec((HALO, wc), lambda i: (0, 0))],
        scratch=[pltpu.VMEM((HALO + ts, wc), F32), _phase_scratch(HALO + ts, wc),
                 pltpu.VMEM((3, SUBLANES, wc), F32), pltpu.VMEM((HALO, SUBLANES, wc), F32)], vmem_mb=56,
    )(proj, proj, a1, da3, lg, lb)


def _mixer_in_bwd(name, proj, da1, dmixed, dgates, cw, wc, wp, ts):
    _, s_len, npc = proj.shape
    nq = _chips_covering(2 * wc, npc)
    gi = wp // len(POOL_WINDOWS)
    hb = ts // HALO
    n_tiles = s_len // ts
    last_hb = s_len // HALO - 1
    d2 = dgates.shape[1]

    def body(p_ref, d1_ref, d1n_ref, dm_ref, dmn_ref, dgt_ref, cw_ref, o_ref, d1s_ref, es_ref, d1p_ref, ep_ref):
        i = pl.program_id(0)
        more = i < n_tiles - 1
        d1s_ref[0:ts, :] = d1_ref[...]
        d1s_ref[ts:ts + HALO, :] = jnp.where(more, d1n_ref[...], 0.0)
        t_abs = i * ts + lax.broadcasted_iota(jnp.int32, (ts + HALO, 1), 0)
        dm_ext = jnp.concatenate([dm_ref[...].astype(F32), jnp.where(more, dmn_ref[...].astype(F32), 0.0)], axis=0)
        for g, win in enumerate(POOL_WINDOWS):
            cs = slice(g * gi, (g + 1) * gi)
            es_ref[:, cs] = dm_ext[:, cs] / jnp.minimum(t_abs + 1, win).astype(F32)
        _make_phases(d1s_ref, d1p_ref)
        _make_phases(es_ref, ep_ref)

        def chunk(r0):
            rows = pl.ds(r0, ROW_CHUNK)
            da0 = cw_ref[0:1, :] * _window(d1s_ref, d1p_ref, CONV_K - 1, r0)
            for k in range(1, CONV_K):
                da0 = da0 + cw_ref[k:k + 1, :] * _window(d1s_ref, d1p_ref, CONV_K - 1 - k, r0)
            glu_a = _cols(p_ref, 0, wc, npc, rows)
            sig = _sigmoid(_cols(p_ref, wc, 2 * wc, npc, rows))
            _store_cols(o_ref, 0, (da0 * sig).astype(BF16), npc, rows)
            _store_cols(o_ref, wc, (da0 * glu_a * sig * (1.0 - sig)).astype(BF16), npc, rows)
            parts = []
            for g, win in enumerate(POOL_WINDOWS):
                cs = slice(g * gi, (g + 1) * gi)
                acc = _window(es_ref, ep_ref, 0, r0, cs)
                for dlt in range(1, win):
                    acc = acc + _window(es_ref, ep_ref, dlt, r0, cs)
                parts.append(acc - dm_ref[rows, cs].astype(F32))
            _store_cols(o_ref, 2 * wc, jnp.concatenate(parts, axis=-1).astype(BF16), npc, rows)

        _for_chunks(ts, chunk)
        _store_cols(o_ref, 2 * wc + wp, dgt_ref[...], npc)

    nxt = lambda i: (jnp.minimum((i + 1) * hb, last_hb), 0)
    return _pcall(
        body, name=name, out_shape=_sds((N_CHIPS, s_len, npc), BF16), grid=(n_tiles,),
        in_specs=[pl.BlockSpec((nq, ts, npc), lambda i: (0, i, 0)),
                  pl.BlockSpec((ts, wc), lambda i: (i, 0)), pl.BlockSpec((HALO, wc), nxt),
                  pl.BlockSpec((ts, wp), lambda i: (i, 0)), pl.BlockSpec((HALO, wp), nxt),
                  pl.BlockSpec((ts, d2), lambda i: (i, 0)),
                  pl.BlockSpec((HALO, wc), lambda i: (0, 0))],
        out_specs=pl.BlockSpec((N_CHIPS, ts, npc), lambda i: (0, i, 0)),
        scratch=[pltpu.VMEM((ts + HALO, wc), F32), pltpu.VMEM((ts + HALO, wp), F32),
                 _phase_scratch(ts + HALO, wc), _phase_scratch(ts + HALO, wp)], vmem_mb=56,
    )(proj, da1, da1, dmixed, dmixed, dgates, cw)


def _ada_fwd(name, c_all, w, b):
    d, cols = w.shape
    tn = 512 if cols % 512 == 0 else cols

    def body(c_ref, w_ref, b_ref, o_ref):
        cv = c_ref[...]
        sc = (cv * _sigmoid(cv)).astype(BF16)
        o_ref[...] = jnp.dot(sc, w_ref[...].astype(BF16), preferred_element_type=F32) + b_ref[...]

    return _pcall(body, name=name, out_shape=_sds((N_DEV, cols), F32), grid=(cols // tn,),
                  in_specs=[pl.BlockSpec((N_DEV, d), lambda j: (0, 0)), pl.BlockSpec((d, tn), lambda j: (0, j)),
                            pl.BlockSpec((1, tn), lambda j: (0, j))],
                  out_specs=pl.BlockSpec((N_DEV, tn), lambda j: (0, j)), vmem_mb=32)(c_all, w, b)


def _adam_math(w, g, m, v):
    m_new = ADAM_B1 * m + (1.0 - ADAM_B1) * g
    v_new = ADAM_B2 * v + (1.0 - ADAM_B2) * (g * g)
    m_hat = m_new / (1.0 - ADAM_B1 ** ADAM_STEP)
    v_hat = v_new / (1.0 - ADAM_B2 ** ADAM_STEP)
    delta = -ADAM_LR * (m_hat / (jnp.sqrt(v_hat) + ADAM_EPS) + ADAM_WD * w)
    return delta, m_new, v_new


def _adamw(name, w, g, m, v):
    rows, cols = w.shape
    tr = _row_tile(rows, cols, 524288)

    def body(w_ref, g_ref, m_ref, v_ref, go_ref, d_ref, mo_ref, vo_ref):
        g = g_ref[...]
        go_ref[...] = g
        d_ref[...], mo_ref[...], vo_ref[...] = _adam_math(w_ref[...], g, m_ref[...], v_ref[...])

    spec = pl.BlockSpec((tr, cols), lambda i: (i, 0))
    return _pcall(body, name=name, out_shape=[_sds(w.shape, F32)] * 4, grid=(rows // tr,), in_specs=[spec] * 4,
                  out_specs=[spec] * 4, vmem_mb=40)(w, g, m, v)


def _ada_grad_adamw(name, c_t, d_ada, w, m, v):
    rows, cols = w.shape
    tr = _tile(rows, 256)
    tc = _tile(cols, 1536) if cols % 1536 == 0 else cols

    def body(c_ref, da_ref, w_ref, m_ref, v_ref, g_ref, d_ref, mo_ref, vo_ref):
        cv = c_ref[...]
        sc = cv * _sigmoid(cv)
        g = sc[:, 0:1] * da_ref[0:1, :]
        for b in range(1, N_DEV):
            g = g + sc[:, b:b + 1] * da_ref[b:b + 1, :]
        g_ref[...] = g
        d_ref[...], mo_ref[...], vo_ref[...] = _adam_math(w_ref[...], g, m_ref[...], v_ref[...])

    spec = pl.BlockSpec((tr, tc), lambda i, j: (i, j))
    return _pcall(body, name=name, out_shape=[_sds(w.shape, F32)] * 4, grid=(rows // tr, cols // tc),
                  in_specs=[pl.BlockSpec((tr, N_DEV), lambda i, j: (i, 0)),
                            pl.BlockSpec((N_DEV, tc), lambda i, j: (0, j)), spec, spec, spec],
                  out_specs=[spec] * 4, vmem_mb=40)(c_t, d_ada, w, m, v)


def _sum_devices(name, gathered, m_per):
    n = gathered.shape[1]

    def body(g_ref, o_ref):
        acc = g_ref[0:m_per, :]
        for dev in range(1, N_DEV):
            acc = acc + g_ref[dev * m_per:(dev + 1) * m_per, :]
        o_ref[...] = acc

    return _pcall(body, name=name, out_shape=_sds((m_per, n), F32),
                  in_specs=[pl.BlockSpec(memory_space=pltpu.VMEM)],
                  out_specs=pl.BlockSpec(memory_space=pltpu.VMEM))(gathered)


def _ffn_fwd(tag, n, w_in_parts, w_out_after_swiglu, dims):
    s_len, d, f_dim = dims["S"], dims["D"], dims["F"]
    tf = f_dim // 4
    tm0, tm = _tile(s_len, 512), _tile(s_len, 1024)
    n_parts = len(w_in_parts)
    nbp = (f_dim // 2) // tf
    nbq = nbp // n_parts

    def ep(accs, ex, outs):
        hh, uu = accs
        sig = _sigmoid(hh)
        silu = hh * sig
        outs[0][0] = (uu * (sig + silu * (1.0 - sig))).astype(BF16)
        outs[0][1] = silu.astype(BF16)
        outs[1][...] = (silu * uu).astype(BF16)

    done = ()
    for part, get_w in enumerate(w_in_parts):
        w_g = get_w()
        col = lambda j, part=part: (j // nbq) * nbp + part * nbq + j % nbq
        done = _matmul(
            f"{tag}_swiglu{part}", n, [w_g, w_g], mode="nn", grid=(2 * nbq, s_len // tm0, 1),
            a_spec=pl.BlockSpec((tm0, d), lambda j, i, k: (i, 0)),
            b_specs=[pl.BlockSpec((None, d, tf), lambda j, i, k, part=part: (j // nbq, 0, part * nbq + j % nbq)),
                     pl.BlockSpec((None, d, tf), lambda j, i, k, part=part: (2 + j // nbq, 0, part * nbq + j % nbq))],
            out_shape=[_sds((2, s_len, f_dim), BF16), _sds((s_len, f_dim), BF16)],
            out_specs=[pl.BlockSpec((2, tm0, tf), lambda j, i, k, col=col: (0, i, col(j))),
                       pl.BlockSpec((tm0, tf), lambda j, i, k, col=col: (i, col(j)))],
            acc_shape=(tm0, tf), epilogue=ep, carry=done)
    hu, act = done
    w_out2d = w_out_after_swiglu()
    tn2 = _tile(d, 1024)
    f = _matmul(
        f"{tag}_down", act, [w_out2d], mode="nn", grid=(s_len // tm, d // tn2, 2),
        a_spec=pl.BlockSpec((tm, 2 * tf), lambda i, j, k: (i, k)),
        b_specs=[pl.BlockSpec((2 * tf, tn2), lambda i, j, k: (k, j))],
        out_shape=_sds((s_len, d), F32), out_specs=pl.BlockSpec((tm, tn2), lambda i, j, k: (i, j)),
        acc_shape=(tm, tn2), epilogue=_ep_store(F32))
    return hu, act, f, w_out2d


def _ffn_bwd(tag, n, hu, act, df, w_in_g, w_out2d, dims, after_dw_out, after_dw_in):
    s_len, d, f_dim = dims["S"], dims["D"], dims["F"]
    tf = f_dim // 4
    tk = _tile(s_len, 2048)
    tn = _tile(d, 1024)
    g_out = _matmul(
        f"{tag}_dw_out", act, [df], mode="tn", grid=(4, d // tn, s_len // tk),
        a_spec=pl.BlockSpec((tk, tf), lambda i, j, k: (k, i)),
        b_specs=[pl.BlockSpec((tk, tn), lambda i, j, k: (k, j))],
        out_shape=_sds((2, 4, tf // 2, d), F32),
        out_specs=pl.BlockSpec((2, None, tf // 2, tn), lambda i, j, k: (0, i, 0, j)),
        acc_shape=(tf, tn), epilogue=_ep_halves(tf // 2))
    after_dw_out(g_out)

    def ep_dhu(accs, ex, outs):
        da = accs[0]
        outs[0][0] = (da * ex[0][0].astype(F32)).astype(BF16)
        outs[0][1] = (da * ex[0][1].astype(F32)).astype(BF16)

    tm = _tile(s_len, 1024)
    hu_spec = pl.BlockSpec((2, tm, tf), lambda j, i, k: (0, i, j))
    dhu = _matmul(
        f"{tag}_dhu", df, [w_out2d], mode="nt", grid=(4, s_len // tm, 1),
        a_spec=pl.BlockSpec((tm, d), lambda j, i, k: (i, 0)),
        b_specs=[pl.BlockSpec((tf, d), lambda j, i, k: (j, 0))],
        extras=[hu], extra_specs=[hu_spec],
        out_shape=_sds((2, s_len, f_dim), BF16), out_specs=hu_spec, acc_shape=(tm, tf), epilogue=ep_dhu)

    hd = d // 2
    rt = hd // 2
    g_in = _matmul(
        f"{tag}_dw_in", n, [dhu], mode="tn", grid=(N_CHIPS, 4, s_len // tk),
        a_spec=pl.BlockSpec((tk, rt), lambda j, i, k: (k, i)),
        b_specs=[pl.BlockSpec((None, tk, 2 * tf), lambda j, i, k: (j // 2, k, j % 2))],
        out_shape=_sds((2, 4, hd, f_dim // 2), F32),
        out_specs=pl.BlockSpec((None, None, rt, 2 * tf), lambda j, i, k: (i // 2, j, i % 2, 0)),
        acc_shape=(rt, 2 * tf), epilogue=_ep_store(F32))
    after_dw_in(g_in)

    tm2 = _tile(s_len, 1024)
    dn = _matmul(
        f"{tag}_dn", dhu, [w_in_g], mode="nt", grid=(s_len // tm2, d // tn, N_CHIPS),
        a_spec=pl.BlockSpec((None, tm2, 2 * tf), lambda i, j, k: (k // 2, i, k % 2)),
        b_specs=[pl.BlockSpec((None, tn, 2 * tf), lambda i, j, k: (k, j, 0))],
        out_shape=_sds((s_len, d), BF16), out_specs=pl.BlockSpec((tm2, tn), lambda i, j, k: (i, j)),
        acc_shape=(tm2, tn), epilogue=_ep_store(BF16))
    return dn


def kernel(x, c, w_ada, b_ada, g_ffn1, w1_in, w1_out, g_mix, w_in, conv_w, conv_b, ln_a_g, ln_a_b, w_a_out, b_a_out, w_b_group, b_b_group, ls_b, w_out, g_ffn2, w2_in, w2_out, g_final, loss_target, m_w_ada, m_b_ada, m_g_ffn1, m_w1_in, m_w1_out, m_g_mix, m_w_in, m_conv_w, m_conv_b, m_ln_a_g, m_ln_a_b, m_w_a_out, m_b_a_out, m_w_b_group, m_b_b_group, m_ls_b, m_w_out, m_g_ffn2, m_w2_in, m_w2_out, m_g_final, v_w_ada, v_b_ada, v_g_ffn1, v_w1_in, v_w1_out, v_g_mix, v_w_in, v_conv_w, v_conv_b, v_ln_a_g, v_ln_a_b, v_w_a_out, v_b_a_out, v_w_b_group, v_b_b_group, v_ls_b, v_w_out, v_g_ffn2, v_w2_in, v_w2_out, v_g_final):
    weights = dict(w_ada=w_ada, b_ada=b_ada, g_ffn1=g_ffn1, w1_in=w1_in, w1_out=w1_out, g_mix=g_mix, w_in=w_in,
                   conv_w=conv_w, conv_b=conv_b, ln_a_g=ln_a_g, ln_a_b=ln_a_b, w_a_out=w_a_out, b_a_out=b_a_out,
                   w_b_group=w_b_group, b_b_group=b_b_group, ls_b=ls_b, w_out=w_out, g_ffn2=g_ffn2, w2_in=w2_in,
                   w2_out=w2_out, g_final=g_final)
    mom1 = dict(w_ada=m_w_ada, b_ada=m_b_ada, g_ffn1=m_g_ffn1, w1_in=m_w1_in, w1_out=m_w1_out, g_mix=m_g_mix,
                w_in=m_w_in, conv_w=m_conv_w, conv_b=m_conv_b, ln_a_g=m_ln_a_g, ln_a_b=m_ln_a_b, w_a_out=m_w_a_out,
                b_a_out=m_b_a_out, w_b_group=m_w_b_group, b_b_group=m_b_b_group, ls_b=m_ls_b, w_out=m_w_out,
                g_ffn2=m_g_ffn2, w2_in=m_w2_in, w2_out=m_w2_out, g_final=m_g_final)
    mom2 = dict(w_ada=v_w_ada, b_ada=v_b_ada, g_ffn1=v_g_ffn1, w1_in=v_w1_in, w1_out=v_w1_out, g_mix=v_g_mix,
                w_in=v_w_in, conv_w=v_conv_w, conv_b=v_conv_b, ln_a_g=v_ln_a_g, ln_a_b=v_ln_a_b, w_a_out=v_w_a_out,
                b_a_out=v_b_a_out, w_b_group=v_w_b_group, b_b_group=v_b_b_group, ls_b=v_ls_b, w_out=v_w_out,
                g_ffn2=v_g_ffn2, w2_in=v_w2_in, w2_out=v_w2_out, g_final=v_g_final)
    order = list(weights)

    s_len, d = x.shape[1], x.shape[2]
    f_dim = w1_out.shape[0] * N_CHIPS
    wc = conv_w.shape[1] * N_CHIPS
    wp = w_b_group.shape[0] * w_b_group.shape[1]
    n_groups, gi, goq = w_b_group.shape
    npc = w_in.shape[1]
    ada_c = w_ada.shape[1]
    dims = dict(S=s_len, D=d, F=f_dim)
    ts = _tile(s_len, 256)

    xi, yi, ci = lax.axis_index("x"), lax.axis_index("y"), lax.axis_index("c")
    q = 2 * xi + yi
    dev = 2 * q + ci
    q_idx = jnp.reshape(q, (1,)).astype(jnp.int32)
    qc_idx = jnp.stack([q, ci]).astype(jnp.int32)
    _PREVIOUS.clear()

    cwq = conv_w.shape[1]
    pack0 = jnp.concatenate([c.reshape(-1), conv_w.reshape(-1), b_b_group.reshape(-1)])
    n0 = -(-pack0.shape[0] // (8 * LANES)) * LANES
    pack0 = jnp.pad(pack0, (0, 8 * n0 - pack0.shape[0])).reshape(8, n0)
    g0 = _allgather_small("gather_small_in", pack0).reshape(N_DEV, 8 * n0)
    c_all = g0[:, :d]
    south = g0[0::2]
    cw_full = jnp.concatenate([south[k, d:d + CONV_K * cwq].reshape(CONV_K, cwq) for k in range(N_CHIPS)], axis=1)
    cw_pad = jnp.pad(cw_full, ((0, HALO - CONV_K), (0, 0)))
    o_bb = d + CONV_K * cwq
    bb_full = jnp.concatenate([south[k, o_bb:o_bb + n_groups * goq].reshape(n_groups, goq) for k in range(N_CHIPS)],
                              axis=1).reshape(1, d)

    as2d = lambda a: a.reshape(-1, a.shape[-1])
    groups = dict(w1_out=["w1_out"], mix=["w_a_out", "w_b_group", "w_out"], w2_in=["w2_in"], w2_out=["w2_out"])
    big = ["w1_in", "w1_out", "w_in", "w_a_out", "w_b_group", "w_out", "w2_in", "w2_out"]
    cast = lambda nm: _cast_into_gathered(f"cast_{nm}", as2d(weights[nm]), q_idx)
    w1_in_gather = _TwoPartGather("w1_in", cast("w1_in"))

    b_ada_mine = lax.dynamic_slice(b_ada, (q * ada_c,), (ada_c,)).reshape(1, ada_c)
    ada_piece = _ada_fwd("ada_fwd", c_all, w_ada, b_ada_mine)
    casts = {nm: cast(nm) for nm in big[1:]}
    g1 = _allgather_small("gather_ada", ada_piece).reshape(N_DEV, N_DEV, ada_c)
    w1_in_gather.start_second()
    ici = {}
    for grp, names in groups.items():
        ici[grp] = _gather_ici(f"gather_{grp}_ici", [casts[nm] for nm in names])
        if grp == "w1_out":
            w_in_gather = _TwoPartGather("w_in", casts["w_in"])
            w_in_gather.start_second()
    ada_rows = lax.dynamic_index_in_dim(g1[0::2], dev, axis=1, keepdims=False)
    ada = ada_rows.reshape(3, 3, 1, d)
    (sh1, sc1, gt1), (sh2, sc2, gt2), (sh3, sc3, gt3) = [[ada[i, j] for j in range(3)] for i in range(3)]

    row = lambda vct: vct.reshape(1, -1)
    g1v, gmv, g2v, gfv = row(g_ffn1), row(g_mix), row(g_ffn2), row(g_final)

    def arrived(grp):
        return _gather_d2d(f"gather_{grp}_d2d", ici[grp].wait())

    def gathered(fwd, grp):
        return {nm: g.reshape(N_CHIPS, 2 * g.shape[2], g.shape[3]) for nm, g in zip(groups[grp], fwd.wait())}

    x2 = x[0]
    tgt = loss_target[0]

    n1 = _norm_mod("ffn1_norm", x2, g1v, sc1, sh1, ts)
    fwd, w1_in_parts = {}, []

    def w1_in_part(part):
        def get():
            w1_in_gather.arrive(part)
            w1_in_parts.append(w1_in_gather.ready(part))
            return w1_in_parts[-1]
        return get

    def w1_out_after_swiglu():
        fwd["w1_out"] = arrived("w1_out")
        w_in_gather.arrive(0)
        return gathered(fwd["w1_out"], "w1_out")["w1_out"].reshape(f_dim, d)

    hu1, act1, f1, w1_out_2d = _ffn_fwd("ffn1", n1, [w1_in_part(0), w1_in_part(1)], w1_out_after_swiglu, dims)
    w1_in_g = w1_in_parts[-1]
    h1, n2 = _residual_norm_mod("mix_norm", x2, f1, gt1, 0.5, gmv, sc2, sh2, ts)

    tm = _tile(s_len, 1024)
    tnp = npc // 2
    proj = ()
    for part in range(2):
        if part:
            w_in_gather.arrive(part)
        w_in_g = w_in_gather.ready(part)
        proj = (_matmul(
            f"mix_proj{part}", n2, [w_in_g], mode="nn", grid=(s_len // tm, N_CHIPS, 1),
            a_spec=pl.BlockSpec((tm, d), lambda i, j, k: (i, 0)),
            b_specs=[pl.BlockSpec((None, d, tnp), lambda i, j, k, part=part: (j, 0, part))],
            out_shape=_sds((N_CHIPS, s_len, npc), BF16),
            out_specs=pl.BlockSpec((None, tm, tnp), lambda i, j, k, part=part: (j, i, part)),
            acc_shape=(tm, tnp), epilogue=_ep_store(BF16), carry=proj),)
    proj = proj[0]
    fwd["mix"] = arrived("mix")
    cbv, lgv, lbv = row(conv_b), row(ln_a_g), row(ln_a_b)
    a3, mixed, conv_out = _mixer_mid("mix_mid", proj, cw_pad, cbv, lgv, lbv, wc, wp, ts)
    wts = gathered(fwd["mix"], "mix")
    w_out_2d = wts["w_out"].reshape(d, d)
    w_a_g = wts["w_a_out"]
    w_b_r = _regroup("regroup_w_b", wts["w_b_group"], n_groups)
    dq = d // N_CHIPS
    ya = _matmul(
        "mix_ya", a3, [w_a_g], mode="nn", grid=(s_len // tm, N_CHIPS, 1),
        a_spec=pl.BlockSpec((tm, wc), lambda i, j, k: (i, 0)),
        b_specs=[pl.BlockSpec((None, wc, dq), lambda i, j, k: (j, 0, 0))],
        out_shape=_sds((s_len, d), BF16), out_specs=pl.BlockSpec((tm, dq), lambda i, j, k: (i, j)),
        acc_shape=(tm, dq), epilogue=_ep_store(BF16))
    yb = _matmul(
        "mix_yb", mixed, [w_b_r], mode="nn", grid=(s_len // tm, n_groups, 1),
        a_spec=pl.BlockSpec((tm, gi), lambda i, j, k: (i, j)),
        b_specs=[pl.BlockSpec((None, gi, dq), lambda i, j, k: (j, 0, 0))],
        out_shape=_sds((s_len, d), BF16), out_specs=pl.BlockSpec((tm, dq), lambda i, j, k: (i, j)),
        acc_shape=(tm, dq), epilogue=_ep_store(BF16))
    bav, lsv = row(b_a_out), row(ls_b)
    z = _gates_fwd("mix_gates", proj, ya, yb, bav, bb_full, lsv, wc, wp, ts)
    tn = _tile(d, 1024)
    mix = _matmul(
        "mix_out", z, [w_out_2d], mode="nn", grid=(s_len // tm, d // tn, 1),
        a_spec=pl.BlockSpec((tm, d), lambda i, j, k: (i, 0)),
        b_specs=[pl.BlockSpec((d, tn), lambda i, j, k: (0, j))],
        out_shape=_sds((s_len, d), F32), out_specs=pl.BlockSpec((tm, tn), lambda i, j, k: (i, j)),
        acc_shape=(tm, tn), epilogue=_ep_store(F32))
    fwd["w2_in"] = arrived("w2_in")
    h2, n3 = _residual_norm_mod("ffn2_norm", h1, mix, gt2, 1.0, g2v, sc3, sh3, ts)
    w2_in_g = gathered(fwd["w2_in"], "w2_in")["w2_in"]
    hu2, act2, f3, w2_out_2d = _ffn_fwd(
        "ffn2", n3, [lambda: w2_in_g],
        lambda: gathered(arrived("w2_out"), "w2_out")["w2_out"].reshape(f_dim, d), dims)

    dh3, df3, d_gf, d_gt3, loss_cols = _final_loss("final_loss", h2, f3, tgt, gt3, 0.5, gfv, ts)
    rs, held = {}, {}
    dn3 = _ffn_bwd(
        "ffn2", n3, hu2, act2, df3, w2_in_g, w2_out_2d, dims,
        after_dw_out=lambda g: held.update(w2_out=g),
        after_dw_in=lambda g: rs.update(ffn2=_ReduceScatter("g_ffn2", ["w2_out", "w2_in"], [held["w2_out"], g],
                                                            qc_idx)))
    dh2, dmix, d_sh3, d_sc3, d_g2, d_gt2 = _norm_mod_bwd("ffn2_norm_bwd", h2, dn3, dh3, g2v, sc3, ts,
                                                         prev=(mix, gt2, 1.0))
    rs["ffn2"].step2()

    tk = s_len
    hq = d // (2 * N_CHIPS)
    gw_out = _matmul(
        "mix_dw_out", z, [dmix], mode="tn", grid=(N_CHIPS, d // tn, s_len // tk),
        a_spec=pl.BlockSpec((tk, 2 * hq), lambda i, j, k: (k, i)),
        b_specs=[pl.BlockSpec((tk, tn), lambda i, j, k: (k, j))],
        out_shape=_sds((2, N_CHIPS, hq, d), F32),
        out_specs=pl.BlockSpec((2, None, hq, tn), lambda i, j, k: (0, i, 0, j)),
        acc_shape=(2 * hq, tn), epilogue=_ep_halves(hq))
    dz = _matmul(
        "mix_dz", dmix, [w_out_2d], mode="nt", grid=(s_len // tm, d // tn, 1),
        a_spec=pl.BlockSpec((tm, d), lambda i, j, k: (i, 0)),
        b_specs=[pl.BlockSpec((tn, d), lambda i, j, k: (j, 0))],
        out_shape=_sds((s_len, d), BF16), out_specs=pl.BlockSpec((tm, tn), lambda i, j, k: (i, j)),
        acc_shape=(tm, tn), epilogue=_ep_store(BF16))
    dya, dyb, dgates, d_ba, d_ls, d_bb = _gates_bwd("mix_gates_bwd", proj, dz, ya, yb, bav, bb_full, lsv, wc, wp, ts)
    gw_a = _matmul(
        "mix_dw_a", a3, [dya], mode="tn", grid=(1, N_CHIPS, s_len // tk),
        a_spec=pl.BlockSpec((tk, wc), lambda i, j, k: (k, 0)),
        b_specs=[pl.BlockSpec((tk, dq), lambda i, j, k: (k, j))],
        out_shape=_sds((2, N_CHIPS, wc // 2, dq), F32),
        out_specs=pl.BlockSpec((2, None, wc // 2, dq), lambda i, j, k: (0, j, 0, 0)),
        acc_shape=(wc, dq), epilogue=_ep_halves(wc // 2))
    da3 = _matmul(
        "mix_da3", dya, [w_a_g], mode="nt", grid=(s_len // tm, 1, N_CHIPS),
        a_spec=pl.BlockSpec((tm, dq), lambda i, j, k: (i, k)),
        b_specs=[pl.BlockSpec((None, wc, dq), lambda i, j, k: (k, 0, 0))],
        out_shape=_sds((s_len, wc), BF16), out_specs=pl.BlockSpec((tm, wc), lambda i, j, k: (i, 0)),
        acc_shape=(tm, wc), epilogue=_ep_store(BF16))
    gpr = n_groups // 2

    def ep_by_chip(accs, ex, outs):
        for k in range(N_CHIPS):
            outs[0][k] = accs[0][:, k * goq:(k + 1) * goq]

    gw_b = _matmul(
        "mix_dw_b", mixed, [dyb], mode="tn", grid=(1, n_groups, s_len // tk),
        a_spec=pl.BlockSpec((tk, gi), lambda i, j, k: (k, j)),
        b_specs=[pl.BlockSpec((tk, dq), lambda i, j, k: (k, j))],
        out_shape=_sds((2, N_CHIPS, gpr * gi, goq), F32),
        out_specs=pl.BlockSpec((None, N_CHIPS, gi, goq), lambda i, j, k: (j // gpr, 0, j % gpr, 0)),
        acc_shape=(gi, dq), epilogue=ep_by_chip)
    dmixed = _matmul(
        "mix_dmixed", dyb, [w_b_r], mode="nt", grid=(s_len // tm, n_groups, 1),
        a_spec=pl.BlockSpec((tm, dq), lambda i, j, k: (i, j)),
        b_specs=[pl.BlockSpec((None, gi, dq), lambda i, j, k: (j, 0, 0))],
        out_shape=_sds((s_len, wp), BF16), out_specs=pl.BlockSpec((tm, gi), lambda i, j, k: (i, j)),
        acc_shape=(tm, gi), epilogue=_ep_store(BF16))
    da1, d_lg, d_lb, d_cb, d_cw = _conv_branch_bwd("mix_conv_bwd", proj, conv_out, da3, lgv, lbv, wc, wp, ts)
    dproj = _mixer_in_bwd("mix_in_bwd", proj, da1, dmixed, dgates, cw_pad, wc, wp, ts)
    hd = d // 2
    rt = hd // 2
    gw_in = _matmul(
        "mix_dw_in", n2, [dproj], mode="tn", grid=(N_CHIPS, 4, 1),
        a_spec=pl.BlockSpec((s_len, rt), lambda j, i, k: (0, i)),
        b_specs=[pl.BlockSpec((None, s_len, npc), lambda j, i, k: (j, 0, 0))],
        out_shape=_sds((2, N_CHIPS, hd, npc), F32),
        out_specs=pl.BlockSpec((None, None, rt, npc), lambda j, i, k: (i // 2, j, i % 2, 0)),
        acc_shape=(rt, npc), epilogue=_ep_store(F32))
    rs["mix"] = _ReduceScatter("g_mix", ["w_in", "w_a_out", "w_b_group", "w_out"], [gw_in, gw_a, gw_b, gw_out],
                               qc_idx)
    rs["ffn2"].step3()
    dn2 = _matmul(
        "mix_dn", dproj, [w_in_g], mode="nt", grid=(s_len // tm, d // tn, N_CHIPS),
        a_spec=pl.BlockSpec((None, tm, npc), lambda i, j, k: (k, i, 0)),
        b_specs=[pl.BlockSpec((None, tn, npc), lambda i, j, k: (k, j, 0))],
        out_shape=_sds((s_len, d), BF16), out_specs=pl.BlockSpec((tm, tn), lambda i, j, k: (i, j)),
        acc_shape=(tm, tn), epilogue=_ep_store(BF16))
    dh1, df1, d_sh2, d_sc2, d_gm, d_gt1 = _norm_mod_bwd("mix_norm_bwd", h1, dn2, dh2, gmv, sc2, ts,
                                                        prev=(f1, gt1, 0.5))
    rs["mix"].step2()

    def w1_in_ready(g):
        rs["w1_in"] = _ReduceScatter("g_w1_in", ["w1_in"], [g], qc_idx)
        rs["w1_out"].step2()
        rs["mix"].step3()

    dn1 = _ffn_bwd(
        "ffn1", n1, hu1, act1, df1, w1_in_g, w1_out_2d, dims,
        after_dw_out=lambda g: rs.update(w1_out=_ReduceScatter("g_w1_out", ["w1_out"], [g], qc_idx)),
        after_dw_in=w1_in_ready)
    grad_x, d_sh1, d_sc1, d_g1 = _norm_mod_bwd("ffn1_norm_bwd", x2, dn1, dh1, g1v, sc1, ts)

    d_ada = jnp.concatenate([d_sh1, d_sc1, d_gt1, d_sh2, d_sc2, d_gt2, d_sh3, d_sc3, d_gt3], axis=1)
    small = [d_ada, d_g1, d_gm, d_cw[:CONV_K].reshape(1, -1), d_cb, d_lg, d_lb, d_ba, d_bb, d_ls, d_g2, d_gf,
             loss_cols]
    sizes = [a.shape[1] for a in small]
    pack1 = jnp.concatenate(small, axis=1).reshape(-1)
    n1p = -(-pack1.shape[0] // (8 * LANES)) * LANES
    pack1 = jnp.pad(pack1, (0, 8 * n1p - pack1.shape[0])).reshape(8, n1p)
    g2 = _allgather_small("gather_small_grads", pack1)
    rs["w1_in"].step2()
    total = _sum_devices("sum_small_grads", g2, 8).reshape(-1)
    offs = [0]
    for sz in sizes:
        offs.append(offs[-1] + sz)
    tot = [total[offs[k]:offs[k + 1]] for k in range(len(sizes))]
    d_ada_all = g2.reshape(N_DEV, 8 * n1p)[:, :sizes[0]]
    loss = jnp.sum(tot[12])

    grads = {}
    grads["b_ada"] = tot[0]
    grads["g_ffn1"], grads["g_mix"] = tot[1], tot[2]
    grads["conv_w"] = lax.dynamic_slice(tot[3].reshape(CONV_K, wc), (0, q * cwq), (CONV_K, cwq))
    grads["conv_b"], grads["ln_a_g"], grads["ln_a_b"], grads["b_a_out"] = tot[4], tot[5], tot[6], tot[7]
    grads["b_b_group"] = lax.dynamic_slice(tot[8].reshape(n_groups, N_CHIPS * goq), (0, q * goq), (n_groups, goq))
    grads["ls_b"], grads["g_ffn2"], grads["g_final"] = tot[9], tot[10], tot[11]

    delta, new_m, new_v = {}, {}, {}

    def adamw_group(reduced):
        for nm, g in reduced.items():
            shp = weights[nm].shape
            go, dl, mo, vo = _adamw(f"adamw_{nm}", as2d(weights[nm]), g, as2d(mom1[nm]), as2d(mom2[nm]))
            grads[nm], delta[nm], new_m[nm], new_v[nm] = go.reshape(shp), dl.reshape(shp), mo.reshape(shp), vo.reshape(shp)

    adamw_group(rs["ffn2"].result())
    rs["w1_out"].step3()
    adamw_group(rs["mix"].result())
    d_ada_mine = lax.dynamic_slice(d_ada_all, (0, q * ada_c), (N_DEV, ada_c))
    grads["w_ada"], delta["w_ada"], new_m["w_ada"], new_v["w_ada"] = _ada_grad_adamw(
        "adamw_w_ada", c_all.T, d_ada_mine, w_ada, m_w_ada, v_w_ada)
    rs["w1_in"].step3()
    smalls = [nm for nm in order if nm not in big and nm != "w_ada"]
    flat = lambda src: jnp.concatenate([src[nm].reshape(-1) for nm in smalls])
    n_small = sum(weights[nm].size for nm in smalls)
    rows_s = -(-n_small // (8 * LANES)) * 8
    packed = [jnp.pad(flat(src), (0, rows_s * LANES - n_small)).reshape(rows_s, LANES)
              for src in (weights, grads, mom1, mom2)]
    _, dl_s, mo_s, vo_s = _adamw("adamw_small", *packed)
    off = 0
    for nm in smalls:
        sz, shp = weights[nm].size, weights[nm].shape
        delta[nm] = dl_s.reshape(-1)[off:off + sz].reshape(shp)
        new_m[nm] = mo_s.reshape(-1)[off:off + sz].reshape(shp)
        new_v[nm] = vo_s.reshape(-1)[off:off + sz].reshape(shp)
        grads[nm] = grads[nm].reshape(shp)
        off += sz
    adamw_group(rs["w1_out"].result())
    adamw_group(rs["w1_in"].result())

    return (loss, grad_x[None], *[grads[nm] for nm in order], *[delta[nm] for nm in order],
            *[new_m[nm] for nm in order], *[new_v[nm] for nm in order])
```

```python
import jax
import jax.numpy as jnp
from jax import lax
from jax.experimental import pallas as pl
from jax.experimental.pallas import tpu as pltpu

F32 = jnp.float32
BF16 = jnp.bfloat16
MESH = pl.DeviceIdType.MESH
ANY = pl.BlockSpec(memory_space=pl.ANY)
HBM = pl.BlockSpec(memory_space=pltpu.HBM)
SEM = pl.BlockSpec(memory_space=pltpu.SEMAPHORE)
EFFECT = pltpu.SideEffectType.DATAFLOW_SIDE_EFFECTING

EPS = 1e-6
CONV_K = 31
HALO = 32
POOL_WINDOWS = (2, 4, 8, 16)
N_CHIPS = 4
N_DEV = 8
LANES = 128

ADAM_LR = 0.001
ADAM_B1 = 0.9
ADAM_B2 = 0.999
ADAM_EPS = 1e-08
ADAM_WD = 0.01
ADAM_STEP = 10

DN = {
    "nn": (((1,), (0,)), ((), ())),
    "nt": (((1,), (1,)), ((), ())),
    "tn": (((0,), (0,)), ((), ())),
}


_PREVIOUS = []


def _ordered(call, args, n_lead, body, token=None, sources=()):
    dep = [pltpu.with_memory_space_constraint(p, pltpu.HBM) if p.size * p.dtype.itemsize >= (1 << 20) else p
           for p in _PREVIOUS if all(p is not a for a in (*args, *sources))]

    def wrapped(*refs):
        return body(*refs[:n_lead], *refs[n_lead + len(dep):])

    outs = call(wrapped, [ANY] * len(dep))(*args, *dep)
    seq = outs if isinstance(outs, (list, tuple)) else [outs]
    _PREVIOUS[:] = [seq[token] if token is not None else
                    next(o for o in seq if jnp.issubdtype(o.dtype, jnp.floating))]
    return outs


def _pcall(body, *, name, out_shape, grid=None, in_specs=None, out_specs=None, scratch=(), aliases=None,
           prefetch=0, vmem_mb=None):
    params = {}
    if grid is not None:
        params["dimension_semantics"] = ("arbitrary",) * len(grid)
    if vmem_mb is not None:
        params["vmem_limit_bytes"] = vmem_mb << 20
    def in_hbm(shape, spec):
        big = shape.size * jnp.dtype(shape.dtype).itemsize >= (1 << 20)
        return pltpu.HBM(shape.shape, shape.dtype) if big and getattr(spec, "memory_space", None) != pltpu.VMEM else shape

    if isinstance(out_shape, (list, tuple)):
        out_shape = [in_hbm(s, sp) for s, sp in zip(out_shape, out_specs)]
    else:
        out_shape = in_hbm(out_shape, out_specs)
    kw = dict(name=name, out_shape=out_shape, compiler_params=pltpu.CompilerParams(**params))
    if aliases:
        kw["input_output_aliases"] = aliases

    def call(wrapped, dep_specs):
        specs = list(in_specs) + dep_specs
        if prefetch:
            return pl.pallas_call(wrapped, grid_spec=pltpu.PrefetchScalarGridSpec(
                num_scalar_prefetch=prefetch, grid=grid, in_specs=specs, out_specs=out_specs,
                scratch_shapes=list(scratch)), **kw)
        if grid is not None:
            return pl.pallas_call(wrapped, grid=grid, in_specs=specs, out_specs=out_specs,
                                  scratch_shapes=list(scratch), **kw)
        return pl.pallas_call(wrapped, in_specs=specs, out_specs=out_specs, scratch_shapes=list(scratch), **kw)

    def run(*args):
        specs = [None] * prefetch + list(in_specs)
        placed = [pltpu.with_memory_space_constraint(a, pltpu.HBM)
                  if a.size * a.dtype.itemsize >= (1 << 20) and getattr(s, "memory_space", None) != pltpu.VMEM else a
                  for a, s in zip(args, specs)]
        return _ordered(call, placed, prefetch + len(in_specs), body, sources=args)

    return run


def _tile(dim, pref):
    t = min(dim, pref)
    assert dim % t == 0, (dim, pref)
    return t


def _sds(shape, dtype):
    return jax.ShapeDtypeStruct(tuple(shape), dtype)


def _sigmoid(v):
    return 0.5 * jnp.tanh(0.5 * v) + 0.5


def _vec(w):
    return pl.BlockSpec((1, w), lambda *_: (0, 0))


def _acc_rows(ref, val, i):
    @pl.when(i == 0)
    def _():
        ref[...] = jnp.zeros_like(ref)

    ref[...] += jnp.sum(val, axis=0, keepdims=True)


def _matmul(name, a, bs, *, mode, grid, a_spec, b_specs, out_shape, out_specs, acc_shape, epilogue,
            extras=(), extra_specs=(), vmem_mb=56, carry=(), col_block=None):
    nb, ne, nk, nc = len(bs), len(extras), grid[2], len(carry)
    dn = DN[mode]

    def body(*all_refs):
        refs = all_refs[:1 + nb + ne] + all_refs[1 + nb + ne + nc:]
        a_ref, b_refs, ex = refs[0], refs[1:1 + nb], refs[1 + nb:1 + nb + ne]
        if col_block:
            outs, av, width = refs[1 + nb + ne:], a_ref[...], b_refs[0].shape[-1]
            for lo in range(0, width, col_block):
                cs = slice(lo, min(lo + col_block, width))
                epilogue([lax.dot_general(av, b[:, cs], dn, preferred_element_type=F32) for b in b_refs], ex, outs, cs)
            return
        if nk == 1:
            outs = refs[1 + nb + ne:]
            accs = [lax.dot_general(a_ref[...], b[...], dn, preferred_element_type=F32) for b in b_refs]
            epilogue(accs, ex, outs)
            return
        outs, acc_refs = refs[1 + nb + ne:-nb], refs[-nb:]
        k = pl.program_id(2)

        @pl.when(k == 0)
        def _():
            for acc in acc_refs:
                acc[...] = jnp.zeros_like(acc)

        for acc, b in zip(acc_refs, b_refs):
            acc[...] += lax.dot_general(a_ref[...], b[...], dn, preferred_element_type=F32)

        @pl.when(k == nk - 1)
        def _():
            epilogue([acc[...] for acc in acc_refs], ex, outs)

    scratch = [pltpu.VMEM(acc_shape, F32) for _ in range(nb)] if nk > 1 else []
    return _pcall(body, name=name, out_shape=out_shape, grid=grid,
                  in_specs=[a_spec, *b_specs, *extra_specs, *[ANY] * nc], out_specs=out_specs, scratch=scratch,
                  aliases={1 + nb + ne + i: i for i in range(nc)}, vmem_mb=vmem_mb)(a, *bs, *extras, *carry)


def _ep_store(dtype):
    def ep(accs, ex, outs):
        outs[0][...] = accs[0].astype(dtype)
    return ep


def _ep_halves(h):
    def ep(accs, ex, outs):
        outs[0][0] = accs[0][:h]
        outs[0][1] = accs[0][h:]
    return ep


def _place():
    x, y, c = lax.axis_index("x"), lax.axis_index("y"), lax.axis_index("c")
    chips = [(1 - x, y), (x, 1 - y), (1 - x, 1 - y)]
    return x, y, c, chips


def _allgather_small(name, block):
    m_per, n = block.shape

    def body(x_ref, out_ref, send_sems, recv_sems, local_sem):
        x, y, c, chips = _place()
        me, sibling = (x, y, c), (x, y, 1 - c)

        def rows(px, py, pc):
            return out_ref.at[pl.ds((4 * px + 2 * py + pc) * m_per, m_per), :]

        def copy(k, blk, to, src=None):
            return pltpu.make_async_remote_copy(
                src_ref=rows(*blk) if src is None else src, dst_ref=rows(*blk),
                send_sem=send_sems.at[k], recv_sem=recv_sems.at[k], device_id=to, device_id_type=MESH)

        mine = pltpu.make_async_copy(x_ref, rows(*me), local_sem)
        mine.start()
        first = [copy(0, me, sibling, src=x_ref)]
        first += [copy(1 + j, me, (*chip, c), src=x_ref) for j, chip in enumerate(chips)]
        for cp in first:
            cp.start()
        passed = [copy(4 + j, (*chip, c), sibling) for j, chip in enumerate(chips)]
        for j, chip in enumerate(chips):
            copy(1 + j, (*chip, c), me).wait_recv()
            passed[j].start()
        copy(0, sibling, me).wait_recv()
        for j, chip in enumerate(chips):
            copy(4 + j, (*chip, 1 - c), me).wait_recv()
        for cp in first + passed:
            cp.wait_send()
        mine.wait()

    return _pcall(
        body, name=name, out_shape=_sds((N_DEV * m_per, n), block.dtype),
        in_specs=[pl.BlockSpec(memory_space=pltpu.VMEM)], out_specs=pl.BlockSpec(memory_space=pltpu.VMEM),
        scratch=[pltpu.SemaphoreType.DMA((7,)), pltpu.SemaphoreType.DMA((7,)), pltpu.SemaphoreType.DMA],
    )(block)


class _SplitCopies:
    def __init__(self, name, arrays, plan, n_copies):
        self.name, self.plan, self.n = name, plan, len(arrays)
        n = self.n

        def body(*refs):
            send, recv, token = refs[n], refs[n + 1], refs[-1]
            for k, (src, dst, _, peer) in enumerate(plan(refs[:n])):
                pltpu.make_async_remote_copy(src_ref=src, dst_ref=dst, send_sem=send.at[k], recv_sem=recv.at[k],
                                             device_id=peer, device_id_type=MESH).start()
            token[...] = jnp.zeros_like(token)

        def call(wrapped, dep_specs):
            return pl.pallas_call(
                wrapped, name=f"{name}_start",
                out_shape=(pltpu.SemaphoreType.DMA((n_copies,)), pltpu.SemaphoreType.DMA((n_copies,)),
                           *[pltpu.HBM(a.shape, a.dtype) for a in arrays], _sds((8, LANES), F32)),
                in_specs=[HBM] * n + dep_specs,
                out_specs=(SEM, SEM, *[HBM] * n, pl.BlockSpec(memory_space=pltpu.VMEM)),
                input_output_aliases={i: 2 + i for i in range(n)},
                compiler_params=pltpu.CompilerParams(has_side_effects=EFFECT))

        outs = _ordered(call, [pltpu.with_memory_space_constraint(a, pltpu.HBM) for a in arrays], n, body, token=-1,
                        sources=arrays)
        self.send, self.recv, self.arrays = outs[0], outs[1], list(outs[2:2 + n])

    def wait(self, arrays=None):
        n, plan = self.n, self.plan
        if arrays is not None:
            self.arrays = list(arrays)

        def body(*refs):
            send, recv, token = refs[n], refs[n + 1], refs[-1]
            for k, (src, _, landing, peer) in enumerate(plan(refs[:n])):
                cp = pltpu.make_async_remote_copy(src_ref=src, dst_ref=landing, send_sem=send.at[k],
                                                  recv_sem=recv.at[k], device_id=peer, device_id_type=MESH)
                cp.wait_send()
                cp.wait_recv()
            token[...] = jnp.zeros_like(token)

        def call(wrapped, dep_specs):
            return pl.pallas_call(
                wrapped, name=f"{self.name}_wait",
                out_shape=(*[pltpu.HBM(a.shape, a.dtype) for a in self.arrays], _sds((8, LANES), F32)),
                in_specs=[HBM] * n + [SEM, SEM] + dep_specs,
                out_specs=(*[HBM] * n, pl.BlockSpec(memory_space=pltpu.VMEM)),
                input_output_aliases={i: i for i in range(n)},
                compiler_params=pltpu.CompilerParams(has_side_effects=EFFECT))

        return list(_ordered(call, [*self.arrays, self.send, self.recv], n + 2, body, token=-1))[:n]


def _col_range(g, part, n_parts):
    width = g.shape[-1] // n_parts
    return (slice(None), pl.ds(part * width, width))


def _gather_ici(name, gathered, part=0, n_parts=1):
    def plan(refs):
        x, y, c, chips = _place()
        q = 2 * x + y
        return [(g.at[(q, c, *_col_range(g, part, n_parts))], g.at[(q, c, *_col_range(g, part, n_parts))],
                 g.at[(2 * px + py, c, *_col_range(g, part, n_parts))], (px, py, c))
                for g in refs for px, py in chips]

    return _SplitCopies(name, gathered, plan, 3 * len(gathered))


def _gather_d2d(name, gathered, part=0, n_parts=1):
    def plan(refs):
        x, y, c, chips = _place()
        return [(g.at[(2 * px + py, c, *_col_range(g, part, n_parts))],
                 g.at[(2 * px + py, c, *_col_range(g, part, n_parts))],
                 g.at[(2 * px + py, 1 - c, *_col_range(g, part, n_parts))], (x, y, 1 - c))
                for g in refs for px, py in chips]

    return _SplitCopies(name, gathered, plan, 3 * len(gathered))


class _TwoPartGather:
    def __init__(self, name, gathered):
        self.name, self.d2d = name, {}
        self.ici = [_gather_ici(f"gather_{name}_a_ici", [gathered], 0, 2)]
        self.buf = self.ici[0].arrays

    def start_second(self):
        self.ici.append(_gather_ici(f"gather_{self.name}_b_ici", self.buf, 1, 2))
        self.buf = self.ici[1].arrays

    def arrive(self, part):
        here = self.ici[part].wait(self.buf)
        self.d2d[part] = _gather_d2d(f"gather_{self.name}_{'ab'[part]}_d2d", here, part, 2)
        self.buf = self.d2d[part].arrays

    def ready(self, part):
        self.buf = self.d2d[part].wait(self.buf)
        g = self.buf[0]
        return g.reshape(N_CHIPS, 2 * g.shape[2], g.shape[3])


def _scatter_sibling(name, grads):
    n = len(grads)

    def plan(refs):
        x, y, c, _ = _place()
        return [(refs[w].at[1 - c], refs[n + w], refs[n + w], (x, y, 1 - c)) for w in range(n)]

    landing = [lax.empty(g.shape[1:], g.dtype) for g in grads]
    return _SplitCopies(name, [*grads, *landing], plan, n)


def _scatter_chips(name, sums):
    n = len(sums)

    def plan(refs):
        x, y, c, chips = _place()
        return [(refs[w].at[2 * px + py], refs[n + w].at[j], refs[n + w].at[j], (px, py, c))
                for w in range(n) for j, (px, py) in enumerate(chips)]

    landing = [lax.empty((3, *s.shape[1:]), s.dtype) for s in sums]
    return _SplitCopies(name, [*sums, *landing], plan, 3 * n)


def _share_final(name, finals):
    def plan(refs):
        x, y, c, _ = _place()
        return [(f.at[c], f.at[c], f.at[1 - c], (x, y, 1 - c)) for f in refs]

    return _SplitCopies(name, finals, plan, len(finals))


def _row_tile(rows, cols, budget_elems=786432):
    best = 8
    for t in range(8, rows + 1, 8):
        if rows % t == 0 and t * cols <= budget_elems:
            best = t
    return best if rows % best == 0 else rows


def _sum_with_sibling(name, grad, recv, qc_idx):
    _, _, h, cols = grad.shape
    tr = _row_tile(h, cols)

    def body(s_ref, g_ref, r_ref, own_ref, pb_ref):
        p = g_ref[...] + r_ref[...]
        pb_ref[...] = p.astype(BF16)

        @pl.when(pl.program_id(1) == s_ref[0])
        def _():
            own_ref[...] = p

    blk = pl.BlockSpec((None, tr, cols), lambda r, k, s: (k, r, 0))
    return _pcall(
        body, name=name, out_shape=[_sds((h, cols), F32), _sds((N_CHIPS, h, cols), BF16)],
        grid=(h // tr, N_CHIPS), prefetch=1,
        in_specs=[pl.BlockSpec((None, None, tr, cols), lambda r, k, s: (s[1], k, r, 0)), blk],
        out_specs=[pl.BlockSpec((tr, cols), lambda r, k, s: (r, 0)), blk], vmem_mb=32,
    )(qc_idx, grad, recv)


def _sum_chips(name, own, recv, qc_idx):
    h, cols = own.shape
    tr = _row_tile(h, cols)

    def body(s_ref, p_ref, t_ref, o_ref):
        o_ref[...] = ((p_ref[...] + t_ref[0].astype(F32)) + t_ref[1].astype(F32)) + t_ref[2].astype(F32)

    return _pcall(
        body, name=name, out_shape=_sds((2, h, cols), F32), grid=(h // tr,), prefetch=1,
        in_specs=[pl.BlockSpec((tr, cols), lambda r, s: (r, 0)),
                  pl.BlockSpec((3, tr, cols), lambda r, s: (0, r, 0))],
        out_specs=pl.BlockSpec((None, tr, cols), lambda r, s: (s[1], r, 0)), vmem_mb=32,
    )(qc_idx, own, recv)


class _ReduceScatter:
    def __init__(self, tag, names, grads, qc_idx):
        self.tag, self.names, self.n, self.qc_idx = tag, names, len(grads), qc_idx
        self.copies = _scatter_sibling(f"{tag}_rs_sibling", grads)

    def step2(self):
        n = self.n
        arrs = self.copies.wait()
        sums = [_sum_with_sibling(f"{nm}_sum_sibling", arrs[w], arrs[n + w], self.qc_idx)
                for w, nm in enumerate(self.names)]
        self.own = [s[0] for s in sums]
        self.copies = _scatter_chips(f"{self.tag}_rs_chips", [s[1] for s in sums])

    def step3(self):
        n = self.n
        arrs = self.copies.wait()
        finals = [_sum_chips(f"{nm}_sum_chips", self.own[w], arrs[n + w], self.qc_idx)
                  for w, nm in enumerate(self.names)]
        self.copies = _share_final(f"{self.tag}_rs_final", finals)

    def result(self):
        return {nm: f.reshape(2 * f.shape[1], f.shape[2]) for nm, f in zip(self.names, self.copies.wait())}


def _cast_into_gathered(name, w, q_idx):
    rows, cols = w.shape
    h = rows // 2
    tr = _row_tile(h, cols, 1 << 20)
    nr = h // tr

    def body(s_ref, w_ref, o_ref):
        o_ref[...] = w_ref[...].astype(BF16)

    return _pcall(body, name=name, out_shape=_sds((N_CHIPS, 2, h, cols), BF16), grid=(2, nr), prefetch=1,
                  in_specs=[pl.BlockSpec((tr, cols), lambda hf, r, s: (hf * nr + r, 0))],
                  out_specs=pl.BlockSpec((None, None, tr, cols), lambda hf, r, s: (s[0], hf, r, 0)),
                  vmem_mb=32)(q_idx, w)


def _regroup(name, w, n_groups):
    n_chips, rows, goq = w.shape
    gi = rows // n_groups

    def body(w_ref, o_ref):
        o_ref[...] = w_ref[...]

    return _pcall(body, name=name, out_shape=_sds((n_groups, gi, n_chips * goq), w.dtype), grid=(n_groups, n_chips),
                  in_specs=[pl.BlockSpec((None, gi, goq), lambda g, k: (k, g, 0))],
                  out_specs=pl.BlockSpec((None, gi, goq), lambda g, k: (g, 0, k)), vmem_mb=32)(w)


def _rms(h):
    r = lax.rsqrt(jnp.mean(h * h, axis=-1, keepdims=True) + EPS)
    return r, h * r


def _norm_mod(name, h, g, sc, sh, ts):
    s_len, d = h.shape

    def body(h_ref, g_ref, sc_ref, sh_ref, n_ref):
        _, xhat = _rms(h_ref[...])
        n_ref[...] = ((xhat * g_ref[...]) * (1.0 + sc_ref[...]) + sh_ref[...]).astype(BF16)

    row = pl.BlockSpec((ts, d), lambda i: (i, 0))
    return _pcall(body, name=name, out_shape=_sds((s_len, d), BF16), grid=(s_len // ts,),
                  in_specs=[row, _vec(d), _vec(d), _vec(d)], out_specs=row, vmem_mb=32)(h, g, sc, sh)


def _residual_norm_mod(name, h, f, gate, cmul, g, sc, sh, ts):
    s_len, d = h.shape

    def body(h_ref, f_ref, gt_ref, g_ref, sc_ref, sh_ref, ho_ref, n_ref):
        hn = h_ref[...] + (cmul * gt_ref[...]) * f_ref[...]
        ho_ref[...] = hn
        _, xhat = _rms(hn)
        n_ref[...] = ((xhat * g_ref[...]) * (1.0 + sc_ref[...]) + sh_ref[...]).astype(BF16)

    row = pl.BlockSpec((ts, d), lambda i: (i, 0))
    return _pcall(body, name=name, out_shape=[_sds((s_len, d), F32), _sds((s_len, d), BF16)],
                  grid=(s_len // ts,), in_specs=[row, row, _vec(d), _vec(d), _vec(d), _vec(d)],
                  out_specs=[row, row], vmem_mb=32)(h, f, gate, g, sc, sh)


def _final_loss(name, h, f, tgt, gate, cmul, g, ts):
    s_len, d = h.shape

    def body(h_ref, f_ref, t_ref, gt_ref, g_ref, dh_ref, df_ref, dg_ref, dgt_ref, loss_ref):
        i = pl.program_id(0)
        fv = f_ref[...]
        coef = cmul * gt_ref[...]
        hn = h_ref[...] + coef * fv
        r, xhat = _rms(hn)
        err = xhat * g_ref[...] - t_ref[...]
        _acc_rows(loss_ref, (0.5 / d) * (err * err), i)
        dy = err * (1.0 / d)
        _acc_rows(dg_ref, dy * xhat, i)
        dxhat = dy * g_ref[...]
        dh = r * (dxhat - xhat * jnp.mean(dxhat * xhat, axis=-1, keepdims=True))
        dh_ref[...] = dh
        _acc_rows(dgt_ref, cmul * (dh * fv), i)
        df_ref[...] = (coef * dh).astype(BF16)

    row = pl.BlockSpec((ts, d), lambda i: (i, 0))
    return _pcall(body, name=name,
                  out_shape=[_sds((s_len, d), F32), _sds((s_len, d), BF16)] + [_sds((1, d), F32)] * 3,
                  grid=(s_len // ts,), in_specs=[row, row, row, _vec(d), _vec(d)],
                  out_specs=[row, row, _vec(d), _vec(d), _vec(d)], vmem_mb=40)(h, f, tgt, gate, g)


def _norm_mod_bwd(name, h, dn, dh_next, g, sc, ts, prev=None):
    s_len, d = h.shape
    has_prev = prev is not None
    cmul = prev[2] if has_prev else None

    def body(*refs):
        if has_prev:
            h_ref, dn_ref, dhn_ref, f_ref, g_ref, sc_ref, gt_ref, dh_ref, df_ref, dsh_ref, dsc_ref, dg_ref, dgt_ref = refs
        else:
            h_ref, dn_ref, dhn_ref, g_ref, sc_ref, dh_ref, dsh_ref, dsc_ref, dg_ref = refs
        i = pl.program_id(0)
        r, xhat = _rms(h_ref[...])
        dn_v = dn_ref[...].astype(F32)
        gv = g_ref[...]
        _acc_rows(dsh_ref, dn_v, i)
        _acc_rows(dsc_ref, dn_v * (xhat * gv), i)
        dnrm = dn_v * (1.0 + sc_ref[...])
        _acc_rows(dg_ref, dnrm * xhat, i)
        dxhat = dnrm * gv
        dh = dhn_ref[...] + r * (dxhat - xhat * jnp.mean(dxhat * xhat, axis=-1, keepdims=True))
        dh_ref[...] = dh
        if has_prev:
            _acc_rows(dgt_ref, cmul * (dh * f_ref[...]), i)
            df_ref[...] = ((cmul * gt_ref[...]) * dh).astype(BF16)

    row = pl.BlockSpec((ts, d), lambda i: (i, 0))
    if has_prev:
        ins, in_specs = [h, dn, dh_next, prev[0], g, sc, prev[1]], [row, row, row, row, _vec(d), _vec(d), _vec(d)]
        out_shape = [_sds((s_len, d), F32), _sds((s_len, d), BF16)] + [_sds((1, d), F32)] * 4
        out_specs = [row, row] + [_vec(d)] * 4
    else:
        ins, in_specs = [h, dn, dh_next, g, sc], [row, row, row, _vec(d), _vec(d)]
        out_shape = [_sds((s_len, d), F32)] + [_sds((1, d), F32)] * 3
        out_specs = [row] + [_vec(d)] * 3
    return _pcall(body, name=name, out_shape=out_shape, grid=(s_len // ts,), in_specs=in_specs,
                  out_specs=out_specs, vmem_mb=40)(*ins)


def _cols(ref, lo, hi, npc, rows=slice(None)):
    parts = []
    while lo < hi:
        q, o = divmod(lo, npc)
        n = min(hi - lo, npc - o)
        parts.append(ref[q, rows, o:o + n].astype(F32))
        lo += n
    return parts[0] if len(parts) == 1 else jnp.concatenate(parts, axis=-1)


def _store_cols(ref, lo, val, npc, rows=slice(None)):
    off, width = 0, val.shape[-1]
    while off < width:
        q, o = divmod(lo + off, npc)
        n = min(width - off, npc - o)
        ref[q, rows, o:o + n] = val[:, off:off + n]
        off += n


def _chips_covering(cols, npc):
    return -(-cols // npc)


SUBLANES = 8
ROW_CHUNK = 32


def _make_phases(src_ref, ph_ref):
    rows = src_ref.shape[0] - SUBLANES
    for b in range(1, SUBLANES):
        ph_ref[b - 1] = src_ref[pl.ds(b, rows), :]


def _window(src_ref, ph_ref, off, r0, cols=slice(None)):
    a, b = divmod(off, SUBLANES)
    start = pl.multiple_of(r0 + SUBLANES * a, SUBLANES)
    if b == 0:
        return src_ref[pl.ds(start, ROW_CHUNK), cols]
    return ph_ref[b - 1, pl.ds(start, ROW_CHUNK), cols]


def _phase_scratch(rows, width):
    return pltpu.VMEM((SUBLANES - 1, rows - SUBLANES, width), F32)


def _conv(a0s_ref, a0p_ref, cw_ref, cb_ref, r0):
    a1 = cb_ref[...] + cw_ref[0:1, :] * _window(a0s_ref, a0p_ref, HALO - CONV_K + 1, r0)
    for k in range(1, CONV_K):
        a1 = a1 + cw_ref[k:k + 1, :] * _window(a0s_ref, a0p_ref, HALO - CONV_K + 1 + k, r0)
    return a1


def _layer_norm(a1, lg_ref, lb_ref):
    mu = jnp.mean(a1, axis=-1, keepdims=True)
    ctr = a1 - mu
    rstd = lax.rsqrt(jnp.mean(ctr * ctr, axis=-1, keepdims=True) + EPS)
    xh = ctr * rstd
    return xh, rstd, xh * lg_ref[...] + lb_ref[...]


def _for_chunks(ts, fn):
    def step(ci, carry):
        fn(pl.multiple_of(ci * ROW_CHUNK, ROW_CHUNK))
        return carry

    lax.fori_loop(0, ts // ROW_CHUNK, step, 0)


def _stage_glu(p_ref, ph_ref, a0s_ref, i, wc, npc, ts):
    a0 = _cols(p_ref, 0, wc, npc) * _sigmoid(_cols(p_ref, wc, 2 * wc, npc))
    a0h = _cols(ph_ref, 0, wc, npc) * _sigmoid(_cols(ph_ref, wc, 2 * wc, npc))
    a0s_ref[0:HALO, :] = jnp.where(i > 0, a0h, 0.0)
    a0s_ref[HALO:HALO + ts, :] = a0


def _mixer_mid(name, proj, cw, cb, lg, lb, wc, wp, ts):
    _, s_len, npc = proj.shape
    nq = _chips_covering(2 * wc + wp, npc)
    gi = wp // len(POOL_WINDOWS)
    hb = ts // HALO

    def body(p_ref, ph_ref, cw_ref, cb_ref, lg_ref, lb_ref, a3_ref, mx_ref, a1_ref, a0s_ref, vs_ref, a0p_ref,
             vp_ref):
        i = pl.program_id(0)
        _stage_glu(p_ref, ph_ref, a0s_ref, i, wc, npc, ts)
        vs_ref[0:HALO, :] = jnp.where(i > 0, _cols(ph_ref, 2 * wc, 2 * wc + wp, npc), 0.0)
        vs_ref[HALO:HALO + ts, :] = _cols(p_ref, 2 * wc, 2 * wc + wp, npc)
        _make_phases(a0s_ref, a0p_ref)
        _make_phases(vs_ref, vp_ref)

        def chunk(r0):
            rows = pl.ds(r0, ROW_CHUNK)
            a1 = _conv(a0s_ref, a0p_ref, cw_ref, cb_ref, r0)
            a1_ref[rows, :] = a1
            _, _, a2 = _layer_norm(a1, lg_ref, lb_ref)
            a3_ref[rows, :] = (a2 * _sigmoid(a2)).astype(BF16)
            t_abs = i * ts + r0 + lax.broadcasted_iota(jnp.int32, (ROW_CHUNK, 1), 0)
            for g, win in enumerate(POOL_WINDOWS):
                cs = slice(g * gi, (g + 1) * gi)
                v_now = _window(vs_ref, vp_ref, HALO, r0, cs)
                acc = v_now
                for dlt in range(1, win):
                    acc = acc + _window(vs_ref, vp_ref, HALO - dlt, r0, cs)
                cnt = jnp.minimum(t_abs + 1, win).astype(F32)
                mx_ref[rows, cs] = (acc / cnt - v_now).astype(BF16)

        _for_chunks(ts, chunk)

    return _pcall(
        body, name=name, out_shape=[_sds((s_len, wc), BF16), _sds((s_len, wp), BF16), _sds((s_len, wc), F32)],
        grid=(s_len // ts,),
        in_specs=[pl.BlockSpec((nq, ts, npc), lambda i: (0, i, 0)),
                  pl.BlockSpec((nq, HALO, npc), lambda i: (0, jnp.maximum(i * hb - 1, 0), 0)),
                  pl.BlockSpec((HALO, wc), lambda i: (0, 0)), _vec(wc), _vec(wc), _vec(wc)],
        out_specs=[pl.BlockSpec((ts, wc), lambda i: (i, 0)), pl.BlockSpec((ts, wp), lambda i: (i, 0)),
                   pl.BlockSpec((ts, wc), lambda i: (i, 0))],
        scratch=[pltpu.VMEM((HALO + ts, wc), F32), pltpu.VMEM((HALO + ts, wp), F32),
                 _phase_scratch(HALO + ts, wc), _phase_scratch(HALO + ts, wp)], vmem_mb=56,
    )(proj, proj, cw, cb, lg, lb)


def _gates_fwd(name, proj, ya, yb, b_a, b_b, ls, wc, wp, ts):
    _, s_len, npc = proj.shape
    d = ya.shape[1]
    g0 = 2 * wc + wp

    def body(p_ref, ya_ref, yb_ref, ba_ref, bb_ref, ls_ref, z_ref):
        ga = _sigmoid(_cols(p_ref, g0, g0 + d, npc))
        gb = _sigmoid(_cols(p_ref, g0 + d, g0 + 2 * d, npc))
        z = ga * (ya_ref[...] + ba_ref[...]) + gb * ((yb_ref[...] + bb_ref[...]) * ls_ref[...])
        z_ref[...] = z.astype(BF16)

    row = pl.BlockSpec((ts, d), lambda i: (i, 0))
    return _pcall(body, name=name, out_shape=_sds((s_len, d), BF16), grid=(s_len // ts,),
                  in_specs=[pl.BlockSpec((N_CHIPS, ts, npc), lambda i: (0, i, 0)), row, row, _vec(d), _vec(d), _vec(d)],
                  out_specs=row, vmem_mb=48)(proj, ya, yb, b_a, b_b, ls)


def _gates_bwd(name, proj, dz, ya, yb, b_a, b_b, ls, wc, wp, ts):
    _, s_len, npc = proj.shape
    d = ya.shape[1]
    g0 = 2 * wc + wp

    def body(p_ref, dz_ref, ya_ref, yb_ref, ba_ref, bb_ref, ls_ref, dya_ref, dyb_ref, dgt_ref, dba_ref, dls_ref,
             dbb_ref):
        i = pl.program_id(0)
        ga = _sigmoid(_cols(p_ref, g0, g0 + d, npc))
        gb = _sigmoid(_cols(p_ref, g0 + d, g0 + 2 * d, npc))
        dz_v = dz_ref[...].astype(F32)
        y_a = ya_ref[...] + ba_ref[...]
        y_b0 = yb_ref[...] + bb_ref[...]
        ls_v = ls_ref[...]
        dya = dz_v * ga
        dya_ref[...] = dya.astype(BF16)
        _acc_rows(dba_ref, dya, i)
        t = dz_v * gb
        _acc_rows(dls_ref, t * y_b0, i)
        dyb = t * ls_v
        dyb_ref[...] = dyb.astype(BF16)
        _acc_rows(dbb_ref, dyb, i)
        dgt_ref[:, 0:d] = (dz_v * y_a * ga * (1.0 - ga)).astype(BF16)
        dgt_ref[:, d:2 * d] = (dz_v * (y_b0 * ls_v) * gb * (1.0 - gb)).astype(BF16)

    row = pl.BlockSpec((ts, d), lambda i: (i, 0))
    return _pcall(
        body, name=name,
        out_shape=[_sds((s_len, d), BF16), _sds((s_len, d), BF16), _sds((s_len, 2 * d), BF16)] + [_sds((1, d), F32)] * 3,
        grid=(s_len // ts,),
        in_specs=[pl.BlockSpec((N_CHIPS, ts, npc), lambda i: (0, i, 0)), row, row, row, _vec(d), _vec(d), _vec(d)],
        out_specs=[row, row, pl.BlockSpec((ts, 2 * d), lambda i: (i, 0))] + [_vec(d)] * 3, vmem_mb=48,
    )(proj, dz, ya, yb, b_a, b_b, ls)


def _conv_branch_bwd(name, proj, a1, da3, lg, lb, wc, wp, ts):
    _, s_len, npc = proj.shape
    nq = _chips_covering(2 * wc, npc)
    hb = ts // HALO

    n_tiles = s_len // ts

    def fold(v):
        return jnp.sum(v.reshape(ROW_CHUNK // SUBLANES, SUBLANES, v.shape[-1]), axis=0)

    def body(p_ref, ph_ref, a1_ref, da3_ref, lg_ref, lb_ref, da1_ref, dlg_ref, dlb_ref, dcb_ref, dcw_ref,
             a0s_ref, a0p_ref, vec8_ref, dcw8_ref):
        i = pl.program_id(0)
        _stage_glu(p_ref, ph_ref, a0s_ref, i, wc, npc, ts)
        _make_phases(a0s_ref, a0p_ref)

        @pl.when(i == 0)
        def _():
            vec8_ref[...] = jnp.zeros_like(vec8_ref)
            dcw8_ref[...] = jnp.zeros_like(dcw8_ref)

        def chunk(r0):
            rows = pl.ds(r0, ROW_CHUNK)
            xh, rstd, a2 = _layer_norm(a1_ref[rows, :], lg_ref, lb_ref)
            sig = _sigmoid(a2)
            da2 = da3_ref[rows, :].astype(F32) * (sig * (1.0 + a2 * (1.0 - sig)))
            vec8_ref[0] += fold(da2 * xh)
            vec8_ref[1] += fold(da2)
            dxh = da2 * lg_ref[...]
            da1 = rstd * (dxh - jnp.mean(dxh, axis=-1, keepdims=True)
                          - xh * jnp.mean(dxh * xh, axis=-1, keepdims=True))
            da1_ref[rows, :] = da1
            vec8_ref[2] += fold(da1)
            for k in range(CONV_K):
                dcw8_ref[k] += fold(da1 * _window(a0s_ref, a0p_ref, HALO - CONV_K + 1 + k, r0))

        _for_chunks(ts, chunk)

        @pl.when(i == n_tiles - 1)
        def _():
            dlg_ref[...] = jnp.sum(vec8_ref[0], axis=0, keepdims=True)
            dlb_ref[...] = jnp.sum(vec8_ref[1], axis=0, keepdims=True)
            dcb_ref[...] = jnp.sum(vec8_ref[2], axis=0, keepdims=True)
            dcw_ref[...] = jnp.sum(dcw8_ref[...], axis=1)

    return _pcall(
        body, name=name,
        out_shape=[_sds((s_len, wc), F32)] + [_sds((1, wc), F32)] * 3 + [_sds((HALO, wc), F32)],
        grid=(s_len // ts,),
        in_specs=[pl.BlockSpec((nq, ts, npc), lambda i: (0, i, 0)),
                  pl.BlockSpec((nq, HALO, npc), lambda i: (0, jnp.maximum(i * hb - 1, 0), 0)),
                  pl.BlockSpec((ts, wc), lambda i: (i, 0)), pl.BlockSpec((ts, wc), lambda i: (i, 0)),
                  _vec(wc), _vec(wc)],
        out_specs=[pl.BlockSpec((ts, wc), lambda i: (i, 0)), _vec(wc), _vec(wc), _vec(wc),
                   pl.BlockSpec((HALO, wc), lambda i: (0, 0))],
        scratch=[pltpu.VMEM((HALO + ts, wc), F32), _phase_scratch(HALO + ts, wc),
                 pltpu.VMEM((3, SUBLANES, wc), F32), pltpu.VMEM((HALO, SUBLANES, wc), F32)], vmem_mb=56,
    )(proj, proj, a1, da3, lg, lb)


def _mixer_in_bwd(name, proj, da1, dmixed, dgates, cw, wc, wp, ts):
    _, s_len, npc = proj.shape
    nq = _chips_covering(2 * wc, npc)
    gi = wp // len(POOL_WINDOWS)
    hb = ts // HALO
    n_tiles = s_len // ts
    last_hb = s_len // HALO - 1
    d2 = dgates.shape[1]

    def body(p_ref, d1_ref, d1n_ref, dm_ref, dmn_ref, dgt_ref, cw_ref, o_ref, d1s_ref, es_ref, d1p_ref, ep_ref):
        i = pl.program_id(0)
        more = i < n_tiles - 1
        d1s_ref[0:ts, :] = d1_ref[...]
        d1s_ref[ts:ts + HALO, :] = jnp.where(more, d1n_ref[...], 0.0)
        t_abs = i * ts + lax.broadcasted_iota(jnp.int32, (ts + HALO, 1), 0)
        dm_ext = jnp.concatenate([dm_ref[...].astype(F32), jnp.where(more, dmn_ref[...].astype(F32), 0.0)], axis=0)
        for g, win in enumerate(POOL_WINDOWS):
            cs = slice(g * gi, (g + 1) * gi)
            es_ref[:, cs] = dm_ext[:, cs] / jnp.minimum(t_abs + 1, win).astype(F32)
        _make_phases(d1s_ref, d1p_ref)
        _make_phases(es_ref, ep_ref)

        def chunk(r0):
            rows = pl.ds(r0, ROW_CHUNK)
            da0 = cw_ref[0:1, :] * _window(d1s_ref, d1p_ref, CONV_K - 1, r0)
            for k in range(1, CONV_K):
                da0 = da0 + cw_ref[k:k + 1, :] * _window(d1s_ref, d1p_ref, CONV_K - 1 - k, r0)
            glu_a = _cols(p_ref, 0, wc, npc, rows)
            sig = _sigmoid(_cols(p_ref, wc, 2 * wc, npc, rows))
            _store_cols(o_ref, 0, (da0 * sig).astype(BF16), npc, rows)
            _store_cols(o_ref, wc, (da0 * glu_a * sig * (1.0 - sig)).astype(BF16), npc, rows)
            parts = []
            for g, win in enumerate(POOL_WINDOWS):
                cs = slice(g * gi, (g + 1) * gi)
                acc = _window(es_ref, ep_ref, 0, r0, cs)
                for dlt in range(1, win):
                    acc = acc + _window(es_ref, ep_ref, dlt, r0, cs)
                parts.append(acc - dm_ref[rows, cs].astype(F32))
            _store_cols(o_ref, 2 * wc, jnp.concatenate(parts, axis=-1).astype(BF16), npc, rows)

        _for_chunks(ts, chunk)
        _store_cols(o_ref, 2 * wc + wp, dgt_ref[...], npc)

    nxt = lambda i: (jnp.minimum((i + 1) * hb, last_hb), 0)
    return _pcall(
        body, name=name, out_shape=_sds((N_CHIPS, s_len, npc), BF16), grid=(n_tiles,),
        in_specs=[pl.BlockSpec((nq, ts, npc), lambda i: (0, i, 0)),
                  pl.BlockSpec((ts, wc), lambda i: (i, 0)), pl.BlockSpec((HALO, wc), nxt),
                  pl.BlockSpec((ts, wp), lambda i: (i, 0)), pl.BlockSpec((HALO, wp), nxt),
                  pl.BlockSpec((ts, d2), lambda i: (i, 0)),
                  pl.BlockSpec((HALO, wc), lambda i: (0, 0))],
        out_specs=pl.BlockSpec((N_CHIPS, ts, npc), lambda i: (0, i, 0)),
        scratch=[pltpu.VMEM((ts + HALO, wc), F32), pltpu.VMEM((ts + HALO, wp), F32),
                 _phase_scratch(ts + HALO, wc), _phase_scratch(ts + HALO, wp)], vmem_mb=56,
    )(proj, da1, da1, dmixed, dmixed, dgates, cw)


def _ada_fwd(name, c_all, w, b):
    d, cols = w.shape
    tn = 512 if cols % 512 == 0 else cols

    def body(c_ref, w_ref, b_ref, o_ref):
        cv = c_ref[...]
        sc = (cv * _sigmoid(cv)).astype(BF16)
        o_ref[...] = jnp.dot(sc, w_ref[...].astype(BF16), preferred_element_type=F32) + b_ref[...]

    return _pcall(body, name=name, out_shape=_sds((N_DEV, cols), F32), grid=(cols // tn,),
                  in_specs=[pl.BlockSpec((N_DEV, d), lambda j: (0, 0)), pl.BlockSpec((d, tn), lambda j: (0, j)),
                            pl.BlockSpec((1, tn), lambda j: (0, j))],
                  out_specs=pl.BlockSpec((N_DEV, tn), lambda j: (0, j)), vmem_mb=32)(c_all, w, b)


def _adam_math(w, g, m, v):
    m_new = ADAM_B1 * m + (1.0 - ADAM_B1) * g
    v_new = ADAM_B2 * v + (1.0 - ADAM_B2) * (g * g)
    m_hat = m_new / (1.0 - ADAM_B1 ** ADAM_STEP)
    v_hat = v_new / (1.0 - ADAM_B2 ** ADAM_STEP)
    delta = -ADAM_LR * (m_hat / (jnp.sqrt(v_hat) + ADAM_EPS) + ADAM_WD * w)
    return delta, m_new, v_new


def _adamw(name, w, g, m, v):
    rows, cols = w.shape
    tr = _row_tile(rows, cols, 524288)

    def body(w_ref, g_ref, m_ref, v_ref, go_ref, d_ref, mo_ref, vo_ref):
        g = g_ref[...]
        go_ref[...] = g
        d_ref[...], mo_ref[...], vo_ref[...] = _adam_math(w_ref[...], g, m_ref[...], v_ref[...])

    spec = pl.BlockSpec((tr, cols), lambda i: (i, 0))
    return _pcall(body, name=name, out_shape=[_sds(w.shape, F32)] * 4, grid=(rows // tr,), in_specs=[spec] * 4,
                  out_specs=[spec] * 4, vmem_mb=40)(w, g, m, v)


def _ada_grad_adamw(name, c_t, d_ada, w, m, v):
    rows, cols = w.shape
    tr = _tile(rows, 256)
    tc = _tile(cols, 1536) if cols % 1536 == 0 else cols

    def body(c_ref, da_ref, w_ref, m_ref, v_ref, g_ref, d_ref, mo_ref, vo_ref):
        cv = c_ref[...]
        sc = cv * _sigmoid(cv)
        g = sc[:, 0:1] * da_ref[0:1, :]
        for b in range(1, N_DEV):
            g = g + sc[:, b:b + 1] * da_ref[b:b + 1, :]
        g_ref[...] = g
        d_ref[...], mo_ref[...], vo_ref[...] = _adam_math(w_ref[...], g, m_ref[...], v_ref[...])

    spec = pl.BlockSpec((tr, tc), lambda i, j: (i, j))
    return _pcall(body, name=name, out_shape=[_sds(w.shape, F32)] * 4, grid=(rows // tr, cols // tc),
                  in_specs=[pl.BlockSpec((tr, N_DEV), lambda i, j: (i, 0)),
                            pl.BlockSpec((N_DEV, tc), lambda i, j: (0, j)), spec, spec, spec],
                  out_specs=[spec] * 4, vmem_mb=40)(c_t, d_ada, w, m, v)


def _sum_devices(name, gathered, m_per):
    n = gathered.shape[1]

    def body(g_ref, o_ref):
        acc = g_ref[0:m_per, :]
        for dev in range(1, N_DEV):
            acc = acc + g_ref[dev * m_per:(dev + 1) * m_per, :]
        o_ref[...] = acc

    return _pcall(body, name=name, out_shape=_sds((m_per, n), F32),
                  in_specs=[pl.BlockSpec(memory_space=pltpu.VMEM)],
                  out_specs=pl.BlockSpec(memory_space=pltpu.VMEM))(gathered)


def _ffn_fwd(tag, n, w_in_parts, w_out_after_swiglu, dims):
    s_len, d, f_dim = dims["S"], dims["D"], dims["F"]
    tf = f_dim // 4
    tm0, tm = _tile(s_len, 512), _tile(s_len, 1024)
    n_parts = len(w_in_parts)
    whole = n_parts == 1
    tw = 2 * tf if whole else tf
    nbp = (f_dim // 2) // tw
    nbq = nbp // n_parts
    mode_kw = dict(pipeline_mode=pl.Buffered(1)) if whole else {}

    def ep(accs, ex, outs, cs=slice(None)):
        hh, uu = accs
        sig = _sigmoid(hh)
        silu = hh * sig
        outs[0][0, :, cs] = (uu * (sig + silu * (1.0 - sig))).astype(BF16)
        outs[0][1, :, cs] = silu.astype(BF16)
        outs[1][:, cs] = (silu * uu).astype(BF16)

    done = ()
    for part, get_w in enumerate(w_in_parts):
        w_g = get_w()
        col = lambda j, part=part: (j // nbq) * nbp + part * nbq + j % nbq
        done = _matmul(
            f"{tag}_swiglu{part}", n, [w_g, w_g], mode="nn", grid=(2 * nbq, s_len // tm0, 1),
            a_spec=pl.BlockSpec((tm0, d), lambda j, i, k: (i, 0)),
            b_specs=[pl.BlockSpec((None, d, tw), lambda j, i, k, part=part: (j // nbq, 0, part * nbq + j % nbq),
                                  **mode_kw),
                     pl.BlockSpec((None, d, tw), lambda j, i, k, part=part: (2 + j // nbq, 0, part * nbq + j % nbq),
                                  **mode_kw)],
            out_shape=[_sds((2, s_len, f_dim), BF16), _sds((s_len, f_dim), BF16)],
            out_specs=[pl.BlockSpec((2, tm0, tw), lambda j, i, k, col=col: (0, i, col(j))),
                       pl.BlockSpec((tm0, tw), lambda j, i, k, col=col: (i, col(j)))],
            acc_shape=(tm0, tw), epilogue=ep, carry=done, col_block=512 if whole else None)
    hu, act = done
    w_out2d = w_out_after_swiglu()
    tn2 = _tile(d, 1024)
    f = _matmul(
        f"{tag}_down", act, [w_out2d], mode="nn", grid=(s_len // tm, d // tn2, 2),
        a_spec=pl.BlockSpec((tm, 2 * tf), lambda i, j, k: (i, k)),
        b_specs=[pl.BlockSpec((2 * tf, tn2), lambda i, j, k: (k, j))],
        out_shape=_sds((s_len, d), F32), out_specs=pl.BlockSpec((tm, tn2), lambda i, j, k: (i, j)),
        acc_shape=(tm, tn2), epilogue=_ep_store(F32))
    return hu, act, f, w_out2d


def _ffn_bwd(tag, n, hu, act, df, w_in_g, w_out2d, dims, after_dw_out, after_dw_in):
    s_len, d, f_dim = dims["S"], dims["D"], dims["F"]
    tf = f_dim // 4
    tk = _tile(s_len, 2048)
    tn = _tile(d, 1024)
    g_out = _matmul(
        f"{tag}_dw_out", act, [df], mode="tn", grid=(4, d // tn, s_len // tk),
        a_spec=pl.BlockSpec((tk, tf), lambda i, j, k: (k, i)),
        b_specs=[pl.BlockSpec((tk, tn), lambda i, j, k: (k, j))],
        out_shape=_sds((2, 4, tf // 2, d), F32),
        out_specs=pl.BlockSpec((2, None, tf // 2, tn), lambda i, j, k: (0, i, 0, j)),
        acc_shape=(tf, tn), epilogue=_ep_halves(tf // 2))
    after_dw_out(g_out)

    def ep_dhu(accs, ex, outs):
        da = accs[0]
        outs[0][0] = (da * ex[0][0].astype(F32)).astype(BF16)
        outs[0][1] = (da * ex[0][1].astype(F32)).astype(BF16)

    tm = _tile(s_len, 512)
    hu_spec = pl.BlockSpec((2, tm, 2 * tf), lambda j, i, k: (0, i, j))
    dhu = _matmul(
        f"{tag}_dhu", df, [w_out2d], mode="nt", grid=(2, s_len // tm, 1),
        a_spec=pl.BlockSpec((tm, d), lambda j, i, k: (i, 0)),
        b_specs=[pl.BlockSpec((2 * tf, d), lambda j, i, k: (j, 0), pipeline_mode=pl.Buffered(1))],
        extras=[hu], extra_specs=[hu_spec],
        out_shape=_sds((2, s_len, f_dim), BF16), out_specs=hu_spec, acc_shape=(tm, 2 * tf), epilogue=ep_dhu)

    hd = d // 2
    rt = hd // 2
    g_in = _matmul(
        f"{tag}_dw_in", n, [dhu], mode="tn", grid=(N_CHIPS, 4, s_len // tk),
        a_spec=pl.BlockSpec((tk, rt), lambda j, i, k: (k, i)),
        b_specs=[pl.BlockSpec((None, tk, 2 * tf), lambda j, i, k: (j // 2, k, j % 2))],
        out_shape=_sds((2, 4, hd, f_dim // 2), F32),
        out_specs=pl.BlockSpec((None, None, rt, 2 * tf), lambda j, i, k: (i // 2, j, i % 2, 0)),
        acc_shape=(rt, 2 * tf), epilogue=_ep_store(F32))
    after_dw_in(g_in)

    tm2 = _tile(s_len, 1024)
    dn = _matmul(
        f"{tag}_dn", dhu, [w_in_g], mode="nt", grid=(s_len // tm2, d // tn, N_CHIPS),
        a_spec=pl.BlockSpec((None, tm2, 2 * tf), lambda i, j, k: (k // 2, i, k % 2)),
        b_specs=[pl.BlockSpec((None, tn, 2 * tf), lambda i, j, k: (k, j, 0))],
        out_shape=_sds((s_len, d), BF16), out_specs=pl.BlockSpec((tm2, tn), lambda i, j, k: (i, j)),
        acc_shape=(tm2, tn), epilogue=_ep_store(BF16))
    return dn


def kernel(x, c, w_ada, b_ada, g_ffn1, w1_in, w1_out, g_mix, w_in, conv_w, conv_b, ln_a_g, ln_a_b, w_a_out, b_a_out, w_b_group, b_b_group, ls_b, w_out, g_ffn2, w2_in, w2_out, g_final, loss_target, m_w_ada, m_b_ada, m_g_ffn1, m_w1_in, m_w1_out, m_g_mix, m_w_in, m_conv_w, m_conv_b, m_ln_a_g, m_ln_a_b, m_w_a_out, m_b_a_out, m_w_b_group, m_b_b_group, m_ls_b, m_w_out, m_g_ffn2, m_w2_in, m_w2_out, m_g_final, v_w_ada, v_b_ada, v_g_ffn1, v_w1_in, v_w1_out, v_g_mix, v_w_in, v_conv_w, v_conv_b, v_ln_a_g, v_ln_a_b, v_w_a_out, v_b_a_out, v_w_b_group, v_b_b_group, v_ls_b, v_w_out, v_g_ffn2, v_w2_in, v_w2_out, v_g_final):
    weights = dict(w_ada=w_ada, b_ada=b_ada, g_ffn1=g_ffn1, w1_in=w1_in, w1_out=w1_out, g_mix=g_mix, w_in=w_in,
                   conv_w=conv_w, conv_b=conv_b, ln_a_g=ln_a_g, ln_a_b=ln_a_b, w_a_out=w_a_out, b_a_out=b_a_out,
                   w_b_group=w_b_group, b_b_group=b_b_group, ls_b=ls_b, w_out=w_out, g_ffn2=g_ffn2, w2_in=w2_in,
                   w2_out=w2_out, g_final=g_final)
    mom1 = dict(w_ada=m_w_ada, b_ada=m_b_ada, g_ffn1=m_g_ffn1, w1_in=m_w1_in, w1_out=m_w1_out, g_mix=m_g_mix,
                w_in=m_w_in, conv_w=m_conv_w, conv_b=m_conv_b, ln_a_g=m_ln_a_g, ln_a_b=m_ln_a_b, w_a_out=m_w_a_out,
                b_a_out=m_b_a_out, w_b_group=m_w_b_group, b_b_group=m_b_b_group, ls_b=m_ls_b, w_out=m_w_out,
                g_ffn2=m_g_ffn2, w2_in=m_w2_in, w2_out=m_w2_out, g_final=m_g_final)
    mom2 = dict(w_ada=v_w_ada, b_ada=v_b_ada, g_ffn1=v_g_ffn1, w1_in=v_w1_in, w1_out=v_w1_out, g_mix=v_g_mix,
                w_in=v_w_in, conv_w=v_conv_w, conv_b=v_conv_b, ln_a_g=v_ln_a_g, ln_a_b=v_ln_a_b, w_a_out=v_w_a_out,
                b_a_out=v_b_a_out, w_b_group=v_w_b_group, b_b_group=v_b_b_group, ls_b=v_ls_b, w_out=v_w_out,
                g_ffn2=v_g_ffn2, w2_in=v_w2_in, w2_out=v_w2_out, g_final=v_g_final)
    order = list(weights)

    s_len, d = x.shape[1], x.shape[2]
    f_dim = w1_out.shape[0] * N_CHIPS
    wc = conv_w.shape[1] * N_CHIPS
    wp = w_b_group.shape[0] * w_b_group.shape[1]
    n_groups, gi, goq = w_b_group.shape
    npc = w_in.shape[1]
    ada_c = w_ada.shape[1]
    dims = dict(S=s_len, D=d, F=f_dim)
    ts = _tile(s_len, 256)

    xi, yi, ci = lax.axis_index("x"), lax.axis_index("y"), lax.axis_index("c")
    q = 2 * xi + yi
    dev = 2 * q + ci
    q_idx = jnp.reshape(q, (1,)).astype(jnp.int32)
    qc_idx = jnp.stack([q, ci]).astype(jnp.int32)
    _PREVIOUS.clear()

    cwq = conv_w.shape[1]
    pack0 = jnp.concatenate([c.reshape(-1), conv_w.reshape(-1), b_b_group.reshape(-1)])
    n0 = -(-pack0.shape[0] // (8 * LANES)) * LANES
    pack0 = jnp.pad(pack0, (0, 8 * n0 - pack0.shape[0])).reshape(8, n0)
    g0 = _allgather_small("gather_small_in", pack0).reshape(N_DEV, 8 * n0)
    c_all = g0[:, :d]
    south = g0[0::2]
    cw_full = jnp.concatenate([south[k, d:d + CONV_K * cwq].reshape(CONV_K, cwq) for k in range(N_CHIPS)], axis=1)
    cw_pad = jnp.pad(cw_full, ((0, HALO - CONV_K), (0, 0)))
    o_bb = d + CONV_K * cwq
    bb_full = jnp.concatenate([south[k, o_bb:o_bb + n_groups * goq].reshape(n_groups, goq) for k in range(N_CHIPS)],
                              axis=1).reshape(1, d)

    as2d = lambda a: a.reshape(-1, a.shape[-1])
    groups = dict(w1_out=["w1_out"], mix=["w_a_out", "w_b_group", "w_out"], w2_in=["w2_in"], w2_out=["w2_out"])
    big = ["w1_in", "w1_out", "w_in", "w_a_out", "w_b_group", "w_out", "w2_in", "w2_out"]
    cast = lambda nm: _cast_into_gathered(f"cast_{nm}", as2d(weights[nm]), q_idx)
    w1_in_gather = _TwoPartGather("w1_in", cast("w1_in"))

    b_ada_mine = lax.dynamic_slice(b_ada, (q * ada_c,), (ada_c,)).reshape(1, ada_c)
    ada_piece = _ada_fwd("ada_fwd", c_all, w_ada, b_ada_mine)
    casts = {nm: cast(nm) for nm in big[1:]}
    g1 = _allgather_small("gather_ada", ada_piece).reshape(N_DEV, N_DEV, ada_c)
    w1_in_gather.start_second()
    ici = {}
    for grp, names in groups.items():
        ici[grp] = _gather_ici(f"gather_{grp}_ici", [casts[nm] for nm in names])
        if grp == "w1_out":
            w_in_gather = _TwoPartGather("w_in", casts["w_in"])
            w_in_gather.start_second()
    ada_rows = lax.dynamic_index_in_dim(g1[0::2], dev, axis=1, keepdims=False)
    ada = ada_rows.reshape(3, 3, 1, d)
    (sh1, sc1, gt1), (sh2, sc2, gt2), (sh3, sc3, gt3) = [[ada[i, j] for j in range(3)] for i in range(3)]

    row = lambda vct: vct.reshape(1, -1)
    g1v, gmv, g2v, gfv = row(g_ffn1), row(g_mix), row(g_ffn2), row(g_final)

    def arrived(grp):
        return _gather_d2d(f"gather_{grp}_d2d", ici[grp].wait())

    def gathered(fwd, grp):
        return {nm: g.reshape(N_CHIPS, 2 * g.shape[2], g.shape[3]) for nm, g in zip(groups[grp], fwd.wait())}

    x2 = x[0]
    tgt = loss_target[0]

    n1 = _norm_mod("ffn1_norm", x2, g1v, sc1, sh1, ts)
    fwd, w1_in_parts = {}, []

    def w1_in_part(part):
        def get():
            w1_in_gather.arrive(part)
            w1_in_parts.append(w1_in_gather.ready(part))
            return w1_in_parts[-1]
        return get

    def w1_out_after_swiglu():
        fwd["w1_out"] = arrived("w1_out")
        w_in_gather.arrive(0)
        return gathered(fwd["w1_out"], "w1_out")["w1_out"].reshape(f_dim, d)

    hu1, act1, f1, w1_out_2d = _ffn_fwd("ffn1", n1, [w1_in_part(0), w1_in_part(1)], w1_out_after_swiglu, dims)
    w1_in_g = w1_in_parts[-1]
    h1, n2 = _residual_norm_mod("mix_norm", x2, f1, gt1, 0.5, gmv, sc2, sh2, ts)

    tm = _tile(s_len, 1024)
    tnp = npc // 2
    proj = ()
    for part in range(2):
        if part:
            w_in_gather.arrive(part)
        w_in_g = w_in_gather.ready(part)
        proj = (_matmul(
            f"mix_proj{part}", n2, [w_in_g], mode="nn", grid=(s_len // tm, N_CHIPS, 1),
            a_spec=pl.BlockSpec((tm, d), lambda i, j, k: (i, 0)),
            b_specs=[pl.BlockSpec((None, d, tnp), lambda i, j, k, part=part: (j, 0, part))],
            out_shape=_sds((N_CHIPS, s_len, npc), BF16),
            out_specs=pl.BlockSpec((None, tm, tnp), lambda i, j, k, part=part: (j, i, part)),
            acc_shape=(tm, tnp), epilogue=_ep_store(BF16), carry=proj),)
    proj = proj[0]
    fwd["mix"] = arrived("mix")
    cbv, lgv, lbv = row(conv_b), row(ln_a_g), row(ln_a_b)
    a3, mixed, conv_out = _mixer_mid("mix_mid", proj, cw_pad, cbv, lgv, lbv, wc, wp, ts)
    wts = gathered(fwd["mix"], "mix")
    w_out_2d = wts["w_out"].reshape(d, d)
    w_a_g = wts["w_a_out"]
    w_b_r = _regroup("regroup_w_b", wts["w_b_group"], n_groups)
    dq = d // N_CHIPS
    ya = _matmul(
        "mix_ya", a3, [w_a_g], mode="nn", grid=(s_len // tm, N_CHIPS, 1),
        a_spec=pl.BlockSpec((tm, wc), lambda i, j, k: (i, 0)),
        b_specs=[pl.BlockSpec((None, wc, dq), lambda i, j, k: (j, 0, 0))],
        out_shape=_sds((s_len, d), BF16), out_specs=pl.BlockSpec((tm, dq), lambda i, j, k: (i, j)),
        acc_shape=(tm, dq), epilogue=_ep_store(BF16))
    yb = _matmul(
        "mix_yb", mixed, [w_b_r], mode="nn", grid=(s_len // tm, n_groups, 1),
        a_spec=pl.BlockSpec((tm, gi), lambda i, j, k: (i, j)),
        b_specs=[pl.BlockSpec((None, gi, dq), lambda i, j, k: (j, 0, 0))],
        out_shape=_sds((s_len, d), BF16), out_specs=pl.BlockSpec((tm, dq), lambda i, j, k: (i, j)),
        acc_shape=(tm, dq), epilogue=_ep_store(BF16))
    bav, lsv = row(b_a_out), row(ls_b)
    z = _gates_fwd("mix_gates", proj, ya, yb, bav, bb_full, lsv, wc, wp, ts)
    tn = _tile(d, 1024)
    mix = _matmul(
        "mix_out", z, [w_out_2d], mode="nn", grid=(s_len // tm, d // tn, 1),
        a_spec=pl.BlockSpec((tm, d), lambda i, j, k: (i, 0)),
        b_specs=[pl.BlockSpec((d, tn), lambda i, j, k: (0, j))],
        out_shape=_sds((s_len, d), F32), out_specs=pl.BlockSpec((tm, tn), lambda i, j, k: (i, j)),
        acc_shape=(tm, tn), epilogue=_ep_store(F32))
    fwd["w2_in"] = arrived("w2_in")
    h2, n3 = _residual_norm_mod("ffn2_norm", h1, mix, gt2, 1.0, g2v, sc3, sh3, ts)
    w2_in_g = gathered(fwd["w2_in"], "w2_in")["w2_in"]
    hu2, act2, f3, w2_out_2d = _ffn_fwd(
        "ffn2", n3, [lambda: w2_in_g],
        lambda: gathered(arrived("w2_out"), "w2_out")["w2_out"].reshape(f_dim, d), dims)

    dh3, df3, d_gf, d_gt3, loss_cols = _final_loss("final_loss", h2, f3, tgt, gt3, 0.5, gfv, ts)
    rs, held = {}, {}
    dn3 = _ffn_bwd(
        "ffn2", n3, hu2, act2, df3, w2_in_g, w2_out_2d, dims,
        after_dw_out=lambda g: held.update(w2_out=g),
        after_dw_in=lambda g: rs.update(ffn2=_ReduceScatter("g_ffn2", ["w2_out", "w2_in"], [held["w2_out"], g],
                                                            qc_idx)))
    dh2, dmix, d_sh3, d_sc3, d_g2, d_gt2 = _norm_mod_bwd("ffn2_norm_bwd", h2, dn3, dh3, g2v, sc3, ts,
                                                         prev=(mix, gt2, 1.0))
    rs["ffn2"].step2()

    tk = s_len
    hq = d // (2 * N_CHIPS)
    gw_out = _matmul(
        "mix_dw_out", z, [dmix], mode="tn", grid=(N_CHIPS, d // tn, s_len // tk),
        a_spec=pl.BlockSpec((tk, 2 * hq), lambda i, j, k: (k, i)),
        b_specs=[pl.BlockSpec((tk, tn), lambda i, j, k: (k, j))],
        out_shape=_sds((2, N_CHIPS, hq, d), F32),
        out_specs=pl.BlockSpec((2, None, hq, tn), lambda i, j, k: (0, i, 0, j)),
        acc_shape=(2 * hq, tn), epilogue=_ep_halves(hq))
    dz = _matmul(
        "mix_dz", dmix, [w_out_2d], mode="nt", grid=(s_len // tm, d // tn, 1),
        a_spec=pl.BlockSpec((tm, d), lambda i, j, k: (i, 0)),
        b_specs=[pl.BlockSpec((tn, d), lambda i, j, k: (j, 0))],
        out_shape=_sds((s_len, d), BF16), out_specs=pl.BlockSpec((tm, tn), lambda i, j, k: (i, j)),
        acc_shape=(tm, tn), epilogue=_ep_store(BF16))
    dya, dyb, dgates, d_ba, d_ls, d_bb = _gates_bwd("mix_gates_bwd", proj, dz, ya, yb, bav, bb_full, lsv, wc, wp, ts)
    gw_a = _matmul(
        "mix_dw_a", a3, [dya], mode="tn", grid=(1, N_CHIPS, s_len // tk),
        a_spec=pl.BlockSpec((tk, wc), lambda i, j, k: (k, 0)),
        b_specs=[pl.BlockSpec((tk, dq), lambda i, j, k: (k, j))],
        out_shape=_sds((2, N_CHIPS, wc // 2, dq), F32),
        out_specs=pl.BlockSpec((2, None, wc // 2, dq), lambda i, j, k: (0, j, 0, 0)),
        acc_shape=(wc, dq), epilogue=_ep_halves(wc // 2))
    da3 = _matmul(
        "mix_da3", dya, [w_a_g], mode="nt", grid=(s_len // tm, 1, N_CHIPS),
        a_spec=pl.BlockSpec((tm, dq), lambda i, j, k: (i, k)),
        b_specs=[pl.BlockSpec((None, wc, dq), lambda i, j, k: (k, 0, 0))],
        out_shape=_sds((s_len, wc), BF16), out_specs=pl.BlockSpec((tm, wc), lambda i, j, k: (i, 0)),
        acc_shape=(tm, wc), epilogue=_ep_store(BF16))
    gpr = n_groups // 2

    def ep_by_chip(accs, ex, outs):
        for k in range(N_CHIPS):
            outs[0][k] = accs[0][:, k * goq:(k + 1) * goq]

    gw_b = _matmul(
        "mix_dw_b", mixed, [dyb], mode="tn", grid=(1, n_groups, s_len // tk),
        a_spec=pl.BlockSpec((tk, gi), lambda i, j, k: (k, j)),
        b_specs=[pl.BlockSpec((tk, dq), lambda i, j, k: (k, j))],
        out_shape=_sds((2, N_CHIPS, gpr * gi, goq), F32),
        out_specs=pl.BlockSpec((None, N_CHIPS, gi, goq), lambda i, j, k: (j // gpr, 0, j % gpr, 0)),
        acc_shape=(gi, dq), epilogue=ep_by_chip)
    dmixed = _matmul(
        "mix_dmixed", dyb, [w_b_r], mode="nt", grid=(s_len // tm, n_groups, 1),
        a_spec=pl.BlockSpec((tm, dq), lambda i, j, k: (i, j)),
        b_specs=[pl.BlockSpec((None, gi, dq), lambda i, j, k: (j, 0, 0))],
        out_shape=_sds((s_len, wp), BF16), out_specs=pl.BlockSpec((tm, gi), lambda i, j, k: (i, j)),
        acc_shape=(tm, gi), epilogue=_ep_store(BF16))
    da1, d_lg, d_lb, d_cb, d_cw = _conv_branch_bwd("mix_conv_bwd", proj, conv_out, da3, lgv, lbv, wc, wp, ts)
    dproj = _mixer_in_bwd("mix_in_bwd", proj, da1, dmixed, dgates, cw_pad, wc, wp, ts)
    hd = d // 2
    rt = hd // 2
    gw_in = _matmul(
        "mix_dw_in", n2, [dproj], mode="tn", grid=(N_CHIPS, 4, 1),
        a_spec=pl.BlockSpec((s_len, rt), lambda j, i, k: (0, i)),
        b_specs=[pl.BlockSpec((None, s_len, npc), lambda j, i, k: (j, 0, 0))],
        out_shape=_sds((2, N_CHIPS, hd, npc), F32),
        out_specs=pl.BlockSpec((None, None, rt, npc), lambda j, i, k: (i // 2, j, i % 2, 0)),
        acc_shape=(rt, npc), epilogue=_ep_store(F32))
    rs["mix"] = _ReduceScatter("g_mix", ["w_in", "w_a_out", "w_b_group", "w_out"], [gw_in, gw_a, gw_b, gw_out],
                               qc_idx)
    rs["ffn2"].step3()
    dn2 = _matmul(
        "mix_dn", dproj, [w_in_g], mode="nt", grid=(s_len // tm, d // tn, N_CHIPS),
        a_spec=pl.BlockSpec((None, tm, npc), lambda i, j, k: (k, i, 0)),
        b_specs=[pl.BlockSpec((None, tn, npc), lambda i, j, k: (k, j, 0))],
        out_shape=_sds((s_len, d), BF16), out_specs=pl.BlockSpec((tm, tn), lambda i, j, k: (i, j)),
        acc_shape=(tm, tn), epilogue=_ep_store(BF16))
    dh1, df1, d_sh2, d_sc2, d_gm, d_gt1 = _norm_mod_bwd("mix_norm_bwd", h1, dn2, dh2, gmv, sc2, ts,
                                                        prev=(f1, gt1, 0.5))
    rs["mix"].step2()

    def w1_in_ready(g):
        rs["w1_in"] = _ReduceScatter("g_w1_in", ["w1_in"], [g], qc_idx)
        rs["w1_out"].step2()
        rs["mix"].step3()

    dn1 = _ffn_bwd(
        "ffn1", n1, hu1, act1, df1, w1_in_g, w1_out_2d, dims,
        after_dw_out=lambda g: rs.update(w1_out=_ReduceScatter("g_w1_out", ["w1_out"], [g], qc_idx)),
        after_dw_in=w1_in_ready)
    grad_x, d_sh1, d_sc1, d_g1 = _norm_mod_bwd("ffn1_norm_bwd", x2, dn1, dh1, g1v, sc1, ts)

    d_ada = jnp.concatenate([d_sh1, d_sc1, d_gt1, d_sh2, d_sc2, d_gt2, d_sh3, d_sc3, d_gt3], axis=1)
    small = [d_ada, d_g1, d_gm, d_cw[:CONV_K].reshape(1, -1), d_cb, d_lg, d_lb, d_ba, d_bb, d_ls, d_g2, d_gf,
             loss_cols]
    sizes = [a.shape[1] for a in small]
    pack1 = jnp.concatenate(small, axis=1).reshape(-1)
    n1p = -(-pack1.shape[0] // (8 * LANES)) * LANES
    pack1 = jnp.pad(pack1, (0, 8 * n1p - pack1.shape[0])).reshape(8, n1p)
    g2 = _allgather_small("gather_small_grads", pack1)
    rs["w1_in"].step2()
    total = _sum_devices("sum_small_grads", g2, 8).reshape(-1)
    offs = [0]
    for sz in sizes:
        offs.append(offs[-1] + sz)
    tot = [total[offs[k]:offs[k + 1]] for k in range(len(sizes))]
    d_ada_all = g2.reshape(N_DEV, 8 * n1p)[:, :sizes[0]]
    loss = jnp.sum(tot[12])

    grads = {}
    grads["b_ada"] = tot[0]
    grads["g_ffn1"], grads["g_mix"] = tot[1], tot[2]
    grads["conv_w"] = lax.dynamic_slice(tot[3].reshape(CONV_K, wc), (0, q * cwq), (CONV_K, cwq))
    grads["conv_b"], grads["ln_a_g"], grads["ln_a_b"], grads["b_a_out"] = tot[4], tot[5], tot[6], tot[7]
    grads["b_b_group"] = lax.dynamic_slice(tot[8].reshape(n_groups, N_CHIPS * goq), (0, q * goq), (n_groups, goq))
    grads["ls_b"], grads["g_ffn2"], grads["g_final"] = tot[9], tot[10], tot[11]

    delta, new_m, new_v = {}, {}, {}

    def adamw_group(reduced):
        for nm, g in reduced.items():
            shp = weights[nm].shape
            go, dl, mo, vo = _adamw(f"adamw_{nm}", as2d(weights[nm]), g, as2d(mom1[nm]), as2d(mom2[nm]))
            grads[nm], delta[nm], new_m[nm], new_v[nm] = go.reshape(shp), dl.reshape(shp), mo.reshape(shp), vo.reshape(shp)

    adamw_group(rs["ffn2"].result())
    rs["w1_out"].step3()
    adamw_group(rs["mix"].result())
    d_ada_mine = lax.dynamic_slice(d_ada_all, (0, q * ada_c), (N_DEV, ada_c))
    grads["w_ada"], delta["w_ada"], new_m["w_ada"], new_v["w_ada"] = _ada_grad_adamw(
        "adamw_w_ada", c_all.T, d_ada_mine, w_ada, m_w_ada, v_w_ada)
    rs["w1_in"].step3()
    smalls = [nm for nm in order if nm not in big and nm != "w_ada"]
    flat = lambda src: jnp.concatenate([src[nm].reshape(-1) for nm in smalls])
    n_small = sum(weights[nm].size for nm in smalls)
    rows_s = -(-n_small // (8 * LANES)) * 8
    packed = [jnp.pad(flat(src), (0, rows_s * LANES - n_small)).reshape(rows_s, LANES)
              for src in (weights, grads, mom1, mom2)]
    _, dl_s, mo_s, vo_s = _adamw("adamw_small", *packed)
    off = 0
    for nm in smalls:
        sz, shp = weights[nm].size, weights[nm].shape
        delta[nm] = dl_s.reshape(-1)[off:off + sz].reshape(shp)
        new_m[nm] = mo_s.reshape(-1)[off:off + sz].reshape(shp)
        new_v[nm] = vo_s.reshape(-1)[off:off + sz].reshape(shp)
        grads[nm] = grads[nm].reshape(shp)
        off += sz
    adamw_group(rs["w1_out"].result())
    adamw_group(rs["w1_in"].result())

    return (loss, grad_x[None], *[grads[nm] for nm in order], *[delta[nm] for nm in order],
            *[new_m[nm] for nm in order], *[new_v[nm] for nm in order])
```

```python
import jax
import jax.numpy as jnp
from jax import lax
from jax.experimental import pallas as pl
from jax.experimental.pallas import tpu as pltpu

F32 = jnp.float32
BF16 = jnp.bfloat16
MESH = pl.DeviceIdType.MESH
ANY = pl.BlockSpec(memory_space=pl.ANY)
HBM = pl.BlockSpec(memory_space=pltpu.HBM)
SEM = pl.BlockSpec(memory_space=pltpu.SEMAPHORE)
EFFECT = pltpu.SideEffectType.DATAFLOW_SIDE_EFFECTING

EPS = 1e-6
CONV_K = 31
HALO = 32
POOL_WINDOWS = (2, 4, 8, 16)
N_CHIPS = 4
N_DEV = 8
LANES = 128

ADAM_LR = 0.001
ADAM_B1 = 0.9
ADAM_B2 = 0.999
ADAM_EPS = 1e-08
ADAM_WD = 0.01
ADAM_STEP = 10

DN = {
    "nn": (((1,), (0,)), ((), ())),
    "nt": (((1,), (1,)), ((), ())),
    "tn": (((0,), (0,)), ((), ())),
}


_PREVIOUS = []


def _ordered(call, args, n_lead, body, token=None, sources=()):
    dep = [pltpu.with_memory_space_constraint(p, pltpu.HBM) if p.size * p.dtype.itemsize >= (1 << 20) else p
           for p in _PREVIOUS if all(p is not a for a in (*args, *sources))]

    def wrapped(*refs):
        return body(*refs[:n_lead], *refs[n_lead + len(dep):])

    outs = call(wrapped, [ANY] * len(dep))(*args, *dep)
    seq = outs if isinstance(outs, (list, tuple)) else [outs]
    _PREVIOUS[:] = [seq[token] if token is not None else
                    next(o for o in seq if jnp.issubdtype(o.dtype, jnp.floating))]
    return outs


def _pcall(body, *, name, out_shape, grid=None, in_specs=None, out_specs=None, scratch=(), aliases=None,
           prefetch=0, vmem_mb=None):
    params = {}
    if grid is not None:
        params["dimension_semantics"] = ("arbitrary",) * len(grid)
    if vmem_mb is not None:
        params["vmem_limit_bytes"] = vmem_mb << 20
    def in_hbm(shape, spec):
        big = shape.size * jnp.dtype(shape.dtype).itemsize >= (1 << 20)
        return pltpu.HBM(shape.shape, shape.dtype) if big and getattr(spec, "memory_space", None) != pltpu.VMEM else shape

    if isinstance(out_shape, (list, tuple)):
        out_shape = [in_hbm(s, sp) for s, sp in zip(out_shape, out_specs)]
    else:
        out_shape = in_hbm(out_shape, out_specs)
    kw = dict(name=name, out_shape=out_shape, compiler_params=pltpu.CompilerParams(**params))
    if aliases:
        kw["input_output_aliases"] = aliases

    def call(wrapped, dep_specs):
        specs = list(in_specs) + dep_specs
        if prefetch:
            return pl.pallas_call(wrapped, grid_spec=pltpu.PrefetchScalarGridSpec(
                num_scalar_prefetch=prefetch, grid=grid, in_specs=specs, out_specs=out_specs,
                scratch_shapes=list(scratch)), **kw)
        if grid is not None:
            return pl.pallas_call(wrapped, grid=grid, in_specs=specs, out_specs=out_specs,
                                  scratch_shapes=list(scratch), **kw)
        return pl.pallas_call(wrapped, in_specs=specs, out_specs=out_specs, scratch_shapes=list(scratch), **kw)

    def run(*args):
        specs = [None] * prefetch + list(in_specs)
        placed = [pltpu.with_memory_space_constraint(a, pltpu.HBM)
                  if a.size * a.dtype.itemsize >= (1 << 20) and getattr(s, "memory_space", None) != pltpu.VMEM else a
                  for a, s in zip(args, specs)]
        return _ordered(call, placed, prefetch + len(in_specs), body, sources=args)

    return run


def _mult(offset, unit):
    return pl.multiple_of(offset, unit)


def _tile(dim, pref):
    t = min(dim, pref)
    assert dim % t == 0, (dim, pref)
    return t


def _sds(shape, dtype):
    return jax.ShapeDtypeStruct(tuple(shape), dtype)


def _sigmoid(v):
    return 0.5 * jnp.tanh(0.5 * v) + 0.5


def _vec(w):
    return pl.BlockSpec((1, w), lambda *_: (0, 0))


def _acc_rows(ref, val, i):
    @pl.when(i == 0)
    def _():
        ref[...] = jnp.zeros_like(ref)

    ref[...] += jnp.sum(val, axis=0, keepdims=True)


def _matmul(name, a, bs, *, mode, grid, a_spec, b_specs, out_shape, out_specs, acc_shape, epilogue,
            extras=(), extra_specs=(), vmem_mb=56, carry=(), col_block=None):
    nb, ne, nk, nc = len(bs), len(extras), grid[2], len(carry)
    dn = DN[mode]

    def body(*all_refs):
        refs = all_refs[:1 + nb + ne] + all_refs[1 + nb + ne + nc:]
        a_ref, b_refs, ex = refs[0], refs[1:1 + nb], refs[1 + nb:1 + nb + ne]
        if col_block:
            outs, av, width = refs[1 + nb + ne:], a_ref[...], b_refs[0].shape[-1]
            for lo in range(0, width, col_block):
                cs = slice(lo, min(lo + col_block, width))
                epilogue([lax.dot_general(av, b[:, cs], dn, preferred_element_type=F32) for b in b_refs], ex, outs, cs)
            return
        if nk == 1:
            outs = refs[1 + nb + ne:]
            accs = [lax.dot_general(a_ref[...], b[...], dn, preferred_element_type=F32) for b in b_refs]
            epilogue(accs, ex, outs)
            return
        outs, acc_refs = refs[1 + nb + ne:-nb], refs[-nb:]
        k = pl.program_id(2)

        @pl.when(k == 0)
        def _():
            for acc in acc_refs:
                acc[...] = jnp.zeros_like(acc)

        for acc, b in zip(acc_refs, b_refs):
            acc[...] += lax.dot_general(a_ref[...], b[...], dn, preferred_element_type=F32)

        @pl.when(k == nk - 1)
        def _():
            epilogue([acc[...] for acc in acc_refs], ex, outs)

    scratch = [pltpu.VMEM(acc_shape, F32) for _ in range(nb)] if nk > 1 else []
    return _pcall(body, name=name, out_shape=out_shape, grid=grid,
                  in_specs=[a_spec, *b_specs, *extra_specs, *[ANY] * nc], out_specs=out_specs, scratch=scratch,
                  aliases={1 + nb + ne + i: i for i in range(nc)}, vmem_mb=vmem_mb)(a, *bs, *extras, *carry)


def _ep_store(dtype):
    def ep(accs, ex, outs):
        outs[0][...] = accs[0].astype(dtype)
    return ep


def _ep_halves(h):
    def ep(accs, ex, outs):
        outs[0][0] = accs[0][:h]
        outs[0][1] = accs[0][h:]
    return ep


def _place():
    x, y, c = lax.axis_index("x"), lax.axis_index("y"), lax.axis_index("c")
    chips = [(1 - x, y), (x, 1 - y), (1 - x, 1 - y)]
    return x, y, c, chips


def _allgather_small(name, block):
    m_per, n = block.shape

    def body(x_ref, out_ref, send_sems, recv_sems, local_sem):
        x, y, c, chips = _place()
        me, sibling = (x, y, c), (x, y, 1 - c)

        def rows(px, py, pc):
            return out_ref.at[pl.ds((4 * px + 2 * py + pc) * m_per, m_per), :]

        def copy(k, blk, to, src=None):
            return pltpu.make_async_remote_copy(
                src_ref=rows(*blk) if src is None else src, dst_ref=rows(*blk),
                send_sem=send_sems.at[k], recv_sem=recv_sems.at[k], device_id=to, device_id_type=MESH)

        mine = pltpu.make_async_copy(x_ref, rows(*me), local_sem)
        mine.start()
        first = [copy(0, me, sibling, src=x_ref)]
        first += [copy(1 + j, me, (*chip, c), src=x_ref) for j, chip in enumerate(chips)]
        for cp in first:
            cp.start()
        passed = [copy(4 + j, (*chip, c), sibling) for j, chip in enumerate(chips)]
        for j, chip in enumerate(chips):
            copy(1 + j, (*chip, c), me).wait_recv()
            passed[j].start()
        copy(0, sibling, me).wait_recv()
        for j, chip in enumerate(chips):
            copy(4 + j, (*chip, 1 - c), me).wait_recv()
        for cp in first + passed:
            cp.wait_send()
        mine.wait()

    return _pcall(
        body, name=name, out_shape=_sds((N_DEV * m_per, n), block.dtype),
        in_specs=[pl.BlockSpec(memory_space=pltpu.VMEM)], out_specs=pl.BlockSpec(memory_space=pltpu.VMEM),
        scratch=[pltpu.SemaphoreType.DMA((7,)), pltpu.SemaphoreType.DMA((7,)), pltpu.SemaphoreType.DMA],
    )(block)


class _SplitCopies:
    def __init__(self, name, arrays, plan, n_copies):
        self.name, self.plan, self.n = name, plan, len(arrays)
        n = self.n

        def body(*refs):
            send, recv, token = refs[n], refs[n + 1], refs[-1]
            for k, (src, dst, _, peer) in enumerate(plan(refs[:n])):
                pltpu.make_async_remote_copy(src_ref=src, dst_ref=dst, send_sem=send.at[k], recv_sem=recv.at[k],
                                             device_id=peer, device_id_type=MESH).start()
            token[...] = jnp.zeros_like(token)

        def call(wrapped, dep_specs):
            return pl.pallas_call(
                wrapped, name=f"{name}_start",
                out_shape=(pltpu.SemaphoreType.DMA((n_copies,)), pltpu.SemaphoreType.DMA((n_copies,)),
                           *[pltpu.HBM(a.shape, a.dtype) for a in arrays], _sds((8, LANES), F32)),
                in_specs=[HBM] * n + dep_specs,
                out_specs=(SEM, SEM, *[HBM] * n, pl.BlockSpec(memory_space=pltpu.VMEM)),
                input_output_aliases={i: 2 + i for i in range(n)},
                compiler_params=pltpu.CompilerParams(has_side_effects=EFFECT))

        outs = _ordered(call, [pltpu.with_memory_space_constraint(a, pltpu.HBM) for a in arrays], n, body, token=-1,
                        sources=arrays)
        self.send, self.recv, self.arrays = outs[0], outs[1], list(outs[2:2 + n])

    def wait(self, arrays=None):
        n, plan = self.n, self.plan
        if arrays is not None:
            self.arrays = list(arrays)

        def body(*refs):
            send, recv, token = refs[n], refs[n + 1], refs[-1]
            for k, (src, _, landing, peer) in enumerate(plan(refs[:n])):
                cp = pltpu.make_async_remote_copy(src_ref=src, dst_ref=landing, send_sem=send.at[k],
                                                  recv_sem=recv.at[k], device_id=peer, device_id_type=MESH)
                cp.wait_send()
                cp.wait_recv()
            token[...] = jnp.zeros_like(token)

        def call(wrapped, dep_specs):
            return pl.pallas_call(
                wrapped, name=f"{self.name}_wait",
                out_shape=(*[pltpu.HBM(a.shape, a.dtype) for a in self.arrays], _sds((8, LANES), F32)),
                in_specs=[HBM] * n + [SEM, SEM] + dep_specs,
                out_specs=(*[HBM] * n, pl.BlockSpec(memory_space=pltpu.VMEM)),
                input_output_aliases={i: i for i in range(n)},
                compiler_params=pltpu.CompilerParams(has_side_effects=EFFECT))

        return list(_ordered(call, [*self.arrays, self.send, self.recv], n + 2, body, token=-1))[:n]


def _col_range(g, cols):
    lo, width = cols if cols is not None else (0, g.shape[-1])
    return (slice(None), pl.ds(lo, width))


def _gather_ici(name, gathered, cols=None):
    def plan(refs):
        x, y, c, chips = _place()
        q = 2 * x + y
        return [(g.at[(q, c, *_col_range(g, cols))], g.at[(q, c, *_col_range(g, cols))],
                 g.at[(2 * px + py, c, *_col_range(g, cols))], (px, py, c))
                for g in refs for px, py in chips]

    return _SplitCopies(name, gathered, plan, 3 * len(gathered))


def _gather_d2d(name, gathered, cols=None):
    def plan(refs):
        x, y, c, chips = _place()
        return [(g.at[(2 * px + py, c, *_col_range(g, cols))], g.at[(2 * px + py, c, *_col_range(g, cols))],
                 g.at[(2 * px + py, 1 - c, *_col_range(g, cols))], (x, y, 1 - c))
                for g in refs for px, py in chips]

    return _SplitCopies(name, gathered, plan, 3 * len(gathered))


MXU_COLS = 256


def _two_parts(width):
    passes = width // MXU_COLS
    first = (passes // 2) * MXU_COLS if width % MXU_COLS == 0 and passes >= 2 else width // 2
    return [(0, first), (first, width - first)]


class _TwoPartGather:
    def __init__(self, name, gathered):
        self.name, self.d2d = name, {}
        self.parts = _two_parts(gathered.shape[-1])
        self.ici = [_gather_ici(f"gather_{name}_a_ici", [gathered], self.parts[0])]
        self.buf = self.ici[0].arrays

    def start_second(self):
        self.ici.append(_gather_ici(f"gather_{self.name}_b_ici", self.buf, self.parts[1]))
        self.buf = self.ici[1].arrays

    def arrive(self, part):
        here = self.ici[part].wait(self.buf)
        self.d2d[part] = _gather_d2d(f"gather_{self.name}_{'ab'[part]}_d2d", here, self.parts[part])
        self.buf = self.d2d[part].arrays

    def ready(self, part):
        self.buf = self.d2d[part].wait(self.buf)
        g = self.buf[0]
        return g.reshape(N_CHIPS, 2 * g.shape[2], g.shape[3])


def _scatter_sibling(name, grads):
    n = len(grads)

    def plan(refs):
        x, y, c, _ = _place()
        return [(refs[w].at[1 - c], refs[n + w], refs[n + w], (x, y, 1 - c)) for w in range(n)]

    landing = [lax.empty(g.shape[1:], g.dtype) for g in grads]
    return _SplitCopies(name, [*grads, *landing], plan, n)


def _scatter_chips(name, sums):
    n = len(sums)

    def plan(refs):
        x, y, c, chips = _place()
        return [(refs[w].at[2 * px + py], refs[n + w].at[j], refs[n + w].at[j], (px, py, c))
                for w in range(n) for j, (px, py) in enumerate(chips)]

    landing = [lax.empty((3, *s.shape[1:]), s.dtype) for s in sums]
    return _SplitCopies(name, [*sums, *landing], plan, 3 * n)


def _share_final(name, finals):
    def plan(refs):
        x, y, c, _ = _place()
        return [(f.at[c], f.at[c], f.at[1 - c], (x, y, 1 - c)) for f in refs]

    return _SplitCopies(name, finals, plan, len(finals))


def _row_tile(rows, cols, budget_elems=786432):
    best = 8
    for t in range(8, rows + 1, 8):
        if rows % t == 0 and t * cols <= budget_elems:
            best = t
    return best if rows % best == 0 else rows


def _sum_with_sibling(name, grad, recv, qc_idx):
    _, _, h, cols = grad.shape
    tr = _row_tile(h, cols)

    def body(s_ref, g_ref, r_ref, own_ref, pb_ref):
        p = g_ref[...] + r_ref[...]
        pb_ref[...] = p.astype(BF16)

        @pl.when(pl.program_id(1) == s_ref[0])
        def _():
            own_ref[...] = p

    blk = pl.BlockSpec((None, tr, cols), lambda r, k, s: (k, r, 0))
    return _pcall(
        body, name=name, out_shape=[_sds((h, cols), F32), _sds((N_CHIPS, h, cols), BF16)],
        grid=(h // tr, N_CHIPS), prefetch=1,
        in_specs=[pl.BlockSpec((None, None, tr, cols), lambda r, k, s: (s[1], k, r, 0)), blk],
        out_specs=[pl.BlockSpec((tr, cols), lambda r, k, s: (r, 0)), blk], vmem_mb=32,
    )(qc_idx, grad, recv)


def _sum_chips(name, own, recv, qc_idx):
    h, cols = own.shape
    tr = _row_tile(h, cols)

    def body(s_ref, p_ref, t_ref, o_ref):
        o_ref[...] = ((p_ref[...] + t_ref[0].astype(F32)) + t_ref[1].astype(F32)) + t_ref[2].astype(F32)

    return _pcall(
        body, name=name, out_shape=_sds((2, h, cols), F32), grid=(h // tr,), prefetch=1,
        in_specs=[pl.BlockSpec((tr, cols), lambda r, s: (r, 0)),
                  pl.BlockSpec((3, tr, cols), lambda r, s: (0, r, 0))],
        out_specs=pl.BlockSpec((None, tr, cols), lambda r, s: (s[1], r, 0)), vmem_mb=32,
    )(qc_idx, own, recv)


class _ReduceScatter:
    def __init__(self, tag, names, grads, qc_idx):
        self.tag, self.names, self.n, self.qc_idx = tag, names, len(grads), qc_idx
        self.copies = _scatter_sibling(f"{tag}_rs_sibling", grads)

    def step2(self):
        n = self.n
        arrs = self.copies.wait()
        sums = [_sum_with_sibling(f"{nm}_sum_sibling", arrs[w], arrs[n + w], self.qc_idx)
                for w, nm in enumerate(self.names)]
        self.own = [s[0] for s in sums]
        self.copies = _scatter_chips(f"{self.tag}_rs_chips", [s[1] for s in sums])

    def step3(self):
        n = self.n
        arrs = self.copies.wait()
        finals = [_sum_chips(f"{nm}_sum_chips", self.own[w], arrs[n + w], self.qc_idx)
                  for w, nm in enumerate(self.names)]
        self.copies = _share_final(f"{self.tag}_rs_final", finals)

    def result(self):
        return {nm: f.reshape(2 * f.shape[1], f.shape[2]) for nm, f in zip(self.names, self.copies.wait())}


def _cast_into_gathered(name, w, q_idx):
    rows, cols = w.shape
    h = rows // 2
    tr = _row_tile(h, cols, 1 << 20)
    nr = h // tr

    def body(s_ref, w_ref, o_ref):
        o_ref[...] = w_ref[...].astype(BF16)

    return _pcall(body, name=name, out_shape=_sds((N_CHIPS, 2, h, cols), BF16), grid=(2, nr), prefetch=1,
                  in_specs=[pl.BlockSpec((tr, cols), lambda hf, r, s: (hf * nr + r, 0))],
                  out_specs=pl.BlockSpec((None, None, tr, cols), lambda hf, r, s: (s[0], hf, r, 0)),
                  vmem_mb=32)(q_idx, w)


def _regroup(name, w, n_groups):
    n_chips, rows, goq = w.shape
    gi = rows // n_groups

    def body(w_ref, o_ref):
        o_ref[...] = w_ref[...]

    return _pcall(body, name=name, out_shape=_sds((n_groups, gi, n_chips * goq), w.dtype), grid=(n_groups, n_chips),
                  in_specs=[pl.BlockSpec((None, gi, goq), lambda g, k: (k, g, 0))],
                  out_specs=pl.BlockSpec((None, gi, goq), lambda g, k: (g, 0, k)), vmem_mb=32)(w)


def _rms(h):
    r = lax.rsqrt(jnp.mean(h * h, axis=-1, keepdims=True) + EPS)
    return r, h * r


def _norm_mod(name, h, g, sc, sh, ts):
    s_len, d = h.shape

    def body(h_ref, g_ref, sc_ref, sh_ref, n_ref):
        _, xhat = _rms(h_ref[...])
        n_ref[...] = ((xhat * g_ref[...]) * (1.0 + sc_ref[...]) + sh_ref[...]).astype(BF16)

    row = pl.BlockSpec((ts, d), lambda i: (i, 0))
    return _pcall(body, name=name, out_shape=_sds((s_len, d), BF16), grid=(s_len // ts,),
                  in_specs=[row, _vec(d), _vec(d), _vec(d)], out_specs=row, vmem_mb=32)(h, g, sc, sh)


def _residual_norm_mod(name, h, f, gate, cmul, g, sc, sh, ts):
    s_len, d = h.shape

    def body(h_ref, f_ref, gt_ref, g_ref, sc_ref, sh_ref, ho_ref, n_ref):
        hn = h_ref[...] + (cmul * gt_ref[...]) * f_ref[...]
        ho_ref[...] = hn
        _, xhat = _rms(hn)
        n_ref[...] = ((xhat * g_ref[...]) * (1.0 + sc_ref[...]) + sh_ref[...]).astype(BF16)

    row = pl.BlockSpec((ts, d), lambda i: (i, 0))
    return _pcall(body, name=name, out_shape=[_sds((s_len, d), F32), _sds((s_len, d), BF16)],
                  grid=(s_len // ts,), in_specs=[row, row, _vec(d), _vec(d), _vec(d), _vec(d)],
                  out_specs=[row, row], vmem_mb=32)(h, f, gate, g, sc, sh)


def _final_loss(name, h, f, tgt, gate, cmul, g, ts):
    s_len, d = h.shape

    def body(h_ref, f_ref, t_ref, gt_ref, g_ref, dh_ref, df_ref, dg_ref, dgt_ref, loss_ref):
        i = pl.program_id(0)
        fv = f_ref[...]
        coef = cmul * gt_ref[...]
        hn = h_ref[...] + coef * fv
        r, xhat = _rms(hn)
        err = xhat * g_ref[...] - t_ref[...]
        _acc_rows(loss_ref, (0.5 / d) * (err * err), i)
        dy = err * (1.0 / d)
        _acc_rows(dg_ref, dy * xhat, i)
        dxhat = dy * g_ref[...]
        dh = r * (dxhat - xhat * jnp.mean(dxhat * xhat, axis=-1, keepdims=True))
        dh_ref[...] = dh
        _acc_rows(dgt_ref, cmul * (dh * fv), i)
        df_ref[...] = (coef * dh).astype(BF16)

    row = pl.BlockSpec((ts, d), lambda i: (i, 0))
    return _pcall(body, name=name,
                  out_shape=[_sds((s_len, d), F32), _sds((s_len, d), BF16)] + [_sds((1, d), F32)] * 3,
                  grid=(s_len // ts,), in_specs=[row, row, row, _vec(d), _vec(d)],
                  out_specs=[row, row, _vec(d), _vec(d), _vec(d)], vmem_mb=40)(h, f, tgt, gate, g)


def _norm_mod_bwd(name, h, dn, dh_next, g, sc, ts, prev=None):
    s_len, d = h.shape
    has_prev = prev is not None
    cmul = prev[2] if has_prev else None

    def body(*refs):
        if has_prev:
            h_ref, dn_ref, dhn_ref, f_ref, g_ref, sc_ref, gt_ref, dh_ref, df_ref, dsh_ref, dsc_ref, dg_ref, dgt_ref = refs
        else:
            h_ref, dn_ref, dhn_ref, g_ref, sc_ref, dh_ref, dsh_ref, dsc_ref, dg_ref = refs
        i = pl.program_id(0)
        r, xhat = _rms(h_ref[...])
        dn_v = dn_ref[...].astype(F32)
        gv = g_ref[...]
        _acc_rows(dsh_ref, dn_v, i)
        _acc_rows(dsc_ref, dn_v * (xhat * gv), i)
        dnrm = dn_v * (1.0 + sc_ref[...])
        _acc_rows(dg_ref, dnrm * xhat, i)
        dxhat = dnrm * gv
        dh = dhn_ref[...] + r * (dxhat - xhat * jnp.mean(dxhat * xhat, axis=-1, keepdims=True))
        dh_ref[...] = dh
        if has_prev:
            _acc_rows(dgt_ref, cmul * (dh * f_ref[...]), i)
            df_ref[...] = ((cmul * gt_ref[...]) * dh).astype(BF16)

    row = pl.BlockSpec((ts, d), lambda i: (i, 0))
    if has_prev:
        ins, in_specs = [h, dn, dh_next, prev[0], g, sc, prev[1]], [row, row, row, row, _vec(d), _vec(d), _vec(d)]
        out_shape = [_sds((s_len, d), F32), _sds((s_len, d), BF16)] + [_sds((1, d), F32)] * 4
        out_specs = [row, row] + [_vec(d)] * 4
    else:
        ins, in_specs = [h, dn, dh_next, g, sc], [row, row, row, _vec(d), _vec(d)]
        out_shape = [_sds((s_len, d), F32)] + [_sds((1, d), F32)] * 3
        out_specs = [row] + [_vec(d)] * 3
    return _pcall(body, name=name, out_shape=out_shape, grid=(s_len // ts,), in_specs=in_specs,
                  out_specs=out_specs, vmem_mb=40)(*ins)


def _cols(ref, lo, hi, npc, rows=slice(None)):
    parts = []
    while lo < hi:
        q, o = divmod(lo, npc)
        n = min(hi - lo, npc - o)
        parts.append(ref[q, rows, o:o + n].astype(F32))
        lo += n
    return parts[0] if len(parts) == 1 else jnp.concatenate(parts, axis=-1)


def _store_cols(ref, lo, val, npc, rows=slice(None)):
    off, width = 0, val.shape[-1]
    while off < width:
        q, o = divmod(lo + off, npc)
        n = min(width - off, npc - o)
        ref[q, rows, o:o + n] = val[:, off:off + n]
        off += n


def _chips_covering(cols, npc):
    return -(-cols // npc)


SUBLANES = 8
ROW_CHUNK = 32


def _make_phases(src_ref, ph_ref):
    rows = src_ref.shape[0] - SUBLANES
    for b in range(1, SUBLANES):
        ph_ref[b - 1] = src_ref[pl.ds(b, rows), :]


def _window(src_ref, ph_ref, off, r0, cols=slice(None)):
    a, b = divmod(off, SUBLANES)
    start = pl.multiple_of(r0 + SUBLANES * a, SUBLANES)
    if b == 0:
        return src_ref[pl.ds(start, ROW_CHUNK), cols]
    return ph_ref[b - 1, pl.ds(start, ROW_CHUNK), cols]


def _phase_scratch(rows, width):
    return pltpu.VMEM((SUBLANES - 1, rows - SUBLANES, width), F32)


def _conv(a0s_ref, a0p_ref, cw_ref, cb_ref, r0):
    a1 = cb_ref[...] + cw_ref[0:1, :] * _window(a0s_ref, a0p_ref, HALO - CONV_K + 1, r0)
    for k in range(1, CONV_K):
        a1 = a1 + cw_ref[k:k + 1, :] * _window(a0s_ref, a0p_ref, HALO - CONV_K + 1 + k, r0)
    return a1


def _layer_norm(a1, lg_ref, lb_ref):
    mu = jnp.mean(a1, axis=-1, keepdims=True)
    ctr = a1 - mu
    rstd = lax.rsqrt(jnp.mean(ctr * ctr, axis=-1, keepdims=True) + EPS)
    xh = ctr * rstd
    return xh, rstd, xh * lg_ref[...] + lb_ref[...]


def _for_chunks(ts, fn):
    def step(ci, carry):
        fn(pl.multiple_of(ci * ROW_CHUNK, ROW_CHUNK))
        return carry

    lax.fori_loop(0, ts // ROW_CHUNK, step, 0)


def _stage_glu(p_ref, ph_ref, a0s_ref, i, wc, npc, ts):
    a0 = _cols(p_ref, 0, wc, npc) * _sigmoid(_cols(p_ref, wc, 2 * wc, npc))
    a0h = _cols(ph_ref, 0, wc, npc) * _sigmoid(_cols(ph_ref, wc, 2 * wc, npc))
    a0s_ref[0:HALO, :] = jnp.where(i > 0, a0h, 0.0)
    a0s_ref[HALO:HALO + ts, :] = a0


def _mixer_mid(name, proj, cw, cb, lg, lb, wc, wp, ts):
    _, s_len, npc = proj.shape
    nq = _chips_covering(2 * wc + wp, npc)
    gi = wp // len(POOL_WINDOWS)
    hb = ts // HALO

    def body(p_ref, ph_ref, cw_ref, cb_ref, lg_ref, lb_ref, a3_ref, mx_ref, a1_ref, a0s_ref, vs_ref, a0p_ref,
             vp_ref):
        i = pl.program_id(0)
        _stage_glu(p_ref, ph_ref, a0s_ref, i, wc, npc, ts)
        vs_ref[0:HALO, :] = jnp.where(i > 0, _cols(ph_ref, 2 * wc, 2 * wc + wp, npc), 0.0)
        vs_ref[HALO:HALO + ts, :] = _cols(p_ref, 2 * wc, 2 * wc + wp, npc)
        _make_phases(a0s_ref, a0p_ref)
        _make_phases(vs_ref, vp_ref)

        def chunk(r0):
            rows = pl.ds(r0, ROW_CHUNK)
            a1 = _conv(a0s_ref, a0p_ref, cw_ref, cb_ref, r0)
            a1_ref[rows, :] = a1
            _, _, a2 = _layer_norm(a1, lg_ref, lb_ref)
            a3_ref[rows, :] = (a2 * _sigmoid(a2)).astype(BF16)
            t_abs = i * ts + r0 + lax.broadcasted_iota(jnp.int32, (ROW_CHUNK, 1), 0)
            for g, win in enumerate(POOL_WINDOWS):
                cs = slice(g * gi, (g + 1) * gi)
                v_now = _window(vs_ref, vp_ref, HALO, r0, cs)
                acc = v_now
                for dlt in range(1, win):
                    acc = acc + _window(vs_ref, vp_ref, HALO - dlt, r0, cs)
                cnt = jnp.minimum(t_abs + 1, win).astype(F32)
                mx_ref[rows, cs] = (acc / cnt - v_now).astype(BF16)

        _for_chunks(ts, chunk)

    return _pcall(
        body, name=name, out_shape=[_sds((s_len, wc), BF16), _sds((s_len, wp), BF16), _sds((s_len, wc), F32)],
        grid=(s_len // ts,),
        in_specs=[pl.BlockSpec((nq, ts, npc), lambda i: (0, i, 0)),
                  pl.BlockSpec((nq, HALO, npc), lambda i: (0, jnp.maximum(i * hb - 1, 0), 0)),
                  pl.BlockSpec((HALO, wc), lambda i: (0, 0)), _vec(wc), _vec(wc), _vec(wc)],
        out_specs=[pl.BlockSpec((ts, wc), lambda i: (i, 0)), pl.BlockSpec((ts, wp), lambda i: (i, 0)),
                   pl.BlockSpec((ts, wc), lambda i: (i, 0))],
        scratch=[pltpu.VMEM((HALO + ts, wc), F32), pltpu.VMEM((HALO + ts, wp), F32),
                 _phase_scratch(HALO + ts, wc), _phase_scratch(HALO + ts, wp)], vmem_mb=56,
    )(proj, proj, cw, cb, lg, lb)


def _gates_fwd(name, proj, ya, yb, b_a, b_b, ls, wc, wp, ts):
    _, s_len, npc = proj.shape
    d = ya.shape[1]
    g0 = 2 * wc + wp

    def body(p_ref, ya_ref, yb_ref, ba_ref, bb_ref, ls_ref, z_ref):
        ga = _sigmoid(_cols(p_ref, g0, g0 + d, npc))
        gb = _sigmoid(_cols(p_ref, g0 + d, g0 + 2 * d, npc))
        z = ga * (ya_ref[...] + ba_ref[...]) + gb * ((yb_ref[...] + bb_ref[...]) * ls_ref[...])
        z_ref[...] = z.astype(BF16)

    row = pl.BlockSpec((ts, d), lambda i: (i, 0))
    return _pcall(body, name=name, out_shape=_sds((s_len, d), BF16), grid=(s_len // ts,),
                  in_specs=[pl.BlockSpec((N_CHIPS, ts, npc), lambda i: (0, i, 0)), row, row, _vec(d), _vec(d), _vec(d)],
                  out_specs=row, vmem_mb=48)(proj, ya, yb, b_a, b_b, ls)


def _gates_bwd(name, proj, dz, ya, yb, b_a, b_b, ls, wc, wp, ts):
    _, s_len, npc = proj.shape
    d = ya.shape[1]
    g0 = 2 * wc + wp

    def body(p_ref, dz_ref, ya_ref, yb_ref, ba_ref, bb_ref, ls_ref, dya_ref, dyb_ref, dgt_ref, dba_ref, dls_ref,
             dbb_ref):
        i = pl.program_id(0)
        ga = _sigmoid(_cols(p_ref, g0, g0 + d, npc))
        gb = _sigmoid(_cols(p_ref, g0 + d, g0 + 2 * d, npc))
        dz_v = dz_ref[...].astype(F32)
        y_a = ya_ref[...] + ba_ref[...]
        y_b0 = yb_ref[...] + bb_ref[...]
        ls_v = ls_ref[...]
        dya = dz_v * ga
        dya_ref[...] = dya.astype(BF16)
        _acc_rows(dba_ref, dya, i)
        t = dz_v * gb
        _acc_rows(dls_ref, t * y_b0, i)
        dyb = t * ls_v
        dyb_ref[...] = dyb.astype(BF16)
        _acc_rows(dbb_ref, dyb, i)
        dgt_ref[:, 0:d] = (dz_v * y_a * ga * (1.0 - ga)).astype(BF16)
        dgt_ref[:, d:2 * d] = (dz_v * (y_b0 * ls_v) * gb * (1.0 - gb)).astype(BF16)

    row = pl.BlockSpec((ts, d), lambda i: (i, 0))
    return _pcall(
        body, name=name,
        out_shape=[_sds((s_len, d), BF16), _sds((s_len, d), BF16), _sds((s_len, 2 * d), BF16)] + [_sds((1, d), F32)] * 3,
        grid=(s_len // ts,),
        in_specs=[pl.BlockSpec((N_CHIPS, ts, npc), lambda i: (0, i, 0)), row, row, row, _vec(d), _vec(d), _vec(d)],
        out_specs=[row, row, pl.BlockSpec((ts, 2 * d), lambda i: (i, 0))] + [_vec(d)] * 3, vmem_mb=48,
    )(proj, dz, ya, yb, b_a, b_b, ls)


def _conv_branch_bwd(name, proj, a1, da3, lg, lb, wc, wp, ts):
    _, s_len, npc = proj.shape
    nq = _chips_covering(2 * wc, npc)
    hb = ts // HALO

    n_tiles = s_len // ts

    def fold(v):
        return jnp.sum(v.reshape(ROW_CHUNK // SUBLANES, SUBLANES, v.shape[-1]), axis=0)

    def body(p_ref, ph_ref, a1_ref, da3_ref, lg_ref, lb_ref, da1_ref, dlg_ref, dlb_ref, dcb_ref, dcw_ref,
             a0s_ref, a0p_ref, vec8_ref, dcw8_ref):
        i = pl.program_id(0)
        _stage_glu(p_ref, ph_ref, a0s_ref, i, wc, npc, ts)
        _make_phases(a0s_ref, a0p_ref)

        @pl.when(i == 0)
        def _():
            vec8_ref[...] = jnp.zeros_like(vec8_ref)
            dcw8_ref[...] = jnp.zeros_like(dcw8_ref)

        def chunk(r0):
            rows = pl.ds(r0, ROW_CHUNK)
            xh, rstd, a2 = _layer_norm(a1_ref[rows, :], lg_ref, lb_ref)
            sig = _sigmoid(a2)
            da2 = da3_ref[rows, :].astype(F32) * (sig * (1.0 + a2 * (1.0 - sig)))
            vec8_ref[0] += fold(da2 * xh)
            vec8_ref[1] += fold(da2)
            dxh = da2 * lg_ref[...]
            da1 = rstd * (dxh - jnp.mean(dxh, axis=-1, keepdims=True)
                          - xh * jnp.mean(dxh * xh, axis=-1, keepdims=True))
            da1_ref[rows, :] = da1
            vec8_ref[2] += fold(da1)
            for k in range(CONV_K):
                dcw8_ref[k] += fold(da1 * _window(a0s_ref, a0p_ref, HALO - CONV_K + 1 + k, r0))

        _for_chunks(ts, chunk)

        @pl.when(i == n_tiles - 1)
        def _():
            dlg_ref[...] = jnp.sum(vec8_ref[0], axis=0, keepdims=True)
            dlb_ref[...] = jnp.sum(vec8_ref[1], axis=0, keepdims=True)
            dcb_ref[...] = jnp.sum(vec8_ref[2], axis=0, keepdims=True)
            dcw_ref[...] = jnp.sum(dcw8_ref[...], axis=1)

    return _pcall(
        body, name=name,
        out_shape=[_sds((s_len, wc), F32)] + [_sds((1, wc), F32)] * 3 + [_sds((HALO, wc), F32)],
        grid=(s_len // ts,),
        in_specs=[pl.BlockSpec((nq, ts, npc), lambda i: (0, i, 0)),
                  pl.BlockSpec((nq, HALO, npc), lambda i: (0, jnp.maximum(i * hb - 1, 0), 0)),
                  pl.BlockSpec((ts, wc), lambda i: (i, 0)), pl.BlockSpec((ts, wc), lambda i: (i, 0)),
                  _vec(wc), _vec(wc)],
        out_specs=[pl.BlockSpec((ts, wc), lambda i: (i, 0)), _vec(wc), _vec(wc), _vec(wc),
                   pl.BlockSpec((HALO, wc), lambda i: (0, 0))],
        scratch=[pltpu.VMEM((HALO + ts, wc), F32), _phase_scratch(HALO + ts, wc),
                 pltpu.VMEM((3, SUBLANES, wc), F32), pltpu.VMEM((HALO, SUBLANES, wc), F32)], vmem_mb=56,
    )(proj, proj, a1, da3, lg, lb)


def _mixer_in_bwd(name, proj, da1, dmixed, dgates, cw, wc, wp, ts):
    _, s_len, npc = proj.shape
    nq = _chips_covering(2 * wc, npc)
    gi = wp // len(POOL_WINDOWS)
    hb = ts // HALO
    n_tiles = s_len // ts
    last_hb = s_len // HALO - 1
    d2 = dgates.shape[1]

    def body(p_ref, d1_ref, d1n_ref, dm_ref, dmn_ref, dgt_ref, cw_ref, o_ref, d1s_ref, es_ref, d1p_ref, ep_ref):
        i = pl.program_id(0)
        more = i < n_tiles - 1
        d1s_ref[0:ts, :] = d1_ref[...]
        d1s_ref[ts:ts + HALO, :] = jnp.where(more, d1n_ref[...], 0.0)
        t_abs = i * ts + lax.broadcasted_iota(jnp.int32, (ts + HALO, 1), 0)
        dm_ext = jnp.concatenate([dm_ref[...].astype(F32), jnp.where(more, dmn_ref[...].astype(F32), 0.0)], axis=0)
        for g, win in enumerate(POOL_WINDOWS):
            cs = slice(g * gi, (g + 1) * gi)
            es_ref[:, cs] = dm_ext[:, cs] / jnp.minimum(t_abs + 1, win).astype(F32)
        _make_phases(d1s_ref, d1p_ref)
        _make_phases(es_ref, ep_ref)

        def chunk(r0):
            rows = pl.ds(r0, ROW_CHUNK)
            da0 = cw_ref[0:1, :] * _window(d1s_ref, d1p_ref, CONV_K - 1, r0)
            for k in range(1, CONV_K):
                da0 = da0 + cw_ref[k:k + 1, :] * _window(d1s_ref, d1p_ref, CONV_K - 1 - k, r0)
            glu_a = _cols(p_ref, 0, wc, npc, rows)
            sig = _sigmoid(_cols(p_ref, wc, 2 * wc, npc, rows))
            _store_cols(o_ref, 0, (da0 * sig).astype(BF16), npc, rows)
            _store_cols(o_ref, wc, (da0 * glu_a * sig * (1.0 - sig)).astype(BF16), npc, rows)
            parts = []
            for g, win in enumerate(POOL_WINDOWS):
                cs = slice(g * gi, (g + 1) * gi)
                acc = _window(es_ref, ep_ref, 0, r0, cs)
                for dlt in range(1, win):
                    acc = acc + _window(es_ref, ep_ref, dlt, r0, cs)
                parts.append(acc - dm_ref[rows, cs].astype(F32))
            _store_cols(o_ref, 2 * wc, jnp.concatenate(parts, axis=-1).astype(BF16), npc, rows)

        _for_chunks(ts, chunk)
        _store_cols(o_ref, 2 * wc + wp, dgt_ref[...], npc)

    nxt = lambda i: (jnp.minimum((i + 1) * hb, last_hb), 0)
    return _pcall(
        body, name=name, out_shape=_sds((N_CHIPS, s_len, npc), BF16), grid=(n_tiles,),
        in_specs=[pl.BlockSpec((nq, ts, npc), lambda i: (0, i, 0)),
                  pl.BlockSpec((ts, wc), lambda i: (i, 0)), pl.BlockSpec((HALO, wc), nxt),
                  pl.BlockSpec((ts, wp), lambda i: (i, 0)), pl.BlockSpec((HALO, wp), nxt),
                  pl.BlockSpec((ts, d2), lambda i: (i, 0)),
                  pl.BlockSpec((HALO, wc), lambda i: (0, 0))],
        out_specs=pl.BlockSpec((N_CHIPS, ts, npc), lambda i: (0, i, 0)),
        scratch=[pltpu.VMEM((ts + HALO, wc), F32), pltpu.VMEM((ts + HALO, wp), F32),
                 _phase_scratch(ts + HALO, wc), _phase_scratch(ts + HALO, wp)], vmem_mb=56,
    )(proj, da1, da1, dmixed, dmixed, dgates, cw)


def _ada_fwd(name, c_all, w, b):
    d, cols = w.shape
    tn = 512 if cols % 512 == 0 else cols

    def body(c_ref, w_ref, b_ref, o_ref):
        cv = c_ref[...]
        sc = (cv * _sigmoid(cv)).astype(BF16)
        o_ref[...] = jnp.dot(sc, w_ref[...].astype(BF16), preferred_element_type=F32) + b_ref[...]

    return _pcall(body, name=name, out_shape=_sds((N_DEV, cols), F32), grid=(cols // tn,),
                  in_specs=[pl.BlockSpec((N_DEV, d), lambda j: (0, 0)), pl.BlockSpec((d, tn), lambda j: (0, j)),
                            pl.BlockSpec((1, tn), lambda j: (0, j))],
                  out_specs=pl.BlockSpec((N_DEV, tn), lambda j: (0, j)), vmem_mb=32)(c_all, w, b)


def _adam_math(w, g, m, v):
    m_new = ADAM_B1 * m + (1.0 - ADAM_B1) * g
    v_new = ADAM_B2 * v + (1.0 - ADAM_B2) * (g * g)
    m_hat = m_new / (1.0 - ADAM_B1 ** ADAM_STEP)
    v_hat = v_new / (1.0 - ADAM_B2 ** ADAM_STEP)
    delta = -ADAM_LR * (m_hat / (jnp.sqrt(v_hat) + ADAM_EPS) + ADAM_WD * w)
    return delta, m_new, v_new


def _adamw(name, w, g, m, v):
    rows, cols = w.shape
    tr = _row_tile(rows, cols, 524288)

    def body(w_ref, g_ref, m_ref, v_ref, go_ref, d_ref, mo_ref, vo_ref):
        g = g_ref[...]
        go_ref[...] = g
        d_ref[...], mo_ref[...], vo_ref[...] = _adam_math(w_ref[...], g, m_ref[...], v_ref[...])

    spec = pl.BlockSpec((tr, cols), lambda i: (i, 0))
    return _pcall(body, name=name, out_shape=[_sds(w.shape, F32)] * 4, grid=(rows // tr,), in_specs=[spec] * 4,
                  out_specs=[spec] * 4, vmem_mb=40)(w, g, m, v)


def _ada_grad_adamw(name, c_t, d_ada, w, m, v):
    rows, cols = w.shape
    tr = _tile(rows, 256)
    tc = _tile(cols, 1536) if cols % 1536 == 0 else cols

    def body(c_ref, da_ref, w_ref, m_ref, v_ref, g_ref, d_ref, mo_ref, vo_ref):
        cv = c_ref[...]
        sc = cv * _sigmoid(cv)
        g = sc[:, 0:1] * da_ref[0:1, :]
        for b in range(1, N_DEV):
            g = g + sc[:, b:b + 1] * da_ref[b:b + 1, :]
        g_ref[...] = g
        d_ref[...], mo_ref[...], vo_ref[...] = _adam_math(w_ref[...], g, m_ref[...], v_ref[...])

    spec = pl.BlockSpec((tr, tc), lambda i, j: (i, j))
    return _pcall(body, name=name, out_shape=[_sds(w.shape, F32)] * 4, grid=(rows // tr, cols // tc),
                  in_specs=[pl.BlockSpec((tr, N_DEV), lambda i, j: (i, 0)),
                            pl.BlockSpec((N_DEV, tc), lambda i, j: (0, j)), spec, spec, spec],
                  out_specs=[spec] * 4, vmem_mb=40)(c_t, d_ada, w, m, v)


def _sum_devices(name, gathered, m_per):
    n = gathered.shape[1]

    def body(g_ref, o_ref):
        acc = g_ref[0:m_per, :]
        for dev in range(1, N_DEV):
            acc = acc + g_ref[dev * m_per:(dev + 1) * m_per, :]
        o_ref[...] = acc

    return _pcall(body, name=name, out_shape=_sds((m_per, n), F32),
                  in_specs=[pl.BlockSpec(memory_space=pltpu.VMEM)],
                  out_specs=pl.BlockSpec(memory_space=pltpu.VMEM))(gathered)


def _ffn_fwd(tag, n, w_in_parts, w_out_after_swiglu, dims):
    s_len, d, f_dim = dims["S"], dims["D"], dims["F"]
    tf = f_dim // 4
    tm0, tm = _tile(s_len, 512), _tile(s_len, 1024)
    p = f_dim // 2

    def ep(accs, ex, outs, cs=slice(None)):
        hh, uu = accs
        sig = _sigmoid(hh)
        silu = hh * sig
        outs[0][0, :, cs] = (uu * (sig + silu * (1.0 - sig))).astype(BF16)
        outs[0][1, :, cs] = silu.astype(BF16)
        outs[1][:, cs] = (silu * uu).astype(BF16)

    done = ()
    for part, (get_w, cols) in enumerate(w_in_parts):
        w_g = get_w().reshape(N_CHIPS * d, p)
        lo, width = cols if cols is not None else (0, p)
        mode_kw = dict(pipeline_mode=pl.Buffered(1)) if cols is None else {}
        el = pl.Element
        done = _matmul(
            f"{tag}_swiglu{part}", n, [w_g, w_g], mode="nn", grid=(2, s_len // tm0, 1),
            a_spec=pl.BlockSpec((tm0, d), lambda j, i, k: (i, 0)),
            b_specs=[pl.BlockSpec((el(d), el(width)), lambda j, i, k, lo=lo: (_mult(j * d, d), lo), **mode_kw),
                     pl.BlockSpec((el(d), el(width)), lambda j, i, k, lo=lo: (_mult((2 + j) * d, d), lo), **mode_kw)],
            out_shape=[_sds((2, s_len, f_dim), BF16), _sds((s_len, f_dim), BF16)],
            out_specs=[pl.BlockSpec((el(2), el(tm0), el(width)),
                                    lambda j, i, k, lo=lo: (0, _mult(i * tm0, tm0), _mult(j * p + lo, LANES))),
                       pl.BlockSpec((el(tm0), el(width)),
                                    lambda j, i, k, lo=lo: (_mult(i * tm0, tm0), _mult(j * p + lo, LANES)))],
            acc_shape=(tm0, width), epilogue=ep, carry=done, col_block=512 if cols is None else None)
    hu, act = done
    w_out2d = w_out_after_swiglu()
    tn2 = _tile(d, 1024)
    f = _matmul(
        f"{tag}_down", act, [w_out2d], mode="nn", grid=(s_len // tm, d // tn2, 2),
        a_spec=pl.BlockSpec((tm, 2 * tf), lambda i, j, k: (i, k)),
        b_specs=[pl.BlockSpec((2 * tf, tn2), lambda i, j, k: (k, j))],
        out_shape=_sds((s_len, d), F32), out_specs=pl.BlockSpec((tm, tn2), lambda i, j, k: (i, j)),
        acc_shape=(tm, tn2), epilogue=_ep_store(F32))
    return hu, act, f, w_out2d


def _ffn_bwd(tag, n, hu, act, df, w_in_g, w_out2d, dims, after_dw_out, after_dw_in):
    s_len, d, f_dim = dims["S"], dims["D"], dims["F"]
    tf = f_dim // 4
    tk = _tile(s_len, 2048)
    tn = _tile(d, 1024)
    g_out = _matmul(
        f"{tag}_dw_out", act, [df], mode="tn", grid=(4, d // tn, s_len // tk),
        a_spec=pl.BlockSpec((tk, tf), lambda i, j, k: (k, i)),
        b_specs=[pl.BlockSpec((tk, tn), lambda i, j, k: (k, j))],
        out_shape=_sds((2, 4, tf // 2, d), F32),
        out_specs=pl.BlockSpec((2, None, tf // 2, tn), lambda i, j, k: (0, i, 0, j)),
        acc_shape=(tf, tn), epilogue=_ep_halves(tf // 2))
    after_dw_out(g_out)

    def ep_dhu(accs, ex, outs):
        da = accs[0]
        outs[0][0] = (da * ex[0][0].astype(F32)).astype(BF16)
        outs[0][1] = (da * ex[0][1].astype(F32)).astype(BF16)

    tm = _tile(s_len, 512)
    hu_spec = pl.BlockSpec((2, tm, 2 * tf), lambda j, i, k: (0, i, j))
    dhu = _matmul(
        f"{tag}_dhu", df, [w_out2d], mode="nt", grid=(2, s_len // tm, 1),
        a_spec=pl.BlockSpec((tm, d), lambda j, i, k: (i, 0)),
        b_specs=[pl.BlockSpec((2 * tf, d), lambda j, i, k: (j, 0), pipeline_mode=pl.Buffered(1))],
        extras=[hu], extra_specs=[hu_spec],
        out_shape=_sds((2, s_len, f_dim), BF16), out_specs=hu_spec, acc_shape=(tm, 2 * tf), epilogue=ep_dhu)

    hd = d // 2
    rt = hd // 2
    g_in = _matmul(
        f"{tag}_dw_in", n, [dhu], mode="tn", grid=(N_CHIPS, 4, s_len // tk),
        a_spec=pl.BlockSpec((tk, rt), lambda j, i, k: (k, i)),
        b_specs=[pl.BlockSpec((None, tk, 2 * tf), lambda j, i, k: (j // 2, k, j % 2))],
        out_shape=_sds((2, 4, hd, f_dim // 2), F32),
        out_specs=pl.BlockSpec((None, None, rt, 2 * tf), lambda j, i, k: (i // 2, j, i % 2, 0)),
        acc_shape=(rt, 2 * tf), epilogue=_ep_store(F32))
    after_dw_in(g_in)

    tm2 = _tile(s_len, 1024)
    dn = _matmul(
        f"{tag}_dn", dhu, [w_in_g], mode="nt", grid=(s_len // tm2, d // tn, N_CHIPS),
        a_spec=pl.BlockSpec((None, tm2, 2 * tf), lambda i, j, k: (k // 2, i, k % 2)),
        b_specs=[pl.BlockSpec((None, tn, 2 * tf), lambda i, j, k: (k, j, 0))],
        out_shape=_sds((s_len, d), BF16), out_specs=pl.BlockSpec((tm2, tn), lambda i, j, k: (i, j)),
        acc_shape=(tm2, tn), epilogue=_ep_store(BF16))
    return dn


def kernel(x, c, w_ada, b_ada, g_ffn1, w1_in, w1_out, g_mix, w_in, conv_w, conv_b, ln_a_g, ln_a_b, w_a_out, b_a_out, w_b_group, b_b_group, ls_b, w_out, g_ffn2, w2_in, w2_out, g_final, loss_target, m_w_ada, m_b_ada, m_g_ffn1, m_w1_in, m_w1_out, m_g_mix, m_w_in, m_conv_w, m_conv_b, m_ln_a_g, m_ln_a_b, m_w_a_out, m_b_a_out, m_w_b_group, m_b_b_group, m_ls_b, m_w_out, m_g_ffn2, m_w2_in, m_w2_out, m_g_final, v_w_ada, v_b_ada, v_g_ffn1, v_w1_in, v_w1_out, v_g_mix, v_w_in, v_conv_w, v_conv_b, v_ln_a_g, v_ln_a_b, v_w_a_out, v_b_a_out, v_w_b_group, v_b_b_group, v_ls_b, v_w_out, v_g_ffn2, v_w2_in, v_w2_out, v_g_final):
    weights = dict(w_ada=w_ada, b_ada=b_ada, g_ffn1=g_ffn1, w1_in=w1_in, w1_out=w1_out, g_mix=g_mix, w_in=w_in,
                   conv_w=conv_w, conv_b=conv_b, ln_a_g=ln_a_g, ln_a_b=ln_a_b, w_a_out=w_a_out, b_a_out=b_a_out,
                   w_b_group=w_b_group, b_b_group=b_b_group, ls_b=ls_b, w_out=w_out, g_ffn2=g_ffn2, w2_in=w2_in,
                   w2_out=w2_out, g_final=g_final)
    mom1 = dict(w_ada=m_w_ada, b_ada=m_b_ada, g_ffn1=m_g_ffn1, w1_in=m_w1_in, w1_out=m_w1_out, g_mix=m_g_mix,
                w_in=m_w_in, conv_w=m_conv_w, conv_b=m_conv_b, ln_a_g=m_ln_a_g, ln_a_b=m_ln_a_b, w_a_out=m_w_a_out,
                b_a_out=m_b_a_out, w_b_group=m_w_b_group, b_b_group=m_b_b_group, ls_b=m_ls_b, w_out=m_w_out,
                g_ffn2=m_g_ffn2, w2_in=m_w2_in, w2_out=m_w2_out, g_final=m_g_final)
    mom2 = dict(w_ada=v_w_ada, b_ada=v_b_ada, g_ffn1=v_g_ffn1, w1_in=v_w1_in, w1_out=v_w1_out, g_mix=v_g_mix,
                w_in=v_w_in, conv_w=v_conv_w, conv_b=v_conv_b, ln_a_g=v_ln_a_g, ln_a_b=v_ln_a_b, w_a_out=v_w_a_out,
                b_a_out=v_b_a_out, w_b_group=v_w_b_group, b_b_group=v_b_b_group, ls_b=v_ls_b, w_out=v_w_out,
                g_ffn2=v_g_ffn2, w2_in=v_w2_in, w2_out=v_w2_out, g_final=v_g_final)
    order = list(weights)

    s_len, d = x.shape[1], x.shape[2]
    f_dim = w1_out.shape[0] * N_CHIPS
    wc = conv_w.shape[1] * N_CHIPS
    wp = w_b_group.shape[0] * w_b_group.shape[1]
    n_groups, gi, goq = w_b_group.shape
    npc = w_in.shape[1]
    ada_c = w_ada.shape[1]
    dims = dict(S=s_len, D=d, F=f_dim)
    ts = _tile(s_len, 256)

    xi, yi, ci = lax.axis_index("x"), lax.axis_index("y"), lax.axis_index("c")
    q = 2 * xi + yi
    dev = 2 * q + ci
    q_idx = jnp.reshape(q, (1,)).astype(jnp.int32)
    qc_idx = jnp.stack([q, ci]).astype(jnp.int32)
    _PREVIOUS.clear()

    cwq = conv_w.shape[1]
    pack0 = jnp.concatenate([c.reshape(-1), conv_w.reshape(-1), b_b_group.reshape(-1)])
    n0 = -(-pack0.shape[0] // (8 * LANES)) * LANES
    pack0 = jnp.pad(pack0, (0, 8 * n0 - pack0.shape[0])).reshape(8, n0)
    g0 = _allgather_small("gather_small_in", pack0).reshape(N_DEV, 8 * n0)
    c_all = g0[:, :d]
    south = g0[0::2]
    cw_full = jnp.concatenate([south[k, d:d + CONV_K * cwq].reshape(CONV_K, cwq) for k in range(N_CHIPS)], axis=1)
    cw_pad = jnp.pad(cw_full, ((0, HALO - CONV_K), (0, 0)))
    o_bb = d + CONV_K * cwq
    bb_full = jnp.concatenate([south[k, o_bb:o_bb + n_groups * goq].reshape(n_groups, goq) for k in range(N_CHIPS)],
                              axis=1).reshape(1, d)

    as2d = lambda a: a.reshape(-1, a.shape[-1])
    groups = dict(w1_out=["w1_out"], mix=["w_a_out", "w_b_group", "w_out"], w2_in=["w2_in"], w2_out=["w2_out"])
    big = ["w1_in", "w1_out", "w_in", "w_a_out", "w_b_group", "w_out", "w2_in", "w2_out"]
    cast = lambda nm: _cast_into_gathered(f"cast_{nm}", as2d(weights[nm]), q_idx)
    w1_in_gather = _TwoPartGather("w1_in", cast("w1_in"))

    b_ada_mine = lax.dynamic_slice(b_ada, (q * ada_c,), (ada_c,)).reshape(1, ada_c)
    ada_piece = _ada_fwd("ada_fwd", c_all, w_ada, b_ada_mine)
    casts = {nm: cast(nm) for nm in big[1:]}
    g1 = _allgather_small("gather_ada", ada_piece).reshape(N_DEV, N_DEV, ada_c)
    w1_in_gather.start_second()
    ici = {}
    for grp, names in groups.items():
        ici[grp] = _gather_ici(f"gather_{grp}_ici", [casts[nm] for nm in names])
        if grp == "w1_out":
            w_in_gather = _TwoPartGather("w_in", casts["w_in"])
            w_in_gather.start_second()
    ada_rows = lax.dynamic_index_in_dim(g1[0::2], dev, axis=1, keepdims=False)
    ada = ada_rows.reshape(3, 3, 1, d)
    (sh1, sc1, gt1), (sh2, sc2, gt2), (sh3, sc3, gt3) = [[ada[i, j] for j in range(3)] for i in range(3)]

    row = lambda vct: vct.reshape(1, -1)
    g1v, gmv, g2v, gfv = row(g_ffn1), row(g_mix), row(g_ffn2), row(g_final)

    def arrived(grp):
        return _gather_d2d(f"gather_{grp}_d2d", ici[grp].wait())

    def gathered(fwd, grp):
        return {nm: g.reshape(N_CHIPS, 2 * g.shape[2], g.shape[3]) for nm, g in zip(groups[grp], fwd.wait())}

    x2 = x[0]
    tgt = loss_target[0]

    n1 = _norm_mod("ffn1_norm", x2, g1v, sc1, sh1, ts)
    fwd, w1_in_parts = {}, []

    def w1_in_part(part):
        def get():
            w1_in_gather.arrive(part)
            w1_in_parts.append(w1_in_gather.ready(part))
            return w1_in_parts[-1]
        return get

    def w1_out_after_swiglu():
        fwd["w1_out"] = arrived("w1_out")
        w_in_gather.arrive(0)
        return gathered(fwd["w1_out"], "w1_out")["w1_out"].reshape(f_dim, d)

    hu1, act1, f1, w1_out_2d = _ffn_fwd(
        "ffn1", n1, [(w1_in_part(part), w1_in_gather.parts[part]) for part in range(2)], w1_out_after_swiglu, dims)
    w1_in_g = w1_in_parts[-1]
    h1, n2 = _residual_norm_mod("mix_norm", x2, f1, gt1, 0.5, gmv, sc2, sh2, ts)

    tm = _tile(s_len, 1024)
    tnp = npc // 2
    proj = ()
    for part in range(2):
        if part:
            w_in_gather.arrive(part)
        w_in_g = w_in_gather.ready(part)
        lo, width = w_in_gather.parts[part]
        el = pl.Element
        proj = (_matmul(
            f"mix_proj{part}", n2, [w_in_g.reshape(N_CHIPS * d, npc)], mode="nn", grid=(s_len // tm, N_CHIPS, 1),
            a_spec=pl.BlockSpec((tm, d), lambda i, j, k: (i, 0)),
            b_specs=[pl.BlockSpec((el(d), el(width)), lambda i, j, k, lo=lo: (_mult(j * d, d), lo))],
            out_shape=_sds((N_CHIPS * s_len, npc), BF16),
            out_specs=pl.BlockSpec((el(tm), el(width)), lambda i, j, k, lo=lo: (_mult(j * s_len + i * tm, tm), lo)),
            acc_shape=(tm, width), epilogue=_ep_store(BF16), carry=proj),)
    proj = proj[0].reshape(N_CHIPS, s_len, npc)
    fwd["mix"] = arrived("mix")
    cbv, lgv, lbv = row(conv_b), row(ln_a_g), row(ln_a_b)
    a3, mixed, conv_out = _mixer_mid("mix_mid", proj, cw_pad, cbv, lgv, lbv, wc, wp, ts)
    wts = gathered(fwd["mix"], "mix")
    w_out_2d = wts["w_out"].reshape(d, d)
    w_a_g = wts["w_a_out"]
    w_b_r = _regroup("regroup_w_b", wts["w_b_group"], n_groups)
    dq = d // N_CHIPS
    ya = _matmul(
        "mix_ya", a3, [w_a_g], mode="nn", grid=(s_len // tm, N_CHIPS, 1),
        a_spec=pl.BlockSpec((tm, wc), lambda i, j, k: (i, 0)),
        b_specs=[pl.BlockSpec((None, wc, dq), lambda i, j, k: (j, 0, 0))],
        out_shape=_sds((s_len, d), BF16), out_specs=pl.BlockSpec((tm, dq), lambda i, j, k: (i, j)),
        acc_shape=(tm, dq), epilogue=_ep_store(BF16))
    yb = _matmul(
        "mix_yb", mixed, [w_b_r], mode="nn", grid=(s_len // tm, n_groups, 1),
        a_spec=pl.BlockSpec((tm, gi), lambda i, j, k: (i, j)),
        b_specs=[pl.BlockSpec((None, gi, dq), lambda i, j, k: (j, 0, 0))],
        out_shape=_sds((s_len, d), BF16), out_specs=pl.BlockSpec((tm, dq), lambda i, j, k: (i, j)),
        acc_shape=(tm, dq), epilogue=_ep_store(BF16))
    bav, lsv = row(b_a_out), row(ls_b)
    z = _gates_fwd("mix_gates", proj, ya, yb, bav, bb_full, lsv, wc, wp, ts)
    tn = _tile(d, 1024)
    mix = _matmul(
        "mix_out", z, [w_out_2d], mode="nn", grid=(s_len // tm, d // tn, 1),
        a_spec=pl.BlockSpec((tm, d), lambda i, j, k: (i, 0)),
        b_specs=[pl.BlockSpec((d, tn), lambda i, j, k: (0, j))],
        out_shape=_sds((s_len, d), F32), out_specs=pl.BlockSpec((tm, tn), lambda i, j, k: (i, j)),
        acc_shape=(tm, tn), epilogue=_ep_store(F32))
    fwd["w2_in"] = arrived("w2_in")
    h2, n3 = _residual_norm_mod("ffn2_norm", h1, mix, gt2, 1.0, g2v, sc3, sh3, ts)
    w2_in_g = gathered(fwd["w2_in"], "w2_in")["w2_in"]
    hu2, act2, f3, w2_out_2d = _ffn_fwd(
        "ffn2", n3, [(lambda: w2_in_g, None)],
        lambda: gathered(arrived("w2_out"), "w2_out")["w2_out"].reshape(f_dim, d), dims)

    dh3, df3, d_gf, d_gt3, loss_cols = _final_loss("final_loss", h2, f3, tgt, gt3, 0.5, gfv, ts)
    rs, held = {}, {}
    dn3 = _ffn_bwd(
        "ffn2", n3, hu2, act2, df3, w2_in_g, w2_out_2d, dims,
        after_dw_out=lambda g: held.update(w2_out=g),
        after_dw_in=lambda g: rs.update(ffn2=_ReduceScatter("g_ffn2", ["w2_out", "w2_in"], [held["w2_out"], g],
                                                            qc_idx)))
    dh2, dmix, d_sh3, d_sc3, d_g2, d_gt2 = _norm_mod_bwd("ffn2_norm_bwd", h2, dn3, dh3, g2v, sc3, ts,
                                                         prev=(mix, gt2, 1.0))
    rs["ffn2"].step2()

    tk = s_len
    hq = d // (2 * N_CHIPS)
    gw_out = _matmul(
        "mix_dw_out", z, [dmix], mode="tn", grid=(N_CHIPS, d // tn, s_len // tk),
        a_spec=pl.BlockSpec((tk, 2 * hq), lambda i, j, k: (k, i)),
        b_specs=[pl.BlockSpec((tk, tn), lambda i, j, k: (k, j))],
        out_shape=_sds((2, N_CHIPS, hq, d), F32),
        out_specs=pl.BlockSpec((2, None, hq, tn), lambda i, j, k: (0, i, 0, j)),
        acc_shape=(2 * hq, tn), epilogue=_ep_halves(hq))
    dz = _matmul(
        "mix_dz", dmix, [w_out_2d], mode="nt", grid=(s_len // tm, d // tn, 1),
        a_spec=pl.BlockSpec((tm, d), lambda i, j, k: (i, 0)),
        b_specs=[pl.BlockSpec((tn, d), lambda i, j, k: (j, 0))],
        out_shape=_sds((s_len, d), BF16), out_specs=pl.BlockSpec((tm, tn), lambda i, j, k: (i, j)),
        acc_shape=(tm, tn), epilogue=_ep_store(BF16))
    dya, dyb, dgates, d_ba, d_ls, d_bb = _gates_bwd("mix_gates_bwd", proj, dz, ya, yb, bav, bb_full, lsv, wc, wp, ts)
    gw_a = _matmul(
        "mix_dw_a", a3, [dya], mode="tn", grid=(1, N_CHIPS, s_len // tk),
        a_spec=pl.BlockSpec((tk, wc), lambda i, j, k: (k, 0)),
        b_specs=[pl.BlockSpec((tk, dq), lambda i, j, k: (k, j))],
        out_shape=_sds((2, N_CHIPS, wc // 2, dq), F32),
        out_specs=pl.BlockSpec((2, None, wc // 2, dq), lambda i, j, k: (0, j, 0, 0)),
        acc_shape=(wc, dq), epilogue=_ep_halves(wc // 2))
    da3 = _matmul(
        "mix_da3", dya, [w_a_g], mode="nt", grid=(s_len // tm, 1, N_CHIPS),
        a_spec=pl.BlockSpec((tm, dq), lambda i, j, k: (i, k)),
        b_specs=[pl.BlockSpec((None, wc, dq), lambda i, j, k: (k, 0, 0))],
        out_shape=_sds((s_len, wc), BF16), out_specs=pl.BlockSpec((tm, wc), lambda i, j, k: (i, 0)),
        acc_shape=(tm, wc), epilogue=_ep_store(BF16))
    gpr = n_groups // 2

    def ep_by_chip(accs, ex, outs):
        for k in range(N_CHIPS):
            outs[0][k] = accs[0][:, k * goq:(k + 1) * goq]

    gw_b = _matmul(
        "mix_dw_b", mixed, [dyb], mode="tn", grid=(1, n_groups, s_len // tk),
        a_spec=pl.BlockSpec((tk, gi), lambda i, j, k: (k, j)),
        b_specs=[pl.BlockSpec((tk, dq), lambda i, j, k: (k, j))],
        out_shape=_sds((2, N_CHIPS, gpr * gi, goq), F32),
        out_specs=pl.BlockSpec((None, N_CHIPS, gi, goq), lambda i, j, k: (j // gpr, 0, j % gpr, 0)),
        acc_shape=(gi, dq), epilogue=ep_by_chip)
    dmixed = _matmul(
        "mix_dmixed", dyb, [w_b_r], mode="nt", grid=(s_len // tm, n_groups, 1),
        a_spec=pl.BlockSpec((tm, dq), lambda i, j, k: (i, j)),
        b_specs=[pl.BlockSpec((None, gi, dq), lambda i, j, k: (j, 0, 0))],
        out_shape=_sds((s_len, wp), BF16), out_specs=pl.BlockSpec((tm, gi), lambda i, j, k: (i, j)),
        acc_shape=(tm, gi), epilogue=_ep_store(BF16))
    da1, d_lg, d_lb, d_cb, d_cw = _conv_branch_bwd("mix_conv_bwd", proj, conv_out, da3, lgv, lbv, wc, wp, ts)
    dproj = _mixer_in_bwd("mix_in_bwd", proj, da1, dmixed, dgates, cw_pad, wc, wp, ts)
    hd = d // 2
    rt = hd // 2
    gw_in = _matmul(
        "mix_dw_in", n2, [dproj], mode="tn", grid=(N_CHIPS, 4, 1),
        a_spec=pl.BlockSpec((s_len, rt), lambda j, i, k: (0, i)),
        b_specs=[pl.BlockSpec((None, s_len, npc), lambda j, i, k: (j, 0, 0))],
        out_shape=_sds((2, N_CHIPS, hd, npc), F32),
        out_specs=pl.BlockSpec((None, None, rt, npc), lambda j, i, k: (i // 2, j, i % 2, 0)),
        acc_shape=(rt, npc), epilogue=_ep_store(F32))
    rs["mix"] = _ReduceScatter("g_mix", ["w_in", "w_a_out", "w_b_group", "w_out"], [gw_in, gw_a, gw_b, gw_out],
                               qc_idx)
    rs["ffn2"].step3()
    dn2 = _matmul(
        "mix_dn", dproj, [w_in_g], mode="nt", grid=(s_len // tm, d // tn, N_CHIPS),
        a_spec=pl.BlockSpec((None, tm, npc), lambda i, j, k: (k, i, 0)),
        b_specs=[pl.BlockSpec((None, tn, npc), lambda i, j, k: (k, j, 0))],
        out_shape=_sds((s_len, d), BF16), out_specs=pl.BlockSpec((tm, tn), lambda i, j, k: (i, j)),
        acc_shape=(tm, tn), epilogue=_ep_store(BF16))
    dh1, df1, d_sh2, d_sc2, d_gm, d_gt1 = _norm_mod_bwd("mix_norm_bwd", h1, dn2, dh2, gmv, sc2, ts,
                                                        prev=(f1, gt1, 0.5))
    rs["mix"].step2()

    def w1_in_ready(g):
        rs["w1_in"] = _ReduceScatter("g_w1_in", ["w1_in"], [g], qc_idx)
        rs["w1_out"].step2()
        rs["mix"].step3()

    dn1 = _ffn_bwd(
        "ffn1", n1, hu1, act1, df1, w1_in_g, w1_out_2d, dims,
        after_dw_out=lambda g: rs.update(w1_out=_ReduceScatter("g_w1_out", ["w1_out"], [g], qc_idx)),
        after_dw_in=w1_in_ready)
    grad_x, d_sh1, d_sc1, d_g1 = _norm_mod_bwd("ffn1_norm_bwd", x2, dn1, dh1, g1v, sc1, ts)

    d_ada = jnp.concatenate([d_sh1, d_sc1, d_gt1, d_sh2, d_sc2, d_gt2, d_sh3, d_sc3, d_gt3], axis=1)
    small = [d_ada, d_g1, d_gm, d_cw[:CONV_K].reshape(1, -1), d_cb, d_lg, d_lb, d_ba, d_bb, d_ls, d_g2, d_gf,
             loss_cols]
    sizes = [a.shape[1] for a in small]
    pack1 = jnp.concatenate(small, axis=1).reshape(-1)
    n1p = -(-pack1.shape[0] // (8 * LANES)) * LANES
    pack1 = jnp.pad(pack1, (0, 8 * n1p - pack1.shape[0])).reshape(8, n1p)
    g2 = _allgather_small("gather_small_grads", pack1)
    rs["w1_in"].step2()
    total = _sum_devices("sum_small_grads", g2, 8).reshape(-1)
    offs = [0]
    for sz in sizes:
        offs.append(offs[-1] + sz)
    tot = [total[offs[k]:offs[k + 1]] for k in range(len(sizes))]
    d_ada_all = g2.reshape(N_DEV, 8 * n1p)[:, :sizes[0]]
    loss = jnp.sum(tot[12])

    grads = {}
    grads["b_ada"] = tot[0]
    grads["g_ffn1"], grads["g_mix"] = tot[1], tot[2]
    grads["conv_w"] = lax.dynamic_slice(tot[3].reshape(CONV_K, wc), (0, q * cwq), (CONV_K, cwq))
    grads["conv_b"], grads["ln_a_g"], grads["ln_a_b"], grads["b_a_out"] = tot[4], tot[5], tot[6], tot[7]
    grads["b_b_group"] = lax.dynamic_slice(tot[8].reshape(n_groups, N_CHIPS * goq), (0, q * goq), (n_groups, goq))
    grads["ls_b"], grads["g_ffn2"], grads["g_final"] = tot[9], tot[10], tot[11]

    delta, new_m, new_v = {}, {}, {}

    def adamw_group(reduced):
        for nm, g in reduced.items():
            shp = weights[nm].shape
            go, dl, mo, vo = _adamw(f"adamw_{nm}", as2d(weights[nm]), g, as2d(mom1[nm]), as2d(mom2[nm]))
            grads[nm], delta[nm], new_m[nm], new_v[nm] = go.reshape(shp), dl.reshape(shp), mo.reshape(shp), vo.reshape(shp)

    adamw_group(rs["ffn2"].result())
    rs["w1_out"].step3()
    adamw_group(rs["mix"].result())
    d_ada_mine = lax.dynamic_slice(d_ada_all, (0, q * ada_c), (N_DEV, ada_c))
    grads["w_ada"], delta["w_ada"], new_m["w_ada"], new_v["w_ada"] = _ada_grad_adamw(
        "adamw_w_ada", c_all.T, d_ada_mine, w_ada, m_w_ada, v_w_ada)
    rs["w1_in"].step3()
    smalls = [nm for nm in order if nm not in big and nm != "w_ada"]
    flat = lambda src: jnp.concatenate([src[nm].reshape(-1) for nm in smalls])
    n_small = sum(weights[nm].size for nm in smalls)
    rows_s = -(-n_small // (8 * LANES)) * 8
    packed = [jnp.pad(flat(src), (0, rows_s * LANES - n_small)).reshape(rows_s, LANES)
              for src in (weights, grads, mom1, mom2)]
    _, dl_s, mo_s, vo_s = _adamw("adamw_small", *packed)
    off = 0
    for nm in smalls:
        sz, shp = weights[nm].size, weights[nm].shape
        delta[nm] = dl_s.reshape(-1)[off:off + sz].reshape(shp)
        new_m[nm] = mo_s.reshape(-1)[off:off + sz].reshape(shp)
        new_v[nm] = vo_s.reshape(-1)[off:off + sz].reshape(shp)
        grads[nm] = grads[nm].reshape(shp)
        off += sz
    adamw_group(rs["w1_out"].result())
    adamw_group(rs["w1_in"].result())

    return (loss, grad_x[None], *[grads[nm] for nm in order], *[delta[nm] for nm in order],
            *[new_m[nm] for nm in order], *[new_v[nm] for nm in order])
```

```python
import jax
import jax.numpy as jnp
from jax import lax
from jax.experimental import pallas as pl
from jax.experimental.pallas import tpu as pltpu

F32 = jnp.float32
BF16 = jnp.bfloat16
MESH = pl.DeviceIdType.MESH
ANY = pl.BlockSpec(memory_space=pl.ANY)
HBM = pl.BlockSpec(memory_space=pltpu.HBM)
SEM = pl.BlockSpec(memory_space=pltpu.SEMAPHORE)
EFFECT = pltpu.SideEffectType.DATAFLOW_SIDE_EFFECTING

EPS = 1e-6
CONV_K = 31
HALO = 32
POOL_WINDOWS = (2, 4, 8, 16)
N_CHIPS = 4
N_DEV = 8
LANES = 128

ADAM_LR = 0.001
ADAM_B1 = 0.9
ADAM_B2 = 0.999
ADAM_EPS = 1e-08
ADAM_WD = 0.01
ADAM_STEP = 10

DN = {
    "nn": (((1,), (0,)), ((), ())),
    "nt": (((1,), (1,)), ((), ())),
    "tn": (((0,), (0,)), ((), ())),
}


_PREVIOUS = []


def _ordered(call, args, n_lead, body, token=None, sources=()):
    dep = [pltpu.with_memory_space_constraint(p, pltpu.HBM) if p.size * p.dtype.itemsize >= (1 << 20) else p
           for p in _PREVIOUS if all(p is not a for a in (*args, *sources))]

    def wrapped(*refs):
        return body(*refs[:n_lead], *refs[n_lead + len(dep):])

    outs = call(wrapped, [ANY] * len(dep))(*args, *dep)
    seq = outs if isinstance(outs, (list, tuple)) else [outs]
    _PREVIOUS[:] = [seq[token] if token is not None else
                    next(o for o in seq if jnp.issubdtype(o.dtype, jnp.floating))]
    return outs


def _pcall(body, *, name, out_shape, grid=None, in_specs=None, out_specs=None, scratch=(), aliases=None,
           prefetch=0, vmem_mb=None):
    params = {}
    if grid is not None:
        params["dimension_semantics"] = ("arbitrary",) * len(grid)
    if vmem_mb is not None:
        params["vmem_limit_bytes"] = vmem_mb << 20
    def in_hbm(shape, spec):
        big = shape.size * jnp.dtype(shape.dtype).itemsize >= (1 << 20)
        return pltpu.HBM(shape.shape, shape.dtype) if big and getattr(spec, "memory_space", None) != pltpu.VMEM else shape

    if isinstance(out_shape, (list, tuple)):
        out_shape = [in_hbm(s, sp) for s, sp in zip(out_shape, out_specs)]
    else:
        out_shape = in_hbm(out_shape, out_specs)
    kw = dict(name=name, out_shape=out_shape, compiler_params=pltpu.CompilerParams(**params))
    if aliases:
        kw["input_output_aliases"] = aliases

    def call(wrapped, dep_specs):
        specs = list(in_specs) + dep_specs
        if prefetch:
            return pl.pallas_call(wrapped, grid_spec=pltpu.PrefetchScalarGridSpec(
                num_scalar_prefetch=prefetch, grid=grid, in_specs=specs, out_specs=out_specs,
                scratch_shapes=list(scratch)), **kw)
        if grid is not None:
            return pl.pallas_call(wrapped, grid=grid, in_specs=specs, out_specs=out_specs,
                                  scratch_shapes=list(scratch), **kw)
        return pl.pallas_call(wrapped, in_specs=specs, out_specs=out_specs, scratch_shapes=list(scratch), **kw)

    def run(*args):
        specs = [None] * prefetch + list(in_specs)
        placed = [pltpu.with_memory_space_constraint(a, pltpu.HBM)
                  if a.size * a.dtype.itemsize >= (1 << 20) and getattr(s, "memory_space", None) != pltpu.VMEM else a
                  for a, s in zip(args, specs)]
        return _ordered(call, placed, prefetch + len(in_specs), body, sources=args)

    return run


def _mult(offset, unit):
    return pl.multiple_of(offset, unit)


def _tile(dim, pref):
    t = min(dim, pref)
    assert dim % t == 0, (dim, pref)
    return t


def _sds(shape, dtype):
    return jax.ShapeDtypeStruct(tuple(shape), dtype)


def _sigmoid(v):
    return 0.5 * jnp.tanh(0.5 * v) + 0.5


def _vec(w):
    return pl.BlockSpec((1, w), lambda *_: (0, 0))


def _acc_rows(ref, val, i):
    @pl.when(i == 0)
    def _():
        ref[...] = jnp.zeros_like(ref)

    ref[...] += jnp.sum(val, axis=0, keepdims=True)


def _matmul(name, a, bs, *, mode, grid, a_spec, b_specs, out_shape, out_specs, acc_shape, epilogue,
            extras=(), extra_specs=(), vmem_mb=56, carry=(), col_block=None):
    nb, ne, nk, nc = len(bs), len(extras), grid[2], len(carry)
    dn = DN[mode]

    def body(*all_refs):
        refs = all_refs[:1 + nb + ne] + all_refs[1 + nb + ne + nc:]
        a_ref, b_refs, ex = refs[0], refs[1:1 + nb], refs[1 + nb:1 + nb + ne]
        if col_block:
            outs, av, width = refs[1 + nb + ne:], a_ref[...], b_refs[0].shape[-1]
            for lo in range(0, width, col_block):
                cs = slice(lo, min(lo + col_block, width))
                epilogue([lax.dot_general(av, b[:, cs], dn, preferred_element_type=F32) for b in b_refs], ex, outs, cs)
            return
        if nk == 1:
            outs = refs[1 + nb + ne:]
            accs = [lax.dot_general(a_ref[...], b[...], dn, preferred_element_type=F32) for b in b_refs]
            epilogue(accs, ex, outs)
            return
        outs, acc_refs = refs[1 + nb + ne:-nb], refs[-nb:]
        k = pl.program_id(2)

        @pl.when(k == 0)
        def _():
            for acc in acc_refs:
                acc[...] = jnp.zeros_like(acc)

        for acc, b in zip(acc_refs, b_refs):
            acc[...] += lax.dot_general(a_ref[...], b[...], dn, preferred_element_type=F32)

        @pl.when(k == nk - 1)
        def _():
            epilogue([acc[...] for acc in acc_refs], ex, outs)

    scratch = [pltpu.VMEM(acc_shape, F32) for _ in range(nb)] if nk > 1 else []
    return _pcall(body, name=name, out_shape=out_shape, grid=grid,
                  in_specs=[a_spec, *b_specs, *extra_specs, *[ANY] * nc], out_specs=out_specs, scratch=scratch,
                  aliases={1 + nb + ne + i: i for i in range(nc)}, vmem_mb=vmem_mb)(a, *bs, *extras, *carry)


def _ep_store(dtype):
    def ep(accs, ex, outs):
        outs[0][...] = accs[0].astype(dtype)
    return ep


def _ep_halves(h):
    def ep(accs, ex, outs):
        outs[0][0] = accs[0][:h]
        outs[0][1] = accs[0][h:]
    return ep


def _place():
    x, y, c = lax.axis_index("x"), lax.axis_index("y"), lax.axis_index("c")
    chips = [(1 - x, y), (x, 1 - y), (1 - x, 1 - y)]
    return x, y, c, chips


def _allgather_small(name, block):
    m_per, n = block.shape

    def body(x_ref, out_ref, send_sems, recv_sems, local_sem):
        x, y, c, chips = _place()
        me, sibling = (x, y, c), (x, y, 1 - c)

        def rows(px, py, pc):
            return out_ref.at[pl.ds((4 * px + 2 * py + pc) * m_per, m_per), :]

        def copy(k, blk, to, src=None):
            return pltpu.make_async_remote_copy(
                src_ref=rows(*blk) if src is None else src, dst_ref=rows(*blk),
                send_sem=send_sems.at[k], recv_sem=recv_sems.at[k], device_id=to, device_id_type=MESH)

        mine = pltpu.make_async_copy(x_ref, rows(*me), local_sem)
        mine.start()
        first = [copy(0, me, sibling, src=x_ref)]
        first += [copy(1 + j, me, (*chip, c), src=x_ref) for j, chip in enumerate(chips)]
        for cp in first:
            cp.start()
        passed = [copy(4 + j, (*chip, c), sibling) for j, chip in enumerate(chips)]
        for j, chip in enumerate(chips):
            copy(1 + j, (*chip, c), me).wait_recv()
            passed[j].start()
        copy(0, sibling, me).wait_recv()
        for j, chip in enumerate(chips):
            copy(4 + j, (*chip, 1 - c), me).wait_recv()
        for cp in first + passed:
            cp.wait_send()
        mine.wait()

    return _pcall(
        body, name=name, out_shape=_sds((N_DEV * m_per, n), block.dtype),
        in_specs=[pl.BlockSpec(memory_space=pltpu.VMEM)], out_specs=pl.BlockSpec(memory_space=pltpu.VMEM),
        scratch=[pltpu.SemaphoreType.DMA((7,)), pltpu.SemaphoreType.DMA((7,)), pltpu.SemaphoreType.DMA],
    )(block)


class _SplitCopies:
    def __init__(self, name, arrays, plan, n_copies):
        self.name, self.plan, self.n = name, plan, len(arrays)
        n = self.n

        def body(*refs):
            send, recv, token = refs[n], refs[n + 1], refs[-1]
            for k, (src, dst, _, peer) in enumerate(plan(refs[:n])):
                pltpu.make_async_remote_copy(src_ref=src, dst_ref=dst, send_sem=send.at[k], recv_sem=recv.at[k],
                                             device_id=peer, device_id_type=MESH).start()
            token[...] = jnp.zeros_like(token)

        def call(wrapped, dep_specs):
            return pl.pallas_call(
                wrapped, name=f"{name}_start",
                out_shape=(pltpu.SemaphoreType.DMA((n_copies,)), pltpu.SemaphoreType.DMA((n_copies,)),
                           *[pltpu.HBM(a.shape, a.dtype) for a in arrays], _sds((8, LANES), F32)),
                in_specs=[HBM] * n + dep_specs,
                out_specs=(SEM, SEM, *[HBM] * n, pl.BlockSpec(memory_space=pltpu.VMEM)),
                input_output_aliases={i: 2 + i for i in range(n)},
                compiler_params=pltpu.CompilerParams(has_side_effects=EFFECT))

        outs = _ordered(call, [pltpu.with_memory_space_constraint(a, pltpu.HBM) for a in arrays], n, body, token=-1,
                        sources=arrays)
        self.send, self.recv, self.arrays = outs[0], outs[1], list(outs[2:2 + n])

    def wait(self, arrays=None):
        n, plan = self.n, self.plan
        if arrays is not None:
            self.arrays = list(arrays)

        def body(*refs):
            send, recv, token = refs[n], refs[n + 1], refs[-1]
            for k, (src, _, landing, peer) in enumerate(plan(refs[:n])):
                cp = pltpu.make_async_remote_copy(src_ref=src, dst_ref=landing, send_sem=send.at[k],
                                                  recv_sem=recv.at[k], device_id=peer, device_id_type=MESH)
                cp.wait_send()
                cp.wait_recv()
            token[...] = jnp.zeros_like(token)

        def call(wrapped, dep_specs):
            return pl.pallas_call(
                wrapped, name=f"{self.name}_wait",
                out_shape=(*[pltpu.HBM(a.shape, a.dtype) for a in self.arrays], _sds((8, LANES), F32)),
                in_specs=[HBM] * n + [SEM, SEM] + dep_specs,
                out_specs=(*[HBM] * n, pl.BlockSpec(memory_space=pltpu.VMEM)),
                input_output_aliases={i: i for i in range(n)},
                compiler_params=pltpu.CompilerParams(has_side_effects=EFFECT))

        return list(_ordered(call, [*self.arrays, self.send, self.recv], n + 2, body, token=-1))[:n]


def _col_range(g, cols):
    lo, width = cols if cols is not None else (0, g.shape[-1])
    return (slice(None), pl.ds(lo, width))


def _gather_ici(name, gathered, cols=None):
    def plan(refs):
        x, y, c, chips = _place()
        q = 2 * x + y
        return [(g.at[(q, c, *_col_range(g, cols))], g.at[(q, c, *_col_range(g, cols))],
                 g.at[(2 * px + py, c, *_col_range(g, cols))], (px, py, c))
                for g in refs for px, py in chips]

    return _SplitCopies(name, gathered, plan, 3 * len(gathered))


def _gather_d2d(name, gathered, cols=None):
    def plan(refs):
        x, y, c, chips = _place()
        return [(g.at[(2 * px + py, c, *_col_range(g, cols))], g.at[(2 * px + py, c, *_col_range(g, cols))],
                 g.at[(2 * px + py, 1 - c, *_col_range(g, cols))], (x, y, 1 - c))
                for g in refs for px, py in chips]

    return _SplitCopies(name, gathered, plan, 3 * len(gathered))


MXU_COLS = 256


def _two_parts(width):
    passes = width // MXU_COLS
    first = (passes // 2) * MXU_COLS if width % MXU_COLS == 0 and passes >= 2 else width // 2
    return [(0, first), (first, width - first)]


class _TwoPartGather:
    def __init__(self, name, gathered):
        self.name, self.d2d = name, {}
        self.parts = _two_parts(gathered.shape[-1])
        self.ici = [_gather_ici(f"gather_{name}_a_ici", [gathered], self.parts[0])]
        self.buf = self.ici[0].arrays

    def start_second(self):
        self.ici.append(_gather_ici(f"gather_{self.name}_b_ici", self.buf, self.parts[1]))
        self.buf = self.ici[1].arrays

    def arrive(self, part):
        here = self.ici[part].wait(self.buf)
        self.d2d[part] = _gather_d2d(f"gather_{self.name}_{'ab'[part]}_d2d", here, self.parts[part])
        self.buf = self.d2d[part].arrays

    def ready(self, part):
        self.buf = self.d2d[part].wait(self.buf)
        g = self.buf[0]
        return g.reshape(N_CHIPS, 2 * g.shape[2], g.shape[3])


def _scatter_sibling(name, grads):
    n = len(grads)

    def plan(refs):
        x, y, c, _ = _place()
        return [(refs[w].at[1 - c], refs[n + w], refs[n + w], (x, y, 1 - c)) for w in range(n)]

    landing = [lax.empty(g.shape[1:], g.dtype) for g in grads]
    return _SplitCopies(name, [*grads, *landing], plan, n)


def _scatter_chips(name, sums):
    n = len(sums)

    def plan(refs):
        x, y, c, chips = _place()
        return [(refs[w].at[2 * px + py], refs[n + w].at[j], refs[n + w].at[j], (px, py, c))
                for w in range(n) for j, (px, py) in enumerate(chips)]

    landing = [lax.empty((3, *s.shape[1:]), s.dtype) for s in sums]
    return _SplitCopies(name, [*sums, *landing], plan, 3 * n)


def _share_final(name, finals):
    def plan(refs):
        x, y, c, _ = _place()
        return [(f.at[c], f.at[c], f.at[1 - c], (x, y, 1 - c)) for f in refs]

    return _SplitCopies(name, finals, plan, len(finals))


def _row_tile(rows, cols, budget_elems=786432):
    best = 8
    for t in range(8, rows + 1, 8):
        if rows % t == 0 and t * cols <= budget_elems:
            best = t
    return best if rows % best == 0 else rows


def _sum_with_sibling(name, grad, recv, qc_idx):
    _, _, h, cols = grad.shape
    tr = _row_tile(h, cols)

    def body(s_ref, g_ref, r_ref, own_ref, pb_ref):
        p = g_ref[...] + r_ref[...]
        pb_ref[...] = p.astype(BF16)

        @pl.when(pl.program_id(1) == s_ref[0])
        def _():
            own_ref[...] = p

    blk = pl.BlockSpec((None, tr, cols), lambda r, k, s: (k, r, 0))
    return _pcall(
        body, name=name, out_shape=[_sds((h, cols), F32), _sds((N_CHIPS, h, cols), BF16)],
        grid=(h // tr, N_CHIPS), prefetch=1,
        in_specs=[pl.BlockSpec((None, None, tr, cols), lambda r, k, s: (s[1], k, r, 0)), blk],
        out_specs=[pl.BlockSpec((tr, cols), lambda r, k, s: (r, 0)), blk], vmem_mb=32,
    )(qc_idx, grad, recv)


def _sum_chips(name, own, recv, qc_idx):
    h, cols = own.shape
    tr = _row_tile(h, cols)

    def body(s_ref, p_ref, t_ref, o_ref):
        o_ref[...] = ((p_ref[...] + t_ref[0].astype(F32)) + t_ref[1].astype(F32)) + t_ref[2].astype(F32)

    return _pcall(
        body, name=name, out_shape=_sds((2, h, cols), F32), grid=(h // tr,), prefetch=1,
        in_specs=[pl.BlockSpec((tr, cols), lambda r, s: (r, 0)),
                  pl.BlockSpec((3, tr, cols), lambda r, s: (0, r, 0))],
        out_specs=pl.BlockSpec((None, tr, cols), lambda r, s: (s[1], r, 0)), vmem_mb=32,
    )(qc_idx, own, recv)


class _ReduceScatter:
    def __init__(self, tag, names, grads, qc_idx):
        self.tag, self.names, self.n, self.qc_idx = tag, names, len(grads), qc_idx
        self.copies = _scatter_sibling(f"{tag}_rs_sibling", grads)

    def step2(self):
        n = self.n
        arrs = self.copies.wait()
        sums = [_sum_with_sibling(f"{nm}_sum_sibling", arrs[w], arrs[n + w], self.qc_idx)
                for w, nm in enumerate(self.names)]
        self.own = [s[0] for s in sums]
        self.copies = _scatter_chips(f"{self.tag}_rs_chips", [s[1] for s in sums])

    def step3(self):
        n = self.n
        arrs = self.copies.wait()
        finals = [_sum_chips(f"{nm}_sum_chips", self.own[w], arrs[n + w], self.qc_idx)
                  for w, nm in enumerate(self.names)]
        self.copies = _share_final(f"{self.tag}_rs_final", finals)

    def result(self):
        return {nm: f.reshape(2 * f.shape[1], f.shape[2]) for nm, f in zip(self.names, self.copies.wait())}


def _cast_into_gathered(name, w, q_idx):
    rows, cols = w.shape
    h = rows // 2
    tr = _row_tile(h, cols, 1 << 20)
    nr = h // tr

    def body(s_ref, w_ref, o_ref):
        o_ref[...] = w_ref[...].astype(BF16)

    return _pcall(body, name=name, out_shape=_sds((N_CHIPS, 2, h, cols), BF16), grid=(2, nr), prefetch=1,
                  in_specs=[pl.BlockSpec((tr, cols), lambda hf, r, s: (hf * nr + r, 0))],
                  out_specs=pl.BlockSpec((None, None, tr, cols), lambda hf, r, s: (s[0], hf, r, 0)),
                  vmem_mb=32)(q_idx, w)


def _regroup(name, w, n_groups):
    n_chips, rows, goq = w.shape
    gi = rows // n_groups

    def body(w_ref, o_ref):
        o_ref[...] = w_ref[...]

    return _pcall(body, name=name, out_shape=_sds((n_groups, gi, n_chips * goq), w.dtype), grid=(n_groups, n_chips),
                  in_specs=[pl.BlockSpec((None, gi, goq), lambda g, k: (k, g, 0))],
                  out_specs=pl.BlockSpec((None, gi, goq), lambda g, k: (g, 0, k)), vmem_mb=32)(w)


def _rms(h):
    r = lax.rsqrt(jnp.mean(h * h, axis=-1, keepdims=True) + EPS)
    return r, h * r


def _norm_mod(name, h, g, sc, sh, ts):
    s_len, d = h.shape

    def body(h_ref, g_ref, sc_ref, sh_ref, n_ref):
        _, xhat = _rms(h_ref[...])
        n_ref[...] = ((xhat * g_ref[...]) * (1.0 + sc_ref[...]) + sh_ref[...]).astype(BF16)

    row = pl.BlockSpec((ts, d), lambda i: (i, 0))
    return _pcall(body, name=name, out_shape=_sds((s_len, d), BF16), grid=(s_len // ts,),
                  in_specs=[row, _vec(d), _vec(d), _vec(d)], out_specs=row, vmem_mb=32)(h, g, sc, sh)


def _residual_norm_mod(name, h, f, gate, cmul, g, sc, sh, ts):
    s_len, d = h.shape

    def body(h_ref, f_ref, gt_ref, g_ref, sc_ref, sh_ref, ho_ref, n_ref):
        hn = h_ref[...] + (cmul * gt_ref[...]) * f_ref[...]
        ho_ref[...] = hn
        _, xhat = _rms(hn)
        n_ref[...] = ((xhat * g_ref[...]) * (1.0 + sc_ref[...]) + sh_ref[...]).astype(BF16)

    row = pl.BlockSpec((ts, d), lambda i: (i, 0))
    return _pcall(body, name=name, out_shape=[_sds((s_len, d), F32), _sds((s_len, d), BF16)],
                  grid=(s_len // ts,), in_specs=[row, row, _vec(d), _vec(d), _vec(d), _vec(d)],
                  out_specs=[row, row], vmem_mb=32)(h, f, gate, g, sc, sh)


def _final_loss(name, h, f, tgt, gate, cmul, g, ts):
    s_len, d = h.shape

    def body(h_ref, f_ref, t_ref, gt_ref, g_ref, dh_ref, df_ref, dg_ref, dgt_ref, loss_ref):
        i = pl.program_id(0)
        fv = f_ref[...]
        coef = cmul * gt_ref[...]
        hn = h_ref[...] + coef * fv
        r, xhat = _rms(hn)
        err = xhat * g_ref[...] - t_ref[...]
        _acc_rows(loss_ref, (0.5 / d) * (err * err), i)
        dy = err * (1.0 / d)
        _acc_rows(dg_ref, dy * xhat, i)
        dxhat = dy * g_ref[...]
        dh = r * (dxhat - xhat * jnp.mean(dxhat * xhat, axis=-1, keepdims=True))
        dh_ref[...] = dh
        _acc_rows(dgt_ref, cmul * (dh * fv), i)
        df_ref[...] = (coef * dh).astype(BF16)

    row = pl.BlockSpec((ts, d), lambda i: (i, 0))
    return _pcall(body, name=name,
                  out_shape=[_sds((s_len, d), F32), _sds((s_len, d), BF16)] + [_sds((1, d), F32)] * 3,
                  grid=(s_len // ts,), in_specs=[row, row, row, _vec(d), _vec(d)],
                  out_specs=[row, row, _vec(d), _vec(d), _vec(d)], vmem_mb=40)(h, f, tgt, gate, g)


def _norm_mod_bwd(name, h, dn, dh_next, g, sc, ts, prev=None):
    s_len, d = h.shape
    has_prev = prev is not None
    cmul = prev[2] if has_prev else None

    def body(*refs):
        if has_prev:
            h_ref, dn_ref, dhn_ref, f_ref, g_ref, sc_ref, gt_ref, dh_ref, df_ref, dsh_ref, dsc_ref, dg_ref, dgt_ref = refs
        else:
            h_ref, dn_ref, dhn_ref, g_ref, sc_ref, dh_ref, dsh_ref, dsc_ref, dg_ref = refs
        i = pl.program_id(0)
        r, xhat = _rms(h_ref[...])
        dn_v = dn_ref[...].astype(F32)
        gv = g_ref[...]
        _acc_rows(dsh_ref, dn_v, i)
        _acc_rows(dsc_ref, dn_v * (xhat * gv), i)
        dnrm = dn_v * (1.0 + sc_ref[...])
        _acc_rows(dg_ref, dnrm * xhat, i)
        dxhat = dnrm * gv
        dh = dhn_ref[...] + r * (dxhat - xhat * jnp.mean(dxhat * xhat, axis=-1, keepdims=True))
        dh_ref[...] = dh
        if has_prev:
            _acc_rows(dgt_ref, cmul * (dh * f_ref[...]), i)
            df_ref[...] = ((cmul * gt_ref[...]) * dh).astype(BF16)

    row = pl.BlockSpec((ts, d), lambda i: (i, 0))
    if has_prev:
        ins, in_specs = [h, dn, dh_next, prev[0], g, sc, prev[1]], [row, row, row, row, _vec(d), _vec(d), _vec(d)]
        out_shape = [_sds((s_len, d), F32), _sds((s_len, d), BF16)] + [_sds((1, d), F32)] * 4
        out_specs = [row, row] + [_vec(d)] * 4
    else:
        ins, in_specs = [h, dn, dh_next, g, sc], [row, row, row, _vec(d), _vec(d)]
        out_shape = [_sds((s_len, d), F32)] + [_sds((1, d), F32)] * 3
        out_specs = [row] + [_vec(d)] * 3
    return _pcall(body, name=name, out_shape=out_shape, grid=(s_len // ts,), in_specs=in_specs,
                  out_specs=out_specs, vmem_mb=40)(*ins)


def _cols(ref, lo, hi, npc, rows=slice(None)):
    parts = []
    while lo < hi:
        q, o = divmod(lo, npc)
        n = min(hi - lo, npc - o)
        parts.append(ref[q, rows, o:o + n].astype(F32))
        lo += n
    return parts[0] if len(parts) == 1 else jnp.concatenate(parts, axis=-1)


def _store_cols(ref, lo, val, npc, rows=slice(None)):
    off, width = 0, val.shape[-1]
    while off < width:
        q, o = divmod(lo + off, npc)
        n = min(width - off, npc - o)
        ref[q, rows, o:o + n] = val[:, off:off + n]
        off += n


def _chips_covering(cols, npc):
    return -(-cols // npc)


SUBLANES = 8
ROW_CHUNK = 32


def _make_phases(src_ref, ph_ref):
    rows = src_ref.shape[0] - SUBLANES
    for b in range(1, SUBLANES):
        ph_ref[b - 1] = src_ref[pl.ds(b, rows), :]


def _window(src_ref, ph_ref, off, r0, cols=slice(None)):
    a, b = divmod(off, SUBLANES)
    start = pl.multiple_of(r0 + SUBLANES * a, SUBLANES)
    if b == 0:
        return src_ref[pl.ds(start, ROW_CHUNK), cols]
    return ph_ref[b - 1, pl.ds(start, ROW_CHUNK), cols]


def _phase_scratch(rows, width):
    return pltpu.VMEM((SUBLANES - 1, rows - SUBLANES, width), F32)


def _conv(a0s_ref, a0p_ref, cw_ref, cb_ref, r0):
    a1 = cb_ref[...] + cw_ref[0:1, :] * _window(a0s_ref, a0p_ref, HALO - CONV_K + 1, r0)
    for k in range(1, CONV_K):
        a1 = a1 + cw_ref[k:k + 1, :] * _window(a0s_ref, a0p_ref, HALO - CONV_K + 1 + k, r0)
    return a1


def _layer_norm(a1, lg_ref, lb_ref):
    mu = jnp.mean(a1, axis=-1, keepdims=True)
    ctr = a1 - mu
    rstd = lax.rsqrt(jnp.mean(ctr * ctr, axis=-1, keepdims=True) + EPS)
    xh = ctr * rstd
    return xh, rstd, xh * lg_ref[...] + lb_ref[...]


def _for_chunks(ts, fn):
    def step(ci, carry):
        fn(pl.multiple_of(ci * ROW_CHUNK, ROW_CHUNK))
        return carry

    lax.fori_loop(0, ts // ROW_CHUNK, step, 0)


def _stage_glu(p_ref, ph_ref, a0s_ref, i, wc, npc, ts):
    a0 = _cols(p_ref, 0, wc, npc) * _sigmoid(_cols(p_ref, wc, 2 * wc, npc))
    a0h = _cols(ph_ref, 0, wc, npc) * _sigmoid(_cols(ph_ref, wc, 2 * wc, npc))
    a0s_ref[0:HALO, :] = jnp.where(i > 0, a0h, 0.0)
    a0s_ref[HALO:HALO + ts, :] = a0


def _mixer_mid(name, proj, cw, cb, lg, lb, wc, wp, ts):
    _, s_len, npc = proj.shape
    nq = _chips_covering(2 * wc + wp, npc)
    gi = wp // len(POOL_WINDOWS)
    hb = ts // HALO

    def body(p_ref, ph_ref, cw_ref, cb_ref, lg_ref, lb_ref, a3_ref, mx_ref, a1_ref, a0s_ref, vs_ref, a0p_ref,
             vp_ref):
        i = pl.program_id(0)
        _stage_glu(p_ref, ph_ref, a0s_ref, i, wc, npc, ts)
        vs_ref[0:HALO, :] = jnp.where(i > 0, _cols(ph_ref, 2 * wc, 2 * wc + wp, npc), 0.0)
        vs_ref[HALO:HALO + ts, :] = _cols(p_ref, 2 * wc, 2 * wc + wp, npc)
        _make_phases(a0s_ref, a0p_ref)
        _make_phases(vs_ref, vp_ref)

        def chunk(r0):
            rows = pl.ds(r0, ROW_CHUNK)
            a1 = _conv(a0s_ref, a0p_ref, cw_ref, cb_ref, r0)
            a1_ref[rows, :] = a1
            _, _, a2 = _layer_norm(a1, lg_ref, lb_ref)
            a3_ref[rows, :] = (a2 * _sigmoid(a2)).astype(BF16)
            t_abs = i * ts + r0 + lax.broadcasted_iota(jnp.int32, (ROW_CHUNK, 1), 0)
            for g, win in enumerate(POOL_WINDOWS):
                cs = slice(g * gi, (g + 1) * gi)
                v_now = _window(vs_ref, vp_ref, HALO, r0, cs)
                acc = v_now
                for dlt in range(1, win):
                    acc = acc + _window(vs_ref, vp_ref, HALO - dlt, r0, cs)
                cnt = jnp.minimum(t_abs + 1, win).astype(F32)
                mx_ref[rows, cs] = (acc / cnt - v_now).astype(BF16)

        _for_chunks(ts, chunk)

    return _pcall(
        body, name=name, out_shape=[_sds((s_len, wc), BF16), _sds((s_len, wp), BF16), _sds((s_len, wc), F32)],
        grid=(s_len // ts,),
        in_specs=[pl.BlockSpec((nq, ts, npc), lambda i: (0, i, 0)),
                  pl.BlockSpec((nq, HALO, npc), lambda i: (0, jnp.maximum(i * hb - 1, 0), 0)),
                  pl.BlockSpec((HALO, wc), lambda i: (0, 0)), _vec(wc), _vec(wc), _vec(wc)],
        out_specs=[pl.BlockSpec((ts, wc), lambda i: (i, 0)), pl.BlockSpec((ts, wp), lambda i: (i, 0)),
                   pl.BlockSpec((ts, wc), lambda i: (i, 0))],
        scratch=[pltpu.VMEM((HALO + ts, wc), F32), pltpu.VMEM((HALO + ts, wp), F32),
                 _phase_scratch(HALO + ts, wc), _phase_scratch(HALO + ts, wp)], vmem_mb=56,
    )(proj, proj, cw, cb, lg, lb)


def _gates_fwd(name, proj, ya, yb, b_a, b_b, ls, wc, wp, ts):
    _, s_len, npc = proj.shape
    d = ya.shape[1]
    g0 = 2 * wc + wp

    def body(p_ref, ya_ref, yb_ref, ba_ref, bb_ref, ls_ref, z_ref):
        ga = _sigmoid(_cols(p_ref, g0, g0 + d, npc))
        gb = _sigmoid(_cols(p_ref, g0 + d, g0 + 2 * d, npc))
        z = ga * (ya_ref[...] + ba_ref[...]) + gb * ((yb_ref[...] + bb_ref[...]) * ls_ref[...])
        z_ref[...] = z.astype(BF16)

    row = pl.BlockSpec((ts, d), lambda i: (i, 0))
    return _pcall(body, name=name, out_shape=_sds((s_len, d), BF16), grid=(s_len // ts,),
                  in_specs=[pl.BlockSpec((N_CHIPS, ts, npc), lambda i: (0, i, 0)), row, row, _vec(d), _vec(d), _vec(d)],
                  out_specs=row, vmem_mb=48)(proj, ya, yb, b_a, b_b, ls)


def _gates_bwd(name, proj, dz, ya, yb, b_a, b_b, ls, wc, wp, ts):
    _, s_len, npc = proj.shape
    d = ya.shape[1]
    g0 = 2 * wc + wp

    def body(p_ref, dz_ref, ya_ref, yb_ref, ba_ref, bb_ref, ls_ref, dya_ref, dyb_ref, dgt_ref, dba_ref, dls_ref,
             dbb_ref):
        i = pl.program_id(0)
        ga = _sigmoid(_cols(p_ref, g0, g0 + d, npc))
        gb = _sigmoid(_cols(p_ref, g0 + d, g0 + 2 * d, npc))
        dz_v = dz_ref[...].astype(F32)
        y_a = ya_ref[...] + ba_ref[...]
        y_b0 = yb_ref[...] + bb_ref[...]
        ls_v = ls_ref[...]
        dya = dz_v * ga
        dya_ref[...] = dya.astype(BF16)
        _acc_rows(dba_ref, dya, i)
        t = dz_v * gb
        _acc_rows(dls_ref, t * y_b0, i)
        dyb = t * ls_v
        dyb_ref[...] = dyb.astype(BF16)
        _acc_rows(dbb_ref, dyb, i)
        dgt_ref[:, 0:d] = (dz_v * y_a * ga * (1.0 - ga)).astype(BF16)
        dgt_ref[:, d:2 * d] = (dz_v * (y_b0 * ls_v) * gb * (1.0 - gb)).astype(BF16)

    row = pl.BlockSpec((ts, d), lambda i: (i, 0))
    return _pcall(
        body, name=name,
        out_shape=[_sds((s_len, d), BF16), _sds((s_len, d), BF16), _sds((s_len, 2 * d), BF16)] + [_sds((1, d), F32)] * 3,
        grid=(s_len // ts,),
        in_specs=[pl.BlockSpec((N_CHIPS, ts, npc), lambda i: (0, i, 0)), row, row, row, _vec(d), _vec(d), _vec(d)],
        out_specs=[row, row, pl.BlockSpec((ts, 2 * d), lambda i: (i, 0))] + [_vec(d)] * 3, vmem_mb=48,
    )(proj, dz, ya, yb, b_a, b_b, ls)


def _conv_branch_bwd(name, proj, a1, da3, lg, lb, wc, wp, ts):
    _, s_len, npc = proj.shape
    nq = _chips_covering(2 * wc, npc)
    hb = ts // HALO

    n_tiles = s_len // ts

    def fold(v):
        return jnp.sum(v.reshape(ROW_CHUNK // SUBLANES, SUBLANES, v.shape[-1]), axis=0)

    def body(p_ref, ph_ref, a1_ref, da3_ref, lg_ref, lb_ref, da1_ref, dlg_ref, dlb_ref, dcb_ref, dcw_ref,
             a0s_ref, a0p_ref, vec8_ref, dcw8_ref):
        i = pl.program_id(0)
        _stage_glu(p_ref, ph_ref, a0s_ref, i, wc, npc, ts)
        _make_phases(a0s_ref, a0p_ref)

        @pl.when(i == 0)
        def _():
            vec8_ref[...] = jnp.zeros_like(vec8_ref)
            dcw8_ref[...] = jnp.zeros_like(dcw8_ref)

        def chunk(r0):
            rows = pl.ds(r0, ROW_CHUNK)
            xh, rstd, a2 = _layer_norm(a1_ref[rows, :], lg_ref, lb_ref)
            sig = _sigmoid(a2)
            da2 = da3_ref[rows, :].astype(F32) * (sig * (1.0 + a2 * (1.0 - sig)))
            vec8_ref[0] += fold(da2 * xh)
            vec8_ref[1] += fold(da2)
            dxh = da2 * lg_ref[...]
            da1 = rstd * (dxh - jnp.mean(dxh, axis=-1, keepdims=True)
                          - xh * jnp.mean(dxh * xh, axis=-1, keepdims=True))
            da1_ref[rows, :] = da1
            vec8_ref[2] += fold(da1)
            for k in range(CONV_K):
                dcw8_ref[k] += fold(da1 * _window(a0s_ref, a0p_ref, HALO - CONV_K + 1 + k, r0))

        _for_chunks(ts, chunk)

        @pl.when(i == n_tiles - 1)
        def _():
            dlg_ref[...] = jnp.sum(vec8_ref[0], axis=0, keepdims=True)
            dlb_ref[...] = jnp.sum(vec8_ref[1], axis=0, keepdims=True)
            dcb_ref[...] = jnp.sum(vec8_ref[2], axis=0, keepdims=True)
            dcw_ref[...] = jnp.sum(dcw8_ref[...], axis=1)

    return _pcall(
        body, name=name,
        out_shape=[_sds((s_len, wc), F32)] + [_sds((1, wc), F32)] * 3 + [_sds((HALO, wc), F32)],
        grid=(s_len // ts,),
        in_specs=[pl.BlockSpec((nq, ts, npc), lambda i: (0, i, 0)),
                  pl.BlockSpec((nq, HALO, npc), lambda i: (0, jnp.maximum(i * hb - 1, 0), 0)),
                  pl.BlockSpec((ts, wc), lambda i: (i, 0)), pl.BlockSpec((ts, wc), lambda i: (i, 0)),
                  _vec(wc), _vec(wc)],
        out_specs=[pl.BlockSpec((ts, wc), lambda i: (i, 0)), _vec(wc), _vec(wc), _vec(wc),
                   pl.BlockSpec((HALO, wc), lambda i: (0, 0))],
        scratch=[pltpu.VMEM((HALO + ts, wc), F32), _phase_scratch(HALO + ts, wc),
                 pltpu.VMEM((3, SUBLANES, wc), F32), pltpu.VMEM((HALO, SUBLANES, wc), F32)], vmem_mb=56,
    )(proj, proj, a1, da3, lg, lb)


def _mixer_in_bwd(name, proj, da1, dmixed, dgates, cw, wc, wp, ts):
    _, s_len, npc = proj.shape
    nq = _chips_covering(2 * wc, npc)
    gi = wp // len(POOL_WINDOWS)
    hb = ts // HALO
    n_tiles = s_len // ts
    last_hb = s_len // HALO - 1
    d2 = dgates.shape[1]

    def body(p_ref, d1_ref, d1n_ref, dm_ref, dmn_ref, dgt_ref, cw_ref, o_ref, d1s_ref, es_ref, d1p_ref, ep_ref):
        i = pl.program_id(0)
        more = i < n_tiles - 1
        d1s_ref[0:ts, :] = d1_ref[...]
        d1s_ref[ts:ts + HALO, :] = jnp.where(more, d1n_ref[...], 0.0)
        t_abs = i * ts + lax.broadcasted_iota(jnp.int32, (ts + HALO, 1), 0)
        dm_ext = jnp.concatenate([dm_ref[...].astype(F32), jnp.where(more, dmn_ref[...].astype(F32), 0.0)], axis=0)
        for g, win in enumerate(POOL_WINDOWS):
            cs = slice(g * gi, (g + 1) * gi)
            es_ref[:, cs] = dm_ext[:, cs] / jnp.minimum(t_abs + 1, win).astype(F32)
        _make_phases(d1s_ref, d1p_ref)
        _make_phases(es_ref, ep_ref)

        def chunk(r0):
            rows = pl.ds(r0, ROW_CHUNK)
            da0 = cw_ref[0:1, :] * _window(d1s_ref, d1p_ref, CONV_K - 1, r0)
            for k in range(1, CONV_K):
                da0 = da0 + cw_ref[k:k + 1, :] * _window(d1s_ref, d1p_ref, CONV_K - 1 - k, r0)
            glu_a = _cols(p_ref, 0, wc, npc, rows)
            sig = _sigmoid(_cols(p_ref, wc, 2 * wc, npc, rows))
            _store_cols(o_ref, 0, (da0 * sig).astype(BF16), npc, rows)
            _store_cols(o_ref, wc, (da0 * glu_a * sig * (1.0 - sig)).astype(BF16), npc, rows)
            parts = []
            for g, win in enumerate(POOL_WINDOWS):
                cs = slice(g * gi, (g + 1) * gi)
                acc = _window(es_ref, ep_ref, 0, r0, cs)
                for dlt in range(1, win):
                    acc = acc + _window(es_ref, ep_ref, dlt, r0, cs)
                parts.append(acc - dm_ref[rows, cs].astype(F32))
            _store_cols(o_ref, 2 * wc, jnp.concatenate(parts, axis=-1).astype(BF16), npc, rows)

        _for_chunks(ts, chunk)
        _store_cols(o_ref, 2 * wc + wp, dgt_ref[...], npc)

    nxt = lambda i: (jnp.minimum((i + 1) * hb, last_hb), 0)
    return _pcall(
        body, name=name, out_shape=_sds((N_CHIPS, s_len, npc), BF16), grid=(n_tiles,),
        in_specs=[pl.BlockSpec((nq, ts, npc), lambda i: (0, i, 0)),
                  pl.BlockSpec((ts, wc), lambda i: (i, 0)), pl.BlockSpec((HALO, wc), nxt),
                  pl.BlockSpec((ts, wp), lambda i: (i, 0)), pl.BlockSpec((HALO, wp), nxt),
                  pl.BlockSpec((ts, d2), lambda i: (i, 0)),
                  pl.BlockSpec((HALO, wc), lambda i: (0, 0))],
        out_specs=pl.BlockSpec((N_CHIPS, ts, npc), lambda i: (0, i, 0)),
        scratch=[pltpu.VMEM((ts + HALO, wc), F32), pltpu.VMEM((ts + HALO, wp), F32),
                 _phase_scratch(ts + HALO, wc), _phase_scratch(ts + HALO, wp)], vmem_mb=56,
    )(proj, da1, da1, dmixed, dmixed, dgates, cw)


def _ada_fwd(name, c_all, w, b):
    d, cols = w.shape
    tn = 512 if cols % 512 == 0 else cols

    def body(c_ref, w_ref, b_ref, o_ref):
        cv = c_ref[...]
        sc = (cv * _sigmoid(cv)).astype(BF16)
        o_ref[...] = jnp.dot(sc, w_ref[...].astype(BF16), preferred_element_type=F32) + b_ref[...]

    return _pcall(body, name=name, out_shape=_sds((N_DEV, cols), F32), grid=(cols // tn,),
                  in_specs=[pl.BlockSpec((N_DEV, d), lambda j: (0, 0)), pl.BlockSpec((d, tn), lambda j: (0, j)),
                            pl.BlockSpec((1, tn), lambda j: (0, j))],
                  out_specs=pl.BlockSpec((N_DEV, tn), lambda j: (0, j)), vmem_mb=32)(c_all, w, b)


def _adam_math(w, g, m, v):
    m_new = ADAM_B1 * m + (1.0 - ADAM_B1) * g
    v_new = ADAM_B2 * v + (1.0 - ADAM_B2) * (g * g)
    m_hat = m_new / (1.0 - ADAM_B1 ** ADAM_STEP)
    v_hat = v_new / (1.0 - ADAM_B2 ** ADAM_STEP)
    delta = -ADAM_LR * (m_hat / (jnp.sqrt(v_hat) + ADAM_EPS) + ADAM_WD * w)
    return delta, m_new, v_new


def _adamw(name, w, g, m, v):
    rows, cols = w.shape
    tr = _row_tile(rows, cols, 524288)

    def body(w_ref, g_ref, m_ref, v_ref, go_ref, d_ref, mo_ref, vo_ref):
        g = g_ref[...]
        go_ref[...] = g
        d_ref[...], mo_ref[...], vo_ref[...] = _adam_math(w_ref[...], g, m_ref[...], v_ref[...])

    spec = pl.BlockSpec((tr, cols), lambda i: (i, 0))
    return _pcall(body, name=name, out_shape=[_sds(w.shape, F32)] * 4, grid=(rows // tr,), in_specs=[spec] * 4,
                  out_specs=[spec] * 4, vmem_mb=40)(w, g, m, v)


def _ada_grad_adamw(name, c_t, d_ada, w, m, v):
    rows, cols = w.shape
    tr = _tile(rows, 256)
    tc = _tile(cols, 1536) if cols % 1536 == 0 else cols

    def body(c_ref, da_ref, w_ref, m_ref, v_ref, g_ref, d_ref, mo_ref, vo_ref):
        cv = c_ref[...]
        sc = cv * _sigmoid(cv)
        g = sc[:, 0:1] * da_ref[0:1, :]
        for b in range(1, N_DEV):
            g = g + sc[:, b:b + 1] * da_ref[b:b + 1, :]
        g_ref[...] = g
        d_ref[...], mo_ref[...], vo_ref[...] = _adam_math(w_ref[...], g, m_ref[...], v_ref[...])

    spec = pl.BlockSpec((tr, tc), lambda i, j: (i, j))
    return _pcall(body, name=name, out_shape=[_sds(w.shape, F32)] * 4, grid=(rows // tr, cols // tc),
                  in_specs=[pl.BlockSpec((tr, N_DEV), lambda i, j: (i, 0)),
                            pl.BlockSpec((N_DEV, tc), lambda i, j: (0, j)), spec, spec, spec],
                  out_specs=[spec] * 4, vmem_mb=40)(c_t, d_ada, w, m, v)


def _sum_devices(name, gathered, m_per):
    n = gathered.shape[1]

    def body(g_ref, o_ref):
        acc = g_ref[0:m_per, :]
        for dev in range(1, N_DEV):
            acc = acc + g_ref[dev * m_per:(dev + 1) * m_per, :]
        o_ref[...] = acc

    return _pcall(body, name=name, out_shape=_sds((m_per, n), F32),
                  in_specs=[pl.BlockSpec(memory_space=pltpu.VMEM)],
                  out_specs=pl.BlockSpec(memory_space=pltpu.VMEM))(gathered)


def _ffn_fwd(tag, n, w_in_parts, w_out_after_swiglu, dims):
    s_len, d, f_dim = dims["S"], dims["D"], dims["F"]
    tf = f_dim // 4
    tm0, tm = _tile(s_len, 512), _tile(s_len, 1024)
    p = f_dim // 2

    def ep(accs, ex, outs, cs=slice(None)):
        hh, uu = accs
        sig = _sigmoid(hh)
        silu = hh * sig
        outs[0][0, :, cs] = (uu * (sig + silu * (1.0 - sig))).astype(BF16)
        outs[0][1, :, cs] = silu.astype(BF16)
        outs[1][:, cs] = (silu * uu).astype(BF16)

    done = ()
    for part, (get_w, cols) in enumerate(w_in_parts):
        w_g = get_w().reshape(N_CHIPS * d, p)
        lo, width = cols if cols is not None else (0, p)
        tr = tm0 if cols is None else tm
        mode_kw = dict(pipeline_mode=pl.Buffered(1))
        el = pl.Element
        done = _matmul(
            f"{tag}_swiglu{part}", n, [w_g, w_g], mode="nn", grid=(2, s_len // tr, 1),
            a_spec=pl.BlockSpec((tr, d), lambda j, i, k: (i, 0)),
            b_specs=[pl.BlockSpec((el(d), el(width)), lambda j, i, k, lo=lo: (_mult(j * d, d), lo), **mode_kw),
                     pl.BlockSpec((el(d), el(width)), lambda j, i, k, lo=lo: (_mult((2 + j) * d, d), lo), **mode_kw)],
            out_shape=[_sds((2, s_len, f_dim), BF16), _sds((s_len, f_dim), BF16)],
            out_specs=[pl.BlockSpec((el(2), el(tr), el(width)),
                                    lambda j, i, k, lo=lo, tr=tr: (0, _mult(i * tr, tr), _mult(j * p + lo, LANES))),
                       pl.BlockSpec((el(tr), el(width)),
                                    lambda j, i, k, lo=lo, tr=tr: (_mult(i * tr, tr), _mult(j * p + lo, LANES)))],
            acc_shape=(tr, width), epilogue=ep, carry=done, col_block=512)
    hu, act = done
    w_out2d = w_out_after_swiglu()
    tn2 = _tile(d, 1024)
    f = _matmul(
        f"{tag}_down", act, [w_out2d], mode="nn", grid=(s_len // tm, d // tn2, 2),
        a_spec=pl.BlockSpec((tm, 2 * tf), lambda i, j, k: (i, k)),
        b_specs=[pl.BlockSpec((2 * tf, tn2), lambda i, j, k: (k, j))],
        out_shape=_sds((s_len, d), F32), out_specs=pl.BlockSpec((tm, tn2), lambda i, j, k: (i, j)),
        acc_shape=(tm, tn2), epilogue=_ep_store(F32))
    return hu, act, f, w_out2d


def _ffn_bwd(tag, n, hu, act, df, w_in_g, w_out2d, dims, after_dw_out, after_dw_in):
    s_len, d, f_dim = dims["S"], dims["D"], dims["F"]
    tf = f_dim // 4
    tk = _tile(s_len, 2048)
    tn = _tile(d, 1024)
    g_out = _matmul(
        f"{tag}_dw_out", act, [df], mode="tn", grid=(4, d // tn, s_len // tk),
        a_spec=pl.BlockSpec((tk, tf), lambda i, j, k: (k, i)),
        b_specs=[pl.BlockSpec((tk, tn), lambda i, j, k: (k, j))],
        out_shape=_sds((2, 4, tf // 2, d), F32),
        out_specs=pl.BlockSpec((2, None, tf // 2, tn), lambda i, j, k: (0, i, 0, j)),
        acc_shape=(tf, tn), epilogue=_ep_halves(tf // 2))
    after_dw_out(g_out)

    def ep_dhu(accs, ex, outs):
        da = accs[0]
        outs[0][0] = (da * ex[0][0].astype(F32)).astype(BF16)
        outs[0][1] = (da * ex[0][1].astype(F32)).astype(BF16)

    tm = _tile(s_len, 512)
    hu_spec = pl.BlockSpec((2, tm, 2 * tf), lambda j, i, k: (0, i, j))
    dhu = _matmul(
        f"{tag}_dhu", df, [w_out2d], mode="nt", grid=(2, s_len // tm, 1),
        a_spec=pl.BlockSpec((tm, d), lambda j, i, k: (i, 0)),
        b_specs=[pl.BlockSpec((2 * tf, d), lambda j, i, k: (j, 0), pipeline_mode=pl.Buffered(1))],
        extras=[hu], extra_specs=[hu_spec],
        out_shape=_sds((2, s_len, f_dim), BF16), out_specs=hu_spec, acc_shape=(tm, 2 * tf), epilogue=ep_dhu)

    hd = d // 2
    rt = hd // 2
    g_in = _matmul(
        f"{tag}_dw_in", n, [dhu], mode="tn", grid=(N_CHIPS, 4, s_len // tk),
        a_spec=pl.BlockSpec((tk, rt), lambda j, i, k: (k, i)),
        b_specs=[pl.BlockSpec((None, tk, 2 * tf), lambda j, i, k: (j // 2, k, j % 2))],
        out_shape=_sds((2, 4, hd, f_dim // 2), F32),
        out_specs=pl.BlockSpec((None, None, rt, 2 * tf), lambda j, i, k: (i // 2, j, i % 2, 0)),
        acc_shape=(rt, 2 * tf), epilogue=_ep_store(F32))
    after_dw_in(g_in)

    tm2 = _tile(s_len, 1024)
    dn = _matmul(
        f"{tag}_dn", dhu, [w_in_g], mode="nt", grid=(s_len // tm2, d // tn, N_CHIPS),
        a_spec=pl.BlockSpec((None, tm2, 2 * tf), lambda i, j, k: (k // 2, i, k % 2)),
        b_specs=[pl.BlockSpec((None, tn, 2 * tf), lambda i, j, k: (k, j, 0))],
        out_shape=_sds((s_len, d), BF16), out_specs=pl.BlockSpec((tm2, tn), lambda i, j, k: (i, j)),
        acc_shape=(tm2, tn), epilogue=_ep_store(BF16))
    return dn


def kernel(x, c, w_ada, b_ada, g_ffn1, w1_in, w1_out, g_mix, w_in, conv_w, conv_b, ln_a_g, ln_a_b, w_a_out, b_a_out, w_b_group, b_b_group, ls_b, w_out, g_ffn2, w2_in, w2_out, g_final, loss_target, m_w_ada, m_b_ada, m_g_ffn1, m_w1_in, m_w1_out, m_g_mix, m_w_in, m_conv_w, m_conv_b, m_ln_a_g, m_ln_a_b, m_w_a_out, m_b_a_out, m_w_b_group, m_b_b_group, m_ls_b, m_w_out, m_g_ffn2, m_w2_in, m_w2_out, m_g_final, v_w_ada, v_b_ada, v_g_ffn1, v_w1_in, v_w1_out, v_g_mix, v_w_in, v_conv_w, v_conv_b, v_ln_a_g, v_ln_a_b, v_w_a_out, v_b_a_out, v_w_b_group, v_b_b_group, v_ls_b, v_w_out, v_g_ffn2, v_w2_in, v_w2_out, v_g_final):
    weights = dict(w_ada=w_ada, b_ada=b_ada, g_ffn1=g_ffn1, w1_in=w1_in, w1_out=w1_out, g_mix=g_mix, w_in=w_in,
                   conv_w=conv_w, conv_b=conv_b, ln_a_g=ln_a_g, ln_a_b=ln_a_b, w_a_out=w_a_out, b_a_out=b_a_out,
                   w_b_group=w_b_group, b_b_group=b_b_group, ls_b=ls_b, w_out=w_out, g_ffn2=g_ffn2, w2_in=w2_in,
                   w2_out=w2_out, g_final=g_final)
    mom1 = dict(w_ada=m_w_ada, b_ada=m_b_ada, g_ffn1=m_g_ffn1, w1_in=m_w1_in, w1_out=m_w1_out, g_mix=m_g_mix,
                w_in=m_w_in, conv_w=m_conv_w, conv_b=m_conv_b, ln_a_g=m_ln_a_g, ln_a_b=m_ln_a_b, w_a_out=m_w_a_out,
                b_a_out=m_b_a_out, w_b_group=m_w_b_group, b_b_group=m_b_b_group, ls_b=m_ls_b, w_out=m_w_out,
                g_ffn2=m_g_ffn2, w2_in=m_w2_in, w2_out=m_w2_out, g_final=m_g_final)
    mom2 = dict(w_ada=v_w_ada, b_ada=v_b_ada, g_ffn1=v_g_ffn1, w1_in=v_w1_in, w1_out=v_w1_out, g_mix=v_g_mix,
                w_in=v_w_in, conv_w=v_conv_w, conv_b=v_conv_b, ln_a_g=v_ln_a_g, ln_a_b=v_ln_a_b, w_a_out=v_w_a_out,
                b_a_out=v_b_a_out, w_b_group=v_w_b_group, b_b_group=v_b_b_group, ls_b=v_ls_b, w_out=v_w_out,
                g_ffn2=v_g_ffn2, w2_in=v_w2_in, w2_out=v_w2_out, g_final=v_g_final)
    order = list(weights)

    s_len, d = x.shape[1], x.shape[2]
    f_dim = w1_out.shape[0] * N_CHIPS
    wc = conv_w.shape[1] * N_CHIPS
    wp = w_b_group.shape[0] * w_b_group.shape[1]
    n_groups, gi, goq = w_b_group.shape
    npc = w_in.shape[1]
    ada_c = w_ada.shape[1]
    dims = dict(S=s_len, D=d, F=f_dim)
    ts = _tile(s_len, 256)

    xi, yi, ci = lax.axis_index("x"), lax.axis_index("y"), lax.axis_index("c")
    q = 2 * xi + yi
    dev = 2 * q + ci
    q_idx = jnp.reshape(q, (1,)).astype(jnp.int32)
    qc_idx = jnp.stack([q, ci]).astype(jnp.int32)
    _PREVIOUS.clear()

    cwq = conv_w.shape[1]
    pack0 = jnp.concatenate([c.reshape(-1), conv_w.reshape(-1), b_b_group.reshape(-1)])
    n0 = -(-pack0.shape[0] // (8 * LANES)) * LANES
    pack0 = jnp.pad(pack0, (0, 8 * n0 - pack0.shape[0])).reshape(8, n0)
    g0 = _allgather_small("gather_small_in", pack0).reshape(N_DEV, 8 * n0)
    c_all = g0[:, :d]
    south = g0[0::2]
    cw_full = jnp.concatenate([south[k, d:d + CONV_K * cwq].reshape(CONV_K, cwq) for k in range(N_CHIPS)], axis=1)
    cw_pad = jnp.pad(cw_full, ((0, HALO - CONV_K), (0, 0)))
    o_bb = d + CONV_K * cwq
    bb_full = jnp.concatenate([south[k, o_bb:o_bb + n_groups * goq].reshape(n_groups, goq) for k in range(N_CHIPS)],
                              axis=1).reshape(1, d)

    as2d = lambda a: a.reshape(-1, a.shape[-1])
    groups = dict(w1_out=["w1_out"], mix=["w_a_out", "w_b_group", "w_out"], w2_in=["w2_in"], w2_out=["w2_out"])
    big = ["w1_in", "w1_out", "w_in", "w_a_out", "w_b_group", "w_out", "w2_in", "w2_out"]
    cast = lambda nm: _cast_into_gathered(f"cast_{nm}", as2d(weights[nm]), q_idx)
    w1_in_gather = _TwoPartGather("w1_in", cast("w1_in"))

    b_ada_mine = lax.dynamic_slice(b_ada, (q * ada_c,), (ada_c,)).reshape(1, ada_c)
    ada_piece = _ada_fwd("ada_fwd", c_all, w_ada, b_ada_mine)
    casts = {nm: cast(nm) for nm in big[1:]}
    g1 = _allgather_small("gather_ada", ada_piece).reshape(N_DEV, N_DEV, ada_c)
    w1_in_gather.start_second()
    ici = {}
    for grp, names in groups.items():
        ici[grp] = _gather_ici(f"gather_{grp}_ici", [casts[nm] for nm in names])
        if grp == "w1_out":
            w_in_gather = _TwoPartGather("w_in", casts["w_in"])
            w_in_gather.start_second()
    ada_rows = lax.dynamic_index_in_dim(g1[0::2], dev, axis=1, keepdims=False)
    ada = ada_rows.reshape(3, 3, 1, d)
    (sh1, sc1, gt1), (sh2, sc2, gt2), (sh3, sc3, gt3) = [[ada[i, j] for j in range(3)] for i in range(3)]

    row = lambda vct: vct.reshape(1, -1)
    g1v, gmv, g2v, gfv = row(g_ffn1), row(g_mix), row(g_ffn2), row(g_final)

    def arrived(grp):
        return _gather_d2d(f"gather_{grp}_d2d", ici[grp].wait())

    def gathered(fwd, grp):
        return {nm: g.reshape(N_CHIPS, 2 * g.shape[2], g.shape[3]) for nm, g in zip(groups[grp], fwd.wait())}

    x2 = x[0]
    tgt = loss_target[0]

    n1 = _norm_mod("ffn1_norm", x2, g1v, sc1, sh1, ts)
    fwd, w1_in_parts = {}, []

    def w1_in_part(part):
        def get():
            w1_in_gather.arrive(part)
            w1_in_parts.append(w1_in_gather.ready(part))
            return w1_in_parts[-1]
        return get

    def w1_out_after_swiglu():
        fwd["w1_out"] = arrived("w1_out")
        w_in_gather.arrive(0)
        return gathered(fwd["w1_out"], "w1_out")["w1_out"].reshape(f_dim, d)

    hu1, act1, f1, w1_out_2d = _ffn_fwd(
        "ffn1", n1, [(w1_in_part(part), w1_in_gather.parts[part]) for part in range(2)], w1_out_after_swiglu, dims)
    w1_in_g = w1_in_parts[-1]
    h1, n2 = _residual_norm_mod("mix_norm", x2, f1, gt1, 0.5, gmv, sc2, sh2, ts)

    tm = _tile(s_len, 1024)
    tnp = npc // 2
    proj = ()
    for part in range(2):
        if part:
            w_in_gather.arrive(part)
        w_in_g = w_in_gather.ready(part)
        lo, width = w_in_gather.parts[part]
        el = pl.Element
        proj = (_matmul(
            f"mix_proj{part}", n2, [w_in_g.reshape(N_CHIPS * d, npc)], mode="nn", grid=(s_len // tm, N_CHIPS, 1),
            a_spec=pl.BlockSpec((tm, d), lambda i, j, k: (i, 0)),
            b_specs=[pl.BlockSpec((el(d), el(width)), lambda i, j, k, lo=lo: (_mult(j * d, d), lo))],
            out_shape=_sds((N_CHIPS * s_len, npc), BF16),
            out_specs=pl.BlockSpec((el(tm), el(width)), lambda i, j, k, lo=lo: (_mult(j * s_len + i * tm, tm), lo)),
            acc_shape=(tm, width), epilogue=_ep_store(BF16), carry=proj),)
    proj = proj[0].reshape(N_CHIPS, s_len, npc)
    fwd["mix"] = arrived("mix")
    cbv, lgv, lbv = row(conv_b), row(ln_a_g), row(ln_a_b)
    a3, mixed, conv_out = _mixer_mid("mix_mid", proj, cw_pad, cbv, lgv, lbv, wc, wp, ts)
    wts = gathered(fwd["mix"], "mix")
    w_out_2d = wts["w_out"].reshape(d, d)
    w_a_g = wts["w_a_out"]
    w_b_r = _regroup("regroup_w_b", wts["w_b_group"], n_groups)
    dq = d // N_CHIPS
    ya = _matmul(
        "mix_ya", a3, [w_a_g], mode="nn", grid=(s_len // tm, N_CHIPS, 1),
        a_spec=pl.BlockSpec((tm, wc), lambda i, j, k: (i, 0)),
        b_specs=[pl.BlockSpec((None, wc, dq), lambda i, j, k: (j, 0, 0))],
        out_shape=_sds((s_len, d), BF16), out_specs=pl.BlockSpec((tm, dq), lambda i, j, k: (i, j)),
        acc_shape=(tm, dq), epilogue=_ep_store(BF16))
    yb = _matmul(
        "mix_yb", mixed, [w_b_r], mode="nn", grid=(s_len // tm, n_groups, 1),
        a_spec=pl.BlockSpec((tm, gi), lambda i, j, k: (i, j)),
        b_specs=[pl.BlockSpec((None, gi, dq), lambda i, j, k: (j, 0, 0))],
        out_shape=_sds((s_len, d), BF16), out_specs=pl.BlockSpec((tm, dq), lambda i, j, k: (i, j)),
        acc_shape=(tm, dq), epilogue=_ep_store(BF16))
    bav, lsv = row(b_a_out), row(ls_b)
    z = _gates_fwd("mix_gates", proj, ya, yb, bav, bb_full, lsv, wc, wp, ts)
    tn = _tile(d, 1024)
    mix = _matmul(
        "mix_out", z, [w_out_2d], mode="nn", grid=(s_len // tm, d // tn, 1),
        a_spec=pl.BlockSpec((tm, d), lambda i, j, k: (i, 0)),
        b_specs=[pl.BlockSpec((d, tn), lambda i, j, k: (0, j))],
        out_shape=_sds((s_len, d), F32), out_specs=pl.BlockSpec((tm, tn), lambda i, j, k: (i, j)),
        acc_shape=(tm, tn), epilogue=_ep_store(F32))
    fwd["w2_in"] = arrived("w2_in")
    h2, n3 = _residual_norm_mod("ffn2_norm", h1, mix, gt2, 1.0, g2v, sc3, sh3, ts)
    w2_in_g = gathered(fwd["w2_in"], "w2_in")["w2_in"]
    hu2, act2, f3, w2_out_2d = _ffn_fwd(
        "ffn2", n3, [(lambda: w2_in_g, None)],
        lambda: gathered(arrived("w2_out"), "w2_out")["w2_out"].reshape(f_dim, d), dims)

    dh3, df3, d_gf, d_gt3, loss_cols = _final_loss("final_loss", h2, f3, tgt, gt3, 0.5, gfv, ts)
    rs, held = {}, {}
    dn3 = _ffn_bwd(
        "ffn2", n3, hu2, act2, df3, w2_in_g, w2_out_2d, dims,
        after_dw_out=lambda g: held.update(w2_out=g),
        after_dw_in=lambda g: rs.update(ffn2=_ReduceScatter("g_ffn2", ["w2_out", "w2_in"], [held["w2_out"], g],
                                                            qc_idx)))
    dh2, dmix, d_sh3, d_sc3, d_g2, d_gt2 = _norm_mod_bwd("ffn2_norm_bwd", h2, dn3, dh3, g2v, sc3, ts,
                                                         prev=(mix, gt2, 1.0))
    rs["ffn2"].step2()

    tk = s_len
    hq = d // (2 * N_CHIPS)
    gw_out = _matmul(
        "mix_dw_out", z, [dmix], mode="tn", grid=(N_CHIPS, d // tn, s_len // tk),
        a_spec=pl.BlockSpec((tk, 2 * hq), lambda i, j, k: (k, i)),
        b_specs=[pl.BlockSpec((tk, tn), lambda i, j, k: (k, j))],
        out_shape=_sds((2, N_CHIPS, hq, d), F32),
        out_specs=pl.BlockSpec((2, None, hq, tn), lambda i, j, k: (0, i, 0, j)),
        acc_shape=(2 * hq, tn), epilogue=_ep_halves(hq))
    dz = _matmul(
        "mix_dz", dmix, [w_out_2d], mode="nt", grid=(s_len // tm, d // tn, 1),
        a_spec=pl.BlockSpec((tm, d), lambda i, j, k: (i, 0)),
        b_specs=[pl.BlockSpec((tn, d), lambda i, j, k: (j, 0))],
        out_shape=_sds((s_len, d), BF16), out_specs=pl.BlockSpec((tm, tn), lambda i, j, k: (i, j)),
        acc_shape=(tm, tn), epilogue=_ep_store(BF16))
    dya, dyb, dgates, d_ba, d_ls, d_bb = _gates_bwd("mix_gates_bwd", proj, dz, ya, yb, bav, bb_full, lsv, wc, wp, ts)
    gw_a = _matmul(
        "mix_dw_a", a3, [dya], mode="tn", grid=(1, N_CHIPS, s_len // tk),
        a_spec=pl.BlockSpec((tk, wc), lambda i, j, k: (k, 0)),
        b_specs=[pl.BlockSpec((tk, dq), lambda i, j, k: (k, j))],
        out_shape=_sds((2, N_CHIPS, wc // 2, dq), F32),
        out_specs=pl.BlockSpec((2, None, wc // 2, dq), lambda i, j, k: (0, j, 0, 0)),
        acc_shape=(wc, dq), epilogue=_ep_halves(wc // 2))
    da3 = _matmul(
        "mix_da3", dya, [w_a_g], mode="nt", grid=(s_len // tm, 1, N_CHIPS),
        a_spec=pl.BlockSpec((tm, dq), lambda i, j, k: (i, k)),
        b_specs=[pl.BlockSpec((None, wc, dq), lambda i, j, k: (k, 0, 0))],
        out_shape=_sds((s_len, wc), BF16), out_specs=pl.BlockSpec((tm, wc), lambda i, j, k: (i, 0)),
        acc_shape=(tm, wc), epilogue=_ep_store(BF16))
    gpr = n_groups // 2

    def ep_by_chip(accs, ex, outs):
        for k in range(N_CHIPS):
            outs[0][k] = accs[0][:, k * goq:(k + 1) * goq]

    gw_b = _matmul(
        "mix_dw_b", mixed, [dyb], mode="tn", grid=(1, n_groups, s_len // tk),
        a_spec=pl.BlockSpec((tk, gi), lambda i, j, k: (k, j)),
        b_specs=[pl.BlockSpec((tk, dq), lambda i, j, k: (k, j))],
        out_shape=_sds((2, N_CHIPS, gpr * gi, goq), F32),
        out_specs=pl.BlockSpec((None, N_CHIPS, gi, goq), lambda i, j, k: (j // gpr, 0, j % gpr, 0)),
        acc_shape=(gi, dq), epilogue=ep_by_chip)
    dmixed = _matmul(
        "mix_dmixed", dyb, [w_b_r], mode="nt", grid=(s_len // tm, n_groups, 1),
        a_spec=pl.BlockSpec((tm, dq), lambda i, j, k: (i, j)),
        b_specs=[pl.BlockSpec((None, gi, dq), lambda i, j, k: (j, 0, 0))],
        out_shape=_sds((s_len, wp), BF16), out_specs=pl.BlockSpec((tm, gi), lambda i, j, k: (i, j)),
        acc_shape=(tm, gi), epilogue=_ep_store(BF16))
    da1, d_lg, d_lb, d_cb, d_cw = _conv_branch_bwd("mix_conv_bwd", proj, conv_out, da3, lgv, lbv, wc, wp, ts)
    dproj = _mixer_in_bwd("mix_in_bwd", proj, da1, dmixed, dgates, cw_pad, wc, wp, ts)
    hd = d // 2
    rt = hd // 2
    gw_in = _matmul(
        "mix_dw_in", n2, [dproj], mode="tn", grid=(N_CHIPS, 4, 1),
        a_spec=pl.BlockSpec((s_len, rt), lambda j, i, k: (0, i)),
        b_specs=[pl.BlockSpec((None, s_len, npc), lambda j, i, k: (j, 0, 0))],
        out_shape=_sds((2, N_CHIPS, hd, npc), F32),
        out_specs=pl.BlockSpec((None, None, rt, npc), lambda j, i, k: (i // 2, j, i % 2, 0)),
        acc_shape=(rt, npc), epilogue=_ep_store(F32))
    rs["mix"] = _ReduceScatter("g_mix", ["w_in", "w_a_out", "w_b_group", "w_out"], [gw_in, gw_a, gw_b, gw_out],
                               qc_idx)
    rs["ffn2"].step3()
    dn2 = _matmul(
        "mix_dn", dproj, [w_in_g], mode="nt", grid=(s_len // tm, d // tn, N_CHIPS),
        a_spec=pl.BlockSpec((None, tm, npc), lambda i, j, k: (k, i, 0)),
        b_specs=[pl.BlockSpec((None, tn, npc), lambda i, j, k: (k, j, 0))],
        out_shape=_sds((s_len, d), BF16), out_specs=pl.BlockSpec((tm, tn), lambda i, j, k: (i, j)),
        acc_shape=(tm, tn), epilogue=_ep_store(BF16))
    dh1, df1, d_sh2, d_sc2, d_gm, d_gt1 = _norm_mod_bwd("mix_norm_bwd", h1, dn2, dh2, gmv, sc2, ts,
                                                        prev=(f1, gt1, 0.5))
    rs["mix"].step2()

    def w1_in_ready(g):
        rs["w1_in"] = _ReduceScatter("g_w1_in", ["w1_in"], [g], qc_idx)
        rs["w1_out"].step2()
        rs["mix"].step3()

    dn1 = _ffn_bwd(
        "ffn1", n1, hu1, act1, df1, w1_in_g, w1_out_2d, dims,
        after_dw_out=lambda g: rs.update(w1_out=_ReduceScatter("g_w1_out", ["w1_out"], [g], qc_idx)),
        after_dw_in=w1_in_ready)
    grad_x, d_sh1, d_sc1, d_g1 = _norm_mod_bwd("ffn1_norm_bwd", x2, dn1, dh1, g1v, sc1, ts)

    d_ada = jnp.concatenate([d_sh1, d_sc1, d_gt1, d_sh2, d_sc2, d_gt2, d_sh3, d_sc3, d_gt3], axis=1)
    small = [d_ada, d_g1, d_gm, d_cw[:CONV_K].reshape(1, -1), d_cb, d_lg, d_lb, d_ba, d_bb, d_ls, d_g2, d_gf,
             loss_cols]
    sizes = [a.shape[1] for a in small]
    pack1 = jnp.concatenate(small, axis=1).reshape(-1)
    n1p = -(-pack1.shape[0] // (8 * LANES)) * LANES
    pack1 = jnp.pad(pack1, (0, 8 * n1p - pack1.shape[0])).reshape(8, n1p)
    g2 = _allgather_small("gather_small_grads", pack1)
    rs["w1_in"].step2()
    total = _sum_devices("sum_small_grads", g2, 8).reshape(-1)
    offs = [0]
    for sz in sizes:
        offs.append(offs[-1] + sz)
    tot = [total[offs[k]:offs[k + 1]] for k in range(len(sizes))]
    d_ada_all = g2.reshape(N_DEV, 8 * n1p)[:, :sizes[0]]
    loss = jnp.sum(tot[12])

    grads = {}
    grads["b_ada"] = tot[0]
    grads["g_ffn1"], grads["g_mix"] = tot[1], tot[2]
    grads["conv_w"] = lax.dynamic_slice(tot[3].reshape(CONV_K, wc), (0, q * cwq), (CONV_K, cwq))
    grads["conv_b"], grads["ln_a_g"], grads["ln_a_b"], grads["b_a_out"] = tot[4], tot[5], tot[6], tot[7]
    grads["b_b_group"] = lax.dynamic_slice(tot[8].reshape(n_groups, N_CHIPS * goq), (0, q * goq), (n_groups, goq))
    grads["ls_b"], grads["g_ffn2"], grads["g_final"] = tot[9], tot[10], tot[11]

    delta, new_m, new_v = {}, {}, {}

    def adamw_group(reduced):
        for nm, g in reduced.items():
            shp = weights[nm].shape
            go, dl, mo, vo = _adamw(f"adamw_{nm}", as2d(weights[nm]), g, as2d(mom1[nm]), as2d(mom2[nm]))
            grads[nm], delta[nm], new_m[nm], new_v[nm] = go.reshape(shp), dl.reshape(shp), mo.reshape(shp), vo.reshape(shp)

    adamw_group(rs["ffn2"].result())
    rs["w1_out"].step3()
    adamw_group(rs["mix"].result())
    d_ada_mine = lax.dynamic_slice(d_ada_all, (0, q * ada_c), (N_DEV, ada_c))
    grads["w_ada"], delta["w_ada"], new_m["w_ada"], new_v["w_ada"] = _ada_grad_adamw(
        "adamw_w_ada", c_all.T, d_ada_mine, w_ada, m_w_ada, v_w_ada)
    rs["w1_in"].step3()
    smalls = [nm for nm in order if nm not in big and nm != "w_ada"]
    flat = lambda src: jnp.concatenate([src[nm].reshape(-1) for nm in smalls])
    n_small = sum(weights[nm].size for nm in smalls)
    rows_s = -(-n_small // (8 * LANES)) * 8
    packed = [jnp.pad(flat(src), (0, rows_s * LANES - n_small)).reshape(rows_s, LANES)
              for src in (weights, grads, mom1, mom2)]
    _, dl_s, mo_s, vo_s = _adamw("adamw_small", *packed)
    off = 0
    for nm in smalls:
        sz, shp = weights[nm].size, weights[nm].shape
        delta[nm] = dl_s.reshape(-1)[off:off + sz].reshape(shp)
        new_m[nm] = mo_s.reshape(-1)[off:off + sz].reshape(shp)
        new_v[nm] = vo_s.reshape(-1)[off:off + sz].reshape(shp)
        grads[nm] = grads[nm].reshape(shp)
        off += sz
    adamw_group(rs["w1_out"].result())
    adamw_group(rs["w1_in"].result())

    return (loss, grad_x[None], *[grads[nm] for nm in order], *[delta[nm] for nm in order],
            *[new_m[nm] for nm in order], *[new_v[nm] for nm in order])
```

```python
import jax
import jax.numpy as jnp
from jax import lax
from jax.experimental import pallas as pl
from jax.experimental.pallas import tpu as pltpu

F32 = jnp.float32
BF16 = jnp.bfloat16
MESH = pl.DeviceIdType.MESH
ANY = pl.BlockSpec(memory_space=pl.ANY)
HBM = pl.BlockSpec(memory_space=pltpu.HBM)
SEM = pl.BlockSpec(memory_space=pltpu.SEMAPHORE)
EFFECT = pltpu.SideEffectType.DATAFLOW_SIDE_EFFECTING

EPS = 1e-6
CONV_K = 31
HALO = 32
POOL_WINDOWS = (2, 4, 8, 16)
N_CHIPS = 4
N_DEV = 8
LANES = 128

ADAM_LR = 0.001
ADAM_B1 = 0.9
ADAM_B2 = 0.999
ADAM_EPS = 1e-08
ADAM_WD = 0.01
ADAM_STEP = 10

DN = {
    "nn": (((1,), (0,)), ((), ())),
    "nt": (((1,), (1,)), ((), ())),
    "tn": (((0,), (0,)), ((), ())),
}


_PREVIOUS = []


def _ordered(call, args, n_lead, body, token=None, sources=()):
    dep = [pltpu.with_memory_space_constraint(p, pltpu.HBM) if p.size * p.dtype.itemsize >= (1 << 20) else p
           for p in _PREVIOUS if all(p is not a for a in (*args, *sources))]

    def wrapped(*refs):
        return body(*refs[:n_lead], *refs[n_lead + len(dep):])

    outs = call(wrapped, [ANY] * len(dep))(*args, *dep)
    seq = outs if isinstance(outs, (list, tuple)) else [outs]
    _PREVIOUS[:] = [seq[token] if token is not None else
                    next(o for o in seq if jnp.issubdtype(o.dtype, jnp.floating))]
    return outs


def _pcall(body, *, name, out_shape, grid=None, in_specs=None, out_specs=None, scratch=(), aliases=None,
           prefetch=0, vmem_mb=None):
    params = {}
    if grid is not None:
        params["dimension_semantics"] = ("arbitrary",) * len(grid)
    if vmem_mb is not None:
        params["vmem_limit_bytes"] = vmem_mb << 20
    def in_hbm(shape, spec):
        big = shape.size * jnp.dtype(shape.dtype).itemsize >= (1 << 20)
        return pltpu.HBM(shape.shape, shape.dtype) if big and getattr(spec, "memory_space", None) != pltpu.VMEM else shape

    if isinstance(out_shape, (list, tuple)):
        out_shape = [in_hbm(s, sp) for s, sp in zip(out_shape, out_specs)]
    else:
        out_shape = in_hbm(out_shape, out_specs)
    kw = dict(name=name, out_shape=out_shape, compiler_params=pltpu.CompilerParams(**params))
    if aliases:
        kw["input_output_aliases"] = aliases

    def call(wrapped, dep_specs):
        specs = list(in_specs) + dep_specs
        if prefetch:
            return pl.pallas_call(wrapped, grid_spec=pltpu.PrefetchScalarGridSpec(
                num_scalar_prefetch=prefetch, grid=grid, in_specs=specs, out_specs=out_specs,
                scratch_shapes=list(scratch)), **kw)
        if grid is not None:
            return pl.pallas_call(wrapped, grid=grid, in_specs=specs, out_specs=out_specs,
                                  scratch_shapes=list(scratch), **kw)
        return pl.pallas_call(wrapped, in_specs=specs, out_specs=out_specs, scratch_shapes=list(scratch), **kw)

    def run(*args):
        specs = [None] * prefetch + list(in_specs)
        placed = [pltpu.with_memory_space_constraint(a, pltpu.HBM)
                  if a.size * a.dtype.itemsize >= (1 << 20) and getattr(s, "memory_space", None) != pltpu.VMEM else a
                  for a, s in zip(args, specs)]
        return _ordered(call, placed, prefetch + len(in_specs), body, sources=args)

    return run


def _mult(offset, unit):
    return pl.multiple_of(offset, unit)


def _tile(dim, pref):
    t = min(dim, pref)
    assert dim % t == 0, (dim, pref)
    return t


def _sds(shape, dtype):
    return jax.ShapeDtypeStruct(tuple(shape), dtype)


def _sigmoid(v):
    return 0.5 * jnp.tanh(0.5 * v) + 0.5


def _vec(w):
    return pl.BlockSpec((1, w), lambda *_: (0, 0))


def _acc_rows(ref, val, i):
    @pl.when(i == 0)
    def _():
        ref[...] = jnp.zeros_like(ref)

    ref[...] += jnp.sum(val, axis=0, keepdims=True)


def _matmul(name, a, bs, *, mode, grid, a_spec, b_specs, out_shape, out_specs, acc_shape, epilogue,
            extras=(), extra_specs=(), vmem_mb=56, carry=(), col_block=None):
    nb, ne, nk, nc = len(bs), len(extras), grid[2], len(carry)
    dn = DN[mode]

    def body(*all_refs):
        refs = all_refs[:1 + nb + ne] + all_refs[1 + nb + ne + nc:]
        a_ref, b_refs, ex = refs[0], refs[1:1 + nb], refs[1 + nb:1 + nb + ne]
        if col_block:
            outs, av, width = refs[1 + nb + ne:], a_ref[...], b_refs[0].shape[-1]
            for lo in range(0, width, col_block):
                cs = slice(lo, min(lo + col_block, width))
                epilogue([lax.dot_general(av, b[:, cs], dn, preferred_element_type=F32) for b in b_refs], ex, outs, cs)
            return
        if nk == 1:
            outs = refs[1 + nb + ne:]
            accs = [lax.dot_general(a_ref[...], b[...], dn, preferred_element_type=F32) for b in b_refs]
            epilogue(accs, ex, outs)
            return
        outs, acc_refs = refs[1 + nb + ne:-nb], refs[-nb:]
        k = pl.program_id(2)

        @pl.when(k == 0)
        def _():
            for acc in acc_refs:
                acc[...] = jnp.zeros_like(acc)

        for acc, b in zip(acc_refs, b_refs):
            acc[...] += lax.dot_general(a_ref[...], b[...], dn, preferred_element_type=F32)

        @pl.when(k == nk - 1)
        def _():
            epilogue([acc[...] for acc in acc_refs], ex, outs)

    scratch = [pltpu.VMEM(acc_shape, F32) for _ in range(nb)] if nk > 1 else []
    return _pcall(body, name=name, out_shape=out_shape, grid=grid,
                  in_specs=[a_spec, *b_specs, *extra_specs, *[ANY] * nc], out_specs=out_specs, scratch=scratch,
                  aliases={1 + nb + ne + i: i for i in range(nc)}, vmem_mb=vmem_mb)(a, *bs, *extras, *carry)


def _ep_store(dtype):
    def ep(accs, ex, outs):
        outs[0][...] = accs[0].astype(dtype)
    return ep


def _ep_halves(h):
    def ep(accs, ex, outs):
        outs[0][0] = accs[0][:h]
        outs[0][1] = accs[0][h:]
    return ep


def _place():
    x, y, c = lax.axis_index("x"), lax.axis_index("y"), lax.axis_index("c")
    chips = [(1 - x, y), (x, 1 - y), (1 - x, 1 - y)]
    return x, y, c, chips


def _allgather_small(name, block):
    m_per, n = block.shape

    def body(x_ref, out_ref, send_sems, recv_sems, local_sem):
        x, y, c, chips = _place()
        me, sibling = (x, y, c), (x, y, 1 - c)

        def rows(px, py, pc):
            return out_ref.at[pl.ds((4 * px + 2 * py + pc) * m_per, m_per), :]

        def copy(k, blk, to, src=None):
            return pltpu.make_async_remote_copy(
                src_ref=rows(*blk) if src is None else src, dst_ref=rows(*blk),
                send_sem=send_sems.at[k], recv_sem=recv_sems.at[k], device_id=to, device_id_type=MESH)

        mine = pltpu.make_async_copy(x_ref, rows(*me), local_sem)
        mine.start()
        first = [copy(0, me, sibling, src=x_ref)]
        first += [copy(1 + j, me, (*chip, c), src=x_ref) for j, chip in enumerate(chips)]
        for cp in first:
            cp.start()
        passed = [copy(4 + j, (*chip, c), sibling) for j, chip in enumerate(chips)]
        for j, chip in enumerate(chips):
            copy(1 + j, (*chip, c), me).wait_recv()
            passed[j].start()
        copy(0, sibling, me).wait_recv()
        for j, chip in enumerate(chips):
            copy(4 + j, (*chip, 1 - c), me).wait_recv()
        for cp in first + passed:
            cp.wait_send()
        mine.wait()

    return _pcall(
        body, name=name, out_shape=_sds((N_DEV * m_per, n), block.dtype),
        in_specs=[pl.BlockSpec(memory_space=pltpu.VMEM)], out_specs=pl.BlockSpec(memory_space=pltpu.VMEM),
        scratch=[pltpu.SemaphoreType.DMA((7,)), pltpu.SemaphoreType.DMA((7,)), pltpu.SemaphoreType.DMA],
    )(block)


class _SplitCopies:
    def __init__(self, name, arrays, plan, n_copies):
        self.name, self.plan, self.n = name, plan, len(arrays)
        n = self.n

        def body(*refs):
            send, recv, token = refs[n], refs[n + 1], refs[-1]
            for k, (src, dst, _, peer) in enumerate(plan(refs[:n])):
                pltpu.make_async_remote_copy(src_ref=src, dst_ref=dst, send_sem=send.at[k], recv_sem=recv.at[k],
                                             device_id=peer, device_id_type=MESH).start()
            token[...] = jnp.zeros_like(token)

        def call(wrapped, dep_specs):
            return pl.pallas_call(
                wrapped, name=f"{name}_start",
                out_shape=(pltpu.SemaphoreType.DMA((n_copies,)), pltpu.SemaphoreType.DMA((n_copies,)),
                           *[pltpu.HBM(a.shape, a.dtype) for a in arrays], _sds((8, LANES), F32)),
                in_specs=[HBM] * n + dep_specs,
                out_specs=(SEM, SEM, *[HBM] * n, pl.BlockSpec(memory_space=pltpu.VMEM)),
                input_output_aliases={i: 2 + i for i in range(n)},
                compiler_params=pltpu.CompilerParams(has_side_effects=EFFECT))

        outs = _ordered(call, [pltpu.with_memory_space_constraint(a, pltpu.HBM) for a in arrays], n, body, token=-1,
                        sources=arrays)
        self.send, self.recv, self.arrays = outs[0], outs[1], list(outs[2:2 + n])

    def wait(self, arrays=None):
        n, plan = self.n, self.plan
        if arrays is not None:
            self.arrays = list(arrays)

        def body(*refs):
            send, recv, token = refs[n], refs[n + 1], refs[-1]
            for k, (src, _, landing, peer) in enumerate(plan(refs[:n])):
                cp = pltpu.make_async_remote_copy(src_ref=src, dst_ref=landing, send_sem=send.at[k],
                                                  recv_sem=recv.at[k], device_id=peer, device_id_type=MESH)
                cp.wait_send()
                cp.wait_recv()
            token[...] = jnp.zeros_like(token)

        def call(wrapped, dep_specs):
            return pl.pallas_call(
                wrapped, name=f"{self.name}_wait",
                out_shape=(*[pltpu.HBM(a.shape, a.dtype) for a in self.arrays], _sds((8, LANES), F32)),
                in_specs=[HBM] * n + [SEM, SEM] + dep_specs,
                out_specs=(*[HBM] * n, pl.BlockSpec(memory_space=pltpu.VMEM)),
                input_output_aliases={i: i for i in range(n)},
                compiler_params=pltpu.CompilerParams(has_side_effects=EFFECT))

        return list(_ordered(call, [*self.arrays, self.send, self.recv], n + 2, body, token=-1))[:n]


def _col_range(g, cols):
    lo, width = cols if cols is not None else (0, g.shape[-1])
    return (slice(None), pl.ds(lo, width))


def _gather_ici(name, gathered, cols=None):
    def plan(refs):
        x, y, c, chips = _place()
        q = 2 * x + y
        return [(g.at[(q, c, *_col_range(g, cols))], g.at[(q, c, *_col_range(g, cols))],
                 g.at[(2 * px + py, c, *_col_range(g, cols))], (px, py, c))
                for g in refs for px, py in chips]

    return _SplitCopies(name, gathered, plan, 3 * len(gathered))


def _gather_d2d(name, gathered, cols=None):
    def plan(refs):
        x, y, c, chips = _place()
        return [(g.at[(2 * px + py, c, *_col_range(g, cols))], g.at[(2 * px + py, c, *_col_range(g, cols))],
                 g.at[(2 * px + py, 1 - c, *_col_range(g, cols))], (x, y, 1 - c))
                for g in refs for px, py in chips]

    return _SplitCopies(name, gathered, plan, 3 * len(gathered))


MXU_COLS = 256


def _col_parts(width, n_parts):
    unit = MXU_COLS if width % MXU_COLS == 0 and width // MXU_COLS >= n_parts else width // n_parts
    units = width // unit
    sizes = [(units + k) // n_parts * unit for k in range(n_parts)]
    sizes[-1] += width - sum(sizes)
    return [(sum(sizes[:k]), sizes[k]) for k in range(n_parts)]


class _PartGather:
    def __init__(self, name, gathered, n_parts):
        self.name, self.d2d = name, {}
        self.parts = _col_parts(gathered.shape[-1], n_parts)
        self.ici = [_gather_ici(f"gather_{name}_0_ici", [gathered], self.parts[0])]
        self.buf = self.ici[0].arrays

    def start_rest(self):
        for part in range(1, len(self.parts)):
            self.ici.append(_gather_ici(f"gather_{self.name}_{part}_ici", self.buf, self.parts[part]))
            self.buf = self.ici[part].arrays

    def arrive(self, part):
        here = self.ici[part].wait(self.buf)
        self.d2d[part] = _gather_d2d(f"gather_{self.name}_{part}_d2d", here, self.parts[part])
        self.buf = self.d2d[part].arrays

    def ready(self, part):
        self.buf = self.d2d[part].wait(self.buf)
        g = self.buf[0]
        return g.reshape(N_CHIPS, 2 * g.shape[2], g.shape[3])


def _scatter_sibling(name, grads):
    n = len(grads)

    def plan(refs):
        x, y, c, _ = _place()
        return [(refs[w].at[1 - c], refs[n + w], refs[n + w], (x, y, 1 - c)) for w in range(n)]

    landing = [lax.empty(g.shape[1:], g.dtype) for g in grads]
    return _SplitCopies(name, [*grads, *landing], plan, n)


def _scatter_chips(name, sums):
    n = len(sums)

    def plan(refs):
        x, y, c, chips = _place()
        return [(refs[w].at[2 * px + py], refs[n + w].at[j], refs[n + w].at[j], (px, py, c))
                for w in range(n) for j, (px, py) in enumerate(chips)]

    landing = [lax.empty((3, *s.shape[1:]), s.dtype) for s in sums]
    return _SplitCopies(name, [*sums, *landing], plan, 3 * n)


def _share_final(name, finals):
    def plan(refs):
        x, y, c, _ = _place()
        return [(f.at[c], f.at[c], f.at[1 - c], (x, y, 1 - c)) for f in refs]

    return _SplitCopies(name, finals, plan, len(finals))


def _row_tile(rows, cols, budget_elems=786432):
    best = 8
    for t in range(8, rows + 1, 8):
        if rows % t == 0 and t * cols <= budget_elems:
            best = t
    return best if rows % best == 0 else rows


def _sum_with_sibling(name, grad, recv, qc_idx):
    _, _, h, cols = grad.shape
    tr = _row_tile(h, cols)

    def body(s_ref, g_ref, r_ref, own_ref, pb_ref):
        p = g_ref[...] + r_ref[...]
        pb_ref[...] = p.astype(BF16)

        @pl.when(pl.program_id(1) == s_ref[0])
        def _():
            own_ref[...] = p

    blk = pl.BlockSpec((None, tr, cols), lambda r, k, s: (k, r, 0))
    return _pcall(
        body, name=name, out_shape=[_sds((h, cols), F32), _sds((N_CHIPS, h, cols), BF16)],
        grid=(h // tr, N_CHIPS), prefetch=1,
        in_specs=[pl.BlockSpec((None, None, tr, cols), lambda r, k, s: (s[1], k, r, 0)), blk],
        out_specs=[pl.BlockSpec((tr, cols), lambda r, k, s: (r, 0)), blk], vmem_mb=32,
    )(qc_idx, grad, recv)


def _sum_chips(name, own, recv, qc_idx):
    h, cols = own.shape
    tr = _row_tile(h, cols)

    def body(s_ref, p_ref, t_ref, o_ref):
        o_ref[...] = ((p_ref[...] + t_ref[0].astype(F32)) + t_ref[1].astype(F32)) + t_ref[2].astype(F32)

    return _pcall(
        body, name=name, out_shape=_sds((2, h, cols), F32), grid=(h // tr,), prefetch=1,
        in_specs=[pl.BlockSpec((tr, cols), lambda r, s: (r, 0)),
                  pl.BlockSpec((3, tr, cols), lambda r, s: (0, r, 0))],
        out_specs=pl.BlockSpec((None, tr, cols), lambda r, s: (s[1], r, 0)), vmem_mb=32,
    )(qc_idx, own, recv)


class _ReduceScatter:
    def __init__(self, tag, names, grads, qc_idx):
        self.tag, self.names, self.n, self.qc_idx = tag, names, len(grads), qc_idx
        self.copies = _scatter_sibling(f"{tag}_rs_sibling", grads)

    def step2(self):
        n = self.n
        arrs = self.copies.wait()
        sums = [_sum_with_sibling(f"{nm}_sum_sibling", arrs[w], arrs[n + w], self.qc_idx)
                for w, nm in enumerate(self.names)]
        self.own = [s[0] for s in sums]
        self.copies = _scatter_chips(f"{self.tag}_rs_chips", [s[1] for s in sums])

    def step3(self):
        n = self.n
        arrs = self.copies.wait()
        finals = [_sum_chips(f"{nm}_sum_chips", self.own[w], arrs[n + w], self.qc_idx)
                  for w, nm in enumerate(self.names)]
        self.copies = _share_final(f"{self.tag}_rs_final", finals)

    def result(self):
        return {nm: f.reshape(2 * f.shape[1], f.shape[2]) for nm, f in zip(self.names, self.copies.wait())}


def _cast_into_gathered(name, w, q_idx):
    rows, cols = w.shape
    h = rows // 2
    tr = _row_tile(h, cols, 1 << 20)
    nr = h // tr

    def body(s_ref, w_ref, o_ref):
        o_ref[...] = w_ref[...].astype(BF16)

    return _pcall(body, name=name, out_shape=_sds((N_CHIPS, 2, h, cols), BF16), grid=(2, nr), prefetch=1,
                  in_specs=[pl.BlockSpec((tr, cols), lambda hf, r, s: (hf * nr + r, 0))],
                  out_specs=pl.BlockSpec((None, None, tr, cols), lambda hf, r, s: (s[0], hf, r, 0)),
                  vmem_mb=32)(q_idx, w)


def _regroup(name, w, n_groups):
    n_chips, rows, goq = w.shape
    gi = rows // n_groups

    def body(w_ref, o_ref):
        o_ref[...] = w_ref[...]

    return _pcall(body, name=name, out_shape=_sds((n_groups, gi, n_chips * goq), w.dtype), grid=(n_groups, n_chips),
                  in_specs=[pl.BlockSpec((None, gi, goq), lambda g, k: (k, g, 0))],
                  out_specs=pl.BlockSpec((None, gi, goq), lambda g, k: (g, 0, k)), vmem_mb=32)(w)


def _rms(h):
    r = lax.rsqrt(jnp.mean(h * h, axis=-1, keepdims=True) + EPS)
    return r, h * r


def _norm_mod(name, h, g, sc, sh, ts):
    s_len, d = h.shape

    def body(h_ref, g_ref, sc_ref, sh_ref, n_ref):
        _, xhat = _rms(h_ref[...])
        n_ref[...] = ((xhat * g_ref[...]) * (1.0 + sc_ref[...]) + sh_ref[...]).astype(BF16)

    row = pl.BlockSpec((ts, d), lambda i: (i, 0))
    return _pcall(body, name=name, out_shape=_sds((s_len, d), BF16), grid=(s_len // ts,),
                  in_specs=[row, _vec(d), _vec(d), _vec(d)], out_specs=row, vmem_mb=32)(h, g, sc, sh)


def _residual_norm_mod(name, h, f, gate, cmul, g, sc, sh, ts):
    s_len, d = h.shape

    def body(h_ref, f_ref, gt_ref, g_ref, sc_ref, sh_ref, ho_ref, n_ref):
        hn = h_ref[...] + (cmul * gt_ref[...]) * f_ref[...]
        ho_ref[...] = hn
        _, xhat = _rms(hn)
        n_ref[...] = ((xhat * g_ref[...]) * (1.0 + sc_ref[...]) + sh_ref[...]).astype(BF16)

    row = pl.BlockSpec((ts, d), lambda i: (i, 0))
    return _pcall(body, name=name, out_shape=[_sds((s_len, d), F32), _sds((s_len, d), BF16)],
                  grid=(s_len // ts,), in_specs=[row, row, _vec(d), _vec(d), _vec(d), _vec(d)],
                  out_specs=[row, row], vmem_mb=32)(h, f, gate, g, sc, sh)


def _final_loss(name, h, f, tgt, gate, cmul, g, ts):
    s_len, d = h.shape

    def body(h_ref, f_ref, t_ref, gt_ref, g_ref, dh_ref, df_ref, dg_ref, dgt_ref, loss_ref):
        i = pl.program_id(0)
        fv = f_ref[...]
        coef = cmul * gt_ref[...]
        hn = h_ref[...] + coef * fv
        r, xhat = _rms(hn)
        err = xhat * g_ref[...] - t_ref[...]
        _acc_rows(loss_ref, (0.5 / d) * (err * err), i)
        dy = err * (1.0 / d)
        _acc_rows(dg_ref, dy * xhat, i)
        dxhat = dy * g_ref[...]
        dh = r * (dxhat - xhat * jnp.mean(dxhat * xhat, axis=-1, keepdims=True))
        dh_ref[...] = dh
        _acc_rows(dgt_ref, cmul * (dh * fv), i)
        df_ref[...] = (coef * dh).astype(BF16)

    row = pl.BlockSpec((ts, d), lambda i: (i, 0))
    return _pcall(body, name=name,
                  out_shape=[_sds((s_len, d), F32), _sds((s_len, d), BF16)] + [_sds((1, d), F32)] * 3,
                  grid=(s_len // ts,), in_specs=[row, row, row, _vec(d), _vec(d)],
                  out_specs=[row, row, _vec(d), _vec(d), _vec(d)], vmem_mb=40)(h, f, tgt, gate, g)


def _norm_mod_bwd(name, h, dn, dh_next, g, sc, ts, prev=None):
    s_len, d = h.shape
    has_prev = prev is not None
    cmul = prev[2] if has_prev else None

    def body(*refs):
        if has_prev:
            h_ref, dn_ref, dhn_ref, f_ref, g_ref, sc_ref, gt_ref, dh_ref, df_ref, dsh_ref, dsc_ref, dg_ref, dgt_ref = refs
        else:
            h_ref, dn_ref, dhn_ref, g_ref, sc_ref, dh_ref, dsh_ref, dsc_ref, dg_ref = refs
        i = pl.program_id(0)
        r, xhat = _rms(h_ref[...])
        dn_v = dn_ref[...].astype(F32)
        gv = g_ref[...]
        _acc_rows(dsh_ref, dn_v, i)
        _acc_rows(dsc_ref, dn_v * (xhat * gv), i)
        dnrm = dn_v * (1.0 + sc_ref[...])
        _acc_rows(dg_ref, dnrm * xhat, i)
        dxhat = dnrm * gv
        dh = dhn_ref[...] + r * (dxhat - xhat * jnp.mean(dxhat * xhat, axis=-1, keepdims=True))
        dh_ref[...] = dh
        if has_prev:
            _acc_rows(dgt_ref, cmul * (dh * f_ref[...]), i)
            df_ref[...] = ((cmul * gt_ref[...]) * dh).astype(BF16)

    row = pl.BlockSpec((ts, d), lambda i: (i, 0))
    if has_prev:
        ins, in_specs = [h, dn, dh_next, prev[0], g, sc, prev[1]], [row, row, row, row, _vec(d), _vec(d), _vec(d)]
        out_shape = [_sds((s_len, d), F32), _sds((s_len, d), BF16)] + [_sds((1, d), F32)] * 4
        out_specs = [row, row] + [_vec(d)] * 4
    else:
        ins, in_specs = [h, dn, dh_next, g, sc], [row, row, row, _vec(d), _vec(d)]
        out_shape = [_sds((s_len, d), F32)] + [_sds((1, d), F32)] * 3
        out_specs = [row] + [_vec(d)] * 3
    return _pcall(body, name=name, out_shape=out_shape, grid=(s_len // ts,), in_specs=in_specs,
                  out_specs=out_specs, vmem_mb=40)(*ins)


def _cols(ref, lo, hi, npc, rows=slice(None)):
    parts = []
    while lo < hi:
        q, o = divmod(lo, npc)
        n = min(hi - lo, npc - o)
        parts.append(ref[q, rows, o:o + n].astype(F32))
        lo += n
    return parts[0] if len(parts) == 1 else jnp.concatenate(parts, axis=-1)


def _store_cols(ref, lo, val, npc, rows=slice(None)):
    off, width = 0, val.shape[-1]
    while off < width:
        q, o = divmod(lo + off, npc)
        n = min(width - off, npc - o)
        ref[q, rows, o:o + n] = val[:, off:off + n]
        off += n


def _chips_covering(cols, npc):
    return -(-cols // npc)


SUBLANES = 8
ROW_CHUNK = 32


def _make_phases(src_ref, ph_ref):
    rows = src_ref.shape[0] - SUBLANES
    for b in range(1, SUBLANES):
        ph_ref[b - 1] = src_ref[pl.ds(b, rows), :]


def _window(src_ref, ph_ref, off, r0, cols=slice(None)):
    a, b = divmod(off, SUBLANES)
    start = pl.multiple_of(r0 + SUBLANES * a, SUBLANES)
    if b == 0:
        return src_ref[pl.ds(start, ROW_CHUNK), cols]
    return ph_ref[b - 1, pl.ds(start, ROW_CHUNK), cols]


def _phase_scratch(rows, width):
    return pltpu.VMEM((SUBLANES - 1, rows - SUBLANES, width), F32)


def _conv(a0s_ref, a0p_ref, cw_ref, cb_ref, r0):
    a1 = cb_ref[...] + cw_ref[0:1, :] * _window(a0s_ref, a0p_ref, HALO - CONV_K + 1, r0)
    for k in range(1, CONV_K):
        a1 = a1 + cw_ref[k:k + 1, :] * _window(a0s_ref, a0p_ref, HALO - CONV_K + 1 + k, r0)
    return a1


def _layer_norm(a1, lg_ref, lb_ref):
    mu = jnp.mean(a1, axis=-1, keepdims=True)
    ctr = a1 - mu
    rstd = lax.rsqrt(jnp.mean(ctr * ctr, axis=-1, keepdims=True) + EPS)
    xh = ctr * rstd
    return xh, rstd, xh * lg_ref[...] + lb_ref[...]


def _for_chunks(ts, fn):
    def step(ci, carry):
        fn(pl.multiple_of(ci * ROW_CHUNK, ROW_CHUNK))
        return carry

    lax.fori_loop(0, ts // ROW_CHUNK, step, 0)


def _stage_glu(p_ref, ph_ref, a0s_ref, i, wc, npc, ts):
    a0 = _cols(p_ref, 0, wc, npc) * _sigmoid(_cols(p_ref, wc, 2 * wc, npc))
    a0h = _cols(ph_ref, 0, wc, npc) * _sigmoid(_cols(ph_ref, wc, 2 * wc, npc))
    a0s_ref[0:HALO, :] = jnp.where(i > 0, a0h, 0.0)
    a0s_ref[HALO:HALO + ts, :] = a0


def _mixer_mid(name, proj, cw, cb, lg, lb, wc, wp, ts):
    _, s_len, npc = proj.shape
    nq = _chips_covering(2 * wc + wp, npc)
    gi = wp // len(POOL_WINDOWS)
    hb = ts // HALO

    def body(p_ref, ph_ref, cw_ref, cb_ref, lg_ref, lb_ref, a3_ref, mx_ref, a1_ref, a0s_ref, vs_ref, a0p_ref,
             vp_ref):
        i = pl.program_id(0)
        _stage_glu(p_ref, ph_ref, a0s_ref, i, wc, npc, ts)
        vs_ref[0:HALO, :] = jnp.where(i > 0, _cols(ph_ref, 2 * wc, 2 * wc + wp, npc), 0.0)
        vs_ref[HALO:HALO + ts, :] = _cols(p_ref, 2 * wc, 2 * wc + wp, npc)
        _make_phases(a0s_ref, a0p_ref)
        _make_phases(vs_ref, vp_ref)

        def chunk(r0):
            rows = pl.ds(r0, ROW_CHUNK)
            a1 = _conv(a0s_ref, a0p_ref, cw_ref, cb_ref, r0)
            a1_ref[rows, :] = a1
            _, _, a2 = _layer_norm(a1, lg_ref, lb_ref)
            a3_ref[rows, :] = (a2 * _sigmoid(a2)).astype(BF16)
            t_abs = i * ts + r0 + lax.broadcasted_iota(jnp.int32, (ROW_CHUNK, 1), 0)
            for g, win in enumerate(POOL_WINDOWS):
                cs = slice(g * gi, (g + 1) * gi)
                v_now = _window(vs_ref, vp_ref, HALO, r0, cs)
                acc = v_now
                for dlt in range(1, win):
                    acc = acc + _window(vs_ref, vp_ref, HALO - dlt, r0, cs)
                cnt = jnp.minimum(t_abs + 1, win).astype(F32)
                mx_ref[rows, cs] = (acc / cnt - v_now).astype(BF16)

        _for_chunks(ts, chunk)

    return _pcall(
        body, name=name, out_shape=[_sds((s_len, wc), BF16), _sds((s_len, wp), BF16), _sds((s_len, wc), F32)],
        grid=(s_len // ts,),
        in_specs=[pl.BlockSpec((nq, ts, npc), lambda i: (0, i, 0)),
                  pl.BlockSpec((nq, HALO, npc), lambda i: (0, jnp.maximum(i * hb - 1, 0), 0)),
                  pl.BlockSpec((HALO, wc), lambda i: (0, 0)), _vec(wc), _vec(wc), _vec(wc)],
        out_specs=[pl.BlockSpec((ts, wc), lambda i: (i, 0)), pl.BlockSpec((ts, wp), lambda i: (i, 0)),
                   pl.BlockSpec((ts, wc), lambda i: (i, 0))],
        scratch=[pltpu.VMEM((HALO + ts, wc), F32), pltpu.VMEM((HALO + ts, wp), F32),
                 _phase_scratch(HALO + ts, wc), _phase_scratch(HALO + ts, wp)], vmem_mb=56,
    )(proj, proj, cw, cb, lg, lb)


def _gates_fwd(name, proj, ya, yb, b_a, b_b, ls, wc, wp, ts):
    _, s_len, npc = proj.shape
    d = ya.shape[1]
    g0 = 2 * wc + wp

    def body(p_ref, ya_ref, yb_ref, ba_ref, bb_ref, ls_ref, z_ref):
        ga = _sigmoid(_cols(p_ref, g0, g0 + d, npc))
        gb = _sigmoid(_cols(p_ref, g0 + d, g0 + 2 * d, npc))
        z = ga * (ya_ref[...] + ba_ref[...]) + gb * ((yb_ref[...] + bb_ref[...]) * ls_ref[...])
        z_ref[...] = z.astype(BF16)

    row = pl.BlockSpec((ts, d), lambda i: (i, 0))
    return _pcall(body, name=name, out_shape=_sds((s_len, d), BF16), grid=(s_len // ts,),
                  in_specs=[pl.BlockSpec((N_CHIPS, ts, npc), lambda i: (0, i, 0)), row, row, _vec(d), _vec(d), _vec(d)],
                  out_specs=row, vmem_mb=48)(proj, ya, yb, b_a, b_b, ls)


def _gates_bwd(name, proj, dz, ya, yb, b_a, b_b, ls, wc, wp, ts):
    _, s_len, npc = proj.shape
    d = ya.shape[1]
    g0 = 2 * wc + wp

    def body(p_ref, dz_ref, ya_ref, yb_ref, ba_ref, bb_ref, ls_ref, dya_ref, dyb_ref, dgt_ref, dba_ref, dls_ref,
             dbb_ref):
        i = pl.program_id(0)
        ga = _sigmoid(_cols(p_ref, g0, g0 + d, npc))
        gb = _sigmoid(_cols(p_ref, g0 + d, g0 + 2 * d, npc))
        dz_v = dz_ref[...].astype(F32)
        y_a = ya_ref[...] + ba_ref[...]
        y_b0 = yb_ref[...] + bb_ref[...]
        ls_v = ls_ref[...]
        dya = dz_v * ga
        dya_ref[...] = dya.astype(BF16)
        _acc_rows(dba_ref, dya, i)
        t = dz_v * gb
        _acc_rows(dls_ref, t * y_b0, i)
        dyb = t * ls_v
        dyb_ref[...] = dyb.astype(BF16)
        _acc_rows(dbb_ref, dyb, i)
        dgt_ref[:, 0:d] = (dz_v * y_a * ga * (1.0 - ga)).astype(BF16)
        dgt_ref[:, d:2 * d] = (dz_v * (y_b0 * ls_v) * gb * (1.0 - gb)).astype(BF16)

    row = pl.BlockSpec((ts, d), lambda i: (i, 0))
    return _pcall(
        body, name=name,
        out_shape=[_sds((s_len, d), BF16), _sds((s_len, d), BF16), _sds((s_len, 2 * d), BF16)] + [_sds((1, d), F32)] * 3,
        grid=(s_len // ts,),
        in_specs=[pl.BlockSpec((N_CHIPS, ts, npc), lambda i: (0, i, 0)), row, row, row, _vec(d), _vec(d), _vec(d)],
        out_specs=[row, row, pl.BlockSpec((ts, 2 * d), lambda i: (i, 0))] + [_vec(d)] * 3, vmem_mb=48,
    )(proj, dz, ya, yb, b_a, b_b, ls)


def _conv_branch_bwd(name, proj, a1, da3, lg, lb, wc, wp, ts):
    _, s_len, npc = proj.shape
    nq = _chips_covering(2 * wc, npc)
    hb = ts // HALO

    n_tiles = s_len // ts

    def fold(v):
        return jnp.sum(v.reshape(ROW_CHUNK // SUBLANES, SUBLANES, v.shape[-1]), axis=0)

    def body(p_ref, ph_ref, a1_ref, da3_ref, lg_ref, lb_ref, da1_ref, dlg_ref, dlb_ref, dcb_ref, dcw_ref,
             a0s_ref, a0p_ref, vec8_ref, dcw8_ref):
        i = pl.program_id(0)
        _stage_glu(p_ref, ph_ref, a0s_ref, i, wc, npc, ts)
        _make_phases(a0s_ref, a0p_ref)

        @pl.when(i == 0)
        def _():
            vec8_ref[...] = jnp.zeros_like(vec8_ref)
            dcw8_ref[...] = jnp.zeros_like(dcw8_ref)

        def chunk(r0):
            rows = pl.ds(r0, ROW_CHUNK)
            xh, rstd, a2 = _layer_norm(a1_ref[rows, :], lg_ref, lb_ref)
            sig = _sigmoid(a2)
            da2 = da3_ref[rows, :].astype(F32) * (sig * (1.0 + a2 * (1.0 - sig)))
            vec8_ref[0] += fold(da2 * xh)
            vec8_ref[1] += fold(da2)
            dxh = da2 * lg_ref[...]
            da1 = rstd * (dxh - jnp.mean(dxh, axis=-1, keepdims=True)
                          - xh * jnp.mean(dxh * xh, axis=-1, keepdims=True))
            da1_ref[rows, :] = da1
            vec8_ref[2] += fold(da1)
            for k in range(CONV_K):
                dcw8_ref[k] += fold(da1 * _window(a0s_ref, a0p_ref, HALO - CONV_K + 1 + k, r0))

        _for_chunks(ts, chunk)

        @pl.when(i == n_tiles - 1)
        def _():
            dlg_ref[...] = jnp.sum(vec8_ref[0], axis=0, keepdims=True)
            dlb_ref[...] = jnp.sum(vec8_ref[1], axis=0, keepdims=True)
            dcb_ref[...] = jnp.sum(vec8_ref[2], axis=0, keepdims=True)
            dcw_ref[...] = jnp.sum(dcw8_ref[...], axis=1)

    return _pcall(
        body, name=name,
        out_shape=[_sds((s_len, wc), F32)] + [_sds((1, wc), F32)] * 3 + [_sds((HALO, wc), F32)],
        grid=(s_len // ts,),
        in_specs=[pl.BlockSpec((nq, ts, npc), lambda i: (0, i, 0)),
                  pl.BlockSpec((nq, HALO, npc), lambda i: (0, jnp.maximum(i * hb - 1, 0), 0)),
                  pl.BlockSpec((ts, wc), lambda i: (i, 0)), pl.BlockSpec((ts, wc), lambda i: (i, 0)),
                  _vec(wc), _vec(wc)],
        out_specs=[pl.BlockSpec((ts, wc), lambda i: (i, 0)), _vec(wc), _vec(wc), _vec(wc),
                   pl.BlockSpec((HALO, wc), lambda i: (0, 0))],
        scratch=[pltpu.VMEM((HALO + ts, wc), F32), _phase_scratch(HALO + ts, wc),
                 pltpu.VMEM((3, SUBLANES, wc), F32), pltpu.VMEM((HALO, SUBLANES, wc), F32)], vmem_mb=56,
    )(proj, proj, a1, da3, lg, lb)


def _mixer_in_bwd(name, proj, da1, dmixed, dgates, cw, wc, wp, ts):
    _, s_len, npc = proj.shape
    nq = _chips_covering(2 * wc, npc)
    gi = wp // len(POOL_WINDOWS)
    hb = ts // HALO
    n_tiles = s_len // ts
    last_hb = s_len // HALO - 1
    d2 = dgates.shape[1]

    def body(p_ref, d1_ref, d1n_ref, dm_ref, dmn_ref, dgt_ref, cw_ref, o_ref, d1s_ref, es_ref, d1p_ref, ep_ref):
        i = pl.program_id(0)
        more = i < n_tiles - 1
        d1s_ref[0:ts, :] = d1_ref[...]
        d1s_ref[ts:ts + HALO, :] = jnp.where(more, d1n_ref[...], 0.0)
        t_abs = i * ts + lax.broadcasted_iota(jnp.int32, (ts + HALO, 1), 0)
        dm_ext = jnp.concatenate([dm_ref[...].astype(F32), jnp.where(more, dmn_ref[...].astype(F32), 0.0)], axis=0)
        for g, win in enumerate(POOL_WINDOWS):
            cs = slice(g * gi, (g + 1) * gi)
            es_ref[:, cs] = dm_ext[:, cs] / jnp.minimum(t_abs + 1, win).astype(F32)
        _make_phases(d1s_ref, d1p_ref)
        _make_phases(es_ref, ep_ref)

        def chunk(r0):
            rows = pl.ds(r0, ROW_CHUNK)
            da0 = cw_ref[0:1, :] * _window(d1s_ref, d1p_ref, CONV_K - 1, r0)
            for k in range(1, CONV_K):
                da0 = da0 + cw_ref[k:k + 1, :] * _window(d1s_ref, d1p_ref, CONV_K - 1 - k, r0)
            glu_a = _cols(p_ref, 0, wc, npc, rows)
            sig = _sigmoid(_cols(p_ref, wc, 2 * wc, npc, rows))
            _store_cols(o_ref, 0, (da0 * sig).astype(BF16), npc, rows)
            _store_cols(o_ref, wc, (da0 * glu_a * sig * (1.0 - sig)).astype(BF16), npc, rows)
            parts = []
            for g, win in enumerate(POOL_WINDOWS):
                cs = slice(g * gi, (g + 1) * gi)
                acc = _window(es_ref, ep_ref, 0, r0, cs)
                for dlt in range(1, win):
                    acc = acc + _window(es_ref, ep_ref, dlt, r0, cs)
                parts.append(acc - dm_ref[rows, cs].astype(F32))
            _store_cols(o_ref, 2 * wc, jnp.concatenate(parts, axis=-1).astype(BF16), npc, rows)

        _for_chunks(ts, chunk)
        _store_cols(o_ref, 2 * wc + wp, dgt_ref[...], npc)

    nxt = lambda i: (jnp.minimum((i + 1) * hb, last_hb), 0)
    return _pcall(
        body, name=name, out_shape=_sds((N_CHIPS, s_len, npc), BF16), grid=(n_tiles,),
        in_specs=[pl.BlockSpec((nq, ts, npc), lambda i: (0, i, 0)),
                  pl.BlockSpec((ts, wc), lambda i: (i, 0)), pl.BlockSpec((HALO, wc), nxt),
                  pl.BlockSpec((ts, wp), lambda i: (i, 0)), pl.BlockSpec((HALO, wp), nxt),
                  pl.BlockSpec((ts, d2), lambda i: (i, 0)),
                  pl.BlockSpec((HALO, wc), lambda i: (0, 0))],
        out_specs=pl.BlockSpec((N_CHIPS, ts, npc), lambda i: (0, i, 0)),
        scratch=[pltpu.VMEM((ts + HALO, wc), F32), pltpu.VMEM((ts + HALO, wp), F32),
                 _phase_scratch(ts + HALO, wc), _phase_scratch(ts + HALO, wp)], vmem_mb=56,
    )(proj, da1, da1, dmixed, dmixed, dgates, cw)


def _ada_fwd(name, c_all, w, b):
    d, cols = w.shape
    tn = 512 if cols % 512 == 0 else cols

    def body(c_ref, w_ref, b_ref, o_ref):
        cv = c_ref[...]
        sc = (cv * _sigmoid(cv)).astype(BF16)
        o_ref[...] = jnp.dot(sc, w_ref[...].astype(BF16), preferred_element_type=F32) + b_ref[...]

    return _pcall(body, name=name, out_shape=_sds((N_DEV, cols), F32), grid=(cols // tn,),
                  in_specs=[pl.BlockSpec((N_DEV, d), lambda j: (0, 0)), pl.BlockSpec((d, tn), lambda j: (0, j)),
                            pl.BlockSpec((1, tn), lambda j: (0, j))],
                  out_specs=pl.BlockSpec((N_DEV, tn), lambda j: (0, j)), vmem_mb=32)(c_all, w, b)


def _adam_math(w, g, m, v):
    m_new = ADAM_B1 * m + (1.0 - ADAM_B1) * g
    v_new = ADAM_B2 * v + (1.0 - ADAM_B2) * (g * g)
    m_hat = m_new / (1.0 - ADAM_B1 ** ADAM_STEP)
    v_hat = v_new / (1.0 - ADAM_B2 ** ADAM_STEP)
    delta = -ADAM_LR * (m_hat / (jnp.sqrt(v_hat) + ADAM_EPS) + ADAM_WD * w)
    return delta, m_new, v_new


def _adamw(name, w, g, m, v):
    rows, cols = w.shape
    tr = _row_tile(rows, cols, 524288)

    def body(w_ref, g_ref, m_ref, v_ref, go_ref, d_ref, mo_ref, vo_ref):
        g = g_ref[...]
        go_ref[...] = g
        d_ref[...], mo_ref[...], vo_ref[...] = _adam_math(w_ref[...], g, m_ref[...], v_ref[...])

    spec = pl.BlockSpec((tr, cols), lambda i: (i, 0))
    return _pcall(body, name=name, out_shape=[_sds(w.shape, F32)] * 4, grid=(rows // tr,), in_specs=[spec] * 4,
                  out_specs=[spec] * 4, vmem_mb=40)(w, g, m, v)


def _ada_grad_adamw(name, c_t, d_ada, w, m, v):
    rows, cols = w.shape
    tr = _tile(rows, 256)
    tc = _tile(cols, 1536) if cols % 1536 == 0 else cols

    def body(c_ref, da_ref, w_ref, m_ref, v_ref, g_ref, d_ref, mo_ref, vo_ref):
        cv = c_ref[...]
        sc = cv * _sigmoid(cv)
        g = sc[:, 0:1] * da_ref[0:1, :]
        for b in range(1, N_DEV):
            g = g + sc[:, b:b + 1] * da_ref[b:b + 1, :]
        g_ref[...] = g
        d_ref[...], mo_ref[...], vo_ref[...] = _adam_math(w_ref[...], g, m_ref[...], v_ref[...])

    spec = pl.BlockSpec((tr, tc), lambda i, j: (i, j))
    return _pcall(body, name=name, out_shape=[_sds(w.shape, F32)] * 4, grid=(rows // tr, cols // tc),
                  in_specs=[pl.BlockSpec((tr, N_DEV), lambda i, j: (i, 0)),
                            pl.BlockSpec((N_DEV, tc), lambda i, j: (0, j)), spec, spec, spec],
                  out_specs=[spec] * 4, vmem_mb=40)(c_t, d_ada, w, m, v)


def _sum_devices(name, gathered, m_per):
    n = gathered.shape[1]

    def body(g_ref, o_ref):
        acc = g_ref[0:m_per, :]
        for dev in range(1, N_DEV):
            acc = acc + g_ref[dev * m_per:(dev + 1) * m_per, :]
        o_ref[...] = acc

    return _pcall(body, name=name, out_shape=_sds((m_per, n), F32),
                  in_specs=[pl.BlockSpec(memory_space=pltpu.VMEM)],
                  out_specs=pl.BlockSpec(memory_space=pltpu.VMEM))(gathered)


def _ffn_fwd(tag, n, w_in_parts, w_out_after_swiglu, dims):
    s_len, d, f_dim = dims["S"], dims["D"], dims["F"]
    tf = f_dim // 4
    tm0, tm = _tile(s_len, 512), _tile(s_len, 1024)
    p = f_dim // 2

    def ep(accs, ex, outs, cs=slice(None)):
        hh, uu = accs
        sig = _sigmoid(hh)
        silu = hh * sig
        outs[0][0, :, cs] = (uu * (sig + silu * (1.0 - sig))).astype(BF16)
        outs[0][1, :, cs] = silu.astype(BF16)
        outs[1][:, cs] = (silu * uu).astype(BF16)

    done = ()
    for part, (get_w, cols) in enumerate(w_in_parts):
        w_g = get_w().reshape(N_CHIPS * d, p)
        lo, width = cols if cols is not None else (0, p)
        mode_kw = dict(pipeline_mode=pl.Buffered(1)) if cols is None else {}
        el = pl.Element
        done = _matmul(
            f"{tag}_swiglu{part}", n, [w_g, w_g], mode="nn", grid=(2, s_len // tm0, 1),
            a_spec=pl.BlockSpec((tm0, d), lambda j, i, k: (i, 0)),
            b_specs=[pl.BlockSpec((el(d), el(width)), lambda j, i, k, lo=lo: (_mult(j * d, d), lo), **mode_kw),
                     pl.BlockSpec((el(d), el(width)), lambda j, i, k, lo=lo: (_mult((2 + j) * d, d), lo), **mode_kw)],
            out_shape=[_sds((2, s_len, f_dim), BF16), _sds((s_len, f_dim), BF16)],
            out_specs=[pl.BlockSpec((el(2), el(tm0), el(width)),
                                    lambda j, i, k, lo=lo: (0, _mult(i * tm0, tm0), _mult(j * p + lo, LANES))),
                       pl.BlockSpec((el(tm0), el(width)),
                                    lambda j, i, k, lo=lo: (_mult(i * tm0, tm0), _mult(j * p + lo, LANES)))],
            acc_shape=(tm0, width), epilogue=ep, carry=done, col_block=512 if cols is None else None)
    hu, act = done
    w_out2d = w_out_after_swiglu()
    tn2 = _tile(d, 1024)
    f = _matmul(
        f"{tag}_down", act, [w_out2d], mode="nn", grid=(s_len // tm, d // tn2, 2),
        a_spec=pl.BlockSpec((tm, 2 * tf), lambda i, j, k: (i, k)),
        b_specs=[pl.BlockSpec((2 * tf, tn2), lambda i, j, k: (k, j))],
        out_shape=_sds((s_len, d), F32), out_specs=pl.BlockSpec((tm, tn2), lambda i, j, k: (i, j)),
        acc_shape=(tm, tn2), epilogue=_ep_store(F32))
    return hu, act, f, w_out2d


def _ffn_bwd(tag, n, hu, act, df, w_in_g, w_out2d, dims, after_dw_out, after_dw_in):
    s_len, d, f_dim = dims["S"], dims["D"], dims["F"]
    tf = f_dim // 4
    tk = _tile(s_len, 2048)
    tn = _tile(d, 1024)
    g_out = _matmul(
        f"{tag}_dw_out", act, [df], mode="tn", grid=(4, d // tn, s_len // tk),
        a_spec=pl.BlockSpec((tk, tf), lambda i, j, k: (k, i)),
        b_specs=[pl.BlockSpec((tk, tn), lambda i, j, k: (k, j))],
        out_shape=_sds((2, 4, tf // 2, d), F32),
        out_specs=pl.BlockSpec((2, None, tf // 2, tn), lambda i, j, k: (0, i, 0, j)),
        acc_shape=(tf, tn), epilogue=_ep_halves(tf // 2))
    after_dw_out(g_out)

    def ep_dhu(accs, ex, outs):
        da = accs[0]
        outs[0][0] = (da * ex[0][0].astype(F32)).astype(BF16)
        outs[0][1] = (da * ex[0][1].astype(F32)).astype(BF16)

    tm = _tile(s_len, 512)
    hu_spec = pl.BlockSpec((2, tm, 2 * tf), lambda j, i, k: (0, i, j))
    dhu = _matmul(
        f"{tag}_dhu", df, [w_out2d], mode="nt", grid=(2, s_len // tm, 1),
        a_spec=pl.BlockSpec((tm, d), lambda j, i, k: (i, 0)),
        b_specs=[pl.BlockSpec((2 * tf, d), lambda j, i, k: (j, 0), pipeline_mode=pl.Buffered(1))],
        extras=[hu], extra_specs=[hu_spec],
        out_shape=_sds((2, s_len, f_dim), BF16), out_specs=hu_spec, acc_shape=(tm, 2 * tf), epilogue=ep_dhu)

    hd = d // 2
    rt = hd // 2
    g_in = _matmul(
        f"{tag}_dw_in", n, [dhu], mode="tn", grid=(N_CHIPS, 4, s_len // tk),
        a_spec=pl.BlockSpec((tk, rt), lambda j, i, k: (k, i)),
        b_specs=[pl.BlockSpec((None, tk, 2 * tf), lambda j, i, k: (j // 2, k, j % 2))],
        out_shape=_sds((2, 4, hd, f_dim // 2), F32),
        out_specs=pl.BlockSpec((None, None, rt, 2 * tf), lambda j, i, k: (i // 2, j, i % 2, 0)),
        acc_shape=(rt, 2 * tf), epilogue=_ep_store(F32))
    after_dw_in(g_in)

    tm2 = _tile(s_len, 1024)
    dn = _matmul(
        f"{tag}_dn", dhu, [w_in_g], mode="nt", grid=(s_len // tm2, d // tn, N_CHIPS),
        a_spec=pl.BlockSpec((None, tm2, 2 * tf), lambda i, j, k: (k // 2, i, k % 2)),
        b_specs=[pl.BlockSpec((None, tn, 2 * tf), lambda i, j, k: (k, j, 0))],
        out_shape=_sds((s_len, d), BF16), out_specs=pl.BlockSpec((tm2, tn), lambda i, j, k: (i, j)),
        acc_shape=(tm2, tn), epilogue=_ep_store(BF16))
    return dn


def kernel(x, c, w_ada, b_ada, g_ffn1, w1_in, w1_out, g_mix, w_in, conv_w, conv_b, ln_a_g, ln_a_b, w_a_out, b_a_out, w_b_group, b_b_group, ls_b, w_out, g_ffn2, w2_in, w2_out, g_final, loss_target, m_w_ada, m_b_ada, m_g_ffn1, m_w1_in, m_w1_out, m_g_mix, m_w_in, m_conv_w, m_conv_b, m_ln_a_g, m_ln_a_b, m_w_a_out, m_b_a_out, m_w_b_group, m_b_b_group, m_ls_b, m_w_out, m_g_ffn2, m_w2_in, m_w2_out, m_g_final, v_w_ada, v_b_ada, v_g_ffn1, v_w1_in, v_w1_out, v_g_mix, v_w_in, v_conv_w, v_conv_b, v_ln_a_g, v_ln_a_b, v_w_a_out, v_b_a_out, v_w_b_group, v_b_b_group, v_ls_b, v_w_out, v_g_ffn2, v_w2_in, v_w2_out, v_g_final):
    weights = dict(w_ada=w_ada, b_ada=b_ada, g_ffn1=g_ffn1, w1_in=w1_in, w1_out=w1_out, g_mix=g_mix, w_in=w_in,
                   conv_w=conv_w, conv_b=conv_b, ln_a_g=ln_a_g, ln_a_b=ln_a_b, w_a_out=w_a_out, b_a_out=b_a_out,
                   w_b_group=w_b_group, b_b_group=b_b_group, ls_b=ls_b, w_out=w_out, g_ffn2=g_ffn2, w2_in=w2_in,
                   w2_out=w2_out, g_final=g_final)
    mom1 = dict(w_ada=m_w_ada, b_ada=m_b_ada, g_ffn1=m_g_ffn1, w1_in=m_w1_in, w1_out=m_w1_out, g_mix=m_g_mix,
                w_in=m_w_in, conv_w=m_conv_w, conv_b=m_conv_b, ln_a_g=m_ln_a_g, ln_a_b=m_ln_a_b, w_a_out=m_w_a_out,
                b_a_out=m_b_a_out, w_b_group=m_w_b_group, b_b_group=m_b_b_group, ls_b=m_ls_b, w_out=m_w_out,
                g_ffn2=m_g_ffn2, w2_in=m_w2_in, w2_out=m_w2_out, g_final=m_g_final)
    mom2 = dict(w_ada=v_w_ada, b_ada=v_b_ada, g_ffn1=v_g_ffn1, w1_in=v_w1_in, w1_out=v_w1_out, g_mix=v_g_mix,
                w_in=v_w_in, conv_w=v_conv_w, conv_b=v_conv_b, ln_a_g=v_ln_a_g, ln_a_b=v_ln_a_b, w_a_out=v_w_a_out,
                b_a_out=v_b_a_out, w_b_group=v_w_b_group, b_b_group=v_b_b_group, ls_b=v_ls_b, w_out=v_w_out,
                g_ffn2=v_g_ffn2, w2_in=v_w2_in, w2_out=v_w2_out, g_final=v_g_final)
    order = list(weights)

    s_len, d = x.shape[1], x.shape[2]
    f_dim = w1_out.shape[0] * N_CHIPS
    wc = conv_w.shape[1] * N_CHIPS
    wp = w_b_group.shape[0] * w_b_group.shape[1]
    n_groups, gi, goq = w_b_group.shape
    npc = w_in.shape[1]
    ada_c = w_ada.shape[1]
    dims = dict(S=s_len, D=d, F=f_dim)
    ts = _tile(s_len, 256)

    xi, yi, ci = lax.axis_index("x"), lax.axis_index("y"), lax.axis_index("c")
    q = 2 * xi + yi
    dev = 2 * q + ci
    q_idx = jnp.reshape(q, (1,)).astype(jnp.int32)
    qc_idx = jnp.stack([q, ci]).astype(jnp.int32)
    _PREVIOUS.clear()

    cwq = conv_w.shape[1]
    pack0 = jnp.concatenate([c.reshape(-1), conv_w.reshape(-1), b_b_group.reshape(-1)])
    n0 = -(-pack0.shape[0] // (8 * LANES)) * LANES
    pack0 = jnp.pad(pack0, (0, 8 * n0 - pack0.shape[0])).reshape(8, n0)
    g0 = _allgather_small("gather_small_in", pack0).reshape(N_DEV, 8 * n0)
    c_all = g0[:, :d]
    south = g0[0::2]
    cw_full = jnp.concatenate([south[k, d:d + CONV_K * cwq].reshape(CONV_K, cwq) for k in range(N_CHIPS)], axis=1)
    cw_pad = jnp.pad(cw_full, ((0, HALO - CONV_K), (0, 0)))
    o_bb = d + CONV_K * cwq
    bb_full = jnp.concatenate([south[k, o_bb:o_bb + n_groups * goq].reshape(n_groups, goq) for k in range(N_CHIPS)],
                              axis=1).reshape(1, d)

    as2d = lambda a: a.reshape(-1, a.shape[-1])
    groups = dict(w1_out=["w1_out"], mix=["w_a_out", "w_b_group", "w_out"], w2_in=["w2_in"], w2_out=["w2_out"])
    big = ["w1_in", "w1_out", "w_in", "w_a_out", "w_b_group", "w_out", "w2_in", "w2_out"]
    cast = lambda nm: _cast_into_gathered(f"cast_{nm}", as2d(weights[nm]), q_idx)
    w1_in_gather = _PartGather("w1_in", cast("w1_in"), 3)

    b_ada_mine = lax.dynamic_slice(b_ada, (q * ada_c,), (ada_c,)).reshape(1, ada_c)
    ada_piece = _ada_fwd("ada_fwd", c_all, w_ada, b_ada_mine)
    casts = {nm: cast(nm) for nm in big[1:]}
    g1 = _allgather_small("gather_ada", ada_piece).reshape(N_DEV, N_DEV, ada_c)
    w1_in_gather.start_rest()
    ici = {}
    for grp, names in groups.items():
        ici[grp] = _gather_ici(f"gather_{grp}_ici", [casts[nm] for nm in names])
        if grp == "w1_out":
            w_in_gather = _PartGather("w_in", casts["w_in"], 2)
            w_in_gather.start_rest()
    ada_rows = lax.dynamic_index_in_dim(g1[0::2], dev, axis=1, keepdims=False)
    ada = ada_rows.reshape(3, 3, 1, d)
    (sh1, sc1, gt1), (sh2, sc2, gt2), (sh3, sc3, gt3) = [[ada[i, j] for j in range(3)] for i in range(3)]

    row = lambda vct: vct.reshape(1, -1)
    g1v, gmv, g2v, gfv = row(g_ffn1), row(g_mix), row(g_ffn2), row(g_final)

    def arrived(grp):
        return _gather_d2d(f"gather_{grp}_d2d", ici[grp].wait())

    def gathered(fwd, grp):
        return {nm: g.reshape(N_CHIPS, 2 * g.shape[2], g.shape[3]) for nm, g in zip(groups[grp], fwd.wait())}

    x2 = x[0]
    tgt = loss_target[0]

    n1 = _norm_mod("ffn1_norm", x2, g1v, sc1, sh1, ts)
    fwd, w1_in_parts = {}, []

    def w1_in_part(part):
        def get():
            w1_in_gather.arrive(part)
            w1_in_parts.append(w1_in_gather.ready(part))
            return w1_in_parts[-1]
        return get

    def w1_out_after_swiglu():
        fwd["w1_out"] = arrived("w1_out")
        w_in_gather.arrive(0)
        return gathered(fwd["w1_out"], "w1_out")["w1_out"].reshape(f_dim, d)

    hu1, act1, f1, w1_out_2d = _ffn_fwd(
        "ffn1", n1, [(w1_in_part(part), cols) for part, cols in enumerate(w1_in_gather.parts)], w1_out_after_swiglu,
        dims)
    w1_in_g = w1_in_parts[-1]
    h1, n2 = _residual_norm_mod("mix_norm", x2, f1, gt1, 0.5, gmv, sc2, sh2, ts)

    tm = _tile(s_len, 1024)
    proj = ()
    for part in range(2):
        if part:
            w_in_gather.arrive(part)
        w_in_g = w_in_gather.ready(part)
        lo, width = w_in_gather.parts[part]
        el = pl.Element
        proj = (_matmul(
            f"mix_proj{part}", n2, [w_in_g.reshape(N_CHIPS * d, npc)], mode="nn", grid=(s_len // tm, N_CHIPS, 1),
            a_spec=pl.BlockSpec((tm, d), lambda i, j, k: (i, 0)),
            b_specs=[pl.BlockSpec((el(d), el(width)), lambda i, j, k, lo=lo: (_mult(j * d, d), lo))],
            out_shape=_sds((N_CHIPS * s_len, npc), BF16),
            out_specs=pl.BlockSpec((el(tm), el(width)), lambda i, j, k, lo=lo: (_mult(j * s_len + i * tm, tm), lo)),
            acc_shape=(tm, width), epilogue=_ep_store(BF16), carry=proj),)
    proj = proj[0].reshape(N_CHIPS, s_len, npc)
    fwd["mix"] = arrived("mix")
    cbv, lgv, lbv = row(conv_b), row(ln_a_g), row(ln_a_b)
    a3, mixed, conv_out = _mixer_mid("mix_mid", proj, cw_pad, cbv, lgv, lbv, wc, wp, ts)
    wts = gathered(fwd["mix"], "mix")
    w_out_2d = wts["w_out"].reshape(d, d)
    w_a_g = wts["w_a_out"]
    w_b_r = _regroup("regroup_w_b", wts["w_b_group"], n_groups)
    dq = d // N_CHIPS
    ya = _matmul(
        "mix_ya", a3, [w_a_g], mode="nn", grid=(s_len // tm, N_CHIPS, 1),
        a_spec=pl.BlockSpec((tm, wc), lambda i, j, k: (i, 0)),
        b_specs=[pl.BlockSpec((None, wc, dq), lambda i, j, k: (j, 0, 0))],
        out_shape=_sds((s_len, d), BF16), out_specs=pl.BlockSpec((tm, dq), lambda i, j, k: (i, j)),
        acc_shape=(tm, dq), epilogue=_ep_store(BF16))
    yb = _matmul(
        "mix_yb", mixed, [w_b_r], mode="nn", grid=(s_len // tm, n_groups, 1),
        a_spec=pl.BlockSpec((tm, gi), lambda i, j, k: (i, j)),
        b_specs=[pl.BlockSpec((None, gi, dq), lambda i, j, k: (j, 0, 0))],
        out_shape=_sds((s_len, d), BF16), out_specs=pl.BlockSpec((tm, dq), lambda i, j, k: (i, j)),
        acc_shape=(tm, dq), epilogue=_ep_store(BF16))
    bav, lsv = row(b_a_out), row(ls_b)
    z = _gates_fwd("mix_gates", proj, ya, yb, bav, bb_full, lsv, wc, wp, ts)
    tn = _tile(d, 1024)
    mix = _matmul(
        "mix_out", z, [w_out_2d], mode="nn", grid=(s_len // tm, d // tn, 1),
        a_spec=pl.BlockSpec((tm, d), lambda i, j, k: (i, 0)),
        b_specs=[pl.BlockSpec((d, tn), lambda i, j, k: (0, j))],
        out_shape=_sds((s_len, d), F32), out_specs=pl.BlockSpec((tm, tn), lambda i, j, k: (i, j)),
        acc_shape=(tm, tn), epilogue=_ep_store(F32))
    fwd["w2_in"] = arrived("w2_in")
    h2, n3 = _residual_norm_mod("ffn2_norm", h1, mix, gt2, 1.0, g2v, sc3, sh3, ts)
    w2_in_g = gathered(fwd["w2_in"], "w2_in")["w2_in"]
    hu2, act2, f3, w2_out_2d = _ffn_fwd(
        "ffn2", n3, [(lambda: w2_in_g, None)],
        lambda: gathered(arrived("w2_out"), "w2_out")["w2_out"].reshape(f_dim, d), dims)

    dh3, df3, d_gf, d_gt3, loss_cols = _final_loss("final_loss", h2, f3, tgt, gt3, 0.5, gfv, ts)
    rs, held = {}, {}
    dn3 = _ffn_bwd(
        "ffn2", n3, hu2, act2, df3, w2_in_g, w2_out_2d, dims,
        after_dw_out=lambda g: held.update(w2_out=g),
        after_dw_in=lambda g: rs.update(ffn2=_ReduceScatter("g_ffn2", ["w2_out", "w2_in"], [held["w2_out"], g],
                                                            qc_idx)))
    dh2, dmix, d_sh3, d_sc3, d_g2, d_gt2 = _norm_mod_bwd("ffn2_norm_bwd", h2, dn3, dh3, g2v, sc3, ts,
                                                         prev=(mix, gt2, 1.0))
    rs["ffn2"].step2()

    tk = s_len
    hq = d // (2 * N_CHIPS)
    gw_out = _matmul(
        "mix_dw_out", z, [dmix], mode="tn", grid=(N_CHIPS, d // tn, s_len // tk),
        a_spec=pl.BlockSpec((tk, 2 * hq), lambda i, j, k: (k, i)),
        b_specs=[pl.BlockSpec((tk, tn), lambda i, j, k: (k, j))],
        out_shape=_sds((2, N_CHIPS, hq, d), F32),
        out_specs=pl.BlockSpec((2, None, hq, tn), lambda i, j, k: (0, i, 0, j)),
        acc_shape=(2 * hq, tn), epilogue=_ep_halves(hq))
    dz = _matmul(
        "mix_dz", dmix, [w_out_2d], mode="nt", grid=(s_len // tm, d // tn, 1),
        a_spec=pl.BlockSpec((tm, d), lambda i, j, k: (i, 0)),
        b_specs=[pl.BlockSpec((tn, d), lambda i, j, k: (j, 0))],
        out_shape=_sds((s_len, d), BF16), out_specs=pl.BlockSpec((tm, tn), lambda i, j, k: (i, j)),
        acc_shape=(tm, tn), epilogue=_ep_store(BF16))
    dya, dyb, dgates, d_ba, d_ls, d_bb = _gates_bwd("mix_gates_bwd", proj, dz, ya, yb, bav, bb_full, lsv, wc, wp, ts)
    gw_a = _matmul(
        "mix_dw_a", a3, [dya], mode="tn", grid=(1, N_CHIPS, s_len // tk),
        a_spec=pl.BlockSpec((tk, wc), lambda i, j, k: (k, 0)),
        b_specs=[pl.BlockSpec((tk, dq), lambda i, j, k: (k, j))],
        out_shape=_sds((2, N_CHIPS, wc // 2, dq), F32),
        out_specs=pl.BlockSpec((2, None, wc // 2, dq), lambda i, j, k: (0, j, 0, 0)),
        acc_shape=(wc, dq), epilogue=_ep_halves(wc // 2))
    da3 = _matmul(
        "mix_da3", dya, [w_a_g], mode="nt", grid=(s_len // tm, 1, N_CHIPS),
        a_spec=pl.BlockSpec((tm, dq), lambda i, j, k: (i, k)),
        b_specs=[pl.BlockSpec((None, wc, dq), lambda i, j, k: (k, 0, 0))],
        out_shape=_sds((s_len, wc), BF16), out_specs=pl.BlockSpec((tm, wc), lambda i, j, k: (i, 0)),
        acc_shape=(tm, wc), epilogue=_ep_store(BF16))
    gpr = n_groups // 2

    def ep_by_chip(accs, ex, outs):
        for k in range(N_CHIPS):
            outs[0][k] = accs[0][:, k * goq:(k + 1) * goq]

    gw_b = _matmul(
        "mix_dw_b", mixed, [dyb], mode="tn", grid=(1, n_groups, s_len // tk),
        a_spec=pl.BlockSpec((tk, gi), lambda i, j, k: (k, j)),
        b_specs=[pl.BlockSpec((tk, dq), lambda i, j, k: (k, j))],
        out_shape=_sds((2, N_CHIPS, gpr * gi, goq), F32),
        out_specs=pl.BlockSpec((None, N_CHIPS, gi, goq), lambda i, j, k: (j // gpr, 0, j % gpr, 0)),
        acc_shape=(gi, dq), epilogue=ep_by_chip)
    dmixed = _matmul(
        "mix_dmixed", dyb, [w_b_r], mode="nt", grid=(s_len // tm, n_groups, 1),
        a_spec=pl.BlockSpec((tm, dq), lambda i, j, k: (i, j)),
        b_specs=[pl.BlockSpec((None, gi, dq), lambda i, j, k: (j, 0, 0))],
        out_shape=_sds((s_len, wp), BF16), out_specs=pl.BlockSpec((tm, gi), lambda i, j, k: (i, j)),
        acc_shape=(tm, gi), epilogue=_ep_store(BF16))
    da1, d_lg, d_lb, d_cb, d_cw = _conv_branch_bwd("mix_conv_bwd", proj, conv_out, da3, lgv, lbv, wc, wp, ts)
    dproj = _mixer_in_bwd("mix_in_bwd", proj, da1, dmixed, dgates, cw_pad, wc, wp, ts)
    hd = d // 2
    rt = hd // 2
    gw_in = _matmul(
        "mix_dw_in", n2, [dproj], mode="tn", grid=(N_CHIPS, 4, 1),
        a_spec=pl.BlockSpec((s_len, rt), lambda j, i, k: (0, i)),
        b_specs=[pl.BlockSpec((None, s_len, npc), lambda j, i, k: (j, 0, 0))],
        out_shape=_sds((2, N_CHIPS, hd, npc), F32),
        out_specs=pl.BlockSpec((None, None, rt, npc), lambda j, i, k: (i // 2, j, i % 2, 0)),
        acc_shape=(rt, npc), epilogue=_ep_store(F32))
    rs["mix"] = _ReduceScatter("g_mix", ["w_in", "w_a_out", "w_b_group", "w_out"], [gw_in, gw_a, gw_b, gw_out],
                               qc_idx)
    rs["ffn2"].step3()
    dn2 = _matmul(
        "mix_dn", dproj, [w_in_g], mode="nt", grid=(s_len // tm, d // tn, N_CHIPS),
        a_spec=pl.BlockSpec((None, tm, npc), lambda i, j, k: (k, i, 0)),
        b_specs=[pl.BlockSpec((None, tn, npc), lambda i, j, k: (k, j, 0))],
        out_shape=_sds((s_len, d), BF16), out_specs=pl.BlockSpec((tm, tn), lambda i, j, k: (i, j)),
        acc_shape=(tm, tn), epilogue=_ep_store(BF16))
    dh1, df1, d_sh2, d_sc2, d_gm, d_gt1 = _norm_mod_bwd("mix_norm_bwd", h1, dn2, dh2, gmv, sc2, ts,
                                                        prev=(f1, gt1, 0.5))
    rs["mix"].step2()

    def w1_in_ready(g):
        rs["w1_in"] = _ReduceScatter("g_w1_in", ["w1_in"], [g], qc_idx)
        rs["w1_out"].step2()
        rs["mix"].step3()

    dn1 = _ffn_bwd(
        "ffn1", n1, hu1, act1, df1, w1_in_g, w1_out_2d, dims,
        after_dw_out=lambda g: rs.update(w1_out=_ReduceScatter("g_w1_out", ["w1_out"], [g], qc_idx)),
        after_dw_in=w1_in_ready)
    grad_x, d_sh1, d_sc1, d_g1 = _norm_mod_bwd("ffn1_norm_bwd", x2, dn1, dh1, g1v, sc1, ts)

    d_ada = jnp.concatenate([d_sh1, d_sc1, d_gt1, d_sh2, d_sc2, d_gt2, d_sh3, d_sc3, d_gt3], axis=1)
    small = [d_ada, d_g1, d_gm, d_cw[:CONV_K].reshape(1, -1), d_cb, d_lg, d_lb, d_ba, d_bb, d_ls, d_g2, d_gf,
             loss_cols]
    sizes = [a.shape[1] for a in small]
    pack1 = jnp.concatenate(small, axis=1).reshape(-1)
    n1p = -(-pack1.shape[0] // (8 * LANES)) * LANES
    pack1 = jnp.pad(pack1, (0, 8 * n1p - pack1.shape[0])).reshape(8, n1p)
    g2 = _allgather_small("gather_small_grads", pack1)
    rs["w1_in"].step2()
    total = _sum_devices("sum_small_grads", g2, 8).reshape(-1)
    offs = [0]
    for sz in sizes:
        offs.append(offs[-1] + sz)
    tot = [total[offs[k]:offs[k + 1]] for k in range(len(sizes))]
    d_ada_all = g2.reshape(N_DEV, 8 * n1p)[:, :sizes[0]]
    loss = jnp.sum(tot[12])

    grads = {}
    grads["b_ada"] = tot[0]
    grads["g_ffn1"], grads["g_mix"] = tot[1], tot[2]
    grads["conv_w"] = lax.dynamic_slice(tot[3].reshape(CONV_K, wc), (0, q * cwq), (CONV_K, cwq))
    grads["conv_b"], grads["ln_a_g"], grads["ln_a_b"], grads["b_a_out"] = tot[4], tot[5], tot[6], tot[7]
    grads["b_b_group"] = lax.dynamic_slice(tot[8].reshape(n_groups, N_CHIPS * goq), (0, q * goq), (n_groups, goq))
    grads["ls_b"], grads["g_ffn2"], grads["g_final"] = tot[9], tot[10], tot[11]

    delta, new_m, new_v = {}, {}, {}

    def adamw_group(reduced):
        for nm, g in reduced.items():
            shp = weights[nm].shape
            go, dl, mo, vo = _adamw(f"adamw_{nm}", as2d(weights[nm]), g, as2d(mom1[nm]), as2d(mom2[nm]))
            grads[nm], delta[nm], new_m[nm], new_v[nm] = go.reshape(shp), dl.reshape(shp), mo.reshape(shp), vo.reshape(shp)

    adamw_group(rs["ffn2"].result())
    rs["w1_out"].step3()
    adamw_group(rs["mix"].result())
    d_ada_mine = lax.dynamic_slice(d_ada_all, (0, q * ada_c), (N_DEV, ada_c))
    grads["w_ada"], delta["w_ada"], new_m["w_ada"], new_v["w_ada"] = _ada_grad_adamw(
        "adamw_w_ada", c_all.T, d_ada_mine, w_ada, m_w_ada, v_w_ada)
    rs["w1_in"].step3()
    smalls = [nm for nm in order if nm not in big and nm != "w_ada"]
    flat = lambda src: jnp.concatenate([src[nm].reshape(-1) for nm in smalls])
    n_small = sum(weights[nm].size for nm in smalls)
    rows_s = -(-n_small // (8 * LANES)) * 8
    packed = [jnp.pad(flat(src), (0, rows_s * LANES - n_small)).reshape(rows_s, LANES)
              for src in (weights, grads, mom1, mom2)]
    _, dl_s, mo_s, vo_s = _adamw("adamw_small", *packed)
    off = 0
    for nm in smalls:
        sz, shp = weights[nm].size, weights[nm].shape
        delta[nm] = dl_s.reshape(-1)[off:off + sz].reshape(shp)
        new_m[nm] = mo_s.reshape(-1)[off:off + sz].reshape(shp)
        new_v[nm] = vo_s.reshape(-1)[off:off + sz].reshape(shp)
        grads[nm] = grads[nm].reshape(shp)
        off += sz
    adamw_group(rs["w1_out"].result())
    adamw_group(rs["w1_in"].result())

    return (loss, grad_x[None], *[grads[nm] for nm in order], *[delta[nm] for nm in order],
            *[new_m[nm] for nm in order], *[new_v[nm] for nm in order])
```

```python
import jax
import jax.numpy as jnp
from jax import lax
from jax.experimental import pallas as pl
from jax.experimental.pallas import tpu as pltpu

F32 = jnp.float32
BF16 = jnp.bfloat16
MESH = pl.DeviceIdType.MESH
ANY = pl.BlockSpec(memory_space=pl.ANY)
HBM = pl.BlockSpec(memory_space=pltpu.HBM)
SEM = pl.BlockSpec(memory_space=pltpu.SEMAPHORE)
EFFECT = pltpu.SideEffectType.DATAFLOW_SIDE_EFFECTING

EPS = 1e-6
CONV_K = 31
HALO = 32
POOL_WINDOWS = (2, 4, 8, 16)
N_CHIPS = 4
N_DEV = 8
LANES = 128

ADAM_LR = 0.001
ADAM_B1 = 0.9
ADAM_B2 = 0.999
ADAM_EPS = 1e-08
ADAM_WD = 0.01
ADAM_STEP = 10

DN = {
    "nn": (((1,), (0,)), ((), ())),
    "nt": (((1,), (1,)), ((), ())),
    "tn": (((0,), (0,)), ((), ())),
}


_PREVIOUS = []


def _ordered(call, args, n_lead, body, token=None, sources=()):
    dep = [pltpu.with_memory_space_constraint(p, pltpu.HBM) if p.size * p.dtype.itemsize >= (1 << 20) else p
           for p in _PREVIOUS if all(p is not a for a in (*args, *sources))]

    def wrapped(*refs):
        return body(*refs[:n_lead], *refs[n_lead + len(dep):])

    outs = call(wrapped, [ANY] * len(dep))(*args, *dep)
    seq = outs if isinstance(outs, (list, tuple)) else [outs]
    _PREVIOUS[:] = [seq[token] if token is not None else
                    next(o for o in seq if jnp.issubdtype(o.dtype, jnp.floating))]
    return outs


def _pcall(body, *, name, out_shape, grid=None, in_specs=None, out_specs=None, scratch=(), aliases=None,
           prefetch=0, vmem_mb=None):
    params = {}
    if grid is not None:
        params["dimension_semantics"] = ("arbitrary",) * len(grid)
    if vmem_mb is not None:
        params["vmem_limit_bytes"] = vmem_mb << 20
    def in_hbm(shape, spec):
        big = shape.size * jnp.dtype(shape.dtype).itemsize >= (1 << 20)
        return pltpu.HBM(shape.shape, shape.dtype) if big and getattr(spec, "memory_space", None) != pltpu.VMEM else shape

    if isinstance(out_shape, (list, tuple)):
        out_shape = [in_hbm(s, sp) for s, sp in zip(out_shape, out_specs)]
    else:
        out_shape = in_hbm(out_shape, out_specs)
    kw = dict(name=name, out_shape=out_shape, compiler_params=pltpu.CompilerParams(**params))
    if aliases:
        kw["input_output_aliases"] = aliases

    def call(wrapped, dep_specs):
        specs = list(in_specs) + dep_specs
        if prefetch:
            return pl.pallas_call(wrapped, grid_spec=pltpu.PrefetchScalarGridSpec(
                num_scalar_prefetch=prefetch, grid=grid, in_specs=specs, out_specs=out_specs,
                scratch_shapes=list(scratch)), **kw)
        if grid is not None:
            return pl.pallas_call(wrapped, grid=grid, in_specs=specs, out_specs=out_specs,
                                  scratch_shapes=list(scratch), **kw)
        return pl.pallas_call(wrapped, in_specs=specs, out_specs=out_specs, scratch_shapes=list(scratch), **kw)

    def run(*args):
        specs = [None] * prefetch + list(in_specs)
        placed = [pltpu.with_memory_space_constraint(a, pltpu.HBM)
                  if a.size * a.dtype.itemsize >= (1 << 20) and getattr(s, "memory_space", None) != pltpu.VMEM else a
                  for a, s in zip(args, specs)]
        return _ordered(call, placed, prefetch + len(in_specs), body, sources=args)

    return run


def _mult(offset, unit):
    return pl.multiple_of(offset, unit)


def _tile(dim, pref):
    t = min(dim, pref)
    assert dim % t == 0, (dim, pref)
    return t


def _sds(shape, dtype):
    return jax.ShapeDtypeStruct(tuple(shape), dtype)


def _sigmoid(v):
    return 0.5 * jnp.tanh(0.5 * v) + 0.5


def _vec(w):
    return pl.BlockSpec((1, w), lambda *_: (0, 0))


def _acc_rows(ref, val, i):
    @pl.when(i == 0)
    def _():
        ref[...] = jnp.zeros_like(ref)

    ref[...] += jnp.sum(val, axis=0, keepdims=True)


def _matmul(name, a, bs, *, mode, grid, a_spec, b_specs, out_shape, out_specs, acc_shape, epilogue,
            extras=(), extra_specs=(), vmem_mb=56, carry=(), col_block=None):
    nb, ne, nk, nc = len(bs), len(extras), grid[2], len(carry)
    dn = DN[mode]

    def body(*all_refs):
        refs = all_refs[:1 + nb + ne] + all_refs[1 + nb + ne + nc:]
        a_ref, b_refs, ex = refs[0], refs[1:1 + nb], refs[1 + nb:1 + nb + ne]
        if col_block:
            outs, av, width = refs[1 + nb + ne:], a_ref[...], b_refs[0].shape[-1]
            for lo in range(0, width, col_block):
                cs = slice(lo, min(lo + col_block, width))
                epilogue([lax.dot_general(av, b[:, cs], dn, preferred_element_type=F32) for b in b_refs], ex, outs, cs)
            return
        if nk == 1:
            outs = refs[1 + nb + ne:]
            accs = [lax.dot_general(a_ref[...], b[...], dn, preferred_element_type=F32) for b in b_refs]
            epilogue(accs, ex, outs)
            return
        outs, acc_refs = refs[1 + nb + ne:-nb], refs[-nb:]
        k = pl.program_id(2)

        @pl.when(k == 0)
        def _():
            for acc in acc_refs:
                acc[...] = jnp.zeros_like(acc)

        for acc, b in zip(acc_refs, b_refs):
            acc[...] += lax.dot_general(a_ref[...], b[...], dn, preferred_element_type=F32)

        @pl.when(k == nk - 1)
        def _():
            epilogue([acc[...] for acc in acc_refs], ex, outs)

    scratch = [pltpu.VMEM(acc_shape, F32) for _ in range(nb)] if nk > 1 else []
    return _pcall(body, name=name, out_shape=out_shape, grid=grid,
                  in_specs=[a_spec, *b_specs, *extra_specs, *[ANY] * nc], out_specs=out_specs, scratch=scratch,
                  aliases={1 + nb + ne + i: i for i in range(nc)}, vmem_mb=vmem_mb)(a, *bs, *extras, *carry)


def _ep_store(dtype):
    def ep(accs, ex, outs):
        outs[0][...] = accs[0].astype(dtype)
    return ep


def _ep_halves(h):
    def ep(accs, ex, outs):
        outs[0][0] = accs[0][:h]
        outs[0][1] = accs[0][h:]
    return ep


def _place():
    x, y, c = lax.axis_index("x"), lax.axis_index("y"), lax.axis_index("c")
    chips = [(1 - x, y), (x, 1 - y), (1 - x, 1 - y)]
    return x, y, c, chips


def _allgather_small(name, block):
    m_per, n = block.shape

    def body(x_ref, out_ref, send_sems, recv_sems, local_sem):
        x, y, c, chips = _place()
        me, sibling = (x, y, c), (x, y, 1 - c)

        def rows(px, py, pc):
            return out_ref.at[pl.ds((4 * px + 2 * py + pc) * m_per, m_per), :]

        def copy(k, blk, to, src=None):
            return pltpu.make_async_remote_copy(
                src_ref=rows(*blk) if src is None else src, dst_ref=rows(*blk),
                send_sem=send_sems.at[k], recv_sem=recv_sems.at[k], device_id=to, device_id_type=MESH)

        mine = pltpu.make_async_copy(x_ref, rows(*me), local_sem)
        mine.start()
        first = [copy(0, me, sibling, src=x_ref)]
        first += [copy(1 + j, me, (*chip, c), src=x_ref) for j, chip in enumerate(chips)]
        for cp in first:
            cp.start()
        passed = [copy(4 + j, (*chip, c), sibling) for j, chip in enumerate(chips)]
        for j, chip in enumerate(chips):
            copy(1 + j, (*chip, c), me).wait_recv()
            passed[j].start()
        copy(0, sibling, me).wait_recv()
        for j, chip in enumerate(chips):
            copy(4 + j, (*chip, 1 - c), me).wait_recv()
        for cp in first + passed:
            cp.wait_send()
        mine.wait()

    return _pcall(
        body, name=name, out_shape=_sds((N_DEV * m_per, n), block.dtype),
        in_specs=[pl.BlockSpec(memory_space=pltpu.VMEM)], out_specs=pl.BlockSpec(memory_space=pltpu.VMEM),
        scratch=[pltpu.SemaphoreType.DMA((7,)), pltpu.SemaphoreType.DMA((7,)), pltpu.SemaphoreType.DMA],
    )(block)


class _SplitCopies:
    def __init__(self, name, arrays, plan, n_copies):
        self.name, self.plan, self.n = name, plan, len(arrays)
        n = self.n

        def body(*refs):
            send, recv, token = refs[n], refs[n + 1], refs[-1]
            for k, (src, dst, _, peer) in enumerate(plan(refs[:n])):
                pltpu.make_async_remote_copy(src_ref=src, dst_ref=dst, send_sem=send.at[k], recv_sem=recv.at[k],
                                             device_id=peer, device_id_type=MESH).start()
            token[...] = jnp.zeros_like(token)

        def call(wrapped, dep_specs):
            return pl.pallas_call(
                wrapped, name=f"{name}_start",
                out_shape=(pltpu.SemaphoreType.DMA((n_copies,)), pltpu.SemaphoreType.DMA((n_copies,)),
                           *[pltpu.HBM(a.shape, a.dtype) for a in arrays], _sds((8, LANES), F32)),
                in_specs=[HBM] * n + dep_specs,
                out_specs=(SEM, SEM, *[HBM] * n, pl.BlockSpec(memory_space=pltpu.VMEM)),
                input_output_aliases={i: 2 + i for i in range(n)},
                compiler_params=pltpu.CompilerParams(has_side_effects=EFFECT))

        outs = _ordered(call, [pltpu.with_memory_space_constraint(a, pltpu.HBM) for a in arrays], n, body, token=-1,
                        sources=arrays)
        self.send, self.recv, self.arrays = outs[0], outs[1], list(outs[2:2 + n])

    def wait(self, arrays=None):
        n, plan = self.n, self.plan
        if arrays is not None:
            self.arrays = list(arrays)

        def body(*refs):
            send, recv, token = refs[n], refs[n + 1], refs[-1]
            for k, (src, _, landing, peer) in enumerate(plan(refs[:n])):
                cp = pltpu.make_async_remote_copy(src_ref=src, dst_ref=landing, send_sem=send.at[k],
                                                  recv_sem=recv.at[k], device_id=peer, device_id_type=MESH)
                cp.wait_send()
                cp.wait_recv()
            token[...] = jnp.zeros_like(token)

        def call(wrapped, dep_specs):
            return pl.pallas_call(
                wrapped, name=f"{self.name}_wait",
                out_shape=(*[pltpu.HBM(a.shape, a.dtype) for a in self.arrays], _sds((8, LANES), F32)),
                in_specs=[HBM] * n + [SEM, SEM] + dep_specs,
                out_specs=(*[HBM] * n, pl.BlockSpec(memory_space=pltpu.VMEM)),
                input_output_aliases={i: i for i in range(n)},
                compiler_params=pltpu.CompilerParams(has_side_effects=EFFECT))

        return list(_ordered(call, [*self.arrays, self.send, self.recv], n + 2, body, token=-1))[:n]


def _col_range(g, cols):
    lo, width = cols if cols is not None else (0, g.shape[-1])
    return (slice(None), pl.ds(lo, width))


def _gather_ici(name, gathered, cols=None):
    def plan(refs):
        x, y, c, chips = _place()
        q = 2 * x + y
        return [(g.at[(q, c, *_col_range(g, cols))], g.at[(q, c, *_col_range(g, cols))],
                 g.at[(2 * px + py, c, *_col_range(g, cols))], (px, py, c))
                for g in refs for px, py in chips]

    return _SplitCopies(name, gathered, plan, 3 * len(gathered))


def _gather_d2d(name, gathered, cols=None):
    def plan(refs):
        x, y, c, chips = _place()
        return [(g.at[(2 * px + py, c, *_col_range(g, cols))], g.at[(2 * px + py, c, *_col_range(g, cols))],
                 g.at[(2 * px + py, 1 - c, *_col_range(g, cols))], (x, y, 1 - c))
                for g in refs for px, py in chips]

    return _SplitCopies(name, gathered, plan, 3 * len(gathered))


MXU_COLS = 256


def _two_parts(width):
    passes = width // MXU_COLS
    first = (passes // 2) * MXU_COLS if width % MXU_COLS == 0 and passes >= 2 else width // 2
    return [(0, first), (first, width - first)]


class _TwoPartGather:
    def __init__(self, name, gathered):
        self.name, self.d2d = name, {}
        self.parts = _two_parts(gathered.shape[-1])
        self.ici = [_gather_ici(f"gather_{name}_a_ici", [gathered], self.parts[0])]
        self.buf = self.ici[0].arrays

    def start_second(self):
        self.ici.append(_gather_ici(f"gather_{self.name}_b_ici", self.buf, self.parts[1]))
        self.buf = self.ici[1].arrays

    def arrive(self, part):
        here = self.ici[part].wait(self.buf)
        self.d2d[part] = _gather_d2d(f"gather_{self.name}_{'ab'[part]}_d2d", here, self.parts[part])
        self.buf = self.d2d[part].arrays

    def ready(self, part):
        self.buf = self.d2d[part].wait(self.buf)
        g = self.buf[0]
        return g.reshape(N_CHIPS, 2 * g.shape[2], g.shape[3])


def _scatter_sibling(name, grads):
    n = len(grads)

    def plan(refs):
        x, y, c, _ = _place()
        return [(refs[w].at[1 - c], refs[n + w], refs[n + w], (x, y, 1 - c)) for w in range(n)]

    landing = [lax.empty(g.shape[1:], g.dtype) for g in grads]
    return _SplitCopies(name, [*grads, *landing], plan, n)


def _scatter_chips(name, sums):
    n = len(sums)

    def plan(refs):
        x, y, c, chips = _place()
        return [(refs[w].at[2 * px + py], refs[n + w].at[j], refs[n + w].at[j], (px, py, c))
                for w in range(n) for j, (px, py) in enumerate(chips)]

    landing = [lax.empty((3, *s.shape[1:]), s.dtype) for s in sums]
    return _SplitCopies(name, [*sums, *landing], plan, 3 * n)


def _share_final(name, finals):
    def plan(refs):
        x, y, c, _ = _place()
        return [(f.at[c], f.at[c], f.at[1 - c], (x, y, 1 - c)) for f in refs]

    return _SplitCopies(name, finals, plan, len(finals))


def _row_tile(rows, cols, budget_elems=786432):
    best = 8
    for t in range(8, rows + 1, 8):
        if rows % t == 0 and t * cols <= budget_elems:
            best = t
    return best if rows % best == 0 else rows


def _sum_with_sibling(name, grad, recv, qc_idx):
    _, _, h, cols = grad.shape
    tr = _row_tile(h, cols)

    def body(s_ref, g_ref, r_ref, own_ref, pb_ref):
        p = g_ref[...] + r_ref[...]
        pb_ref[...] = p.astype(BF16)

        @pl.when(pl.program_id(1) == s_ref[0])
        def _():
            own_ref[...] = p

    blk = pl.BlockSpec((None, tr, cols), lambda r, k, s: (k, r, 0))
    return _pcall(
        body, name=name, out_shape=[_sds((h, cols), F32), _sds((N_CHIPS, h, cols), BF16)],
        grid=(h // tr, N_CHIPS), prefetch=1,
        in_specs=[pl.BlockSpec((None, None, tr, cols), lambda r, k, s: (s[1], k, r, 0)), blk],
        out_specs=[pl.BlockSpec((tr, cols), lambda r, k, s: (r, 0)), blk], vmem_mb=32,
    )(qc_idx, grad, recv)


def _sum_chips(name, own, recv, qc_idx):
    h, cols = own.shape
    tr = _row_tile(h, cols)

    def body(s_ref, p_ref, t_ref, o_ref):
        o_ref[...] = ((p_ref[...] + t_ref[0].astype(F32)) + t_ref[1].astype(F32)) + t_ref[2].astype(F32)

    return _pcall(
        body, name=name, out_shape=_sds((2, h, cols), F32), grid=(h // tr,), prefetch=1,
        in_specs=[pl.BlockSpec((tr, cols), lambda r, s: (r, 0)),
                  pl.BlockSpec((3, tr, cols), lambda r, s: (0, r, 0))],
        out_specs=pl.BlockSpec((None, tr, cols), lambda r, s: (s[1], r, 0)), vmem_mb=32,
    )(qc_idx, own, recv)


class _ReduceScatter:
    def __init__(self, tag, names, grads, qc_idx):
        self.tag, self.names, self.n, self.qc_idx = tag, names, len(grads), qc_idx
        self.copies = _scatter_sibling(f"{tag}_rs_sibling", grads)

    def step2(self):
        n = self.n
        arrs = self.copies.wait()
        sums = [_sum_with_sibling(f"{nm}_sum_sibling", arrs[w], arrs[n + w], self.qc_idx)
                for w, nm in enumerate(self.names)]
        self.own = [s[0] for s in sums]
        self.copies = _scatter_chips(f"{self.tag}_rs_chips", [s[1] for s in sums])

    def step3(self):
        n = self.n
        arrs = self.copies.wait()
        finals = [_sum_chips(f"{nm}_sum_chips", self.own[w], arrs[n + w], self.qc_idx)
                  for w, nm in enumerate(self.names)]
        self.copies = _share_final(f"{self.tag}_rs_final", finals)

    def result(self):
        return {nm: f.reshape(2 * f.shape[1], f.shape[2]) for nm, f in zip(self.names, self.copies.wait())}


def _cast_into_gathered(name, w, q_idx):
    rows, cols = w.shape
    h = rows // 2
    tr = _row_tile(h, cols, 1 << 20)
    nr = h // tr

    def body(s_ref, w_ref, o_ref):
        o_ref[...] = w_ref[...].astype(BF16)

    return _pcall(body, name=name, out_shape=_sds((N_CHIPS, 2, h, cols), BF16), grid=(2, nr), prefetch=1,
                  in_specs=[pl.BlockSpec((tr, cols), lambda hf, r, s: (hf * nr + r, 0))],
                  out_specs=pl.BlockSpec((None, None, tr, cols), lambda hf, r, s: (s[0], hf, r, 0)),
                  vmem_mb=32)(q_idx, w)


def _regroup(name, w, n_groups):
    n_chips, rows, goq = w.shape
    gi = rows // n_groups

    def body(w_ref, o_ref):
        o_ref[...] = w_ref[...]

    return _pcall(body, name=name, out_shape=_sds((n_groups, gi, n_chips * goq), w.dtype), grid=(n_groups, n_chips),
                  in_specs=[pl.BlockSpec((None, gi, goq), lambda g, k: (k, g, 0))],
                  out_specs=pl.BlockSpec((None, gi, goq), lambda g, k: (g, 0, k)), vmem_mb=32)(w)


def _rms(h):
    r = lax.rsqrt(jnp.mean(h * h, axis=-1, keepdims=True) + EPS)
    return r, h * r


def _norm_mod(name, h, g, sc, sh, ts):
    s_len, d = h.shape

    def body(h_ref, g_ref, sc_ref, sh_ref, n_ref):
        _, xhat = _rms(h_ref[...])
        n_ref[...] = ((xhat * g_ref[...]) * (1.0 + sc_ref[...]) + sh_ref[...]).astype(BF16)

    row = pl.BlockSpec((ts, d), lambda i: (i, 0))
    return _pcall(body, name=name, out_shape=_sds((s_len, d), BF16), grid=(s_len // ts,),
                  in_specs=[row, _vec(d), _vec(d), _vec(d)], out_specs=row, vmem_mb=32)(h, g, sc, sh)


def _residual_norm_mod(name, h, f, gate, cmul, g, sc, sh, ts):
    s_len, d = h.shape

    def body(h_ref, f_ref, gt_ref, g_ref, sc_ref, sh_ref, ho_ref, n_ref):
        hn = h_ref[...] + (cmul * gt_ref[...]) * f_ref[...]
        ho_ref[...] = hn
        _, xhat = _rms(hn)
        n_ref[...] = ((xhat * g_ref[...]) * (1.0 + sc_ref[...]) + sh_ref[...]).astype(BF16)

    row = pl.BlockSpec((ts, d), lambda i: (i, 0))
    return _pcall(body, name=name, out_shape=[_sds((s_len, d), F32), _sds((s_len, d), BF16)],
                  grid=(s_len // ts,), in_specs=[row, row, _vec(d), _vec(d), _vec(d), _vec(d)],
                  out_specs=[row, row], vmem_mb=32)(h, f, gate, g, sc, sh)


def _final_loss(name, h, f, tgt, gate, cmul, g, ts):
    s_len, d = h.shape

    def body(h_ref, f_ref, t_ref, gt_ref, g_ref, dh_ref, df_ref, dg_ref, dgt_ref, loss_ref):
        i = pl.program_id(0)
        fv = f_ref[...]
        coef = cmul * gt_ref[...]
        hn = h_ref[...] + coef * fv
        r, xhat = _rms(hn)
        err = xhat * g_ref[...] - t_ref[...]
        _acc_rows(loss_ref, (0.5 / d) * (err * err), i)
        dy = err * (1.0 / d)
        _acc_rows(dg_ref, dy * xhat, i)
        dxhat = dy * g_ref[...]
        dh = r * (dxhat - xhat * jnp.mean(dxhat * xhat, axis=-1, keepdims=True))
        dh_ref[...] = dh
        _acc_rows(dgt_ref, cmul * (dh * fv), i)
        df_ref[...] = (coef * dh).astype(BF16)

    row = pl.BlockSpec((ts, d), lambda i: (i, 0))
    return _pcall(body, name=name,
                  out_shape=[_sds((s_len, d), F32), _sds((s_len, d), BF16)] + [_sds((1, d), F32)] * 3,
                  grid=(s_len // ts,), in_specs=[row, row, row, _vec(d), _vec(d)],
                  out_specs=[row, row, _vec(d), _vec(d), _vec(d)], vmem_mb=40)(h, f, tgt, gate, g)


def _norm_mod_bwd(name, h, dn, dh_next, g, sc, ts, prev=None):
    s_len, d = h.shape
    has_prev = prev is not None
    cmul = prev[2] if has_prev else None

    def body(*refs):
        if has_prev:
            h_ref, dn_ref, dhn_ref, f_ref, g_ref, sc_ref, gt_ref, dh_ref, df_ref, dsh_ref, dsc_ref, dg_ref, dgt_ref = refs
        else:
            h_ref, dn_ref, dhn_ref, g_ref, sc_ref, dh_ref, dsh_ref, dsc_ref, dg_ref = refs
        i = pl.program_id(0)
        r, xhat = _rms(h_ref[...])
        dn_v = dn_ref[...].astype(F32)
        gv = g_ref[...]
        _acc_rows(dsh_ref, dn_v, i)
        _acc_rows(dsc_ref, dn_v * (xhat * gv), i)
        dnrm = dn_v * (1.0 + sc_ref[...])
        _acc_rows(dg_ref, dnrm * xhat, i)
        dxhat = dnrm * gv
        dh = dhn_ref[...] + r * (dxhat - xhat * jnp.mean(dxhat * xhat, axis=-1, keepdims=True))
        dh_ref[...] = dh
        if has_prev:
            _acc_rows(dgt_ref, cmul * (dh * f_ref[...]), i)
            df_ref[...] = ((cmul * gt_ref[...]) * dh).astype(BF16)

    row = pl.BlockSpec((ts, d), lambda i: (i, 0))
    if has_prev:
        ins, in_specs = [h, dn, dh_next, prev[0], g, sc, prev[1]], [row, row, row, row, _vec(d), _vec(d), _vec(d)]
        out_shape = [_sds((s_len, d), F32), _sds((s_len, d), BF16)] + [_sds((1, d), F32)] * 4
        out_specs = [row, row] + [_vec(d)] * 4
    else:
        ins, in_specs = [h, dn, dh_next, g, sc], [row, row, row, _vec(d), _vec(d)]
        out_shape = [_sds((s_len, d), F32)] + [_sds((1, d), F32)] * 3
        out_specs = [row] + [_vec(d)] * 3
    return _pcall(body, name=name, out_shape=out_shape, grid=(s_len // ts,), in_specs=in_specs,
                  out_specs=out_specs, vmem_mb=40)(*ins)


def _cols(ref, lo, hi, npc, rows=slice(None)):
    parts = []
    while lo < hi:
        q, o = divmod(lo, npc)
        n = min(hi - lo, npc - o)
        parts.append(ref[q, rows, o:o + n].astype(F32))
        lo += n
    return parts[0] if len(parts) == 1 else jnp.concatenate(parts, axis=-1)


def _store_cols(ref, lo, val, npc, rows=slice(None)):
    off, width = 0, val.shape[-1]
    while off < width:
        q, o = divmod(lo + off, npc)
        n = min(width - off, npc - o)
        ref[q, rows, o:o + n] = val[:, off:off + n]
        off += n


def _chips_covering(cols, npc):
    return -(-cols // npc)


SUBLANES = 8
ROW_CHUNK = 32


def _make_phases(src_ref, ph_ref):
    rows = src_ref.shape[0] - SUBLANES
    for b in range(1, SUBLANES):
        ph_ref[b - 1] = src_ref[pl.ds(b, rows), :]


def _window(src_ref, ph_ref, off, r0, cols=slice(None)):
    a, b = divmod(off, SUBLANES)
    start = pl.multiple_of(r0 + SUBLANES * a, SUBLANES)
    if b == 0:
        return src_ref[pl.ds(start, ROW_CHUNK), cols]
    return ph_ref[b - 1, pl.ds(start, ROW_CHUNK), cols]


def _phase_scratch(rows, width):
    return pltpu.VMEM((SUBLANES - 1, rows - SUBLANES, width), F32)


def _conv(a0s_ref, a0p_ref, cw_ref, cb_ref, r0):
    a1 = cb_ref[...] + cw_ref[0:1, :] * _window(a0s_ref, a0p_ref, HALO - CONV_K + 1, r0)
    for k in range(1, CONV_K):
        a1 = a1 + cw_ref[k:k + 1, :] * _window(a0s_ref, a0p_ref, HALO - CONV_K + 1 + k, r0)
    return a1


def _layer_norm(a1, lg_ref, lb_ref):
    mu = jnp.mean(a1, axis=-1, keepdims=True)
    ctr = a1 - mu
    rstd = lax.rsqrt(jnp.mean(ctr * ctr, axis=-1, keepdims=True) + EPS)
    xh = ctr * rstd
    return xh, rstd, xh * lg_ref[...] + lb_ref[...]


def _for_chunks(ts, fn):
    def step(ci, carry):
        fn(pl.multiple_of(ci * ROW_CHUNK, ROW_CHUNK))
        return carry

    lax.fori_loop(0, ts // ROW_CHUNK, step, 0)


def _stage_glu(p_ref, ph_ref, a0s_ref, i, wc, npc, ts):
    a0 = _cols(p_ref, 0, wc, npc) * _sigmoid(_cols(p_ref, wc, 2 * wc, npc))
    a0h = _cols(ph_ref, 0, wc, npc) * _sigmoid(_cols(ph_ref, wc, 2 * wc, npc))
    a0s_ref[0:HALO, :] = jnp.where(i > 0, a0h, 0.0)
    a0s_ref[HALO:HALO + ts, :] = a0


def _mixer_mid(name, proj, cw, cb, lg, lb, wc, wp, ts):
    _, s_len, npc = proj.shape
    nq = _chips_covering(2 * wc + wp, npc)
    gi = wp // len(POOL_WINDOWS)
    hb = ts // HALO

    def body(p_ref, ph_ref, cw_ref, cb_ref, lg_ref, lb_ref, a3_ref, mx_ref, a1_ref, a0s_ref, vs_ref, a0p_ref,
             vp_ref):
        i = pl.program_id(0)
        _stage_glu(p_ref, ph_ref, a0s_ref, i, wc, npc, ts)
        vs_ref[0:HALO, :] = jnp.where(i > 0, _cols(ph_ref, 2 * wc, 2 * wc + wp, npc), 0.0)
        vs_ref[HALO:HALO + ts, :] = _cols(p_ref, 2 * wc, 2 * wc + wp, npc)
        _make_phases(a0s_ref, a0p_ref)
        _make_phases(vs_ref, vp_ref)

        def chunk(r0):
            rows = pl.ds(r0, ROW_CHUNK)
            a1 = _conv(a0s_ref, a0p_ref, cw_ref, cb_ref, r0)
            a1_ref[rows, :] = a1
            _, _, a2 = _layer_norm(a1, lg_ref, lb_ref)
            a3_ref[rows, :] = (a2 * _sigmoid(a2)).astype(BF16)
            t_abs = i * ts + r0 + lax.broadcasted_iota(jnp.int32, (ROW_CHUNK, 1), 0)
            for g, win in enumerate(POOL_WINDOWS):
                cs = slice(g * gi, (g + 1) * gi)
                v_now = _window(vs_ref, vp_ref, HALO, r0, cs)
                acc = v_now
                for dlt in range(1, win):
                    acc = acc + _window(vs_ref, vp_ref, HALO - dlt, r0, cs)
                cnt = jnp.minimum(t_abs + 1, win).astype(F32)
                mx_ref[rows, cs] = (acc / cnt - v_now).astype(BF16)

        _for_chunks(ts, chunk)

    return _pcall(
        body, name=name, out_shape=[_sds((s_len, wc), BF16), _sds((s_len, wp), BF16), _sds((s_len, wc), F32)],
        grid=(s_len // ts,),
        in_specs=[pl.BlockSpec((nq, ts, npc), lambda i: (0, i, 0)),
                  pl.BlockSpec((nq, HALO, npc), lambda i: (0, jnp.maximum(i * hb - 1, 0), 0)),
                  pl.BlockSpec((HALO, wc), lambda i: (0, 0)), _vec(wc), _vec(wc), _vec(wc)],
        out_specs=[pl.BlockSpec((ts, wc), lambda i: (i, 0)), pl.BlockSpec((ts, wp), lambda i: (i, 0)),
                   pl.BlockSpec((ts, wc), lambda i: (i, 0))],
        scratch=[pltpu.VMEM((HALO + ts, wc), F32), pltpu.VMEM((HALO + ts, wp), F32),
                 _phase_scratch(HALO + ts, wc), _phase_scratch(HALO + ts, wp)], vmem_mb=56,
    )(proj, proj, cw, cb, lg, lb)


def _gates_fwd(name, proj, ya, yb, b_a, b_b, ls, wc, wp, ts):
    _, s_len, npc = proj.shape
    d = ya.shape[1]
    g0 = 2 * wc + wp

    def body(p_ref, ya_ref, yb_ref, ba_ref, bb_ref, ls_ref, z_ref):
        ga = _sigmoid(_cols(p_ref, g0, g0 + d, npc))
        gb = _sigmoid(_cols(p_ref, g0 + d, g0 + 2 * d, npc))
        z = ga * (ya_ref[...] + ba_ref[...]) + gb * ((yb_ref[...] + bb_ref[...]) * ls_ref[...])
        z_ref[...] = z.astype(BF16)

    row = pl.BlockSpec((ts, d), lambda i: (i, 0))
    return _pcall(body, name=name, out_shape=_sds((s_len, d), BF16), grid=(s_len // ts,),
                  in_specs=[pl.BlockSpec((N_CHIPS, ts, npc), lambda i: (0, i, 0)), row, row, _vec(d), _vec(d), _vec(d)],
                  out_specs=row, vmem_mb=48)(proj, ya, yb, b_a, b_b, ls)


def _gates_bwd(name, proj, dz, ya, yb, b_a, b_b, ls, wc, wp, ts):
    _, s_len, npc = proj.shape
    d = ya.shape[1]
    g0 = 2 * wc + wp

    def body(p_ref, dz_ref, ya_ref, yb_ref, ba_ref, bb_ref, ls_ref, dya_ref, dyb_ref, dgt_ref, dba_ref, dls_ref,
             dbb_ref):
        i = pl.program_id(0)
        ga = _sigmoid(_cols(p_ref, g0, g0 + d, npc))
        gb = _sigmoid(_cols(p_ref, g0 + d, g0 + 2 * d, npc))
        dz_v = dz_ref[...].astype(F32)
        y_a = ya_ref[...] + ba_ref[...]
        y_b0 = yb_ref[...] + bb_ref[...]
        ls_v = ls_ref[...]
        dya = dz_v * ga
        dya_ref[...] = dya.astype(BF16)
        _acc_rows(dba_ref, dya, i)
        t = dz_v * gb
        _acc_rows(dls_ref, t * y_b0, i)
        dyb = t * ls_v
        dyb_ref[...] = dyb.astype(BF16)
        _acc_rows(dbb_ref, dyb, i)
        dgt_ref[:, 0:d] = (dz_v * y_a * ga * (1.0 - ga)).astype(BF16)
        dgt_ref[:, d:2 * d] = (dz_v * (y_b0 * ls_v) * gb * (1.0 - gb)).astype(BF16)

    row = pl.BlockSpec((ts, d), lambda i: (i, 0))
    return _pcall(
        body, name=name,
        out_shape=[_sds((s_len, d), BF16), _sds((s_len, d), BF16), _sds((s_len, 2 * d), BF16)] + [_sds((1, d), F32)] * 3,
        grid=(s_len // ts,),
        in_specs=[pl.BlockSpec((N_CHIPS, ts, npc), lambda i: (0, i, 0)), row, row, row, _vec(d), _vec(d), _vec(d)],
        out_specs=[row, row, pl.BlockSpec((ts, 2 * d), lambda i: (i, 0))] + [_vec(d)] * 3, vmem_mb=48,
    )(proj, dz, ya, yb, b_a, b_b, ls)


def _conv_branch_bwd(name, proj, a1, da3, lg, lb, wc, wp, ts):
    _, s_len, npc = proj.shape
    nq = _chips_covering(2 * wc, npc)
    hb = ts // HALO

    n_tiles = s_len // ts

    def fold(v):
        return jnp.sum(v.reshape(ROW_CHUNK // SUBLANES, SUBLANES, v.shape[-1]), axis=0)

    def body(p_ref, ph_ref, a1_ref, da3_ref, lg_ref, lb_ref, da1_ref, dlg_ref, dlb_ref, dcb_ref, dcw_ref,
             a0s_ref, a0p_ref, vec8_ref, dcw8_ref):
        i = pl.program_id(0)
        _stage_glu(p_ref, ph_ref, a0s_ref, i, wc, npc, ts)
        _make_phases(a0s_ref, a0p_ref)

        @pl.when(i == 0)
        def _():
            vec8_ref[...] = jnp.zeros_like(vec8_ref)
            dcw8_ref[...] = jnp.zeros_like(dcw8_ref)

        def chunk(r0):
            rows = pl.ds(r0, ROW_CHUNK)
            xh, rstd, a2 = _layer_norm(a1_ref[rows, :], lg_ref, lb_ref)
            sig = _sigmoid(a2)
            da2 = da3_ref[rows, :].astype(F32) * (sig * (1.0 + a2 * (1.0 - sig)))
            vec8_ref[0] += fold(da2 * xh)
            vec8_ref[1] += fold(da2)
            dxh = da2 * lg_ref[...]
            da1 = rstd * (dxh - jnp.mean(dxh, axis=-1, keepdims=True)
                          - xh * jnp.mean(dxh * xh, axis=-1, keepdims=True))
            da1_ref[rows, :] = da1
            vec8_ref[2] += fold(da1)
            for k in range(CONV_K):
                dcw8_ref[k] += fold(da1 * _window(a0s_ref, a0p_ref, HALO - CONV_K + 1 + k, r0))

        _for_chunks(ts, chunk)

        @pl.when(i == n_tiles - 1)
        def _():
            dlg_ref[...] = jnp.sum(vec8_ref[0], axis=0, keepdims=True)
            dlb_ref[...] = jnp.sum(vec8_ref[1], axis=0, keepdims=True)
            dcb_ref[...] = jnp.sum(vec8_ref[2], axis=0, keepdims=True)
            dcw_ref[...] = jnp.sum(dcw8_ref[...], axis=1)

    return _pcall(
        body, name=name,
        out_shape=[_sds((s_len, wc), F32)] + [_sds((1, wc), F32)] * 3 + [_sds((HALO, wc), F32)],
        grid=(s_len // ts,),
        in_specs=[pl.BlockSpec((nq, ts, npc), lambda i: (0, i, 0)),
                  pl.BlockSpec((nq, HALO, npc), lambda i: (0, jnp.maximum(i * hb - 1, 0), 0)),
                  pl.BlockSpec((ts, wc), lambda i: (i, 0)), pl.BlockSpec((ts, wc), lambda i: (i, 0)),
                  _vec(wc), _vec(wc)],
        out_specs=[pl.BlockSpec((ts, wc), lambda i: (i, 0)), _vec(wc), _vec(wc), _vec(wc),
                   pl.BlockSpec((HALO, wc), lambda i: (0, 0))],
        scratch=[pltpu.VMEM((HALO + ts, wc), F32), _phase_scratch(HALO + ts, wc),
                 pltpu.VMEM((3, SUBLANES, wc), F32), pltpu.VMEM((HALO, SUBLANES, wc), F32)], vmem_mb=56,
    )(proj, proj, a1, da3, lg, lb)


def _mixer_in_bwd(name, proj, da1, dmixed, dgates, cw, wc, wp, ts):
    _, s_len, npc = proj.shape
    nq = _chips_covering(2 * wc, npc)
    gi = wp // len(POOL_WINDOWS)
    hb = ts // HALO
    n_tiles = s_len // ts
    last_hb = s_len // HALO - 1
    d2 = dgates.shape[1]

    def body(p_ref, d1_ref, d1n_ref, dm_ref, dmn_ref, dgt_ref, cw_ref, o_ref, d1s_ref, es_ref, d1p_ref, ep_ref):
        i = pl.program_id(0)
        more = i < n_tiles - 1
        d1s_ref[0:ts, :] = d1_ref[...]
        d1s_ref[ts:ts + HALO, :] = jnp.where(more, d1n_ref[...], 0.0)
        t_abs = i * ts + lax.broadcasted_iota(jnp.int32, (ts + HALO, 1), 0)
        dm_ext = jnp.concatenate([dm_ref[...].astype(F32), jnp.where(more, dmn_ref[...].astype(F32), 0.0)], axis=0)
        for g, win in enumerate(POOL_WINDOWS):
            cs = slice(g * gi, (g + 1) * gi)
            es_ref[:, cs] = dm_ext[:, cs] / jnp.minimum(t_abs + 1, win).astype(F32)
        _make_phases(d1s_ref, d1p_ref)
        _make_phases(es_ref, ep_ref)

        def chunk(r0):
            rows = pl.ds(r0, ROW_CHUNK)
            da0 = cw_ref[0:1, :] * _window(d1s_ref, d1p_ref, CONV_K - 1, r0)
            for k in range(1, CONV_K):
                da0 = da0 + cw_ref[k:k + 1, :] * _window(d1s_ref, d1p_ref, CONV_K - 1 - k, r0)
            glu_a = _cols(p_ref, 0, wc, npc, rows)
            sig = _sigmoid(_cols(p_ref, wc, 2 * wc, npc, rows))
            _store_cols(o_ref, 0, (da0 * sig).astype(BF16), npc, rows)
            _store_cols(o_ref, wc, (da0 * glu_a * sig * (1.0 - sig)).astype(BF16), npc, rows)
            parts = []
            for g, win in enumerate(POOL_WINDOWS):
                cs = slice(g * gi, (g + 1) * gi)
                acc = _window(es_ref, ep_ref, 0, r0, cs)
                for dlt in range(1, win):
                    acc = acc + _window(es_ref, ep_ref, dlt, r0, cs)
                parts.append(acc - dm_ref[rows, cs].astype(F32))
            _store_cols(o_ref, 2 * wc, jnp.concatenate(parts, axis=-1).astype(BF16), npc, rows)

        _for_chunks(ts, chunk)
        _store_cols(o_ref, 2 * wc + wp, dgt_ref[...], npc)

    nxt = lambda i: (jnp.minimum((i + 1) * hb, last_hb), 0)
    return _pcall(
        body, name=name, out_shape=_sds((N_CHIPS, s_len, npc), BF16), grid=(n_tiles,),
        in_specs=[pl.BlockSpec((nq, ts, npc), lambda i: (0, i, 0)),
                  pl.BlockSpec((ts, wc), lambda i: (i, 0)), pl.BlockSpec((HALO, wc), nxt),
                  pl.BlockSpec((ts, wp), lambda i: (i, 0)), pl.BlockSpec((HALO, wp), nxt),
                  pl.BlockSpec((ts, d2), lambda i: (i, 0)),
                  pl.BlockSpec((HALO, wc), lambda i: (0, 0))],
        out_specs=pl.BlockSpec((N_CHIPS, ts, npc), lambda i: (0, i, 0)),
        scratch=[pltpu.VMEM((ts + HALO, wc), F32), pltpu.VMEM((ts + HALO, wp), F32),
                 _phase_scratch(ts + HALO, wc), _phase_scratch(ts + HALO, wp)], vmem_mb=56,
    )(proj, da1, da1, dmixed, dmixed, dgates, cw)


def _ada_fwd(name, c_all, w, b):
    d, cols = w.shape
    tn = 512 if cols % 512 == 0 else cols

    def body(c_ref, w_ref, b_ref, o_ref):
        cv = c_ref[...]
        sc = (cv * _sigmoid(cv)).astype(BF16)
        o_ref[...] = jnp.dot(sc, w_ref[...].astype(BF16), preferred_element_type=F32) + b_ref[...]

    return _pcall(body, name=name, out_shape=_sds((N_DEV, cols), F32), grid=(cols // tn,),
                  in_specs=[pl.BlockSpec((N_DEV, d), lambda j: (0, 0)), pl.BlockSpec((d, tn), lambda j: (0, j)),
                            pl.BlockSpec((1, tn), lambda j: (0, j))],
                  out_specs=pl.BlockSpec((N_DEV, tn), lambda j: (0, j)), vmem_mb=32)(c_all, w, b)


def _adam_math(w, g, m, v):
    m_new = ADAM_B1 * m + (1.0 - ADAM_B1) * g
    v_new = ADAM_B2 * v + (1.0 - ADAM_B2) * (g * g)
    m_hat = m_new / (1.0 - ADAM_B1 ** ADAM_STEP)
    v_hat = v_new / (1.0 - ADAM_B2 ** ADAM_STEP)
    delta = -ADAM_LR * (m_hat / (jnp.sqrt(v_hat) + ADAM_EPS) + ADAM_WD * w)
    return delta, m_new, v_new


def _adamw(name, w, g, m, v):
    rows, cols = w.shape
    tr = _row_tile(rows, cols, 524288)

    def body(w_ref, g_ref, m_ref, v_ref, go_ref, d_ref, mo_ref, vo_ref):
        g = g_ref[...]
        go_ref[...] = g
        d_ref[...], mo_ref[...], vo_ref[...] = _adam_math(w_ref[...], g, m_ref[...], v_ref[...])

    spec = pl.BlockSpec((tr, cols), lambda i: (i, 0))
    return _pcall(body, name=name, out_shape=[_sds(w.shape, F32)] * 4, grid=(rows // tr,), in_specs=[spec] * 4,
                  out_specs=[spec] * 4, vmem_mb=40)(w, g, m, v)


def _ada_grad_adamw(name, c_t, d_ada, w, m, v):
    rows, cols = w.shape
    tr = _tile(rows, 256)
    tc = _tile(cols, 1536) if cols % 1536 == 0 else cols

    def body(c_ref, da_ref, w_ref, m_ref, v_ref, g_ref, d_ref, mo_ref, vo_ref):
        cv = c_ref[...]
        sc = cv * _sigmoid(cv)
        g = sc[:, 0:1] * da_ref[0:1, :]
        for b in range(1, N_DEV):
            g = g + sc[:, b:b + 1] * da_ref[b:b + 1, :]
        g_ref[...] = g
        d_ref[...], mo_ref[...], vo_ref[...] = _adam_math(w_ref[...], g, m_ref[...], v_ref[...])

    spec = pl.BlockSpec((tr, tc), lambda i, j: (i, j))
    return _pcall(body, name=name, out_shape=[_sds(w.shape, F32)] * 4, grid=(rows // tr, cols // tc),
                  in_specs=[pl.BlockSpec((tr, N_DEV), lambda i, j: (i, 0)),
                            pl.BlockSpec((N_DEV, tc), lambda i, j: (0, j)), spec, spec, spec],
                  out_specs=[spec] * 4, vmem_mb=40)(c_t, d_ada, w, m, v)


def _sum_devices(name, gathered, m_per):
    n = gathered.shape[1]

    def body(g_ref, o_ref):
        acc = g_ref[0:m_per, :]
        for dev in range(1, N_DEV):
            acc = acc + g_ref[dev * m_per:(dev + 1) * m_per, :]
        o_ref[...] = acc

    return _pcall(body, name=name, out_shape=_sds((m_per, n), F32),
                  in_specs=[pl.BlockSpec(memory_space=pltpu.VMEM)],
                  out_specs=pl.BlockSpec(memory_space=pltpu.VMEM))(gathered)


def _ffn_fwd(tag, n, w_in_parts, w_out_after_swiglu, dims):
    s_len, d, f_dim = dims["S"], dims["D"], dims["F"]
    tf = f_dim // 4
    tm0, tm = _tile(s_len, 512), _tile(s_len, 1024)
    p = f_dim // 2

    def ep(accs, ex, outs, cs=slice(None)):
        hh, uu = accs
        sig = _sigmoid(hh)
        silu = hh * sig
        outs[0][0, :, cs] = (uu * (sig + silu * (1.0 - sig))).astype(BF16)
        outs[0][1, :, cs] = silu.astype(BF16)
        outs[1][:, cs] = (silu * uu).astype(BF16)

    done = ()
    for part, (get_w, cols) in enumerate(w_in_parts):
        w_g = get_w().reshape(N_CHIPS * d, p)
        lo, width = cols if cols is not None else (0, p)
        mode_kw = dict(pipeline_mode=pl.Buffered(1)) if cols is None else {}
        el = pl.Element
        done = _matmul(
            f"{tag}_swiglu{part}", n, [w_g, w_g], mode="nn", grid=(2, s_len // tm0, 1),
            a_spec=pl.BlockSpec((tm0, d), lambda j, i, k: (i, 0)),
            b_specs=[pl.BlockSpec((el(d), el(width)), lambda j, i, k, lo=lo: (_mult(j * d, d), lo), **mode_kw),
                     pl.BlockSpec((el(d), el(width)), lambda j, i, k, lo=lo: (_mult((2 + j) * d, d), lo), **mode_kw)],
            out_shape=[_sds((2, s_len, f_dim), BF16), _sds((s_len, f_dim), BF16)],
            out_specs=[pl.BlockSpec((el(2), el(tm0), el(width)),
                                    lambda j, i, k, lo=lo: (0, _mult(i * tm0, tm0), _mult(j * p + lo, LANES))),
                       pl.BlockSpec((el(tm0), el(width)),
                                    lambda j, i, k, lo=lo: (_mult(i * tm0, tm0), _mult(j * p + lo, LANES)))],
            acc_shape=(tm0, width), epilogue=ep, carry=done, col_block=512 if cols is None else None)
    hu, act = done
    w_out2d = w_out_after_swiglu()
    tn2 = _tile(d, 1024)
    f = _matmul(
        f"{tag}_down", act, [w_out2d], mode="nn", grid=(s_len // tm, d // tn2, 2),
        a_spec=pl.BlockSpec((tm, 2 * tf), lambda i, j, k: (i, k)),
        b_specs=[pl.BlockSpec((2 * tf, tn2), lambda i, j, k: (k, j))],
        out_shape=_sds((s_len, d), F32), out_specs=pl.BlockSpec((tm, tn2), lambda i, j, k: (i, j)),
        acc_shape=(tm, tn2), epilogue=_ep_store(F32))
    return hu, act, f, w_out2d


def _ffn_bwd(tag, n, hu, act, df, w_in_g, w_out2d, dims, after_dw_out, after_dw_in):
    s_len, d, f_dim = dims["S"], dims["D"], dims["F"]
    tf = f_dim // 4
    tk = _tile(s_len, 2048)
    tn = _tile(d, 1024)
    g_out = _matmul(
        f"{tag}_dw_out", act, [df], mode="tn", grid=(4, d // tn, s_len // tk),
        a_spec=pl.BlockSpec((tk, tf), lambda i, j, k: (k, i)),
        b_specs=[pl.BlockSpec((tk, tn), lambda i, j, k: (k, j))],
        out_shape=_sds((2, 4, tf // 2, d), F32),
        out_specs=pl.BlockSpec((2, None, tf // 2, tn), lambda i, j, k: (0, i, 0, j)),
        acc_shape=(tf, tn), epilogue=_ep_halves(tf // 2))
    after_dw_out(g_out)

    def ep_dhu(accs, ex, outs):
        da = accs[0]
        outs[0][0] = (da * ex[0][0].astype(F32)).astype(BF16)
        outs[0][1] = (da * ex[0][1].astype(F32)).astype(BF16)

    tm = _tile(s_len, 512)
    hu_spec = pl.BlockSpec((2, tm, 2 * tf), lambda j, i, k: (0, i, j))
    dhu = _matmul(
        f"{tag}_dhu", df, [w_out2d], mode="nt", grid=(2, s_len // tm, 1),
        a_spec=pl.BlockSpec((tm, d), lambda j, i, k: (i, 0)),
        b_specs=[pl.BlockSpec((2 * tf, d), lambda j, i, k: (j, 0), pipeline_mode=pl.Buffered(1))],
        extras=[hu], extra_specs=[hu_spec],
        out_shape=_sds((2, s_len, f_dim), BF16), out_specs=hu_spec, acc_shape=(tm, 2 * tf), epilogue=ep_dhu)

    hd = d // 2
    rt = hd // 2
    g_in = _matmul(
        f"{tag}_dw_in", n, [dhu], mode="tn", grid=(N_CHIPS, 4, s_len // tk),
        a_spec=pl.BlockSpec((tk, rt), lambda j, i, k: (k, i)),
        b_specs=[pl.BlockSpec((None, tk, 2 * tf), lambda j, i, k: (j // 2, k, j % 2))],
        out_shape=_sds((2, 4, hd, f_dim // 2), F32),
        out_specs=pl.BlockSpec((None, None, rt, 2 * tf), lambda j, i, k: (i // 2, j, i % 2, 0)),
        acc_shape=(rt, 2 * tf), epilogue=_ep_store(F32))
    after_dw_in(g_in)

    tm2 = _tile(s_len, 1024)
    dn = _matmul(
        f"{tag}_dn", dhu, [w_in_g], mode="nt", grid=(s_len // tm2, d // tn, N_CHIPS),
        a_spec=pl.BlockSpec((None, tm2, 2 * tf), lambda i, j, k: (k // 2, i, k % 2)),
        b_specs=[pl.BlockSpec((None, tn, 2 * tf), lambda i, j, k: (k, j, 0))],
        out_shape=_sds((s_len, d), BF16), out_specs=pl.BlockSpec((tm2, tn), lambda i, j, k: (i, j)),
        acc_shape=(tm2, tn), epilogue=_ep_store(BF16))
    return dn


def kernel(x, c, w_ada, b_ada, g_ffn1, w1_in, w1_out, g_mix, w_in, conv_w, conv_b, ln_a_g, ln_a_b, w_a_out, b_a_out, w_b_group, b_b_group, ls_b, w_out, g_ffn2, w2_in, w2_out, g_final, loss_target, m_w_ada, m_b_ada, m_g_ffn1, m_w1_in, m_w1_out, m_g_mix, m_w_in, m_conv_w, m_conv_b, m_ln_a_g, m_ln_a_b, m_w_a_out, m_b_a_out, m_w_b_group, m_b_b_group, m_ls_b, m_w_out, m_g_ffn2, m_w2_in, m_w2_out, m_g_final, v_w_ada, v_b_ada, v_g_ffn1, v_w1_in, v_w1_out, v_g_mix, v_w_in, v_conv_w, v_conv_b, v_ln_a_g, v_ln_a_b, v_w_a_out, v_b_a_out, v_w_b_group, v_b_b_group, v_ls_b, v_w_out, v_g_ffn2, v_w2_in, v_w2_out, v_g_final):
    weights = dict(w_ada=w_ada, b_ada=b_ada, g_ffn1=g_ffn1, w1_in=w1_in, w1_out=w1_out, g_mix=g_mix, w_in=w_in,
                   conv_w=conv_w, conv_b=conv_b, ln_a_g=ln_a_g, ln_a_b=ln_a_b, w_a_out=w_a_out, b_a_out=b_a_out,
                   w_b_group=w_b_group, b_b_group=b_b_group, ls_b=ls_b, w_out=w_out, g_ffn2=g_ffn2, w2_in=w2_in,
                   w2_out=w2_out, g_final=g_final)
    mom1 = dict(w_ada=m_w_ada, b_ada=m_b_ada, g_ffn1=m_g_ffn1, w1_in=m_w1_in, w1_out=m_w1_out, g_mix=m_g_mix,
                w_in=m_w_in, conv_w=m_conv_w, conv_b=m_conv_b, ln_a_g=m_ln_a_g, ln_a_b=m_ln_a_b, w_a_out=m_w_a_out,
                b_a_out=m_b_a_out, w_b_group=m_w_b_group, b_b_group=m_b_b_group, ls_b=m_ls_b, w_out=m_w_out,
                g_ffn2=m_g_ffn2, w2_in=m_w2_in, w2_out=m_w2_out, g_final=m_g_final)
    mom2 = dict(w_ada=v_w_ada, b_ada=v_b_ada, g_ffn1=v_g_ffn1, w1_in=v_w1_in, w1_out=v_w1_out, g_mix=v_g_mix,
                w_in=v_w_in, conv_w=v_conv_w, conv_b=v_conv_b, ln_a_g=v_ln_a_g, ln_a_b=v_ln_a_b, w_a_out=v_w_a_out,
                b_a_out=v_b_a_out, w_b_group=v_w_b_group, b_b_group=v_b_b_group, ls_b=v_ls_b, w_out=v_w_out,
                g_ffn2=v_g_ffn2, w2_in=v_w2_in, w2_out=v_w2_out, g_final=v_g_final)
    order = list(weights)

    s_len, d = x.shape[1], x.shape[2]
    f_dim = w1_out.shape[0] * N_CHIPS
    wc = conv_w.shape[1] * N_CHIPS
    wp = w_b_group.shape[0] * w_b_group.shape[1]
    n_groups, gi, goq = w_b_group.shape
    npc = w_in.shape[1]
    ada_c = w_ada.shape[1]
    dims = dict(S=s_len, D=d, F=f_dim)
    ts = _tile(s_len, 256)

    xi, yi, ci = lax.axis_index("x"), lax.axis_index("y"), lax.axis_index("c")
    q = 2 * xi + yi
    dev = 2 * q + ci
    q_idx = jnp.reshape(q, (1,)).astype(jnp.int32)
    qc_idx = jnp.stack([q, ci]).astype(jnp.int32)
    _PREVIOUS.clear()

    cwq = conv_w.shape[1]
    pack0 = jnp.concatenate([c.reshape(-1), conv_w.reshape(-1), b_b_group.reshape(-1)])
    n0 = -(-pack0.shape[0] // (8 * LANES)) * LANES
    pack0 = jnp.pad(pack0, (0, 8 * n0 - pack0.shape[0])).reshape(8, n0)
    g0 = _allgather_small("gather_small_in", pack0).reshape(N_DEV, 8 * n0)
    c_all = g0[:, :d]
    south = g0[0::2]
    cw_full = jnp.concatenate([south[k, d:d + CONV_K * cwq].reshape(CONV_K, cwq) for k in range(N_CHIPS)], axis=1)
    cw_pad = jnp.pad(cw_full, ((0, HALO - CONV_K), (0, 0)))
    o_bb = d + CONV_K * cwq
    bb_full = jnp.concatenate([south[k, o_bb:o_bb + n_groups * goq].reshape(n_groups, goq) for k in range(N_CHIPS)],
                              axis=1).reshape(1, d)

    as2d = lambda a: a.reshape(-1, a.shape[-1])
    groups = dict(w1_out=["w1_out"], mix=["w_a_out", "w_b_group", "w_out"], w2_in=["w2_in"], w2_out=["w2_out"])
    big = ["w1_in", "w1_out", "w_in", "w_a_out", "w_b_group", "w_out", "w2_in", "w2_out"]
    cast = lambda nm: _cast_into_gathered(f"cast_{nm}", as2d(weights[nm]), q_idx)
    w1_in_gather = _TwoPartGather("w1_in", cast("w1_in"))

    b_ada_mine = lax.dynamic_slice(b_ada, (q * ada_c,), (ada_c,)).reshape(1, ada_c)
    ada_piece = _ada_fwd("ada_fwd", c_all, w_ada, b_ada_mine)
    casts = {nm: cast(nm) for nm in big[1:]}
    g1 = _allgather_small("gather_ada", ada_piece).reshape(N_DEV, N_DEV, ada_c)
    w1_in_gather.start_second()
    ici = {}
    for grp, names in groups.items():
        ici[grp] = _gather_ici(f"gather_{grp}_ici", [casts[nm] for nm in names])
        if grp == "w1_out":
            w_in_gather = _TwoPartGather("w_in", casts["w_in"])
            w_in_gather.start_second()
    ada_rows = lax.dynamic_index_in_dim(g1[0::2], dev, axis=1, keepdims=False)
    ada = ada_rows.reshape(3, 3, 1, d)
    (sh1, sc1, gt1), (sh2, sc2, gt2), (sh3, sc3, gt3) = [[ada[i, j] for j in range(3)] for i in range(3)]

    row = lambda vct: vct.reshape(1, -1)
    g1v, gmv, g2v, gfv = row(g_ffn1), row(g_mix), row(g_ffn2), row(g_final)

    def arrived(grp):
        return _gather_d2d(f"gather_{grp}_d2d", ici[grp].wait())

    def gathered(fwd, grp):
        return {nm: g.reshape(N_CHIPS, 2 * g.shape[2], g.shape[3]) for nm, g in zip(groups[grp], fwd.wait())}

    x2 = x[0]
    tgt = loss_target[0]

    n1 = _norm_mod("ffn1_norm", x2, g1v, sc1, sh1, ts)
    fwd, w1_in_parts = {}, []

    def w1_in_part(part):
        def get():
            w1_in_gather.arrive(part)
            w1_in_parts.append(w1_in_gather.ready(part))
            return w1_in_parts[-1]
        return get

    def w1_out_after_swiglu():
        fwd["w1_out"] = arrived("w1_out")
        return gathered(fwd["w1_out"], "w1_out")["w1_out"].reshape(f_dim, d)

    hu1, act1, f1, w1_out_2d = _ffn_fwd(
        "ffn1", n1, [(w1_in_part(part), w1_in_gather.parts[part]) for part in range(2)], w1_out_after_swiglu, dims)
    w1_in_g = w1_in_parts[-1]
    w_in_gather.arrive(0)
    h1, n2 = _residual_norm_mod("mix_norm", x2, f1, gt1, 0.5, gmv, sc2, sh2, ts)

    tm = _tile(s_len, 1024)
    tnp = npc // 2
    proj = ()
    for part in range(2):
        if part:
            w_in_gather.arrive(part)
        w_in_g = w_in_gather.ready(part)
        lo, width = w_in_gather.parts[part]
        el = pl.Element
        proj = (_matmul(
            f"mix_proj{part}", n2, [w_in_g.reshape(N_CHIPS * d, npc)], mode="nn", grid=(s_len // tm, N_CHIPS, 1),
            a_spec=pl.BlockSpec((tm, d), lambda i, j, k: (i, 0)),
            b_specs=[pl.BlockSpec((el(d), el(width)), lambda i, j, k, lo=lo: (_mult(j * d, d), lo))],
            out_shape=_sds((N_CHIPS * s_len, npc), BF16),
            out_specs=pl.BlockSpec((el(tm), el(width)), lambda i, j, k, lo=lo: (_mult(j * s_len + i * tm, tm), lo)),
            acc_shape=(tm, width), epilogue=_ep_store(BF16), carry=proj),)
    proj = proj[0].reshape(N_CHIPS, s_len, npc)
    fwd["mix"] = arrived("mix")
    cbv, lgv, lbv = row(conv_b), row(ln_a_g), row(ln_a_b)
    a3, mixed, conv_out = _mixer_mid("mix_mid", proj, cw_pad, cbv, lgv, lbv, wc, wp, ts)
    wts = gathered(fwd["mix"], "mix")
    w_out_2d = wts["w_out"].reshape(d, d)
    w_a_g = wts["w_a_out"]
    w_b_r = _regroup("regroup_w_b", wts["w_b_group"], n_groups)
    dq = d // N_CHIPS
    ya = _matmul(
        "mix_ya", a3, [w_a_g], mode="nn", grid=(s_len // tm, N_CHIPS, 1),
        a_spec=pl.BlockSpec((tm, wc), lambda i, j, k: (i, 0)),
        b_specs=[pl.BlockSpec((None, wc, dq), lambda i, j, k: (j, 0, 0))],
        out_shape=_sds((s_len, d), BF16), out_specs=pl.BlockSpec((tm, dq), lambda i, j, k: (i, j)),
        acc_shape=(tm, dq), epilogue=_ep_store(BF16))
    yb = _matmul(
        "mix_yb", mixed, [w_b_r], mode="nn", grid=(s_len // tm, n_groups, 1),
        a_spec=pl.BlockSpec((tm, gi), lambda i, j, k: (i, j)),
        b_specs=[pl.BlockSpec((None, gi, dq), lambda i, j, k: (j, 0, 0))],
        out_shape=_sds((s_len, d), BF16), out_specs=pl.BlockSpec((tm, dq), lambda i, j, k: (i, j)),
        acc_shape=(tm, dq), epilogue=_ep_store(BF16))
    bav, lsv = row(b_a_out), row(ls_b)
    z = _gates_fwd("mix_gates", proj, ya, yb, bav, bb_full, lsv, wc, wp, ts)
    tn = _tile(d, 1024)
    mix = _matmul(
        "mix_out", z, [w_out_2d], mode="nn", grid=(s_len // tm, d // tn, 1),
        a_spec=pl.BlockSpec((tm, d), lambda i, j, k: (i, 0)),
        b_specs=[pl.BlockSpec((d, tn), lambda i, j, k: (0, j))],
        out_shape=_sds((s_len, d), F32), out_specs=pl.BlockSpec((tm, tn), lambda i, j, k: (i, j)),
        acc_shape=(tm, tn), epilogue=_ep_store(F32))
    fwd["w2_in"] = arrived("w2_in")
    h2, n3 = _residual_norm_mod("ffn2_norm", h1, mix, gt2, 1.0, g2v, sc3, sh3, ts)
    w2_in_g = gathered(fwd["w2_in"], "w2_in")["w2_in"]
    hu2, act2, f3, w2_out_2d = _ffn_fwd(
        "ffn2", n3, [(lambda: w2_in_g, None)],
        lambda: gathered(arrived("w2_out"), "w2_out")["w2_out"].reshape(f_dim, d), dims)

    dh3, df3, d_gf, d_gt3, loss_cols = _final_loss("final_loss", h2, f3, tgt, gt3, 0.5, gfv, ts)
    rs, held = {}, {}
    dn3 = _ffn_bwd(
        "ffn2", n3, hu2, act2, df3, w2_in_g, w2_out_2d, dims,
        after_dw_out=lambda g: held.update(w2_out=g),
        after_dw_in=lambda g: rs.update(ffn2=_ReduceScatter("g_ffn2", ["w2_out", "w2_in"], [held["w2_out"], g],
                                                            qc_idx)))
    dh2, dmix, d_sh3, d_sc3, d_g2, d_gt2 = _norm_mod_bwd("ffn2_norm_bwd", h2, dn3, dh3, g2v, sc3, ts,
                                                         prev=(mix, gt2, 1.0))
    rs["ffn2"].step2()

    tk = s_len
    hq = d // (2 * N_CHIPS)
    gw_out = _matmul(
        "mix_dw_out", z, [dmix], mode="tn", grid=(N_CHIPS, d // tn, s_len // tk),
        a_spec=pl.BlockSpec((tk, 2 * hq), lambda i, j, k: (k, i)),
        b_specs=[pl.BlockSpec((tk, tn), lambda i, j, k: (k, j))],
        out_shape=_sds((2, N_CHIPS, hq, d), F32),
        out_specs=pl.BlockSpec((2, None, hq, tn), lambda i, j, k: (0, i, 0, j)),
        acc_shape=(2 * hq, tn), epilogue=_ep_halves(hq))
    dz = _matmul(
        "mix_dz", dmix, [w_out_2d], mode="nt", grid=(s_len // tm, d // tn, 1),
        a_spec=pl.BlockSpec((tm, d), lambda i, j, k: (i, 0)),
        b_specs=[pl.BlockSpec((tn, d), lambda i, j, k: (j, 0))],
        out_shape=_sds((s_len, d), BF16), out_specs=pl.BlockSpec((tm, tn), lambda i, j, k: (i, j)),
        acc_shape=(tm, tn), epilogue=_ep_store(BF16))
    dya, dyb, dgates, d_ba, d_ls, d_bb = _gates_bwd("mix_gates_bwd", proj, dz, ya, yb, bav, bb_full, lsv, wc, wp, ts)
    gw_a = _matmul(
        "mix_dw_a", a3, [dya], mode="tn", grid=(1, N_CHIPS, s_len // tk),
        a_spec=pl.BlockSpec((tk, wc), lambda i, j, k: (k, 0)),
        b_specs=[pl.BlockSpec((tk, dq), lambda i, j, k: (k, j))],
        out_shape=_sds((2, N_CHIPS, wc // 2, dq), F32),
        out_specs=pl.BlockSpec((2, None, wc // 2, dq), lambda i, j, k: (0, j, 0, 0)),
        acc_shape=(wc, dq), epilogue=_ep_halves(wc // 2))
    da3 = _matmul(
        "mix_da3", dya, [w_a_g], mode="nt", grid=(s_len // tm, 1, N_CHIPS),
        a_spec=pl.BlockSpec((tm, dq), lambda i, j, k: (i, k)),
        b_specs=[pl.BlockSpec((None, wc, dq), lambda i, j, k: (k, 0, 0))],
        out_shape=_sds((s_len, wc), BF16), out_specs=pl.BlockSpec((tm, wc), lambda i, j, k: (i, 0)),
        acc_shape=(tm, wc), epilogue=_ep_store(BF16))
    gpr = n_groups // 2

    def ep_by_chip(accs, ex, outs):
        for k in range(N_CHIPS):
            outs[0][k] = accs[0][:, k * goq:(k + 1) * goq]

    gw_b = _matmul(
        "mix_dw_b", mixed, [dyb], mode="tn", grid=(1, n_groups, s_len // tk),
        a_spec=pl.BlockSpec((tk, gi), lambda i, j, k: (k, j)),
        b_specs=[pl.BlockSpec((tk, dq), lambda i, j, k: (k, j))],
        out_shape=_sds((2, N_CHIPS, gpr * gi, goq), F32),
        out_specs=pl.BlockSpec((None, N_CHIPS, gi, goq), lambda i, j, k: (j // gpr, 0, j % gpr, 0)),
        acc_shape=(gi, dq), epilogue=ep_by_chip)
    dmixed = _matmul(
        "mix_dmixed", dyb, [w_b_r], mode="nt", grid=(s_len // tm, n_groups, 1),
        a_spec=pl.BlockSpec((tm, dq), lambda i, j, k: (i, j)),
        b_specs=[pl.BlockSpec((None, gi, dq), lambda i, j, k: (j, 0, 0))],
        out_shape=_sds((s_len, wp), BF16), out_specs=pl.BlockSpec((tm, gi), lambda i, j, k: (i, j)),
        acc_shape=(tm, gi), epilogue=_ep_store(BF16))
    da1, d_lg, d_lb, d_cb, d_cw = _conv_branch_bwd("mix_conv_bwd", proj, conv_out, da3, lgv, lbv, wc, wp, ts)
    dproj = _mixer_in_bwd("mix_in_bwd", proj, da1, dmixed, dgates, cw_pad, wc, wp, ts)
    hd = d // 2
    rt = hd // 2
    gw_in = _matmul(
        "mix_dw_in", n2, [dproj], mode="tn", grid=(N_CHIPS, 4, 1),
        a_spec=pl.BlockSpec((s_len, rt), lambda j, i, k: (0, i)),
        b_specs=[pl.BlockSpec((None, s_len, npc), lambda j, i, k: (j, 0, 0))],
        out_shape=_sds((2, N_CHIPS, hd, npc), F32),
        out_specs=pl.BlockSpec((None, None, rt, npc), lambda j, i, k: (i // 2, j, i % 2, 0)),
        acc_shape=(rt, npc), epilogue=_ep_store(F32))
    rs["mix"] = _ReduceScatter("g_mix", ["w_in", "w_a_out", "w_b_group", "w_out"], [gw_in, gw_a, gw_b, gw_out],
                               qc_idx)
    rs["ffn2"].step3()
    dn2 = _matmul(
        "mix_dn", dproj, [w_in_g], mode="nt", grid=(s_len // tm, d // tn, N_CHIPS),
        a_spec=pl.BlockSpec((None, tm, npc), lambda i, j, k: (k, i, 0)),
        b_specs=[pl.BlockSpec((None, tn, npc), lambda i, j, k: (k, j, 0))],
        out_shape=_sds((s_len, d), BF16), out_specs=pl.BlockSpec((tm, tn), lambda i, j, k: (i, j)),
        acc_shape=(tm, tn), epilogue=_ep_store(BF16))
    dh1, df1, d_sh2, d_sc2, d_gm, d_gt1 = _norm_mod_bwd("mix_norm_bwd", h1, dn2, dh2, gmv, sc2, ts,
                                                        prev=(f1, gt1, 0.5))
    rs["mix"].step2()

    def w1_in_ready(g):
        rs["w1_in"] = _ReduceScatter("g_w1_in", ["w1_in"], [g], qc_idx)
        rs["w1_out"].step2()
        rs["mix"].step3()

    dn1 = _ffn_bwd(
        "ffn1", n1, hu1, act1, df1, w1_in_g, w1_out_2d, dims,
        after_dw_out=lambda g: rs.update(w1_out=_ReduceScatter("g_w1_out", ["w1_out"], [g], qc_idx)),
        after_dw_in=w1_in_ready)
    grad_x, d_sh1, d_sc1, d_g1 = _norm_mod_bwd("ffn1_norm_bwd", x2, dn1, dh1, g1v, sc1, ts)

    d_ada = jnp.concatenate([d_sh1, d_sc1, d_gt1, d_sh2, d_sc2, d_gt2, d_sh3, d_sc3, d_gt3], axis=1)
    small = [d_ada, d_g1, d_gm, d_cw[:CONV_K].reshape(1, -1), d_cb, d_lg, d_lb, d_ba, d_bb, d_ls, d_g2, d_gf,
             loss_cols]
    sizes = [a.shape[1] for a in small]
    pack1 = jnp.concatenate(small, axis=1).reshape(-1)
    n1p = -(-pack1.shape[0] // (8 * LANES)) * LANES
    pack1 = jnp.pad(pack1, (0, 8 * n1p - pack1.shape[0])).reshape(8, n1p)
    g2 = _allgather_small("gather_small_grads", pack1)
    rs["w1_in"].step2()
    total = _sum_devices("sum_small_grads", g2, 8).reshape(-1)
    offs = [0]
    for sz in sizes:
        offs.append(offs[-1] + sz)
    tot = [total[offs[k]:offs[k + 1]] for k in range(len(sizes))]
    d_ada_all = g2.reshape(N_DEV, 8 * n1p)[:, :sizes[0]]
    loss = jnp.sum(tot[12])

    grads = {}
    grads["b_ada"] = tot[0]
    grads["g_ffn1"], grads["g_mix"] = tot[1], tot[2]
    grads["conv_w"] = lax.dynamic_slice(tot[3].reshape(CONV_K, wc), (0, q * cwq), (CONV_K, cwq))
    grads["conv_b"], grads["ln_a_g"], grads["ln_a_b"], grads["b_a_out"] = tot[4], tot[5], tot[6], tot[7]
    grads["b_b_group"] = lax.dynamic_slice(tot[8].reshape(n_groups, N_CHIPS * goq), (0, q * goq), (n_groups, goq))
    grads["ls_b"], grads["g_ffn2"], grads["g_final"] = tot[9], tot[10], tot[11]

    delta, new_m, new_v = {}, {}, {}

    def adamw_group(reduced):
        for nm, g in reduced.items():
            shp = weights[nm].shape
            go, dl, mo, vo = _adamw(f"adamw_{nm}", as2d(weights[nm]), g, as2d(mom1[nm]), as2d(mom2[nm]))
            grads[nm], delta[nm], new_m[nm], new_v[nm] = go.reshape(shp), dl.reshape(shp), mo.reshape(shp), vo.reshape(shp)

    adamw_group(rs["ffn2"].result())
    rs["w1_out"].step3()
    adamw_group(rs["mix"].result())
    d_ada_mine = lax.dynamic_slice(d_ada_all, (0, q * ada_c), (N_DEV, ada_c))
    grads["w_ada"], delta["w_ada"], new_m["w_ada"], new_v["w_ada"] = _ada_grad_adamw(
        "adamw_w_ada", c_all.T, d_ada_mine, w_ada, m_w_ada, v_w_ada)
    rs["w1_in"].step3()
    smalls = [nm for nm in order if nm not in big and nm != "w_ada"]
    flat = lambda src: jnp.concatenate([src[nm].reshape(-1) for nm in smalls])
    n_small = sum(weights[nm].size for nm in smalls)
    rows_s = -(-n_small // (8 * LANES)) * 8
    packed = [jnp.pad(flat(src), (0, rows_s * LANES - n_small)).reshape(rows_s, LANES)
              for src in (weights, grads, mom1, mom2)]
    _, dl_s, mo_s, vo_s = _adamw("adamw_small", *packed)
    off = 0
    for nm in smalls:
        sz, shp = weights[nm].size, weights[nm].shape
        delta[nm] = dl_s.reshape(-1)[off:off + sz].reshape(shp)
        new_m[nm] = mo_s.reshape(-1)[off:off + sz].reshape(shp)
        new_v[nm] = vo_s.reshape(-1)[off:off + sz].reshape(shp)
        grads[nm] = grads[nm].reshape(shp)
        off += sz
    adamw_group(rs["w1_out"].result())
    adamw_group(rs["w1_in"].result())

    return (loss, grad_x[None], *[grads[nm] for nm in order], *[delta[nm] for nm in order],
            *[new_m[nm] for nm in order], *[new_v[nm] for nm in order])
```

```python
import jax
import jax.numpy as jnp
from jax import lax
from jax.experimental import pallas as pl
from jax.experimental.pallas import tpu as pltpu

F32 = jnp.float32
BF16 = jnp.bfloat16
MESH = pl.DeviceIdType.MESH
ANY = pl.BlockSpec(memory_space=pl.ANY)
HBM = pl.BlockSpec(memory_space=pltpu.HBM)
SEM = pl.BlockSpec(memory_space=pltpu.SEMAPHORE)
EFFECT = pltpu.SideEffectType.DATAFLOW_SIDE_EFFECTING

EPS = 1e-6
CONV_K = 31
HALO = 32
POOL_WINDOWS = (2, 4, 8, 16)
N_CHIPS = 4
N_DEV = 8
LANES = 128

ADAM_LR = 0.001
ADAM_B1 = 0.9
ADAM_B2 = 0.999
ADAM_EPS = 1e-08
ADAM_WD = 0.01
ADAM_STEP = 10

DN = {
    "nn": (((1,), (0,)), ((), ())),
    "nt": (((1,), (1,)), ((), ())),
    "tn": (((0,), (0,)), ((), ())),
}


_PREVIOUS = []


def _ordered(call, args, n_lead, body, token=None, sources=()):
    dep = [pltpu.with_memory_space_constraint(p, pltpu.HBM) if p.size * p.dtype.itemsize >= (1 << 20) else p
           for p in _PREVIOUS if all(p is not a for a in (*args, *sources))]

    def wrapped(*refs):
        return body(*refs[:n_lead], *refs[n_lead + len(dep):])

    outs = call(wrapped, [ANY] * len(dep))(*args, *dep)
    seq = outs if isinstance(outs, (list, tuple)) else [outs]
    _PREVIOUS[:] = [seq[token] if token is not None else
                    next(o for o in seq if jnp.issubdtype(o.dtype, jnp.floating))]
    return outs


def _pcall(body, *, name, out_shape, grid=None, in_specs=None, out_specs=None, scratch=(), aliases=None,
           prefetch=0, vmem_mb=None):
    params = {}
    if grid is not None:
        params["dimension_semantics"] = ("arbitrary",) * len(grid)
    if vmem_mb is not None:
        params["vmem_limit_bytes"] = vmem_mb << 20
    def in_hbm(shape, spec):
        big = shape.size * jnp.dtype(shape.dtype).itemsize >= (1 << 20)
        return pltpu.HBM(shape.shape, shape.dtype) if big and getattr(spec, "memory_space", None) != pltpu.VMEM else shape

    if isinstance(out_shape, (list, tuple)):
        out_shape = [in_hbm(s, sp) for s, sp in zip(out_shape, out_specs)]
    else:
        out_shape = in_hbm(out_shape, out_specs)
    kw = dict(name=name, out_shape=out_shape, compiler_params=pltpu.CompilerParams(**params))
    if aliases:
        kw["input_output_aliases"] = aliases

    def call(wrapped, dep_specs):
        specs = list(in_specs) + dep_specs
        if prefetch:
            return pl.pallas_call(wrapped, grid_spec=pltpu.PrefetchScalarGridSpec(
                num_scalar_prefetch=prefetch, grid=grid, in_specs=specs, out_specs=out_specs,
                scratch_shapes=list(scratch)), **kw)
        if grid is not None:
            return pl.pallas_call(wrapped, grid=grid, in_specs=specs, out_specs=out_specs,
                                  scratch_shapes=list(scratch), **kw)
        return pl.pallas_call(wrapped, in_specs=specs, out_specs=out_specs, scratch_shapes=list(scratch), **kw)

    def run(*args):
        specs = [None] * prefetch + list(in_specs)
        placed = [pltpu.with_memory_space_constraint(a, pltpu.HBM)
                  if a.size * a.dtype.itemsize >= (1 << 20) and getattr(s, "memory_space", None) != pltpu.VMEM else a
                  for a, s in zip(args, specs)]
        return _ordered(call, placed, prefetch + len(in_specs), body, sources=args)

    return run


def _mult(offset, unit):
    return pl.multiple_of(offset, unit)


def _tile(dim, pref):
    t = min(dim, pref)
    assert dim % t == 0, (dim, pref)
    return t


def _sds(shape, dtype):
    return jax.ShapeDtypeStruct(tuple(shape), dtype)


def _sigmoid(v):
    return 0.5 * jnp.tanh(0.5 * v) + 0.5


def _vec(w):
    return pl.BlockSpec((1, w), lambda *_: (0, 0))


def _acc_rows(ref, val, i):
    @pl.when(i == 0)
    def _():
        ref[...] = jnp.zeros_like(ref)

    ref[...] += jnp.sum(val, axis=0, keepdims=True)


def _matmul(name, a, bs, *, mode, grid, a_spec, b_specs, out_shape, out_specs, acc_shape, epilogue,
            extras=(), extra_specs=(), vmem_mb=56, carry=(), col_block=None):
    nb, ne, nk, nc = len(bs), len(extras), grid[2], len(carry)
    dn = DN[mode]

    def body(*all_refs):
        refs = all_refs[:1 + nb + ne] + all_refs[1 + nb + ne + nc:]
        a_ref, b_refs, ex = refs[0], refs[1:1 + nb], refs[1 + nb:1 + nb + ne]
        if col_block:
            outs, av, width = refs[1 + nb + ne:], a_ref[...], b_refs[0].shape[-1]
            for lo in range(0, width, col_block):
                cs = slice(lo, min(lo + col_block, width))
                epilogue([lax.dot_general(av, b[:, cs], dn, preferred_element_type=F32) for b in b_refs], ex, outs, cs)
            return
        if nk == 1:
            outs = refs[1 + nb + ne:]
            accs = [lax.dot_general(a_ref[...], b[...], dn, preferred_element_type=F32) for b in b_refs]
            epilogue(accs, ex, outs)
            return
        outs, acc_refs = refs[1 + nb + ne:-nb], refs[-nb:]
        k = pl.program_id(2)

        @pl.when(k == 0)
        def _():
            for acc in acc_refs:
                acc[...] = jnp.zeros_like(acc)

        for acc, b in zip(acc_refs, b_refs):
            acc[...] += lax.dot_general(a_ref[...], b[...], dn, preferred_element_type=F32)

        @pl.when(k == nk - 1)
        def _():
            epilogue([acc[...] for acc in acc_refs], ex, outs)

    scratch = [pltpu.VMEM(acc_shape, F32) for _ in range(nb)] if nk > 1 else []
    return _pcall(body, name=name, out_shape=out_shape, grid=grid,
                  in_specs=[a_spec, *b_specs, *extra_specs, *[ANY] * nc], out_specs=out_specs, scratch=scratch,
                  aliases={1 + nb + ne + i: i for i in range(nc)}, vmem_mb=vmem_mb)(a, *bs, *extras, *carry)


def _ep_store(dtype):
    def ep(accs, ex, outs):
        outs[0][...] = accs[0].astype(dtype)
    return ep


def _ep_halves(h):
    def ep(accs, ex, outs):
        outs[0][0] = accs[0][:h]
        outs[0][1] = accs[0][h:]
    return ep


def _place():
    x, y, c = lax.axis_index("x"), lax.axis_index("y"), lax.axis_index("c")
    chips = [(1 - x, y), (x, 1 - y), (1 - x, 1 - y)]
    return x, y, c, chips


def _allgather_small(name, block):
    m_per, n = block.shape

    def body(x_ref, out_ref, send_sems, recv_sems, local_sem):
        x, y, c, chips = _place()
        me, sibling = (x, y, c), (x, y, 1 - c)

        def rows(px, py, pc):
            return out_ref.at[pl.ds((4 * px + 2 * py + pc) * m_per, m_per), :]

        def copy(k, blk, to, src=None):
            return pltpu.make_async_remote_copy(
                src_ref=rows(*blk) if src is None else src, dst_ref=rows(*blk),
                send_sem=send_sems.at[k], recv_sem=recv_sems.at[k], device_id=to, device_id_type=MESH)

        mine = pltpu.make_async_copy(x_ref, rows(*me), local_sem)
        mine.start()
        first = [copy(0, me, sibling, src=x_ref)]
        first += [copy(1 + j, me, (*chip, c), src=x_ref) for j, chip in enumerate(chips)]
        for cp in first:
            cp.start()
        passed = [copy(4 + j, (*chip, c), sibling) for j, chip in enumerate(chips)]
        for j, chip in enumerate(chips):
            copy(1 + j, (*chip, c), me).wait_recv()
            passed[j].start()
        copy(0, sibling, me).wait_recv()
        for j, chip in enumerate(chips):
            copy(4 + j, (*chip, 1 - c), me).wait_recv()
        for cp in first + passed:
            cp.wait_send()
        mine.wait()

    return _pcall(
        body, name=name, out_shape=_sds((N_DEV * m_per, n), block.dtype),
        in_specs=[pl.BlockSpec(memory_space=pltpu.VMEM)], out_specs=pl.BlockSpec(memory_space=pltpu.VMEM),
        scratch=[pltpu.SemaphoreType.DMA((7,)), pltpu.SemaphoreType.DMA((7,)), pltpu.SemaphoreType.DMA],
    )(block)


class _SplitCopies:
    def __init__(self, name, arrays, plan, n_copies):
        self.name, self.plan, self.n = name, plan, len(arrays)
        n = self.n

        def body(*refs):
            send, recv, token = refs[n], refs[n + 1], refs[-1]
            for k, (src, dst, _, peer) in enumerate(plan(refs[:n])):
                pltpu.make_async_remote_copy(src_ref=src, dst_ref=dst, send_sem=send.at[k], recv_sem=recv.at[k],
                                             device_id=peer, device_id_type=MESH).start()
            token[...] = jnp.zeros_like(token)

        def call(wrapped, dep_specs):
            return pl.pallas_call(
                wrapped, name=f"{name}_start",
                out_shape=(pltpu.SemaphoreType.DMA((n_copies,)), pltpu.SemaphoreType.DMA((n_copies,)),
                           *[pltpu.HBM(a.shape, a.dtype) for a in arrays], _sds((8, LANES), F32)),
                in_specs=[HBM] * n + dep_specs,
                out_specs=(SEM, SEM, *[HBM] * n, pl.BlockSpec(memory_space=pltpu.VMEM)),
                input_output_aliases={i: 2 + i for i in range(n)},
                compiler_params=pltpu.CompilerParams(has_side_effects=EFFECT))

        outs = _ordered(call, [pltpu.with_memory_space_constraint(a, pltpu.HBM) for a in arrays], n, body, token=-1,
                        sources=arrays)
        self.send, self.recv, self.arrays = outs[0], outs[1], list(outs[2:2 + n])

    def wait(self, arrays=None):
        n, plan = self.n, self.plan
        if arrays is not None:
            self.arrays = list(arrays)

        def body(*refs):
            send, recv, token = refs[n], refs[n + 1], refs[-1]
            for k, (src, _, landing, peer) in enumerate(plan(refs[:n])):
                cp = pltpu.make_async_remote_copy(src_ref=src, dst_ref=landing, send_sem=send.at[k],
                                                  recv_sem=recv.at[k], device_id=peer, device_id_type=MESH)
                cp.wait_send()
                cp.wait_recv()
            token[...] = jnp.zeros_like(token)

        def call(wrapped, dep_specs):
            return pl.pallas_call(
                wrapped, name=f"{self.name}_wait",
                out_shape=(*[pltpu.HBM(a.shape, a.dtype) for a in self.arrays], _sds((8, LANES), F32)),
                in_specs=[HBM] * n + [SEM, SEM] + dep_specs,
                out_specs=(*[HBM] * n, pl.BlockSpec(memory_space=pltpu.VMEM)),
                input_output_aliases={i: i for i in range(n)},
                compiler_params=pltpu.CompilerParams(has_side_effects=EFFECT))

        return list(_ordered(call, [*self.arrays, self.send, self.recv], n + 2, body, token=-1))[:n]


def _col_range(g, cols):
    lo, width = cols if cols is not None else (0, g.shape[-1])
    return (slice(None), pl.ds(lo, width))


def _gather_ici(name, gathered, cols=None):
    def plan(refs):
        x, y, c, chips = _place()
        q = 2 * x + y
        return [(g.at[(q, c, *_col_range(g, cols))], g.at[(q, c, *_col_range(g, cols))],
                 g.at[(2 * px + py, c, *_col_range(g, cols))], (px, py, c))
                for g in refs for px, py in chips]

    return _SplitCopies(name, gathered, plan, 3 * len(gathered))


def _gather_d2d(name, gathered, cols=None):
    def plan(refs):
        x, y, c, chips = _place()
        return [(g.at[(2 * px + py, c, *_col_range(g, cols))], g.at[(2 * px + py, c, *_col_range(g, cols))],
                 g.at[(2 * px + py, 1 - c, *_col_range(g, cols))], (x, y, 1 - c))
                for g in refs for px, py in chips]

    return _SplitCopies(name, gathered, plan, 3 * len(gathered))


MXU_COLS = 256


def _two_parts(width):
    passes = width // MXU_COLS
    first = (passes // 2) * MXU_COLS if width % MXU_COLS == 0 and passes >= 2 else width // 2
    return [(0, first), (first, width - first)]


class _TwoPartGather:
    def __init__(self, name, gathered):
        self.name, self.d2d = name, {}
        self.parts = _two_parts(gathered.shape[-1])
        self.ici = [_gather_ici(f"gather_{name}_a_ici", [gathered], self.parts[0])]
        self.buf = self.ici[0].arrays

    def start_second(self):
        self.ici.append(_gather_ici(f"gather_{self.name}_b_ici", self.buf, self.parts[1]))
        self.buf = self.ici[1].arrays

    def arrive(self, part):
        here = self.ici[part].wait(self.buf)
        self.d2d[part] = _gather_d2d(f"gather_{self.name}_{'ab'[part]}_d2d", here, self.parts[part])
        self.buf = self.d2d[part].arrays

    def ready(self, part):
        self.buf = self.d2d[part].wait(self.buf)
        g = self.buf[0]
        return g.reshape(N_CHIPS, 2 * g.shape[2], g.shape[3])


def _scatter_sibling(name, grads):
    n = len(grads)

    def plan(refs):
        x, y, c, _ = _place()
        return [(refs[w].at[1 - c], refs[n + w], refs[n + w], (x, y, 1 - c)) for w in range(n)]

    landing = [lax.empty(g.shape[1:], g.dtype) for g in grads]
    return _SplitCopies(name, [*grads, *landing], plan, n)


def _scatter_chips(name, sums):
    n = len(sums)

    def plan(refs):
        x, y, c, chips = _place()
        return [(refs[w].at[2 * px + py], refs[n + w].at[j], refs[n + w].at[j], (px, py, c))
                for w in range(n) for j, (px, py) in enumerate(chips)]

    landing = [lax.empty((3, *s.shape[1:]), s.dtype) for s in sums]
    return _SplitCopies(name, [*sums, *landing], plan, 3 * n)


def _share_final(name, finals):
    def plan(refs):
        x, y, c, _ = _place()
        return [(f.at[c], f.at[c], f.at[1 - c], (x, y, 1 - c)) for f in refs]

    return _SplitCopies(name, finals, plan, len(finals))


def _row_tile(rows, cols, budget_elems=786432):
    best = 8
    for t in range(8, rows + 1, 8):
        if rows % t == 0 and t * cols <= budget_elems:
            best = t
    return best if rows % best == 0 else rows


def _sum_with_sibling(name, grad, recv, qc_idx):
    _, _, h, cols = grad.shape
    tr = _row_tile(h, cols)

    def body(s_ref, g_ref, r_ref, own_ref, pb_ref):
        p = g_ref[...] + r_ref[...]
        pb_ref[...] = p.astype(BF16)

        @pl.when(pl.program_id(1) == s_ref[0])
        def _():
            own_ref[...] = p

    blk = pl.BlockSpec((None, tr, cols), lambda r, k, s: (k, r, 0))
    return _pcall(
        body, name=name, out_shape=[_sds((h, cols), F32), _sds((N_CHIPS, h, cols), BF16)],
        grid=(h // tr, N_CHIPS), prefetch=1,
        in_specs=[pl.BlockSpec((None, None, tr, cols), lambda r, k, s: (s[1], k, r, 0)), blk],
        out_specs=[pl.BlockSpec((tr, cols), lambda r, k, s: (r, 0)), blk], vmem_mb=32,
    )(qc_idx, grad, recv)


def _sum_chips(name, own, recv, qc_idx):
    h, cols = own.shape
    tr = _row_tile(h, cols)

    def body(s_ref, p_ref, t_ref, o_ref):
        o_ref[...] = ((p_ref[...] + t_ref[0].astype(F32)) + t_ref[1].astype(F32)) + t_ref[2].astype(F32)

    return _pcall(
        body, name=name, out_shape=_sds((2, h, cols), F32), grid=(h // tr,), prefetch=1,
        in_specs=[pl.BlockSpec((tr, cols), lambda r, s: (r, 0)),
                  pl.BlockSpec((3, tr, cols), lambda r, s: (0, r, 0))],
        out_specs=pl.BlockSpec((None, tr, cols), lambda r, s: (s[1], r, 0)), vmem_mb=32,
    )(qc_idx, own, recv)


class _ReduceScatter:
    def __init__(self, tag, names, grads, qc_idx):
        self.tag, self.names, self.n, self.qc_idx = tag, names, len(grads), qc_idx
        self.copies = _scatter_sibling(f"{tag}_rs_sibling", grads)

    def step2(self):
        n = self.n
        arrs = self.copies.wait()
        sums = [_sum_with_sibling(f"{nm}_sum_sibling", arrs[w], arrs[n + w], self.qc_idx)
                for w, nm in enumerate(self.names)]
        self.own = [s[0] for s in sums]
        self.copies = _scatter_chips(f"{self.tag}_rs_chips", [s[1] for s in sums])

    def step3(self):
        n = self.n
        arrs = self.copies.wait()
        finals = [_sum_chips(f"{nm}_sum_chips", self.own[w], arrs[n + w], self.qc_idx)
                  for w, nm in enumerate(self.names)]
        self.copies = _share_final(f"{self.tag}_rs_final", finals)

    def result(self):
        return {nm: f.reshape(2 * f.shape[1], f.shape[2]) for nm, f in zip(self.names, self.copies.wait())}


def _cast_into_gathered(name, w, q_idx):
    rows, cols = w.shape
    h = rows // 2
    tr = _row_tile(h, cols, 1 << 20)
    nr = h // tr

    def body(s_ref, w_ref, o_ref):
        o_ref[...] = w_ref[...].astype(BF16)

    return _pcall(body, name=name, out_shape=_sds((N_CHIPS, 2, h, cols), BF16), grid=(2, nr), prefetch=1,
                  in_specs=[pl.BlockSpec((tr, cols), lambda hf, r, s: (hf * nr + r, 0))],
                  out_specs=pl.BlockSpec((None, None, tr, cols), lambda hf, r, s: (s[0], hf, r, 0)),
                  vmem_mb=32)(q_idx, w)


def _regroup(name, w, n_groups):
    n_chips, rows, goq = w.shape
    gi = rows // n_groups

    def body(w_ref, o_ref):
        o_ref[...] = w_ref[...]

    return _pcall(body, name=name, out_shape=_sds((n_groups, gi, n_chips * goq), w.dtype), grid=(n_groups, n_chips),
                  in_specs=[pl.BlockSpec((None, gi, goq), lambda g, k: (k, g, 0))],
                  out_specs=pl.BlockSpec((None, gi, goq), lambda g, k: (g, 0, k)), vmem_mb=32)(w)


def _rms(h):
    r = lax.rsqrt(jnp.mean(h * h, axis=-1, keepdims=True) + EPS)
    return r, h * r


def _norm_mod(name, h, g, sc, sh, ts):
    s_len, d = h.shape

    def body(h_ref, g_ref, sc_ref, sh_ref, n_ref):
        _, xhat = _rms(h_ref[...])
        n_ref[...] = ((xhat * g_ref[...]) * (1.0 + sc_ref[...]) + sh_ref[...]).astype(BF16)

    row = pl.BlockSpec((ts, d), lambda i: (i, 0))
    return _pcall(body, name=name, out_shape=_sds((s_len, d), BF16), grid=(s_len // ts,),
                  in_specs=[row, _vec(d), _vec(d), _vec(d)], out_specs=row, vmem_mb=32)(h, g, sc, sh)


def _residual_norm_mod(name, h, f, gate, cmul, g, sc, sh, ts):
    s_len, d = h.shape

    def body(h_ref, f_ref, gt_ref, g_ref, sc_ref, sh_ref, ho_ref, n_ref):
        hn = h_ref[...] + (cmul * gt_ref[...]) * f_ref[...]
        ho_ref[...] = hn
        _, xhat = _rms(hn)
        n_ref[...] = ((xhat * g_ref[...]) * (1.0 + sc_ref[...]) + sh_ref[...]).astype(BF16)

    row = pl.BlockSpec((ts, d), lambda i: (i, 0))
    return _pcall(body, name=name, out_shape=[_sds((s_len, d), F32), _sds((s_len, d), BF16)],
                  grid=(s_len // ts,), in_specs=[row, row, _vec(d), _vec(d), _vec(d), _vec(d)],
                  out_specs=[row, row], vmem_mb=32)(h, f, gate, g, sc, sh)


def _final_loss(name, h, f, tgt, gate, cmul, g, ts):
    s_len, d = h.shape

    def body(h_ref, f_ref, t_ref, gt_ref, g_ref, dh_ref, df_ref, dg_ref, dgt_ref, loss_ref):
        i = pl.program_id(0)
        fv = f_ref[...]
        coef = cmul * gt_ref[...]
        hn = h_ref[...] + coef * fv
        r, xhat = _rms(hn)
        err = xhat * g_ref[...] - t_ref[...]
        _acc_rows(loss_ref, (0.5 / d) * (err * err), i)
        dy = err * (1.0 / d)
        _acc_rows(dg_ref, dy * xhat, i)
        dxhat = dy * g_ref[...]
        dh = r * (dxhat - xhat * jnp.mean(dxhat * xhat, axis=-1, keepdims=True))
        dh_ref[...] = dh
        _acc_rows(dgt_ref, cmul * (dh * fv), i)
        df_ref[...] = (coef * dh).astype(BF16)

    row = pl.BlockSpec((ts, d), lambda i: (i, 0))
    return _pcall(body, name=name,
                  out_shape=[_sds((s_len, d), F32), _sds((s_len, d), BF16)] + [_sds((1, d), F32)] * 3,
                  grid=(s_len // ts,), in_specs=[row, row, row, _vec(d), _vec(d)],
                  out_specs=[row, row, _vec(d), _vec(d), _vec(d)], vmem_mb=40)(h, f, tgt, gate, g)


def _norm_mod_bwd(name, h, dn, dh_next, g, sc, ts, prev=None):
    s_len, d = h.shape
    has_prev = prev is not None
    cmul = prev[2] if has_prev else None

    def body(*refs):
        if has_prev:
            h_ref, dn_ref, dhn_ref, f_ref, g_ref, sc_ref, gt_ref, dh_ref, df_ref, dsh_ref, dsc_ref, dg_ref, dgt_ref = refs
        else:
            h_ref, dn_ref, dhn_ref, g_ref, sc_ref, dh_ref, dsh_ref, dsc_ref, dg_ref = refs
        i = pl.program_id(0)
        r, xhat = _rms(h_ref[...])
        dn_v = dn_ref[...].astype(F32)
        gv = g_ref[...]
        _acc_rows(dsh_ref, dn_v, i)
        _acc_rows(dsc_ref, dn_v * (xhat * gv), i)
        dnrm = dn_v * (1.0 + sc_ref[...])
        _acc_rows(dg_ref, dnrm * xhat, i)
        dxhat = dnrm * gv
        dh = dhn_ref[...] + r * (dxhat - xhat * jnp.mean(dxhat * xhat, axis=-1, keepdims=True))
        dh_ref[...] = dh
        if has_prev:
            _acc_rows(dgt_ref, cmul * (dh * f_ref[...]), i)
            df_ref[...] = ((cmul * gt_ref[...]) * dh).astype(BF16)

    row = pl.BlockSpec((ts, d), lambda i: (i, 0))
    if has_prev:
        ins, in_specs = [h, dn, dh_next, prev[0], g, sc, prev[1]], [row, row, row, row, _vec(d), _vec(d), _vec(d)]
        out_shape = [_sds((s_len, d), F32), _sds((s_len, d), BF16)] + [_sds((1, d), F32)] * 4
        out_specs = [row, row] + [_vec(d)] * 4
    else:
        ins, in_specs = [h, dn, dh_next, g, sc], [row, row, row, _vec(d), _vec(d)]
        out_shape = [_sds((s_len, d), F32)] + [_sds((1, d), F32)] * 3
        out_specs = [row] + [_vec(d)] * 3
    return _pcall(body, name=name, out_shape=out_shape, grid=(s_len // ts,), in_specs=in_specs,
                  out_specs=out_specs, vmem_mb=40)(*ins)


def _cols(ref, lo, hi, npc, rows=slice(None)):
    parts = []
    while lo < hi:
        q, o = divmod(lo, npc)
        n = min(hi - lo, npc - o)
        parts.append(ref[q, rows, o:o + n].astype(F32))
        lo += n
    return parts[0] if len(parts) == 1 else jnp.concatenate(parts, axis=-1)


def _store_cols(ref, lo, val, npc, rows=slice(None)):
    off, width = 0, val.shape[-1]
    while off < width:
        q, o = divmod(lo + off, npc)
        n = min(width - off, npc - o)
        ref[q, rows, o:o + n] = val[:, off:off + n]
        off += n


def _chips_covering(cols, npc):
    return -(-cols // npc)


SUBLANES = 8
ROW_CHUNK = 32


def _make_phases(src_ref, ph_ref):
    rows = src_ref.shape[0] - SUBLANES
    for b in range(1, SUBLANES):
        ph_ref[b - 1] = src_ref[pl.ds(b, rows), :]


def _window(src_ref, ph_ref, off, r0, cols=slice(None)):
    a, b = divmod(off, SUBLANES)
    start = pl.multiple_of(r0 + SUBLANES * a, SUBLANES)
    if b == 0:
        return src_ref[pl.ds(start, ROW_CHUNK), cols]
    return ph_ref[b - 1, pl.ds(start, ROW_CHUNK), cols]


def _phase_scratch(rows, width):
    return pltpu.VMEM((SUBLANES - 1, rows - SUBLANES, width), F32)


def _conv(a0s_ref, a0p_ref, cw_ref, cb_ref, r0):
    a1 = cb_ref[...] + cw_ref[0:1, :] * _window(a0s_ref, a0p_ref, HALO - CONV_K + 1, r0)
    for k in range(1, CONV_K):
        a1 = a1 + cw_ref[k:k + 1, :] * _window(a0s_ref, a0p_ref, HALO - CONV_K + 1 + k, r0)
    return a1


def _layer_norm(a1, lg_ref, lb_ref):
    mu = jnp.mean(a1, axis=-1, keepdims=True)
    ctr = a1 - mu
    rstd = lax.rsqrt(jnp.mean(ctr * ctr, axis=-1, keepdims=True) + EPS)
    xh = ctr * rstd
    return xh, rstd, xh * lg_ref[...] + lb_ref[...]


def _for_chunks(ts, fn):
    def step(ci, carry):
        fn(pl.multiple_of(ci * ROW_CHUNK, ROW_CHUNK))
        return carry

    lax.fori_loop(0, ts // ROW_CHUNK, step, 0)


def _stage_glu(p_ref, ph_ref, a0s_ref, i, wc, npc, ts):
    a0 = _cols(p_ref, 0, wc, npc) * _sigmoid(_cols(p_ref, wc, 2 * wc, npc))
    a0h = _cols(ph_ref, 0, wc, npc) * _sigmoid(_cols(ph_ref, wc, 2 * wc, npc))
    a0s_ref[0:HALO, :] = jnp.where(i > 0, a0h, 0.0)
    a0s_ref[HALO:HALO + ts, :] = a0


def _mixer_mid(name, proj, cw, cb, lg, lb, wc, wp, ts):
    _, s_len, npc = proj.shape
    nq = _chips_covering(2 * wc + wp, npc)
    gi = wp // len(POOL_WINDOWS)
    hb = ts // HALO

    def body(p_ref, ph_ref, cw_ref, cb_ref, lg_ref, lb_ref, a3_ref, mx_ref, a1_ref, a0s_ref, vs_ref, a0p_ref,
             vp_ref):
        i = pl.program_id(0)
        _stage_glu(p_ref, ph_ref, a0s_ref, i, wc, npc, ts)
        vs_ref[0:HALO, :] = jnp.where(i > 0, _cols(ph_ref, 2 * wc, 2 * wc + wp, npc), 0.0)
        vs_ref[HALO:HALO + ts, :] = _cols(p_ref, 2 * wc, 2 * wc + wp, npc)
        _make_phases(a0s_ref, a0p_ref)
        _make_phases(vs_ref, vp_ref)

        def chunk(r0):
            rows = pl.ds(r0, ROW_CHUNK)
            a1 = _conv(a0s_ref, a0p_ref, cw_ref, cb_ref, r0)
            a1_ref[rows, :] = a1
            _, _, a2 = _layer_norm(a1, lg_ref, lb_ref)
            a3_ref[rows, :] = (a2 * _sigmoid(a2)).astype(BF16)
            t_abs = i * ts + r0 + lax.broadcasted_iota(jnp.int32, (ROW_CHUNK, 1), 0)
            for g, win in enumerate(POOL_WINDOWS):
                cs = slice(g * gi, (g + 1) * gi)
                v_now = _window(vs_ref, vp_ref, HALO, r0, cs)
                acc = v_now
                for dlt in range(1, win):
                    acc = acc + _window(vs_ref, vp_ref, HALO - dlt, r0, cs)
                cnt = jnp.minimum(t_abs + 1, win).astype(F32)
                mx_ref[rows, cs] = (acc / cnt - v_now).astype(BF16)

        _for_chunks(ts, chunk)

    return _pcall(
        body, name=name, out_shape=[_sds((s_len, wc), BF16), _sds((s_len, wp), BF16), _sds((s_len, wc), F32)],
        grid=(s_len // ts,),
        in_specs=[pl.BlockSpec((nq, ts, npc), lambda i: (0, i, 0)),
                  pl.BlockSpec((nq, HALO, npc), lambda i: (0, jnp.maximum(i * hb - 1, 0), 0)),
                  pl.BlockSpec((HALO, wc), lambda i: (0, 0)), _vec(wc), _vec(wc), _vec(wc)],
        out_specs=[pl.BlockSpec((ts, wc), lambda i: (i, 0)), pl.BlockSpec((ts, wp), lambda i: (i, 0)),
                   pl.BlockSpec((ts, wc), lambda i: (i, 0))],
        scratch=[pltpu.VMEM((HALO + ts, wc), F32), pltpu.VMEM((HALO + ts, wp), F32),
                 _phase_scratch(HALO + ts, wc), _phase_scratch(HALO + ts, wp)], vmem_mb=56,
    )(proj, proj, cw, cb, lg, lb)


def _gates_fwd(name, proj, ya, yb, b_a, b_b, ls, wc, wp, ts):
    _, s_len, npc = proj.shape
    d = ya.shape[1]
    g0 = 2 * wc + wp

    def body(p_ref, ya_ref, yb_ref, ba_ref, bb_ref, ls_ref, z_ref):
        ga = _sigmoid(_cols(p_ref, g0, g0 + d, npc))
        gb = _sigmoid(_cols(p_ref, g0 + d, g0 + 2 * d, npc))
        z = ga * (ya_ref[...] + ba_ref[...]) + gb * ((yb_ref[...] + bb_ref[...]) * ls_ref[...])
        z_ref[...] = z.astype(BF16)

    row = pl.BlockSpec((ts, d), lambda i: (i, 0))
    return _pcall(body, name=name, out_shape=_sds((s_len, d), BF16), grid=(s_len // ts,),
                  in_specs=[pl.BlockSpec((N_CHIPS, ts, npc), lambda i: (0, i, 0)), row, row, _vec(d), _vec(d), _vec(d)],
                  out_specs=row, vmem_mb=48)(proj, ya, yb, b_a, b_b, ls)


def _gates_bwd(name, proj, dz, ya, yb, b_a, b_b, ls, wc, wp, ts):
    _, s_len, npc = proj.shape
    d = ya.shape[1]
    g0 = 2 * wc + wp

    def body(p_ref, dz_ref, ya_ref, yb_ref, ba_ref, bb_ref, ls_ref, dya_ref, dyb_ref, dgt_ref, dba_ref, dls_ref,
             dbb_ref):
        i = pl.program_id(0)
        ga = _sigmoid(_cols(p_ref, g0, g0 + d, npc))
        gb = _sigmoid(_cols(p_ref, g0 + d, g0 + 2 * d, npc))
        dz_v = dz_ref[...].astype(F32)
        y_a = ya_ref[...] + ba_ref[...]
        y_b0 = yb_ref[...] + bb_ref[...]
        ls_v = ls_ref[...]
        dya = dz_v * ga
        dya_ref[...] = dya.astype(BF16)
        _acc_rows(dba_ref, dya, i)
        t = dz_v * gb
        _acc_rows(dls_ref, t * y_b0, i)
        dyb = t * ls_v
        dyb_ref[...] = dyb.astype(BF16)
        _acc_rows(dbb_ref, dyb, i)
        dgt_ref[:, 0:d] = (dz_v * y_a * ga * (1.0 - ga)).astype(BF16)
        dgt_ref[:, d:2 * d] = (dz_v * (y_b0 * ls_v) * gb * (1.0 - gb)).astype(BF16)

    row = pl.BlockSpec((ts, d), lambda i: (i, 0))
    return _pcall(
        body, name=name,
        out_shape=[_sds((s_len, d), BF16), _sds((s_len, d), BF16), _sds((s_len, 2 * d), BF16)] + [_sds((1, d), F32)] * 3,
        grid=(s_len // ts,),
        in_specs=[pl.BlockSpec((N_CHIPS, ts, npc), lambda i: (0, i, 0)), row, row, row, _vec(d), _vec(d), _vec(d)],
        out_specs=[row, row, pl.BlockSpec((ts, 2 * d), lambda i: (i, 0))] + [_vec(d)] * 3, vmem_mb=48,
    )(proj, dz, ya, yb, b_a, b_b, ls)


def _conv_branch_bwd(name, proj, a1, da3, lg, lb, wc, wp, ts):
    _, s_len, npc = proj.shape
    nq = _chips_covering(2 * wc, npc)
    hb = ts // HALO

    n_tiles = s_len // ts

    def fold(v):
        return jnp.sum(v.reshape(ROW_CHUNK // SUBLANES, SUBLANES, v.shape[-1]), axis=0)

    def body(p_ref, ph_ref, a1_ref, da3_ref, lg_ref, lb_ref, da1_ref, dlg_ref, dlb_ref, dcb_ref, dcw_ref,
             a0s_ref, a0p_ref, vec8_ref, dcw8_ref):
        i = pl.program_id(0)
        _stage_glu(p_ref, ph_ref, a0s_ref, i, wc, npc, ts)
        _make_phases(a0s_ref, a0p_ref)

        @pl.when(i == 0)
        def _():
            vec8_ref[...] = jnp.zeros_like(vec8_ref)
            dcw8_ref[...] = jnp.zeros_like(dcw8_ref)

        def chunk(r0):
            rows = pl.ds(r0, ROW_CHUNK)
            xh, rstd, a2 = _layer_norm(a1_ref[rows, :], lg_ref, lb_ref)
            sig = _sigmoid(a2)
            da2 = da3_ref[rows, :].astype(F32) * (sig * (1.0 + a2 * (1.0 - sig)))
            vec8_ref[0] += fold(da2 * xh)
            vec8_ref[1] += fold(da2)
            dxh = da2 * lg_ref[...]
            da1 = rstd * (dxh - jnp.mean(dxh, axis=-1, keepdims=True)
                          - xh * jnp.mean(dxh * xh, axis=-1, keepdims=True))
            da1_ref[rows, :] = da1
            vec8_ref[2] += fold(da1)
            for k in range(CONV_K):
                dcw8_ref[k] += fold(da1 * _window(a0s_ref, a0p_ref, HALO - CONV_K + 1 + k, r0))

        _for_chunks(ts, chunk)

        @pl.when(i == n_tiles - 1)
        def _():
            dlg_ref[...] = jnp.sum(vec8_ref[0], axis=0, keepdims=True)
            dlb_ref[...] = jnp.sum(vec8_ref[1], axis=0, keepdims=True)
            dcb_ref[...] = jnp.sum(vec8_ref[2], axis=0, keepdims=True)
            dcw_ref[...] = jnp.sum(dcw8_ref[...], axis=1)

    return _pcall(
        body, name=name,
        out_shape=[_sds((s_len, wc), F32)] + [_sds((1, wc), F32)] * 3 + [_sds((HALO, wc), F32)],
        grid=(s_len // ts,),
        in_specs=[pl.BlockSpec((nq, ts, npc), lambda i: (0, i, 0)),
                  pl.BlockSpec((nq, HALO, npc), lambda i: (0, jnp.maximum(i * hb - 1, 0), 0)),
                  pl.BlockSpec((ts, wc), lambda i: (i, 0)), pl.BlockSpec((ts, wc), lambda i: (i, 0)),
                  _vec(wc), _vec(wc)],
        out_specs=[pl.BlockSpec((ts, wc), lambda i: (i, 0)), _vec(wc), _vec(wc), _vec(wc),
                   pl.BlockSpec((HALO, wc), lambda i: (0, 0))],
        scratch=[pltpu.VMEM((HALO + ts, wc), F32), _phase_scratch(HALO + ts, wc),
                 pltpu.VMEM((3, SUBLANES, wc), F32), pltpu.VMEM((HALO, SUBLANES, wc), F32)], vmem_mb=56,
    )(proj, proj, a1, da3, lg, lb)


def _mixer_in_bwd(name, proj, da1, dmixed, dgates, cw, wc, wp, ts):
    _, s_len, npc = proj.shape
    nq = _chips_covering(2 * wc, npc)
    gi = wp // len(POOL_WINDOWS)
    hb = ts // HALO
    n_tiles = s_len // ts
    last_hb = s_len // HALO - 1
    d2 = dgates.shape[1]

    def body(p_ref, d1_ref, d1n_ref, dm_ref, dmn_ref, dgt_ref, cw_ref, o_ref, d1s_ref, es_ref, d1p_ref, ep_ref):
        i = pl.program_id(0)
        more = i < n_tiles - 1
        d1s_ref[0:ts, :] = d1_ref[...]
        d1s_ref[ts:ts + HALO, :] = jnp.where(more, d1n_ref[...], 0.0)
        t_abs = i * ts + lax.broadcasted_iota(jnp.int32, (ts + HALO, 1), 0)
        dm_ext = jnp.concatenate([dm_ref[...].astype(F32), jnp.where(more, dmn_ref[...].astype(F32), 0.0)], axis=0)
        for g, win in enumerate(POOL_WINDOWS):
            cs = slice(g * gi, (g + 1) * gi)
            es_ref[:, cs] = dm_ext[:, cs] / jnp.minimum(t_abs + 1, win).astype(F32)
        _make_phases(d1s_ref, d1p_ref)
        _make_phases(es_ref, ep_ref)

        def chunk(r0):
            rows = pl.ds(r0, ROW_CHUNK)
            da0 = cw_ref[0:1, :] * _window(d1s_ref, d1p_ref, CONV_K - 1, r0)
            for k in range(1, CONV_K):
                da0 = da0 + cw_ref[k:k + 1, :] * _window(d1s_ref, d1p_ref, CONV_K - 1 - k, r0)
            glu_a = _cols(p_ref, 0, wc, npc, rows)
            sig = _sigmoid(_cols(p_ref, wc, 2 * wc, npc, rows))
            _store_cols(o_ref, 0, (da0 * sig).astype(BF16), npc, rows)
            _store_cols(o_ref, wc, (da0 * glu_a * sig * (1.0 - sig)).astype(BF16), npc, rows)
            parts = []
            for g, win in enumerate(POOL_WINDOWS):
                cs = slice(g * gi, (g + 1) * gi)
                acc = _window(es_ref, ep_ref, 0, r0, cs)
                for dlt in range(1, win):
                    acc = acc + _window(es_ref, ep_ref, dlt, r0, cs)
                parts.append(acc - dm_ref[rows, cs].astype(F32))
            _store_cols(o_ref, 2 * wc, jnp.concatenate(parts, axis=-1).astype(BF16), npc, rows)

        _for_chunks(ts, chunk)
        _store_cols(o_ref, 2 * wc + wp, dgt_ref[...], npc)

    nxt = lambda i: (jnp.minimum((i + 1) * hb, last_hb), 0)
    return _pcall(
        body, name=name, out_shape=_sds((N_CHIPS, s_len, npc), BF16), grid=(n_tiles,),
        in_specs=[pl.BlockSpec((nq, ts, npc), lambda i: (0, i, 0)),
                  pl.BlockSpec((ts, wc), lambda i: (i, 0)), pl.BlockSpec((HALO, wc), nxt),
                  pl.BlockSpec((ts, wp), lambda i: (i, 0)), pl.BlockSpec((HALO, wp), nxt),
                  pl.BlockSpec((ts, d2), lambda i: (i, 0)),
                  pl.BlockSpec((HALO, wc), lambda i: (0, 0))],
        out_specs=pl.BlockSpec((N_CHIPS, ts, npc), lambda i: (0, i, 0)),
        scratch=[pltpu.VMEM((ts + HALO, wc), F32), pltpu.VMEM((ts + HALO, wp), F32),
                 _phase_scratch(ts + HALO, wc), _phase_scratch(ts + HALO, wp)], vmem_mb=56,
    )(proj, da1, da1, dmixed, dmixed, dgates, cw)


def _ada_fwd(name, c_all, w, b):
    d, cols = w.shape
    tn = 512 if cols % 512 == 0 else cols

    def body(c_ref, w_ref, b_ref, o_ref):
        cv = c_ref[...]
        sc = (cv * _sigmoid(cv)).astype(BF16)
        o_ref[...] = jnp.dot(sc, w_ref[...].astype(BF16), preferred_element_type=F32) + b_ref[...]

    return _pcall(body, name=name, out_shape=_sds((N_DEV, cols), F32), grid=(cols // tn,),
                  in_specs=[pl.BlockSpec((N_DEV, d), lambda j: (0, 0)), pl.BlockSpec((d, tn), lambda j: (0, j)),
                            pl.BlockSpec((1, tn), lambda j: (0, j))],
                  out_specs=pl.BlockSpec((N_DEV, tn), lambda j: (0, j)), vmem_mb=32)(c_all, w, b)


def _adam_math(w, g, m, v):
    m_new = ADAM_B1 * m + (1.0 - ADAM_B1) * g
    v_new = ADAM_B2 * v + (1.0 - ADAM_B2) * (g * g)
    m_hat = m_new / (1.0 - ADAM_B1 ** ADAM_STEP)
    v_hat = v_new / (1.0 - ADAM_B2 ** ADAM_STEP)
    delta = -ADAM_LR * (m_hat / (jnp.sqrt(v_hat) + ADAM_EPS) + ADAM_WD * w)
    return delta, m_new, v_new


def _adamw(name, w, g, m, v):
    rows, cols = w.shape
    tr = _row_tile(rows, cols, 524288)
    n = rows // tr
    depth = 3

    def body(w_hbm, g_hbm, m_hbm, v_hbm, go_hbm, d_hbm, mo_hbm, vo_hbm, ibuf, obuf, isem, osem):
        ins, outs = (w_hbm, g_hbm, m_hbm, v_hbm), (go_hbm, d_hbm, mo_hbm, vo_hbm)

        def tile(t):
            return pl.ds(t * tr if isinstance(t, int) else pl.multiple_of(t * tr, SUBLANES), tr)

        def read(t, a):
            return pltpu.make_async_copy(ins[a].at[tile(t)], ibuf.at[t % depth, a], isem.at[t % depth, a])

        def write(t, a):
            return pltpu.make_async_copy(obuf.at[t % 2, a], outs[a].at[tile(t)], osem.at[t % 2, a])

        for t in range(min(depth - 1, n)):
            for a in range(4):
                read(t, a).start()

        def step(i, carry):
            @pl.when(i + depth - 1 < n)
            def _():
                for a in range(4):
                    read(i + depth - 1, a).start()

            for a in range(4):
                read(i, a).wait()

            @pl.when(i >= 2)
            def _():
                for a in range(4):
                    write(i - 2, a).wait()

            slot, o = i % depth, i % 2
            gv = ibuf[slot, 1]
            obuf[o, 0] = gv
            obuf[o, 1], obuf[o, 2], obuf[o, 3] = _adam_math(ibuf[slot, 0], gv, ibuf[slot, 2], ibuf[slot, 3])
            for a in range(4):
                write(i, a).start()
            return carry

        lax.fori_loop(0, n, step, 0)
        for t in range(max(n - 2, 0), n):
            for a in range(4):
                write(t, a).wait()

    return _pcall(body, name=name, out_shape=[_sds(w.shape, F32)] * 4, in_specs=[ANY] * 4, out_specs=[ANY] * 4,
                  scratch=[pltpu.VMEM((depth, 4, tr, cols), F32), pltpu.VMEM((2, 4, tr, cols), F32),
                           pltpu.SemaphoreType.DMA((depth, 4)), pltpu.SemaphoreType.DMA((2, 4))],
                  vmem_mb=48)(w, g, m, v)


def _ada_grad_adamw(name, c_t, d_ada, w, m, v):
    rows, cols = w.shape
    tr = _tile(rows, 256)
    tc = _tile(cols, 1536) if cols % 1536 == 0 else cols

    def body(c_ref, da_ref, w_ref, m_ref, v_ref, g_ref, d_ref, mo_ref, vo_ref):
        cv = c_ref[...]
        sc = cv * _sigmoid(cv)
        g = sc[:, 0:1] * da_ref[0:1, :]
        for b in range(1, N_DEV):
            g = g + sc[:, b:b + 1] * da_ref[b:b + 1, :]
        g_ref[...] = g
        d_ref[...], mo_ref[...], vo_ref[...] = _adam_math(w_ref[...], g, m_ref[...], v_ref[...])

    spec = pl.BlockSpec((tr, tc), lambda i, j: (i, j))
    return _pcall(body, name=name, out_shape=[_sds(w.shape, F32)] * 4, grid=(rows // tr, cols // tc),
                  in_specs=[pl.BlockSpec((tr, N_DEV), lambda i, j: (i, 0)),
                            pl.BlockSpec((N_DEV, tc), lambda i, j: (0, j)), spec, spec, spec],
                  out_specs=[spec] * 4, vmem_mb=40)(c_t, d_ada, w, m, v)


def _sum_devices(name, gathered, m_per):
    n = gathered.shape[1]

    def body(g_ref, o_ref):
        acc = g_ref[0:m_per, :]
        for dev in range(1, N_DEV):
            acc = acc + g_ref[dev * m_per:(dev + 1) * m_per, :]
        o_ref[...] = acc

    return _pcall(body, name=name, out_shape=_sds((m_per, n), F32),
                  in_specs=[pl.BlockSpec(memory_space=pltpu.VMEM)],
                  out_specs=pl.BlockSpec(memory_space=pltpu.VMEM))(gathered)


def _ffn_fwd(tag, n, w_in_parts, w_out_after_swiglu, dims):
    s_len, d, f_dim = dims["S"], dims["D"], dims["F"]
    tf = f_dim // 4
    tm0, tm = _tile(s_len, 512), _tile(s_len, 1024)
    p = f_dim // 2

    def ep(accs, ex, outs, cs=slice(None)):
        hh, uu = accs
        sig = _sigmoid(hh)
        silu = hh * sig
        outs[0][0, :, cs] = (uu * (sig + silu * (1.0 - sig))).astype(BF16)
        outs[0][1, :, cs] = silu.astype(BF16)
        outs[1][:, cs] = (silu * uu).astype(BF16)

    done = ()
    for part, (get_w, cols) in enumerate(w_in_parts):
        w_g = get_w().reshape(N_CHIPS * d, p)
        lo, width = cols if cols is not None else (0, p)
        mode_kw = dict(pipeline_mode=pl.Buffered(1)) if cols is None else {}
        el = pl.Element
        done = _matmul(
            f"{tag}_swiglu{part}", n, [w_g, w_g], mode="nn", grid=(2, s_len // tm0, 1),
            a_spec=pl.BlockSpec((tm0, d), lambda j, i, k: (i, 0)),
            b_specs=[pl.BlockSpec((el(d), el(width)), lambda j, i, k, lo=lo: (_mult(j * d, d), lo), **mode_kw),
                     pl.BlockSpec((el(d), el(width)), lambda j, i, k, lo=lo: (_mult((2 + j) * d, d), lo), **mode_kw)],
            out_shape=[_sds((2, s_len, f_dim), BF16), _sds((s_len, f_dim), BF16)],
            out_specs=[pl.BlockSpec((el(2), el(tm0), el(width)),
                                    lambda j, i, k, lo=lo: (0, _mult(i * tm0, tm0), _mult(j * p + lo, LANES))),
                       pl.BlockSpec((el(tm0), el(width)),
                                    lambda j, i, k, lo=lo: (_mult(i * tm0, tm0), _mult(j * p + lo, LANES)))],
            acc_shape=(tm0, width), epilogue=ep, carry=done, col_block=512 if cols is None else None)
    hu, act = done
    w_out2d = w_out_after_swiglu()
    tn2 = _tile(d, 1024)
    f = _matmul(
        f"{tag}_down", act, [w_out2d], mode="nn", grid=(s_len // tm, d // tn2, 2),
        a_spec=pl.BlockSpec((tm, 2 * tf), lambda i, j, k: (i, k)),
        b_specs=[pl.BlockSpec((2 * tf, tn2), lambda i, j, k: (k, j))],
        out_shape=_sds((s_len, d), F32), out_specs=pl.BlockSpec((tm, tn2), lambda i, j, k: (i, j)),
        acc_shape=(tm, tn2), epilogue=_ep_store(F32))
    return hu, act, f, w_out2d


def _ffn_bwd(tag, n, hu, act, df, w_in_g, w_out2d, dims, after_dw_out, after_dw_in):
    s_len, d, f_dim = dims["S"], dims["D"], dims["F"]
    tf = f_dim // 4
    tk = _tile(s_len, 2048)
    tn = _tile(d, 1024)
    g_out = _matmul(
        f"{tag}_dw_out", act, [df], mode="tn", grid=(4, d // tn, s_len // tk),
        a_spec=pl.BlockSpec((tk, tf), lambda i, j, k: (k, i)),
        b_specs=[pl.BlockSpec((tk, tn), lambda i, j, k: (k, j))],
        out_shape=_sds((2, 4, tf // 2, d), F32),
        out_specs=pl.BlockSpec((2, None, tf // 2, tn), lambda i, j, k: (0, i, 0, j)),
        acc_shape=(tf, tn), epilogue=_ep_halves(tf // 2))
    after_dw_out(g_out)

    def ep_dhu(accs, ex, outs):
        da = accs[0]
        outs[0][0] = (da * ex[0][0].astype(F32)).astype(BF16)
        outs[0][1] = (da * ex[0][1].astype(F32)).astype(BF16)

    tm = _tile(s_len, 512)
    hu_spec = pl.BlockSpec((2, tm, 2 * tf), lambda j, i, k: (0, i, j))
    dhu = _matmul(
        f"{tag}_dhu", df, [w_out2d], mode="nt", grid=(2, s_len // tm, 1),
        a_spec=pl.BlockSpec((tm, d), lambda j, i, k: (i, 0)),
        b_specs=[pl.BlockSpec((2 * tf, d), lambda j, i, k: (j, 0), pipeline_mode=pl.Buffered(1))],
        extras=[hu], extra_specs=[hu_spec],
        out_shape=_sds((2, s_len, f_dim), BF16), out_specs=hu_spec, acc_shape=(tm, 2 * tf), epilogue=ep_dhu)

    hd = d // 2
    rt = hd // 2
    g_in = _matmul(
        f"{tag}_dw_in", n, [dhu], mode="tn", grid=(N_CHIPS, 4, s_len // tk),
        a_spec=pl.BlockSpec((tk, rt), lambda j, i, k: (k, i)),
        b_specs=[pl.BlockSpec((None, tk, 2 * tf), lambda j, i, k: (j // 2, k, j % 2))],
        out_shape=_sds((2, 4, hd, f_dim // 2), F32),
        out_specs=pl.BlockSpec((None, None, rt, 2 * tf), lambda j, i, k: (i // 2, j, i % 2, 0)),
        acc_shape=(rt, 2 * tf), epilogue=_ep_store(F32))
    after_dw_in(g_in)

    tm2 = _tile(s_len, 1024)
    dn = _matmul(
        f"{tag}_dn", dhu, [w_in_g], mode="nt", grid=(s_len // tm2, d // tn, N_CHIPS),
        a_spec=pl.BlockSpec((None, tm2, 2 * tf), lambda i, j, k: (k // 2, i, k % 2)),
        b_specs=[pl.BlockSpec((None, tn, 2 * tf), lambda i, j, k: (k, j, 0))],
        out_shape=_sds((s_len, d), BF16), out_specs=pl.BlockSpec((tm2, tn), lambda i, j, k: (i, j)),
        acc_shape=(tm2, tn), epilogue=_ep_store(BF16))
    return dn


def kernel(x, c, w_ada, b_ada, g_ffn1, w1_in, w1_out, g_mix, w_in, conv_w, conv_b, ln_a_g, ln_a_b, w_a_out, b_a_out, w_b_group, b_b_group, ls_b, w_out, g_ffn2, w2_in, w2_out, g_final, loss_target, m_w_ada, m_b_ada, m_g_ffn1, m_w1_in, m_w1_out, m_g_mix, m_w_in, m_conv_w, m_conv_b, m_ln_a_g, m_ln_a_b, m_w_a_out, m_b_a_out, m_w_b_group, m_b_b_group, m_ls_b, m_w_out, m_g_ffn2, m_w2_in, m_w2_out, m_g_final, v_w_ada, v_b_ada, v_g_ffn1, v_w1_in, v_w1_out, v_g_mix, v_w_in, v_conv_w, v_conv_b, v_ln_a_g, v_ln_a_b, v_w_a_out, v_b_a_out, v_w_b_group, v_b_b_group, v_ls_b, v_w_out, v_g_ffn2, v_w2_in, v_w2_out, v_g_final):
    weights = dict(w_ada=w_ada, b_ada=b_ada, g_ffn1=g_ffn1, w1_in=w1_in, w1_out=w1_out, g_mix=g_mix, w_in=w_in,
                   conv_w=conv_w, conv_b=conv_b, ln_a_g=ln_a_g, ln_a_b=ln_a_b, w_a_out=w_a_out, b_a_out=b_a_out,
                   w_b_group=w_b_group, b_b_group=b_b_group, ls_b=ls_b, w_out=w_out, g_ffn2=g_ffn2, w2_in=w2_in,
                   w2_out=w2_out, g_final=g_final)
    mom1 = dict(w_ada=m_w_ada, b_ada=m_b_ada, g_ffn1=m_g_ffn1, w1_in=m_w1_in, w1_out=m_w1_out, g_mix=m_g_mix,
                w_in=m_w_in, conv_w=m_conv_w, conv_b=m_conv_b, ln_a_g=m_ln_a_g, ln_a_b=m_ln_a_b, w_a_out=m_w_a_out,
                b_a_out=m_b_a_out, w_b_group=m_w_b_group, b_b_group=m_b_b_group, ls_b=m_ls_b, w_out=m_w_out,
                g_ffn2=m_g_ffn2, w2_in=m_w2_in, w2_out=m_w2_out, g_final=m_g_final)
    mom2 = dict(w_ada=v_w_ada, b_ada=v_b_ada, g_ffn1=v_g_ffn1, w1_in=v_w1_in, w1_out=v_w1_out, g_mix=v_g_mix,
                w_in=v_w_in, conv_w=v_conv_w, conv_b=v_conv_b, ln_a_g=v_ln_a_g, ln_a_b=v_ln_a_b, w_a_out=v_w_a_out,
                b_a_out=v_b_a_out, w_b_group=v_w_b_group, b_b_group=v_b_b_group, ls_b=v_ls_b, w_out=v_w_out,
                g_ffn2=v_g_ffn2, w2_in=v_w2_in, w2_out=v_w2_out, g_final=v_g_final)
    order = list(weights)

    s_len, d = x.shape[1], x.shape[2]
    f_dim = w1_out.shape[0] * N_CHIPS
    wc = conv_w.shape[1] * N_CHIPS
    wp = w_b_group.shape[0] * w_b_group.shape[1]
    n_groups, gi, goq = w_b_group.shape
    npc = w_in.shape[1]
    ada_c = w_ada.shape[1]
    dims = dict(S=s_len, D=d, F=f_dim)
    ts = _tile(s_len, 256)

    xi, yi, ci = lax.axis_index("x"), lax.axis_index("y"), lax.axis_index("c")
    q = 2 * xi + yi
    dev = 2 * q + ci
    q_idx = jnp.reshape(q, (1,)).astype(jnp.int32)
    qc_idx = jnp.stack([q, ci]).astype(jnp.int32)
    _PREVIOUS.clear()

    cwq = conv_w.shape[1]
    pack0 = jnp.concatenate([c.reshape(-1), conv_w.reshape(-1), b_b_group.reshape(-1)])
    n0 = -(-pack0.shape[0] // (8 * LANES)) * LANES
    pack0 = jnp.pad(pack0, (0, 8 * n0 - pack0.shape[0])).reshape(8, n0)
    g0 = _allgather_small("gather_small_in", pack0).reshape(N_DEV, 8 * n0)
    c_all = g0[:, :d]
    south = g0[0::2]
    cw_full = jnp.concatenate([south[k, d:d + CONV_K * cwq].reshape(CONV_K, cwq) for k in range(N_CHIPS)], axis=1)
    cw_pad = jnp.pad(cw_full, ((0, HALO - CONV_K), (0, 0)))
    o_bb = d + CONV_K * cwq
    bb_full = jnp.concatenate([south[k, o_bb:o_bb + n_groups * goq].reshape(n_groups, goq) for k in range(N_CHIPS)],
                              axis=1).reshape(1, d)

    as2d = lambda a: a.reshape(-1, a.shape[-1])
    groups = dict(w1_out=["w1_out"], mix=["w_a_out", "w_b_group", "w_out"], w2_in=["w2_in"], w2_out=["w2_out"])
    big = ["w1_in", "w1_out", "w_in", "w_a_out", "w_b_group", "w_out", "w2_in", "w2_out"]
    cast = lambda nm: _cast_into_gathered(f"cast_{nm}", as2d(weights[nm]), q_idx)
    w1_in_gather = _TwoPartGather("w1_in", cast("w1_in"))

    b_ada_mine = lax.dynamic_slice(b_ada, (q * ada_c,), (ada_c,)).reshape(1, ada_c)
    ada_piece = _ada_fwd("ada_fwd", c_all, w_ada, b_ada_mine)
    casts = {nm: cast(nm) for nm in big[1:]}
    g1 = _allgather_small("gather_ada", ada_piece).reshape(N_DEV, N_DEV, ada_c)
    w1_in_gather.start_second()
    ici = {}
    for grp, names in groups.items():
        ici[grp] = _gather_ici(f"gather_{grp}_ici", [casts[nm] for nm in names])
        if grp == "w1_out":
            w_in_gather = _TwoPartGather("w_in", casts["w_in"])
            w_in_gather.start_second()
    ada_rows = lax.dynamic_index_in_dim(g1[0::2], dev, axis=1, keepdims=False)
    ada = ada_rows.reshape(3, 3, 1, d)
    (sh1, sc1, gt1), (sh2, sc2, gt2), (sh3, sc3, gt3) = [[ada[i, j] for j in range(3)] for i in range(3)]

    row = lambda vct: vct.reshape(1, -1)
    g1v, gmv, g2v, gfv = row(g_ffn1), row(g_mix), row(g_ffn2), row(g_final)

    def arrived(grp):
        return _gather_d2d(f"gather_{grp}_d2d", ici[grp].wait())

    def gathered(fwd, grp):
        return {nm: g.reshape(N_CHIPS, 2 * g.shape[2], g.shape[3]) for nm, g in zip(groups[grp], fwd.wait())}

    x2 = x[0]
    tgt = loss_target[0]

    n1 = _norm_mod("ffn1_norm", x2, g1v, sc1, sh1, ts)
    fwd, w1_in_parts = {}, []

    def w1_in_part(part):
        def get():
            w1_in_gather.arrive(part)
            w1_in_parts.append(w1_in_gather.ready(part))
            return w1_in_parts[-1]
        return get

    def w1_out_after_swiglu():
        fwd["w1_out"] = arrived("w1_out")
        w_in_gather.arrive(0)
        return gathered(fwd["w1_out"], "w1_out")["w1_out"].reshape(f_dim, d)

    hu1, act1, f1, w1_out_2d = _ffn_fwd(
        "ffn1", n1, [(w1_in_part(part), w1_in_gather.parts[part]) for part in range(2)], w1_out_after_swiglu, dims)
    w1_in_g = w1_in_parts[-1]
    h1, n2 = _residual_norm_mod("mix_norm", x2, f1, gt1, 0.5, gmv, sc2, sh2, ts)

    tm = _tile(s_len, 1024)
    tnp = npc // 2
    proj = ()
    for part in range(2):
        if part:
            w_in_gather.arrive(part)
        w_in_g = w_in_gather.ready(part)
        lo, width = w_in_gather.parts[part]
        el = pl.Element
        proj = (_matmul(
            f"mix_proj{part}", n2, [w_in_g.reshape(N_CHIPS * d, npc)], mode="nn", grid=(s_len // tm, N_CHIPS, 1),
            a_spec=pl.BlockSpec((tm, d), lambda i, j, k: (i, 0)),
            b_specs=[pl.BlockSpec((el(d), el(width)), lambda i, j, k, lo=lo: (_mult(j * d, d), lo))],
            out_shape=_sds((N_CHIPS * s_len, npc), BF16),
            out_specs=pl.BlockSpec((el(tm), el(width)), lambda i, j, k, lo=lo: (_mult(j * s_len + i * tm, tm), lo)),
            acc_shape=(tm, width), epilogue=_ep_store(BF16), carry=proj),)
    proj = proj[0].reshape(N_CHIPS, s_len, npc)
    fwd["mix"] = arrived("mix")
    cbv, lgv, lbv = row(conv_b), row(ln_a_g), row(ln_a_b)
    a3, mixed, conv_out = _mixer_mid("mix_mid", proj, cw_pad, cbv, lgv, lbv, wc, wp, ts)
    wts = gathered(fwd["mix"], "mix")
    w_out_2d = wts["w_out"].reshape(d, d)
    w_a_g = wts["w_a_out"]
    w_b_r = _regroup("regroup_w_b", wts["w_b_group"], n_groups)
    dq = d // N_CHIPS
    ya = _matmul(
        "mix_ya", a3, [w_a_g], mode="nn", grid=(s_len // tm, N_CHIPS, 1),
        a_spec=pl.BlockSpec((tm, wc), lambda i, j, k: (i, 0)),
        b_specs=[pl.BlockSpec((None, wc, dq), lambda i, j, k: (j, 0, 0))],
        out_shape=_sds((s_len, d), BF16), out_specs=pl.BlockSpec((tm, dq), lambda i, j, k: (i, j)),
        acc_shape=(tm, dq), epilogue=_ep_store(BF16))
    yb = _matmul(
        "mix_yb", mixed, [w_b_r], mode="nn", grid=(s_len // tm, n_groups, 1),
        a_spec=pl.BlockSpec((tm, gi), lambda i, j, k: (i, j)),
        b_specs=[pl.BlockSpec((None, gi, dq), lambda i, j, k: (j, 0, 0))],
        out_shape=_sds((s_len, d), BF16), out_specs=pl.BlockSpec((tm, dq), lambda i, j, k: (i, j)),
        acc_shape=(tm, dq), epilogue=_ep_store(BF16))
    bav, lsv = row(b_a_out), row(ls_b)
    z = _gates_fwd("mix_gates", proj, ya, yb, bav, bb_full, lsv, wc, wp, ts)
    tn = _tile(d, 1024)
    mix = _matmul(
        "mix_out", z, [w_out_2d], mode="nn", grid=(s_len // tm, d // tn, 1),
        a_spec=pl.BlockSpec((tm, d), lambda i, j, k: (i, 0)),
        b_specs=[pl.BlockSpec((d, tn), lambda i, j, k: (0, j))],
        out_shape=_sds((s_len, d), F32), out_specs=pl.BlockSpec((tm, tn), lambda i, j, k: (i, j)),
        acc_shape=(tm, tn), epilogue=_ep_store(F32))
    fwd["w2_in"] = arrived("w2_in")
    h2, n3 = _residual_norm_mod("ffn2_norm", h1, mix, gt2, 1.0, g2v, sc3, sh3, ts)
    w2_in_g = gathered(fwd["w2_in"], "w2_in")["w2_in"]
    hu2, act2, f3, w2_out_2d = _ffn_fwd(
        "ffn2", n3, [(lambda: w2_in_g, None)],
        lambda: gathered(arrived("w2_out"), "w2_out")["w2_out"].reshape(f_dim, d), dims)

    dh3, df3, d_gf, d_gt3, loss_cols = _final_loss("final_loss", h2, f3, tgt, gt3, 0.5, gfv, ts)
    rs, held = {}, {}
    dn3 = _ffn_bwd(
        "ffn2", n3, hu2, act2, df3, w2_in_g, w2_out_2d, dims,
        after_dw_out=lambda g: held.update(w2_out=g),
        after_dw_in=lambda g: rs.update(ffn2=_ReduceScatter("g_ffn2", ["w2_out", "w2_in"], [held["w2_out"], g],
                                                            qc_idx)))
    dh2, dmix, d_sh3, d_sc3, d_g2, d_gt2 = _norm_mod_bwd("ffn2_norm_bwd", h2, dn3, dh3, g2v, sc3, ts,
                                                         prev=(mix, gt2, 1.0))
    rs["ffn2"].step2()

    tk = s_len
    hq = d // (2 * N_CHIPS)
    gw_out = _matmul(
        "mix_dw_out", z, [dmix], mode="tn", grid=(N_CHIPS, d // tn, s_len // tk),
        a_spec=pl.BlockSpec((tk, 2 * hq), lambda i, j, k: (k, i)),
        b_specs=[pl.BlockSpec((tk, tn), lambda i, j, k: (k, j))],
        out_shape=_sds((2, N_CHIPS, hq, d), F32),
        out_specs=pl.BlockSpec((2, None, hq, tn), lambda i, j, k: (0, i, 0, j)),
        acc_shape=(2 * hq, tn), epilogue=_ep_halves(hq))
    dz = _matmul(
        "mix_dz", dmix, [w_out_2d], mode="nt", grid=(s_len // tm, d // tn, 1),
        a_spec=pl.BlockSpec((tm, d), lambda i, j, k: (i, 0)),
        b_specs=[pl.BlockSpec((tn, d), lambda i, j, k: (j, 0))],
        out_shape=_sds((s_len, d), BF16), out_specs=pl.BlockSpec((tm, tn), lambda i, j, k: (i, j)),
        acc_shape=(tm, tn), epilogue=_ep_store(BF16))
    dya, dyb, dgates, d_ba, d_ls, d_bb = _gates_bwd("mix_gates_bwd", proj, dz, ya, yb, bav, bb_full, lsv, wc, wp, ts)
    gw_a = _matmul(
        "mix_dw_a", a3, [dya], mode="tn", grid=(1, N_CHIPS, s_len // tk),
        a_spec=pl.BlockSpec((tk, wc), lambda i, j, k: (k, 0)),
        b_specs=[pl.BlockSpec((tk, dq), lambda i, j, k: (k, j))],
        out_shape=_sds((2, N_CHIPS, wc // 2, dq), F32),
        out_specs=pl.BlockSpec((2, None, wc // 2, dq), lambda i, j, k: (0, j, 0, 0)),
        acc_shape=(wc, dq), epilogue=_ep_halves(wc // 2))
    da3 = _matmul(
        "mix_da3", dya, [w_a_g], mode="nt", grid=(s_len // tm, 1, N_CHIPS),
        a_spec=pl.BlockSpec((tm, dq), lambda i, j, k: (i, k)),
        b_specs=[pl.BlockSpec((None, wc, dq), lambda i, j, k: (k, 0, 0))],
        out_shape=_sds((s_len, wc), BF16), out_specs=pl.BlockSpec((tm, wc), lambda i, j, k: (i, 0)),
        acc_shape=(tm, wc), epilogue=_ep_store(BF16))
    gpr = n_groups // 2

    def ep_by_chip(accs, ex, outs):
        for k in range(N_CHIPS):
            outs[0][k] = accs[0][:, k * goq:(k + 1) * goq]

    gw_b = _matmul(
        "mix_dw_b", mixed, [dyb], mode="tn", grid=(1, n_groups, s_len // tk),
        a_spec=pl.BlockSpec((tk, gi), lambda i, j, k: (k, j)),
        b_specs=[pl.BlockSpec((tk, dq), lambda i, j, k: (k, j))],
        out_shape=_sds((2, N_CHIPS, gpr * gi, goq), F32),
        out_specs=pl.BlockSpec((None, N_CHIPS, gi, goq), lambda i, j, k: (j // gpr, 0, j % gpr, 0)),
        acc_shape=(gi, dq), epilogue=ep_by_chip)
    dmixed = _matmul(
        "mix_dmixed", dyb, [w_b_r], mode="nt", grid=(s_len // tm, n_groups, 1),
        a_spec=pl.BlockSpec((tm, dq), lambda i, j, k: (i, j)),
        b_specs=[pl.BlockSpec((None, gi, dq), lambda i, j, k: (j, 0, 0))],
        out_shape=_sds((s_len, wp), BF16), out_specs=pl.BlockSpec((tm, gi), lambda i, j, k: (i, j)),
        acc_shape=(tm, gi), epilogue=_ep_store(BF16))
    da1, d_lg, d_lb, d_cb, d_cw = _conv_branch_bwd("mix_conv_bwd", proj, conv_out, da3, lgv, lbv, wc, wp, ts)
    dproj = _mixer_in_bwd("mix_in_bwd", proj, da1, dmixed, dgates, cw_pad, wc, wp, ts)
    hd = d // 2
    rt = hd // 2
    gw_in = _matmul(
        "mix_dw_in", n2, [dproj], mode="tn", grid=(N_CHIPS, 4, 1),
        a_spec=pl.BlockSpec((s_len, rt), lambda j, i, k: (0, i)),
        b_specs=[pl.BlockSpec((None, s_len, npc), lambda j, i, k: (j, 0, 0))],
        out_shape=_sds((2, N_CHIPS, hd, npc), F32),
        out_specs=pl.BlockSpec((None, None, rt, npc), lambda j, i, k: (i // 2, j, i % 2, 0)),
        acc_shape=(rt, npc), epilogue=_ep_store(F32))
    rs["mix"] = _ReduceScatter("g_mix", ["w_in", "w_a_out", "w_b_group", "w_out"], [gw_in, gw_a, gw_b, gw_out],
                               qc_idx)
    rs["ffn2"].step3()
    dn2 = _matmul(
        "mix_dn", dproj, [w_in_g], mode="nt", grid=(s_len // tm, d // tn, N_CHIPS),
        a_spec=pl.BlockSpec((None, tm, npc), lambda i, j, k: (k, i, 0)),
        b_specs=[pl.BlockSpec((None, tn, npc), lambda i, j, k: (k, j, 0))],
        out_shape=_sds((s_len, d), BF16), out_specs=pl.BlockSpec((tm, tn), lambda i, j, k: (i, j)),
        acc_shape=(tm, tn), epilogue=_ep_store(BF16))
    dh1, df1, d_sh2, d_sc2, d_gm, d_gt1 = _norm_mod_bwd("mix_norm_bwd", h1, dn2, dh2, gmv, sc2, ts,
                                                        prev=(f1, gt1, 0.5))
    rs["mix"].step2()

    def w1_in_ready(g):
        rs["w1_in"] = _ReduceScatter("g_w1_in", ["w1_in"], [g], qc_idx)
        rs["w1_out"].step2()
        rs["mix"].step3()

    dn1 = _ffn_bwd(
        "ffn1", n1, hu1, act1, df1, w1_in_g, w1_out_2d, dims,
        after_dw_out=lambda g: rs.update(w1_out=_ReduceScatter("g_w1_out", ["w1_out"], [g], qc_idx)),
        after_dw_in=w1_in_ready)
    grad_x, d_sh1, d_sc1, d_g1 = _norm_mod_bwd("ffn1_norm_bwd", x2, dn1, dh1, g1v, sc1, ts)

    d_ada = jnp.concatenate([d_sh1, d_sc1, d_gt1, d_sh2, d_sc2, d_gt2, d_sh3, d_sc3, d_gt3], axis=1)
    small = [d_ada, d_g1, d_gm, d_cw[:CONV_K].reshape(1, -1), d_cb, d_lg, d_lb, d_ba, d_bb, d_ls, d_g2, d_gf,
             loss_cols]
    sizes = [a.shape[1] for a in small]
    pack1 = jnp.concatenate(small, axis=1).reshape(-1)
    n1p = -(-pack1.shape[0] // (8 * LANES)) * LANES
    pack1 = jnp.pad(pack1, (0, 8 * n1p - pack1.shape[0])).reshape(8, n1p)
    g2 = _allgather_small("gather_small_grads", pack1)
    rs["w1_in"].step2()
    total = _sum_devices("sum_small_grads", g2, 8).reshape(-1)
    offs = [0]
    for sz in sizes:
        offs.append(offs[-1] + sz)
    tot = [total[offs[k]:offs[k + 1]] for k in range(len(sizes))]
    d_ada_all = g2.reshape(N_DEV, 8 * n1p)[:, :sizes[0]]
    loss = jnp.sum(tot[12])

    grads = {}
    grads["b_ada"] = tot[0]
    grads["g_ffn1"], grads["g_mix"] = tot[1], tot[2]
    grads["conv_w"] = lax.dynamic_slice(tot[3].reshape(CONV_K, wc), (0, q * cwq), (CONV_K, cwq))
    grads["conv_b"], grads["ln_a_g"], grads["ln_a_b"], grads["b_a_out"] = tot[4], tot[5], tot[6], tot[7]
    grads["b_b_group"] = lax.dynamic_slice(tot[8].reshape(n_groups, N_CHIPS * goq), (0, q * goq), (n_groups, goq))
    grads["ls_b"], grads["g_ffn2"], grads["g_final"] = tot[9], tot[10], tot[11]

    delta, new_m, new_v = {}, {}, {}

    def adamw_group(reduced):
        for nm, g in reduced.items():
            shp = weights[nm].shape
            go, dl, mo, vo = _adamw(f"adamw_{nm}", as2d(weights[nm]), g, as2d(mom1[nm]), as2d(mom2[nm]))
            grads[nm], delta[nm], new_m[nm], new_v[nm] = go.reshape(shp), dl.reshape(shp), mo.reshape(shp), vo.reshape(shp)

    adamw_group(rs["ffn2"].result())
    rs["w1_out"].step3()
    adamw_group(rs["mix"].result())
    d_ada_mine = lax.dynamic_slice(d_ada_all, (0, q * ada_c), (N_DEV, ada_c))
    grads["w_ada"], delta["w_ada"], new_m["w_ada"], new_v["w_ada"] = _ada_grad_adamw(
        "adamw_w_ada", c_all.T, d_ada_mine, w_ada, m_w_ada, v_w_ada)
    rs["w1_in"].step3()
    smalls = [nm for nm in order if nm not in big and nm != "w_ada"]
    flat = lambda src: jnp.concatenate([src[nm].reshape(-1) for nm in smalls])
    n_small = sum(weights[nm].size for nm in smalls)
    rows_s = -(-n_small // (8 * LANES)) * 8
    packed = [jnp.pad(flat(src), (0, rows_s * LANES - n_small)).reshape(rows_s, LANES)
              for src in (weights, grads, mom1, mom2)]
    _, dl_s, mo_s, vo_s = _adamw("adamw_small", *packed)
    off = 0
    for nm in smalls:
        sz, shp = weights[nm].size, weights[nm].shape
        delta[nm] = dl_s.reshape(-1)[off:off + sz].reshape(shp)
        new_m[nm] = mo_s.reshape(-1)[off:off + sz].reshape(shp)
        new_v[nm] = vo_s.reshape(-1)[off:off + sz].reshape(shp)
        grads[nm] = grads[nm].reshape(shp)
        off += sz
    adamw_group(rs["w1_out"].result())
    adamw_group(rs["w1_in"].result())

    return (loss, grad_x[None], *[grads[nm] for nm in order], *[delta[nm] for nm in order],
            *[new_m[nm] for nm in order], *[new_v[nm] for nm in order])
```

```python
import jax
import jax.numpy as jnp
from jax import lax
from jax.experimental import pallas as pl
from jax.experimental.pallas import tpu as pltpu

F32 = jnp.float32
BF16 = jnp.bfloat16
MESH = pl.DeviceIdType.MESH
ANY = pl.BlockSpec(memory_space=pl.ANY)
HBM = pl.BlockSpec(memory_space=pltpu.HBM)
SEM = pl.BlockSpec(memory_space=pltpu.SEMAPHORE)
EFFECT = pltpu.SideEffectType.DATAFLOW_SIDE_EFFECTING

EPS = 1e-6
CONV_K = 31
HALO = 32
POOL_WINDOWS = (2, 4, 8, 16)
N_CHIPS = 4
N_DEV = 8
LANES = 128

ADAM_LR = 0.001
ADAM_B1 = 0.9
ADAM_B2 = 0.999
ADAM_EPS = 1e-08
ADAM_WD = 0.01
ADAM_STEP = 10

DN = {
    "nn": (((1,), (0,)), ((), ())),
    "nt": (((1,), (1,)), ((), ())),
    "tn": (((0,), (0,)), ((), ())),
}


_PREVIOUS = []


def _ordered(call, args, n_lead, body, token=None, sources=()):
    dep = [pltpu.with_memory_space_constraint(p, pltpu.HBM) if p.size * p.dtype.itemsize >= (1 << 20) else p
           for p in _PREVIOUS if all(p is not a for a in (*args, *sources))]

    def wrapped(*refs):
        return body(*refs[:n_lead], *refs[n_lead + len(dep):])

    outs = call(wrapped, [ANY] * len(dep))(*args, *dep)
    seq = outs if isinstance(outs, (list, tuple)) else [outs]
    _PREVIOUS[:] = [seq[token] if token is not None else
                    next(o for o in seq if jnp.issubdtype(o.dtype, jnp.floating))]
    return outs


def _pcall(body, *, name, out_shape, grid=None, in_specs=None, out_specs=None, scratch=(), aliases=None,
           prefetch=0, vmem_mb=None):
    params = {}
    if grid is not None:
        params["dimension_semantics"] = ("arbitrary",) * len(grid)
    if vmem_mb is not None:
        params["vmem_limit_bytes"] = vmem_mb << 20
    def in_hbm(shape, spec):
        big = shape.size * jnp.dtype(shape.dtype).itemsize >= (1 << 20)
        return pltpu.HBM(shape.shape, shape.dtype) if big and getattr(spec, "memory_space", None) != pltpu.VMEM else shape

    if isinstance(out_shape, (list, tuple)):
        out_shape = [in_hbm(s, sp) for s, sp in zip(out_shape, out_specs)]
    else:
        out_shape = in_hbm(out_shape, out_specs)
    kw = dict(name=name, out_shape=out_shape, compiler_params=pltpu.CompilerParams(**params))
    if aliases:
        kw["input_output_aliases"] = aliases

    def call(wrapped, dep_specs):
        specs = list(in_specs) + dep_specs
        if prefetch:
            return pl.pallas_call(wrapped, grid_spec=pltpu.PrefetchScalarGridSpec(
                num_scalar_prefetch=prefetch, grid=grid, in_specs=specs, out_specs=out_specs,
                scratch_shapes=list(scratch)), **kw)
        if grid is not None:
            return pl.pallas_call(wrapped, grid=grid, in_specs=specs, out_specs=out_specs,
                                  scratch_shapes=list(scratch), **kw)
        return pl.pallas_call(wrapped, in_specs=specs, out_specs=out_specs, scratch_shapes=list(scratch), **kw)

    def run(*args):
        specs = [None] * prefetch + list(in_specs)
        placed = [pltpu.with_memory_space_constraint(a, pltpu.HBM)
                  if a.size * a.dtype.itemsize >= (1 << 20) and getattr(s, "memory_space", None) != pltpu.VMEM else a
                  for a, s in zip(args, specs)]
        return _ordered(call, placed, prefetch + len(in_specs), body, sources=args)

    return run


def _mult(offset, unit):
    return pl.multiple_of(offset, unit)


def _tile(dim, pref):
    t = min(dim, pref)
    assert dim % t == 0, (dim, pref)
    return t


def _sds(shape, dtype):
    return jax.ShapeDtypeStruct(tuple(shape), dtype)


def _sigmoid(v):
    return 0.5 * jnp.tanh(0.5 * v) + 0.5


def _vec(w):
    return pl.BlockSpec((1, w), lambda *_: (0, 0))


def _acc_rows(ref, val, i):
    @pl.when(i == 0)
    def _():
        ref[...] = jnp.zeros_like(ref)

    ref[...] += jnp.sum(val, axis=0, keepdims=True)


def _matmul(name, a, bs, *, mode, grid, a_spec, b_specs, out_shape, out_specs, acc_shape, epilogue,
            extras=(), extra_specs=(), vmem_mb=56, carry=(), col_block=None):
    nb, ne, nk, nc = len(bs), len(extras), grid[2], len(carry)
    dn = DN[mode]

    def body(*all_refs):
        refs = all_refs[:1 + nb + ne] + all_refs[1 + nb + ne + nc:]
        a_ref, b_refs, ex = refs[0], refs[1:1 + nb], refs[1 + nb:1 + nb + ne]
        if col_block:
            outs, av, width = refs[1 + nb + ne:], a_ref[...], b_refs[0].shape[-1]
            for lo in range(0, width, col_block):
                cs = slice(lo, min(lo + col_block, width))
                epilogue([lax.dot_general(av, b[:, cs], dn, preferred_element_type=F32) for b in b_refs], ex, outs, cs)
            return
        if nk == 1:
            outs = refs[1 + nb + ne:]
            accs = [lax.dot_general(a_ref[...], b[...], dn, preferred_element_type=F32) for b in b_refs]
            epilogue(accs, ex, outs)
            return
        outs, acc_refs = refs[1 + nb + ne:-nb], refs[-nb:]
        k = pl.program_id(2)

        @pl.when(k == 0)
        def _():
            for acc in acc_refs:
                acc[...] = jnp.zeros_like(acc)

        for acc, b in zip(acc_refs, b_refs):
            acc[...] += lax.dot_general(a_ref[...], b[...], dn, preferred_element_type=F32)

        @pl.when(k == nk - 1)
        def _():
            epilogue([acc[...] for acc in acc_refs], ex, outs)

    scratch = [pltpu.VMEM(acc_shape, F32) for _ in range(nb)] if nk > 1 else []
    return _pcall(body, name=name, out_shape=out_shape, grid=grid,
                  in_specs=[a_spec, *b_specs, *extra_specs, *[ANY] * nc], out_specs=out_specs, scratch=scratch,
                  aliases={1 + nb + ne + i: i for i in range(nc)}, vmem_mb=vmem_mb)(a, *bs, *extras, *carry)


def _ep_store(dtype):
    def ep(accs, ex, outs):
        outs[0][...] = accs[0].astype(dtype)
    return ep


def _ep_halves(h):
    def ep(accs, ex, outs):
        outs[0][0] = accs[0][:h]
        outs[0][1] = accs[0][h:]
    return ep


def _place():
    x, y, c = lax.axis_index("x"), lax.axis_index("y"), lax.axis_index("c")
    chips = [(1 - x, y), (x, 1 - y), (1 - x, 1 - y)]
    return x, y, c, chips


def _allgather_small(name, block):
    m_per, n = block.shape

    def body(x_ref, out_ref, send_sems, recv_sems, local_sem):
        x, y, c, chips = _place()
        me, sibling = (x, y, c), (x, y, 1 - c)

        def rows(px, py, pc):
            return out_ref.at[pl.ds((4 * px + 2 * py + pc) * m_per, m_per), :]

        def copy(k, blk, to, src=None):
            return pltpu.make_async_remote_copy(
                src_ref=rows(*blk) if src is None else src, dst_ref=rows(*blk),
                send_sem=send_sems.at[k], recv_sem=recv_sems.at[k], device_id=to, device_id_type=MESH)

        mine = pltpu.make_async_copy(x_ref, rows(*me), local_sem)
        mine.start()
        first = [copy(0, me, sibling, src=x_ref)]
        first += [copy(1 + j, me, (*chip, c), src=x_ref) for j, chip in enumerate(chips)]
        for cp in first:
            cp.start()
        passed = [copy(4 + j, (*chip, c), sibling) for j, chip in enumerate(chips)]
        for j, chip in enumerate(chips):
            copy(1 + j, (*chip, c), me).wait_recv()
            passed[j].start()
        copy(0, sibling, me).wait_recv()
        for j, chip in enumerate(chips):
            copy(4 + j, (*chip, 1 - c), me).wait_recv()
        for cp in first + passed:
            cp.wait_send()
        mine.wait()

    return _pcall(
        body, name=name, out_shape=_sds((N_DEV * m_per, n), block.dtype),
        in_specs=[pl.BlockSpec(memory_space=pltpu.VMEM)], out_specs=pl.BlockSpec(memory_space=pltpu.VMEM),
        scratch=[pltpu.SemaphoreType.DMA((7,)), pltpu.SemaphoreType.DMA((7,)), pltpu.SemaphoreType.DMA],
    )(block)


class _SplitCopies:
    def __init__(self, name, arrays, plan, n_copies):
        self.name, self.plan, self.n = name, plan, len(arrays)
        n = self.n

        def body(*refs):
            send, recv, token = refs[n], refs[n + 1], refs[-1]
            for k, (src, dst, _, peer) in enumerate(plan(refs[:n])):
                pltpu.make_async_remote_copy(src_ref=src, dst_ref=dst, send_sem=send.at[k], recv_sem=recv.at[k],
                                             device_id=peer, device_id_type=MESH).start()
            token[...] = jnp.zeros_like(token)

        def call(wrapped, dep_specs):
            return pl.pallas_call(
                wrapped, name=f"{name}_start",
                out_shape=(pltpu.SemaphoreType.DMA((n_copies,)), pltpu.SemaphoreType.DMA((n_copies,)),
                           *[pltpu.HBM(a.shape, a.dtype) for a in arrays], _sds((8, LANES), F32)),
                in_specs=[HBM] * n + dep_specs,
                out_specs=(SEM, SEM, *[HBM] * n, pl.BlockSpec(memory_space=pltpu.VMEM)),
                input_output_aliases={i: 2 + i for i in range(n)},
                compiler_params=pltpu.CompilerParams(has_side_effects=EFFECT))

        outs = _ordered(call, [pltpu.with_memory_space_constraint(a, pltpu.HBM) for a in arrays], n, body, token=-1,
                        sources=arrays)
        self.send, self.recv, self.arrays = outs[0], outs[1], list(outs[2:2 + n])

    def wait(self, arrays=None):
        n, plan = self.n, self.plan
        if arrays is not None:
            self.arrays = list(arrays)

        def body(*refs):
            send, recv, token = refs[n], refs[n + 1], refs[-1]
            for k, (src, _, landing, peer) in enumerate(plan(refs[:n])):
                cp = pltpu.make_async_remote_copy(src_ref=src, dst_ref=landing, send_sem=send.at[k],
                                                  recv_sem=recv.at[k], device_id=peer, device_id_type=MESH)
                cp.wait_send()
                cp.wait_recv()
            token[...] = jnp.zeros_like(token)

        def call(wrapped, dep_specs):
            return pl.pallas_call(
                wrapped, name=f"{self.name}_wait",
                out_shape=(*[pltpu.HBM(a.shape, a.dtype) for a in self.arrays], _sds((8, LANES), F32)),
                in_specs=[HBM] * n + [SEM, SEM] + dep_specs,
                out_specs=(*[HBM] * n, pl.BlockSpec(memory_space=pltpu.VMEM)),
                input_output_aliases={i: i for i in range(n)},
                compiler_params=pltpu.CompilerParams(has_side_effects=EFFECT))

        return list(_ordered(call, [*self.arrays, self.send, self.recv], n + 2, body, token=-1))[:n]


def _col_range(g, cols):
    lo, width = cols if cols is not None else (0, g.shape[-1])
    return (slice(None), pl.ds(lo, width))


def _gather_ici(name, gathered, cols=None):
    def plan(refs):
        x, y, c, chips = _place()
        q = 2 * x + y
        return [(g.at[(q, c, *_col_range(g, cols))], g.at[(q, c, *_col_range(g, cols))],
                 g.at[(2 * px + py, c, *_col_range(g, cols))], (px, py, c))
                for g in refs for px, py in chips]

    return _SplitCopies(name, gathered, plan, 3 * len(gathered))


def _gather_d2d(name, gathered, cols=None):
    def plan(refs):
        x, y, c, chips = _place()
        return [(g.at[(2 * px + py, c, *_col_range(g, cols))], g.at[(2 * px + py, c, *_col_range(g, cols))],
                 g.at[(2 * px + py, 1 - c, *_col_range(g, cols))], (x, y, 1 - c))
                for g in refs for px, py in chips]

    return _SplitCopies(name, gathered, plan, 3 * len(gathered))


MXU_COLS = 256


def _two_parts(width):
    passes = width // MXU_COLS
    first = (passes // 2) * MXU_COLS if width % MXU_COLS == 0 and passes >= 2 else width // 2
    return [(0, first), (first, width - first)]


class _TwoPartGather:
    def __init__(self, name, gathered):
        self.name, self.d2d = name, {}
        self.parts = _two_parts(gathered.shape[-1])
        self.ici = [_gather_ici(f"gather_{name}_a_ici", [gathered], self.parts[0])]
        self.buf = self.ici[0].arrays

    def start_second(self):
        self.ici.append(_gather_ici(f"gather_{self.name}_b_ici", self.buf, self.parts[1]))
        self.buf = self.ici[1].arrays

    def arrive(self, part):
        here = self.ici[part].wait(self.buf)
        self.d2d[part] = _gather_d2d(f"gather_{self.name}_{'ab'[part]}_d2d", here, self.parts[part])
        self.buf = self.d2d[part].arrays

    def ready(self, part):
        self.buf = self.d2d[part].wait(self.buf)
        g = self.buf[0]
        return g.reshape(N_CHIPS, 2 * g.shape[2], g.shape[3])


def _send_to_all(name, piece):
    def plan(refs):
        x, y, c, _ = _place()
        me = 4 * x + 2 * y + c
        out = []
        for fx, fy, fc in [(a, b, e) for a in (0, 1) for b in (0, 1) for e in (0, 1)][1:]:
            px, py, pc = (1 - x if fx else x), (1 - y if fy else y), (1 - c if fc else c)
            out.append((refs[0], refs[1].at[me], refs[1].at[4 * px + 2 * py + pc], (px, py, pc)))
        return out

    return _SplitCopies(name, [piece, lax.empty((N_DEV, *piece.shape), piece.dtype)], plan, N_DEV - 1)


def _scatter_sibling(name, grads):
    n = len(grads)

    def plan(refs):
        x, y, c, _ = _place()
        return [(refs[w].at[1 - c], refs[n + w], refs[n + w], (x, y, 1 - c)) for w in range(n)]

    landing = [lax.empty(g.shape[1:], g.dtype) for g in grads]
    return _SplitCopies(name, [*grads, *landing], plan, n)


def _scatter_chips(name, sums):
    n = len(sums)

    def plan(refs):
        x, y, c, chips = _place()
        return [(refs[w].at[2 * px + py], refs[n + w].at[j], refs[n + w].at[j], (px, py, c))
                for w in range(n) for j, (px, py) in enumerate(chips)]

    landing = [lax.empty((3, *s.shape[1:]), s.dtype) for s in sums]
    return _SplitCopies(name, [*sums, *landing], plan, 3 * n)


def _share_final(name, finals):
    def plan(refs):
        x, y, c, _ = _place()
        return [(f.at[c], f.at[c], f.at[1 - c], (x, y, 1 - c)) for f in refs]

    return _SplitCopies(name, finals, plan, len(finals))


def _row_tile(rows, cols, budget_elems=786432):
    best = 8
    for t in range(8, rows + 1, 8):
        if rows % t == 0 and t * cols <= budget_elems:
            best = t
    return best if rows % best == 0 else rows


def _sum_with_sibling(name, grad, recv, qc_idx):
    _, _, h, cols = grad.shape
    tr = _row_tile(h, cols)

    def body(s_ref, g_ref, r_ref, own_ref, pb_ref):
        p = g_ref[...] + r_ref[...]
        pb_ref[...] = p.astype(BF16)

        @pl.when(pl.program_id(1) == s_ref[0])
        def _():
            own_ref[...] = p

    blk = pl.BlockSpec((None, tr, cols), lambda r, k, s: (k, r, 0))
    return _pcall(
        body, name=name, out_shape=[_sds((h, cols), F32), _sds((N_CHIPS, h, cols), BF16)],
        grid=(h // tr, N_CHIPS), prefetch=1,
        in_specs=[pl.BlockSpec((None, None, tr, cols), lambda r, k, s: (s[1], k, r, 0)), blk],
        out_specs=[pl.BlockSpec((tr, cols), lambda r, k, s: (r, 0)), blk], vmem_mb=32,
    )(qc_idx, grad, recv)


def _sum_chips(name, own, recv, qc_idx):
    h, cols = own.shape
    tr = _row_tile(h, cols)

    def body(s_ref, p_ref, t_ref, o_ref):
        o_ref[...] = ((p_ref[...] + t_ref[0].astype(F32)) + t_ref[1].astype(F32)) + t_ref[2].astype(F32)

    return _pcall(
        body, name=name, out_shape=_sds((2, h, cols), F32), grid=(h // tr,), prefetch=1,
        in_specs=[pl.BlockSpec((tr, cols), lambda r, s: (r, 0)),
                  pl.BlockSpec((3, tr, cols), lambda r, s: (0, r, 0))],
        out_specs=pl.BlockSpec((None, tr, cols), lambda r, s: (s[1], r, 0)), vmem_mb=32,
    )(qc_idx, own, recv)


class _ReduceScatter:
    def __init__(self, tag, names, grads, qc_idx):
        self.tag, self.names, self.n, self.qc_idx = tag, names, len(grads), qc_idx
        self.copies = _scatter_sibling(f"{tag}_rs_sibling", grads)

    def step2(self):
        n = self.n
        arrs = self.copies.wait()
        sums = [_sum_with_sibling(f"{nm}_sum_sibling", arrs[w], arrs[n + w], self.qc_idx)
                for w, nm in enumerate(self.names)]
        self.own = [s[0] for s in sums]
        self.copies = _scatter_chips(f"{self.tag}_rs_chips", [s[1] for s in sums])

    def step3(self):
        n = self.n
        arrs = self.copies.wait()
        finals = [_sum_chips(f"{nm}_sum_chips", self.own[w], arrs[n + w], self.qc_idx)
                  for w, nm in enumerate(self.names)]
        self.copies = _share_final(f"{self.tag}_rs_final", finals)

    def result(self):
        return {nm: f.reshape(2 * f.shape[1], f.shape[2]) for nm, f in zip(self.names, self.copies.wait())}


def _cast_into_gathered(name, w, q_idx):
    rows, cols = w.shape
    h = rows // 2
    tr = _row_tile(h, cols, 1 << 20)
    nr = h // tr

    def body(s_ref, w_ref, o_ref):
        o_ref[...] = w_ref[...].astype(BF16)

    return _pcall(body, name=name, out_shape=_sds((N_CHIPS, 2, h, cols), BF16), grid=(2, nr), prefetch=1,
                  in_specs=[pl.BlockSpec((tr, cols), lambda hf, r, s: (hf * nr + r, 0))],
                  out_specs=pl.BlockSpec((None, None, tr, cols), lambda hf, r, s: (s[0], hf, r, 0)),
                  vmem_mb=32)(q_idx, w)


def _regroup(name, w, n_groups):
    n_chips, rows, goq = w.shape
    gi = rows // n_groups

    def body(w_ref, o_ref):
        o_ref[...] = w_ref[...]

    return _pcall(body, name=name, out_shape=_sds((n_groups, gi, n_chips * goq), w.dtype), grid=(n_groups, n_chips),
                  in_specs=[pl.BlockSpec((None, gi, goq), lambda g, k: (k, g, 0))],
                  out_specs=pl.BlockSpec((None, gi, goq), lambda g, k: (g, 0, k)), vmem_mb=32)(w)


def _rms(h):
    r = lax.rsqrt(jnp.mean(h * h, axis=-1, keepdims=True) + EPS)
    return r, h * r


def _norm_mod(name, h, g, sc, sh, ts):
    s_len, d = h.shape

    def body(h_ref, g_ref, sc_ref, sh_ref, n_ref):
        _, xhat = _rms(h_ref[...])
        n_ref[...] = ((xhat * g_ref[...]) * (1.0 + sc_ref[...]) + sh_ref[...]).astype(BF16)

    row = pl.BlockSpec((ts, d), lambda i: (i, 0))
    return _pcall(body, name=name, out_shape=_sds((s_len, d), BF16), grid=(s_len // ts,),
                  in_specs=[row, _vec(d), _vec(d), _vec(d)], out_specs=row, vmem_mb=32)(h, g, sc, sh)


def _residual_norm_mod(name, h, f, gate, cmul, g, sc, sh, ts):
    s_len, d = h.shape

    def body(h_ref, f_ref, gt_ref, g_ref, sc_ref, sh_ref, ho_ref, n_ref):
        hn = h_ref[...] + (cmul * gt_ref[...]) * f_ref[...]
        ho_ref[...] = hn
        _, xhat = _rms(hn)
        n_ref[...] = ((xhat * g_ref[...]) * (1.0 + sc_ref[...]) + sh_ref[...]).astype(BF16)

    row = pl.BlockSpec((ts, d), lambda i: (i, 0))
    return _pcall(body, name=name, out_shape=[_sds((s_len, d), F32), _sds((s_len, d), BF16)],
                  grid=(s_len // ts,), in_specs=[row, row, _vec(d), _vec(d), _vec(d), _vec(d)],
                  out_specs=[row, row], vmem_mb=32)(h, f, gate, g, sc, sh)


def _final_loss(name, h, f, tgt, gate, cmul, g, ts):
    s_len, d = h.shape

    def body(h_ref, f_ref, t_ref, gt_ref, g_ref, dh_ref, df_ref, dg_ref, dgt_ref, loss_ref):
        i = pl.program_id(0)
        fv = f_ref[...]
        coef = cmul * gt_ref[...]
        hn = h_ref[...] + coef * fv
        r, xhat = _rms(hn)
        err = xhat * g_ref[...] - t_ref[...]
        _acc_rows(loss_ref, (0.5 / d) * (err * err), i)
        dy = err * (1.0 / d)
        _acc_rows(dg_ref, dy * xhat, i)
        dxhat = dy * g_ref[...]
        dh = r * (dxhat - xhat * jnp.mean(dxhat * xhat, axis=-1, keepdims=True))
        dh_ref[...] = dh
        _acc_rows(dgt_ref, cmul * (dh * fv), i)
        df_ref[...] = (coef * dh).astype(BF16)

    row = pl.BlockSpec((ts, d), lambda i: (i, 0))
    return _pcall(body, name=name,
                  out_shape=[_sds((s_len, d), F32), _sds((s_len, d), BF16)] + [_sds((1, d), F32)] * 3,
                  grid=(s_len // ts,), in_specs=[row, row, row, _vec(d), _vec(d)],
                  out_specs=[row, row, _vec(d), _vec(d), _vec(d)], vmem_mb=40)(h, f, tgt, gate, g)


def _norm_mod_bwd(name, h, dn, dh_next, g, sc, ts, prev=None):
    s_len, d = h.shape
    has_prev = prev is not None
    cmul = prev[2] if has_prev else None

    def body(*refs):
        if has_prev:
            h_ref, dn_ref, dhn_ref, f_ref, g_ref, sc_ref, gt_ref, dh_ref, df_ref, dsh_ref, dsc_ref, dg_ref, dgt_ref = refs
        else:
            h_ref, dn_ref, dhn_ref, g_ref, sc_ref, dh_ref, dsh_ref, dsc_ref, dg_ref = refs
        i = pl.program_id(0)
        r, xhat = _rms(h_ref[...])
        dn_v = dn_ref[...].astype(F32)
        gv = g_ref[...]
        _acc_rows(dsh_ref, dn_v, i)
        _acc_rows(dsc_ref, dn_v * (xhat * gv), i)
        dnrm = dn_v * (1.0 + sc_ref[...])
        _acc_rows(dg_ref, dnrm * xhat, i)
        dxhat = dnrm * gv
        dh = dhn_ref[...] + r * (dxhat - xhat * jnp.mean(dxhat * xhat, axis=-1, keepdims=True))
        dh_ref[...] = dh
        if has_prev:
            _acc_rows(dgt_ref, cmul * (dh * f_ref[...]), i)
            df_ref[...] = ((cmul * gt_ref[...]) * dh).astype(BF16)

    row = pl.BlockSpec((ts, d), lambda i: (i, 0))
    if has_prev:
        ins, in_specs = [h, dn, dh_next, prev[0], g, sc, prev[1]], [row, row, row, row, _vec(d), _vec(d), _vec(d)]
        out_shape = [_sds((s_len, d), F32), _sds((s_len, d), BF16)] + [_sds((1, d), F32)] * 4
        out_specs = [row, row] + [_vec(d)] * 4
    else:
        ins, in_specs = [h, dn, dh_next, g, sc], [row, row, row, _vec(d), _vec(d)]
        out_shape = [_sds((s_len, d), F32)] + [_sds((1, d), F32)] * 3
        out_specs = [row] + [_vec(d)] * 3
    return _pcall(body, name=name, out_shape=out_shape, grid=(s_len // ts,), in_specs=in_specs,
                  out_specs=out_specs, vmem_mb=40)(*ins)


def _cols(ref, lo, hi, npc, rows=slice(None)):
    parts = []
    while lo < hi:
        q, o = divmod(lo, npc)
        n = min(hi - lo, npc - o)
        parts.append(ref[q, rows, o:o + n].astype(F32))
        lo += n
    return parts[0] if len(parts) == 1 else jnp.concatenate(parts, axis=-1)


def _store_cols(ref, lo, val, npc, rows=slice(None)):
    off, width = 0, val.shape[-1]
    while off < width:
        q, o = divmod(lo + off, npc)
        n = min(width - off, npc - o)
        ref[q, rows, o:o + n] = val[:, off:off + n]
        off += n


def _chips_covering(cols, npc):
    return -(-cols // npc)


SUBLANES = 8
ROW_CHUNK = 32


def _make_phases(src_ref, ph_ref):
    rows = src_ref.shape[0] - SUBLANES
    for b in range(1, SUBLANES):
        ph_ref[b - 1] = src_ref[pl.ds(b, rows), :]


def _window(src_ref, ph_ref, off, r0, cols=slice(None)):
    a, b = divmod(off, SUBLANES)
    start = pl.multiple_of(r0 + SUBLANES * a, SUBLANES)
    if b == 0:
        return src_ref[pl.ds(start, ROW_CHUNK), cols]
    return ph_ref[b - 1, pl.ds(start, ROW_CHUNK), cols]


def _phase_scratch(rows, width):
    return pltpu.VMEM((SUBLANES - 1, rows - SUBLANES, width), F32)


def _conv(a0s_ref, a0p_ref, cw_ref, cb_ref, r0):
    a1 = cb_ref[...] + cw_ref[0:1, :] * _window(a0s_ref, a0p_ref, HALO - CONV_K + 1, r0)
    for k in range(1, CONV_K):
        a1 = a1 + cw_ref[k:k + 1, :] * _window(a0s_ref, a0p_ref, HALO - CONV_K + 1 + k, r0)
    return a1


def _layer_norm(a1, lg_ref, lb_ref):
    mu = jnp.mean(a1, axis=-1, keepdims=True)
    ctr = a1 - mu
    rstd = lax.rsqrt(jnp.mean(ctr * ctr, axis=-1, keepdims=True) + EPS)
    xh = ctr * rstd
    return xh, rstd, xh * lg_ref[...] + lb_ref[...]


def _for_chunks(ts, fn):
    def step(ci, carry):
        fn(pl.multiple_of(ci * ROW_CHUNK, ROW_CHUNK))
        return carry

    lax.fori_loop(0, ts // ROW_CHUNK, step, 0)


def _stage_glu(p_ref, ph_ref, a0s_ref, i, wc, npc, ts):
    a0 = _cols(p_ref, 0, wc, npc) * _sigmoid(_cols(p_ref, wc, 2 * wc, npc))
    a0h = _cols(ph_ref, 0, wc, npc) * _sigmoid(_cols(ph_ref, wc, 2 * wc, npc))
    a0s_ref[0:HALO, :] = jnp.where(i > 0, a0h, 0.0)
    a0s_ref[HALO:HALO + ts, :] = a0


def _mixer_mid(name, proj, cw, cb, lg, lb, wc, wp, ts):
    _, s_len, npc = proj.shape
    nq = _chips_covering(2 * wc + wp, npc)
    gi = wp // len(POOL_WINDOWS)
    hb = ts // HALO

    def body(p_ref, ph_ref, cw_ref, cb_ref, lg_ref, lb_ref, a3_ref, mx_ref, a1_ref, a0s_ref, vs_ref, a0p_ref,
             vp_ref):
        i = pl.program_id(0)
        _stage_glu(p_ref, ph_ref, a0s_ref, i, wc, npc, ts)
        vs_ref[0:HALO, :] = jnp.where(i > 0, _cols(ph_ref, 2 * wc, 2 * wc + wp, npc), 0.0)
        vs_ref[HALO:HALO + ts, :] = _cols(p_ref, 2 * wc, 2 * wc + wp, npc)
        _make_phases(a0s_ref, a0p_ref)
        _make_phases(vs_ref, vp_ref)

        def chunk(r0):
            rows = pl.ds(r0, ROW_CHUNK)
            a1 = _conv(a0s_ref, a0p_ref, cw_ref, cb_ref, r0)
            a1_ref[rows, :] = a1
            _, _, a2 = _layer_norm(a1, lg_ref, lb_ref)
            a3_ref[rows, :] = (a2 * _sigmoid(a2)).astype(BF16)
            t_abs = i * ts + r0 + lax.broadcasted_iota(jnp.int32, (ROW_CHUNK, 1), 0)
            for g, win in enumerate(POOL_WINDOWS):
                cs = slice(g * gi, (g + 1) * gi)
                v_now = _window(vs_ref, vp_ref, HALO, r0, cs)
                acc = v_now
                for dlt in range(1, win):
                    acc = acc + _window(vs_ref, vp_ref, HALO - dlt, r0, cs)
                cnt = jnp.minimum(t_abs + 1, win).astype(F32)
                mx_ref[rows, cs] = (acc / cnt - v_now).astype(BF16)

        _for_chunks(ts, chunk)

    return _pcall(
        body, name=name, out_shape=[_sds((s_len, wc), BF16), _sds((s_len, wp), BF16), _sds((s_len, wc), F32)],
        grid=(s_len // ts,),
        in_specs=[pl.BlockSpec((nq, ts, npc), lambda i: (0, i, 0)),
                  pl.BlockSpec((nq, HALO, npc), lambda i: (0, jnp.maximum(i * hb - 1, 0), 0)),
                  pl.BlockSpec((HALO, wc), lambda i: (0, 0)), _vec(wc), _vec(wc), _vec(wc)],
        out_specs=[pl.BlockSpec((ts, wc), lambda i: (i, 0)), pl.BlockSpec((ts, wp), lambda i: (i, 0)),
                   pl.BlockSpec((ts, wc), lambda i: (i, 0))],
        scratch=[pltpu.VMEM((HALO + ts, wc), F32), pltpu.VMEM((HALO + ts, wp), F32),
                 _phase_scratch(HALO + ts, wc), _phase_scratch(HALO + ts, wp)], vmem_mb=56,
    )(proj, proj, cw, cb, lg, lb)


def _gates_fwd(name, proj, ya, yb, b_a, b_b, ls, wc, wp, ts):
    _, s_len, npc = proj.shape
    d = ya.shape[1]
    g0 = 2 * wc + wp

    def body(p_ref, ya_ref, yb_ref, ba_ref, bb_ref, ls_ref, z_ref):
        ga = _sigmoid(_cols(p_ref, g0, g0 + d, npc))
        gb = _sigmoid(_cols(p_ref, g0 + d, g0 + 2 * d, npc))
        z = ga * (ya_ref[...] + ba_ref[...]) + gb * ((yb_ref[...] + bb_ref[...]) * ls_ref[...])
        z_ref[...] = z.astype(BF16)

    row = pl.BlockSpec((ts, d), lambda i: (i, 0))
    return _pcall(body, name=name, out_shape=_sds((s_len, d), BF16), grid=(s_len // ts,),
                  in_specs=[pl.BlockSpec((N_CHIPS, ts, npc), lambda i: (0, i, 0)), row, row, _vec(d), _vec(d), _vec(d)],
                  out_specs=row, vmem_mb=48)(proj, ya, yb, b_a, b_b, ls)


def _gates_bwd(name, proj, dz, ya, yb, b_a, b_b, ls, wc, wp, ts):
    _, s_len, npc = proj.shape
    d = ya.shape[1]
    g0 = 2 * wc + wp

    def body(p_ref, dz_ref, ya_ref, yb_ref, ba_ref, bb_ref, ls_ref, dya_ref, dyb_ref, dgt_ref, dba_ref, dls_ref,
             dbb_ref):
        i = pl.program_id(0)
        ga = _sigmoid(_cols(p_ref, g0, g0 + d, npc))
        gb = _sigmoid(_cols(p_ref, g0 + d, g0 + 2 * d, npc))
        dz_v = dz_ref[...].astype(F32)
        y_a = ya_ref[...] + ba_ref[...]
        y_b0 = yb_ref[...] + bb_ref[...]
        ls_v = ls_ref[...]
        dya = dz_v * ga
        dya_ref[...] = dya.astype(BF16)
        _acc_rows(dba_ref, dya, i)
        t = dz_v * gb
        _acc_rows(dls_ref, t * y_b0, i)
        dyb = t * ls_v
        dyb_ref[...] = dyb.astype(BF16)
        _acc_rows(dbb_ref, dyb, i)
        dgt_ref[:, 0:d] = (dz_v * y_a * ga * (1.0 - ga)).astype(BF16)
        dgt_ref[:, d:2 * d] = (dz_v * (y_b0 * ls_v) * gb * (1.0 - gb)).astype(BF16)

    row = pl.BlockSpec((ts, d), lambda i: (i, 0))
    return _pcall(
        body, name=name,
        out_shape=[_sds((s_len, d), BF16), _sds((s_len, d), BF16), _sds((s_len, 2 * d), BF16)] + [_sds((1, d), F32)] * 3,
        grid=(s_len // ts,),
        in_specs=[pl.BlockSpec((N_CHIPS, ts, npc), lambda i: (0, i, 0)), row, row, row, _vec(d), _vec(d), _vec(d)],
        out_specs=[row, row, pl.BlockSpec((ts, 2 * d), lambda i: (i, 0))] + [_vec(d)] * 3, vmem_mb=48,
    )(proj, dz, ya, yb, b_a, b_b, ls)


def _conv_branch_bwd(name, proj, a1, da3, lg, lb, wc, wp, ts):
    _, s_len, npc = proj.shape
    nq = _chips_covering(2 * wc, npc)
    hb = ts // HALO

    n_tiles = s_len // ts

    def fold(v):
        return jnp.sum(v.reshape(ROW_CHUNK // SUBLANES, SUBLANES, v.shape[-1]), axis=0)

    def body(p_ref, ph_ref, a1_ref, da3_ref, lg_ref, lb_ref, da1_ref, dlg_ref, dlb_ref, dcb_ref, dcw_ref,
             a0s_ref, a0p_ref, vec8_ref, dcw8_ref):
        i = pl.program_id(0)
        _stage_glu(p_ref, ph_ref, a0s_ref, i, wc, npc, ts)
        _make_phases(a0s_ref, a0p_ref)

        @pl.when(i == 0)
        def _():
            vec8_ref[...] = jnp.zeros_like(vec8_ref)
            dcw8_ref[...] = jnp.zeros_like(dcw8_ref)

        def chunk(r0):
            rows = pl.ds(r0, ROW_CHUNK)
            xh, rstd, a2 = _layer_norm(a1_ref[rows, :], lg_ref, lb_ref)
            sig = _sigmoid(a2)
            da2 = da3_ref[rows, :].astype(F32) * (sig * (1.0 + a2 * (1.0 - sig)))
            vec8_ref[0] += fold(da2 * xh)
            vec8_ref[1] += fold(da2)
            dxh = da2 * lg_ref[...]
            da1 = rstd * (dxh - jnp.mean(dxh, axis=-1, keepdims=True)
                          - xh * jnp.mean(dxh * xh, axis=-1, keepdims=True))
            da1_ref[rows, :] = da1
            vec8_ref[2] += fold(da1)
            for k in range(CONV_K):
                dcw8_ref[k] += fold(da1 * _window(a0s_ref, a0p_ref, HALO - CONV_K + 1 + k, r0))

        _for_chunks(ts, chunk)

        @pl.when(i == n_tiles - 1)
        def _():
            dlg_ref[...] = jnp.sum(vec8_ref[0], axis=0, keepdims=True)
            dlb_ref[...] = jnp.sum(vec8_ref[1], axis=0, keepdims=True)
            dcb_ref[...] = jnp.sum(vec8_ref[2], axis=0, keepdims=True)
            dcw_ref[...] = jnp.sum(dcw8_ref[...], axis=1)

    return _pcall(
        body, name=name,
        out_shape=[_sds((s_len, wc), F32)] + [_sds((1, wc), F32)] * 3 + [_sds((HALO, wc), F32)],
        grid=(s_len // ts,),
        in_specs=[pl.BlockSpec((nq, ts, npc), lambda i: (0, i, 0)),
                  pl.BlockSpec((nq, HALO, npc), lambda i: (0, jnp.maximum(i * hb - 1, 0), 0)),
                  pl.BlockSpec((ts, wc), lambda i: (i, 0)), pl.BlockSpec((ts, wc), lambda i: (i, 0)),
                  _vec(wc), _vec(wc)],
        out_specs=[pl.BlockSpec((ts, wc), lambda i: (i, 0)), _vec(wc), _vec(wc), _vec(wc),
                   pl.BlockSpec((HALO, wc), lambda i: (0, 0))],
        scratch=[pltpu.VMEM((HALO + ts, wc), F32), _phase_scratch(HALO + ts, wc),
                 pltpu.VMEM((3, SUBLANES, wc), F32), pltpu.VMEM((HALO, SUBLANES, wc), F32)], vmem_mb=56,
    )(proj, proj, a1, da3, lg, lb)


def _mixer_in_bwd(name, proj, da1, dmixed, dgates, cw, wc, wp, ts):
    _, s_len, npc = proj.shape
    nq = _chips_covering(2 * wc, npc)
    gi = wp // len(POOL_WINDOWS)
    hb = ts // HALO
    n_tiles = s_len // ts
    last_hb = s_len // HALO - 1
    d2 = dgates.shape[1]

    def body(p_ref, d1_ref, d1n_ref, dm_ref, dmn_ref, dgt_ref, cw_ref, o_ref, d1s_ref, es_ref, d1p_ref, ep_ref):
        i = pl.program_id(0)
        more = i < n_tiles - 1
        d1s_ref[0:ts, :] = d1_ref[...]
        d1s_ref[ts:ts + HALO, :] = jnp.where(more, d1n_ref[...], 0.0)
        t_abs = i * ts + lax.broadcasted_iota(jnp.int32, (ts + HALO, 1), 0)
        dm_ext = jnp.concatenate([dm_ref[...].astype(F32), jnp.where(more, dmn_ref[...].astype(F32), 0.0)], axis=0)
        for g, win in enumerate(POOL_WINDOWS):
            cs = slice(g * gi, (g + 1) * gi)
            es_ref[:, cs] = dm_ext[:, cs] / jnp.minimum(t_abs + 1, win).astype(F32)
        _make_phases(d1s_ref, d1p_ref)
        _make_phases(es_ref, ep_ref)

        def chunk(r0):
            rows = pl.ds(r0, ROW_CHUNK)
            da0 = cw_ref[0:1, :] * _window(d1s_ref, d1p_ref, CONV_K - 1, r0)
            for k in range(1, CONV_K):
                da0 = da0 + cw_ref[k:k + 1, :] * _window(d1s_ref, d1p_ref, CONV_K - 1 - k, r0)
            glu_a = _cols(p_ref, 0, wc, npc, rows)
            sig = _sigmoid(_cols(p_ref, wc, 2 * wc, npc, rows))
            _store_cols(o_ref, 0, (da0 * sig).astype(BF16), npc, rows)
            _store_cols(o_ref, wc, (da0 * glu_a * sig * (1.0 - sig)).astype(BF16), npc, rows)
            parts = []
            for g, win in enumerate(POOL_WINDOWS):
                cs = slice(g * gi, (g + 1) * gi)
                acc = _window(es_ref, ep_ref, 0, r0, cs)
                for dlt in range(1, win):
                    acc = acc + _window(es_ref, ep_ref, dlt, r0, cs)
                parts.append(acc - dm_ref[rows, cs].astype(F32))
            _store_cols(o_ref, 2 * wc, jnp.concatenate(parts, axis=-1).astype(BF16), npc, rows)

        _for_chunks(ts, chunk)
        _store_cols(o_ref, 2 * wc + wp, dgt_ref[...], npc)

    nxt = lambda i: (jnp.minimum((i + 1) * hb, last_hb), 0)
    return _pcall(
        body, name=name, out_shape=_sds((N_CHIPS, s_len, npc), BF16), grid=(n_tiles,),
        in_specs=[pl.BlockSpec((nq, ts, npc), lambda i: (0, i, 0)),
                  pl.BlockSpec((ts, wc), lambda i: (i, 0)), pl.BlockSpec((HALO, wc), nxt),
                  pl.BlockSpec((ts, wp), lambda i: (i, 0)), pl.BlockSpec((HALO, wp), nxt),
                  pl.BlockSpec((ts, d2), lambda i: (i, 0)),
                  pl.BlockSpec((HALO, wc), lambda i: (0, 0))],
        out_specs=pl.BlockSpec((N_CHIPS, ts, npc), lambda i: (0, i, 0)),
        scratch=[pltpu.VMEM((ts + HALO, wc), F32), pltpu.VMEM((ts + HALO, wp), F32),
                 _phase_scratch(ts + HALO, wc), _phase_scratch(ts + HALO, wp)], vmem_mb=56,
    )(proj, da1, da1, dmixed, dmixed, dgates, cw)


def _ada_fwd(name, c_all, w, b):
    d, cols = w.shape
    tn = 512 if cols % 512 == 0 else cols

    def body(c_ref, w_ref, b_ref, o_ref):
        cv = c_ref[...]
        sc = (cv * _sigmoid(cv)).astype(BF16)
        o_ref[...] = jnp.dot(sc, w_ref[...].astype(BF16), preferred_element_type=F32) + b_ref[...]

    return _pcall(body, name=name, out_shape=_sds((N_DEV, cols), F32), grid=(cols // tn,),
                  in_specs=[pl.BlockSpec((N_DEV, d), lambda j: (0, 0)), pl.BlockSpec((d, tn), lambda j: (0, j)),
                            pl.BlockSpec((1, tn), lambda j: (0, j))],
                  out_specs=pl.BlockSpec((N_DEV, tn), lambda j: (0, j)), vmem_mb=32)(c_all, w, b)


def _adam_math(w, g, m, v):
    m_new = ADAM_B1 * m + (1.0 - ADAM_B1) * g
    v_new = ADAM_B2 * v + (1.0 - ADAM_B2) * (g * g)
    m_hat = m_new / (1.0 - ADAM_B1 ** ADAM_STEP)
    v_hat = v_new / (1.0 - ADAM_B2 ** ADAM_STEP)
    delta = -ADAM_LR * (m_hat / (jnp.sqrt(v_hat) + ADAM_EPS) + ADAM_WD * w)
    return delta, m_new, v_new


def _adamw(name, w, g, m, v):
    rows, cols = w.shape
    tr = _row_tile(rows, cols, 524288)

    def body(w_ref, g_ref, m_ref, v_ref, go_ref, d_ref, mo_ref, vo_ref):
        g = g_ref[...]
        go_ref[...] = g
        d_ref[...], mo_ref[...], vo_ref[...] = _adam_math(w_ref[...], g, m_ref[...], v_ref[...])

    spec = pl.BlockSpec((tr, cols), lambda i: (i, 0))
    return _pcall(body, name=name, out_shape=[_sds(w.shape, F32)] * 4, grid=(rows // tr,), in_specs=[spec] * 4,
                  out_specs=[spec] * 4, vmem_mb=40)(w, g, m, v)


def _ada_grad_adamw(name, c_t, d_ada, w, m, v):
    rows, cols = w.shape
    tr = _tile(rows, 256)
    tc = _tile(cols, 1536) if cols % 1536 == 0 else cols

    def body(c_ref, da_ref, w_ref, m_ref, v_ref, g_ref, d_ref, mo_ref, vo_ref):
        cv = c_ref[...]
        sc = cv * _sigmoid(cv)
        g = sc[:, 0:1] * da_ref[0:1, :]
        for b in range(1, N_DEV):
            g = g + sc[:, b:b + 1] * da_ref[b:b + 1, :]
        g_ref[...] = g
        d_ref[...], mo_ref[...], vo_ref[...] = _adam_math(w_ref[...], g, m_ref[...], v_ref[...])

    spec = pl.BlockSpec((tr, tc), lambda i, j: (i, j))
    return _pcall(body, name=name, out_shape=[_sds(w.shape, F32)] * 4, grid=(rows // tr, cols // tc),
                  in_specs=[pl.BlockSpec((tr, N_DEV), lambda i, j: (i, 0)),
                            pl.BlockSpec((N_DEV, tc), lambda i, j: (0, j)), spec, spec, spec],
                  out_specs=[spec] * 4, vmem_mb=40)(c_t, d_ada, w, m, v)


def _sum_devices(name, gathered, m_per):
    n = gathered.shape[1]

    def body(g_ref, o_ref):
        acc = g_ref[0:m_per, :]
        for dev in range(1, N_DEV):
            acc = acc + g_ref[dev * m_per:(dev + 1) * m_per, :]
        o_ref[...] = acc

    return _pcall(body, name=name, out_shape=_sds((m_per, n), F32),
                  in_specs=[pl.BlockSpec(memory_space=pltpu.VMEM)],
                  out_specs=pl.BlockSpec(memory_space=pltpu.VMEM))(gathered)


def _ffn_fwd(tag, n, w_in_parts, w_out_after_swiglu, dims):
    s_len, d, f_dim = dims["S"], dims["D"], dims["F"]
    tf = f_dim // 4
    tm0, tm = _tile(s_len, 512), _tile(s_len, 1024)
    p = f_dim // 2

    def ep(accs, ex, outs, cs=slice(None)):
        hh, uu = accs
        sig = _sigmoid(hh)
        silu = hh * sig
        outs[0][0, :, cs] = (uu * (sig + silu * (1.0 - sig))).astype(BF16)
        outs[0][1, :, cs] = silu.astype(BF16)
        outs[1][:, cs] = (silu * uu).astype(BF16)

    done = ()
    for part, (get_w, cols) in enumerate(w_in_parts):
        w_g = get_w().reshape(N_CHIPS * d, p)
        lo, width = cols if cols is not None else (0, p)
        mode_kw = dict(pipeline_mode=pl.Buffered(1)) if cols is None else {}
        el = pl.Element
        done = _matmul(
            f"{tag}_swiglu{part}", n, [w_g, w_g], mode="nn", grid=(2, s_len // tm0, 1),
            a_spec=pl.BlockSpec((tm0, d), lambda j, i, k: (i, 0)),
            b_specs=[pl.BlockSpec((el(d), el(width)), lambda j, i, k, lo=lo: (_mult(j * d, d), lo), **mode_kw),
                     pl.BlockSpec((el(d), el(width)), lambda j, i, k, lo=lo: (_mult((2 + j) * d, d), lo), **mode_kw)],
            out_shape=[_sds((2, s_len, f_dim), BF16), _sds((s_len, f_dim), BF16)],
            out_specs=[pl.BlockSpec((el(2), el(tm0), el(width)),
                                    lambda j, i, k, lo=lo: (0, _mult(i * tm0, tm0), _mult(j * p + lo, LANES))),
                       pl.BlockSpec((el(tm0), el(width)),
                                    lambda j, i, k, lo=lo: (_mult(i * tm0, tm0), _mult(j * p + lo, LANES)))],
            acc_shape=(tm0, width), epilogue=ep, carry=done, col_block=512 if cols is None else None)
    hu, act = done
    w_out2d = w_out_after_swiglu()
    tn2 = _tile(d, 1024)
    f = _matmul(
        f"{tag}_down", act, [w_out2d], mode="nn", grid=(s_len // tm, d // tn2, 2),
        a_spec=pl.BlockSpec((tm, 2 * tf), lambda i, j, k: (i, k)),
        b_specs=[pl.BlockSpec((2 * tf, tn2), lambda i, j, k: (k, j))],
        out_shape=_sds((s_len, d), F32), out_specs=pl.BlockSpec((tm, tn2), lambda i, j, k: (i, j)),
        acc_shape=(tm, tn2), epilogue=_ep_store(F32))
    return hu, act, f, w_out2d


def _ffn_bwd(tag, n, hu, act, df, w_in_g, w_out2d, dims, after_dw_out, after_dw_in):
    s_len, d, f_dim = dims["S"], dims["D"], dims["F"]
    tf = f_dim // 4
    tk = _tile(s_len, 2048)
    tn = _tile(d, 1024)
    g_out = _matmul(
        f"{tag}_dw_out", act, [df], mode="tn", grid=(4, d // tn, s_len // tk),
        a_spec=pl.BlockSpec((tk, tf), lambda i, j, k: (k, i)),
        b_specs=[pl.BlockSpec((tk, tn), lambda i, j, k: (k, j))],
        out_shape=_sds((2, 4, tf // 2, d), F32),
        out_specs=pl.BlockSpec((2, None, tf // 2, tn), lambda i, j, k: (0, i, 0, j)),
        acc_shape=(tf, tn), epilogue=_ep_halves(tf // 2))
    after_dw_out(g_out)

    def ep_dhu(accs, ex, outs):
        da = accs[0]
        outs[0][0] = (da * ex[0][0].astype(F32)).astype(BF16)
        outs[0][1] = (da * ex[0][1].astype(F32)).astype(BF16)

    tm = _tile(s_len, 512)
    hu_spec = pl.BlockSpec((2, tm, 2 * tf), lambda j, i, k: (0, i, j))
    dhu = _matmul(
        f"{tag}_dhu", df, [w_out2d], mode="nt", grid=(2, s_len // tm, 1),
        a_spec=pl.BlockSpec((tm, d), lambda j, i, k: (i, 0)),
        b_specs=[pl.BlockSpec((2 * tf, d), lambda j, i, k: (j, 0), pipeline_mode=pl.Buffered(1))],
        extras=[hu], extra_specs=[hu_spec],
        out_shape=_sds((2, s_len, f_dim), BF16), out_specs=hu_spec, acc_shape=(tm, 2 * tf), epilogue=ep_dhu)

    hd = d // 2
    rt = hd // 2
    g_in = _matmul(
        f"{tag}_dw_in", n, [dhu], mode="tn", grid=(N_CHIPS, 4, s_len // tk),
        a_spec=pl.BlockSpec((tk, rt), lambda j, i, k: (k, i)),
        b_specs=[pl.BlockSpec((None, tk, 2 * tf), lambda j, i, k: (j // 2, k, j % 2))],
        out_shape=_sds((2, 4, hd, f_dim // 2), F32),
        out_specs=pl.BlockSpec((None, None, rt, 2 * tf), lambda j, i, k: (i // 2, j, i % 2, 0)),
        acc_shape=(rt, 2 * tf), epilogue=_ep_store(F32))
    after_dw_in(g_in)

    tm2 = _tile(s_len, 1024)
    dn = _matmul(
        f"{tag}_dn", dhu, [w_in_g], mode="nt", grid=(s_len // tm2, d // tn, N_CHIPS),
        a_spec=pl.BlockSpec((None, tm2, 2 * tf), lambda i, j, k: (k // 2, i, k % 2)),
        b_specs=[pl.BlockSpec((None, tn, 2 * tf), lambda i, j, k: (k, j, 0))],
        out_shape=_sds((s_len, d), BF16), out_specs=pl.BlockSpec((tm2, tn), lambda i, j, k: (i, j)),
        acc_shape=(tm2, tn), epilogue=_ep_store(BF16))
    return dn


def kernel(x, c, w_ada, b_ada, g_ffn1, w1_in, w1_out, g_mix, w_in, conv_w, conv_b, ln_a_g, ln_a_b, w_a_out, b_a_out, w_b_group, b_b_group, ls_b, w_out, g_ffn2, w2_in, w2_out, g_final, loss_target, m_w_ada, m_b_ada, m_g_ffn1, m_w1_in, m_w1_out, m_g_mix, m_w_in, m_conv_w, m_conv_b, m_ln_a_g, m_ln_a_b, m_w_a_out, m_b_a_out, m_w_b_group, m_b_b_group, m_ls_b, m_w_out, m_g_ffn2, m_w2_in, m_w2_out, m_g_final, v_w_ada, v_b_ada, v_g_ffn1, v_w1_in, v_w1_out, v_g_mix, v_w_in, v_conv_w, v_conv_b, v_ln_a_g, v_ln_a_b, v_w_a_out, v_b_a_out, v_w_b_group, v_b_b_group, v_ls_b, v_w_out, v_g_ffn2, v_w2_in, v_w2_out, v_g_final):
    weights = dict(w_ada=w_ada, b_ada=b_ada, g_ffn1=g_ffn1, w1_in=w1_in, w1_out=w1_out, g_mix=g_mix, w_in=w_in,
                   conv_w=conv_w, conv_b=conv_b, ln_a_g=ln_a_g, ln_a_b=ln_a_b, w_a_out=w_a_out, b_a_out=b_a_out,
                   w_b_group=w_b_group, b_b_group=b_b_group, ls_b=ls_b, w_out=w_out, g_ffn2=g_ffn2, w2_in=w2_in,
                   w2_out=w2_out, g_final=g_final)
    mom1 = dict(w_ada=m_w_ada, b_ada=m_b_ada, g_ffn1=m_g_ffn1, w1_in=m_w1_in, w1_out=m_w1_out, g_mix=m_g_mix,
                w_in=m_w_in, conv_w=m_conv_w, conv_b=m_conv_b, ln_a_g=m_ln_a_g, ln_a_b=m_ln_a_b, w_a_out=m_w_a_out,
                b_a_out=m_b_a_out, w_b_group=m_w_b_group, b_b_group=m_b_b_group, ls_b=m_ls_b, w_out=m_w_out,
                g_ffn2=m_g_ffn2, w2_in=m_w2_in, w2_out=m_w2_out, g_final=m_g_final)
    mom2 = dict(w_ada=v_w_ada, b_ada=v_b_ada, g_ffn1=v_g_ffn1, w1_in=v_w1_in, w1_out=v_w1_out, g_mix=v_g_mix,
                w_in=v_w_in, conv_w=v_conv_w, conv_b=v_conv_b, ln_a_g=v_ln_a_g, ln_a_b=v_ln_a_b, w_a_out=v_w_a_out,
                b_a_out=v_b_a_out, w_b_group=v_w_b_group, b_b_group=v_b_b_group, ls_b=v_ls_b, w_out=v_w_out,
                g_ffn2=v_g_ffn2, w2_in=v_w2_in, w2_out=v_w2_out, g_final=v_g_final)
    order = list(weights)

    s_len, d = x.shape[1], x.shape[2]
    f_dim = w1_out.shape[0] * N_CHIPS
    wc = conv_w.shape[1] * N_CHIPS
    wp = w_b_group.shape[0] * w_b_group.shape[1]
    n_groups, gi, goq = w_b_group.shape
    npc = w_in.shape[1]
    ada_c = w_ada.shape[1]
    dims = dict(S=s_len, D=d, F=f_dim)
    ts = _tile(s_len, 256)

    xi, yi, ci = lax.axis_index("x"), lax.axis_index("y"), lax.axis_index("c")
    q = 2 * xi + yi
    dev = 2 * q + ci
    q_idx = jnp.reshape(q, (1,)).astype(jnp.int32)
    qc_idx = jnp.stack([q, ci]).astype(jnp.int32)
    _PREVIOUS.clear()

    cwq = conv_w.shape[1]
    pack0 = jnp.concatenate([c.reshape(-1), conv_w.reshape(-1), b_b_group.reshape(-1)])
    n0 = -(-pack0.shape[0] // (8 * LANES)) * LANES
    pack0 = jnp.pad(pack0, (0, 8 * n0 - pack0.shape[0])).reshape(8, n0)
    g0 = _allgather_small("gather_small_in", pack0).reshape(N_DEV, 8 * n0)
    c_all = g0[:, :d]
    south = g0[0::2]
    cw_full = jnp.concatenate([south[k, d:d + CONV_K * cwq].reshape(CONV_K, cwq) for k in range(N_CHIPS)], axis=1)
    cw_pad = jnp.pad(cw_full, ((0, HALO - CONV_K), (0, 0)))
    o_bb = d + CONV_K * cwq
    bb_full = jnp.concatenate([south[k, o_bb:o_bb + n_groups * goq].reshape(n_groups, goq) for k in range(N_CHIPS)],
                              axis=1).reshape(1, d)

    as2d = lambda a: a.reshape(-1, a.shape[-1])
    groups = dict(w1_out=["w1_out"], mix=["w_a_out", "w_b_group", "w_out"], w2_in=["w2_in"], w2_out=["w2_out"])
    big = ["w1_in", "w1_out", "w_in", "w_a_out", "w_b_group", "w_out", "w2_in", "w2_out"]
    cast = lambda nm: _cast_into_gathered(f"cast_{nm}", as2d(weights[nm]), q_idx)
    w1_in_gather = _TwoPartGather("w1_in", cast("w1_in"))

    b_ada_mine = lax.dynamic_slice(b_ada, (q * ada_c,), (ada_c,)).reshape(1, ada_c)
    ada_piece = _ada_fwd("ada_fwd", c_all, w_ada, b_ada_mine)
    ada_sent = _send_to_all("gather_ada", ada_piece)
    casts = {nm: cast(nm) for nm in big[1:]}
    w1_in_gather.start_second()
    ici = {}
    for grp, names in groups.items():
        ici[grp] = _gather_ici(f"gather_{grp}_ici", [casts[nm] for nm in names])
        if grp == "w1_out":
            w_in_gather = _TwoPartGather("w_in", casts["w_in"])
            w_in_gather.start_second()
    g1 = lax.dynamic_update_index_in_dim(ada_sent.wait()[1], ada_piece, dev, 0)
    ada_rows = lax.dynamic_index_in_dim(g1[0::2], dev, axis=1, keepdims=False)
    ada = ada_rows.reshape(3, 3, 1, d)
    (sh1, sc1, gt1), (sh2, sc2, gt2), (sh3, sc3, gt3) = [[ada[i, j] for j in range(3)] for i in range(3)]

    row = lambda vct: vct.reshape(1, -1)
    g1v, gmv, g2v, gfv = row(g_ffn1), row(g_mix), row(g_ffn2), row(g_final)

    def arrived(grp):
        return _gather_d2d(f"gather_{grp}_d2d", ici[grp].wait())

    def gathered(fwd, grp):
        return {nm: g.reshape(N_CHIPS, 2 * g.shape[2], g.shape[3]) for nm, g in zip(groups[grp], fwd.wait())}

    x2 = x[0]
    tgt = loss_target[0]

    n1 = _norm_mod("ffn1_norm", x2, g1v, sc1, sh1, ts)
    fwd, w1_in_parts = {}, []

    def w1_in_part(part):
        def get():
            w1_in_gather.arrive(part)
            w1_in_parts.append(w1_in_gather.ready(part))
            return w1_in_parts[-1]
        return get

    def w1_out_after_swiglu():
        fwd["w1_out"] = arrived("w1_out")
        w_in_gather.arrive(0)
        return gathered(fwd["w1_out"], "w1_out")["w1_out"].reshape(f_dim, d)

    hu1, act1, f1, w1_out_2d = _ffn_fwd(
        "ffn1", n1, [(w1_in_part(part), w1_in_gather.parts[part]) for part in range(2)], w1_out_after_swiglu, dims)
    w1_in_g = w1_in_parts[-1]
    h1, n2 = _residual_norm_mod("mix_norm", x2, f1, gt1, 0.5, gmv, sc2, sh2, ts)

    tm = _tile(s_len, 1024)
    tnp = npc // 2
    proj = ()
    for part in range(2):
        if part:
            w_in_gather.arrive(part)
        w_in_g = w_in_gather.ready(part)
        lo, width = w_in_gather.parts[part]
        el = pl.Element
        proj = (_matmul(
            f"mix_proj{part}", n2, [w_in_g.reshape(N_CHIPS * d, npc)], mode="nn", grid=(s_len // tm, N_CHIPS, 1),
            a_spec=pl.BlockSpec((tm, d), lambda i, j, k: (i, 0)),
            b_specs=[pl.BlockSpec((el(d), el(width)), lambda i, j, k, lo=lo: (_mult(j * d, d), lo))],
            out_shape=_sds((N_CHIPS * s_len, npc), BF16),
            out_specs=pl.BlockSpec((el(tm), el(width)), lambda i, j, k, lo=lo: (_mult(j * s_len + i * tm, tm), lo)),
            acc_shape=(tm, width), epilogue=_ep_store(BF16), carry=proj),)
    proj = proj[0].reshape(N_CHIPS, s_len, npc)
    fwd["mix"] = arrived("mix")
    cbv, lgv, lbv = row(conv_b), row(ln_a_g), row(ln_a_b)
    a3, mixed, conv_out = _mixer_mid("mix_mid", proj, cw_pad, cbv, lgv, lbv, wc, wp, ts)
    wts = gathered(fwd["mix"], "mix")
    w_out_2d = wts["w_out"].reshape(d, d)
    w_a_g = wts["w_a_out"]
    w_b_r = _regroup("regroup_w_b", wts["w_b_group"], n_groups)
    dq = d // N_CHIPS
    ya = _matmul(
        "mix_ya", a3, [w_a_g], mode="nn", grid=(s_len // tm, N_CHIPS, 1),
        a_spec=pl.BlockSpec((tm, wc), lambda i, j, k: (i, 0)),
        b_specs=[pl.BlockSpec((None, wc, dq), lambda i, j, k: (j, 0, 0))],
        out_shape=_sds((s_len, d), BF16), out_specs=pl.BlockSpec((tm, dq), lambda i, j, k: (i, j)),
        acc_shape=(tm, dq), epilogue=_ep_store(BF16))
    yb = _matmul(
        "mix_yb", mixed, [w_b_r], mode="nn", grid=(s_len // tm, n_groups, 1),
        a_spec=pl.BlockSpec((tm, gi), lambda i, j, k: (i, j)),
        b_specs=[pl.BlockSpec((None, gi, dq), lambda i, j, k: (j, 0, 0))],
        out_shape=_sds((s_len, d), BF16), out_specs=pl.BlockSpec((tm, dq), lambda i, j, k: (i, j)),
        acc_shape=(tm, dq), epilogue=_ep_store(BF16))
    bav, lsv = row(b_a_out), row(ls_b)
    z = _gates_fwd("mix_gates", proj, ya, yb, bav, bb_full, lsv, wc, wp, ts)
    tn = _tile(d, 1024)
    mix = _matmul(
        "mix_out", z, [w_out_2d], mode="nn", grid=(s_len // tm, d // tn, 1),
        a_spec=pl.BlockSpec((tm, d), lambda i, j, k: (i, 0)),
        b_specs=[pl.BlockSpec((d, tn), lambda i, j, k: (0, j))],
        out_shape=_sds((s_len, d), F32), out_specs=pl.BlockSpec((tm, tn), lambda i, j, k: (i, j)),
        acc_shape=(tm, tn), epilogue=_ep_store(F32))
    fwd["w2_in"] = arrived("w2_in")
    h2, n3 = _residual_norm_mod("ffn2_norm", h1, mix, gt2, 1.0, g2v, sc3, sh3, ts)
    w2_in_g = gathered(fwd["w2_in"], "w2_in")["w2_in"]
    hu2, act2, f3, w2_out_2d = _ffn_fwd(
        "ffn2", n3, [(lambda: w2_in_g, None)],
        lambda: gathered(arrived("w2_out"), "w2_out")["w2_out"].reshape(f_dim, d), dims)

    dh3, df3, d_gf, d_gt3, loss_cols = _final_loss("final_loss", h2, f3, tgt, gt3, 0.5, gfv, ts)
    rs, held = {}, {}
    dn3 = _ffn_bwd(
        "ffn2", n3, hu2, act2, df3, w2_in_g, w2_out_2d, dims,
        after_dw_out=lambda g: held.update(w2_out=g),
        after_dw_in=lambda g: rs.update(ffn2=_ReduceScatter("g_ffn2", ["w2_out", "w2_in"], [held["w2_out"], g],
                                                            qc_idx)))
    dh2, dmix, d_sh3, d_sc3, d_g2, d_gt2 = _norm_mod_bwd("ffn2_norm_bwd", h2, dn3, dh3, g2v, sc3, ts,
                                                         prev=(mix, gt2, 1.0))
    rs["ffn2"].step2()

    tk = s_len
    hq = d // (2 * N_CHIPS)
    gw_out = _matmul(
        "mix_dw_out", z, [dmix], mode="tn", grid=(N_CHIPS, d // tn, s_len // tk),
        a_spec=pl.BlockSpec((tk, 2 * hq), lambda i, j, k: (k, i)),
        b_specs=[pl.BlockSpec((tk, tn), lambda i, j, k: (k, j))],
        out_shape=_sds((2, N_CHIPS, hq, d), F32),
        out_specs=pl.BlockSpec((2, None, hq, tn), lambda i, j, k: (0, i, 0, j)),
        acc_shape=(2 * hq, tn), epilogue=_ep_halves(hq))
    dz = _matmul(
        "mix_dz", dmix, [w_out_2d], mode="nt", grid=(s_len // tm, d // tn, 1),
        a_spec=pl.BlockSpec((tm, d), lambda i, j, k: (i, 0)),
        b_specs=[pl.BlockSpec((tn, d), lambda i, j, k: (j, 0))],
        out_shape=_sds((s_len, d), BF16), out_specs=pl.BlockSpec((tm, tn), lambda i, j, k: (i, j)),
        acc_shape=(tm, tn), epilogue=_ep_store(BF16))
    dya, dyb, dgates, d_ba, d_ls, d_bb = _gates_bwd("mix_gates_bwd", proj, dz, ya, yb, bav, bb_full, lsv, wc, wp, ts)
    gw_a = _matmul(
        "mix_dw_a", a3, [dya], mode="tn", grid=(1, N_CHIPS, s_len // tk),
        a_spec=pl.BlockSpec((tk, wc), lambda i, j, k: (k, 0)),
        b_specs=[pl.BlockSpec((tk, dq), lambda i, j, k: (k, j))],
        out_shape=_sds((2, N_CHIPS, wc // 2, dq), F32),
        out_specs=pl.BlockSpec((2, None, wc // 2, dq), lambda i, j, k: (0, j, 0, 0)),
        acc_shape=(wc, dq), epilogue=_ep_halves(wc // 2))
    da3 = _matmul(
        "mix_da3", dya, [w_a_g], mode="nt", grid=(s_len // tm, 1, N_CHIPS),
        a_spec=pl.BlockSpec((tm, dq), lambda i, j, k: (i, k)),
        b_specs=[pl.BlockSpec((None, wc, dq), lambda i, j, k: (k, 0, 0))],
        out_shape=_sds((s_len, wc), BF16), out_specs=pl.BlockSpec((tm, wc), lambda i, j, k: (i, 0)),
        acc_shape=(tm, wc), epilogue=_ep_store(BF16))
    gpr = n_groups // 2

    def ep_by_chip(accs, ex, outs):
        for k in range(N_CHIPS):
            outs[0][k] = accs[0][:, k * goq:(k + 1) * goq]

    gw_b = _matmul(
        "mix_dw_b", mixed, [dyb], mode="tn", grid=(1, n_groups, s_len // tk),
        a_spec=pl.BlockSpec((tk, gi), lambda i, j, k: (k, j)),
        b_specs=[pl.BlockSpec((tk, dq), lambda i, j, k: (k, j))],
        out_shape=_sds((2, N_CHIPS, gpr * gi, goq), F32),
        out_specs=pl.BlockSpec((None, N_CHIPS, gi, goq), lambda i, j, k: (j // gpr, 0, j % gpr, 0)),
        acc_shape=(gi, dq), epilogue=ep_by_chip)
    dmixed = _matmul(
        "mix_dmixed", dyb, [w_b_r], mode="nt", grid=(s_len // tm, n_groups, 1),
        a_spec=pl.BlockSpec((tm, dq), lambda i, j, k: (i, j)),
        b_specs=[pl.BlockSpec((None, gi, dq), lambda i, j, k: (j, 0, 0))],
        out_shape=_sds((s_len, wp), BF16), out_specs=pl.BlockSpec((tm, gi), lambda i, j, k: (i, j)),
        acc_shape=(tm, gi), epilogue=_ep_store(BF16))
    da1, d_lg, d_lb, d_cb, d_cw = _conv_branch_bwd("mix_conv_bwd", proj, conv_out, da3, lgv, lbv, wc, wp, ts)
    dproj = _mixer_in_bwd("mix_in_bwd", proj, da1, dmixed, dgates, cw_pad, wc, wp, ts)
    hd = d // 2
    rt = hd // 2
    gw_in = _matmul(
        "mix_dw_in", n2, [dproj], mode="tn", grid=(N_CHIPS, 4, 1),
        a_spec=pl.BlockSpec((s_len, rt), lambda j, i, k: (0, i)),
        b_specs=[pl.BlockSpec((None, s_len, npc), lambda j, i, k: (j, 0, 0))],
        out_shape=_sds((2, N_CHIPS, hd, npc), F32),
        out_specs=pl.BlockSpec((None, None, rt, npc), lambda j, i, k: (i // 2, j, i % 2, 0)),
        acc_shape=(rt, npc), epilogue=_ep_store(F32))
    rs["mix"] = _ReduceScatter("g_mix", ["w_in", "w_a_out", "w_b_group", "w_out"], [gw_in, gw_a, gw_b, gw_out],
                               qc_idx)
    rs["ffn2"].step3()
    dn2 = _matmul(
        "mix_dn", dproj, [w_in_g], mode="nt", grid=(s_len // tm, d // tn, N_CHIPS),
        a_spec=pl.BlockSpec((None, tm, npc), lambda i, j, k: (k, i, 0)),
        b_specs=[pl.BlockSpec((None, tn, npc), lambda i, j, k: (k, j, 0))],
        out_shape=_sds((s_len, d), BF16), out_specs=pl.BlockSpec((tm, tn), lambda i, j, k: (i, j)),
        acc_shape=(tm, tn), epilogue=_ep_store(BF16))
    dh1, df1, d_sh2, d_sc2, d_gm, d_gt1 = _norm_mod_bwd("mix_norm_bwd", h1, dn2, dh2, gmv, sc2, ts,
                                                        prev=(f1, gt1, 0.5))
    rs["mix"].step2()

    def w1_in_ready(g):
        rs["w1_in"] = _ReduceScatter("g_w1_in", ["w1_in"], [g], qc_idx)
        rs["w1_out"].step2()
        rs["mix"].step3()

    dn1 = _ffn_bwd(
        "ffn1", n1, hu1, act1, df1, w1_in_g, w1_out_2d, dims,
        after_dw_out=lambda g: rs.update(w1_out=_ReduceScatter("g_w1_out", ["w1_out"], [g], qc_idx)),
        after_dw_in=w1_in_ready)
    grad_x, d_sh1, d_sc1, d_g1 = _norm_mod_bwd("ffn1_norm_bwd", x2, dn1, dh1, g1v, sc1, ts)

    d_ada = jnp.concatenate([d_sh1, d_sc1, d_gt1, d_sh2, d_sc2, d_gt2, d_sh3, d_sc3, d_gt3], axis=1)
    small = [d_ada, d_g1, d_gm, d_cw[:CONV_K].reshape(1, -1), d_cb, d_lg, d_lb, d_ba, d_bb, d_ls, d_g2, d_gf,
             loss_cols]
    sizes = [a.shape[1] for a in small]
    pack1 = jnp.concatenate(small, axis=1).reshape(-1)
    n1p = -(-pack1.shape[0] // (8 * LANES)) * LANES
    pack1 = jnp.pad(pack1, (0, 8 * n1p - pack1.shape[0])).reshape(8, n1p)
    g2 = _allgather_small("gather_small_grads", pack1)
    rs["w1_in"].step2()
    total = _sum_devices("sum_small_grads", g2, 8).reshape(-1)
    offs = [0]
    for sz in sizes:
        offs.append(offs[-1] + sz)
    tot = [total[offs[k]:offs[k + 1]] for k in range(len(sizes))]
    d_ada_all = g2.reshape(N_DEV, 8 * n1p)[:, :sizes[0]]
    loss = jnp.sum(tot[12])

    grads = {}
    grads["b_ada"] = tot[0]
    grads["g_ffn1"], grads["g_mix"] = tot[1], tot[2]
    grads["conv_w"] = lax.dynamic_slice(tot[3].reshape(CONV_K, wc), (0, q * cwq), (CONV_K, cwq))
    grads["conv_b"], grads["ln_a_g"], grads["ln_a_b"], grads["b_a_out"] = tot[4], tot[5], tot[6], tot[7]
    grads["b_b_group"] = lax.dynamic_slice(tot[8].reshape(n_groups, N_CHIPS * goq), (0, q * goq), (n_groups, goq))
    grads["ls_b"], grads["g_ffn2"], grads["g_final"] = tot[9], tot[10], tot[11]

    delta, new_m, new_v = {}, {}, {}

    def adamw_group(reduced):
        for nm, g in reduced.items():
            shp = weights[nm].shape
            go, dl, mo, vo = _adamw(f"adamw_{nm}", as2d(weights[nm]), g, as2d(mom1[nm]), as2d(mom2[nm]))
            grads[nm], delta[nm], new_m[nm], new_v[nm] = go.reshape(shp), dl.reshape(shp), mo.reshape(shp), vo.reshape(shp)

    adamw_group(rs["ffn2"].result())
    rs["w1_out"].step3()
    adamw_group(rs["mix"].result())
    d_ada_mine = lax.dynamic_slice(d_ada_all, (0, q * ada_c), (N_DEV, ada_c))
    grads["w_ada"], delta["w_ada"], new_m["w_ada"], new_v["w_ada"] = _ada_grad_adamw(
        "adamw_w_ada", c_all.T, d_ada_mine, w_ada, m_w_ada, v_w_ada)
    rs["w1_in"].step3()
    smalls = [nm for nm in order if nm not in big and nm != "w_ada"]
    flat = lambda src: jnp.concatenate([src[nm].reshape(-1) for nm in smalls])
    n_small = sum(weights[nm].size for nm in smalls)
    rows_s = -(-n_small // (8 * LANES)) * 8
    packed = [jnp.pad(flat(src), (0, rows_s * LANES - n_small)).reshape(rows_s, LANES)
              for src in (weights, grads, mom1, mom2)]
    _, dl_s, mo_s, vo_s = _adamw("adamw_small", *packed)
    off = 0
    for nm in smalls:
        sz, shp = weights[nm].size, weights[nm].shape
        delta[nm] = dl_s.reshape(-1)[off:off + sz].reshape(shp)
        new_m[nm] = mo_s.reshape(-1)[off:off + sz].reshape(shp)
        new_v[nm] = vo_s.reshape(-1)[off:off + sz].reshape(shp)
        grads[nm] = grads[nm].reshape(shp)
        off += sz
    adamw_group(rs["w1_out"].result())
    adamw_group(rs["w1_in"].result())

    return (loss, grad_x[None], *[grads[nm] for nm in order], *[delta[nm] for nm in order],
            *[new_m[nm] for nm in order], *[new_v[nm] for nm in order])
```
